```python
import jax, jax.numpy as jnp
from jax import lax
import numpy as np

D_MODEL = 1024
BATCH = 4
SEQ = 4096
DEPTH = 2
DEC_BATCH = 128
DEC_SEQ = 1
PAST_LEN = 8192
PAGE_SIZE = 128

N_HEADS = 8
N_KV = 2
HEAD_DIM = 64
GROUP = N_HEADS // N_KV
WINDOW = 128
BLOCK = WINDOW
GLA_HEADS = 4
GLA_DK = 64
GLA_DV = 128
GATE_RANK = 16
GATE_TAU = 16.0
CHUNK = 16
D_FF = 2816
N_EXPERTS = 8
TOP_K = 2
D_FF_EXPERT = 1408
N_DENSE = (DEPTH + 1) // 2
N_MOE = DEPTH // 2
EPS = 1e-6

Q_W = N_HEADS * HEAD_DIM
KV_W = N_KV * HEAD_DIM
GK_W = GLA_HEADS * GLA_DK
GV_W = GLA_HEADS * GLA_DV
PROJ_W = Q_W + 2 * KV_W + 2 * GK_W + 2 * GV_W + GATE_RANK + 2 * D_MODEL

kernel_name = 'hybrid_swa_gla_adaln_decode_step'


def _split_points():
    sizes = (Q_W, KV_W, KV_W, GK_W, GK_W, GV_W, GATE_RANK, GV_W, D_MODEL, D_MODEL)
    pts, acc = [], 0
    for sz in sizes[:-1]:
        acc += sz
        pts.append(acc)
    return pts


def _alibi_slopes():
    return jnp.asarray(2.0 ** (-8.0 * np.arange(1, N_HEADS + 1) / N_HEADS), dtype=jnp.float32)


def _rms(x, g):
    x32 = x.astype(jnp.float32)
    y = x32 * lax.rsqrt(jnp.mean(x32 * x32, axis=-1, keepdims=True) + EPS)
    return (y * g.astype(jnp.float32)).astype(x.dtype)


def _sink_attend(scores, valid, sink, v, eq):
    scores = jnp.where(valid, scores, -jnp.inf)
    m = jnp.maximum(jnp.max(scores, axis=-1, keepdims=True), sink)
    p = jnp.exp(scores - m)
    denom = jnp.sum(p, axis=-1, keepdims=True) + jnp.exp(sink - m)
    return jnp.einsum(eq, (p / denom).astype(v.dtype), v)


def _swa_prompt(q, k, v, sinks):
    b, s = q.shape[0], q.shape[1]
    nb = s // BLOCK
    qb = q.reshape(b, nb, BLOCK, N_KV, GROUP, HEAD_DIM)
    kb = k.reshape(b, nb, BLOCK, N_KV, HEAD_DIM)
    vb = v.reshape(b, nb, BLOCK, N_KV, HEAD_DIM)
    pad = ((0, 0), (1, 0), (0, 0), (0, 0), (0, 0))
    kk = jnp.concatenate([jnp.pad(kb, pad)[:, :-1], kb], axis=2)
    vv = jnp.concatenate([jnp.pad(vb, pad)[:, :-1], vb], axis=2)
    scores = jnp.einsum('bnqkgd,bnskd->bnkgqs', qb, kk).astype(jnp.float32) * (HEAD_DIM ** -0.5)
    qpos = jnp.arange(BLOCK) + BLOCK
    kpos = jnp.arange(2 * BLOCK)
    dist = qpos[:, None] - kpos[None, :]
    slopes = _alibi_slopes().reshape(N_KV, GROUP)
    scores = scores - slopes[:, :, None, None] * dist.astype(jnp.float32)
    valid = ((dist >= 0) & (dist <= WINDOW))[None] & ((kpos[None, None, :] >= BLOCK) | (jnp.arange(nb)[:, None, None] > 0))
    sink = sinks.astype(jnp.float32).reshape(N_KV, GROUP)[None, None, :, :, None, None]
    out = _sink_attend(scores, valid[None, :, None, None], sink, vv, 'bnkgqs,bnskd->bnqkgd')
    return out.reshape(b, s, Q_W)


def _swa_sample(q, k, v, buf_k, buf_v, sinks):
    b, t = q.shape[0], q.shape[1]
    w = buf_k.shape[1]
    kk = jnp.concatenate([buf_k, k.astype(buf_k.dtype)], axis=1)
    vv = jnp.concatenate([buf_v, v.astype(buf_v.dtype)], axis=1)
    qg = q.reshape(b, t, N_KV, GROUP, HEAD_DIM)
    scores = jnp.einsum('btkgd,bskd->bkgts', qg, kk).astype(jnp.float32) * (HEAD_DIM ** -0.5)
    dist = (jnp.arange(t) + w)[:, None] - jnp.arange(w + t)[None, :]
    slopes = _alibi_slopes().reshape(N_KV, GROUP)
    scores = scores - slopes[:, :, None, None] * dist.astype(jnp.float32)
    valid = ((dist >= 0) & (dist <= WINDOW))[None, None, None]
    sink = sinks.astype(jnp.float32).reshape(N_KV, GROUP)[None, :, :, None, None]
    out = _sink_attend(scores, valid, sink, vv, 'bkgts,bskd->btkgd')
    return out.reshape(b, t, Q_W), kk[:, t:], vv[:, t:]


def _gla_chunked(q, k, v, log_a, S0):
    b, s, h = q.shape[0], q.shape[1], q.shape[2]
    nc = s // CHUNK

    def blk(t):
        return t.reshape(b, nc, CHUNK, h, t.shape[-1]).transpose(0, 3, 1, 2, 4)

    q, k, v, la = blk(q), blk(k), blk(v), blk(log_a)
    cum = jnp.cumsum(la, axis=3)
    causal = jnp.tril(jnp.ones((CHUNK, CHUNK), dtype=bool))
    diff = cum[:, :, :, :, None, :] - cum[:, :, :, None, :, :]
    decay_ij = jnp.exp(jnp.where(causal[:, :, None], diff, -jnp.inf))
    attn = jnp.einsum('bhnik,bhnjk,bhnijk->bhnij', q, k, decay_ij)
    o = jnp.einsum('bhnij,bhnjv->bhniv', attn, v)
    last = cum[:, :, :, -1:, :]
    ds = jnp.einsum('bhnjk,bhnjv->bhnkv', k * jnp.exp(last - cum), v)
    dec = jnp.exp(last[:, :, :, 0, :])

    def step(S, inp):
        d, u = inp
        return d[..., None] * S + u, S

    S_fin, S_in = lax.scan(step, S0, (jnp.moveaxis(dec, 2, 0), jnp.moveaxis(ds, 2, 0)))
    S_in = jnp.moveaxis(S_in, 0, 2)
    o = o + jnp.einsum('bhnik,bhnkv->bhniv', q * jnp.exp(cum), S_in)
    return o.transpose(0, 2, 3, 1, 4).reshape(b, s, h, GLA_DV), S_fin


def _gla_recurrent(q, k, v, log_a, S0):
    def step(S, inp):
        qt, kt, vt, lat = inp
        S = jnp.exp(lat)[..., None] * S + kt[..., :, None] * vt[..., None, :]
        return S, jnp.einsum('bhk,bhkv->bhv', qt, S)

    S, o = lax.scan(step, S0, (jnp.moveaxis(q, 1, 0), jnp.moveaxis(k, 1, 0), jnp.moveaxis(v, 1, 0), jnp.moveaxis(log_a, 1, 0)))
    return jnp.moveaxis(o, 0, 1), S


def _mixer(h, w_in, q_g, k_g, sinks, wa2, ba, gn_g, wpa, wpb, w_out, cache):
    b, s, _ = h.shape
    z = h @ w_in
    qa, ka, va, qb_, kb_, vb_, ga, rb, gate_a, gate_b = jnp.split(z, _split_points(), axis=-1)
    q = _rms(qa.reshape(b, s, N_HEADS, HEAD_DIM), q_g)
    k = _rms(ka.reshape(b, s, N_KV, HEAD_DIM), k_g)
    v = va.reshape(b, s, N_KV, HEAD_DIM)
    gq = qb_.reshape(b, s, GLA_HEADS, GLA_DK).astype(jnp.float32) * (GLA_DK ** -0.5)
    gk = kb_.reshape(b, s, GLA_HEADS, GLA_DK).astype(jnp.float32)
    gv = vb_.reshape(b, s, GLA_HEADS, GLA_DV).astype(jnp.float32)
    log_a = (jax.nn.log_sigmoid((ga @ wa2 + ba).astype(jnp.float32)) / GATE_TAU).reshape(b, s, GLA_HEADS, GLA_DK)
    if cache is None:
        a_out = _swa_prompt(q, k, v, sinks)
        cw = min(WINDOW, s)
        new_k, new_v = k[:, s - cw:], v[:, s - cw:]
        S0 = jnp.zeros((b, GLA_HEADS, GLA_DK, GLA_DV), jnp.float32)
        g_out, S = _gla_chunked(gq, gk, gv, log_a, S0)
        state_dtype = h.dtype
    else:
        buf_k, buf_v, S_prev = cache
        a_out, new_k, new_v = _swa_sample(q, k, v, buf_k, buf_v, sinks)
        g_out, S = _gla_recurrent(gq, gk, gv, log_a, S_prev.astype(jnp.float32))
        state_dtype = S_prev.dtype
    g_out = _rms(g_out, gn_g).reshape(b, s, GV_W).astype(h.dtype) * jax.nn.silu(rb)
    merged = jax.nn.sigmoid(gate_a) * (a_out @ wpa) + jax.nn.sigmoid(gate_b) * (g_out @ wpb)
    return merged @ w_out, (new_k, new_v, S.astype(state_dtype))


def _swiglu(h, wg, wu, wd):
    return (jax.nn.silu(h @ wg) * (h @ wu)) @ wd


def _moe(h, rw, rb, wg, wu, wd):
    logits = (h @ rw + rb).astype(jnp.float32)
    top_v, top_i = lax.top_k(logits, TOP_K)
    probs = jax.nn.softmax(top_v, axis=-1)
    gates = jnp.sum(jax.nn.one_hot(top_i, N_EXPERTS, dtype=jnp.float32) * probs[..., None], axis=1).astype(h.dtype)
    out = jnp.zeros_like(h)
    for e in range(N_EXPERTS):
        out = out + gates[:, e:e + 1] * _swiglu(h, wg[e], wu[e], wd[e])
    return out


def _layer(x, c, ada_w, ada_b, n1, n2, mix_params, ffn_params, is_moe, cache):
    mod = (jax.nn.silu(c) @ ada_w + ada_b)[:, None, :]
    sh1, sc1, g1, sh2, sc2, g2 = jnp.split(mod, 6, axis=-1)
    h = _rms(x, n1) * (1 + sc1) + sh1
    mix, new_state = _mixer(h, *mix_params, cache)
    x = x + g1 * mix
    h = _rms(x, n2) * (1 + sc2) + sh2
    b, s, d = h.shape
    h2 = h.reshape(b * s, d)
    f = _moe(h2, *ffn_params) if is_moe else _swiglu(h2, *ffn_params)
    x = x + g2 * f.reshape(b, s, d)
    return x, new_state


def setup_inputs(seed: int = 0) -> dict:
    key = jax.random.key(seed)
    ks = jax.random.split(key, 32)
    f32 = jnp.float32

    def nrm(k, shape, scale):
        return jax.random.normal(k, shape, f32) * scale

    cw = min(WINDOW, PAST_LEN)
    D = D_MODEL
    return {
        'x_prompt': nrm(ks[0], (BATCH, SEQ, D), 1.0),
        'x_sample': nrm(ks[1], (DEC_BATCH, DEC_SEQ, D), 1.0),
        'cache_k': nrm(ks[2], (DEPTH, DEC_BATCH, cw, N_KV, HEAD_DIM), 1.0),
        'cache_v': nrm(ks[3], (DEPTH, DEC_BATCH, cw, N_KV, HEAD_DIM), 1.0),
        'state_gla': nrm(ks[4], (DEPTH, DEC_BATCH, GLA_HEADS, GLA_DK, GLA_DV), 4.0),
        'c_prompt': nrm(ks[5], (BATCH, D), 1.0),
        'c_sample': nrm(ks[6], (DEC_BATCH, D), 1.0),
        'ada_w': nrm(ks[7], (DEPTH, D, 6 * D), 0.5 * D ** -0.5),
        'ada_b': nrm(ks[8], (DEPTH, 6 * D), 0.02),
        'norm1_g': 1.0 + nrm(ks[9], (DEPTH, D), 0.05),
        'norm2_g': 1.0 + nrm(ks[10], (DEPTH, D), 0.05),
        'w_in': nrm(ks[11], (DEPTH, D, PROJ_W), D ** -0.5),
        'q_norm_g': 1.0 + nrm(ks[12], (DEPTH, HEAD_DIM), 0.05),
        'k_norm_g': 1.0 + nrm(ks[13], (DEPTH, HEAD_DIM), 0.05),
        'attn_sinks': nrm(ks[14], (DEPTH, N_HEADS), 1.0),
        'gla_wa2': nrm(ks[15], (DEPTH, GATE_RANK, GK_W), GATE_RANK ** -0.5),
        'gla_ba': nrm(ks[16], (DEPTH, GK_W), 0.1),
        'gla_norm_g': 1.0 + nrm(ks[17], (DEPTH, GLA_DV), 0.05),
        'w_branch_a': nrm(ks[18], (DEPTH, Q_W, D), Q_W ** -0.5),
        'w_branch_b': nrm(ks[19], (DEPTH, GV_W, D), GV_W ** -0.5),
        'w_out': nrm(ks[20], (DEPTH, D, D), D ** -0.5),
        'ffn_w_gate': nrm(ks[21], (N_DENSE, D, D_FF), D ** -0.5),
        'ffn_w_up': nrm(ks[22], (N_DENSE, D, D_FF), D ** -0.5),
        'ffn_w_down': nrm(ks[23], (N_DENSE, D_FF, D), D_FF ** -0.5),
        'router_w': nrm(ks[24], (N_MOE, D, N_EXPERTS), D ** -0.5),
        'router_b': nrm(ks[25], (N_MOE, N_EXPERTS), 0.01),
        'moe_w_gate': nrm(ks[26], (N_MOE, N_EXPERTS, D, D_FF_EXPERT), D ** -0.5),
        'moe_w_up': nrm(ks[27], (N_MOE, N_EXPERTS, D, D_FF_EXPERT), D ** -0.5),
        'moe_w_down': nrm(ks[28], (N_MOE, N_EXPERTS, D_FF_EXPERT, D), D_FF_EXPERT ** -0.5),
    }


def reference(x_prompt, x_sample, cache_k, cache_v, state_gla, c_prompt, c_sample, ada_w, ada_b, norm1_g, norm2_g, w_in, q_norm_g, k_norm_g, attn_sinks, gla_wa2, gla_ba, gla_norm_g, w_branch_a, w_branch_b, w_out, ffn_w_gate, ffn_w_up, ffn_w_down, router_w, router_b, moe_w_gate, moe_w_up, moe_w_down):
    xp, xs = x_prompt, x_sample
    kp_l, vp_l, sp_l, ks_l, vs_l, ss_l = [], [], [], [], [], []
    for l in range(DEPTH):
        mix_params = (w_in[l], q_norm_g[l], k_norm_g[l], attn_sinks[l], gla_wa2[l], gla_ba[l], gla_norm_g[l], w_branch_a[l], w_branch_b[l], w_out[l])
        i = l // 2
        if l % 2 == 0:
            ffn_params, is_moe = (ffn_w_gate[i], ffn_w_up[i], ffn_w_down[i]), False
        else:
            ffn_params, is_moe = (router_w[i], router_b[i], moe_w_gate[i], moe_w_up[i], moe_w_down[i]), True
        xp, (kp, vp, sp) = _layer(xp, c_prompt, ada_w[l], ada_b[l], norm1_g[l], norm2_g[l], mix_params, ffn_params, is_moe, None)
        xs, (ksn, vsn, ssn) = _layer(xs, c_sample, ada_w[l], ada_b[l], norm1_g[l], norm2_g[l], mix_params, ffn_params, is_moe, (cache_k[l], cache_v[l], state_gla[l]))
        kp_l.append(kp); vp_l.append(vp); sp_l.append(sp)
        ks_l.append(ksn); vs_l.append(vsn); ss_l.append(ssn)
    return (xp, xs, jnp.stack(kp_l), jnp.stack(vp_l), jnp.stack(sp_l), jnp.stack(ks_l), jnp.stack(vs_l), jnp.stack(ss_l))
```

```python
import functools

import jax
import jax.numpy as jnp
import numpy as np
from jax import lax
from jax.experimental import pallas as pl
from jax.experimental.pallas import tpu as pltpu

D_MODEL = 1024
BATCH = 4
SEQ = 4096
DEPTH = 2
DEC_BATCH = 128
N_HEADS = 8
N_KV = 2
HEAD_DIM = 64
GROUP = N_HEADS // N_KV
WINDOW = 128
GLA_HEADS = 4
GLA_DK = 64
GLA_DV = 128
GATE_RANK = 16
GATE_TAU = 16.0
D_FF = 2816
N_EXPERTS = 8
D_FF_EXPERT = 1408
EPS = 1e-6

Q_W = N_HEADS * HEAD_DIM
KV_W = N_KV * HEAD_DIM
GK_W = GLA_HEADS * GLA_DK
GV_W = GLA_HEADS * GLA_DV

LANES = 128
GATE_PAD = LANES
ROUTER_PAD = LANES
VMEM_LIMIT = 56 * 1024 * 1024

F32 = jnp.float32
BF16 = jnp.bfloat16

_C_Q = 0
_C_K = _C_Q + Q_W
_C_V = _C_K + KV_W
_C_GQ = _C_V + KV_W
_C_GK = _C_GQ + GK_W
_C_GV = _C_GK + GK_W
_C_RB = _C_GV + GV_W
_C_GA = _C_RB + GV_W
_C_GB = _C_GA + D_MODEL
_C_LR = _C_GB + D_MODEL
PROJ_PAD = _C_LR + GATE_PAD

GLA_CHUNK = 128
GLA_LEVELS = 7


def _params(sem, vmem=VMEM_LIMIT):
    return pltpu.CompilerParams(dimension_semantics=sem, vmem_limit_bytes=vmem)


def _dot(a, b):
    return jnp.dot(a, b, preferred_element_type=F32)


def _dot_nt(a, b):
    return lax.dot_general(a, b, (((1,), (1,)), ((), ())), preferred_element_type=F32)


def _dot_tn(a, b):
    return lax.dot_general(a, b, (((0,), (0,)), ((), ())), preferred_element_type=F32)


def _sigmoid(x):
    return 1.0 / (1.0 + jnp.exp(-x))


def _silu(x):
    return x * _sigmoid(x)


def _ada_kernel(c_ref, w_ref, b_ref, o_ref):
    c = c_ref[...]
    o_ref[...] = _dot(_silu(c).astype(BF16), w_ref[...].astype(BF16)) + b_ref[...]


def _ada(c_all, ada_w, ada_b):
    rows = c_all.shape[0]
    tn = 1024
    return pl.pallas_call(
        _ada_kernel,
        grid=(DEPTH, 6 * D_MODEL // tn),
        in_specs=[
            pl.BlockSpec((rows, D_MODEL), lambda l, j: (0, 0)),
            pl.BlockSpec((None, D_MODEL, tn), lambda l, j: (l, 0, j)),
            pl.BlockSpec((None, 1, tn), lambda l, j: (l, 0, j)),
        ],
        out_specs=pl.BlockSpec((None, rows, tn), lambda l, j: (l, 0, j)),
        out_shape=jax.ShapeDtypeStruct((DEPTH, rows, 6 * D_MODEL), F32),
        compiler_params=_params(("parallel", "parallel")),
        name="ada_mod",
    )(c_all, ada_w, ada_b.reshape(DEPTH, 1, 6 * D_MODEL))


def _mixin_kernel(x_ref, sh_ref, sc_ref, n1_ref, w_ref, bd_ref, qg_ref, kg_ref, wa2_ref, ba_ref,
                  q_ref, k_ref, v_ref, gq_ref, gk_ref, gv_ref, la_ref, rb_ref, sga_ref, sgb_ref):
    x = x_ref[...]
    ms = jnp.mean(x * x, axis=-1, keepdims=True)
    h = x * lax.rsqrt(ms + EPS) * n1_ref[...]
    h = h * (1.0 + sc_ref[...]) + sh_ref[...]
    hb = h.astype(BF16)

    def proj(a, b):
        return _dot(hb, w_ref[:, a:b])

    q = proj(_C_Q, _C_K)
    ssq = _dot((q * q).astype(BF16), bd_ref[...])
    q_ref[...] = (q * lax.rsqrt(ssq * (1.0 / HEAD_DIM) + EPS) * qg_ref[...]).astype(BF16)
    k = proj(_C_K, _C_V)
    ssk = _dot((k * k).astype(BF16), bd_ref[0:KV_W, 0:KV_W])
    k_ref[...] = k * lax.rsqrt(ssk * (1.0 / HEAD_DIM) + EPS) * kg_ref[...]
    v_ref[...] = proj(_C_V, _C_GQ)
    gq_ref[...] = proj(_C_GQ, _C_GK) * (GLA_DK ** -0.5)
    gk_ref[...] = proj(_C_GK, _C_GV)
    gv_ref[...] = proj(_C_GV, _C_RB)
    rb_ref[...] = _silu(proj(_C_RB, _C_GA)).astype(BF16)
    sga_ref[...] = _sigmoid(proj(_C_GA, _C_GB)).astype(BF16)
    sgb_ref[...] = _sigmoid(proj(_C_GB, _C_LR)).astype(BF16)
    ga = proj(_C_LR, PROJ_PAD)
    xg = _dot(ga.astype(BF16), wa2_ref[...]) + ba_ref[...]
    la_ref[...] = (jnp.minimum(xg, 0.0) - jnp.log1p(jnp.exp(-jnp.abs(xg)))) * (1.0 / GATE_TAU)


def _mixin(x, sh, sc, n1, w, bd, qg, kg, wa2, ba, *, tm, per_row_mod):
    n = x.shape[0]
    nt = n // tm
    mod_spec = _mod_spec(tm, n, sh, per_row_mod)

    def row(i):
        return (i, 0)

    def const(shape):
        return pl.BlockSpec(shape, lambda i: (0,) * len(shape))

    def out(width, dtype):
        return pl.BlockSpec((tm, width), row), jax.ShapeDtypeStruct((n, width), dtype)

    outs = [out(Q_W, BF16), out(KV_W, F32), out(KV_W, F32), out(GK_W, F32), out(GK_W, F32),
            out(GV_W, F32), out(GK_W, F32), out(GV_W, BF16), out(D_MODEL, BF16), out(D_MODEL, BF16)]
    return pl.pallas_call(
        _mixin_kernel,
        grid=(nt,),
        in_specs=[
            pl.BlockSpec((tm, D_MODEL), row), mod_spec, mod_spec, const((1, D_MODEL)),
            const((D_MODEL, PROJ_PAD)), const((Q_W, Q_W)), const((1, Q_W)), const((1, KV_W)),
            const((GATE_PAD, GK_W)), const((1, GK_W)),
        ],
        out_specs=[o[0] for o in outs],
        out_shape=[o[1] for o in outs],
        compiler_params=_params(("parallel",)),
        name="mixer_in",
    )(x, sh, sc, n1, w, bd, qg, kg, wa2, ba)


def _head_slope(h):
    return float(2.0 ** (-8.0 * (h + 1) / N_HEADS))


def _swa_prompt_kernel(sink_ref, q_ref, kp_ref, kc_ref, vp_ref, vc_ref, o_ref):
    n = pl.program_id(1)
    blk = WINDOW
    kk = jnp.concatenate([kp_ref[...], kc_ref[...]], axis=0).astype(BF16)
    vv = jnp.concatenate([vp_ref[...], vc_ref[...]], axis=0).astype(BF16)
    row = lax.broadcasted_iota(jnp.int32, (blk, 2 * blk), 0)
    col = lax.broadcasted_iota(jnp.int32, (blk, 2 * blk), 1)
    dist = row + blk - col
    first_key = jnp.where(n > 0, 0, blk)
    valid = (dist >= 0) & (dist <= WINDOW) & (col >= first_key)
    distf = dist.astype(F32)
    klane = lax.broadcasted_iota(jnp.int32, (2 * blk, KV_W), 1)
    kk_kv = [jnp.where((klane // HEAD_DIM) == kv, kk, jnp.zeros_like(kk)) for kv in range(N_KV)]
    olane = lax.broadcasted_iota(jnp.int32, (blk, KV_W), 1)
    q = q_ref[...]
    outs = []
    for g in range(GROUP):
        qp = q[:, g * KV_W:(g + 1) * KV_W]
        pair = []
        for kv in range(N_KV):
            h = kv * GROUP + g
            s = _dot_nt(qp, kk_kv[kv]) - _head_slope(h) * distf
            s = jnp.where(valid, s, -jnp.inf)
            sink = sink_ref[h]
            m = jnp.maximum(jnp.max(s, axis=-1, keepdims=True), sink)
            p = jnp.exp(s - m)
            den = jnp.sum(p, axis=-1, keepdims=True) + jnp.exp(sink - m)
            pair.append(_dot(p.astype(BF16), vv) * (1.0 / den))
        outs.append(jnp.where(olane < HEAD_DIM, pair[0], pair[1]))
    o_ref[...] = jnp.concatenate(outs, axis=-1).astype(BF16)


def _swa_prompt(sinks, q, k, v):
    nb = SEQ // WINDOW

    def cur(b, n):
        return (b * nb + n, 0)

    def prev(b, n):
        return (b * nb + jnp.maximum(n - 1, 0), 0)

    return pl.pallas_call(
        _swa_prompt_kernel,
        grid=(BATCH, nb),
        in_specs=[
            pl.BlockSpec(memory_space=pltpu.SMEM),
            pl.BlockSpec((WINDOW, Q_W), cur),
            pl.BlockSpec((WINDOW, KV_W), prev), pl.BlockSpec((WINDOW, KV_W), cur),
            pl.BlockSpec((WINDOW, KV_W), prev), pl.BlockSpec((WINDOW, KV_W), cur),
        ],
        out_specs=pl.BlockSpec((WINDOW, Q_W), cur),
        out_shape=jax.ShapeDtypeStruct((BATCH * SEQ, Q_W), BF16),
        compiler_params=_params(("parallel", "parallel")),
        name="swa_prompt",
    )(sinks, q, k, k, v, v)


SAMPLE_TB = 8
KEY_PAD = 2 * WINDOW


def _swa_sample_kernel(q_ref, kn_ref, vn_ref, ck_ref, cv_ref, seg_ref, sl_ref, sk_ref,
                       o_ref, ok_ref, ov_ref):
    w = WINDOW
    srow = lax.broadcasted_iota(jnp.int32, (KEY_PAD, LANES), 0)
    distf = (w - srow).astype(F32)
    key_ok = srow <= w
    slope = sl_ref[...]
    sink = sk_ref[...]
    orow = lax.broadcasted_iota(jnp.int32, (LANES, KV_W), 0)
    olane = lax.broadcasted_iota(jnp.int32, (LANES, KV_W), 1)
    own_kv = (olane // HEAD_DIM) == (orow % N_KV)
    pad = jnp.zeros((KEY_PAD - w - 8, KV_W), F32)
    row8 = lax.broadcasted_iota(jnp.int32, (8, KV_W), 0)
    qf = q_ref[...].astype(F32)
    for bi in range(SAMPLE_TB):
        kc = ck_ref[bi]
        vc = cv_ref[bi]
        kn = kn_ref[bi:bi + 1, :]
        vn = vn_ref[bi:bi + 1, :]
        ok_ref[bi, pl.ds(0, w - 1), :] = ck_ref[bi, pl.ds(1, w - 1), :]
        ok_ref[bi, pl.ds(w - 1, 1), :] = kn
        ov_ref[bi, pl.ds(0, w - 1), :] = cv_ref[bi, pl.ds(1, w - 1), :]
        ov_ref[bi, pl.ds(w - 1, 1), :] = vn
        kn8 = jnp.where(row8 == 0, jnp.broadcast_to(kn, (8, KV_W)), 0.0)
        vn8 = jnp.where(row8 == 0, jnp.broadcast_to(vn, (8, KV_W)), 0.0)
        kfull = jnp.concatenate([kc, kn8, pad], axis=0)
        vfull = jnp.concatenate([vc, vn8, pad], axis=0).astype(BF16)
        s = jnp.zeros((KEY_PAD, LANES), F32)
        for g in range(GROUP):
            qrow = qf[bi:bi + 1, g * KV_W:(g + 1) * KV_W]
            s = s + _dot((kfull * qrow).astype(BF16), seg_ref[g])
        s = s - slope * distf
        s = jnp.where(key_ok, s, -jnp.inf)
        m = jnp.maximum(jnp.max(s, axis=0, keepdims=True), sink)
        p = jnp.exp(s - m)
        den = jnp.sum(p, axis=0, keepdims=True) + jnp.exp(sink - m)
        p = p * (1.0 / den)
        r = _dot_tn(p.astype(BF16), vfull)
        r = jnp.where(own_kv, r, 0.0)
        o_ref[bi] = r[0:N_HEADS, :]


def _swa_sample(q, kn, vn, ck, cv, seg, slopes, sinks):
    tb = SAMPLE_TB
    nb = DEC_BATCH // tb
    w = WINDOW

    def const(shape):
        return pl.BlockSpec(shape, lambda i: (0,) * len(shape))

    return pl.pallas_call(
        _swa_sample_kernel,
        grid=(nb,),
        in_specs=[
            pl.BlockSpec((tb, Q_W), lambda i: (i, 0)),
            pl.BlockSpec((tb, KV_W), lambda i: (i, 0)),
            pl.BlockSpec((tb, KV_W), lambda i: (i, 0)),
            pl.BlockSpec((tb, w, KV_W), lambda i: (i, 0, 0)),
            pl.BlockSpec((tb, w, KV_W), lambda i: (i, 0, 0)),
            const((GROUP, KV_W, LANES)), const((1, LANES)), const((1, LANES)),
        ],
        out_specs=[
            pl.BlockSpec((tb, N_HEADS, KV_W), lambda i: (i, 0, 0)),
            pl.BlockSpec((tb, w, KV_W), lambda i: (i, 0, 0)),
            pl.BlockSpec((tb, w, KV_W), lambda i: (i, 0, 0)),
        ],
        out_shape=[
            jax.ShapeDtypeStruct((DEC_BATCH, N_HEADS, KV_W), F32),
            jax.ShapeDtypeStruct((DEC_BATCH, w, KV_W), F32),
            jax.ShapeDtypeStruct((DEC_BATCH, w, KV_W), F32),
        ],
        compiler_params=_params(("parallel",)),
        name="swa_sample",
    )(q, kn, vn, ck, cv, seg, slopes, sinks)


def _gla_cum_matrix():
    c = GLA_CHUNK
    tri = np.tril(np.ones((c, c), np.float32))
    i = np.arange(c)
    blocks = []
    for lvl in range(GLA_LEVELS):
        half = 1 << lvl
        mid = (i // (2 * half)) * (2 * half) + half - 1
        blocks.append(tri - tri[mid])
    blocks.append(tri)
    return np.concatenate(blocks, axis=0)


def _split3(x):
    hi = x.astype(BF16)
    r1 = x - hi.astype(F32)
    mid = r1.astype(BF16)
    lo = (r1 - mid.astype(F32)).astype(BF16)
    return hi, mid, lo


def _gla_prompt_kernel(q_ref, k_ref, la_ref, v_ref, rb_ref, gn_ref, ut_ref, o_ref, s_ref, st_ref):
    c = pl.program_id(1)
    cl = GLA_CHUNK

    @pl.when(c == 0)
    def _():
        st_ref[0] = jnp.zeros(st_ref.shape[1:], F32)

    ut = ut_ref[...]
    hi, mid, lo = _split3(la_ref[...])
    tall = _dot(ut, hi) + _dot(ut, mid) + _dot(ut, lo)
    q = q_ref[...]
    k = k_ref[...]
    lane = lax.broadcasted_iota(jnp.int32, (cl, GK_W), 1)
    row = lax.broadcasted_iota(jnp.int32, (cl, GK_W), 0)
    head_of_lane = lane // GLA_DK
    ri = lax.broadcasted_iota(jnp.int32, (cl, cl), 0)
    ci = lax.broadcasted_iota(jnp.int32, (cl, cl), 1)
    attn = [jnp.zeros((cl, cl), F32) for _ in range(GLA_HEADS)]
    for lvl in range(GLA_LEVELS + 1):
        if lvl < GLA_LEVELS:
            e = jnp.exp(-jnp.abs(tall[lvl * cl:(lvl + 1) * cl]))
            upper = ((row >> lvl) & 1) == 1
            qt = jnp.where(upper, q * e, 0.0).astype(BF16)
            kt = jnp.where(upper, 0.0, k * e).astype(BF16)
            pair_ok = (ri >> (lvl + 1)) == (ci >> (lvl + 1))
        else:
            qt = q.astype(BF16)
            kt = k.astype(BF16)
            pair_ok = ri == ci
        for h in range(GLA_HEADS):
            kth = jnp.where(head_of_lane == h, kt, jnp.zeros_like(kt))
            attn[h] = attn[h] + jnp.where(pair_ok, _dot_nt(qt, kth), 0.0)

    cum = tall[GLA_LEVELS * cl:(GLA_LEVELS + 1) * cl]
    last = cum[cl - 1:cl, :]
    qe = (q * jnp.exp(cum)).astype(BF16)
    kd = (k * jnp.exp(last - cum)).astype(BF16)
    ones = jnp.ones((cl, LANES), BF16)
    dec = jnp.exp(_dot_tn(hi, ones) + _dot_tn(mid, ones) + _dot_tn(lo, ones))
    v = v_ref[...]
    rb = rb_ref[...]
    outs = []
    for h in range(GLA_HEADS):
        vh = v[:, h * GLA_DV:(h + 1) * GLA_DV].astype(BF16)
        st = st_ref[c % 2, h]
        o = _dot(attn[h].astype(BF16), vh) + _dot(qe, st.astype(BF16))
        kdh = jnp.where(head_of_lane == h, kd, jnp.zeros_like(kd))
        st_ref[(c + 1) % 2, h] = st * dec + _dot_tn(kdh, vh)
        ms = jnp.mean(o * o, axis=-1, keepdims=True)
        g = o * lax.rsqrt(ms + EPS) * gn_ref[...]
        outs.append(g * rb[:, h * GLA_DV:(h + 1) * GLA_DV].astype(F32))
    o_ref[...] = jnp.concatenate(outs, axis=-1).astype(BF16)

    @pl.when(c == pl.num_programs(1) - 1)
    def _():
        fin = (SEQ // GLA_CHUNK) % 2
        s_ref[...] = st_ref[fin, 0] + st_ref[fin, 1] + st_ref[fin, 2] + st_ref[fin, 3]


def _gla_prompt(gq, gk, la, gv, rbs, gn, ut):
    cl = GLA_CHUNK
    nc = SEQ // cl

    def row(b, c):
        return (b * nc + c, 0)

    return pl.pallas_call(
        _gla_prompt_kernel,
        grid=(BATCH, nc),
        in_specs=[
            pl.BlockSpec((cl, GK_W), row), pl.BlockSpec((cl, GK_W), row), pl.BlockSpec((cl, GK_W), row),
            pl.BlockSpec((cl, GV_W), row), pl.BlockSpec((cl, GV_W), row),
            pl.BlockSpec((1, GLA_DV), lambda b, c: (0, 0)),
            pl.BlockSpec(((GLA_LEVELS + 1) * cl, cl), lambda b, c: (0, 0)),
        ],
        out_specs=[
            pl.BlockSpec((cl, GV_W), row),
            pl.BlockSpec((None, GK_W, GLA_DV), lambda b, c: (b, 0, 0)),
        ],
        out_shape=[
            jax.ShapeDtypeStruct((BATCH * SEQ, GV_W), BF16),
            jax.ShapeDtypeStruct((BATCH, GK_W, GLA_DV), F32),
        ],
        scratch_shapes=[pltpu.VMEM((2, GLA_HEADS, GK_W, GLA_DV), F32)],
        compiler_params=_params(("parallel", "arbitrary")),
        name="gla_prompt",
    )(gq, gk, la, gv, rbs, gn, ut)


def _gla_sample_kernel(q_ref, k_ref, la_ref, v_ref, rb_ref, gn_ref, s_ref, o_ref, so_ref):
    tb = SAMPLE_TB
    dec = jnp.exp(la_ref[...])
    pieces = []
    for x in (dec, k_ref[...], q_ref[...]):
        hi, mid, lo = _split3(x)
        stacked = jnp.concatenate(
            [hi.astype(F32), mid.astype(F32), lo.astype(F32), jnp.zeros_like(x)], axis=0)
        pieces.append(stacked.astype(BF16))
    prow = lax.broadcasted_iota(jnp.int32, (4 * tb, LANES), 0)
    rb = rb_ref[...].astype(F32)
    v = v_ref[...]
    for bi in range(tb):
        sel = jnp.where((prow % tb) == bi, 1.0, 0.0).astype(BF16)
        a_col, k_col, q_col = [_dot_tn(p, sel) for p in pieces]
        for h in range(GLA_HEADS):
            rs = slice(h * GLA_DK, (h + 1) * GLA_DK)
            vs = slice(h * GLA_DV, (h + 1) * GLA_DV)
            s_new = a_col[rs] * s_ref[bi, rs, :] + k_col[rs] * v[bi:bi + 1, vs]
            so_ref[bi, rs, :] = s_new
            o = jnp.sum(q_col[rs] * s_new, axis=0, keepdims=True)
            ms = jnp.mean(o * o, axis=-1, keepdims=True)
            g = o * lax.rsqrt(ms + EPS) * gn_ref[...]
            o_ref[bi:bi + 1, vs] = g * rb[bi:bi + 1, vs]


def _gla_sample(gq, gk, la, gv, rbs, gn, state):
    tb = SAMPLE_TB
    nb = DEC_BATCH // tb

    def row(w):
        return pl.BlockSpec((tb, w), lambda i: (i, 0))

    st_spec = pl.BlockSpec((tb, GK_W, GLA_DV), lambda i: (i, 0, 0))
    return pl.pallas_call(
        _gla_sample_kernel,
        grid=(nb,),
        in_specs=[row(GK_W), row(GK_W), row(GK_W), row(GV_W), row(GV_W),
                  pl.BlockSpec((1, GLA_DV), lambda i: (0, 0)), st_spec],
        out_specs=[row(GV_W), st_spec],
        out_shape=[
            jax.ShapeDtypeStruct((DEC_BATCH, GV_W), F32),
            jax.ShapeDtypeStruct((DEC_BATCH, GK_W, GLA_DV), F32),
        ],
        compiler_params=_params(("parallel",)),
        name="gla_sample",
    )(gq, gk, la, gv, rbs, gn, state)


def _merge_kernel(x_ref, a_ref, g_ref, sga_ref, sgb_ref, wpa_ref, wpb_ref, wo_ref,
                  g1_ref, sh_ref, sc_ref, n2_ref, x1_ref, h2_ref):
    ya = _dot(a_ref[...].astype(BF16), wpa_ref[...])
    yb = _dot(g_ref[...].astype(BF16), wpb_ref[...])
    merged = sga_ref[...].astype(F32) * ya + sgb_ref[...].astype(F32) * yb
    mix = _dot(merged.astype(BF16), wo_ref[...])
    x1 = x_ref[...] + g1_ref[...] * mix
    x1_ref[...] = x1
    ms = jnp.mean(x1 * x1, axis=-1, keepdims=True)
    h = x1 * lax.rsqrt(ms + EPS) * n2_ref[...]
    h2_ref[...] = (h * (1.0 + sc_ref[...]) + sh_ref[...]).astype(BF16)


def _mod_spec(tm, n, mod, per_row_mod):
    if per_row_mod:
        return pl.BlockSpec((tm, D_MODEL), lambda i, *_: (i, 0))
    tiles_per_mod = (n // mod.shape[0]) // tm
    return pl.BlockSpec((None, 1, D_MODEL), lambda i, *_: (i // tiles_per_mod, 0, 0))


def _merge(x, a, g, sga, sgb, wpa, wpb, wo, g1, sh2, sc2, n2, *, tm, per_row_mod):
    n = x.shape[0]
    mod_spec = _mod_spec(tm, n, g1, per_row_mod)

    def row(w):
        return pl.BlockSpec((tm, w), lambda i: (i, 0))

    def const(shape):
        return pl.BlockSpec(shape, lambda i: (0,) * len(shape))

    return pl.pallas_call(
        _merge_kernel,
        grid=(n // tm,),
        in_specs=[row(D_MODEL), row(a.shape[1]), row(GV_W), row(D_MODEL), row(D_MODEL),
                  const(wpa.shape), const(wpb.shape), const(wo.shape),
                  mod_spec, mod_spec, mod_spec, const((1, D_MODEL))],
        out_specs=[row(D_MODEL), row(D_MODEL)],
        out_shape=[jax.ShapeDtypeStruct((n, D_MODEL), F32), jax.ShapeDtypeStruct((n, D_MODEL), BF16)],
        compiler_params=_params(("parallel",)),
        name="merge_out",
    )(x, a, g, sga, sgb, wpa, wpb, wo, g1, sh2, sc2, n2)


def _ffn_kernel(h_ref, x_ref, g2_ref, wg_ref, wu_ref, wd_ref, o_ref, acc_ref):
    f = pl.program_id(1)
    hb = h_ref[...]
    act = (_silu(_dot(hb, wg_ref[...])) * _dot(hb, wu_ref[...])).astype(BF16)
    y = _dot(act, wd_ref[...])

    @pl.when(f == 0)
    def _():
        acc_ref[...] = y

    @pl.when(f > 0)
    def _():
        acc_ref[...] += y

    @pl.when(f == pl.num_programs(1) - 1)
    def _():
        o_ref[...] = x_ref[...] + g2_ref[...] * acc_ref[...]


def _ffn(h2, x1, g2, wg, wu, wd, *, tm, tf, per_row_mod):
    n = h2.shape[0]
    mod_spec = _mod_spec(tm, n, g2, per_row_mod)
    return pl.pallas_call(
        _ffn_kernel,
        grid=(n // tm, D_FF // tf),
        in_specs=[
            pl.BlockSpec((tm, D_MODEL), lambda i, f: (i, 0)),
            pl.BlockSpec((tm, D_MODEL), lambda i, f: (i, 0)),
            mod_spec,
            pl.BlockSpec((D_MODEL, tf), lambda i, f: (0, f)),
            pl.BlockSpec((D_MODEL, tf), lambda i, f: (0, f)),
            pl.BlockSpec((tf, D_MODEL), lambda i, f: (f, 0)),
        ],
        out_specs=pl.BlockSpec((tm, D_MODEL), lambda i, f: (i, 0)),
        out_shape=jax.ShapeDtypeStruct((n, D_MODEL), F32),
        scratch_shapes=[pltpu.VMEM((tm, D_MODEL), F32)],
        compiler_params=_params(("parallel", "arbitrary")),
        name="ffn_dense",
    )(h2, x1, g2, wg, wu, wd)


def _moe_kernel(h_ref, x_ref, g2_ref, rw_ref, rbias_ref, wg_ref, wu_ref, wd_ref, o_ref, acc_ref, gate_ref):
    e = pl.program_id(1)
    hb = h_ref[...]
    tm = hb.shape[0]
    lane = lax.broadcasted_iota(jnp.int32, (tm, ROUTER_PAD), 1).astype(F32)

    @pl.when(e == 0)
    def _():
        logits = _dot(hb, rw_ref[...]) + rbias_ref[...]
        lg = jnp.where(lane < N_EXPERTS, logits, -jnp.inf)
        m1 = jnp.max(lg, axis=-1, keepdims=True)
        i1 = jnp.min(jnp.where(lg == m1, lane, float(ROUTER_PAD)), axis=-1, keepdims=True)
        lg2 = jnp.where(lane == i1, -jnp.inf, lg)
        m2 = jnp.max(lg2, axis=-1, keepdims=True)
        i2 = jnp.min(jnp.where(lg2 == m2, lane, float(ROUTER_PAD)), axis=-1, keepdims=True)
        e2 = jnp.exp(m2 - m1)
        p1 = 1.0 / (1.0 + e2)
        p2 = e2 * p1
        gate_ref[...] = jnp.where(lane == i1, p1, 0.0) + jnp.where(lane == i2, p2, 0.0)
        acc_ref[...] = jnp.zeros_like(acc_ref)

    ge = jnp.sum(jnp.where(lane == e.astype(F32), gate_ref[...], 0.0), axis=-1, keepdims=True)
    act = (_silu(_dot(hb, wg_ref[...])) * _dot(hb, wu_ref[...])).astype(BF16)
    acc_ref[...] += ge * _dot(act, wd_ref[...])

    @pl.when(e == pl.num_programs(1) - 1)
    def _():
        o_ref[...] = x_ref[...] + g2_ref[...] * acc_ref[...]


def _moe(h2, x1, g2, rw, rbias, wg, wu, wd, *, tm, per_row_mod):
    n = h2.shape[0]
    mod_spec = _mod_spec(tm, n, g2, per_row_mod)
    fe = D_FF_EXPERT
    return pl.pallas_call(
        _moe_kernel,
        grid=(n // tm, N_EXPERTS),
        in_specs=[
            pl.BlockSpec((tm, D_MODEL), lambda i, e: (i, 0)),
            pl.BlockSpec((tm, D_MODEL), lambda i, e: (i, 0)),
            mod_spec,
            pl.BlockSpec((D_MODEL, ROUTER_PAD), lambda i, e: (0, 0)),
            pl.BlockSpec((1, ROUTER_PAD), lambda i, e: (0, 0)),
            pl.BlockSpec((None, D_MODEL, fe), lambda i, e: (e, 0, 0)),
            pl.BlockSpec((None, D_MODEL, fe), lambda i, e: (e, 0, 0)),
            pl.BlockSpec((None, fe, D_MODEL), lambda i, e: (e, 0, 0)),
        ],
        out_specs=pl.BlockSpec((tm, D_MODEL), lambda i, e: (i, 0)),
        out_shape=jax.ShapeDtypeStruct((n, D_MODEL), F32),
        scratch_shapes=[pltpu.VMEM((tm, D_MODEL), F32), pltpu.VMEM((tm, ROUTER_PAD), F32)],
        compiler_params=_params(("parallel", "arbitrary")),
        name="moe",
    )(h2, x1, g2, rw, rbias, wg, wu, wd)


def _head_perm():
    idx = []
    for g in range(GROUP):
        for kv in range(N_KV):
            h = kv * GROUP + g
            idx.extend(range(h * HEAD_DIM, (h + 1) * HEAD_DIM))
    return np.asarray(idx, np.int32)


def _relayout_w_in(w):
    pts = np.cumsum([0, Q_W, KV_W, KV_W, GK_W, GK_W, GV_W, GATE_RANK, GV_W, D_MODEL, D_MODEL])
    qa, ka, va, qb, kb, vb, ga, rb, gta, gtb = [w[:, pts[i]:pts[i + 1]] for i in range(10)]
    qa = qa[:, _head_perm()]
    ga = jnp.pad(ga, ((0, 0), (0, GATE_PAD - GATE_RANK)))
    return jnp.concatenate([qa, ka, va, qb, kb, vb, rb, gta, gtb, ga], axis=1).astype(BF16)


def _expand_wpa(wpa_perm):
    out = jnp.zeros((N_HEADS, N_KV, HEAD_DIM, D_MODEL), wpa_perm.dtype)
    src = wpa_perm.reshape(GROUP, N_KV, HEAD_DIM, D_MODEL)
    for g in range(GROUP):
        for kv in range(N_KV):
            out = out.at[g * N_KV + kv, kv].set(src[g, kv])
    return out.reshape(N_HEADS * KV_W, D_MODEL)


def _sample_lane_consts(sinks):
    j = np.arange(N_HEADS)
    head = (j % N_KV) * GROUP + j // N_KV
    slopes = np.zeros((1, LANES), np.float32)
    slopes[0, :N_HEADS] = 2.0 ** (-8.0 * (head + 1) / N_HEADS)
    sink_row = jnp.zeros((1, LANES), F32).at[0, :N_HEADS].set(sinks[head])
    seg = np.zeros((GROUP, KV_W, LANES), np.float32)
    for g in range(GROUP):
        for kv in range(N_KV):
            seg[g, kv * HEAD_DIM:(kv + 1) * HEAD_DIM, g * N_KV + kv] = 1.0
    return jnp.asarray(seg, BF16), jnp.asarray(slopes), sink_row


def kernel(x_prompt, x_sample, cache_k, cache_v, state_gla, c_prompt, c_sample, ada_w, ada_b, norm1_g, norm2_g, w_in, q_norm_g, k_norm_g, attn_sinks, gla_wa2, gla_ba, gla_norm_g, w_branch_a, w_branch_b, w_out, ffn_w_gate, ffn_w_up, ffn_w_down, router_w, router_b, moe_w_gate, moe_w_up, moe_w_down):
    n_p = BATCH * SEQ
    xp = x_prompt.reshape(n_p, D_MODEL)
    xs = x_sample.reshape(DEC_BATCH, D_MODEL)

    c_rows = BATCH + DEC_BATCH
    c_pad = -c_rows % 8
    c_all = jnp.pad(jnp.concatenate([c_prompt, c_sample], axis=0), ((0, c_pad), (0, 0)))
    mod = _ada(c_all, ada_w, ada_b)

    bd = jnp.asarray(np.kron(np.eye(N_HEADS), np.ones((HEAD_DIM, HEAD_DIM))), BF16)
    ut = jnp.asarray(_gla_cum_matrix(), BF16)
    perm = _head_perm()

    kp_l, vp_l, sp_l, ks_l, vs_l, ss_l = [], [], [], [], [], []
    for l in range(DEPTH):
        m = mod[l].reshape(c_rows + c_pad, 6, D_MODEL)
        mod_p = [m[:BATCH, i].reshape(BATCH, 1, D_MODEL) for i in range(6)]
        mod_s = [m[BATCH:c_rows, i] for i in range(6)]

        w = _relayout_w_in(w_in[l])
        qg = (jnp.tile(q_norm_g[l], N_HEADS) * (HEAD_DIM ** -0.5)).reshape(1, Q_W)
        kg = jnp.tile(k_norm_g[l], N_KV).reshape(1, KV_W)
        wa2 = jnp.pad(gla_wa2[l], ((0, GATE_PAD - GATE_RANK), (0, 0))).astype(BF16)
        ba = gla_ba[l].reshape(1, GK_W)
        n1 = norm1_g[l].reshape(1, D_MODEL)
        n2 = norm2_g[l].reshape(1, D_MODEL)
        gn = gla_norm_g[l].reshape(1, GLA_DV)
        wpa = w_branch_a[l][perm].astype(BF16)
        wpa_x = _expand_wpa(wpa)
        wpb = w_branch_b[l].astype(BF16)
        wo = w_out[l].astype(BF16)
        sinks_perm = attn_sinks[l]
        seg, slope_row, sink_row = _sample_lane_consts(attn_sinks[l])

        q, k, v, gq, gk, gv, la, rbs, sga, sgb = _mixin(
            xp, mod_p[0], mod_p[1], n1, w, bd, qg, kg, wa2, ba, tm=512, per_row_mod=False)
        a_out = _swa_prompt(sinks_perm, q, k, v)
        g_out, s_fin = _gla_prompt(gq, gk, la, gv, rbs, gn, ut)
        x1, h2 = _merge(xp, a_out, g_out, sga, sgb, wpa, wpb, wo, mod_p[2], mod_p[3], mod_p[4], n2,
                        tm=512, per_row_mod=False)
        kp_l.append(k.reshape(BATCH, SEQ, N_KV, HEAD_DIM)[:, SEQ - WINDOW:])
        vp_l.append(v.reshape(BATCH, SEQ, N_KV, HEAD_DIM)[:, SEQ - WINDOW:])
        sp_l.append(s_fin.reshape(BATCH, GLA_HEADS, GLA_DK, GLA_DV))

        qs, ksn, vsn, gqs, gks, gvs, las, rbss, sgas, sgbs = _mixin(
            xs, mod_s[0], mod_s[1], n1, w, bd, qg, kg, wa2, ba, tm=DEC_BATCH, per_row_mod=True)
        a_s, nk, nv = _swa_sample(qs, ksn, vsn, cache_k[l].reshape(DEC_BATCH, WINDOW, KV_W),
                                  cache_v[l].reshape(DEC_BATCH, WINDOW, KV_W), seg, slope_row, sink_row)
        g_s, s_new = _gla_sample(gqs, gks, las, gvs, rbss, gn, state_gla[l].reshape(DEC_BATCH, GK_W, GLA_DV))
        x1s, h2s = _merge(xs, a_s.reshape(DEC_BATCH, N_HEADS * KV_W), g_s, sgas, sgbs, wpa_x, wpb, wo,
                          mod_s[2], mod_s[3], mod_s[4], n2, tm=DEC_BATCH, per_row_mod=True)
        ks_l.append(nk.reshape(DEC_BATCH, WINDOW, N_KV, HEAD_DIM))
        vs_l.append(nv.reshape(DEC_BATCH, WINDOW, N_KV, HEAD_DIM))
        ss_l.append(s_new.reshape(DEC_BATCH, GLA_HEADS, GLA_DK, GLA_DV))

        i = l // 2
        if l % 2 == 0:
            wg, wu, wd = ffn_w_gate[i].astype(BF16), ffn_w_up[i].astype(BF16), ffn_w_down[i].astype(BF16)
            xp = _ffn(h2, x1, mod_p[5], wg, wu, wd, tm=512, tf=D_FF // 2, per_row_mod=False)
            xs = _ffn(h2s, x1s, mod_s[5], wg, wu, wd, tm=DEC_BATCH, tf=D_FF // 2, per_row_mod=True)
        else:
            rw = jnp.pad(router_w[i], ((0, 0), (0, ROUTER_PAD - N_EXPERTS))).astype(BF16)
            rbias = jnp.pad(router_b[i], (0, ROUTER_PAD - N_EXPERTS)).reshape(1, ROUTER_PAD)
            wg, wu, wd = moe_w_gate[i].astype(BF16), moe_w_up[i].astype(BF16), moe_w_down[i].astype(BF16)
            xp = _moe(h2, x1, mod_p[5], rw, rbias, wg, wu, wd, tm=512, per_row_mod=False)
            xs = _moe(h2s, x1s, mod_s[5], rw, rbias, wg, wu, wd, tm=DEC_BATCH, per_row_mod=True)

    return (xp.reshape(BATCH, SEQ, D_MODEL), xs.reshape(DEC_BATCH, 1, D_MODEL),
            jnp.stack(kp_l), jnp.stack(vp_l), jnp.stack(sp_l),
            jnp.stack(ks_l), jnp.stack(vs_l), jnp.stack(ss_l))
```

```python
import functools

import jax
import jax.numpy as jnp
import numpy as np
from jax import lax
from jax.experimental import pallas as pl
from jax.experimental.pallas import tpu as pltpu

D_MODEL = 1024
BATCH = 4
SEQ = 4096
DEPTH = 2
DEC_BATCH = 128
N_HEADS = 8
N_KV = 2
HEAD_DIM = 64
GROUP = N_HEADS // N_KV
WINDOW = 128
GLA_HEADS = 4
GLA_DK = 64
GLA_DV = 128
GATE_RANK = 16
GATE_TAU = 16.0
D_FF = 2816
N_EXPERTS = 8
D_FF_EXPERT = 1408
EPS = 1e-6

Q_W = N_HEADS * HEAD_DIM
KV_W = N_KV * HEAD_DIM
GK_W = GLA_HEADS * GLA_DK
GV_W = GLA_HEADS * GLA_DV

LANES = 128
GATE_PAD = LANES
ROUTER_PAD = LANES
VMEM_LIMIT = 56 * 1024 * 1024

F32 = jnp.float32
BF16 = jnp.bfloat16

_C_Q = 0
_C_K = _C_Q + Q_W
_C_V = _C_K + KV_W
_C_GQ = _C_V + KV_W
_C_GK = _C_GQ + GK_W
_C_GV = _C_GK + GK_W
_C_RB = _C_GV + GV_W
_C_GA = _C_RB + GV_W
_C_GB = _C_GA + D_MODEL
_C_LR = _C_GB + D_MODEL
PROJ_PAD = _C_LR + GATE_PAD

GLA_CHUNK = 128
GLA_LEVELS = 7


def _params(sem, vmem=VMEM_LIMIT):
    return pltpu.CompilerParams(dimension_semantics=sem, vmem_limit_bytes=vmem)


def _dot(a, b):
    return jnp.dot(a, b, preferred_element_type=F32)


def _dot_nt(a, b):
    return lax.dot_general(a, b, (((1,), (1,)), ((), ())), preferred_element_type=F32)


def _dot_tn(a, b):
    return lax.dot_general(a, b, (((0,), (0,)), ((), ())), preferred_element_type=F32)


def _sigmoid(x):
    return 1.0 / (1.0 + jnp.exp(-x))


def _silu(x):
    return x * _sigmoid(x)


def _ada_kernel(c_ref, w_ref, b_ref, o_ref):
    c = c_ref[...]
    o_ref[...] = _dot(_silu(c).astype(BF16), w_ref[...].astype(BF16)) + b_ref[...]


def _ada(c_all, ada_w, ada_b):
    rows = c_all.shape[0]
    tn = 1024
    return pl.pallas_call(
        _ada_kernel,
        grid=(DEPTH, 6 * D_MODEL // tn),
        in_specs=[
            pl.BlockSpec((rows, D_MODEL), lambda l, j: (0, 0)),
            pl.BlockSpec((None, D_MODEL, tn), lambda l, j: (l, 0, j)),
            pl.BlockSpec((None, 1, tn), lambda l, j: (l, 0, j)),
        ],
        out_specs=pl.BlockSpec((None, rows, tn), lambda l, j: (l, 0, j)),
        out_shape=jax.ShapeDtypeStruct((DEPTH, rows, 6 * D_MODEL), F32),
        compiler_params=_params(("parallel", "parallel")),
        name="ada_mod",
    )(c_all, ada_w, ada_b.reshape(DEPTH, 1, 6 * D_MODEL))


def _mixin_kernel(x_ref, sh_ref, sc_ref, n1_ref, w_ref, bd_ref, qg_ref, kg_ref, wa2_ref, ba_ref,
                  q_ref, k_ref, v_ref, gq_ref, gk_ref, gv_ref, la_ref, rb_ref, sga_ref, sgb_ref):
    x = x_ref[...]
    ms = jnp.mean(x * x, axis=-1, keepdims=True)
    h = x * lax.rsqrt(ms + EPS) * n1_ref[...]
    h = h * (1.0 + sc_ref[...]) + sh_ref[...]
    hb = h.astype(BF16)

    def proj(a, b):
        return _dot(hb, w_ref[:, a:b])

    q = proj(_C_Q, _C_K)
    ssq = _dot((q * q).astype(BF16), bd_ref[...])
    q_ref[...] = (q * lax.rsqrt(ssq * (1.0 / HEAD_DIM) + EPS) * qg_ref[...]).astype(BF16)
    k = proj(_C_K, _C_V)
    ssk = _dot((k * k).astype(BF16), bd_ref[0:KV_W, 0:KV_W])
    k_ref[...] = k * lax.rsqrt(ssk * (1.0 / HEAD_DIM) + EPS) * kg_ref[...]
    v_ref[...] = proj(_C_V, _C_GQ)
    gq_ref[...] = proj(_C_GQ, _C_GK) * (GLA_DK ** -0.5)
    gk_ref[...] = proj(_C_GK, _C_GV)
    gv_ref[...] = proj(_C_GV, _C_RB)
    rb_ref[...] = _silu(proj(_C_RB, _C_GA)).astype(BF16)
    sga_ref[...] = _sigmoid(proj(_C_GA, _C_GB)).astype(BF16)
    sgb_ref[...] = _sigmoid(proj(_C_GB, _C_LR)).astype(BF16)
    ga = proj(_C_LR, PROJ_PAD)
    xg = _dot(ga.astype(BF16), wa2_ref[...]) + ba_ref[...]
    la_ref[...] = (jnp.minimum(xg, 0.0) - jnp.log1p(jnp.exp(-jnp.abs(xg)))) * (1.0 / GATE_TAU)


def _mixin(x, sh, sc, n1, w, bd, qg, kg, wa2, ba, *, tm, per_row_mod):
    n = x.shape[0]
    nt = n // tm
    mod_spec = _mod_spec(tm, n, sh, per_row_mod)

    def row(i):
        return (i, 0)

    def const(shape):
        return pl.BlockSpec(shape, lambda i: (0,) * len(shape))

    def out(width, dtype):
        return pl.BlockSpec((tm, width), row), jax.ShapeDtypeStruct((n, width), dtype)

    outs = [out(Q_W, BF16), out(KV_W, F32), out(KV_W, F32), out(GK_W, F32), out(GK_W, F32),
            out(GV_W, F32), out(GK_W, F32), out(GV_W, BF16), out(D_MODEL, BF16), out(D_MODEL, BF16)]
    return pl.pallas_call(
        _mixin_kernel,
        grid=(nt,),
        in_specs=[
            pl.BlockSpec((tm, D_MODEL), row), mod_spec, mod_spec, const((1, D_MODEL)),
            const((D_MODEL, PROJ_PAD)), const((Q_W, Q_W)), const((1, Q_W)), const((1, KV_W)),
            const((GATE_PAD, GK_W)), const((1, GK_W)),
        ],
        out_specs=[o[0] for o in outs],
        out_shape=[o[1] for o in outs],
        compiler_params=_params(("parallel",)),
        name="mixer_in",
    )(x, sh, sc, n1, w, bd, qg, kg, wa2, ba)


def _head_slope(h):
    return float(2.0 ** (-8.0 * (h + 1) / N_HEADS))


def _swa_prompt_kernel(sink_ref, q_ref, kp_ref, kc_ref, vp_ref, vc_ref, o_ref):
    n = pl.program_id(1)
    blk = WINDOW
    kk = jnp.concatenate([kp_ref[...], kc_ref[...]], axis=0).astype(BF16)
    vv = jnp.concatenate([vp_ref[...], vc_ref[...]], axis=0).astype(BF16)
    row = lax.broadcasted_iota(jnp.int32, (blk, 2 * blk), 0)
    col = lax.broadcasted_iota(jnp.int32, (blk, 2 * blk), 1)
    dist = row + blk - col
    first_key = jnp.where(n > 0, 0, blk)
    valid = (dist >= 0) & (dist <= WINDOW) & (col >= first_key)
    distf = dist.astype(F32)
    klane = lax.broadcasted_iota(jnp.int32, (2 * blk, KV_W), 1)
    kk_kv = [jnp.where((klane // HEAD_DIM) == kv, kk, jnp.zeros_like(kk)) for kv in range(N_KV)]
    olane = lax.broadcasted_iota(jnp.int32, (blk, KV_W), 1)
    q = q_ref[...]
    outs = []
    for g in range(GROUP):
        qp = q[:, g * KV_W:(g + 1) * KV_W]
        pair = []
        for kv in range(N_KV):
            h = kv * GROUP + g
            s = _dot_nt(qp, kk_kv[kv]) - _head_slope(h) * distf
            s = jnp.where(valid, s, -jnp.inf)
            sink = sink_ref[h]
            m = jnp.maximum(jnp.max(s, axis=-1, keepdims=True), sink)
            p = jnp.exp(s - m)
            den = jnp.sum(p, axis=-1, keepdims=True) + jnp.exp(sink - m)
            pair.append(_dot(p.astype(BF16), vv) * (1.0 / den))
        outs.append(jnp.where(olane < HEAD_DIM, pair[0], pair[1]))
    o_ref[...] = jnp.concatenate(outs, axis=-1).astype(BF16)


def _swa_prompt(sinks, q, k, v):
    nb = SEQ // WINDOW

    def cur(b, n):
        return (b * nb + n, 0)

    def prev(b, n):
        return (b * nb + jnp.maximum(n - 1, 0), 0)

    return pl.pallas_call(
        _swa_prompt_kernel,
        grid=(BATCH, nb),
        in_specs=[
            pl.BlockSpec(memory_space=pltpu.SMEM),
            pl.BlockSpec((WINDOW, Q_W), cur),
            pl.BlockSpec((WINDOW, KV_W), prev), pl.BlockSpec((WINDOW, KV_W), cur),
            pl.BlockSpec((WINDOW, KV_W), prev), pl.BlockSpec((WINDOW, KV_W), cur),
        ],
        out_specs=pl.BlockSpec((WINDOW, Q_W), cur),
        out_shape=jax.ShapeDtypeStruct((BATCH * SEQ, Q_W), BF16),
        compiler_params=_params(("parallel", "parallel")),
        name="swa_prompt",
    )(sinks, q, k, k, v, v)


SAMPLE_TB = 8
KEY_PAD = 2 * WINDOW


def _swa_sample_kernel(q_ref, kn_ref, vn_ref, ck_ref, cv_ref, seg_ref, sl_ref, sk_ref,
                       o_ref, ok_ref, ov_ref):
    w = WINDOW
    srow = lax.broadcasted_iota(jnp.int32, (KEY_PAD, LANES), 0)
    distf = (w - srow).astype(F32)
    key_ok = srow <= w
    slope = sl_ref[...]
    sink = sk_ref[...]
    orow = lax.broadcasted_iota(jnp.int32, (LANES, KV_W), 0)
    olane = lax.broadcasted_iota(jnp.int32, (LANES, KV_W), 1)
    own_kv = (olane // HEAD_DIM) == (orow % N_KV)
    pad = jnp.zeros((KEY_PAD - w - 8, KV_W), F32)
    row8 = lax.broadcasted_iota(jnp.int32, (8, KV_W), 0)
    qf = q_ref[...].astype(F32)
    for bi in range(SAMPLE_TB):
        kc = ck_ref[bi]
        vc = cv_ref[bi]
        kn = kn_ref[bi:bi + 1, :]
        vn = vn_ref[bi:bi + 1, :]
        ok_ref[bi, pl.ds(0, w - 1), :] = ck_ref[bi, pl.ds(1, w - 1), :]
        ok_ref[bi, pl.ds(w - 1, 1), :] = kn
        ov_ref[bi, pl.ds(0, w - 1), :] = cv_ref[bi, pl.ds(1, w - 1), :]
        ov_ref[bi, pl.ds(w - 1, 1), :] = vn
        kn8 = jnp.where(row8 == 0, jnp.broadcast_to(kn, (8, KV_W)), 0.0)
        vn8 = jnp.where(row8 == 0, jnp.broadcast_to(vn, (8, KV_W)), 0.0)
        kfull = jnp.concatenate([kc, kn8, pad], axis=0)
        vfull = jnp.concatenate([vc, vn8, pad], axis=0).astype(BF16)
        s = jnp.zeros((KEY_PAD, LANES), F32)
        for g in range(GROUP):
            qrow = qf[bi:bi + 1, g * KV_W:(g + 1) * KV_W]
            s = s + _dot((kfull * qrow).astype(BF16), seg_ref[g])
        s = s - slope * distf
        s = jnp.where(key_ok, s, -jnp.inf)
        m = jnp.maximum(jnp.max(s, axis=0, keepdims=True), sink)
        p = jnp.exp(s - m)
        den = jnp.sum(p, axis=0, keepdims=True) + jnp.exp(sink - m)
        p = p * (1.0 / den)
        r = _dot_tn(p.astype(BF16), vfull)
        r = jnp.where(own_kv, r, 0.0)
        o_ref[bi] = r[0:N_HEADS, :]


def _swa_sample(q, kn, vn, ck, cv, seg, slopes, sinks):
    tb = SAMPLE_TB
    nb = DEC_BATCH // tb
    w = WINDOW

    def const(shape):
        return pl.BlockSpec(shape, lambda i: (0,) * len(shape))

    return pl.pallas_call(
        _swa_sample_kernel,
        grid=(nb,),
        in_specs=[
            pl.BlockSpec((tb, Q_W), lambda i: (i, 0)),
            pl.BlockSpec((tb, KV_W), lambda i: (i, 0)),
            pl.BlockSpec((tb, KV_W), lambda i: (i, 0)),
            pl.BlockSpec((tb, w, KV_W), lambda i: (i, 0, 0)),
            pl.BlockSpec((tb, w, KV_W), lambda i: (i, 0, 0)),
            const((GROUP, KV_W, LANES)), const((1, LANES)), const((1, LANES)),
        ],
        out_specs=[
            pl.BlockSpec((tb, N_HEADS, KV_W), lambda i: (i, 0, 0)),
            pl.BlockSpec((tb, w, KV_W), lambda i: (i, 0, 0)),
            pl.BlockSpec((tb, w, KV_W), lambda i: (i, 0, 0)),
        ],
        out_shape=[
            jax.ShapeDtypeStruct((DEC_BATCH, N_HEADS, KV_W), F32),
            jax.ShapeDtypeStruct((DEC_BATCH, w, KV_W), F32),
            jax.ShapeDtypeStruct((DEC_BATCH, w, KV_W), F32),
        ],
        compiler_params=_params(("parallel",)),
        name="swa_sample",
    )(q, kn, vn, ck, cv, seg, slopes, sinks)


def _gla_cum_matrix():
    c = GLA_CHUNK
    tri = np.tril(np.ones((c, c), np.float32))
    i = np.arange(c)
    blocks = []
    for lvl in range(GLA_LEVELS):
        half = 1 << lvl
        mid = (i // (2 * half)) * (2 * half) + half - 1
        blocks.append(tri - tri[mid])
    blocks.append(tri)
    return np.concatenate(blocks, axis=0)


def _split3(x):
    hi = x.astype(BF16)
    r1 = x - hi.astype(F32)
    mid = r1.astype(BF16)
    lo = (r1 - mid.astype(F32)).astype(BF16)
    return hi, mid, lo


def _gla_prompt_kernel(q_ref, k_ref, la_ref, v_ref, rb_ref, gn_ref, ut_ref, o_ref, s_ref, st_ref):
    c = pl.program_id(1)
    cl = GLA_CHUNK

    @pl.when(c == 0)
    def _():
        st_ref[0] = jnp.zeros(st_ref.shape[1:], F32)

    ut = ut_ref[...]
    hi, mid, lo = _split3(la_ref[...])
    tall = _dot(ut, hi) + _dot(ut, mid) + _dot(ut, lo)
    q = q_ref[...]
    k = k_ref[...]
    lane = lax.broadcasted_iota(jnp.int32, (cl, GK_W), 1)
    row = lax.broadcasted_iota(jnp.int32, (cl, GK_W), 0)
    head_of_lane = lane // GLA_DK
    ri = lax.broadcasted_iota(jnp.int32, (cl, cl), 0)
    ci = lax.broadcasted_iota(jnp.int32, (cl, cl), 1)
    attn = [jnp.zeros((cl, cl), F32) for _ in range(GLA_HEADS)]
    for lvl in range(GLA_LEVELS + 1):
        if lvl < GLA_LEVELS:
            e = jnp.exp(-jnp.abs(tall[lvl * cl:(lvl + 1) * cl]))
            upper = ((row >> lvl) & 1) == 1
            qt = jnp.where(upper, q * e, 0.0).astype(BF16)
            kt = jnp.where(upper, 0.0, k * e).astype(BF16)
            pair_ok = (ri >> (lvl + 1)) == (ci >> (lvl + 1))
        else:
            qt = q.astype(BF16)
            kt = k.astype(BF16)
            pair_ok = ri == ci
        for h in range(GLA_HEADS):
            kth = jnp.where(head_of_lane == h, kt, jnp.zeros_like(kt))
            attn[h] = attn[h] + jnp.where(pair_ok, _dot_nt(qt, kth), 0.0)

    cum = tall[GLA_LEVELS * cl:(GLA_LEVELS + 1) * cl]
    last = cum[cl - 1:cl, :]
    qe = (q * jnp.exp(cum)).astype(BF16)
    kd = (k * jnp.exp(last - cum)).astype(BF16)
    ones = jnp.ones((cl, LANES), BF16)
    dec = jnp.exp(_dot_tn(hi, ones) + _dot_tn(mid, ones) + _dot_tn(lo, ones))
    v = v_ref[...]
    rb = rb_ref[...]
    outs = []
    for h in range(GLA_HEADS):
        vh = v[:, h * GLA_DV:(h + 1) * GLA_DV].astype(BF16)
        st = st_ref[c % 2, h]
        o = _dot(attn[h].astype(BF16), vh) + _dot(qe, st.astype(BF16))
        kdh = jnp.where(head_of_lane == h, kd, jnp.zeros_like(kd))
        st_ref[(c + 1) % 2, h] = st * dec + _dot_tn(kdh, vh)
        ms = jnp.mean(o * o, axis=-1, keepdims=True)
        g = o * lax.rsqrt(ms + EPS) * gn_ref[...]
        outs.append(g * rb[:, h * GLA_DV:(h + 1) * GLA_DV].astype(F32))
    o_ref[...] = jnp.concatenate(outs, axis=-1).astype(BF16)

    @pl.when(c == pl.num_programs(1) - 1)
    def _():
        fin = (SEQ // GLA_CHUNK) % 2
        s_ref[...] = st_ref[fin, 0] + st_ref[fin, 1] + st_ref[fin, 2] + st_ref[fin, 3]


def _gla_prompt(gq, gk, la, gv, rbs, gn, ut):
    cl = GLA_CHUNK
    nc = SEQ // cl

    def row(b, c):
        return (b * nc + c, 0)

    return pl.pallas_call(
        _gla_prompt_kernel,
        grid=(BATCH, nc),
        in_specs=[
            pl.BlockSpec((cl, GK_W), row), pl.BlockSpec((cl, GK_W), row), pl.BlockSpec((cl, GK_W), row),
            pl.BlockSpec((cl, GV_W), row), pl.BlockSpec((cl, GV_W), row),
            pl.BlockSpec((1, GLA_DV), lambda b, c: (0, 0)),
            pl.BlockSpec(((GLA_LEVELS + 1) * cl, cl), lambda b, c: (0, 0)),
        ],
        out_specs=[
            pl.BlockSpec((cl, GV_W), row),
            pl.BlockSpec((None, GK_W, GLA_DV), lambda b, c: (b, 0, 0)),
        ],
        out_shape=[
            jax.ShapeDtypeStruct((BATCH * SEQ, GV_W), BF16),
            jax.ShapeDtypeStruct((BATCH, GK_W, GLA_DV), F32),
        ],
        scratch_shapes=[pltpu.VMEM((2, GLA_HEADS, GK_W, GLA_DV), F32)],
        compiler_params=_params(("parallel", "arbitrary")),
        name="gla_prompt",
    )(gq, gk, la, gv, rbs, gn, ut)


def _gla_sample_kernel(q_ref, k_ref, la_ref, v_ref, rb_ref, gn_ref, s_ref, o_ref, so_ref):
    tb = SAMPLE_TB
    dec = jnp.exp(la_ref[...])
    pieces = []
    for x in (dec, k_ref[...], q_ref[...]):
        hi, mid, lo = _split3(x)
        stacked = jnp.concatenate(
            [hi.astype(F32), mid.astype(F32), lo.astype(F32), jnp.zeros_like(x)], axis=0)
        pieces.append(stacked.astype(BF16))
    prow = lax.broadcasted_iota(jnp.int32, (4 * tb, LANES), 0)
    rb = rb_ref[...].astype(F32)
    v = v_ref[...]
    for bi in range(tb):
        sel = jnp.where((prow % tb) == bi, 1.0, 0.0).astype(BF16)
        a_col, k_col, q_col = [_dot_tn(p, sel) for p in pieces]
        for h in range(GLA_HEADS):
            rs = slice(h * GLA_DK, (h + 1) * GLA_DK)
            vs = slice(h * GLA_DV, (h + 1) * GLA_DV)
            s_new = a_col[rs] * s_ref[bi, rs, :] + k_col[rs] * v[bi:bi + 1, vs]
            so_ref[bi, rs, :] = s_new
            o = jnp.sum(q_col[rs] * s_new, axis=0, keepdims=True)
            ms = jnp.mean(o * o, axis=-1, keepdims=True)
            g = o * lax.rsqrt(ms + EPS) * gn_ref[...]
            o_ref[bi:bi + 1, vs] = g * rb[bi:bi + 1, vs]


def _gla_sample(gq, gk, la, gv, rbs, gn, state):
    tb = SAMPLE_TB
    nb = DEC_BATCH // tb

    def row(w):
        return pl.BlockSpec((tb, w), lambda i: (i, 0))

    st_spec = pl.BlockSpec((tb, GK_W, GLA_DV), lambda i: (i, 0, 0))
    return pl.pallas_call(
        _gla_sample_kernel,
        grid=(nb,),
        in_specs=[row(GK_W), row(GK_W), row(GK_W), row(GV_W), row(GV_W),
                  pl.BlockSpec((1, GLA_DV), lambda i: (0, 0)), st_spec],
        out_specs=[row(GV_W), st_spec],
        out_shape=[
            jax.ShapeDtypeStruct((DEC_BATCH, GV_W), F32),
            jax.ShapeDtypeStruct((DEC_BATCH, GK_W, GLA_DV), F32),
        ],
        compiler_params=_params(("parallel",)),
        name="gla_sample",
    )(gq, gk, la, gv, rbs, gn, state)


def _merge_kernel(x_ref, a_ref, g_ref, sga_ref, sgb_ref, wpa_ref, wpb_ref, wo_ref,
                  g1_ref, sh_ref, sc_ref, n2_ref, x1_ref, h2_ref):
    ya = _dot(a_ref[...].astype(BF16), wpa_ref[...])
    yb = _dot(g_ref[...].astype(BF16), wpb_ref[...])
    merged = sga_ref[...].astype(F32) * ya + sgb_ref[...].astype(F32) * yb
    mix = _dot(merged.astype(BF16), wo_ref[...])
    x1 = x_ref[...] + g1_ref[...] * mix
    x1_ref[...] = x1
    ms = jnp.mean(x1 * x1, axis=-1, keepdims=True)
    h = x1 * lax.rsqrt(ms + EPS) * n2_ref[...]
    h2_ref[...] = (h * (1.0 + sc_ref[...]) + sh_ref[...]).astype(h2_ref.dtype)


def _mod_spec(tm, n, mod, per_row_mod):
    if per_row_mod:
        return pl.BlockSpec((tm, D_MODEL), lambda i, *_: (i, 0))
    tiles_per_mod = (n // mod.shape[0]) // tm
    return pl.BlockSpec((None, 1, D_MODEL), lambda i, *_: (i // tiles_per_mod, 0, 0))


def _merge(x, a, g, sga, sgb, wpa, wpb, wo, g1, sh2, sc2, n2, *, tm, per_row_mod, h2_dtype=BF16):
    n = x.shape[0]
    mod_spec = _mod_spec(tm, n, g1, per_row_mod)

    def row(w):
        return pl.BlockSpec((tm, w), lambda i: (i, 0))

    def const(shape):
        return pl.BlockSpec(shape, lambda i: (0,) * len(shape))

    return pl.pallas_call(
        _merge_kernel,
        grid=(n // tm,),
        in_specs=[row(D_MODEL), row(a.shape[1]), row(GV_W), row(D_MODEL), row(D_MODEL),
                  const(wpa.shape), const(wpb.shape), const(wo.shape),
                  mod_spec, mod_spec, mod_spec, const((1, D_MODEL))],
        out_specs=[row(D_MODEL), row(D_MODEL)],
        out_shape=[jax.ShapeDtypeStruct((n, D_MODEL), F32), jax.ShapeDtypeStruct((n, D_MODEL), h2_dtype)],
        compiler_params=_params(("parallel",)),
        name="merge_out",
    )(x, a, g, sga, sgb, wpa, wpb, wo, g1, sh2, sc2, n2)


def _ffn_kernel(h_ref, x_ref, g2_ref, wg_ref, wu_ref, wd_ref, o_ref, acc_ref):
    f = pl.program_id(1)
    hb = h_ref[...]
    act = (_silu(_dot(hb, wg_ref[...])) * _dot(hb, wu_ref[...])).astype(BF16)
    y = _dot(act, wd_ref[...])

    @pl.when(f == 0)
    def _():
        acc_ref[...] = y

    @pl.when(f > 0)
    def _():
        acc_ref[...] += y

    @pl.when(f == pl.num_programs(1) - 1)
    def _():
        o_ref[...] = x_ref[...] + g2_ref[...] * acc_ref[...]


def _ffn(h2, x1, g2, wg, wu, wd, *, tm, tf, per_row_mod):
    n = h2.shape[0]
    mod_spec = _mod_spec(tm, n, g2, per_row_mod)
    return pl.pallas_call(
        _ffn_kernel,
        grid=(n // tm, D_FF // tf),
        in_specs=[
            pl.BlockSpec((tm, D_MODEL), lambda i, f: (i, 0)),
            pl.BlockSpec((tm, D_MODEL), lambda i, f: (i, 0)),
            mod_spec,
            pl.BlockSpec((D_MODEL, tf), lambda i, f: (0, f)),
            pl.BlockSpec((D_MODEL, tf), lambda i, f: (0, f)),
            pl.BlockSpec((tf, D_MODEL), lambda i, f: (f, 0)),
        ],
        out_specs=pl.BlockSpec((tm, D_MODEL), lambda i, f: (i, 0)),
        out_shape=jax.ShapeDtypeStruct((n, D_MODEL), F32),
        scratch_shapes=[pltpu.VMEM((tm, D_MODEL), F32)],
        compiler_params=_params(("parallel", "arbitrary")),
        name="ffn_dense",
    )(h2, x1, g2, wg, wu, wd)


def _moe_kernel(h_ref, x_ref, g2_ref, rw_ref, rbias_ref, wg_ref, wu_ref, wd_ref, o_ref, acc_ref, gate_ref):
    e = pl.program_id(1)
    hb = h_ref[...]
    tm = hb.shape[0]
    lane = lax.broadcasted_iota(jnp.int32, (tm, ROUTER_PAD), 1).astype(F32)

    @pl.when(e == 0)
    def _():
        logits = _dot(hb, rw_ref[...]) + rbias_ref[...]
        i1, i2, p1, p2 = _top2(logits, lane)
        gate_ref[...] = jnp.where(lane == i1, p1, 0.0) + jnp.where(lane == i2, p2, 0.0)
        acc_ref[...] = jnp.zeros_like(acc_ref)

    ge = jnp.sum(jnp.where(lane == e.astype(F32), gate_ref[...], 0.0), axis=-1, keepdims=True)
    act = (_silu(_dot(hb, wg_ref[...])) * _dot(hb, wu_ref[...])).astype(BF16)
    acc_ref[...] += ge * _dot(act, wd_ref[...])

    @pl.when(e == pl.num_programs(1) - 1)
    def _():
        o_ref[...] = x_ref[...] + g2_ref[...] * acc_ref[...]


def _moe(h2, x1, g2, rw, rbias, wg, wu, wd, *, tm, per_row_mod):
    n = h2.shape[0]
    mod_spec = _mod_spec(tm, n, g2, per_row_mod)
    fe = D_FF_EXPERT
    return pl.pallas_call(
        _moe_kernel,
        grid=(n // tm, N_EXPERTS),
        in_specs=[
            pl.BlockSpec((tm, D_MODEL), lambda i, e: (i, 0)),
            pl.BlockSpec((tm, D_MODEL), lambda i, e: (i, 0)),
            mod_spec,
            pl.BlockSpec((D_MODEL, ROUTER_PAD), lambda i, e: (0, 0)),
            pl.BlockSpec((1, ROUTER_PAD), lambda i, e: (0, 0)),
            pl.BlockSpec((None, D_MODEL, fe), lambda i, e: (e, 0, 0)),
            pl.BlockSpec((None, D_MODEL, fe), lambda i, e: (e, 0, 0)),
            pl.BlockSpec((None, fe, D_MODEL), lambda i, e: (e, 0, 0)),
        ],
        out_specs=pl.BlockSpec((tm, D_MODEL), lambda i, e: (i, 0)),
        out_shape=jax.ShapeDtypeStruct((n, D_MODEL), F32),
        scratch_shapes=[pltpu.VMEM((tm, D_MODEL), F32), pltpu.VMEM((tm, ROUTER_PAD), F32)],
        compiler_params=_params(("parallel", "arbitrary")),
        name="moe",
    )(h2, x1, g2, rw, rbias, wg, wu, wd)


def _top2(logits, lane):
    lg = jnp.where(lane < N_EXPERTS, logits, -jnp.inf)
    m1 = jnp.max(lg, axis=-1, keepdims=True)
    i1 = jnp.min(jnp.where(lg == m1, lane, float(ROUTER_PAD)), axis=-1, keepdims=True)
    lg2 = jnp.where(lane == i1, -jnp.inf, lg)
    m2 = jnp.max(lg2, axis=-1, keepdims=True)
    i2 = jnp.min(jnp.where(lg2 == m2, lane, float(ROUTER_PAD)), axis=-1, keepdims=True)
    e2 = jnp.exp(m2 - m1)
    p1 = 1.0 / (1.0 + e2)
    return i1, i2, p1, e2 * p1


def _router_kernel(h_ref, rw_ref, rbias_ref, o_ref):
    lane = lax.broadcasted_iota(jnp.int32, o_ref.shape, 1).astype(F32)
    logits = _dot(h_ref[...].astype(BF16), rw_ref[...]) + rbias_ref[...]
    i1, i2, p1, p2 = _top2(logits, lane)
    o_ref[...] = jnp.where(lane == 0.0, p1, jnp.where(lane == 1.0, p2,
                           jnp.where(lane == 2.0, i1, jnp.where(lane == 3.0, i2, 0.0))))


def _router(h2, rw, rbias, *, tm):
    n = h2.shape[0]
    return pl.pallas_call(
        _router_kernel,
        grid=(n // tm,),
        in_specs=[
            pl.BlockSpec((tm, D_MODEL), lambda i: (i, 0)),
            pl.BlockSpec((D_MODEL, ROUTER_PAD), lambda i: (0, 0)),
            pl.BlockSpec((1, ROUTER_PAD), lambda i: (0, 0)),
        ],
        out_specs=pl.BlockSpec((tm, ROUTER_PAD), lambda i: (i, 0)),
        out_shape=jax.ShapeDtypeStruct((n, ROUTER_PAD), F32),
        compiler_params=_params(("parallel",)),
        name="router",
    )(h2, rw, rbias)


def _route_tables(e1, e2, n, tm, n_tiles):
    e_flat = jnp.concatenate([e1, e2])
    onehot = (e_flat[:, None] == jnp.arange(N_EXPERTS, dtype=jnp.int32)[None, :]).astype(jnp.int32)
    csum = jnp.cumsum(onehot, axis=0)
    rank = jnp.sum(onehot * (csum - onehot), axis=1)
    cnt = csum[-1]
    gsz = ((cnt + tm - 1) // tm) * tm
    gend = jnp.cumsum(gsz)
    pos = (gend - gsz)[e_flat] + rank
    j = jnp.arange(2 * n, dtype=jnp.int32)
    total = n_tiles * tm
    src = jnp.zeros((total,), jnp.int32).at[pos].set(j % n)
    r = jnp.arange(total, dtype=jnp.int32)
    spill = 2 * n + ((r // tm) % 2) * tm + r % tm
    dst = spill.at[pos].set(j)
    tile_start = jnp.arange(n_tiles, dtype=jnp.int32) * tm
    tile_expert = jnp.sum((tile_start[:, None] >= gend[None, :]).astype(jnp.int32), axis=1)
    tile_expert = jnp.minimum(tile_expert, N_EXPERTS - 1)
    live = (gend[-1] // tm).reshape(1)
    return tile_expert, live, src, dst


def _moe_sparse_kernel(te_ref, live_ref, src_ref, dst_ref, h_hbm, wg_ref, wu_ref, wd_ref,
                       y_hbm, xbuf, ybuf, gsem, ssem):
    del te_ref
    t = pl.program_id(0)
    live = live_ref[0]
    tm = xbuf.shape[1]
    slot = t % 2

    def gather_row(tile, sl, r):
        tok = src_ref[tile * tm + r]
        return pltpu.make_async_copy(h_hbm.at[pl.ds(tok, 1)], xbuf.at[sl, pl.ds(r, 1)], gsem.at[sl])

    def scatter_row(tile, sl, r):
        row = dst_ref[tile * tm + r]
        return pltpu.make_async_copy(ybuf.at[sl, pl.ds(r, 1)], y_hbm.at[pl.ds(row, 1)], ssem.at[sl])

    def start_rows(make, tile, sl):
        def body(r, carry):
            make(tile, sl, r).start()
            return carry
        lax.fori_loop(0, tm, body, 0, unroll=8)

    def wait_gather(sl):
        pltpu.make_async_copy(h_hbm.at[pl.ds(0, tm)], xbuf.at[sl], gsem.at[sl]).wait()

    def wait_scatter(sl):
        pltpu.make_async_copy(ybuf.at[sl], y_hbm.at[pl.ds(0, tm)], ssem.at[sl]).wait()

    @pl.when(t == 0)
    def _():
        start_rows(gather_row, 0, 0)

    @pl.when(t < live)
    def _():
        wait_gather(slot)

        @pl.when(t + 1 < live)
        def _():
            start_rows(gather_row, t + 1, 1 - slot)

        hb = xbuf[slot].astype(BF16)
        act = (_silu(_dot(hb, wg_ref[...])) * _dot(hb, wu_ref[...])).astype(BF16)
        y = _dot(act, wd_ref[...])

        @pl.when(t >= 2)
        def _():
            wait_scatter(slot)

        ybuf[slot] = y
        start_rows(scatter_row, t, slot)

    @pl.when(t == pl.num_programs(0) - 1)
    def _():
        wait_scatter(0)
        wait_scatter(1)
        n_out = y_hbm.shape[0] - 2 * tm
        for sl in range(2):
            fill = pltpu.make_async_copy(ybuf.at[sl], y_hbm.at[pl.ds(n_out + sl * tm, tm)], ssem.at[sl])
            fill.start()
            fill.wait()


def _moe_sparse(h2, tile_expert, live, src, dst, wg, wu, wd, *, tm, n_tiles):
    n = h2.shape[0]
    fe = D_FF_EXPERT
    grid_spec = pltpu.PrefetchScalarGridSpec(
        num_scalar_prefetch=4,
        grid=(n_tiles,),
        in_specs=[
            pl.BlockSpec(memory_space=pl.ANY),
            pl.BlockSpec((None, D_MODEL, fe), lambda t, te, *_: (te[t], 0, 0)),
            pl.BlockSpec((None, D_MODEL, fe), lambda t, te, *_: (te[t], 0, 0)),
            pl.BlockSpec((None, fe, D_MODEL), lambda t, te, *_: (te[t], 0, 0)),
        ],
        out_specs=pl.BlockSpec(memory_space=pl.ANY),
        scratch_shapes=[
            pltpu.VMEM((2, tm, D_MODEL), F32),
            pltpu.VMEM((2, tm, D_MODEL), F32),
            pltpu.SemaphoreType.DMA((2,)),
            pltpu.SemaphoreType.DMA((2,)),
        ],
    )
    return pl.pallas_call(
        _moe_sparse_kernel,
        grid_spec=grid_spec,
        out_shape=jax.ShapeDtypeStruct((2 * n + 2 * tm, D_MODEL), F32),
        compiler_params=_params(("arbitrary",)),
        name="moe_sparse",
    )(tile_expert, live, src, dst, h2, wg, wu, wd)


def _moe_combine_kernel(x_ref, g2_ref, r_ref, y0_ref, y1_ref, o_ref):
    r = r_ref[...]
    f = r[:, 0:1] * y0_ref[...] + r[:, 1:2] * y1_ref[...]
    o_ref[...] = x_ref[...] + g2_ref[...] * f


def _moe_combine(x1, g2, route, ycat, *, tm):
    n = x1.shape[0]
    mod_spec = _mod_spec(tm, n, g2, False)
    nt = n // tm
    return pl.pallas_call(
        _moe_combine_kernel,
        grid=(nt,),
        in_specs=[
            pl.BlockSpec((tm, D_MODEL), lambda i: (i, 0)),
            mod_spec,
            pl.BlockSpec((tm, ROUTER_PAD), lambda i: (i, 0)),
            pl.BlockSpec((tm, D_MODEL), lambda i: (i, 0)),
            pl.BlockSpec((tm, D_MODEL), lambda i: (i + nt, 0)),
        ],
        out_specs=pl.BlockSpec((tm, D_MODEL), lambda i: (i, 0)),
        out_shape=jax.ShapeDtypeStruct((n, D_MODEL), F32),
        compiler_params=_params(("parallel",)),
        name="moe_combine",
    )(x1, g2, route, ycat, ycat)


def _moe_routed(h2, x1, g2, rw, rbias, wg, wu, wd, *, tm):
    n = h2.shape[0]
    n_tiles = (2 * n) // tm + N_EXPERTS
    route = _router(h2, rw, rbias, tm=tm)
    e1 = route[:, 2].astype(jnp.int32)
    e2 = route[:, 3].astype(jnp.int32)
    tile_expert, live, src, dst = _route_tables(e1, e2, n, tm, n_tiles)
    ycat = _moe_sparse(h2, tile_expert, live, src, dst, wg, wu, wd, tm=tm, n_tiles=n_tiles)
    return _moe_combine(x1, g2, route, ycat, tm=tm)


def _head_perm():
    idx = []
    for g in range(GROUP):
        for kv in range(N_KV):
            h = kv * GROUP + g
            idx.extend(range(h * HEAD_DIM, (h + 1) * HEAD_DIM))
    return np.asarray(idx, np.int32)


def _relayout_w_in(w):
    pts = np.cumsum([0, Q_W, KV_W, KV_W, GK_W, GK_W, GV_W, GATE_RANK, GV_W, D_MODEL, D_MODEL])
    qa, ka, va, qb, kb, vb, ga, rb, gta, gtb = [w[:, pts[i]:pts[i + 1]] for i in range(10)]
    qa = qa[:, _head_perm()]
    ga = jnp.pad(ga, ((0, 0), (0, GATE_PAD - GATE_RANK)))
    return jnp.concatenate([qa, ka, va, qb, kb, vb, rb, gta, gtb, ga], axis=1).astype(BF16)


def _expand_wpa(wpa_perm):
    out = jnp.zeros((N_HEADS, N_KV, HEAD_DIM, D_MODEL), wpa_perm.dtype)
    src = wpa_perm.reshape(GROUP, N_KV, HEAD_DIM, D_MODEL)
    for g in range(GROUP):
        for kv in range(N_KV):
            out = out.at[g * N_KV + kv, kv].set(src[g, kv])
    return out.reshape(N_HEADS * KV_W, D_MODEL)


def _sample_lane_consts(sinks):
    j = np.arange(N_HEADS)
    head = (j % N_KV) * GROUP + j // N_KV
    slopes = np.zeros((1, LANES), np.float32)
    slopes[0, :N_HEADS] = 2.0 ** (-8.0 * (head + 1) / N_HEADS)
    sink_row = jnp.zeros((1, LANES), F32).at[0, :N_HEADS].set(sinks[head])
    seg = np.zeros((GROUP, KV_W, LANES), np.float32)
    for g in range(GROUP):
        for kv in range(N_KV):
            seg[g, kv * HEAD_DIM:(kv + 1) * HEAD_DIM, g * N_KV + kv] = 1.0
    return jnp.asarray(seg, BF16), jnp.asarray(slopes), sink_row


def kernel(x_prompt, x_sample, cache_k, cache_v, state_gla, c_prompt, c_sample, ada_w, ada_b, norm1_g, norm2_g, w_in, q_norm_g, k_norm_g, attn_sinks, gla_wa2, gla_ba, gla_norm_g, w_branch_a, w_branch_b, w_out, ffn_w_gate, ffn_w_up, ffn_w_down, router_w, router_b, moe_w_gate, moe_w_up, moe_w_down):
    n_p = BATCH * SEQ
    xp = x_prompt.reshape(n_p, D_MODEL)
    xs = x_sample.reshape(DEC_BATCH, D_MODEL)

    c_rows = BATCH + DEC_BATCH
    c_pad = -c_rows % 8
    c_all = jnp.pad(jnp.concatenate([c_prompt, c_sample], axis=0), ((0, c_pad), (0, 0)))
    mod = _ada(c_all, ada_w, ada_b)

    bd = jnp.asarray(np.kron(np.eye(N_HEADS), np.ones((HEAD_DIM, HEAD_DIM))), BF16)
    ut = jnp.asarray(_gla_cum_matrix(), BF16)
    perm = _head_perm()

    kp_l, vp_l, sp_l, ks_l, vs_l, ss_l = [], [], [], [], [], []
    for l in range(DEPTH):
        m = mod[l].reshape(c_rows + c_pad, 6, D_MODEL)
        mod_p = [m[:BATCH, i].reshape(BATCH, 1, D_MODEL) for i in range(6)]
        mod_s = [m[BATCH:c_rows, i] for i in range(6)]

        w = _relayout_w_in(w_in[l])
        qg = (jnp.tile(q_norm_g[l], N_HEADS) * (HEAD_DIM ** -0.5)).reshape(1, Q_W)
        kg = jnp.tile(k_norm_g[l], N_KV).reshape(1, KV_W)
        wa2 = jnp.pad(gla_wa2[l], ((0, GATE_PAD - GATE_RANK), (0, 0))).astype(BF16)
        ba = gla_ba[l].reshape(1, GK_W)
        n1 = norm1_g[l].reshape(1, D_MODEL)
        n2 = norm2_g[l].reshape(1, D_MODEL)
        gn = gla_norm_g[l].reshape(1, GLA_DV)
        wpa = w_branch_a[l][perm].astype(BF16)
        wpa_x = _expand_wpa(wpa)
        wpb = w_branch_b[l].astype(BF16)
        wo = w_out[l].astype(BF16)
        sinks_perm = attn_sinks[l]
        seg, slope_row, sink_row = _sample_lane_consts(attn_sinks[l])

        q, k, v, gq, gk, gv, la, rbs, sga, sgb = _mixin(
            xp, mod_p[0], mod_p[1], n1, w, bd, qg, kg, wa2, ba, tm=512, per_row_mod=False)
        a_out = _swa_prompt(sinks_perm, q, k, v)
        g_out, s_fin = _gla_prompt(gq, gk, la, gv, rbs, gn, ut)
        x1, h2 = _merge(xp, a_out, g_out, sga, sgb, wpa, wpb, wo, mod_p[2], mod_p[3], mod_p[4], n2,
                        tm=512, per_row_mod=False, h2_dtype=BF16 if l % 2 == 0 else F32)
        kp_l.append(k.reshape(BATCH, SEQ, N_KV, HEAD_DIM)[:, SEQ - WINDOW:])
        vp_l.append(v.reshape(BATCH, SEQ, N_KV, HEAD_DIM)[:, SEQ - WINDOW:])
        sp_l.append(s_fin.reshape(BATCH, GLA_HEADS, GLA_DK, GLA_DV))

        qs, ksn, vsn, gqs, gks, gvs, las, rbss, sgas, sgbs = _mixin(
            xs, mod_s[0], mod_s[1], n1, w, bd, qg, kg, wa2, ba, tm=DEC_BATCH, per_row_mod=True)
        a_s, nk, nv = _swa_sample(qs, ksn, vsn, cache_k[l].reshape(DEC_BATCH, WINDOW, KV_W),
                                  cache_v[l].reshape(DEC_BATCH, WINDOW, KV_W), seg, slope_row, sink_row)
        g_s, s_new = _gla_sample(gqs, gks, las, gvs, rbss, gn, state_gla[l].reshape(DEC_BATCH, GK_W, GLA_DV))
        x1s, h2s = _merge(xs, a_s.reshape(DEC_BATCH, N_HEADS * KV_W), g_s, sgas, sgbs, wpa_x, wpb, wo,
                          mod_s[2], mod_s[3], mod_s[4], n2, tm=DEC_BATCH, per_row_mod=True)
        ks_l.append(nk.reshape(DEC_BATCH, WINDOW, N_KV, HEAD_DIM))
        vs_l.append(nv.reshape(DEC_BATCH, WINDOW, N_KV, HEAD_DIM))
        ss_l.append(s_new.reshape(DEC_BATCH, GLA_HEADS, GLA_DK, GLA_DV))

        i = l // 2
        if l % 2 == 0:
            wg, wu, wd = ffn_w_gate[i].astype(BF16), ffn_w_up[i].astype(BF16), ffn_w_down[i].astype(BF16)
            xp = _ffn(h2, x1, mod_p[5], wg, wu, wd, tm=512, tf=D_FF // 2, per_row_mod=False)
            xs = _ffn(h2s, x1s, mod_s[5], wg, wu, wd, tm=DEC_BATCH, tf=D_FF // 2, per_row_mod=True)
        else:
            rw = jnp.pad(router_w[i], ((0, 0), (0, ROUTER_PAD - N_EXPERTS))).astype(BF16)
            rbias = jnp.pad(router_b[i], (0, ROUTER_PAD - N_EXPERTS)).reshape(1, ROUTER_PAD)
            wg, wu, wd = moe_w_gate[i].astype(BF16), moe_w_up[i].astype(BF16), moe_w_down[i].astype(BF16)
            xp = _moe_routed(h2, x1, mod_p[5], rw, rbias, wg, wu, wd, tm=512)
            xs = _moe(h2s, x1s, mod_s[5], rw, rbias, wg, wu, wd, tm=DEC_BATCH, per_row_mod=True)

    return (xp.reshape(BATCH, SEQ, D_MODEL), xs.reshape(DEC_BATCH, 1, D_MODEL),
            jnp.stack(kp_l), jnp.stack(vp_l), jnp.stack(sp_l),
            jnp.stack(ks_l), jnp.stack(vs_l), jnp.stack(ss_l))
```

```python
import functools

import jax
import jax.numpy as jnp
import numpy as np
from jax import lax
from jax.experimental import pallas as pl
from jax.experimental.pallas import tpu as pltpu

D_MODEL = 1024
BATCH = 4
SEQ = 4096
DEPTH = 2
DEC_BATCH = 128
N_HEADS = 8
N_KV = 2
HEAD_DIM = 64
GROUP = N_HEADS // N_KV
WINDOW = 128
GLA_HEADS = 4
GLA_DK = 64
GLA_DV = 128
GATE_RANK = 16
GATE_TAU = 16.0
D_FF = 2816
N_EXPERTS = 8
D_FF_EXPERT = 1408
EPS = 1e-6

Q_W = N_HEADS * HEAD_DIM
KV_W = N_KV * HEAD_DIM
GK_W = GLA_HEADS * GLA_DK
GV_W = GLA_HEADS * GLA_DV

LANES = 128
GATE_PAD = LANES
ROUTER_PAD = LANES
VMEM_LIMIT = 56 * 1024 * 1024

F32 = jnp.float32
BF16 = jnp.bfloat16

_C_Q = 0
_C_K = _C_Q + Q_W
_C_V = _C_K + KV_W
_C_GQ = _C_V + KV_W
_C_GK = _C_GQ + GK_W
_C_GV = _C_GK + GK_W
_C_RB = _C_GV + GV_W
_C_GA = _C_RB + GV_W
_C_GB = _C_GA + D_MODEL
_C_LR = _C_GB + D_MODEL
PROJ_PAD = _C_LR + GATE_PAD

GLA_CHUNK = 128
GLA_LEVELS = 7


def _params(sem, vmem=VMEM_LIMIT):
    return pltpu.CompilerParams(dimension_semantics=sem, vmem_limit_bytes=vmem)


def _dot(a, b):
    return jnp.dot(a, b, preferred_element_type=F32)


def _dot_nt(a, b):
    return lax.dot_general(a, b, (((1,), (1,)), ((), ())), preferred_element_type=F32)


def _dot_tn(a, b):
    return lax.dot_general(a, b, (((0,), (0,)), ((), ())), preferred_element_type=F32)


def _sigmoid(x):
    return 1.0 / (1.0 + jnp.exp(-x))


def _silu(x):
    return x * _sigmoid(x)


def _ada_kernel(c_ref, w_ref, b_ref, o_ref):
    c = c_ref[...]
    o_ref[...] = _dot(_silu(c).astype(BF16), w_ref[...].astype(BF16)) + b_ref[...]


def _ada(c_all, ada_w, ada_b):
    rows = c_all.shape[0]
    tn = 1024
    return pl.pallas_call(
        _ada_kernel,
        grid=(DEPTH, 6 * D_MODEL // tn),
        in_specs=[
            pl.BlockSpec((rows, D_MODEL), lambda l, j: (0, 0)),
            pl.BlockSpec((None, D_MODEL, tn), lambda l, j: (l, 0, j)),
            pl.BlockSpec((None, 1, tn), lambda l, j: (l, 0, j)),
        ],
        out_specs=pl.BlockSpec((None, rows, tn), lambda l, j: (l, 0, j)),
        out_shape=jax.ShapeDtypeStruct((DEPTH, rows, 6 * D_MODEL), F32),
        compiler_params=_params(("parallel", "parallel")),
        name="ada_mod",
    )(c_all, ada_w, ada_b.reshape(DEPTH, 1, 6 * D_MODEL))


def _mixin_kernel(x_ref, sh_ref, sc_ref, n1_ref, w_ref, bd_ref, qg_ref, kg_ref, wa2_ref, ba_ref,
                  q_ref, k_ref, v_ref, gq_ref, gk_ref, gv_ref, la_ref, rb_ref, sga_ref, sgb_ref):
    x = x_ref[...]
    ms = jnp.mean(x * x, axis=-1, keepdims=True)
    h = x * lax.rsqrt(ms + EPS) * n1_ref[...]
    h = h * (1.0 + sc_ref[...]) + sh_ref[...]
    hb = h.astype(BF16)

    def proj(a, b):
        return _dot(hb, w_ref[:, a:b])

    q = proj(_C_Q, _C_K)
    ssq = _dot((q * q).astype(BF16), bd_ref[...])
    q_ref[...] = (q * lax.rsqrt(ssq * (1.0 / HEAD_DIM) + EPS) * qg_ref[...]).astype(BF16)
    k = proj(_C_K, _C_V)
    ssk = _dot((k * k).astype(BF16), bd_ref[0:KV_W, 0:KV_W])
    k_ref[...] = k * lax.rsqrt(ssk * (1.0 / HEAD_DIM) + EPS) * kg_ref[...]
    v_ref[...] = proj(_C_V, _C_GQ)
    gq_ref[...] = proj(_C_GQ, _C_GK) * (GLA_DK ** -0.5)
    gk_ref[...] = proj(_C_GK, _C_GV)
    gv_ref[...] = proj(_C_GV, _C_RB)
    rb_ref[...] = _silu(proj(_C_RB, _C_GA)).astype(BF16)
    sga_ref[...] = _sigmoid(proj(_C_GA, _C_GB)).astype(BF16)
    sgb_ref[...] = _sigmoid(proj(_C_GB, _C_LR)).astype(BF16)
    ga = proj(_C_LR, PROJ_PAD)
    xg = _dot(ga.astype(BF16), wa2_ref[...]) + ba_ref[...]
    la_ref[...] = (jnp.minimum(xg, 0.0) - jnp.log1p(jnp.exp(-jnp.abs(xg)))) * (1.0 / GATE_TAU)


def _mixin(x, sh, sc, n1, w, bd, qg, kg, wa2, ba, *, tm, per_row_mod):
    n = x.shape[0]
    nt = n // tm
    mod_spec = _mod_spec(tm, n, sh, per_row_mod)

    def row(i):
        return (i, 0)

    def const(shape):
        return pl.BlockSpec(shape, lambda i: (0,) * len(shape))

    def out(width, dtype):
        return pl.BlockSpec((tm, width), row), jax.ShapeDtypeStruct((n, width), dtype)

    outs = [out(Q_W, BF16), out(KV_W, F32), out(KV_W, F32), out(GK_W, F32), out(GK_W, F32),
            out(GV_W, F32), out(GK_W, F32), out(GV_W, BF16), out(D_MODEL, BF16), out(D_MODEL, BF16)]
    return pl.pallas_call(
        _mixin_kernel,
        grid=(nt,),
        in_specs=[
            pl.BlockSpec((tm, D_MODEL), row), mod_spec, mod_spec, const((1, D_MODEL)),
            const((D_MODEL, PROJ_PAD)), const((Q_W, Q_W)), const((1, Q_W)), const((1, KV_W)),
            const((GATE_PAD, GK_W)), const((1, GK_W)),
        ],
        out_specs=[o[0] for o in outs],
        out_shape=[o[1] for o in outs],
        compiler_params=_params(("parallel",)),
        name="mixer_in",
    )(x, sh, sc, n1, w, bd, qg, kg, wa2, ba)


def _head_slope(h):
    return float(2.0 ** (-8.0 * (h + 1) / N_HEADS))


def _swa_prompt_kernel(sink_ref, q_ref, kp_ref, kc_ref, vp_ref, vc_ref, o_ref):
    n = pl.program_id(1)
    blk = WINDOW
    kk = jnp.concatenate([kp_ref[...], kc_ref[...]], axis=0).astype(BF16)
    vv = jnp.concatenate([vp_ref[...], vc_ref[...]], axis=0).astype(BF16)
    row = lax.broadcasted_iota(jnp.int32, (blk, 2 * blk), 0)
    col = lax.broadcasted_iota(jnp.int32, (blk, 2 * blk), 1)
    dist = row + blk - col
    first_key = jnp.where(n > 0, 0, blk)
    valid = (dist >= 0) & (dist <= WINDOW) & (col >= first_key)
    distf = dist.astype(F32)
    klane = lax.broadcasted_iota(jnp.int32, (2 * blk, KV_W), 1)
    kk_kv = [jnp.where((klane // HEAD_DIM) == kv, kk, jnp.zeros_like(kk)) for kv in range(N_KV)]
    olane = lax.broadcasted_iota(jnp.int32, (blk, KV_W), 1)
    q = q_ref[...]
    outs = []
    for g in range(GROUP):
        qp = q[:, g * KV_W:(g + 1) * KV_W]
        pair = []
        for kv in range(N_KV):
            h = kv * GROUP + g
            s = _dot_nt(qp, kk_kv[kv]) - _head_slope(h) * distf
            s = jnp.where(valid, s, -jnp.inf)
            sink = sink_ref[h]
            m = jnp.maximum(jnp.max(s, axis=-1, keepdims=True), sink)
            p = jnp.exp(s - m)
            den = jnp.sum(p, axis=-1, keepdims=True) + jnp.exp(sink - m)
            pair.append(_dot(p.astype(BF16), vv) * (1.0 / den))
        outs.append(jnp.where(olane < HEAD_DIM, pair[0], pair[1]))
    o_ref[...] = jnp.concatenate(outs, axis=-1).astype(BF16)


def _swa_prompt(sinks, q, k, v):
    nb = SEQ // WINDOW

    def cur(b, n):
        return (b * nb + n, 0)

    def prev(b, n):
        return (b * nb + jnp.maximum(n - 1, 0), 0)

    return pl.pallas_call(
        _swa_prompt_kernel,
        grid=(BATCH, nb),
        in_specs=[
            pl.BlockSpec(memory_space=pltpu.SMEM),
            pl.BlockSpec((WINDOW, Q_W), cur),
            pl.BlockSpec((WINDOW, KV_W), prev), pl.BlockSpec((WINDOW, KV_W), cur),
            pl.BlockSpec((WINDOW, KV_W), prev), pl.BlockSpec((WINDOW, KV_W), cur),
        ],
        out_specs=pl.BlockSpec((WINDOW, Q_W), cur),
        out_shape=jax.ShapeDtypeStruct((BATCH * SEQ, Q_W), BF16),
        compiler_params=_params(("parallel", "parallel")),
        name="swa_prompt",
    )(sinks, q, k, k, v, v)


SAMPLE_TB = 8
KEY_PAD = 2 * WINDOW


def _swa_sample_kernel(q_ref, kn_ref, vn_ref, ck_ref, cv_ref, seg_ref, sl_ref, sk_ref,
                       o_ref, ok_ref, ov_ref):
    w = WINDOW
    srow = lax.broadcasted_iota(jnp.int32, (KEY_PAD, LANES), 0)
    distf = (w - srow).astype(F32)
    key_ok = srow <= w
    slope = sl_ref[...]
    sink = sk_ref[...]
    orow = lax.broadcasted_iota(jnp.int32, (LANES, KV_W), 0)
    olane = lax.broadcasted_iota(jnp.int32, (LANES, KV_W), 1)
    own_kv = (olane // HEAD_DIM) == (orow % N_KV)
    pad = jnp.zeros((KEY_PAD - w - 8, KV_W), F32)
    row8 = lax.broadcasted_iota(jnp.int32, (8, KV_W), 0)
    qf = q_ref[...].astype(F32)
    for bi in range(SAMPLE_TB):
        kc = ck_ref[bi]
        vc = cv_ref[bi]
        kn = kn_ref[bi:bi + 1, :]
        vn = vn_ref[bi:bi + 1, :]
        ok_ref[bi, pl.ds(0, w - 1), :] = ck_ref[bi, pl.ds(1, w - 1), :]
        ok_ref[bi, pl.ds(w - 1, 1), :] = kn
        ov_ref[bi, pl.ds(0, w - 1), :] = cv_ref[bi, pl.ds(1, w - 1), :]
        ov_ref[bi, pl.ds(w - 1, 1), :] = vn
        kn8 = jnp.where(row8 == 0, jnp.broadcast_to(kn, (8, KV_W)), 0.0)
        vn8 = jnp.where(row8 == 0, jnp.broadcast_to(vn, (8, KV_W)), 0.0)
        kfull = jnp.concatenate([kc, kn8, pad], axis=0)
        vfull = jnp.concatenate([vc, vn8, pad], axis=0).astype(BF16)
        s = jnp.zeros((KEY_PAD, LANES), F32)
        for g in range(GROUP):
            qrow = qf[bi:bi + 1, g * KV_W:(g + 1) * KV_W]
            s = s + _dot((kfull * qrow).astype(BF16), seg_ref[g])
        s = s - slope * distf
        s = jnp.where(key_ok, s, -jnp.inf)
        m = jnp.maximum(jnp.max(s, axis=0, keepdims=True), sink)
        p = jnp.exp(s - m)
        den = jnp.sum(p, axis=0, keepdims=True) + jnp.exp(sink - m)
        p = p * (1.0 / den)
        r = _dot_tn(p.astype(BF16), vfull)
        r = jnp.where(own_kv, r, 0.0)
        o_ref[bi] = r[0:N_HEADS, :]


def _swa_sample(q, kn, vn, ck, cv, seg, slopes, sinks):
    tb = SAMPLE_TB
    nb = DEC_BATCH // tb
    w = WINDOW

    def const(shape):
        return pl.BlockSpec(shape, lambda i: (0,) * len(shape))

    return pl.pallas_call(
        _swa_sample_kernel,
        grid=(nb,),
        in_specs=[
            pl.BlockSpec((tb, Q_W), lambda i: (i, 0)),
            pl.BlockSpec((tb, KV_W), lambda i: (i, 0)),
            pl.BlockSpec((tb, KV_W), lambda i: (i, 0)),
            pl.BlockSpec((tb, w, KV_W), lambda i: (i, 0, 0)),
            pl.BlockSpec((tb, w, KV_W), lambda i: (i, 0, 0)),
            const((GROUP, KV_W, LANES)), const((1, LANES)), const((1, LANES)),
        ],
        out_specs=[
            pl.BlockSpec((tb, N_HEADS, KV_W), lambda i: (i, 0, 0)),
            pl.BlockSpec((tb, w, KV_W), lambda i: (i, 0, 0)),
            pl.BlockSpec((tb, w, KV_W), lambda i: (i, 0, 0)),
        ],
        out_shape=[
            jax.ShapeDtypeStruct((DEC_BATCH, N_HEADS, KV_W), F32),
            jax.ShapeDtypeStruct((DEC_BATCH, w, KV_W), F32),
            jax.ShapeDtypeStruct((DEC_BATCH, w, KV_W), F32),
        ],
        compiler_params=_params(("parallel",)),
        name="swa_sample",
    )(q, kn, vn, ck, cv, seg, slopes, sinks)


def _gla_cum_matrix():
    c = GLA_CHUNK
    tri = np.tril(np.ones((c, c), np.float32))
    i = np.arange(c)
    blocks = []
    for lvl in range(GLA_LEVELS):
        half = 1 << lvl
        mid = (i // (2 * half)) * (2 * half) + half - 1
        blocks.append(tri - tri[mid])
    blocks.append(tri)
    return np.concatenate(blocks, axis=0)


def _split3(x):
    hi = x.astype(BF16)
    r1 = x - hi.astype(F32)
    mid = r1.astype(BF16)
    lo = (r1 - mid.astype(F32)).astype(BF16)
    return hi, mid, lo


def _gla_prompt_kernel(q_ref, k_ref, la_ref, v_ref, rb_ref, gn_ref, ut_ref, o_ref, s_ref, st_ref):
    c = pl.program_id(1)
    cl = GLA_CHUNK

    @pl.when(c == 0)
    def _():
        st_ref[0] = jnp.zeros(st_ref.shape[1:], F32)

    ut = ut_ref[...]
    hi, mid, lo = _split3(la_ref[...])
    tall = _dot(ut, hi) + _dot(ut, mid) + _dot(ut, lo)
    q = q_ref[...]
    k = k_ref[...]
    lane = lax.broadcasted_iota(jnp.int32, (cl, GK_W), 1)
    row = lax.broadcasted_iota(jnp.int32, (cl, GK_W), 0)
    head_of_lane = lane // GLA_DK
    ri = lax.broadcasted_iota(jnp.int32, (cl, cl), 0)
    ci = lax.broadcasted_iota(jnp.int32, (cl, cl), 1)
    attn = [jnp.zeros((cl, cl), F32) for _ in range(GLA_HEADS)]
    for lvl in range(GLA_LEVELS + 1):
        if lvl < GLA_LEVELS:
            e = jnp.exp(-jnp.abs(tall[lvl * cl:(lvl + 1) * cl]))
            upper = ((row >> lvl) & 1) == 1
            qt = jnp.where(upper, q * e, 0.0).astype(BF16)
            kt = jnp.where(upper, 0.0, k * e).astype(BF16)
            pair_ok = (ri >> (lvl + 1)) == (ci >> (lvl + 1))
        else:
            qt = q.astype(BF16)
            kt = k.astype(BF16)
            pair_ok = ri == ci
        for h in range(GLA_HEADS):
            kth = jnp.where(head_of_lane == h, kt, jnp.zeros_like(kt))
            attn[h] = attn[h] + jnp.where(pair_ok, _dot_nt(qt, kth), 0.0)

    cum = tall[GLA_LEVELS * cl:(GLA_LEVELS + 1) * cl]
    last = cum[cl - 1:cl, :]
    qe = (q * jnp.exp(cum)).astype(BF16)
    kd = (k * jnp.exp(last - cum)).astype(BF16)
    ones = jnp.ones((cl, LANES), BF16)
    dec = jnp.exp(_dot_tn(hi, ones) + _dot_tn(mid, ones) + _dot_tn(lo, ones))
    v = v_ref[...]
    rb = rb_ref[...]
    outs = []
    for h in range(GLA_HEADS):
        vh = v[:, h * GLA_DV:(h + 1) * GLA_DV].astype(BF16)
        st = st_ref[c % 2, h]
        o = _dot(attn[h].astype(BF16), vh) + _dot(qe, st.astype(BF16))
        kdh = jnp.where(head_of_lane == h, kd, jnp.zeros_like(kd))
        st_ref[(c + 1) % 2, h] = st * dec + _dot_tn(kdh, vh)
        ms = jnp.mean(o * o, axis=-1, keepdims=True)
        g = o * lax.rsqrt(ms + EPS) * gn_ref[...]
        outs.append(g * rb[:, h * GLA_DV:(h + 1) * GLA_DV].astype(F32))
    o_ref[...] = jnp.concatenate(outs, axis=-1).astype(BF16)

    @pl.when(c == pl.num_programs(1) - 1)
    def _():
        fin = (SEQ // GLA_CHUNK) % 2
        s_ref[...] = st_ref[fin, 0] + st_ref[fin, 1] + st_ref[fin, 2] + st_ref[fin, 3]


def _gla_prompt(gq, gk, la, gv, rbs, gn, ut):
    cl = GLA_CHUNK
    nc = SEQ // cl

    def row(b, c):
        return (b * nc + c, 0)

    return pl.pallas_call(
        _gla_prompt_kernel,
        grid=(BATCH, nc),
        in_specs=[
            pl.BlockSpec((cl, GK_W), row), pl.BlockSpec((cl, GK_W), row), pl.BlockSpec((cl, GK_W), row),
            pl.BlockSpec((cl, GV_W), row), pl.BlockSpec((cl, GV_W), row),
            pl.BlockSpec((1, GLA_DV), lambda b, c: (0, 0)),
            pl.BlockSpec(((GLA_LEVELS + 1) * cl, cl), lambda b, c: (0, 0)),
        ],
        out_specs=[
            pl.BlockSpec((cl, GV_W), row),
            pl.BlockSpec((None, GK_W, GLA_DV), lambda b, c: (b, 0, 0)),
        ],
        out_shape=[
            jax.ShapeDtypeStruct((BATCH * SEQ, GV_W), BF16),
            jax.ShapeDtypeStruct((BATCH, GK_W, GLA_DV), F32),
        ],
        scratch_shapes=[pltpu.VMEM((2, GLA_HEADS, GK_W, GLA_DV), F32)],
        compiler_params=_params(("parallel", "arbitrary")),
        name="gla_prompt",
    )(gq, gk, la, gv, rbs, gn, ut)


def _gla_sample_kernel(q_ref, k_ref, la_ref, v_ref, rb_ref, gn_ref, s_ref, o_ref, so_ref):
    tb = SAMPLE_TB
    dec = jnp.exp(la_ref[...])
    pieces = []
    for x in (dec, k_ref[...], q_ref[...]):
        hi, mid, lo = _split3(x)
        stacked = jnp.concatenate(
            [hi.astype(F32), mid.astype(F32), lo.astype(F32), jnp.zeros_like(x)], axis=0)
        pieces.append(stacked.astype(BF16))
    prow = lax.broadcasted_iota(jnp.int32, (4 * tb, LANES), 0)
    rb = rb_ref[...].astype(F32)
    v = v_ref[...]
    for bi in range(tb):
        sel = jnp.where((prow % tb) == bi, 1.0, 0.0).astype(BF16)
        a_col, k_col, q_col = [_dot_tn(p, sel) for p in pieces]
        for h in range(GLA_HEADS):
            rs = slice(h * GLA_DK, (h + 1) * GLA_DK)
            vs = slice(h * GLA_DV, (h + 1) * GLA_DV)
            s_new = a_col[rs] * s_ref[bi, rs, :] + k_col[rs] * v[bi:bi + 1, vs]
            so_ref[bi, rs, :] = s_new
            o = jnp.sum(q_col[rs] * s_new, axis=0, keepdims=True)
            ms = jnp.mean(o * o, axis=-1, keepdims=True)
            g = o * lax.rsqrt(ms + EPS) * gn_ref[...]
            o_ref[bi:bi + 1, vs] = g * rb[bi:bi + 1, vs]


def _gla_sample(gq, gk, la, gv, rbs, gn, state):
    tb = SAMPLE_TB
    nb = DEC_BATCH // tb

    def row(w):
        return pl.BlockSpec((tb, w), lambda i: (i, 0))

    st_spec = pl.BlockSpec((tb, GK_W, GLA_DV), lambda i: (i, 0, 0))
    return pl.pallas_call(
        _gla_sample_kernel,
        grid=(nb,),
        in_specs=[row(GK_W), row(GK_W), row(GK_W), row(GV_W), row(GV_W),
                  pl.BlockSpec((1, GLA_DV), lambda i: (0, 0)), st_spec],
        out_specs=[row(GV_W), st_spec],
        out_shape=[
            jax.ShapeDtypeStruct((DEC_BATCH, GV_W), F32),
            jax.ShapeDtypeStruct((DEC_BATCH, GK_W, GLA_DV), F32),
        ],
        compiler_params=_params(("parallel",)),
        name="gla_sample",
    )(gq, gk, la, gv, rbs, gn, state)


def _merge_kernel(x_ref, a_ref, g_ref, sga_ref, sgb_ref, wpa_ref, wpb_ref, wo_ref,
                  g1_ref, sh_ref, sc_ref, n2_ref, x1_ref, h2_ref):
    ya = _dot(a_ref[...].astype(BF16), wpa_ref[...])
    yb = _dot(g_ref[...].astype(BF16), wpb_ref[...])
    merged = sga_ref[...].astype(F32) * ya + sgb_ref[...].astype(F32) * yb
    mix = _dot(merged.astype(BF16), wo_ref[...])
    x1 = x_ref[...] + g1_ref[...] * mix
    x1_ref[...] = x1
    ms = jnp.mean(x1 * x1, axis=-1, keepdims=True)
    h = x1 * lax.rsqrt(ms + EPS) * n2_ref[...]
    h2_ref[...] = (h * (1.0 + sc_ref[...]) + sh_ref[...]).astype(h2_ref.dtype)


def _mod_spec(tm, n, mod, per_row_mod):
    if per_row_mod:
        return pl.BlockSpec((tm, D_MODEL), lambda i, *_: (i, 0))
    tiles_per_mod = (n // mod.shape[0]) // tm
    return pl.BlockSpec((None, 1, D_MODEL), lambda i, *_: (i // tiles_per_mod, 0, 0))


def _merge(x, a, g, sga, sgb, wpa, wpb, wo, g1, sh2, sc2, n2, *, tm, per_row_mod, h2_dtype=BF16):
    n = x.shape[0]
    mod_spec = _mod_spec(tm, n, g1, per_row_mod)

    def row(w):
        return pl.BlockSpec((tm, w), lambda i: (i, 0))

    def const(shape):
        return pl.BlockSpec(shape, lambda i: (0,) * len(shape))

    return pl.pallas_call(
        _merge_kernel,
        grid=(n // tm,),
        in_specs=[row(D_MODEL), row(a.shape[1]), row(GV_W), row(D_MODEL), row(D_MODEL),
                  const(wpa.shape), const(wpb.shape), const(wo.shape),
                  mod_spec, mod_spec, mod_spec, const((1, D_MODEL))],
        out_specs=[row(D_MODEL), row(D_MODEL)],
        out_shape=[jax.ShapeDtypeStruct((n, D_MODEL), F32), jax.ShapeDtypeStruct((n, D_MODEL), h2_dtype)],
        compiler_params=_params(("parallel",)),
        name="merge_out",
    )(x, a, g, sga, sgb, wpa, wpb, wo, g1, sh2, sc2, n2)


def _ffn_kernel(h_ref, x_ref, g2_ref, wg_ref, wu_ref, wd_ref, o_ref, acc_ref):
    f = pl.program_id(1)
    hb = h_ref[...]
    act = (_silu(_dot(hb, wg_ref[...])) * _dot(hb, wu_ref[...])).astype(BF16)
    y = _dot(act, wd_ref[...])

    @pl.when(f == 0)
    def _():
        acc_ref[...] = y

    @pl.when(f > 0)
    def _():
        acc_ref[...] += y

    @pl.when(f == pl.num_programs(1) - 1)
    def _():
        o_ref[...] = x_ref[...] + g2_ref[...] * acc_ref[...]


def _ffn(h2, x1, g2, wg, wu, wd, *, tm, tf, per_row_mod):
    n = h2.shape[0]
    mod_spec = _mod_spec(tm, n, g2, per_row_mod)
    return pl.pallas_call(
        _ffn_kernel,
        grid=(n // tm, D_FF // tf),
        in_specs=[
            pl.BlockSpec((tm, D_MODEL), lambda i, f: (i, 0)),
            pl.BlockSpec((tm, D_MODEL), lambda i, f: (i, 0)),
            mod_spec,
            pl.BlockSpec((D_MODEL, tf), lambda i, f: (0, f)),
            pl.BlockSpec((D_MODEL, tf), lambda i, f: (0, f)),
            pl.BlockSpec((tf, D_MODEL), lambda i, f: (f, 0)),
        ],
        out_specs=pl.BlockSpec((tm, D_MODEL), lambda i, f: (i, 0)),
        out_shape=jax.ShapeDtypeStruct((n, D_MODEL), F32),
        scratch_shapes=[pltpu.VMEM((tm, D_MODEL), F32)],
        compiler_params=_params(("parallel", "arbitrary")),
        name="ffn_dense",
    )(h2, x1, g2, wg, wu, wd)


def _moe_kernel(h_ref, x_ref, g2_ref, rw_ref, rbias_ref, wg_ref, wu_ref, wd_ref, o_ref, acc_ref, gate_ref):
    e = pl.program_id(1)
    hb = h_ref[...]
    tm = hb.shape[0]
    lane = lax.broadcasted_iota(jnp.int32, (tm, ROUTER_PAD), 1).astype(F32)

    @pl.when(e == 0)
    def _():
        logits = _dot(hb, rw_ref[...]) + rbias_ref[...]
        i1, i2, p1, p2 = _top2(logits, lane)
        gate_ref[...] = jnp.where(lane == i1, p1, 0.0) + jnp.where(lane == i2, p2, 0.0)
        acc_ref[...] = jnp.zeros_like(acc_ref)

    ge = jnp.sum(jnp.where(lane == e.astype(F32), gate_ref[...], 0.0), axis=-1, keepdims=True)
    act = (_silu(_dot(hb, wg_ref[...])) * _dot(hb, wu_ref[...])).astype(BF16)
    acc_ref[...] += ge * _dot(act, wd_ref[...])

    @pl.when(e == pl.num_programs(1) - 1)
    def _():
        o_ref[...] = x_ref[...] + g2_ref[...] * acc_ref[...]


def _moe(h2, x1, g2, rw, rbias, wg, wu, wd, *, tm, per_row_mod):
    n = h2.shape[0]
    mod_spec = _mod_spec(tm, n, g2, per_row_mod)
    fe = D_FF_EXPERT
    return pl.pallas_call(
        _moe_kernel,
        grid=(n // tm, N_EXPERTS),
        in_specs=[
            pl.BlockSpec((tm, D_MODEL), lambda i, e: (i, 0)),
            pl.BlockSpec((tm, D_MODEL), lambda i, e: (i, 0)),
            mod_spec,
            pl.BlockSpec((D_MODEL, ROUTER_PAD), lambda i, e: (0, 0)),
            pl.BlockSpec((1, ROUTER_PAD), lambda i, e: (0, 0)),
            pl.BlockSpec((None, D_MODEL, fe), lambda i, e: (e, 0, 0)),
            pl.BlockSpec((None, D_MODEL, fe), lambda i, e: (e, 0, 0)),
            pl.BlockSpec((None, fe, D_MODEL), lambda i, e: (e, 0, 0)),
        ],
        out_specs=pl.BlockSpec((tm, D_MODEL), lambda i, e: (i, 0)),
        out_shape=jax.ShapeDtypeStruct((n, D_MODEL), F32),
        scratch_shapes=[pltpu.VMEM((tm, D_MODEL), F32), pltpu.VMEM((tm, ROUTER_PAD), F32)],
        compiler_params=_params(("parallel", "arbitrary")),
        name="moe",
    )(h2, x1, g2, rw, rbias, wg, wu, wd)


def _top2(logits, lane):
    lg = jnp.where(lane < N_EXPERTS, logits, -jnp.inf)
    m1 = jnp.max(lg, axis=-1, keepdims=True)
    i1 = jnp.min(jnp.where(lg == m1, lane, float(ROUTER_PAD)), axis=-1, keepdims=True)
    lg2 = jnp.where(lane == i1, -jnp.inf, lg)
    m2 = jnp.max(lg2, axis=-1, keepdims=True)
    i2 = jnp.min(jnp.where(lg2 == m2, lane, float(ROUTER_PAD)), axis=-1, keepdims=True)
    e2 = jnp.exp(m2 - m1)
    p1 = 1.0 / (1.0 + e2)
    return i1, i2, p1, e2 * p1


def _router_kernel(h_ref, rw_ref, rbias_ref, o_ref):
    lane = lax.broadcasted_iota(jnp.int32, o_ref.shape, 1).astype(F32)
    logits = _dot(h_ref[...].astype(BF16), rw_ref[...]) + rbias_ref[...]
    i1, i2, p1, p2 = _top2(logits, lane)
    o_ref[...] = jnp.where(lane == 0.0, p1, jnp.where(lane == 1.0, p2,
                           jnp.where(lane == 2.0, i1, jnp.where(lane == 3.0, i2, 0.0))))


def _router(h2, rw, rbias, *, tm):
    n = h2.shape[0]
    return pl.pallas_call(
        _router_kernel,
        grid=(n // tm,),
        in_specs=[
            pl.BlockSpec((tm, D_MODEL), lambda i: (i, 0)),
            pl.BlockSpec((D_MODEL, ROUTER_PAD), lambda i: (0, 0)),
            pl.BlockSpec((1, ROUTER_PAD), lambda i: (0, 0)),
        ],
        out_specs=pl.BlockSpec((tm, ROUTER_PAD), lambda i: (i, 0)),
        out_shape=jax.ShapeDtypeStruct((n, ROUTER_PAD), F32),
        compiler_params=_params(("parallel",)),
        name="router",
    )(h2, rw, rbias)


def _route_tables(e1, e2, n, tm, n_tiles):
    e_flat = jnp.concatenate([e1, e2])
    onehot = (e_flat[:, None] == jnp.arange(N_EXPERTS, dtype=jnp.int32)[None, :]).astype(jnp.int32)
    csum = jnp.cumsum(onehot, axis=0)
    rank = jnp.sum(onehot * (csum - onehot), axis=1)
    cnt = csum[-1]
    gsz = ((cnt + tm - 1) // tm) * tm
    gend = jnp.cumsum(gsz)
    pos = (gend - gsz)[e_flat] + rank
    j = jnp.arange(2 * n, dtype=jnp.int32)
    total = n_tiles * tm
    inv = jnp.full((total,), -1, jnp.int32).at[pos].set(j)
    r = jnp.arange(total, dtype=jnp.int32)
    spill = 2 * n + r % tm
    src = jnp.concatenate([jnp.where(inv >= 0, inv % n, 0), jnp.zeros((tm,), jnp.int32)])
    dst = jnp.concatenate([spill[:tm], jnp.where(inv >= 0, inv, spill)])
    tile_start = jnp.arange(n_tiles + 1, dtype=jnp.int32) * tm
    tile_expert = jnp.sum((tile_start[:, None] >= gend[None, :]).astype(jnp.int32), axis=1)
    tile_expert = jnp.minimum(tile_expert, N_EXPERTS - 1)
    live = (gend[-1] // tm).reshape(1)
    return tile_expert, live, src, dst


def _moe_sparse_kernel(te_ref, live_ref, src_ref, dst_ref, h_hbm, wg_ref, wu_ref, wd_ref,
                       y_hbm, xbuf, hbuf, ybuf, gsem, ssem):
    del te_ref
    t = pl.program_id(0)
    live = live_ref[0]
    tm = xbuf.shape[0]

    def gather_row(tile, r):
        tok = src_ref[tile * tm + r]
        return pltpu.make_async_copy(h_hbm.at[pl.ds(tok, 1)], xbuf.at[pl.ds(r, 1)], gsem)

    def scatter_row(tile, r):
        row = dst_ref[tile * tm + r]
        return pltpu.make_async_copy(ybuf.at[pl.ds(r, 1)], y_hbm.at[pl.ds(row, 1)], ssem)

    def start_rows(make, tile):
        def body(r, carry):
            make(tile, r).start()
            return carry
        lax.fori_loop(0, tm, body, 0, unroll=8)

    def wait_gather():
        pltpu.make_async_copy(h_hbm.at[pl.ds(0, tm)], xbuf, gsem).wait()

    def wait_scatter():
        pltpu.make_async_copy(ybuf, y_hbm.at[pl.ds(0, tm)], ssem).wait()

    @pl.when(t == 0)
    def _():
        ybuf[...] = jnp.zeros(ybuf.shape, F32)
        start_rows(gather_row, 0)

    @pl.when(t < live)
    def _():
        wait_gather()
        hbuf[...] = xbuf[...].astype(BF16)
        for r in range(tm):
            gather_row(t + 1, r).start()
            scatter_row(t, r).start()

        hb = hbuf[...]
        act = (_silu(_dot(hb, wg_ref[...])) * _dot(hb, wu_ref[...])).astype(BF16)
        y = _dot(act, wd_ref[...])
        wait_scatter()
        ybuf[...] = y

    @pl.when(t == live)
    def _():
        wait_gather()
        start_rows(scatter_row, t)
        wait_scatter()
        fill = pltpu.make_async_copy(ybuf, y_hbm.at[pl.ds(y_hbm.shape[0] - tm, tm)], ssem)
        fill.start()
        fill.wait()


def _moe_sparse(h2, tile_expert, live, src, dst, wg, wu, wd, *, tm, n_tiles):
    n = h2.shape[0]
    fe = D_FF_EXPERT
    grid_spec = pltpu.PrefetchScalarGridSpec(
        num_scalar_prefetch=4,
        grid=(n_tiles + 1,),
        in_specs=[
            pl.BlockSpec(memory_space=pl.ANY),
            pl.BlockSpec((None, D_MODEL, fe), lambda t, te, *_: (te[t], 0, 0)),
            pl.BlockSpec((None, D_MODEL, fe), lambda t, te, *_: (te[t], 0, 0)),
            pl.BlockSpec((None, fe, D_MODEL), lambda t, te, *_: (te[t], 0, 0)),
        ],
        out_specs=pl.BlockSpec(memory_space=pl.ANY),
        scratch_shapes=[
            pltpu.VMEM((tm, D_MODEL), F32),
            pltpu.VMEM((tm, D_MODEL), BF16),
            pltpu.VMEM((tm, D_MODEL), F32),
            pltpu.SemaphoreType.DMA(()),
            pltpu.SemaphoreType.DMA(()),
        ],
    )
    return pl.pallas_call(
        _moe_sparse_kernel,
        grid_spec=grid_spec,
        out_shape=jax.ShapeDtypeStruct((2 * n + tm, D_MODEL), F32),
        compiler_params=_params(("arbitrary",)),
        name="moe_sparse",
    )(tile_expert, live, src, dst, h2, wg, wu, wd)


def _moe_combine_kernel(x_ref, g2_ref, r_ref, y0_ref, y1_ref, o_ref):
    r = r_ref[...]
    f = r[:, 0:1] * y0_ref[...] + r[:, 1:2] * y1_ref[...]
    o_ref[...] = x_ref[...] + g2_ref[...] * f


def _moe_combine(x1, g2, route, ycat, *, tm):
    n = x1.shape[0]
    mod_spec = _mod_spec(tm, n, g2, False)
    nt = n // tm
    return pl.pallas_call(
        _moe_combine_kernel,
        grid=(nt,),
        in_specs=[
            pl.BlockSpec((tm, D_MODEL), lambda i: (i, 0)),
            mod_spec,
            pl.BlockSpec((tm, ROUTER_PAD), lambda i: (i, 0)),
            pl.BlockSpec((tm, D_MODEL), lambda i: (i, 0)),
            pl.BlockSpec((tm, D_MODEL), lambda i: (i + nt, 0)),
        ],
        out_specs=pl.BlockSpec((tm, D_MODEL), lambda i: (i, 0)),
        out_shape=jax.ShapeDtypeStruct((n, D_MODEL), F32),
        compiler_params=_params(("parallel",)),
        name="moe_combine",
    )(x1, g2, route, ycat, ycat)


def _moe_routed(h2, x1, g2, rw, rbias, wg, wu, wd, *, tm):
    n = h2.shape[0]
    n_tiles = (2 * n) // tm + N_EXPERTS
    route = _router(h2, rw, rbias, tm=tm)
    e1 = route[:, 2].astype(jnp.int32)
    e2 = route[:, 3].astype(jnp.int32)
    tile_expert, live, src, dst = _route_tables(e1, e2, n, tm, n_tiles)
    ycat = _moe_sparse(h2, tile_expert, live, src, dst, wg, wu, wd, tm=tm, n_tiles=n_tiles)
    return _moe_combine(x1, g2, route, ycat, tm=tm)


def _head_perm():
    idx = []
    for g in range(GROUP):
        for kv in range(N_KV):
            h = kv * GROUP + g
            idx.extend(range(h * HEAD_DIM, (h + 1) * HEAD_DIM))
    return np.asarray(idx, np.int32)


def _relayout_w_in(w):
    pts = np.cumsum([0, Q_W, KV_W, KV_W, GK_W, GK_W, GV_W, GATE_RANK, GV_W, D_MODEL, D_MODEL])
    qa, ka, va, qb, kb, vb, ga, rb, gta, gtb = [w[:, pts[i]:pts[i + 1]] for i in range(10)]
    qa = qa[:, _head_perm()]
    ga = jnp.pad(ga, ((0, 0), (0, GATE_PAD - GATE_RANK)))
    return jnp.concatenate([qa, ka, va, qb, kb, vb, rb, gta, gtb, ga], axis=1).astype(BF16)


def _expand_wpa(wpa_perm):
    src = wpa_perm.reshape(GROUP, N_KV, 1, HEAD_DIM, D_MODEL)
    own = jnp.asarray(np.eye(N_KV, dtype=bool).reshape(1, N_KV, N_KV, 1, 1))
    return jnp.where(own, src, jnp.zeros_like(src)).reshape(N_HEADS * KV_W, D_MODEL)


def _sample_lane_consts(sinks):
    j = np.arange(N_HEADS)
    head = (j % N_KV) * GROUP + j // N_KV
    slopes = np.zeros((1, LANES), np.float32)
    slopes[0, :N_HEADS] = 2.0 ** (-8.0 * (head + 1) / N_HEADS)
    sink_row = jnp.zeros((1, LANES), F32).at[0, :N_HEADS].set(sinks[head])
    seg = np.zeros((GROUP, KV_W, LANES), np.float32)
    for g in range(GROUP):
        for kv in range(N_KV):
            seg[g, kv * HEAD_DIM:(kv + 1) * HEAD_DIM, g * N_KV + kv] = 1.0
    return jnp.asarray(seg, BF16), jnp.asarray(slopes), sink_row


def kernel(x_prompt, x_sample, cache_k, cache_v, state_gla, c_prompt, c_sample, ada_w, ada_b, norm1_g, norm2_g, w_in, q_norm_g, k_norm_g, attn_sinks, gla_wa2, gla_ba, gla_norm_g, w_branch_a, w_branch_b, w_out, ffn_w_gate, ffn_w_up, ffn_w_down, router_w, router_b, moe_w_gate, moe_w_up, moe_w_down):
    n_p = BATCH * SEQ
    xp = x_prompt.reshape(n_p, D_MODEL)
    xs = x_sample.reshape(DEC_BATCH, D_MODEL)

    c_rows = BATCH + DEC_BATCH
    c_pad = -c_rows % 8
    c_all = jnp.pad(jnp.concatenate([c_prompt, c_sample], axis=0), ((0, c_pad), (0, 0)))
    mod = _ada(c_all, ada_w, ada_b)

    bd = jnp.asarray(np.kron(np.eye(N_HEADS), np.ones((HEAD_DIM, HEAD_DIM))), BF16)
    ut = jnp.asarray(_gla_cum_matrix(), BF16)
    perm = _head_perm()

    kp_l, vp_l, sp_l, ks_l, vs_l, ss_l = [], [], [], [], [], []
    for l in range(DEPTH):
        m = mod[l].reshape(c_rows + c_pad, 6, D_MODEL)
        mod_p = [m[:BATCH, i].reshape(BATCH, 1, D_MODEL) for i in range(6)]
        mod_s = [m[BATCH:c_rows, i] for i in range(6)]

        w = _relayout_w_in(w_in[l])
        qg = (jnp.tile(q_norm_g[l], N_HEADS) * (HEAD_DIM ** -0.5)).reshape(1, Q_W)
        kg = jnp.tile(k_norm_g[l], N_KV).reshape(1, KV_W)
        wa2 = jnp.pad(gla_wa2[l], ((0, GATE_PAD - GATE_RANK), (0, 0))).astype(BF16)
        ba = gla_ba[l].reshape(1, GK_W)
        n1 = norm1_g[l].reshape(1, D_MODEL)
        n2 = norm2_g[l].reshape(1, D_MODEL)
        gn = gla_norm_g[l].reshape(1, GLA_DV)
        wpa = w_branch_a[l][perm].astype(BF16)
        wpa_x = _expand_wpa(wpa)
        wpb = w_branch_b[l].astype(BF16)
        wo = w_out[l].astype(BF16)
        sinks_perm = attn_sinks[l]
        seg, slope_row, sink_row = _sample_lane_consts(attn_sinks[l])

        q, k, v, gq, gk, gv, la, rbs, sga, sgb = _mixin(
            xp, mod_p[0], mod_p[1], n1, w, bd, qg, kg, wa2, ba, tm=512, per_row_mod=False)
        a_out = _swa_prompt(sinks_perm, q, k, v)
        g_out, s_fin = _gla_prompt(gq, gk, la, gv, rbs, gn, ut)
        x1, h2 = _merge(xp, a_out, g_out, sga, sgb, wpa, wpb, wo, mod_p[2], mod_p[3], mod_p[4], n2,
                        tm=512, per_row_mod=False, h2_dtype=BF16 if l % 2 == 0 else F32)
        kp_l.append(k.reshape(BATCH, SEQ, N_KV, HEAD_DIM)[:, SEQ - WINDOW:])
        vp_l.append(v.reshape(BATCH, SEQ, N_KV, HEAD_DIM)[:, SEQ - WINDOW:])
        sp_l.append(s_fin.reshape(BATCH, GLA_HEADS, GLA_DK, GLA_DV))

        qs, ksn, vsn, gqs, gks, gvs, las, rbss, sgas, sgbs = _mixin(
            xs, mod_s[0], mod_s[1], n1, w, bd, qg, kg, wa2, ba, tm=DEC_BATCH, per_row_mod=True)
        a_s, nk, nv = _swa_sample(qs, ksn, vsn, cache_k[l].reshape(DEC_BATCH, WINDOW, KV_W),
                                  cache_v[l].reshape(DEC_BATCH, WINDOW, KV_W), seg, slope_row, sink_row)
        g_s, s_new = _gla_sample(gqs, gks, las, gvs, rbss, gn, state_gla[l].reshape(DEC_BATCH, GK_W, GLA_DV))
        x1s, h2s = _merge(xs, a_s.reshape(DEC_BATCH, N_HEADS * KV_W), g_s, sgas, sgbs, wpa_x, wpb, wo,
                          mod_s[2], mod_s[3], mod_s[4], n2, tm=DEC_BATCH, per_row_mod=True)
        ks_l.append(nk.reshape(DEC_BATCH, WINDOW, N_KV, HEAD_DIM))
        vs_l.append(nv.reshape(DEC_BATCH, WINDOW, N_KV, HEAD_DIM))
        ss_l.append(s_new.reshape(DEC_BATCH, GLA_HEADS, GLA_DK, GLA_DV))

        i = l // 2
        if l % 2 == 0:
            wg, wu, wd = ffn_w_gate[i].astype(BF16), ffn_w_up[i].astype(BF16), ffn_w_down[i].astype(BF16)
            xp = _ffn(h2, x1, mod_p[5], wg, wu, wd, tm=512, tf=D_FF // 2, per_row_mod=False)
            xs = _ffn(h2s, x1s, mod_s[5], wg, wu, wd, tm=DEC_BATCH, tf=D_FF // 2, per_row_mod=True)
        else:
            rw = jnp.pad(router_w[i], ((0, 0), (0, ROUTER_PAD - N_EXPERTS))).astype(BF16)
            rbias = jnp.pad(router_b[i], (0, ROUTER_PAD - N_EXPERTS)).reshape(1, ROUTER_PAD)
            wg, wu, wd = moe_w_gate[i].astype(BF16), moe_w_up[i].astype(BF16), moe_w_down[i].astype(BF16)
            xp = _moe_routed(h2, x1, mod_p[5], rw, rbias, wg, wu, wd, tm=512)
            xs = _moe(h2s, x1s, mod_s[5], rw, rbias, wg, wu, wd, tm=DEC_BATCH, per_row_mod=True)

    return (xp.reshape(BATCH, SEQ, D_MODEL), xs.reshape(DEC_BATCH, 1, D_MODEL),
            jnp.stack(kp_l), jnp.stack(vp_l), jnp.stack(sp_l),
            jnp.stack(ks_l), jnp.stack(vs_l), jnp.stack(ss_l))
```

```python
import functools

import jax
import jax.numpy as jnp
import numpy as np
from jax import lax
from jax.experimental import pallas as pl
from jax.experimental.pallas import tpu as pltpu

D_MODEL = 1024
BATCH = 4
SEQ = 4096
DEPTH = 2
DEC_BATCH = 128
N_HEADS = 8
N_KV = 2
HEAD_DIM = 64
GROUP = N_HEADS // N_KV
WINDOW = 128
GLA_HEADS = 4
GLA_DK = 64
GLA_DV = 128
GATE_RANK = 16
GATE_TAU = 16.0
D_FF = 2816
N_EXPERTS = 8
D_FF_EXPERT = 1408
EPS = 1e-6

Q_W = N_HEADS * HEAD_DIM
KV_W = N_KV * HEAD_DIM
GK_W = GLA_HEADS * GLA_DK
GV_W = GLA_HEADS * GLA_DV

LANES = 128
GATE_PAD = LANES
ROUTER_PAD = LANES
VMEM_LIMIT = 56 * 1024 * 1024

F32 = jnp.float32
BF16 = jnp.bfloat16

_C_Q = 0
_C_K = _C_Q + Q_W
_C_V = _C_K + KV_W
_C_GQ = _C_V + KV_W
_C_GK = _C_GQ + GK_W
_C_GV = _C_GK + GK_W
_C_RB = _C_GV + GV_W
_C_GA = _C_RB + GV_W
_C_GB = _C_GA + D_MODEL
_C_LR = _C_GB + D_MODEL
PROJ_PAD = _C_LR + GATE_PAD

GLA_CHUNK = 128
GLA_LEVELS = 7


def _params(sem, vmem=VMEM_LIMIT):
    return pltpu.CompilerParams(dimension_semantics=sem, vmem_limit_bytes=vmem)


def _dot(a, b):
    return jnp.dot(a, b, preferred_element_type=F32)


def _dot_nt(a, b):
    return lax.dot_general(a, b, (((1,), (1,)), ((), ())), preferred_element_type=F32)


def _dot_tn(a, b):
    return lax.dot_general(a, b, (((0,), (0,)), ((), ())), preferred_element_type=F32)


def _sigmoid(x):
    return 1.0 / (1.0 + jnp.exp(-x))


def _silu(x):
    return x * _sigmoid(x)


def _ada_kernel(c_ref, w_ref, b_ref, o_ref):
    c = c_ref[...]
    o_ref[...] = _dot(_silu(c).astype(BF16), w_ref[...].astype(BF16)) + b_ref[...]


def _ada(c_all, ada_w, ada_b):
    rows = c_all.shape[0]
    tn = 1024
    return pl.pallas_call(
        _ada_kernel,
        grid=(DEPTH, 6 * D_MODEL // tn),
        in_specs=[
            pl.BlockSpec((rows, D_MODEL), lambda l, j: (0, 0)),
            pl.BlockSpec((None, D_MODEL, tn), lambda l, j: (l, 0, j)),
            pl.BlockSpec((None, 1, tn), lambda l, j: (l, 0, j)),
        ],
        out_specs=pl.BlockSpec((None, rows, tn), lambda l, j: (l, 0, j)),
        out_shape=jax.ShapeDtypeStruct((DEPTH, rows, 6 * D_MODEL), F32),
        compiler_params=_params(("parallel", "parallel")),
        name="ada_mod",
    )(c_all, ada_w, ada_b.reshape(DEPTH, 1, 6 * D_MODEL))


def _mixin_kernel(x_ref, sh_ref, sc_ref, n1_ref, w_ref, bd_ref, qg_ref, kg_ref, wa2_ref, ba_ref,
                  q_ref, k_ref, v_ref, gq_ref, gk_ref, gv_ref, la_ref, rb_ref, sga_ref, sgb_ref):
    x = x_ref[...]
    ms = jnp.mean(x * x, axis=-1, keepdims=True)
    h = x * lax.rsqrt(ms + EPS) * n1_ref[...]
    h = h * (1.0 + sc_ref[...]) + sh_ref[...]
    hb = h.astype(BF16)

    def proj(a, b):
        return _dot(hb, w_ref[:, a:b])

    q = proj(_C_Q, _C_K)
    ssq = _dot((q * q).astype(BF16), bd_ref[...])
    q_ref[...] = (q * lax.rsqrt(ssq * (1.0 / HEAD_DIM) + EPS) * qg_ref[...]).astype(BF16)
    k = proj(_C_K, _C_V)
    ssk = _dot((k * k).astype(BF16), bd_ref[0:KV_W, 0:KV_W])
    k_ref[...] = k * lax.rsqrt(ssk * (1.0 / HEAD_DIM) + EPS) * kg_ref[...]
    v_ref[...] = proj(_C_V, _C_GQ)
    gq_ref[...] = proj(_C_GQ, _C_GK) * (GLA_DK ** -0.5)
    gk_ref[...] = proj(_C_GK, _C_GV)
    gv_ref[...] = proj(_C_GV, _C_RB)
    rb_ref[...] = _silu(proj(_C_RB, _C_GA)).astype(BF16)
    sga_ref[...] = _sigmoid(proj(_C_GA, _C_GB)).astype(BF16)
    sgb_ref[...] = _sigmoid(proj(_C_GB, _C_LR)).astype(BF16)
    ga = proj(_C_LR, PROJ_PAD)
    xg = _dot(ga.astype(BF16), wa2_ref[...]) + ba_ref[...]
    la_ref[...] = (jnp.minimum(xg, 0.0) - jnp.log1p(jnp.exp(-jnp.abs(xg)))) * (1.0 / GATE_TAU)


def _mixin(x, sh, sc, n1, w, bd, qg, kg, wa2, ba, *, tm, per_row_mod):
    n = x.shape[0]
    nt = n // tm
    mod_spec = _mod_spec(tm, n, sh, per_row_mod)

    def row(i):
        return (i, 0)

    def const(shape):
        return pl.BlockSpec(shape, lambda i: (0,) * len(shape))

    def out(width, dtype):
        return pl.BlockSpec((tm, width), row), jax.ShapeDtypeStruct((n, width), dtype)

    outs = [out(Q_W, BF16), out(KV_W, F32), out(KV_W, F32), out(GK_W, F32), out(GK_W, F32),
            out(GV_W, F32), out(GK_W, F32), out(GV_W, BF16), out(D_MODEL, BF16), out(D_MODEL, BF16)]
    return pl.pallas_call(
        _mixin_kernel,
        grid=(nt,),
        in_specs=[
            pl.BlockSpec((tm, D_MODEL), row), mod_spec, mod_spec, const((1, D_MODEL)),
            const((D_MODEL, PROJ_PAD)), const((Q_W, Q_W)), const((1, Q_W)), const((1, KV_W)),
            const((GATE_PAD, GK_W)), const((1, GK_W)),
        ],
        out_specs=[o[0] for o in outs],
        out_shape=[o[1] for o in outs],
        compiler_params=_params(("parallel",)),
        name="mixer_in",
    )(x, sh, sc, n1, w, bd, qg, kg, wa2, ba)


def _head_slope(h):
    return float(2.0 ** (-8.0 * (h + 1) / N_HEADS))


def _swa_prompt_kernel(sink_ref, q_ref, kp_ref, kc_ref, vp_ref, vc_ref, o_ref):
    n = pl.program_id(1)
    blk = WINDOW
    kk = jnp.concatenate([kp_ref[...], kc_ref[...]], axis=0).astype(BF16)
    vv = jnp.concatenate([vp_ref[...], vc_ref[...]], axis=0).astype(BF16)
    row = lax.broadcasted_iota(jnp.int32, (blk, 2 * blk), 0)
    col = lax.broadcasted_iota(jnp.int32, (blk, 2 * blk), 1)
    dist = row + blk - col
    first_key = jnp.where(n > 0, 0, blk)
    valid = (dist >= 0) & (dist <= WINDOW) & (col >= first_key)
    distf = dist.astype(F32)
    klane = lax.broadcasted_iota(jnp.int32, (2 * blk, KV_W), 1)
    kk_kv = [jnp.where((klane // HEAD_DIM) == kv, kk, jnp.zeros_like(kk)) for kv in range(N_KV)]
    olane = lax.broadcasted_iota(jnp.int32, (blk, KV_W), 1)
    q = q_ref[...]
    outs = []
    for g in range(GROUP):
        qp = q[:, g * KV_W:(g + 1) * KV_W]
        pair = []
        for kv in range(N_KV):
            h = kv * GROUP + g
            s = _dot_nt(qp, kk_kv[kv]) - _head_slope(h) * distf
            s = jnp.where(valid, s, -jnp.inf)
            sink = sink_ref[h]
            m = jnp.maximum(jnp.max(s, axis=-1, keepdims=True), sink)
            p = jnp.exp(s - m)
            den = jnp.sum(p, axis=-1, keepdims=True) + jnp.exp(sink - m)
            pair.append(_dot(p.astype(BF16), vv) * (1.0 / den))
        outs.append(jnp.where(olane < HEAD_DIM, pair[0], pair[1]))
    o_ref[...] = jnp.concatenate(outs, axis=-1).astype(BF16)


def _swa_prompt(sinks, q, k, v):
    nb = SEQ // WINDOW

    def cur(b, n):
        return (b * nb + n, 0)

    def prev(b, n):
        return (b * nb + jnp.maximum(n - 1, 0), 0)

    return pl.pallas_call(
        _swa_prompt_kernel,
        grid=(BATCH, nb),
        in_specs=[
            pl.BlockSpec(memory_space=pltpu.SMEM),
            pl.BlockSpec((WINDOW, Q_W), cur),
            pl.BlockSpec((WINDOW, KV_W), prev), pl.BlockSpec((WINDOW, KV_W), cur),
            pl.BlockSpec((WINDOW, KV_W), prev), pl.BlockSpec((WINDOW, KV_W), cur),
        ],
        out_specs=pl.BlockSpec((WINDOW, Q_W), cur),
        out_shape=jax.ShapeDtypeStruct((BATCH * SEQ, Q_W), BF16),
        compiler_params=_params(("parallel", "parallel")),
        name="swa_prompt",
    )(sinks, q, k, k, v, v)


SAMPLE_TB = 8
KEY_PAD = 2 * WINDOW


def _swa_sample_kernel(q_ref, kn_ref, vn_ref, ck_ref, cv_ref, seg_ref, sl_ref, sk_ref,
                       o_ref, ok_ref, ov_ref):
    w = WINDOW
    srow = lax.broadcasted_iota(jnp.int32, (KEY_PAD, LANES), 0)
    distf = (w - srow).astype(F32)
    key_ok = srow <= w
    slope = sl_ref[...]
    sink = sk_ref[...]
    orow = lax.broadcasted_iota(jnp.int32, (LANES, KV_W), 0)
    olane = lax.broadcasted_iota(jnp.int32, (LANES, KV_W), 1)
    own_kv = (olane // HEAD_DIM) == (orow % N_KV)
    pad = jnp.zeros((KEY_PAD - w - 8, KV_W), F32)
    row8 = lax.broadcasted_iota(jnp.int32, (8, KV_W), 0)
    qf = q_ref[...].astype(F32)
    for bi in range(SAMPLE_TB):
        kc = ck_ref[bi]
        vc = cv_ref[bi]
        kn = kn_ref[bi:bi + 1, :]
        vn = vn_ref[bi:bi + 1, :]
        ok_ref[bi, pl.ds(0, w - 1), :] = ck_ref[bi, pl.ds(1, w - 1), :]
        ok_ref[bi, pl.ds(w - 1, 1), :] = kn
        ov_ref[bi, pl.ds(0, w - 1), :] = cv_ref[bi, pl.ds(1, w - 1), :]
        ov_ref[bi, pl.ds(w - 1, 1), :] = vn
        kn8 = jnp.where(row8 == 0, jnp.broadcast_to(kn, (8, KV_W)), 0.0)
        vn8 = jnp.where(row8 == 0, jnp.broadcast_to(vn, (8, KV_W)), 0.0)
        kfull = jnp.concatenate([kc, kn8, pad], axis=0)
        vfull = jnp.concatenate([vc, vn8, pad], axis=0).astype(BF16)
        s = jnp.zeros((KEY_PAD, LANES), F32)
        for g in range(GROUP):
            qrow = qf[bi:bi + 1, g * KV_W:(g + 1) * KV_W]
            s = s + _dot((kfull * qrow).astype(BF16), seg_ref[g])
        s = s - slope * distf
        s = jnp.where(key_ok, s, -jnp.inf)
        m = jnp.maximum(jnp.max(s, axis=0, keepdims=True), sink)
        p = jnp.exp(s - m)
        den = jnp.sum(p, axis=0, keepdims=True) + jnp.exp(sink - m)
        p = p * (1.0 / den)
        r = _dot_tn(p.astype(BF16), vfull)
        r = jnp.where(own_kv, r, 0.0)
        o_ref[bi] = r[0:N_HEADS, :]


def _swa_sample(q, kn, vn, ck, cv, seg, slopes, sinks):
    tb = SAMPLE_TB
    nb = DEC_BATCH // tb
    w = WINDOW

    def const(shape):
        return pl.BlockSpec(shape, lambda i: (0,) * len(shape))

    return pl.pallas_call(
        _swa_sample_kernel,
        grid=(nb,),
        in_specs=[
            pl.BlockSpec((tb, Q_W), lambda i: (i, 0)),
            pl.BlockSpec((tb, KV_W), lambda i: (i, 0)),
            pl.BlockSpec((tb, KV_W), lambda i: (i, 0)),
            pl.BlockSpec((tb, w, KV_W), lambda i: (i, 0, 0)),
            pl.BlockSpec((tb, w, KV_W), lambda i: (i, 0, 0)),
            const((GROUP, KV_W, LANES)), const((1, LANES)), const((1, LANES)),
        ],
        out_specs=[
            pl.BlockSpec((tb, N_HEADS, KV_W), lambda i: (i, 0, 0)),
            pl.BlockSpec((tb, w, KV_W), lambda i: (i, 0, 0)),
            pl.BlockSpec((tb, w, KV_W), lambda i: (i, 0, 0)),
        ],
        out_shape=[
            jax.ShapeDtypeStruct((DEC_BATCH, N_HEADS, KV_W), F32),
            jax.ShapeDtypeStruct((DEC_BATCH, w, KV_W), F32),
            jax.ShapeDtypeStruct((DEC_BATCH, w, KV_W), F32),
        ],
        compiler_params=_params(("parallel",)),
        name="swa_sample",
    )(q, kn, vn, ck, cv, seg, slopes, sinks)


def _gla_cum_matrix():
    c = GLA_CHUNK
    tri = np.tril(np.ones((c, c), np.float32))
    i = np.arange(c)
    blocks = []
    for lvl in range(GLA_LEVELS):
        half = 1 << lvl
        mid = (i // (2 * half)) * (2 * half) + half - 1
        blocks.append(tri - tri[mid])
    blocks.append(tri)
    return np.concatenate(blocks, axis=0)


def _split3(x):
    hi = x.astype(BF16)
    r1 = x - hi.astype(F32)
    mid = r1.astype(BF16)
    lo = (r1 - mid.astype(F32)).astype(BF16)
    return hi, mid, lo


def _gla_prompt_kernel(q_ref, k_ref, la_ref, v_ref, rb_ref, gn_ref, ut_ref, o_ref, s_ref, st_ref):
    c = pl.program_id(1)
    cl = GLA_CHUNK

    @pl.when(c == 0)
    def _():
        st_ref[0] = jnp.zeros(st_ref.shape[1:], F32)

    ut = ut_ref[...]
    hi, mid, lo = _split3(la_ref[...])
    tall = _dot(ut, hi) + _dot(ut, mid) + _dot(ut, lo)
    q = q_ref[...]
    k = k_ref[...]
    lane = lax.broadcasted_iota(jnp.int32, (cl, GK_W), 1)
    row = lax.broadcasted_iota(jnp.int32, (cl, GK_W), 0)
    head_of_lane = lane // GLA_DK
    ri = lax.broadcasted_iota(jnp.int32, (cl, cl), 0)
    ci = lax.broadcasted_iota(jnp.int32, (cl, cl), 1)
    attn = [jnp.zeros((cl, cl), F32) for _ in range(GLA_HEADS)]
    for lvl in range(GLA_LEVELS + 1):
        if lvl < GLA_LEVELS:
            e = jnp.exp(-jnp.abs(tall[lvl * cl:(lvl + 1) * cl]))
            upper = ((row >> lvl) & 1) == 1
            qt = jnp.where(upper, q * e, 0.0).astype(BF16)
            kt = jnp.where(upper, 0.0, k * e).astype(BF16)
            pair_ok = (ri >> (lvl + 1)) == (ci >> (lvl + 1))
        else:
            qt = q.astype(BF16)
            kt = k.astype(BF16)
            pair_ok = ri == ci
        for h in range(GLA_HEADS):
            kth = jnp.where(head_of_lane == h, kt, jnp.zeros_like(kt))
            attn[h] = attn[h] + jnp.where(pair_ok, _dot_nt(qt, kth), 0.0)

    cum = tall[GLA_LEVELS * cl:(GLA_LEVELS + 1) * cl]
    last = cum[cl - 1:cl, :]
    qe = (q * jnp.exp(cum)).astype(BF16)
    kd = (k * jnp.exp(last - cum)).astype(BF16)
    ones = jnp.ones((cl, LANES), BF16)
    dec = jnp.exp(_dot_tn(hi, ones) + _dot_tn(mid, ones) + _dot_tn(lo, ones))
    v = v_ref[...]
    rb = rb_ref[...]
    outs = []
    for h in range(GLA_HEADS):
        vh = v[:, h * GLA_DV:(h + 1) * GLA_DV].astype(BF16)
        st = st_ref[c % 2, h]
        o = _dot(attn[h].astype(BF16), vh) + _dot(qe, st.astype(BF16))
        kdh = jnp.where(head_of_lane == h, kd, jnp.zeros_like(kd))
        st_ref[(c + 1) % 2, h] = st * dec + _dot_tn(kdh, vh)
        ms = jnp.mean(o * o, axis=-1, keepdims=True)
        g = o * lax.rsqrt(ms + EPS) * gn_ref[...]
        outs.append(g * rb[:, h * GLA_DV:(h + 1) * GLA_DV].astype(F32))
    o_ref[...] = jnp.concatenate(outs, axis=-1).astype(BF16)

    @pl.when(c == pl.num_programs(1) - 1)
    def _():
        fin = (SEQ // GLA_CHUNK) % 2
        s_ref[...] = st_ref[fin, 0] + st_ref[fin, 1] + st_ref[fin, 2] + st_ref[fin, 3]


def _gla_prompt(gq, gk, la, gv, rbs, gn, ut):
    cl = GLA_CHUNK
    nc = SEQ // cl

    def row(b, c):
        return (b * nc + c, 0)

    return pl.pallas_call(
        _gla_prompt_kernel,
        grid=(BATCH, nc),
        in_specs=[
            pl.BlockSpec((cl, GK_W), row), pl.BlockSpec((cl, GK_W), row), pl.BlockSpec((cl, GK_W), row),
            pl.BlockSpec((cl, GV_W), row), pl.BlockSpec((cl, GV_W), row),
            pl.BlockSpec((1, GLA_DV), lambda b, c: (0, 0)),
            pl.BlockSpec(((GLA_LEVELS + 1) * cl, cl), lambda b, c: (0, 0)),
        ],
        out_specs=[
            pl.BlockSpec((cl, GV_W), row),
            pl.BlockSpec((None, GK_W, GLA_DV), lambda b, c: (b, 0, 0)),
        ],
        out_shape=[
            jax.ShapeDtypeStruct((BATCH * SEQ, GV_W), BF16),
            jax.ShapeDtypeStruct((BATCH, GK_W, GLA_DV), F32),
        ],
        scratch_shapes=[pltpu.VMEM((2, GLA_HEADS, GK_W, GLA_DV), F32)],
        compiler_params=_params(("parallel", "arbitrary")),
        name="gla_prompt",
    )(gq, gk, la, gv, rbs, gn, ut)


def _gla_sample_kernel(q_ref, k_ref, la_ref, v_ref, rb_ref, gn_ref, s_ref, o_ref, so_ref):
    tb = SAMPLE_TB
    dec = jnp.exp(la_ref[...])
    pieces = []
    for x in (dec, k_ref[...], q_ref[...]):
        hi, mid, lo = _split3(x)
        stacked = jnp.concatenate(
            [hi.astype(F32), mid.astype(F32), lo.astype(F32), jnp.zeros_like(x)], axis=0)
        pieces.append(stacked.astype(BF16))
    prow = lax.broadcasted_iota(jnp.int32, (4 * tb, LANES), 0)
    rb = rb_ref[...].astype(F32)
    v = v_ref[...]
    for bi in range(tb):
        sel = jnp.where((prow % tb) == bi, 1.0, 0.0).astype(BF16)
        a_col, k_col, q_col = [_dot_tn(p, sel) for p in pieces]
        for h in range(GLA_HEADS):
            rs = slice(h * GLA_DK, (h + 1) * GLA_DK)
            vs = slice(h * GLA_DV, (h + 1) * GLA_DV)
            s_new = a_col[rs] * s_ref[bi, rs, :] + k_col[rs] * v[bi:bi + 1, vs]
            so_ref[bi, rs, :] = s_new
            o = jnp.sum(q_col[rs] * s_new, axis=0, keepdims=True)
            ms = jnp.mean(o * o, axis=-1, keepdims=True)
            g = o * lax.rsqrt(ms + EPS) * gn_ref[...]
            o_ref[bi:bi + 1, vs] = g * rb[bi:bi + 1, vs]


def _gla_sample(gq, gk, la, gv, rbs, gn, state):
    tb = SAMPLE_TB
    nb = DEC_BATCH // tb

    def row(w):
        return pl.BlockSpec((tb, w), lambda i: (i, 0))

    st_spec = pl.BlockSpec((tb, GK_W, GLA_DV), lambda i: (i, 0, 0))
    return pl.pallas_call(
        _gla_sample_kernel,
        grid=(nb,),
        in_specs=[row(GK_W), row(GK_W), row(GK_W), row(GV_W), row(GV_W),
                  pl.BlockSpec((1, GLA_DV), lambda i: (0, 0)), st_spec],
        out_specs=[row(GV_W), st_spec],
        out_shape=[
            jax.ShapeDtypeStruct((DEC_BATCH, GV_W), F32),
            jax.ShapeDtypeStruct((DEC_BATCH, GK_W, GLA_DV), F32),
        ],
        compiler_params=_params(("parallel",)),
        name="gla_sample",
    )(gq, gk, la, gv, rbs, gn, state)


def _merge_kernel(x_ref, a_ref, g_ref, sga_ref, sgb_ref, wpa_ref, wpb_ref, wo_ref,
                  g1_ref, sh_ref, sc_ref, n2_ref, *rest):
    with_router = len(rest) == 7
    x1_ref, h2_ref = rest[3:5] if with_router else rest
    ya = _dot(a_ref[...].astype(BF16), wpa_ref[...])
    yb = _dot(g_ref[...].astype(BF16), wpb_ref[...])
    merged = sga_ref[...].astype(F32) * ya + sgb_ref[...].astype(F32) * yb
    mix = _dot(merged.astype(BF16), wo_ref[...])
    x1 = x_ref[...] + g1_ref[...] * mix
    x1_ref[...] = x1
    ms = jnp.mean(x1 * x1, axis=-1, keepdims=True)
    h = x1 * lax.rsqrt(ms + EPS) * n2_ref[...]
    h2 = h * (1.0 + sc_ref[...]) + sh_ref[...]
    h2_ref[...] = h2.astype(h2_ref.dtype)
    if with_router:
        rw_ref, rbias_ref, tri_ref, route_ref, cnt_ref = rest[0], rest[1], rest[2], rest[5], rest[6]
        logits = _dot(h2.astype(BF16), rw_ref[...]) + rbias_ref[...]
        packed, counts = _route_pack(logits, tri_ref[...])
        route_ref[...] = packed
        cnt_ref[...] = jnp.broadcast_to(counts, cnt_ref.shape)


def _mod_spec(tm, n, mod, per_row_mod):
    if per_row_mod:
        return pl.BlockSpec((tm, D_MODEL), lambda i, *_: (i, 0))
    tiles_per_mod = (n // mod.shape[0]) // tm
    return pl.BlockSpec((None, 1, D_MODEL), lambda i, *_: (i // tiles_per_mod, 0, 0))


def _merge(x, a, g, sga, sgb, wpa, wpb, wo, g1, sh2, sc2, n2, *, tm, per_row_mod, h2_dtype=BF16,
           router=None):
    n = x.shape[0]
    mod_spec = _mod_spec(tm, n, g1, per_row_mod)

    def row(w):
        return pl.BlockSpec((tm, w), lambda i: (i, 0))

    def const(shape):
        return pl.BlockSpec(shape, lambda i: (0,) * len(shape))

    in_specs = [row(D_MODEL), row(a.shape[1]), row(GV_W), row(D_MODEL), row(D_MODEL),
                const(wpa.shape), const(wpb.shape), const(wo.shape),
                mod_spec, mod_spec, mod_spec, const((1, D_MODEL))]
    out_specs = [row(D_MODEL), row(D_MODEL)]
    out_shape = [jax.ShapeDtypeStruct((n, D_MODEL), F32), jax.ShapeDtypeStruct((n, D_MODEL), h2_dtype)]
    args = [x, a, g, sga, sgb, wpa, wpb, wo, g1, sh2, sc2, n2]
    if router is not None:
        tri = jnp.asarray(np.tril(np.ones((tm, tm), np.float32), -1), BF16)
        in_specs += [const(router[0].shape), const(router[1].shape), const((tm, tm))]
        out_specs += [row(ROUTER_PAD), pl.BlockSpec((None, 8, ROUTER_PAD), lambda i: (i, 0, 0))]
        out_shape += [jax.ShapeDtypeStruct((n, ROUTER_PAD), F32),
                      jax.ShapeDtypeStruct((n // tm, 8, ROUTER_PAD), F32)]
        args += [router[0], router[1], tri]
    return pl.pallas_call(
        _merge_kernel,
        grid=(n // tm,),
        in_specs=in_specs,
        out_specs=out_specs,
        out_shape=out_shape,
        compiler_params=_params(("parallel",)),
        name="merge_out",
    )(*args)


def _ffn_kernel(h_ref, x_ref, g2_ref, wg_ref, wu_ref, wd_ref, o_ref, acc_ref):
    f = pl.program_id(1)
    hb = h_ref[...]
    act = (_silu(_dot(hb, wg_ref[...])) * _dot(hb, wu_ref[...])).astype(BF16)
    y = _dot(act, wd_ref[...])

    @pl.when(f == 0)
    def _():
        acc_ref[...] = y

    @pl.when(f > 0)
    def _():
        acc_ref[...] += y

    @pl.when(f == pl.num_programs(1) - 1)
    def _():
        o_ref[...] = x_ref[...] + g2_ref[...] * acc_ref[...]


def _ffn(h2, x1, g2, wg, wu, wd, *, tm, tf, per_row_mod):
    n = h2.shape[0]
    mod_spec = _mod_spec(tm, n, g2, per_row_mod)
    return pl.pallas_call(
        _ffn_kernel,
        grid=(n // tm, D_FF // tf),
        in_specs=[
            pl.BlockSpec((tm, D_MODEL), lambda i, f: (i, 0)),
            pl.BlockSpec((tm, D_MODEL), lambda i, f: (i, 0)),
            mod_spec,
            pl.BlockSpec((D_MODEL, tf), lambda i, f: (0, f)),
            pl.BlockSpec((D_MODEL, tf), lambda i, f: (0, f)),
            pl.BlockSpec((tf, D_MODEL), lambda i, f: (f, 0)),
        ],
        out_specs=pl.BlockSpec((tm, D_MODEL), lambda i, f: (i, 0)),
        out_shape=jax.ShapeDtypeStruct((n, D_MODEL), F32),
        scratch_shapes=[pltpu.VMEM((tm, D_MODEL), F32)],
        compiler_params=_params(("parallel", "arbitrary")),
        name="ffn_dense",
    )(h2, x1, g2, wg, wu, wd)


def _moe_kernel(h_ref, x_ref, g2_ref, rw_ref, rbias_ref, wg_ref, wu_ref, wd_ref, o_ref, acc_ref, gate_ref):
    e = pl.program_id(1)
    hb = h_ref[...]
    tm = hb.shape[0]
    lane = lax.broadcasted_iota(jnp.int32, (tm, ROUTER_PAD), 1).astype(F32)

    @pl.when(e == 0)
    def _():
        logits = _dot(hb, rw_ref[...]) + rbias_ref[...]
        i1, i2, p1, p2 = _top2(logits, lane)
        gate_ref[...] = jnp.where(lane == i1, p1, 0.0) + jnp.where(lane == i2, p2, 0.0)
        acc_ref[...] = jnp.zeros_like(acc_ref)

    ge = jnp.sum(jnp.where(lane == e.astype(F32), gate_ref[...], 0.0), axis=-1, keepdims=True)
    act = (_silu(_dot(hb, wg_ref[...])) * _dot(hb, wu_ref[...])).astype(BF16)
    acc_ref[...] += ge * _dot(act, wd_ref[...])

    @pl.when(e == pl.num_programs(1) - 1)
    def _():
        o_ref[...] = x_ref[...] + g2_ref[...] * acc_ref[...]


def _moe(h2, x1, g2, rw, rbias, wg, wu, wd, *, tm, per_row_mod):
    n = h2.shape[0]
    mod_spec = _mod_spec(tm, n, g2, per_row_mod)
    fe = D_FF_EXPERT
    return pl.pallas_call(
        _moe_kernel,
        grid=(n // tm, N_EXPERTS),
        in_specs=[
            pl.BlockSpec((tm, D_MODEL), lambda i, e: (i, 0)),
            pl.BlockSpec((tm, D_MODEL), lambda i, e: (i, 0)),
            mod_spec,
            pl.BlockSpec((D_MODEL, ROUTER_PAD), lambda i, e: (0, 0)),
            pl.BlockSpec((1, ROUTER_PAD), lambda i, e: (0, 0)),
            pl.BlockSpec((None, D_MODEL, fe), lambda i, e: (e, 0, 0)),
            pl.BlockSpec((None, D_MODEL, fe), lambda i, e: (e, 0, 0)),
            pl.BlockSpec((None, fe, D_MODEL), lambda i, e: (e, 0, 0)),
        ],
        out_specs=pl.BlockSpec((tm, D_MODEL), lambda i, e: (i, 0)),
        out_shape=jax.ShapeDtypeStruct((n, D_MODEL), F32),
        scratch_shapes=[pltpu.VMEM((tm, D_MODEL), F32), pltpu.VMEM((tm, ROUTER_PAD), F32)],
        compiler_params=_params(("parallel", "arbitrary")),
        name="moe",
    )(h2, x1, g2, rw, rbias, wg, wu, wd)


def _top2(logits, lane):
    lg = jnp.where(lane < N_EXPERTS, logits, -jnp.inf)
    m1 = jnp.max(lg, axis=-1, keepdims=True)
    i1 = jnp.min(jnp.where(lg == m1, lane, float(ROUTER_PAD)), axis=-1, keepdims=True)
    lg2 = jnp.where(lane == i1, -jnp.inf, lg)
    m2 = jnp.max(lg2, axis=-1, keepdims=True)
    i2 = jnp.min(jnp.where(lg2 == m2, lane, float(ROUTER_PAD)), axis=-1, keepdims=True)
    e2 = jnp.exp(m2 - m1)
    p1 = 1.0 / (1.0 + e2)
    return i1, i2, p1, e2 * p1


def _route_pack(logits, tri):
    lane = lax.broadcasted_iota(jnp.int32, logits.shape, 1).astype(F32)
    i1, i2, p1, p2 = _top2(logits, lane)
    oh1 = jnp.where(lane == i1, 1.0, 0.0)
    oh2 = jnp.where(lane == i2, 1.0, 0.0)
    cnt1 = jnp.sum(oh1, axis=0, keepdims=True)
    cnt2 = jnp.sum(oh2, axis=0, keepdims=True)
    rank1 = jnp.sum(_dot(tri, oh1.astype(BF16)) * oh1, axis=-1, keepdims=True)
    rank2 = jnp.sum((_dot(tri, oh2.astype(BF16)) + cnt1) * oh2, axis=-1, keepdims=True)
    packed = jnp.zeros_like(logits)
    for k, val in enumerate((p1, p2, i1, i2, rank1, rank2)):
        packed = jnp.where(lane == float(k), val, packed)
    return packed, cnt1 + cnt2


def _route_tables(route, tile_counts, tm, n_tiles):
    n = route.shape[0]
    e = route[:, 2:4].astype(jnp.int32)
    rank = route[:, 4:6].astype(jnp.int32)
    tile_cnt = tile_counts[:, 0, :N_EXPERTS].astype(jnp.int32)
    cnt = jnp.sum(tile_cnt, axis=0)
    gsz = ((cnt + tm - 1) // tm) * tm
    gend = jnp.cumsum(gsz)
    seg_start = (gend - gsz)[None, :] + jnp.cumsum(tile_cnt, axis=0) - tile_cnt
    token_tile = jnp.arange(n, dtype=jnp.int32) // (n // tile_cnt.shape[0])
    pos = seg_start.reshape(-1)[token_tile[:, None] * N_EXPERTS + e] + rank
    pos = pos.T.reshape(2 * n)
    tile_start = jnp.arange(n_tiles, dtype=jnp.int32) * tm
    tile_expert = jnp.sum((tile_start[:, None] >= gend[None, :]).astype(jnp.int32), axis=1)
    tile_expert = jnp.minimum(tile_expert, N_EXPERTS - 1)
    live = (gend[-1] // tm).reshape(1)
    return pos, tile_expert, live, gend - gsz + cnt, gend


def _row_copy_wait(src_hbm, dst, sem, rows):
    pltpu.make_async_copy(src_hbm.at[pl.ds(0, rows)], dst, sem).wait()


def _moe_dispatch_kernel(pos_ref, pad_lo_ref, pad_hi_ref, live_ref, h_ref, xs_hbm, zbuf, sem, zsem):
    i = pl.program_id(0)
    tt = h_ref.shape[0]
    n = tt * pl.num_programs(0)
    tm = zbuf.shape[0]

    base = i * tt
    for r in range(tt):
        row = h_ref.at[pl.ds(r, 1)]
        pltpu.make_async_copy(row, xs_hbm.at[pl.ds(pos_ref[base + r], 1)], sem).start()
        pltpu.make_async_copy(row, xs_hbm.at[pl.ds(pos_ref[n + base + r], 1)], sem).start()
    for _ in range(2):
        pltpu.make_async_copy(h_ref, xs_hbm.at[pl.ds(0, tt)], sem).wait()

    @pl.when(i == pl.num_programs(0) - 1)
    def _():
        zbuf[...] = jnp.zeros(zbuf.shape, F32)

        def fill_row(p):
            return pltpu.make_async_copy(zbuf.at[pl.ds(0, 1)], xs_hbm.at[pl.ds(p, 1)], zsem)

        def fill_tile(t):
            return pltpu.make_async_copy(zbuf, xs_hbm.at[pl.ds(pl.multiple_of(t * tm, tm), tm)], zsem)

        for e in range(N_EXPERTS):
            lo, hi = pad_lo_ref[e], pad_hi_ref[e]
            lax.fori_loop(lo, hi, lambda p, c: (fill_row(p).start(), c)[1], 0)
            lax.fori_loop(lo, hi, lambda p, c: (fill_row(p).wait(), c)[1], 0)
        lo, hi = live_ref[0], xs_hbm.shape[0] // tm
        lax.fori_loop(lo, hi, lambda t, c: (fill_tile(t).start(), c)[1], 0)
        lax.fori_loop(lo, hi, lambda t, c: (fill_tile(t).wait(), c)[1], 0)


def _moe_dispatch(h2, pos, pad_lo, pad_hi, live, *, tt, tm, n_tiles):
    n = h2.shape[0]
    grid_spec = pltpu.PrefetchScalarGridSpec(
        num_scalar_prefetch=4,
        grid=(n // tt,),
        in_specs=[pl.BlockSpec((tt, D_MODEL), lambda i, *_: (i, 0))],
        out_specs=pl.BlockSpec(memory_space=pl.ANY),
        scratch_shapes=[pltpu.VMEM((tm, D_MODEL), F32), pltpu.SemaphoreType.DMA(()),
                        pltpu.SemaphoreType.DMA(())],
    )
    return pl.pallas_call(
        _moe_dispatch_kernel,
        grid_spec=grid_spec,
        out_shape=jax.ShapeDtypeStruct((n_tiles * tm, D_MODEL), F32),
        compiler_params=_params(("arbitrary",)),
        name="moe_dispatch",
    )(pos, pad_lo, pad_hi, live, h2)


def _moe_expert_kernel(te_ref, live_ref, x_ref, wg_ref, wu_ref, wd_ref, y_ref):
    del te_ref
    is_live = pl.program_id(0) < live_ref[0]

    @pl.when(is_live)
    def _():
        hb = x_ref[...].astype(BF16)
        act = (_silu(_dot(hb, wg_ref[...])) * _dot(hb, wu_ref[...])).astype(BF16)
        y_ref[...] = _dot(act, wd_ref[...])

    @pl.when(jnp.logical_not(is_live))
    def _():
        y_ref[...] = jnp.zeros(y_ref.shape, F32)


def _moe_experts(xs, tile_expert, live, wg, wu, wd, *, tm, n_tiles):
    fe = D_FF_EXPERT

    def in_tile(t, te, live):
        return (jnp.minimum(t, live[0] - 1), 0)

    def out_tile(t, te, live):
        return (t, 0)

    def expert(t, te, live):
        return (te[t], 0, 0)

    grid_spec = pltpu.PrefetchScalarGridSpec(
        num_scalar_prefetch=2,
        grid=(n_tiles,),
        in_specs=[
            pl.BlockSpec((tm, D_MODEL), in_tile),
            pl.BlockSpec((None, D_MODEL, fe), expert),
            pl.BlockSpec((None, D_MODEL, fe), expert),
            pl.BlockSpec((None, fe, D_MODEL), expert),
        ],
        out_specs=pl.BlockSpec((tm, D_MODEL), out_tile),
    )
    return pl.pallas_call(
        _moe_expert_kernel,
        grid_spec=grid_spec,
        out_shape=jax.ShapeDtypeStruct(xs.shape, F32),
        compiler_params=_params(("arbitrary",)),
        name="moe_experts",
    )(tile_expert, live, xs, wg, wu, wd)


def _moe_combine_kernel(pos_ref, x_ref, g2_ref, r_ref, ys_hbm, o_ref, buf, sem):
    i = pl.program_id(0)
    nt = pl.num_programs(0)
    tt = x_ref.shape[0]
    n = nt * tt

    def fetch(tile, sl):
        base = tile * tt
        for r in range(tt):
            for s in range(2):
                row = pos_ref[s * n + base + r]
                pltpu.make_async_copy(ys_hbm.at[pl.ds(row, 1)], buf.at[sl, pl.ds(s * tt + r, 1)],
                                      sem.at[sl]).start()

    @pl.when(i == 0)
    def _():
        fetch(0, 0)

    @pl.when(i + 1 < nt)
    def _():
        fetch(i + 1, (i + 1) % 2)

    sl = i % 2
    _row_copy_wait(ys_hbm, buf.at[sl], sem.at[sl], 2 * tt)
    r = r_ref[...]
    f = r[:, 0:1] * buf[sl, pl.ds(0, tt), :] + r[:, 1:2] * buf[sl, pl.ds(tt, tt), :]
    o_ref[...] = x_ref[...] + g2_ref[...] * f


def _moe_combine(x1, g2, route, pos, ys, *, tt):
    n = x1.shape[0]
    tiles_per_mod = (n // g2.shape[0]) // tt
    grid_spec = pltpu.PrefetchScalarGridSpec(
        num_scalar_prefetch=1,
        grid=(n // tt,),
        in_specs=[
            pl.BlockSpec((tt, D_MODEL), lambda i, pos: (i, 0)),
            pl.BlockSpec((None, 1, D_MODEL), lambda i, pos: (i // tiles_per_mod, 0, 0)),
            pl.BlockSpec((tt, ROUTER_PAD), lambda i, pos: (i, 0)),
            pl.BlockSpec(memory_space=pl.ANY),
        ],
        out_specs=pl.BlockSpec((tt, D_MODEL), lambda i, pos: (i, 0)),
        scratch_shapes=[pltpu.VMEM((2, 2 * tt, D_MODEL), F32), pltpu.SemaphoreType.DMA((2,))],
    )
    return pl.pallas_call(
        _moe_combine_kernel,
        grid_spec=grid_spec,
        out_shape=jax.ShapeDtypeStruct((n, D_MODEL), F32),
        compiler_params=_params(("arbitrary",)),
        name="moe_combine",
    )(pos, x1, g2, route, ys)


def _moe_routed(h2, x1, g2, route, tile_counts, wg, wu, wd, *, tm, tt):
    n = h2.shape[0]
    n_tiles = (2 * n) // tm + N_EXPERTS
    pos, tile_expert, live, pad_lo, pad_hi = _route_tables(route, tile_counts, tm, n_tiles)
    xs = _moe_dispatch(h2, pos, pad_lo, pad_hi, live, tt=tt, tm=tm, n_tiles=n_tiles)
    ys = _moe_experts(xs, tile_expert, live, wg, wu, wd, tm=tm, n_tiles=n_tiles)
    return _moe_combine(x1, g2, route, pos, ys, tt=tt)


def _head_perm():
    idx = []
    for g in range(GROUP):
        for kv in range(N_KV):
            h = kv * GROUP + g
            idx.extend(range(h * HEAD_DIM, (h + 1) * HEAD_DIM))
    return np.asarray(idx, np.int32)


def _relayout_w_in(w):
    pts = np.cumsum([0, Q_W, KV_W, KV_W, GK_W, GK_W, GV_W, GATE_RANK, GV_W, D_MODEL, D_MODEL])
    qa, ka, va, qb, kb, vb, ga, rb, gta, gtb = [w[:, pts[i]:pts[i + 1]] for i in range(10)]
    qa = qa[:, _head_perm()]
    ga = jnp.pad(ga, ((0, 0), (0, GATE_PAD - GATE_RANK)))
    return jnp.concatenate([qa, ka, va, qb, kb, vb, rb, gta, gtb, ga], axis=1).astype(BF16)


def _expand_wpa(wpa_perm):
    src = wpa_perm.reshape(GROUP, N_KV, 1, HEAD_DIM, D_MODEL)
    own = jnp.asarray(np.eye(N_KV, dtype=bool).reshape(1, N_KV, N_KV, 1, 1))
    return jnp.where(own, src, jnp.zeros_like(src)).reshape(N_HEADS * KV_W, D_MODEL)


def _sample_lane_consts(sinks):
    j = np.arange(N_HEADS)
    head = (j % N_KV) * GROUP + j // N_KV
    slopes = np.zeros((1, LANES), np.float32)
    slopes[0, :N_HEADS] = 2.0 ** (-8.0 * (head + 1) / N_HEADS)
    sink_row = jnp.zeros((1, LANES), F32).at[0, :N_HEADS].set(sinks[head])
    seg = np.zeros((GROUP, KV_W, LANES), np.float32)
    for g in range(GROUP):
        for kv in range(N_KV):
            seg[g, kv * HEAD_DIM:(kv + 1) * HEAD_DIM, g * N_KV + kv] = 1.0
    return jnp.asarray(seg, BF16), jnp.asarray(slopes), sink_row


def kernel(x_prompt, x_sample, cache_k, cache_v, state_gla, c_prompt, c_sample, ada_w, ada_b, norm1_g, norm2_g, w_in, q_norm_g, k_norm_g, attn_sinks, gla_wa2, gla_ba, gla_norm_g, w_branch_a, w_branch_b, w_out, ffn_w_gate, ffn_w_up, ffn_w_down, router_w, router_b, moe_w_gate, moe_w_up, moe_w_down):
    n_p = BATCH * SEQ
    xp = x_prompt.reshape(n_p, D_MODEL)
    xs = x_sample.reshape(DEC_BATCH, D_MODEL)

    c_rows = BATCH + DEC_BATCH
    c_pad = -c_rows % 8
    c_all = jnp.pad(jnp.concatenate([c_prompt, c_sample], axis=0), ((0, c_pad), (0, 0)))
    mod = _ada(c_all, ada_w, ada_b)

    bd = jnp.asarray(np.kron(np.eye(N_HEADS), np.ones((HEAD_DIM, HEAD_DIM))), BF16)
    ut = jnp.asarray(_gla_cum_matrix(), BF16)
    perm = _head_perm()

    kp_l, vp_l, sp_l, ks_l, vs_l, ss_l = [], [], [], [], [], []
    for l in range(DEPTH):
        m = mod[l].reshape(c_rows + c_pad, 6, D_MODEL)
        mod_p = [m[:BATCH, i].reshape(BATCH, 1, D_MODEL) for i in range(6)]
        mod_s = [m[BATCH:c_rows, i] for i in range(6)]

        w = _relayout_w_in(w_in[l])
        qg = (jnp.tile(q_norm_g[l], N_HEADS) * (HEAD_DIM ** -0.5)).reshape(1, Q_W)
        kg = jnp.tile(k_norm_g[l], N_KV).reshape(1, KV_W)
        wa2 = jnp.pad(gla_wa2[l], ((0, GATE_PAD - GATE_RANK), (0, 0))).astype(BF16)
        ba = gla_ba[l].reshape(1, GK_W)
        n1 = norm1_g[l].reshape(1, D_MODEL)
        n2 = norm2_g[l].reshape(1, D_MODEL)
        gn = gla_norm_g[l].reshape(1, GLA_DV)
        wpa = w_branch_a[l][perm].astype(BF16)
        wpa_x = _expand_wpa(wpa)
        wpb = w_branch_b[l].astype(BF16)
        wo = w_out[l].astype(BF16)
        sinks_perm = attn_sinks[l]
        seg, slope_row, sink_row = _sample_lane_consts(attn_sinks[l])

        q, k, v, gq, gk, gv, la, rbs, sga, sgb = _mixin(
            xp, mod_p[0], mod_p[1], n1, w, bd, qg, kg, wa2, ba, tm=512, per_row_mod=False)
        a_out = _swa_prompt(sinks_perm, q, k, v)
        g_out, s_fin = _gla_prompt(gq, gk, la, gv, rbs, gn, ut)
        is_moe = l % 2 == 1
        if is_moe:
            rw = jnp.pad(router_w[l // 2], ((0, 0), (0, ROUTER_PAD - N_EXPERTS))).astype(BF16)
            rbias = jnp.pad(router_b[l // 2], (0, ROUTER_PAD - N_EXPERTS)).reshape(1, ROUTER_PAD)
        merged = _merge(xp, a_out, g_out, sga, sgb, wpa, wpb, wo, mod_p[2], mod_p[3], mod_p[4], n2,
                        tm=512, per_row_mod=False, h2_dtype=F32 if is_moe else BF16,
                        router=(rw, rbias) if is_moe else None)
        x1, h2 = merged[0], merged[1]
        kp_l.append(k.reshape(BATCH, SEQ, N_KV, HEAD_DIM)[:, SEQ - WINDOW:])
        vp_l.append(v.reshape(BATCH, SEQ, N_KV, HEAD_DIM)[:, SEQ - WINDOW:])
        sp_l.append(s_fin.reshape(BATCH, GLA_HEADS, GLA_DK, GLA_DV))

        qs, ksn, vsn, gqs, gks, gvs, las, rbss, sgas, sgbs = _mixin(
            xs, mod_s[0], mod_s[1], n1, w, bd, qg, kg, wa2, ba, tm=DEC_BATCH, per_row_mod=True)
        a_s, nk, nv = _swa_sample(qs, ksn, vsn, cache_k[l].reshape(DEC_BATCH, WINDOW, KV_W),
                                  cache_v[l].reshape(DEC_BATCH, WINDOW, KV_W), seg, slope_row, sink_row)
        g_s, s_new = _gla_sample(gqs, gks, las, gvs, rbss, gn, state_gla[l].reshape(DEC_BATCH, GK_W, GLA_DV))
        x1s, h2s = _merge(xs, a_s.reshape(DEC_BATCH, N_HEADS * KV_W), g_s, sgas, sgbs, wpa_x, wpb, wo,
                          mod_s[2], mod_s[3], mod_s[4], n2, tm=DEC_BATCH, per_row_mod=True)
        ks_l.append(nk.reshape(DEC_BATCH, WINDOW, N_KV, HEAD_DIM))
        vs_l.append(nv.reshape(DEC_BATCH, WINDOW, N_KV, HEAD_DIM))
        ss_l.append(s_new.reshape(DEC_BATCH, GLA_HEADS, GLA_DK, GLA_DV))

        i = l // 2
        if not is_moe:
            wg, wu, wd = ffn_w_gate[i].astype(BF16), ffn_w_up[i].astype(BF16), ffn_w_down[i].astype(BF16)
            xp = _ffn(h2, x1, mod_p[5], wg, wu, wd, tm=512, tf=D_FF // 2, per_row_mod=False)
            xs = _ffn(h2s, x1s, mod_s[5], wg, wu, wd, tm=DEC_BATCH, tf=D_FF // 2, per_row_mod=True)
        else:
            wg, wu, wd = moe_w_gate[i].astype(BF16), moe_w_up[i].astype(BF16), moe_w_down[i].astype(BF16)
            xp = _moe_routed(h2, x1, mod_p[5], merged[2], merged[3], wg, wu, wd, tm=512, tt=512)
            xs = _moe(h2s, x1s, mod_s[5], rw, rbias, wg, wu, wd, tm=DEC_BATCH, per_row_mod=True)

    return (xp.reshape(BATCH, SEQ, D_MODEL), xs.reshape(DEC_BATCH, 1, D_MODEL),
            jnp.stack(kp_l), jnp.stack(vp_l), jnp.stack(sp_l),
            jnp.stack(ks_l), jnp.stack(vs_l), jnp.stack(ss_l))
```

```python
import functools

import jax
import jax.numpy as jnp
import numpy as np
from jax import lax
from jax.experimental import pallas as pl
from jax.experimental.pallas import tpu as pltpu

D_MODEL = 1024
BATCH = 4
SEQ = 4096
DEPTH = 2
DEC_BATCH = 128
N_HEADS = 8
N_KV = 2
HEAD_DIM = 64
GROUP = N_HEADS // N_KV
WINDOW = 128
GLA_HEADS = 4
GLA_DK = 64
GLA_DV = 128
GATE_RANK = 16
GATE_TAU = 16.0
D_FF = 2816
N_EXPERTS = 8
D_FF_EXPERT = 1408
EPS = 1e-6

Q_W = N_HEADS * HEAD_DIM
KV_W = N_KV * HEAD_DIM
GK_W = GLA_HEADS * GLA_DK
GV_W = GLA_HEADS * GLA_DV

LANES = 128
GATE_PAD = LANES
ROUTER_PAD = LANES
VMEM_LIMIT = 56 * 1024 * 1024

F32 = jnp.float32
BF16 = jnp.bfloat16

_C_Q = 0
_C_K = _C_Q + Q_W
_C_V = _C_K + KV_W
_C_GQ = _C_V + KV_W
_C_GK = _C_GQ + GK_W
_C_GV = _C_GK + GK_W
_C_RB = _C_GV + GV_W
_C_GA = _C_RB + GV_W
_C_GB = _C_GA + D_MODEL
_C_LR = _C_GB + D_MODEL
PROJ_PAD = _C_LR + GATE_PAD

GLA_CHUNK = 128
GLA_LEVELS = 7


def _params(sem, vmem=VMEM_LIMIT):
    return pltpu.CompilerParams(dimension_semantics=sem, vmem_limit_bytes=vmem)


def _dot(a, b):
    return jnp.dot(a, b, preferred_element_type=F32)


def _dot_nt(a, b):
    return lax.dot_general(a, b, (((1,), (1,)), ((), ())), preferred_element_type=F32)


def _dot_tn(a, b):
    return lax.dot_general(a, b, (((0,), (0,)), ((), ())), preferred_element_type=F32)


def _sigmoid(x):
    return 1.0 / (1.0 + jnp.exp(-x))


def _silu(x):
    return x * _sigmoid(x)


def _ada_kernel(c_ref, w_ref, b_ref, o_ref):
    c = c_ref[...]
    o_ref[...] = _dot(_silu(c).astype(BF16), w_ref[...].astype(BF16)) + b_ref[...]


def _ada(c_all, ada_w, ada_b):
    rows = c_all.shape[0]
    tn = 1024
    return pl.pallas_call(
        _ada_kernel,
        grid=(DEPTH, 6 * D_MODEL // tn),
        in_specs=[
            pl.BlockSpec((rows, D_MODEL), lambda l, j: (0, 0)),
            pl.BlockSpec((None, D_MODEL, tn), lambda l, j: (l, 0, j)),
            pl.BlockSpec((None, 1, tn), lambda l, j: (l, 0, j)),
        ],
        out_specs=pl.BlockSpec((None, rows, tn), lambda l, j: (l, 0, j)),
        out_shape=jax.ShapeDtypeStruct((DEPTH, rows, 6 * D_MODEL), F32),
        compiler_params=_params(("parallel", "parallel")),
        name="ada_mod",
    )(c_all, ada_w, ada_b.reshape(DEPTH, 1, 6 * D_MODEL))


def _mixin_kernel(x_ref, sh_ref, sc_ref, n1_ref, w_ref, bd_ref, qg_ref, kg_ref, wa2_ref, ba_ref,
                  q_ref, k_ref, v_ref, gq_ref, gk_ref, gv_ref, la_ref, rb_ref, sga_ref, sgb_ref):
    x = x_ref[...]
    ms = jnp.mean(x * x, axis=-1, keepdims=True)
    h = x * lax.rsqrt(ms + EPS) * n1_ref[...]
    h = h * (1.0 + sc_ref[...]) + sh_ref[...]
    hb = h.astype(BF16)

    def proj(a, b):
        return _dot(hb, w_ref[:, a:b])

    q = proj(_C_Q, _C_K)
    ssq = _dot((q * q).astype(BF16), bd_ref[...])
    q_ref[...] = (q * lax.rsqrt(ssq * (1.0 / HEAD_DIM) + EPS) * qg_ref[...]).astype(BF16)
    k = proj(_C_K, _C_V)
    ssk = _dot((k * k).astype(BF16), bd_ref[0:KV_W, 0:KV_W])
    k_ref[...] = k * lax.rsqrt(ssk * (1.0 / HEAD_DIM) + EPS) * kg_ref[...]
    v_ref[...] = proj(_C_V, _C_GQ)
    gq_ref[...] = proj(_C_GQ, _C_GK) * (GLA_DK ** -0.5)
    gk_ref[...] = proj(_C_GK, _C_GV)
    gv_ref[...] = proj(_C_GV, _C_RB)
    rb_ref[...] = _silu(proj(_C_RB, _C_GA)).astype(BF16)
    sga_ref[...] = _sigmoid(proj(_C_GA, _C_GB)).astype(BF16)
    sgb_ref[...] = _sigmoid(proj(_C_GB, _C_LR)).astype(BF16)
    ga = proj(_C_LR, PROJ_PAD)
    xg = _dot(ga.astype(BF16), wa2_ref[...]) + ba_ref[...]
    la_ref[...] = (jnp.minimum(xg, 0.0) - jnp.log1p(jnp.exp(-jnp.abs(xg)))) * (1.0 / GATE_TAU)


def _mixin(x, sh, sc, n1, w, bd, qg, kg, wa2, ba, *, tm, per_row_mod):
    n = x.shape[0]
    nt = n // tm
    mod_spec = _mod_spec(tm, n, sh, per_row_mod)

    def row(i):
        return (i, 0)

    def const(shape):
        return pl.BlockSpec(shape, lambda i: (0,) * len(shape))

    def out(width, dtype):
        return pl.BlockSpec((tm, width), row), jax.ShapeDtypeStruct((n, width), dtype)

    outs = [out(Q_W, BF16), out(KV_W, F32), out(KV_W, F32), out(GK_W, F32), out(GK_W, F32),
            out(GV_W, F32), out(GK_W, F32), out(GV_W, BF16), out(D_MODEL, BF16), out(D_MODEL, BF16)]
    return pl.pallas_call(
        _mixin_kernel,
        grid=(nt,),
        in_specs=[
            pl.BlockSpec((tm, D_MODEL), row), mod_spec, mod_spec, const((1, D_MODEL)),
            const((D_MODEL, PROJ_PAD)), const((Q_W, Q_W)), const((1, Q_W)), const((1, KV_W)),
            const((GATE_PAD, GK_W)), const((1, GK_W)),
        ],
        out_specs=[o[0] for o in outs],
        out_shape=[o[1] for o in outs],
        compiler_params=_params(("parallel",)),
        name="mixer_in",
    )(x, sh, sc, n1, w, bd, qg, kg, wa2, ba)


SWA_BLOCKS = 4


def _head_slope(h):
    return float(2.0 ** (-8.0 * (h + 1) / N_HEADS))


def _swa_prompt_kernel(sink_ref, q_ref, kp_ref, kc_ref, vp_ref, vc_ref, o_ref):
    n = pl.program_id(1)
    blk = WINDOW
    kall = jnp.concatenate([kp_ref[...], kc_ref[...]], axis=0).astype(BF16)
    vall = jnp.concatenate([vp_ref[...], vc_ref[...]], axis=0).astype(BF16)
    row = lax.broadcasted_iota(jnp.int32, (blk, 2 * blk), 0)
    col = lax.broadcasted_iota(jnp.int32, (blk, 2 * blk), 1)
    dist = row + blk - col
    in_window = (dist >= 0) & (dist <= WINDOW)
    first_key = jnp.where(n > 0, 0, blk)
    distf = dist.astype(F32)
    klane = lax.broadcasted_iota(jnp.int32, kall.shape, 1)
    kall_kv = [jnp.where((klane // HEAD_DIM) == kv, kall, jnp.zeros_like(kall)) for kv in range(N_KV)]
    olane = lax.broadcasted_iota(jnp.int32, (blk, KV_W), 1)
    q = q_ref[...]
    for j in range(SWA_BLOCKS):
        valid = in_window & (col >= first_key) if j == 0 else in_window
        vv = vall[j * blk:(j + 2) * blk]
        outs = []
        for g in range(GROUP):
            qp = q[j * blk:(j + 1) * blk, g * KV_W:(g + 1) * KV_W]
            pair = []
            for kv in range(N_KV):
                h = kv * GROUP + g
                s = _dot_nt(qp, kall_kv[kv][j * blk:(j + 2) * blk]) - _head_slope(h) * distf
                s = jnp.where(valid, s, -jnp.inf)
                sink = sink_ref[h]
                m = jnp.maximum(jnp.max(s, axis=-1, keepdims=True), sink)
                p = jnp.exp(s - m)
                den = jnp.sum(p, axis=-1, keepdims=True) + jnp.exp(sink - m)
                pair.append(_dot(p.astype(BF16), vv) * (1.0 / den))
            outs.append(jnp.where(olane < HEAD_DIM, pair[0], pair[1]))
        o_ref[j * blk:(j + 1) * blk, :] = jnp.concatenate(outs, axis=-1).astype(BF16)


def _swa_prompt(sinks, q, k, v):
    nb = SEQ // WINDOW
    steps = nb // SWA_BLOCKS
    tq = SWA_BLOCKS * WINDOW

    def cur(b, n):
        return (b * steps + n, 0)

    def prev(b, n):
        return (b * nb + jnp.maximum(n * SWA_BLOCKS - 1, 0), 0)

    return pl.pallas_call(
        _swa_prompt_kernel,
        grid=(BATCH, steps),
        in_specs=[
            pl.BlockSpec(memory_space=pltpu.SMEM),
            pl.BlockSpec((tq, Q_W), cur),
            pl.BlockSpec((WINDOW, KV_W), prev), pl.BlockSpec((tq, KV_W), cur),
            pl.BlockSpec((WINDOW, KV_W), prev), pl.BlockSpec((tq, KV_W), cur),
        ],
        out_specs=pl.BlockSpec((tq, Q_W), cur),
        out_shape=jax.ShapeDtypeStruct((BATCH * SEQ, Q_W), BF16),
        compiler_params=_params(("parallel", "parallel")),
        name="swa_prompt",
    )(sinks, q, k, k, v, v)


SAMPLE_TB = 8
KEY_PAD = 2 * WINDOW


def _swa_sample_kernel(q_ref, kn_ref, vn_ref, ck_ref, cv_ref, seg_ref, sl_ref, sk_ref,
                       o_ref, ok_ref, ov_ref):
    w = WINDOW
    srow = lax.broadcasted_iota(jnp.int32, (KEY_PAD, LANES), 0)
    distf = (w - srow).astype(F32)
    key_ok = srow <= w
    slope = sl_ref[...]
    sink = sk_ref[...]
    orow = lax.broadcasted_iota(jnp.int32, (LANES, KV_W), 0)
    olane = lax.broadcasted_iota(jnp.int32, (LANES, KV_W), 1)
    own_kv = (olane // HEAD_DIM) == (orow % N_KV)
    pad = jnp.zeros((KEY_PAD - w - 8, KV_W), F32)
    row8 = lax.broadcasted_iota(jnp.int32, (8, KV_W), 0)
    qf = q_ref[...].astype(F32)
    for bi in range(SAMPLE_TB):
        kc = ck_ref[bi]
        vc = cv_ref[bi]
        kn = kn_ref[bi:bi + 1, :]
        vn = vn_ref[bi:bi + 1, :]
        ok_ref[bi, pl.ds(0, w - 1), :] = ck_ref[bi, pl.ds(1, w - 1), :]
        ok_ref[bi, pl.ds(w - 1, 1), :] = kn
        ov_ref[bi, pl.ds(0, w - 1), :] = cv_ref[bi, pl.ds(1, w - 1), :]
        ov_ref[bi, pl.ds(w - 1, 1), :] = vn
        kn8 = jnp.where(row8 == 0, jnp.broadcast_to(kn, (8, KV_W)), 0.0)
        vn8 = jnp.where(row8 == 0, jnp.broadcast_to(vn, (8, KV_W)), 0.0)
        kfull = jnp.concatenate([kc, kn8, pad], axis=0)
        vfull = jnp.concatenate([vc, vn8, pad], axis=0).astype(BF16)
        s = jnp.zeros((KEY_PAD, LANES), F32)
        for g in range(GROUP):
            qrow = qf[bi:bi + 1, g * KV_W:(g + 1) * KV_W]
            s = s + _dot((kfull * qrow).astype(BF16), seg_ref[g])
        s = s - slope * distf
        s = jnp.where(key_ok, s, -jnp.inf)
        m = jnp.maximum(jnp.max(s, axis=0, keepdims=True), sink)
        p = jnp.exp(s - m)
        den = jnp.sum(p, axis=0, keepdims=True) + jnp.exp(sink - m)
        p = p * (1.0 / den)
        r = _dot_tn(p.astype(BF16), vfull)
        r = jnp.where(own_kv, r, 0.0)
        o_ref[bi] = r[0:N_HEADS, :]


def _swa_sample(q, kn, vn, ck, cv, seg, slopes, sinks):
    tb = SAMPLE_TB
    nb = DEC_BATCH // tb
    w = WINDOW

    def const(shape):
        return pl.BlockSpec(shape, lambda i: (0,) * len(shape))

    return pl.pallas_call(
        _swa_sample_kernel,
        grid=(nb,),
        in_specs=[
            pl.BlockSpec((tb, Q_W), lambda i: (i, 0)),
            pl.BlockSpec((tb, KV_W), lambda i: (i, 0)),
            pl.BlockSpec((tb, KV_W), lambda i: (i, 0)),
            pl.BlockSpec((tb, w, KV_W), lambda i: (i, 0, 0)),
            pl.BlockSpec((tb, w, KV_W), lambda i: (i, 0, 0)),
            const((GROUP, KV_W, LANES)), const((1, LANES)), const((1, LANES)),
        ],
        out_specs=[
            pl.BlockSpec((tb, N_HEADS, KV_W), lambda i: (i, 0, 0)),
            pl.BlockSpec((tb, w, KV_W), lambda i: (i, 0, 0)),
            pl.BlockSpec((tb, w, KV_W), lambda i: (i, 0, 0)),
        ],
        out_shape=[
            jax.ShapeDtypeStruct((DEC_BATCH, N_HEADS, KV_W), F32),
            jax.ShapeDtypeStruct((DEC_BATCH, w, KV_W), F32),
            jax.ShapeDtypeStruct((DEC_BATCH, w, KV_W), F32),
        ],
        compiler_params=_params(("parallel",)),
        name="swa_sample",
    )(q, kn, vn, ck, cv, seg, slopes, sinks)


def _gla_cum_matrix():
    c = GLA_CHUNK
    tri = np.tril(np.ones((c, c), np.float32))
    i = np.arange(c)
    blocks = []
    for lvl in range(GLA_LEVELS):
        half = 1 << lvl
        mid = (i // (2 * half)) * (2 * half) + half - 1
        blocks.append(tri - tri[mid])
    blocks.append(tri)
    return np.concatenate(blocks, axis=0)


def _split3(x):
    hi = x.astype(BF16)
    r1 = x - hi.astype(F32)
    mid = r1.astype(BF16)
    lo = (r1 - mid.astype(F32)).astype(BF16)
    return hi, mid, lo


def _gla_prompt_kernel(q_ref, k_ref, la_ref, v_ref, rb_ref, gn_ref, ut_ref, o_ref, s_ref, st_ref):
    c = pl.program_id(1)
    cl = GLA_CHUNK

    @pl.when(c == 0)
    def _():
        st_ref[0] = jnp.zeros(st_ref.shape[1:], F32)

    ut = ut_ref[...]
    hi, mid, lo = _split3(la_ref[...])
    tall = _dot(ut, hi) + _dot(ut, mid) + _dot(ut, lo)
    q = q_ref[...]
    k = k_ref[...]
    lane = lax.broadcasted_iota(jnp.int32, (cl, GK_W), 1)
    row = lax.broadcasted_iota(jnp.int32, (cl, GK_W), 0)
    head_of_lane = lane // GLA_DK
    ri = lax.broadcasted_iota(jnp.int32, (GLA_HEADS * cl, cl), 0) & (cl - 1)
    ci = lax.broadcasted_iota(jnp.int32, (GLA_HEADS * cl, cl), 1)
    attn_all = jnp.zeros((GLA_HEADS * cl, cl), F32)
    for lvl in range(GLA_LEVELS + 1):
        if lvl < GLA_LEVELS:
            e = jnp.exp(-jnp.abs(tall[lvl * cl:(lvl + 1) * cl]))
            upper = ((row >> lvl) & 1) == 1
            qt = jnp.where(upper, q * e, 0.0).astype(BF16)
            kt = jnp.where(upper, 0.0, k * e).astype(BF16)
            pair_ok = (ri >> (lvl + 1)) == (ci >> (lvl + 1))
        else:
            qt = q.astype(BF16)
            kt = k.astype(BF16)
            pair_ok = ri == ci
        q_heads = jnp.concatenate(
            [jnp.where(head_of_lane == h, qt, jnp.zeros_like(qt)) for h in range(GLA_HEADS)], axis=0)
        attn_all = attn_all + jnp.where(pair_ok, _dot_nt(q_heads, kt), 0.0)
    attn = [attn_all[h * cl:(h + 1) * cl] for h in range(GLA_HEADS)]

    cum = tall[GLA_LEVELS * cl:(GLA_LEVELS + 1) * cl]
    last = cum[cl - 1:cl, :]
    qe = (q * jnp.exp(cum)).astype(BF16)
    kd = (k * jnp.exp(last - cum)).astype(BF16)
    ones = jnp.ones((cl, LANES), BF16)
    dec = jnp.exp(_dot_tn(hi, ones) + _dot_tn(mid, ones) + _dot_tn(lo, ones))
    v = v_ref[...]
    rb = rb_ref[...]
    outs = []
    for h in range(GLA_HEADS):
        vh = v[:, h * GLA_DV:(h + 1) * GLA_DV].astype(BF16)
        st = st_ref[c % 2, h]
        o = _dot(attn[h].astype(BF16), vh) + _dot(qe, st.astype(BF16))
        kdh = jnp.where(head_of_lane == h, kd, jnp.zeros_like(kd))
        st_ref[(c + 1) % 2, h] = st * dec + _dot_tn(kdh, vh)
        ms = jnp.mean(o * o, axis=-1, keepdims=True)
        g = o * lax.rsqrt(ms + EPS) * gn_ref[...]
        outs.append(g * rb[:, h * GLA_DV:(h + 1) * GLA_DV].astype(F32))
    o_ref[...] = jnp.concatenate(outs, axis=-1).astype(BF16)

    @pl.when(c == pl.num_programs(1) - 1)
    def _():
        fin = (SEQ // GLA_CHUNK) % 2
        s_ref[...] = st_ref[fin, 0] + st_ref[fin, 1] + st_ref[fin, 2] + st_ref[fin, 3]


def _gla_prompt(gq, gk, la, gv, rbs, gn, ut):
    cl = GLA_CHUNK
    nc = SEQ // cl

    def row(b, c):
        return (b * nc + c, 0)

    return pl.pallas_call(
        _gla_prompt_kernel,
        grid=(BATCH, nc),
        in_specs=[
            pl.BlockSpec((cl, GK_W), row), pl.BlockSpec((cl, GK_W), row), pl.BlockSpec((cl, GK_W), row),
            pl.BlockSpec((cl, GV_W), row), pl.BlockSpec((cl, GV_W), row),
            pl.BlockSpec((1, GLA_DV), lambda b, c: (0, 0)),
            pl.BlockSpec(((GLA_LEVELS + 1) * cl, cl), lambda b, c: (0, 0)),
        ],
        out_specs=[
            pl.BlockSpec((cl, GV_W), row),
            pl.BlockSpec((None, GK_W, GLA_DV), lambda b, c: (b, 0, 0)),
        ],
        out_shape=[
            jax.ShapeDtypeStruct((BATCH * SEQ, GV_W), BF16),
            jax.ShapeDtypeStruct((BATCH, GK_W, GLA_DV), F32),
        ],
        scratch_shapes=[pltpu.VMEM((2, GLA_HEADS, GK_W, GLA_DV), F32)],
        compiler_params=_params(("parallel", "arbitrary")),
        name="gla_prompt",
    )(gq, gk, la, gv, rbs, gn, ut)


def _gla_sample_kernel(q_ref, k_ref, la_ref, v_ref, rb_ref, gn_ref, s_ref, o_ref, so_ref):
    tb = SAMPLE_TB
    dec = jnp.exp(la_ref[...])
    pieces = []
    for x in (dec, k_ref[...], q_ref[...]):
        hi, mid, lo = _split3(x)
        stacked = jnp.concatenate(
            [hi.astype(F32), mid.astype(F32), lo.astype(F32), jnp.zeros_like(x)], axis=0)
        pieces.append(stacked.astype(BF16))
    prow = lax.broadcasted_iota(jnp.int32, (4 * tb, LANES), 0)
    rb = rb_ref[...].astype(F32)
    v = v_ref[...]
    for bi in range(tb):
        sel = jnp.where((prow % tb) == bi, 1.0, 0.0).astype(BF16)
        a_col, k_col, q_col = [_dot_tn(p, sel) for p in pieces]
        for h in range(GLA_HEADS):
            rs = slice(h * GLA_DK, (h + 1) * GLA_DK)
            vs = slice(h * GLA_DV, (h + 1) * GLA_DV)
            s_new = a_col[rs] * s_ref[bi, rs, :] + k_col[rs] * v[bi:bi + 1, vs]
            so_ref[bi, rs, :] = s_new
            o = jnp.sum(q_col[rs] * s_new, axis=0, keepdims=True)
            ms = jnp.mean(o * o, axis=-1, keepdims=True)
            g = o * lax.rsqrt(ms + EPS) * gn_ref[...]
            o_ref[bi:bi + 1, vs] = g * rb[bi:bi + 1, vs]


def _gla_sample(gq, gk, la, gv, rbs, gn, state):
    tb = SAMPLE_TB
    nb = DEC_BATCH // tb

    def row(w):
        return pl.BlockSpec((tb, w), lambda i: (i, 0))

    st_spec = pl.BlockSpec((tb, GK_W, GLA_DV), lambda i: (i, 0, 0))
    return pl.pallas_call(
        _gla_sample_kernel,
        grid=(nb,),
        in_specs=[row(GK_W), row(GK_W), row(GK_W), row(GV_W), row(GV_W),
                  pl.BlockSpec((1, GLA_DV), lambda i: (0, 0)), st_spec],
        out_specs=[row(GV_W), st_spec],
        out_shape=[
            jax.ShapeDtypeStruct((DEC_BATCH, GV_W), F32),
            jax.ShapeDtypeStruct((DEC_BATCH, GK_W, GLA_DV), F32),
        ],
        compiler_params=_params(("parallel",)),
        name="gla_sample",
    )(gq, gk, la, gv, rbs, gn, state)


def _merge_kernel(x_ref, a_ref, g_ref, sga_ref, sgb_ref, wpa_ref, wpb_ref, wo_ref,
                  g1_ref, sh_ref, sc_ref, n2_ref, *rest):
    with_router = len(rest) == 7
    x1_ref, h2_ref = rest[3:5] if with_router else rest
    ya = _dot(a_ref[...].astype(BF16), wpa_ref[...])
    yb = _dot(g_ref[...].astype(BF16), wpb_ref[...])
    merged = sga_ref[...].astype(F32) * ya + sgb_ref[...].astype(F32) * yb
    mix = _dot(merged.astype(BF16), wo_ref[...])
    x1 = x_ref[...] + g1_ref[...] * mix
    x1_ref[...] = x1
    ms = jnp.mean(x1 * x1, axis=-1, keepdims=True)
    h = x1 * lax.rsqrt(ms + EPS) * n2_ref[...]
    h2 = h * (1.0 + sc_ref[...]) + sh_ref[...]
    h2_ref[...] = h2.astype(h2_ref.dtype)
    if with_router:
        rw_ref, rbias_ref, tri_ref, route_ref, cnt_ref = rest[0], rest[1], rest[2], rest[5], rest[6]
        logits = _dot(h2.astype(BF16), rw_ref[...]) + rbias_ref[...]
        packed, counts = _route_pack(logits, tri_ref[...])
        route_ref[...] = packed
        cnt_ref[...] = jnp.broadcast_to(counts, cnt_ref.shape)


def _mod_spec(tm, n, mod, per_row_mod):
    if per_row_mod:
        return pl.BlockSpec((tm, D_MODEL), lambda i, *_: (i, 0))
    tiles_per_mod = (n // mod.shape[0]) // tm
    return pl.BlockSpec((None, 1, D_MODEL), lambda i, *_: (i // tiles_per_mod, 0, 0))


def _merge(x, a, g, sga, sgb, wpa, wpb, wo, g1, sh2, sc2, n2, *, tm, per_row_mod, h2_dtype=BF16,
           router=None):
    n = x.shape[0]
    mod_spec = _mod_spec(tm, n, g1, per_row_mod)

    def row(w):
        return pl.BlockSpec((tm, w), lambda i: (i, 0))

    def const(shape):
        return pl.BlockSpec(shape, lambda i: (0,) * len(shape))

    in_specs = [row(D_MODEL), row(a.shape[1]), row(GV_W), row(D_MODEL), row(D_MODEL),
                const(wpa.shape), const(wpb.shape), const(wo.shape),
                mod_spec, mod_spec, mod_spec, const((1, D_MODEL))]
    out_specs = [row(D_MODEL), row(D_MODEL)]
    out_shape = [jax.ShapeDtypeStruct((n, D_MODEL), F32), jax.ShapeDtypeStruct((n, D_MODEL), h2_dtype)]
    args = [x, a, g, sga, sgb, wpa, wpb, wo, g1, sh2, sc2, n2]
    if router is not None:
        tri = jnp.asarray(np.tril(np.ones((tm, tm), np.float32), -1), BF16)
        in_specs += [const(router[0].shape), const(router[1].shape), const((tm, tm))]
        out_specs += [row(ROUTER_PAD), pl.BlockSpec((None, 8, ROUTER_PAD), lambda i: (i, 0, 0))]
        out_shape += [jax.ShapeDtypeStruct((n, ROUTER_PAD), F32),
                      jax.ShapeDtypeStruct((n // tm, 8, ROUTER_PAD), F32)]
        args += [router[0], router[1], tri]
    return pl.pallas_call(
        _merge_kernel,
        grid=(n // tm,),
        in_specs=in_specs,
        out_specs=out_specs,
        out_shape=out_shape,
        compiler_params=_params(("parallel",)),
        name="merge_out",
    )(*args)


def _ffn_kernel(h_ref, x_ref, g2_ref, wg_ref, wu_ref, wd_ref, o_ref, acc_ref):
    f = pl.program_id(1)
    hb = h_ref[...]
    act = (_silu(_dot(hb, wg_ref[...])) * _dot(hb, wu_ref[...])).astype(BF16)
    y = _dot(act, wd_ref[...])

    @pl.when(f == 0)
    def _():
        acc_ref[...] = y

    @pl.when(f > 0)
    def _():
        acc_ref[...] += y

    @pl.when(f == pl.num_programs(1) - 1)
    def _():
        o_ref[...] = x_ref[...] + g2_ref[...] * acc_ref[...]


def _ffn(h2, x1, g2, wg, wu, wd, *, tm, tf, per_row_mod):
    n = h2.shape[0]
    mod_spec = _mod_spec(tm, n, g2, per_row_mod)
    return pl.pallas_call(
        _ffn_kernel,
        grid=(n // tm, D_FF // tf),
        in_specs=[
            pl.BlockSpec((tm, D_MODEL), lambda i, f: (i, 0)),
            pl.BlockSpec((tm, D_MODEL), lambda i, f: (i, 0)),
            mod_spec,
            pl.BlockSpec((D_MODEL, tf), lambda i, f: (0, f)),
            pl.BlockSpec((D_MODEL, tf), lambda i, f: (0, f)),
            pl.BlockSpec((tf, D_MODEL), lambda i, f: (f, 0)),
        ],
        out_specs=pl.BlockSpec((tm, D_MODEL), lambda i, f: (i, 0)),
        out_shape=jax.ShapeDtypeStruct((n, D_MODEL), F32),
        scratch_shapes=[pltpu.VMEM((tm, D_MODEL), F32)],
        compiler_params=_params(("parallel", "arbitrary")),
        name="ffn_dense",
    )(h2, x1, g2, wg, wu, wd)


def _moe_kernel(h_ref, x_ref, g2_ref, rw_ref, rbias_ref, wg_ref, wu_ref, wd_ref, o_ref, acc_ref, gate_ref):
    e = pl.program_id(1)
    hb = h_ref[...]
    tm = hb.shape[0]
    lane = lax.broadcasted_iota(jnp.int32, (tm, ROUTER_PAD), 1).astype(F32)

    @pl.when(e == 0)
    def _():
        logits = _dot(hb, rw_ref[...]) + rbias_ref[...]
        i1, i2, p1, p2 = _top2(logits, lane)
        gate_ref[...] = jnp.where(lane == i1, p1, 0.0) + jnp.where(lane == i2, p2, 0.0)
        acc_ref[...] = jnp.zeros_like(acc_ref)

    ge = jnp.sum(jnp.where(lane == e.astype(F32), gate_ref[...], 0.0), axis=-1, keepdims=True)
    act = (_silu(_dot(hb, wg_ref[...])) * _dot(hb, wu_ref[...])).astype(BF16)
    acc_ref[...] += ge * _dot(act, wd_ref[...])

    @pl.when(e == pl.num_programs(1) - 1)
    def _():
        o_ref[...] = x_ref[...] + g2_ref[...] * acc_ref[...]


def _moe(h2, x1, g2, rw, rbias, wg, wu, wd, *, tm, per_row_mod):
    n = h2.shape[0]
    mod_spec = _mod_spec(tm, n, g2, per_row_mod)
    fe = D_FF_EXPERT
    return pl.pallas_call(
        _moe_kernel,
        grid=(n // tm, N_EXPERTS),
        in_specs=[
            pl.BlockSpec((tm, D_MODEL), lambda i, e: (i, 0)),
            pl.BlockSpec((tm, D_MODEL), lambda i, e: (i, 0)),
            mod_spec,
            pl.BlockSpec((D_MODEL, ROUTER_PAD), lambda i, e: (0, 0)),
            pl.BlockSpec((1, ROUTER_PAD), lambda i, e: (0, 0)),
            pl.BlockSpec((None, D_MODEL, fe), lambda i, e: (e, 0, 0)),
            pl.BlockSpec((None, D_MODEL, fe), lambda i, e: (e, 0, 0)),
            pl.BlockSpec((None, fe, D_MODEL), lambda i, e: (e, 0, 0)),
        ],
        out_specs=pl.BlockSpec((tm, D_MODEL), lambda i, e: (i, 0)),
        out_shape=jax.ShapeDtypeStruct((n, D_MODEL), F32),
        scratch_shapes=[pltpu.VMEM((tm, D_MODEL), F32), pltpu.VMEM((tm, ROUTER_PAD), F32)],
        compiler_params=_params(("parallel", "arbitrary")),
        name="moe",
    )(h2, x1, g2, rw, rbias, wg, wu, wd)


def _top2(logits, lane):
    lg = jnp.where(lane < N_EXPERTS, logits, -jnp.inf)
    m1 = jnp.max(lg, axis=-1, keepdims=True)
    i1 = jnp.min(jnp.where(lg == m1, lane, float(ROUTER_PAD)), axis=-1, keepdims=True)
    lg2 = jnp.where(lane == i1, -jnp.inf, lg)
    m2 = jnp.max(lg2, axis=-1, keepdims=True)
    i2 = jnp.min(jnp.where(lg2 == m2, lane, float(ROUTER_PAD)), axis=-1, keepdims=True)
    e2 = jnp.exp(m2 - m1)
    p1 = 1.0 / (1.0 + e2)
    return i1, i2, p1, e2 * p1


def _route_pack(logits, tri):
    lane = lax.broadcasted_iota(jnp.int32, logits.shape, 1).astype(F32)
    i1, i2, p1, p2 = _top2(logits, lane)
    oh1 = jnp.where(lane == i1, 1.0, 0.0)
    oh2 = jnp.where(lane == i2, 1.0, 0.0)
    cnt1 = jnp.sum(oh1, axis=0, keepdims=True)
    cnt2 = jnp.sum(oh2, axis=0, keepdims=True)
    rank1 = jnp.sum(_dot(tri, oh1.astype(BF16)) * oh1, axis=-1, keepdims=True)
    rank2 = jnp.sum((_dot(tri, oh2.astype(BF16)) + cnt1) * oh2, axis=-1, keepdims=True)
    packed = jnp.zeros_like(logits)
    for k, val in enumerate((p1, p2, i1, i2, rank1, rank2)):
        packed = jnp.where(lane == float(k), val, packed)
    return packed, cnt1 + cnt2


def _route_tables(route, tile_counts, tm, n_tiles):
    n = route.shape[0]
    e = route[:, 2:4].astype(jnp.int32)
    rank = route[:, 4:6].astype(jnp.int32)
    tile_cnt = tile_counts[:, 0, :N_EXPERTS].astype(jnp.int32)
    cnt = jnp.sum(tile_cnt, axis=0)
    gsz = ((cnt + tm - 1) // tm) * tm
    gend = jnp.cumsum(gsz)
    seg_start = (gend - gsz)[None, :] + jnp.cumsum(tile_cnt, axis=0) - tile_cnt
    seg_of_token = jnp.repeat(seg_start, n // tile_cnt.shape[0], axis=0)
    experts = jnp.arange(N_EXPERTS, dtype=jnp.int32)
    pos = [jnp.sum(jnp.where(e[:, k:k + 1] == experts, seg_of_token, 0), axis=1) + rank[:, k]
           for k in range(2)]
    pos = jnp.concatenate(pos)
    tile_start = jnp.arange(n_tiles, dtype=jnp.int32) * tm
    tile_expert = jnp.sum((tile_start[:, None] >= gend[None, :]).astype(jnp.int32), axis=1)
    tile_expert = jnp.minimum(tile_expert, N_EXPERTS - 1)
    live = (gend[-1] // tm).reshape(1)
    return pos, tile_expert, live, gend - gsz + cnt, gend


def _row_copy_wait(src_hbm, dst, sem, rows):
    pltpu.make_async_copy(src_hbm.at[pl.ds(0, rows)], dst, sem).wait()


def _moe_dispatch_kernel(pos_ref, pad_lo_ref, pad_hi_ref, live_ref, h_ref, xs_hbm, zbuf, sem, zsem):
    i = pl.program_id(0)
    tt = h_ref.shape[0]
    n = tt * pl.num_programs(0)
    tm = zbuf.shape[0]

    base = i * tt
    for r in range(tt):
        row = h_ref.at[pl.ds(r, 1)]
        pltpu.make_async_copy(row, xs_hbm.at[pl.ds(pos_ref[base + r], 1)], sem).start(priority=0)
        pltpu.make_async_copy(row, xs_hbm.at[pl.ds(pos_ref[n + base + r], 1)], sem).start(priority=1)
    for _ in range(2):
        pltpu.make_async_copy(h_ref, xs_hbm.at[pl.ds(0, tt)], sem).wait()

    @pl.when(i == pl.num_programs(0) - 1)
    def _():
        zbuf[...] = jnp.zeros(zbuf.shape, F32)

        def fill_row(p):
            return pltpu.make_async_copy(zbuf.at[pl.ds(0, 1)], xs_hbm.at[pl.ds(p, 1)], zsem)

        def fill_tile(t):
            return pltpu.make_async_copy(zbuf, xs_hbm.at[pl.ds(pl.multiple_of(t * tm, tm), tm)], zsem)

        for e in range(N_EXPERTS):
            lo, hi = pad_lo_ref[e], pad_hi_ref[e]
            lax.fori_loop(lo, hi, lambda p, c: (fill_row(p).start(), c)[1], 0)
            lax.fori_loop(lo, hi, lambda p, c: (fill_row(p).wait(), c)[1], 0)
        lo, hi = live_ref[0], xs_hbm.shape[0] // tm
        lax.fori_loop(lo, hi, lambda t, c: (fill_tile(t).start(), c)[1], 0)
        lax.fori_loop(lo, hi, lambda t, c: (fill_tile(t).wait(), c)[1], 0)


def _moe_dispatch(h2, pos, pad_lo, pad_hi, live, *, tt, tm, n_tiles):
    n = h2.shape[0]
    grid_spec = pltpu.PrefetchScalarGridSpec(
        num_scalar_prefetch=4,
        grid=(n // tt,),
        in_specs=[pl.BlockSpec((tt, D_MODEL), lambda i, *_: (i, 0))],
        out_specs=pl.BlockSpec(memory_space=pl.ANY),
        scratch_shapes=[pltpu.VMEM((tm, D_MODEL), F32), pltpu.SemaphoreType.DMA(()),
                        pltpu.SemaphoreType.DMA(())],
    )
    return pl.pallas_call(
        _moe_dispatch_kernel,
        grid_spec=grid_spec,
        out_shape=jax.ShapeDtypeStruct((n_tiles * tm, D_MODEL), F32),
        compiler_params=_params(("arbitrary",)),
        name="moe_dispatch",
    )(pos, pad_lo, pad_hi, live, h2)


def _moe_expert_kernel(te_ref, live_ref, x_ref, wg_ref, wu_ref, wd_ref, y_ref):
    del te_ref
    is_live = pl.program_id(0) < live_ref[0]

    @pl.when(is_live)
    def _():
        hb = x_ref[...].astype(BF16)
        act = (_silu(_dot(hb, wg_ref[...])) * _dot(hb, wu_ref[...])).astype(BF16)
        y_ref[...] = _dot(act, wd_ref[...])

    @pl.when(jnp.logical_not(is_live))
    def _():
        y_ref[...] = jnp.zeros(y_ref.shape, F32)


def _moe_experts(xs, tile_expert, live, wg, wu, wd, *, tm, n_tiles):
    fe = D_FF_EXPERT

    def in_tile(t, te, live):
        return (jnp.minimum(t, live[0] - 1), 0)

    def out_tile(t, te, live):
        return (t, 0)

    def expert(t, te, live):
        return (te[t], 0, 0)

    grid_spec = pltpu.PrefetchScalarGridSpec(
        num_scalar_prefetch=2,
        grid=(n_tiles,),
        in_specs=[
            pl.BlockSpec((tm, D_MODEL), in_tile),
            pl.BlockSpec((None, D_MODEL, fe), expert),
            pl.BlockSpec((None, D_MODEL, fe), expert),
            pl.BlockSpec((None, fe, D_MODEL), expert),
        ],
        out_specs=pl.BlockSpec((tm, D_MODEL), out_tile),
    )
    return pl.pallas_call(
        _moe_expert_kernel,
        grid_spec=grid_spec,
        out_shape=jax.ShapeDtypeStruct(xs.shape, F32),
        compiler_params=_params(("arbitrary",)),
        name="moe_experts",
    )(tile_expert, live, xs, wg, wu, wd)


def _moe_combine_kernel(pos_ref, x_ref, g2_ref, r_ref, ys_hbm, o_ref, buf, sem):
    i = pl.program_id(0)
    nt = pl.num_programs(0)
    tt = x_ref.shape[0]
    n = nt * tt

    def fetch(tile, sl):
        base = tile * tt
        for r in range(tt):
            for s in range(2):
                row = pos_ref[s * n + base + r]
                pltpu.make_async_copy(ys_hbm.at[pl.ds(row, 1)], buf.at[sl, pl.ds(s * tt + r, 1)],
                                      sem.at[sl]).start(priority=s)

    @pl.when(i == 0)
    def _():
        fetch(0, 0)

    @pl.when(i + 1 < nt)
    def _():
        fetch(i + 1, (i + 1) % 2)

    sl = i % 2
    _row_copy_wait(ys_hbm, buf.at[sl], sem.at[sl], 2 * tt)
    r = r_ref[...]
    f = r[:, 0:1] * buf[sl, pl.ds(0, tt), :] + r[:, 1:2] * buf[sl, pl.ds(tt, tt), :]
    o_ref[...] = x_ref[...] + g2_ref[...] * f


def _moe_combine(x1, g2, route, pos, ys, *, tt):
    n = x1.shape[0]
    tiles_per_mod = (n // g2.shape[0]) // tt
    grid_spec = pltpu.PrefetchScalarGridSpec(
        num_scalar_prefetch=1,
        grid=(n // tt,),
        in_specs=[
            pl.BlockSpec((tt, D_MODEL), lambda i, pos: (i, 0)),
            pl.BlockSpec((None, 1, D_MODEL), lambda i, pos: (i // tiles_per_mod, 0, 0)),
            pl.BlockSpec((tt, ROUTER_PAD), lambda i, pos: (i, 0)),
            pl.BlockSpec(memory_space=pl.ANY),
        ],
        out_specs=pl.BlockSpec((tt, D_MODEL), lambda i, pos: (i, 0)),
        scratch_shapes=[pltpu.VMEM((2, 2 * tt, D_MODEL), F32), pltpu.SemaphoreType.DMA((2,))],
    )
    return pl.pallas_call(
        _moe_combine_kernel,
        grid_spec=grid_spec,
        out_shape=jax.ShapeDtypeStruct((n, D_MODEL), F32),
        compiler_params=_params(("arbitrary",)),
        name="moe_combine",
    )(pos, x1, g2, route, ys)


def _moe_routed(h2, x1, g2, route, tile_counts, wg, wu, wd, *, tm, tt):
    n = h2.shape[0]
    n_tiles = (2 * n) // tm + N_EXPERTS
    pos, tile_expert, live, pad_lo, pad_hi = _route_tables(route, tile_counts, tm, n_tiles)
    xs = _moe_dispatch(h2, pos, pad_lo, pad_hi, live, tt=tt, tm=tm, n_tiles=n_tiles)
    ys = _moe_experts(xs, tile_expert, live, wg, wu, wd, tm=tm, n_tiles=n_tiles)
    return _moe_combine(x1, g2, route, pos, ys, tt=tt)


def _head_perm():
    idx = []
    for g in range(GROUP):
        for kv in range(N_KV):
            h = kv * GROUP + g
            idx.extend(range(h * HEAD_DIM, (h + 1) * HEAD_DIM))
    return np.asarray(idx, np.int32)


def _relayout_w_in(w):
    pts = np.cumsum([0, Q_W, KV_W, KV_W, GK_W, GK_W, GV_W, GATE_RANK, GV_W, D_MODEL, D_MODEL])
    qa, ka, va, qb, kb, vb, ga, rb, gta, gtb = [w[:, pts[i]:pts[i + 1]] for i in range(10)]
    qa = qa[:, _head_perm()]
    ga = jnp.pad(ga, ((0, 0), (0, GATE_PAD - GATE_RANK)))
    return jnp.concatenate([qa, ka, va, qb, kb, vb, rb, gta, gtb, ga], axis=1).astype(BF16)


def _expand_wpa(wpa_perm):
    src = wpa_perm.reshape(GROUP, N_KV, 1, HEAD_DIM, D_MODEL)
    own = jnp.asarray(np.eye(N_KV, dtype=bool).reshape(1, N_KV, N_KV, 1, 1))
    return jnp.where(own, src, jnp.zeros_like(src)).reshape(N_HEADS * KV_W, D_MODEL)


def _sample_lane_consts(sinks):
    j = np.arange(N_HEADS)
    head = (j % N_KV) * GROUP + j // N_KV
    slopes = np.zeros((1, LANES), np.float32)
    slopes[0, :N_HEADS] = 2.0 ** (-8.0 * (head + 1) / N_HEADS)
    sink_row = jnp.zeros((1, LANES), F32).at[0, :N_HEADS].set(sinks[head])
    seg = np.zeros((GROUP, KV_W, LANES), np.float32)
    for g in range(GROUP):
        for kv in range(N_KV):
            seg[g, kv * HEAD_DIM:(kv + 1) * HEAD_DIM, g * N_KV + kv] = 1.0
    return jnp.asarray(seg, BF16), jnp.asarray(slopes), sink_row


def kernel(x_prompt, x_sample, cache_k, cache_v, state_gla, c_prompt, c_sample, ada_w, ada_b, norm1_g, norm2_g, w_in, q_norm_g, k_norm_g, attn_sinks, gla_wa2, gla_ba, gla_norm_g, w_branch_a, w_branch_b, w_out, ffn_w_gate, ffn_w_up, ffn_w_down, router_w, router_b, moe_w_gate, moe_w_up, moe_w_down):
    n_p = BATCH * SEQ
    xp = x_prompt.reshape(n_p, D_MODEL)
    xs = x_sample.reshape(DEC_BATCH, D_MODEL)

    c_rows = BATCH + DEC_BATCH
    c_pad = -c_rows % 8
    c_all = jnp.pad(jnp.concatenate([c_prompt, c_sample], axis=0), ((0, c_pad), (0, 0)))
    mod = _ada(c_all, ada_w, ada_b)

    bd = jnp.asarray(np.kron(np.eye(N_HEADS), np.ones((HEAD_DIM, HEAD_DIM))), BF16)
    ut = jnp.asarray(_gla_cum_matrix(), BF16)
    perm = _head_perm()

    kp_l, vp_l, sp_l, ks_l, vs_l, ss_l = [], [], [], [], [], []
    for l in range(DEPTH):
        m = mod[l].reshape(c_rows + c_pad, 6, D_MODEL)
        mod_p = [m[:BATCH, i].reshape(BATCH, 1, D_MODEL) for i in range(6)]
        mod_s = [m[BATCH:c_rows, i] for i in range(6)]

        w = _relayout_w_in(w_in[l])
        qg = (jnp.tile(q_norm_g[l], N_HEADS) * (HEAD_DIM ** -0.5)).reshape(1, Q_W)
        kg = jnp.tile(k_norm_g[l], N_KV).reshape(1, KV_W)
        wa2 = jnp.pad(gla_wa2[l], ((0, GATE_PAD - GATE_RANK), (0, 0))).astype(BF16)
        ba = gla_ba[l].reshape(1, GK_W)
        n1 = norm1_g[l].reshape(1, D_MODEL)
        n2 = norm2_g[l].reshape(1, D_MODEL)
        gn = gla_norm_g[l].reshape(1, GLA_DV)
        wpa = w_branch_a[l][perm].astype(BF16)
        wpa_x = _expand_wpa(wpa)
        wpb = w_branch_b[l].astype(BF16)
        wo = w_out[l].astype(BF16)
        sinks_perm = attn_sinks[l]
        seg, slope_row, sink_row = _sample_lane_consts(attn_sinks[l])

        q, k, v, gq, gk, gv, la, rbs, sga, sgb = _mixin(
            xp, mod_p[0], mod_p[1], n1, w, bd, qg, kg, wa2, ba, tm=512, per_row_mod=False)
        a_out = _swa_prompt(sinks_perm, q, k, v)
        g_out, s_fin = _gla_prompt(gq, gk, la, gv, rbs, gn, ut)
        is_moe = l % 2 == 1
        if is_moe:
            rw = jnp.pad(router_w[l // 2], ((0, 0), (0, ROUTER_PAD - N_EXPERTS))).astype(BF16)
            rbias = jnp.pad(router_b[l // 2], (0, ROUTER_PAD - N_EXPERTS)).reshape(1, ROUTER_PAD)
        merged = _merge(xp, a_out, g_out, sga, sgb, wpa, wpb, wo, mod_p[2], mod_p[3], mod_p[4], n2,
                        tm=512, per_row_mod=False, h2_dtype=F32 if is_moe else BF16,
                        router=(rw, rbias) if is_moe else None)
        x1, h2 = merged[0], merged[1]
        kp_l.append(k.reshape(BATCH, SEQ, N_KV, HEAD_DIM)[:, SEQ - WINDOW:])
        vp_l.append(v.reshape(BATCH, SEQ, N_KV, HEAD_DIM)[:, SEQ - WINDOW:])
        sp_l.append(s_fin.reshape(BATCH, GLA_HEADS, GLA_DK, GLA_DV))

        qs, ksn, vsn, gqs, gks, gvs, las, rbss, sgas, sgbs = _mixin(
            xs, mod_s[0], mod_s[1], n1, w, bd, qg, kg, wa2, ba, tm=DEC_BATCH, per_row_mod=True)
        a_s, nk, nv = _swa_sample(qs, ksn, vsn, cache_k[l].reshape(DEC_BATCH, WINDOW, KV_W),
                                  cache_v[l].reshape(DEC_BATCH, WINDOW, KV_W), seg, slope_row, sink_row)
        g_s, s_new = _gla_sample(gqs, gks, las, gvs, rbss, gn, state_gla[l].reshape(DEC_BATCH, GK_W, GLA_DV))
        x1s, h2s = _merge(xs, a_s.reshape(DEC_BATCH, N_HEADS * KV_W), g_s, sgas, sgbs, wpa_x, wpb, wo,
                          mod_s[2], mod_s[3], mod_s[4], n2, tm=DEC_BATCH, per_row_mod=True)
        ks_l.append(nk.reshape(DEC_BATCH, WINDOW, N_KV, HEAD_DIM))
        vs_l.append(nv.reshape(DEC_BATCH, WINDOW, N_KV, HEAD_DIM))
        ss_l.append(s_new.reshape(DEC_BATCH, GLA_HEADS, GLA_DK, GLA_DV))

        i = l // 2
        if not is_moe:
            wg, wu, wd = ffn_w_gate[i].astype(BF16), ffn_w_up[i].astype(BF16), ffn_w_down[i].astype(BF16)
            xp = _ffn(h2, x1, mod_p[5], wg, wu, wd, tm=512, tf=D_FF // 2, per_row_mod=False)
            xs = _ffn(h2s, x1s, mod_s[5], wg, wu, wd, tm=DEC_BATCH, tf=D_FF // 2, per_row_mod=True)
        else:
            wg, wu, wd = moe_w_gate[i].astype(BF16), moe_w_up[i].astype(BF16), moe_w_down[i].astype(BF16)
            xp = _moe_routed(h2, x1, mod_p[5], merged[2], merged[3], wg, wu, wd, tm=512, tt=512)
            xs = _moe(h2s, x1s, mod_s[5], rw, rbias, wg, wu, wd, tm=DEC_BATCH, per_row_mod=True)

    return (xp.reshape(BATCH, SEQ, D_MODEL), xs.reshape(DEC_BATCH, 1, D_MODEL),
            jnp.stack(kp_l), jnp.stack(vp_l), jnp.stack(sp_l),
            jnp.stack(ks_l), jnp.stack(vs_l), jnp.stack(ss_l))
```

```python
import functools

import jax
import jax.numpy as jnp
import numpy as np
from jax import lax
from jax.experimental import pallas as pl
from jax.experimental.pallas import tpu as pltpu

D_MODEL = 1024
BATCH = 4
SEQ = 4096
DEPTH = 2
DEC_BATCH = 128
N_HEADS = 8
N_KV = 2
HEAD_DIM = 64
GROUP = N_HEADS // N_KV
WINDOW = 128
GLA_HEADS = 4
GLA_DK = 64
GLA_DV = 128
GATE_RANK = 16
GATE_TAU = 16.0
D_FF = 2816
N_EXPERTS = 8
D_FF_EXPERT = 1408
EPS = 1e-6

Q_W = N_HEADS * HEAD_DIM
KV_W = N_KV * HEAD_DIM
GK_W = GLA_HEADS * GLA_DK
GV_W = GLA_HEADS * GLA_DV

LANES = 128
GATE_PAD = LANES
ROUTER_PAD = LANES
VMEM_LIMIT = 56 * 1024 * 1024

F32 = jnp.float32
BF16 = jnp.bfloat16

_C_Q = 0
_C_K = _C_Q + Q_W
_C_V = _C_K + KV_W
_C_GQ = _C_V + KV_W
_C_GK = _C_GQ + GK_W
_C_GV = _C_GK + GK_W
_C_RB = _C_GV + GV_W
_C_GA = _C_RB + GV_W
_C_GB = _C_GA + D_MODEL
_C_LR = _C_GB + D_MODEL
PROJ_PAD = _C_LR + GATE_PAD

GLA_CHUNK = 128
GLA_LEVELS = 7
GLA_MXU_LEVELS = 3


def _params(sem, vmem=VMEM_LIMIT):
    return pltpu.CompilerParams(dimension_semantics=sem, vmem_limit_bytes=vmem)


def _dot(a, b):
    return jnp.dot(a, b, preferred_element_type=F32)


def _dot_nt(a, b):
    return lax.dot_general(a, b, (((1,), (1,)), ((), ())), preferred_element_type=F32)


def _dot_tn(a, b):
    return lax.dot_general(a, b, (((0,), (0,)), ((), ())), preferred_element_type=F32)


def _sigmoid(x):
    return 1.0 / (1.0 + jnp.exp(-x))


def _silu(x):
    return x * _sigmoid(x)


def _ada_kernel(c_ref, w_ref, b_ref, o_ref):
    c = c_ref[...]
    o_ref[...] = _dot(_silu(c).astype(BF16), w_ref[...].astype(BF16)) + b_ref[...]


def _ada(c_all, ada_w, ada_b):
    rows = c_all.shape[0]
    tn = 1024
    return pl.pallas_call(
        _ada_kernel,
        grid=(DEPTH, 6 * D_MODEL // tn),
        in_specs=[
            pl.BlockSpec((rows, D_MODEL), lambda l, j: (0, 0)),
            pl.BlockSpec((None, D_MODEL, tn), lambda l, j: (l, 0, j)),
            pl.BlockSpec((None, 1, tn), lambda l, j: (l, 0, j)),
        ],
        out_specs=pl.BlockSpec((None, rows, tn), lambda l, j: (l, 0, j)),
        out_shape=jax.ShapeDtypeStruct((DEPTH, rows, 6 * D_MODEL), F32),
        compiler_params=_params(("parallel", "parallel")),
        name="ada_mod",
    )(c_all, ada_w, ada_b.reshape(DEPTH, 1, 6 * D_MODEL))


def _mixin_kernel(x_ref, sh_ref, sc_ref, n1_ref, w_ref, bd_ref, qg_ref, kg_ref, wa2_ref, ba_ref,
                  q_ref, k_ref, v_ref, gq_ref, gk_ref, gv_ref, la_ref, rb_ref, sga_ref, sgb_ref):
    x = x_ref[...]
    ms = jnp.mean(x * x, axis=-1, keepdims=True)
    h = x * lax.rsqrt(ms + EPS) * n1_ref[...]
    h = h * (1.0 + sc_ref[...]) + sh_ref[...]
    hb = h.astype(BF16)

    def proj(a, b):
        return _dot(hb, w_ref[:, a:b])

    q = proj(_C_Q, _C_K)
    ssq = _dot((q * q).astype(BF16), bd_ref[...])
    q_ref[...] = (q * lax.rsqrt(ssq * (1.0 / HEAD_DIM) + EPS) * qg_ref[...]).astype(BF16)
    k = proj(_C_K, _C_V)
    ssk = _dot((k * k).astype(BF16), bd_ref[0:KV_W, 0:KV_W])
    k_ref[...] = k * lax.rsqrt(ssk * (1.0 / HEAD_DIM) + EPS) * kg_ref[...]
    v_ref[...] = proj(_C_V, _C_GQ)
    gq_ref[...] = proj(_C_GQ, _C_GK) * (GLA_DK ** -0.5)
    gk_ref[...] = proj(_C_GK, _C_GV)
    gv_ref[...] = proj(_C_GV, _C_RB)
    rb_ref[...] = _silu(proj(_C_RB, _C_GA)).astype(BF16)
    sga_ref[...] = _sigmoid(proj(_C_GA, _C_GB)).astype(BF16)
    sgb_ref[...] = _sigmoid(proj(_C_GB, _C_LR)).astype(BF16)
    ga = proj(_C_LR, PROJ_PAD)
    xg = _dot(ga.astype(BF16), wa2_ref[...]) + ba_ref[...]
    la_ref[...] = (jnp.minimum(xg, 0.0) - jnp.log1p(jnp.exp(-jnp.abs(xg)))) * (1.0 / GATE_TAU)


def _mixin(x, sh, sc, n1, w, bd, qg, kg, wa2, ba, *, tm, per_row_mod):
    n = x.shape[0]
    nt = n // tm
    mod_spec = _mod_spec(tm, n, sh, per_row_mod)

    def row(i):
        return (i, 0)

    def const(shape):
        return pl.BlockSpec(shape, lambda i: (0,) * len(shape))

    def out(width, dtype):
        return pl.BlockSpec((tm, width), row), jax.ShapeDtypeStruct((n, width), dtype)

    outs = [out(Q_W, BF16), out(KV_W, F32), out(KV_W, F32), out(GK_W, F32), out(GK_W, F32),
            out(GV_W, F32), out(GK_W, F32), out(GV_W, BF16), out(D_MODEL, BF16), out(D_MODEL, BF16)]
    return pl.pallas_call(
        _mixin_kernel,
        grid=(nt,),
        in_specs=[
            pl.BlockSpec((tm, D_MODEL), row), mod_spec, mod_spec, const((1, D_MODEL)),
            const((D_MODEL, PROJ_PAD)), const((Q_W, Q_W)), const((1, Q_W)), const((1, KV_W)),
            const((GATE_PAD, GK_W)), const((1, GK_W)),
        ],
        out_specs=[o[0] for o in outs],
        out_shape=[o[1] for o in outs],
        compiler_params=_params(("parallel",)),
        name="mixer_in",
    )(x, sh, sc, n1, w, bd, qg, kg, wa2, ba)


SWA_BLOCKS = 8


def _head_slope(h):
    return float(2.0 ** (-8.0 * (h + 1) / N_HEADS))


def _swa_prompt_kernel(sink_ref, q_ref, kp_ref, kc_ref, vp_ref, vc_ref, o_ref):
    n = pl.program_id(1)
    blk = WINDOW
    kall = jnp.concatenate([kp_ref[...], kc_ref[...]], axis=0).astype(BF16)
    vall = jnp.concatenate([vp_ref[...], vc_ref[...]], axis=0).astype(BF16)
    row = lax.broadcasted_iota(jnp.int32, (blk, 2 * blk), 0)
    col = lax.broadcasted_iota(jnp.int32, (blk, 2 * blk), 1)
    dist = row + blk - col
    in_window = (dist >= 0) & (dist <= WINDOW)
    first_key = jnp.where(n > 0, 0, blk)
    distf = dist.astype(F32)
    klane = lax.broadcasted_iota(jnp.int32, kall.shape, 1)
    kall_kv = [jnp.where((klane // HEAD_DIM) == kv, kall, jnp.zeros_like(kall)) for kv in range(N_KV)]
    olane = lax.broadcasted_iota(jnp.int32, (blk, KV_W), 1)
    q = q_ref[...]
    for j in range(SWA_BLOCKS):
        valid = in_window & (col >= first_key) if j == 0 else in_window
        vv = vall[j * blk:(j + 2) * blk]
        outs = []
        for g in range(GROUP):
            qp = q[j * blk:(j + 1) * blk, g * KV_W:(g + 1) * KV_W]
            pair = []
            for kv in range(N_KV):
                h = kv * GROUP + g
                s = _dot_nt(qp, kall_kv[kv][j * blk:(j + 2) * blk]) - _head_slope(h) * distf
                s = jnp.where(valid, s, -jnp.inf)
                sink = sink_ref[h]
                m = jnp.maximum(jnp.max(s, axis=-1, keepdims=True), sink)
                p = jnp.exp(s - m)
                den = jnp.sum(p, axis=-1, keepdims=True) + jnp.exp(sink - m)
                pair.append(_dot(p.astype(BF16), vv) * (1.0 / den))
            outs.append(jnp.where(olane < HEAD_DIM, pair[0], pair[1]))
        o_ref[j * blk:(j + 1) * blk, :] = jnp.concatenate(outs, axis=-1).astype(BF16)


def _swa_prompt(sinks, q, k, v):
    nb = SEQ // WINDOW
    steps = nb // SWA_BLOCKS
    tq = SWA_BLOCKS * WINDOW

    def cur(b, n):
        return (b * steps + n, 0)

    def prev(b, n):
        return (b * nb + jnp.maximum(n * SWA_BLOCKS - 1, 0), 0)

    return pl.pallas_call(
        _swa_prompt_kernel,
        grid=(BATCH, steps),
        in_specs=[
            pl.BlockSpec(memory_space=pltpu.SMEM),
            pl.BlockSpec((tq, Q_W), cur),
            pl.BlockSpec((WINDOW, KV_W), prev), pl.BlockSpec((tq, KV_W), cur),
            pl.BlockSpec((WINDOW, KV_W), prev), pl.BlockSpec((tq, KV_W), cur),
        ],
        out_specs=pl.BlockSpec((tq, Q_W), cur),
        out_shape=jax.ShapeDtypeStruct((BATCH * SEQ, Q_W), BF16),
        compiler_params=_params(("parallel", "parallel")),
        name="swa_prompt",
    )(sinks, q, k, k, v, v)


SAMPLE_TB = 8
KEY_PAD = 2 * WINDOW


def _swa_sample_kernel(q_ref, kn_ref, vn_ref, ck_ref, cv_ref, seg_ref, sl_ref, sk_ref,
                       o_ref, ok_ref, ov_ref):
    w = WINDOW
    srow = lax.broadcasted_iota(jnp.int32, (KEY_PAD, LANES), 0)
    distf = (w - srow).astype(F32)
    key_ok = srow <= w
    slope = sl_ref[...]
    sink = sk_ref[...]
    orow = lax.broadcasted_iota(jnp.int32, (LANES, KV_W), 0)
    olane = lax.broadcasted_iota(jnp.int32, (LANES, KV_W), 1)
    own_kv = (olane // HEAD_DIM) == (orow % N_KV)
    pad = jnp.zeros((KEY_PAD - w - 8, KV_W), F32)
    row8 = lax.broadcasted_iota(jnp.int32, (8, KV_W), 0)
    qf = q_ref[...].astype(F32)
    for bi in range(SAMPLE_TB):
        kc = ck_ref[bi]
        vc = cv_ref[bi]
        kn = kn_ref[bi:bi + 1, :]
        vn = vn_ref[bi:bi + 1, :]
        ok_ref[bi, pl.ds(0, w - 1), :] = ck_ref[bi, pl.ds(1, w - 1), :]
        ok_ref[bi, pl.ds(w - 1, 1), :] = kn
        ov_ref[bi, pl.ds(0, w - 1), :] = cv_ref[bi, pl.ds(1, w - 1), :]
        ov_ref[bi, pl.ds(w - 1, 1), :] = vn
        kn8 = jnp.where(row8 == 0, jnp.broadcast_to(kn, (8, KV_W)), 0.0)
        vn8 = jnp.where(row8 == 0, jnp.broadcast_to(vn, (8, KV_W)), 0.0)
        kfull = jnp.concatenate([kc, kn8, pad], axis=0)
        vfull = jnp.concatenate([vc, vn8, pad], axis=0).astype(BF16)
        s = jnp.zeros((KEY_PAD, LANES), F32)
        for g in range(GROUP):
            qrow = qf[bi:bi + 1, g * KV_W:(g + 1) * KV_W]
            s = s + _dot((kfull * qrow).astype(BF16), seg_ref[g])
        s = s - slope * distf
        s = jnp.where(key_ok, s, -jnp.inf)
        m = jnp.maximum(jnp.max(s, axis=0, keepdims=True), sink)
        p = jnp.exp(s - m)
        den = jnp.sum(p, axis=0, keepdims=True) + jnp.exp(sink - m)
        p = p * (1.0 / den)
        r = _dot_tn(p.astype(BF16), vfull)
        r = jnp.where(own_kv, r, 0.0)
        o_ref[bi] = r[0:N_HEADS, :]


def _swa_sample(q, kn, vn, ck, cv, seg, slopes, sinks):
    tb = SAMPLE_TB
    nb = DEC_BATCH // tb
    w = WINDOW

    def const(shape):
        return pl.BlockSpec(shape, lambda i: (0,) * len(shape))

    return pl.pallas_call(
        _swa_sample_kernel,
        grid=(nb,),
        in_specs=[
            pl.BlockSpec((tb, Q_W), lambda i: (i, 0)),
            pl.BlockSpec((tb, KV_W), lambda i: (i, 0)),
            pl.BlockSpec((tb, KV_W), lambda i: (i, 0)),
            pl.BlockSpec((tb, w, KV_W), lambda i: (i, 0, 0)),
            pl.BlockSpec((tb, w, KV_W), lambda i: (i, 0, 0)),
            const((GROUP, KV_W, LANES)), const((1, LANES)), const((1, LANES)),
        ],
        out_specs=[
            pl.BlockSpec((tb, N_HEADS, KV_W), lambda i: (i, 0, 0)),
            pl.BlockSpec((tb, w, KV_W), lambda i: (i, 0, 0)),
            pl.BlockSpec((tb, w, KV_W), lambda i: (i, 0, 0)),
        ],
        out_shape=[
            jax.ShapeDtypeStruct((DEC_BATCH, N_HEADS, KV_W), F32),
            jax.ShapeDtypeStruct((DEC_BATCH, w, KV_W), F32),
            jax.ShapeDtypeStruct((DEC_BATCH, w, KV_W), F32),
        ],
        compiler_params=_params(("parallel",)),
        name="swa_sample",
    )(q, kn, vn, ck, cv, seg, slopes, sinks)


def _gla_cum_matrix():
    c = GLA_CHUNK
    tri = np.tril(np.ones((c, c), np.float32))
    i = np.arange(c)
    blocks = []
    for lvl in range(GLA_MXU_LEVELS):
        half = 1 << lvl
        mid = (i // (2 * half)) * (2 * half) + half - 1
        blocks.append(tri - tri[mid])
    blocks.append(tri)
    return np.concatenate(blocks, axis=0)


def _split3(x):
    hi = x.astype(BF16)
    r1 = x - hi.astype(F32)
    mid = r1.astype(BF16)
    lo = (r1 - mid.astype(F32)).astype(BF16)
    return hi, mid, lo


def _gla_prompt_kernel(q_ref, k_ref, la_ref, v_ref, rb_ref, gn_ref, ut_ref, o_ref, s_ref, st_ref):
    c = pl.program_id(1)
    cl = GLA_CHUNK

    @pl.when(c == 0)
    def _():
        st_ref[0] = jnp.zeros(st_ref.shape[1:], F32)

    ut = ut_ref[...]
    hi, mid, lo = _split3(la_ref[...])
    tall = _dot(ut, hi) + _dot(ut, mid) + _dot(ut, lo)
    q = q_ref[...]
    k = k_ref[...]
    lane = lax.broadcasted_iota(jnp.int32, (cl, GK_W), 1)
    row = lax.broadcasted_iota(jnp.int32, (cl, GK_W), 0)
    head_of_lane = lane // GLA_DK
    ri = lax.broadcasted_iota(jnp.int32, (GLA_HEADS * cl, cl), 0) & (cl - 1)
    ci = lax.broadcasted_iota(jnp.int32, (GLA_HEADS * cl, cl), 1)
    attn_all = jnp.zeros((GLA_HEADS * cl, cl), F32)
    cum = tall[GLA_MXU_LEVELS * cl:(GLA_MXU_LEVELS + 1) * cl]
    for lvl in range(GLA_LEVELS + 1):
        if lvl < GLA_LEVELS:
            if lvl < GLA_MXU_LEVELS:
                t_lvl = tall[lvl * cl:(lvl + 1) * cl]
            else:
                half = 1 << lvl
                mids = [jnp.broadcast_to(cum[b0 + half - 1:b0 + half, :], (2 * half, GK_W))
                        for b0 in range(0, cl, 2 * half)]
                t_lvl = cum - (mids[0] if len(mids) == 1 else jnp.concatenate(mids, axis=0))
            e = jnp.exp(-jnp.abs(t_lvl))
            upper = ((row >> lvl) & 1) == 1
            qt = jnp.where(upper, q * e, 0.0).astype(BF16)
            kt = jnp.where(upper, 0.0, k * e).astype(BF16)
            pair_ok = (ri >> (lvl + 1)) == (ci >> (lvl + 1))
        else:
            qt = q.astype(BF16)
            kt = k.astype(BF16)
            pair_ok = ri == ci
        q_heads = jnp.concatenate(
            [jnp.where(head_of_lane == h, qt, jnp.zeros_like(qt)) for h in range(GLA_HEADS)], axis=0)
        attn_all = attn_all + jnp.where(pair_ok, _dot_nt(q_heads, kt), 0.0)
    attn = [attn_all[h * cl:(h + 1) * cl] for h in range(GLA_HEADS)]

    last = cum[cl - 1:cl, :]
    qe = (q * jnp.exp(cum)).astype(BF16)
    kd = (k * jnp.exp(last - cum)).astype(BF16)
    ones = jnp.ones((cl, LANES), BF16)
    dec = jnp.exp(_dot_tn(hi, ones) + _dot_tn(mid, ones) + _dot_tn(lo, ones))
    v = v_ref[...]
    rb = rb_ref[...]
    outs = []
    for h in range(GLA_HEADS):
        vh = v[:, h * GLA_DV:(h + 1) * GLA_DV].astype(BF16)
        st = st_ref[c % 2, h]
        o = _dot(attn[h].astype(BF16), vh) + _dot(qe, st.astype(BF16))
        kdh = jnp.where(head_of_lane == h, kd, jnp.zeros_like(kd))
        st_ref[(c + 1) % 2, h] = st * dec + _dot_tn(kdh, vh)
        ms = jnp.mean(o * o, axis=-1, keepdims=True)
        g = o * lax.rsqrt(ms + EPS) * gn_ref[...]
        outs.append(g * rb[:, h * GLA_DV:(h + 1) * GLA_DV].astype(F32))
    o_ref[...] = jnp.concatenate(outs, axis=-1).astype(BF16)

    @pl.when(c == pl.num_programs(1) - 1)
    def _():
        fin = (SEQ // GLA_CHUNK) % 2
        s_ref[...] = st_ref[fin, 0] + st_ref[fin, 1] + st_ref[fin, 2] + st_ref[fin, 3]


def _gla_prompt(gq, gk, la, gv, rbs, gn, ut):
    cl = GLA_CHUNK
    nc = SEQ // cl

    def row(b, c):
        return (b * nc + c, 0)

    return pl.pallas_call(
        _gla_prompt_kernel,
        grid=(BATCH, nc),
        in_specs=[
            pl.BlockSpec((cl, GK_W), row), pl.BlockSpec((cl, GK_W), row), pl.BlockSpec((cl, GK_W), row),
            pl.BlockSpec((cl, GV_W), row), pl.BlockSpec((cl, GV_W), row),
            pl.BlockSpec((1, GLA_DV), lambda b, c: (0, 0)),
            pl.BlockSpec(((GLA_MXU_LEVELS + 1) * cl, cl), lambda b, c: (0, 0)),
        ],
        out_specs=[
            pl.BlockSpec((cl, GV_W), row),
            pl.BlockSpec((None, GK_W, GLA_DV), lambda b, c: (b, 0, 0)),
        ],
        out_shape=[
            jax.ShapeDtypeStruct((BATCH * SEQ, GV_W), BF16),
            jax.ShapeDtypeStruct((BATCH, GK_W, GLA_DV), F32),
        ],
        scratch_shapes=[pltpu.VMEM((2, GLA_HEADS, GK_W, GLA_DV), F32)],
        compiler_params=_params(("parallel", "arbitrary")),
        name="gla_prompt",
    )(gq, gk, la, gv, rbs, gn, ut)


def _gla_sample_kernel(q_ref, k_ref, la_ref, v_ref, rb_ref, gn_ref, s_ref, o_ref, so_ref):
    tb = SAMPLE_TB
    dec = jnp.exp(la_ref[...])
    pieces = []
    for x in (dec, k_ref[...], q_ref[...]):
        hi, mid, lo = _split3(x)
        stacked = jnp.concatenate(
            [hi.astype(F32), mid.astype(F32), lo.astype(F32), jnp.zeros_like(x)], axis=0)
        pieces.append(stacked.astype(BF16))
    prow = lax.broadcasted_iota(jnp.int32, (4 * tb, LANES), 0)
    rb = rb_ref[...].astype(F32)
    v = v_ref[...]
    for bi in range(tb):
        sel = jnp.where((prow % tb) == bi, 1.0, 0.0).astype(BF16)
        a_col, k_col, q_col = [_dot_tn(p, sel) for p in pieces]
        for h in range(GLA_HEADS):
            rs = slice(h * GLA_DK, (h + 1) * GLA_DK)
            vs = slice(h * GLA_DV, (h + 1) * GLA_DV)
            s_new = a_col[rs] * s_ref[bi, rs, :] + k_col[rs] * v[bi:bi + 1, vs]
            so_ref[bi, rs, :] = s_new
            o = jnp.sum(q_col[rs] * s_new, axis=0, keepdims=True)
            ms = jnp.mean(o * o, axis=-1, keepdims=True)
            g = o * lax.rsqrt(ms + EPS) * gn_ref[...]
            o_ref[bi:bi + 1, vs] = g * rb[bi:bi + 1, vs]


def _gla_sample(gq, gk, la, gv, rbs, gn, state):
    tb = SAMPLE_TB
    nb = DEC_BATCH // tb

    def row(w):
        return pl.BlockSpec((tb, w), lambda i: (i, 0))

    st_spec = pl.BlockSpec((tb, GK_W, GLA_DV), lambda i: (i, 0, 0))
    return pl.pallas_call(
        _gla_sample_kernel,
        grid=(nb,),
        in_specs=[row(GK_W), row(GK_W), row(GK_W), row(GV_W), row(GV_W),
                  pl.BlockSpec((1, GLA_DV), lambda i: (0, 0)), st_spec],
        out_specs=[row(GV_W), st_spec],
        out_shape=[
            jax.ShapeDtypeStruct((DEC_BATCH, GV_W), F32),
            jax.ShapeDtypeStruct((DEC_BATCH, GK_W, GLA_DV), F32),
        ],
        compiler_params=_params(("parallel",)),
        name="gla_sample",
    )(gq, gk, la, gv, rbs, gn, state)


def _merge_kernel(x_ref, a_ref, g_ref, sga_ref, sgb_ref, wpa_ref, wpb_ref, wo_ref,
                  g1_ref, sh_ref, sc_ref, n2_ref, *rest):
    with_router = len(rest) == 7
    x1_ref, h2_ref = rest[3:5] if with_router else rest
    ya = _dot(a_ref[...].astype(BF16), wpa_ref[...])
    yb = _dot(g_ref[...].astype(BF16), wpb_ref[...])
    merged = sga_ref[...].astype(F32) * ya + sgb_ref[...].astype(F32) * yb
    mix = _dot(merged.astype(BF16), wo_ref[...])
    x1 = x_ref[...] + g1_ref[...] * mix
    x1_ref[...] = x1
    ms = jnp.mean(x1 * x1, axis=-1, keepdims=True)
    h = x1 * lax.rsqrt(ms + EPS) * n2_ref[...]
    h2 = h * (1.0 + sc_ref[...]) + sh_ref[...]
    h2_ref[...] = h2.astype(h2_ref.dtype)
    if with_router:
        rw_ref, rbias_ref, tri_ref, route_ref, cnt_ref = rest[0], rest[1], rest[2], rest[5], rest[6]
        logits = _dot(h2.astype(BF16), rw_ref[...]) + rbias_ref[...]
        packed, counts = _route_pack(logits, tri_ref[...])
        route_ref[...] = packed
        cnt_ref[...] = jnp.broadcast_to(counts, cnt_ref.shape)


def _mod_spec(tm, n, mod, per_row_mod):
    if per_row_mod:
        return pl.BlockSpec((tm, D_MODEL), lambda i, *_: (i, 0))
    tiles_per_mod = (n // mod.shape[0]) // tm
    return pl.BlockSpec((None, 1, D_MODEL), lambda i, *_: (i // tiles_per_mod, 0, 0))


def _merge(x, a, g, sga, sgb, wpa, wpb, wo, g1, sh2, sc2, n2, *, tm, per_row_mod, h2_dtype=BF16,
           router=None):
    n = x.shape[0]
    mod_spec = _mod_spec(tm, n, g1, per_row_mod)

    def row(w):
        return pl.BlockSpec((tm, w), lambda i: (i, 0))

    def const(shape):
        return pl.BlockSpec(shape, lambda i: (0,) * len(shape))

    in_specs = [row(D_MODEL), row(a.shape[1]), row(GV_W), row(D_MODEL), row(D_MODEL),
                const(wpa.shape), const(wpb.shape), const(wo.shape),
                mod_spec, mod_spec, mod_spec, const((1, D_MODEL))]
    out_specs = [row(D_MODEL), row(D_MODEL)]
    out_shape = [jax.ShapeDtypeStruct((n, D_MODEL), F32), jax.ShapeDtypeStruct((n, D_MODEL), h2_dtype)]
    args = [x, a, g, sga, sgb, wpa, wpb, wo, g1, sh2, sc2, n2]
    if router is not None:
        tri = jnp.asarray(np.tril(np.ones((tm, tm), np.float32), -1), BF16)
        in_specs += [const(router[0].shape), const(router[1].shape), const((tm, tm))]
        out_specs += [row(ROUTER_PAD), pl.BlockSpec((None, 8, ROUTER_PAD), lambda i: (i, 0, 0))]
        out_shape += [jax.ShapeDtypeStruct((n, ROUTER_PAD), F32),
                      jax.ShapeDtypeStruct((n // tm, 8, ROUTER_PAD), F32)]
        args += [router[0], router[1], tri]
    return pl.pallas_call(
        _merge_kernel,
        grid=(n // tm,),
        in_specs=in_specs,
        out_specs=out_specs,
        out_shape=out_shape,
        compiler_params=_params(("parallel",)),
        name="merge_out",
    )(*args)


def _ffn_kernel(h_ref, x_ref, g2_ref, wg_ref, wu_ref, wd_ref, o_ref, acc_ref):
    f = pl.program_id(1)
    hb = h_ref[...]
    act = (_silu(_dot(hb, wg_ref[...])) * _dot(hb, wu_ref[...])).astype(BF16)
    y = _dot(act, wd_ref[...])

    @pl.when(f == 0)
    def _():
        acc_ref[...] = y

    @pl.when(f > 0)
    def _():
        acc_ref[...] += y

    @pl.when(f == pl.num_programs(1) - 1)
    def _():
        o_ref[...] = x_ref[...] + g2_ref[...] * acc_ref[...]


def _ffn(h2, x1, g2, wg, wu, wd, *, tm, tf, per_row_mod):
    n = h2.shape[0]
    mod_spec = _mod_spec(tm, n, g2, per_row_mod)
    return pl.pallas_call(
        _ffn_kernel,
        grid=(n // tm, D_FF // tf),
        in_specs=[
            pl.BlockSpec((tm, D_MODEL), lambda i, f: (i, 0)),
            pl.BlockSpec((tm, D_MODEL), lambda i, f: (i, 0)),
            mod_spec,
            pl.BlockSpec((D_MODEL, tf), lambda i, f: (0, f)),
            pl.BlockSpec((D_MODEL, tf), lambda i, f: (0, f)),
            pl.BlockSpec((tf, D_MODEL), lambda i, f: (f, 0)),
        ],
        out_specs=pl.BlockSpec((tm, D_MODEL), lambda i, f: (i, 0)),
        out_shape=jax.ShapeDtypeStruct((n, D_MODEL), F32),
        scratch_shapes=[pltpu.VMEM((tm, D_MODEL), F32)],
        compiler_params=_params(("parallel", "arbitrary")),
        name="ffn_dense",
    )(h2, x1, g2, wg, wu, wd)


def _moe_kernel(h_ref, x_ref, g2_ref, rw_ref, rbias_ref, wg_ref, wu_ref, wd_ref, o_ref, acc_ref, gate_ref):
    e = pl.program_id(1)
    hb = h_ref[...]
    tm = hb.shape[0]
    lane = lax.broadcasted_iota(jnp.int32, (tm, ROUTER_PAD), 1).astype(F32)

    @pl.when(e == 0)
    def _():
        logits = _dot(hb, rw_ref[...]) + rbias_ref[...]
        i1, i2, p1, p2 = _top2(logits, lane)
        gate_ref[...] = jnp.where(lane == i1, p1, 0.0) + jnp.where(lane == i2, p2, 0.0)
        acc_ref[...] = jnp.zeros_like(acc_ref)

    ge = jnp.sum(jnp.where(lane == e.astype(F32), gate_ref[...], 0.0), axis=-1, keepdims=True)
    act = (_silu(_dot(hb, wg_ref[...])) * _dot(hb, wu_ref[...])).astype(BF16)
    acc_ref[...] += ge * _dot(act, wd_ref[...])

    @pl.when(e == pl.num_programs(1) - 1)
    def _():
        o_ref[...] = x_ref[...] + g2_ref[...] * acc_ref[...]


def _moe(h2, x1, g2, rw, rbias, wg, wu, wd, *, tm, per_row_mod):
    n = h2.shape[0]
    mod_spec = _mod_spec(tm, n, g2, per_row_mod)
    fe = D_FF_EXPERT
    return pl.pallas_call(
        _moe_kernel,
        grid=(n // tm, N_EXPERTS),
        in_specs=[
            pl.BlockSpec((tm, D_MODEL), lambda i, e: (i, 0)),
            pl.BlockSpec((tm, D_MODEL), lambda i, e: (i, 0)),
            mod_spec,
            pl.BlockSpec((D_MODEL, ROUTER_PAD), lambda i, e: (0, 0)),
            pl.BlockSpec((1, ROUTER_PAD), lambda i, e: (0, 0)),
            pl.BlockSpec((None, D_MODEL, fe), lambda i, e: (e, 0, 0)),
            pl.BlockSpec((None, D_MODEL, fe), lambda i, e: (e, 0, 0)),
            pl.BlockSpec((None, fe, D_MODEL), lambda i, e: (e, 0, 0)),
        ],
        out_specs=pl.BlockSpec((tm, D_MODEL), lambda i, e: (i, 0)),
        out_shape=jax.ShapeDtypeStruct((n, D_MODEL), F32),
        scratch_shapes=[pltpu.VMEM((tm, D_MODEL), F32), pltpu.VMEM((tm, ROUTER_PAD), F32)],
        compiler_params=_params(("parallel", "arbitrary")),
        name="moe",
    )(h2, x1, g2, rw, rbias, wg, wu, wd)


def _top2(logits, lane):
    lg = jnp.where(lane < N_EXPERTS, logits, -jnp.inf)
    m1 = jnp.max(lg, axis=-1, keepdims=True)
    i1 = jnp.min(jnp.where(lg == m1, lane, float(ROUTER_PAD)), axis=-1, keepdims=True)
    lg2 = jnp.where(lane == i1, -jnp.inf, lg)
    m2 = jnp.max(lg2, axis=-1, keepdims=True)
    i2 = jnp.min(jnp.where(lg2 == m2, lane, float(ROUTER_PAD)), axis=-1, keepdims=True)
    e2 = jnp.exp(m2 - m1)
    p1 = 1.0 / (1.0 + e2)
    return i1, i2, p1, e2 * p1


def _route_pack(logits, tri):
    lane = lax.broadcasted_iota(jnp.int32, logits.shape, 1).astype(F32)
    i1, i2, p1, p2 = _top2(logits, lane)
    oh1 = jnp.where(lane == i1, 1.0, 0.0)
    oh2 = jnp.where(lane == i2, 1.0, 0.0)
    cnt1 = jnp.sum(oh1, axis=0, keepdims=True)
    cnt2 = jnp.sum(oh2, axis=0, keepdims=True)
    rank1 = jnp.sum(_dot(tri, oh1.astype(BF16)) * oh1, axis=-1, keepdims=True)
    rank2 = jnp.sum((_dot(tri, oh2.astype(BF16)) + cnt1) * oh2, axis=-1, keepdims=True)
    packed = jnp.zeros_like(logits)
    for k, val in enumerate((p1, p2, i1, i2, rank1, rank2)):
        packed = jnp.where(lane == float(k), val, packed)
    return packed, cnt1 + cnt2


def _route_tables(route, tile_counts, tm, n_tiles):
    n = route.shape[0]
    e = route[:, 2:4].astype(jnp.int32)
    rank = route[:, 4:6].astype(jnp.int32)
    tile_cnt = tile_counts[:, 0, :N_EXPERTS].astype(jnp.int32)
    cnt = jnp.sum(tile_cnt, axis=0)
    gsz = ((cnt + tm - 1) // tm) * tm
    gend = jnp.cumsum(gsz)
    seg_start = (gend - gsz)[None, :] + jnp.cumsum(tile_cnt, axis=0) - tile_cnt
    seg_of_token = jnp.repeat(seg_start, n // tile_cnt.shape[0], axis=0)
    experts = jnp.arange(N_EXPERTS, dtype=jnp.int32)
    pos = [jnp.sum(jnp.where(e[:, k:k + 1] == experts, seg_of_token, 0), axis=1) + rank[:, k]
           for k in range(2)]
    pos = jnp.concatenate(pos)
    tile_start = jnp.arange(n_tiles, dtype=jnp.int32) * tm
    tile_expert = jnp.sum((tile_start[:, None] >= gend[None, :]).astype(jnp.int32), axis=1)
    tile_expert = jnp.minimum(tile_expert, N_EXPERTS - 1)
    live = (gend[-1] // tm).reshape(1)
    return pos, tile_expert, live, gend - gsz + cnt, gend


def _row_copy_wait(src_hbm, dst, sem, rows):
    pltpu.make_async_copy(src_hbm.at[pl.ds(0, rows)], dst, sem).wait()


def _moe_dispatch_kernel(pos_ref, pad_lo_ref, pad_hi_ref, live_ref, h_ref, xs_hbm, zbuf, sem, zsem):
    i = pl.program_id(0)
    tt = h_ref.shape[0]
    n = tt * pl.num_programs(0)
    tm = zbuf.shape[0]

    base = i * tt
    for r in range(tt):
        row = h_ref.at[pl.ds(r, 1)]
        pltpu.make_async_copy(row, xs_hbm.at[pl.ds(pos_ref[base + r], 1)], sem).start(priority=0)
        pltpu.make_async_copy(row, xs_hbm.at[pl.ds(pos_ref[n + base + r], 1)], sem).start(priority=1)
    for _ in range(2):
        pltpu.make_async_copy(h_ref, xs_hbm.at[pl.ds(0, tt)], sem).wait()

    @pl.when(i == pl.num_programs(0) - 1)
    def _():
        zbuf[...] = jnp.zeros(zbuf.shape, F32)

        def fill_row(p):
            return pltpu.make_async_copy(zbuf.at[pl.ds(0, 1)], xs_hbm.at[pl.ds(p, 1)], zsem)

        def fill_tile(t):
            return pltpu.make_async_copy(zbuf, xs_hbm.at[pl.ds(pl.multiple_of(t * tm, tm), tm)], zsem)

        for e in range(N_EXPERTS):
            lo, hi = pad_lo_ref[e], pad_hi_ref[e]
            lax.fori_loop(lo, hi, lambda p, c: (fill_row(p).start(), c)[1], 0)
            lax.fori_loop(lo, hi, lambda p, c: (fill_row(p).wait(), c)[1], 0)
        lo, hi = live_ref[0], xs_hbm.shape[0] // tm
        lax.fori_loop(lo, hi, lambda t, c: (fill_tile(t).start(), c)[1], 0)
        lax.fori_loop(lo, hi, lambda t, c: (fill_tile(t).wait(), c)[1], 0)


def _moe_dispatch(h2, pos, pad_lo, pad_hi, live, *, tt, tm, n_tiles):
    n = h2.shape[0]
    grid_spec = pltpu.PrefetchScalarGridSpec(
        num_scalar_prefetch=4,
        grid=(n // tt,),
        in_specs=[pl.BlockSpec((tt, D_MODEL), lambda i, *_: (i, 0))],
        out_specs=pl.BlockSpec(memory_space=pl.ANY),
        scratch_shapes=[pltpu.VMEM((tm, D_MODEL), F32), pltpu.SemaphoreType.DMA(()),
                        pltpu.SemaphoreType.DMA(())],
    )
    return pl.pallas_call(
        _moe_dispatch_kernel,
        grid_spec=grid_spec,
        out_shape=jax.ShapeDtypeStruct((n_tiles * tm, D_MODEL), F32),
        compiler_params=_params(("arbitrary",)),
        name="moe_dispatch",
    )(pos, pad_lo, pad_hi, live, h2)


def _moe_expert_kernel(te_ref, live_ref, x_ref, wg_ref, wu_ref, wd_ref, y_ref):
    del te_ref
    is_live = pl.program_id(0) < live_ref[0]

    @pl.when(is_live)
    def _():
        hb = x_ref[...].astype(BF16)
        act = (_silu(_dot(hb, wg_ref[...])) * _dot(hb, wu_ref[...])).astype(BF16)
        y_ref[...] = _dot(act, wd_ref[...])

    @pl.when(jnp.logical_not(is_live))
    def _():
        y_ref[...] = jnp.zeros(y_ref.shape, F32)


def _moe_experts(xs, tile_expert, live, wg, wu, wd, *, tm, n_tiles):
    fe = D_FF_EXPERT

    def in_tile(t, te, live):
        return (jnp.minimum(t, live[0] - 1), 0)

    def out_tile(t, te, live):
        return (t, 0)

    def expert(t, te, live):
        return (te[t], 0, 0)

    grid_spec = pltpu.PrefetchScalarGridSpec(
        num_scalar_prefetch=2,
        grid=(n_tiles,),
        in_specs=[
            pl.BlockSpec((tm, D_MODEL), in_tile),
            pl.BlockSpec((None, D_MODEL, fe), expert),
            pl.BlockSpec((None, D_MODEL, fe), expert),
            pl.BlockSpec((None, fe, D_MODEL), expert),
        ],
        out_specs=pl.BlockSpec((tm, D_MODEL), out_tile),
    )
    return pl.pallas_call(
        _moe_expert_kernel,
        grid_spec=grid_spec,
        out_shape=jax.ShapeDtypeStruct(xs.shape, F32),
        compiler_params=_params(("arbitrary",)),
        name="moe_experts",
    )(tile_expert, live, xs, wg, wu, wd)


def _moe_combine_kernel(pos_ref, x_ref, g2_ref, r_ref, ys_hbm, o_ref, buf, sem):
    i = pl.program_id(0)
    nt = pl.num_programs(0)
    tt = x_ref.shape[0]
    n = nt * tt

    def fetch(tile, sl):
        base = tile * tt
        for r in range(tt):
            for s in range(2):
                row = pos_ref[s * n + base + r]
                pltpu.make_async_copy(ys_hbm.at[pl.ds(row, 1)], buf.at[sl, pl.ds(s * tt + r, 1)],
                                      sem.at[sl]).start(priority=s)

    @pl.when(i == 0)
    def _():
        fetch(0, 0)

    @pl.when(i + 1 < nt)
    def _():
        fetch(i + 1, (i + 1) % 2)

    sl = i % 2
    _row_copy_wait(ys_hbm, buf.at[sl], sem.at[sl], 2 * tt)
    r = r_ref[...]
    f = r[:, 0:1] * buf[sl, pl.ds(0, tt), :] + r[:, 1:2] * buf[sl, pl.ds(tt, tt), :]
    o_ref[...] = x_ref[...] + g2_ref[...] * f


def _moe_combine(x1, g2, route, pos, ys, *, tt):
    n = x1.shape[0]
    tiles_per_mod = (n // g2.shape[0]) // tt
    grid_spec = pltpu.PrefetchScalarGridSpec(
        num_scalar_prefetch=1,
        grid=(n // tt,),
        in_specs=[
            pl.BlockSpec((tt, D_MODEL), lambda i, pos: (i, 0)),
            pl.BlockSpec((None, 1, D_MODEL), lambda i, pos: (i // tiles_per_mod, 0, 0)),
            pl.BlockSpec((tt, ROUTER_PAD), lambda i, pos: (i, 0)),
            pl.BlockSpec(memory_space=pl.ANY),
        ],
        out_specs=pl.BlockSpec((tt, D_MODEL), lambda i, pos: (i, 0)),
        scratch_shapes=[pltpu.VMEM((2, 2 * tt, D_MODEL), F32), pltpu.SemaphoreType.DMA((2,))],
    )
    return pl.pallas_call(
        _moe_combine_kernel,
        grid_spec=grid_spec,
        out_shape=jax.ShapeDtypeStruct((n, D_MODEL), F32),
        compiler_params=_params(("arbitrary",)),
        name="moe_combine",
    )(pos, x1, g2, route, ys)


def _moe_routed(h2, x1, g2, route, tile_counts, wg, wu, wd, *, tm, tt):
    n = h2.shape[0]
    n_tiles = (2 * n) // tm + N_EXPERTS
    pos, tile_expert, live, pad_lo, pad_hi = _route_tables(route, tile_counts, tm, n_tiles)
    xs = _moe_dispatch(h2, pos, pad_lo, pad_hi, live, tt=tt, tm=tm, n_tiles=n_tiles)
    ys = _moe_experts(xs, tile_expert, live, wg, wu, wd, tm=tm, n_tiles=n_tiles)
    return _moe_combine(x1, g2, route, pos, ys, tt=tt)


def _head_perm():
    idx = []
    for g in range(GROUP):
        for kv in range(N_KV):
            h = kv * GROUP + g
            idx.extend(range(h * HEAD_DIM, (h + 1) * HEAD_DIM))
    return np.asarray(idx, np.int32)


def _relayout_w_in(w):
    pts = np.cumsum([0, Q_W, KV_W, KV_W, GK_W, GK_W, GV_W, GATE_RANK, GV_W, D_MODEL, D_MODEL])
    qa, ka, va, qb, kb, vb, ga, rb, gta, gtb = [w[:, pts[i]:pts[i + 1]] for i in range(10)]
    qa = qa[:, _head_perm()]
    ga = jnp.pad(ga, ((0, 0), (0, GATE_PAD - GATE_RANK)))
    return jnp.concatenate([qa, ka, va, qb, kb, vb, rb, gta, gtb, ga], axis=1).astype(BF16)


def _expand_wpa(wpa_perm):
    src = wpa_perm.reshape(GROUP, N_KV, 1, HEAD_DIM, D_MODEL)
    own = jnp.asarray(np.eye(N_KV, dtype=bool).reshape(1, N_KV, N_KV, 1, 1))
    return jnp.where(own, src, jnp.zeros_like(src)).reshape(N_HEADS * KV_W, D_MODEL)


def _sample_lane_consts(sinks):
    j = np.arange(N_HEADS)
    head = (j % N_KV) * GROUP + j // N_KV
    slopes = np.zeros((1, LANES), np.float32)
    slopes[0, :N_HEADS] = 2.0 ** (-8.0 * (head + 1) / N_HEADS)
    sink_row = jnp.zeros((1, LANES), F32).at[0, :N_HEADS].set(sinks[head])
    seg = np.zeros((GROUP, KV_W, LANES), np.float32)
    for g in range(GROUP):
        for kv in range(N_KV):
            seg[g, kv * HEAD_DIM:(kv + 1) * HEAD_DIM, g * N_KV + kv] = 1.0
    return jnp.asarray(seg, BF16), jnp.asarray(slopes), sink_row


def kernel(x_prompt, x_sample, cache_k, cache_v, state_gla, c_prompt, c_sample, ada_w, ada_b, norm1_g, norm2_g, w_in, q_norm_g, k_norm_g, attn_sinks, gla_wa2, gla_ba, gla_norm_g, w_branch_a, w_branch_b, w_out, ffn_w_gate, ffn_w_up, ffn_w_down, router_w, router_b, moe_w_gate, moe_w_up, moe_w_down):
    n_p = BATCH * SEQ
    xp = x_prompt.reshape(n_p, D_MODEL)
    xs = x_sample.reshape(DEC_BATCH, D_MODEL)

    c_rows = BATCH + DEC_BATCH
    c_pad = -c_rows % 8
    c_all = jnp.pad(jnp.concatenate([c_prompt, c_sample], axis=0), ((0, c_pad), (0, 0)))
    mod = _ada(c_all, ada_w, ada_b)

    bd = jnp.asarray(np.kron(np.eye(N_HEADS), np.ones((HEAD_DIM, HEAD_DIM))), BF16)
    ut = jnp.asarray(_gla_cum_matrix(), BF16)
    perm = _head_perm()

    kp_l, vp_l, sp_l, ks_l, vs_l, ss_l = [], [], [], [], [], []
    for l in range(DEPTH):
        m = mod[l].reshape(c_rows + c_pad, 6, D_MODEL)
        mod_p = [m[:BATCH, i].reshape(BATCH, 1, D_MODEL) for i in range(6)]
        mod_s = [m[BATCH:c_rows, i] for i in range(6)]

        w = _relayout_w_in(w_in[l])
        qg = (jnp.tile(q_norm_g[l], N_HEADS) * (HEAD_DIM ** -0.5)).reshape(1, Q_W)
        kg = jnp.tile(k_norm_g[l], N_KV).reshape(1, KV_W)
        wa2 = jnp.pad(gla_wa2[l], ((0, GATE_PAD - GATE_RANK), (0, 0))).astype(BF16)
        ba = gla_ba[l].reshape(1, GK_W)
        n1 = norm1_g[l].reshape(1, D_MODEL)
        n2 = norm2_g[l].reshape(1, D_MODEL)
        gn = gla_norm_g[l].reshape(1, GLA_DV)
        wpa = w_branch_a[l][perm].astype(BF16)
        wpa_x = _expand_wpa(wpa)
        wpb = w_branch_b[l].astype(BF16)
        wo = w_out[l].astype(BF16)
        sinks_perm = attn_sinks[l]
        seg, slope_row, sink_row = _sample_lane_consts(attn_sinks[l])

        q, k, v, gq, gk, gv, la, rbs, sga, sgb = _mixin(
            xp, mod_p[0], mod_p[1], n1, w, bd, qg, kg, wa2, ba, tm=512, per_row_mod=False)
        a_out = _swa_prompt(sinks_perm, q, k, v)
        g_out, s_fin = _gla_prompt(gq, gk, la, gv, rbs, gn, ut)
        is_moe = l % 2 == 1
        if is_moe:
            rw = jnp.pad(router_w[l // 2], ((0, 0), (0, ROUTER_PAD - N_EXPERTS))).astype(BF16)
            rbias = jnp.pad(router_b[l // 2], (0, ROUTER_PAD - N_EXPERTS)).reshape(1, ROUTER_PAD)
        merged = _merge(xp, a_out, g_out, sga, sgb, wpa, wpb, wo, mod_p[2], mod_p[3], mod_p[4], n2,
                        tm=512, per_row_mod=False, h2_dtype=F32 if is_moe else BF16,
                        router=(rw, rbias) if is_moe else None)
        x1, h2 = merged[0], merged[1]
        kp_l.append(k.reshape(BATCH, SEQ, N_KV, HEAD_DIM)[:, SEQ - WINDOW:])
        vp_l.append(v.reshape(BATCH, SEQ, N_KV, HEAD_DIM)[:, SEQ - WINDOW:])
        sp_l.append(s_fin.reshape(BATCH, GLA_HEADS, GLA_DK, GLA_DV))

        qs, ksn, vsn, gqs, gks, gvs, las, rbss, sgas, sgbs = _mixin(
            xs, mod_s[0], mod_s[1], n1, w, bd, qg, kg, wa2, ba, tm=DEC_BATCH, per_row_mod=True)
        a_s, nk, nv = _swa_sample(qs, ksn, vsn, cache_k[l].reshape(DEC_BATCH, WINDOW, KV_W),
                                  cache_v[l].reshape(DEC_BATCH, WINDOW, KV_W), seg, slope_row, sink_row)
        g_s, s_new = _gla_sample(gqs, gks, las, gvs, rbss, gn, state_gla[l].reshape(DEC_BATCH, GK_W, GLA_DV))
        x1s, h2s = _merge(xs, a_s.reshape(DEC_BATCH, N_HEADS * KV_W), g_s, sgas, sgbs, wpa_x, wpb, wo,
                          mod_s[2], mod_s[3], mod_s[4], n2, tm=DEC_BATCH, per_row_mod=True)
        ks_l.append(nk.reshape(DEC_BATCH, WINDOW, N_KV, HEAD_DIM))
        vs_l.append(nv.reshape(DEC_BATCH, WINDOW, N_KV, HEAD_DIM))
        ss_l.append(s_new.reshape(DEC_BATCH, GLA_HEADS, GLA_DK, GLA_DV))

        i = l // 2
        if not is_moe:
            wg, wu, wd = ffn_w_gate[i].astype(BF16), ffn_w_up[i].astype(BF16), ffn_w_down[i].astype(BF16)
            xp = _ffn(h2, x1, mod_p[5], wg, wu, wd, tm=512, tf=D_FF // 2, per_row_mod=False)
            xs = _ffn(h2s, x1s, mod_s[5], wg, wu, wd, tm=DEC_BATCH, tf=D_FF // 2, per_row_mod=True)
        else:
            wg, wu, wd = moe_w_gate[i].astype(BF16), moe_w_up[i].astype(BF16), moe_w_down[i].astype(BF16)
            xp = _moe_routed(h2, x1, mod_p[5], merged[2], merged[3], wg, wu, wd, tm=512, tt=512)
            xs = _moe(h2s, x1s, mod_s[5], rw, rbias, wg, wu, wd, tm=DEC_BATCH, per_row_mod=True)

    return (xp.reshape(BATCH, SEQ, D_MODEL), xs.reshape(DEC_BATCH, 1, D_MODEL),
            jnp.stack(kp_l), jnp.stack(vp_l), jnp.stack(sp_l),
            jnp.stack(ks_l), jnp.stack(vs_l), jnp.stack(ss_l))
```

```python
import functools

import jax
import jax.numpy as jnp
import numpy as np
from jax import lax
from jax.experimental import pallas as pl
from jax.experimental.pallas import tpu as pltpu

D_MODEL = 1024
BATCH = 4
SEQ = 4096
DEPTH = 2
DEC_BATCH = 128
N_HEADS = 8
N_KV = 2
HEAD_DIM = 64
GROUP = N_HEADS // N_KV
WINDOW = 128
GLA_HEADS = 4
GLA_DK = 64
GLA_DV = 128
GATE_RANK = 16
GATE_TAU = 16.0
D_FF = 2816
N_EXPERTS = 8
D_FF_EXPERT = 1408
EPS = 1e-6

Q_W = N_HEADS * HEAD_DIM
KV_W = N_KV * HEAD_DIM
GK_W = GLA_HEADS * GLA_DK
GV_W = GLA_HEADS * GLA_DV

LANES = 128
GATE_PAD = LANES
ROUTER_PAD = LANES
VMEM_LIMIT = 56 * 1024 * 1024

F32 = jnp.float32
BF16 = jnp.bfloat16

_C_Q = 0
_C_K = _C_Q + Q_W
_C_V = _C_K + KV_W
_C_GQ = _C_V + KV_W
_C_GK = _C_GQ + GK_W
_C_GV = _C_GK + GK_W
_C_RB = _C_GV + GV_W
_C_GA = _C_RB + GV_W
_C_GB = _C_GA + D_MODEL
_C_LR = _C_GB + D_MODEL
PROJ_PAD = _C_LR + GATE_PAD

GLA_CHUNK = 128
GLA_LEVELS = 7
GLA_MXU_LEVELS = 3


def _params(sem, vmem=VMEM_LIMIT):
    return pltpu.CompilerParams(dimension_semantics=sem, vmem_limit_bytes=vmem)


def _dot(a, b):
    return jnp.dot(a, b, preferred_element_type=F32)


def _dot_nt(a, b):
    return lax.dot_general(a, b, (((1,), (1,)), ((), ())), preferred_element_type=F32)


def _dot_tn(a, b):
    return lax.dot_general(a, b, (((0,), (0,)), ((), ())), preferred_element_type=F32)


def _sigmoid(x):
    return 1.0 / (1.0 + jnp.exp(-x))


def _silu(x):
    return x * _sigmoid(x)


def _ada_kernel(c_ref, w_ref, b_ref, o_ref):
    c = c_ref[...]
    o_ref[...] = _dot(_silu(c).astype(BF16), w_ref[...].astype(BF16)) + b_ref[...]


def _ada(c_all, ada_w, ada_b):
    rows = c_all.shape[0]
    tn = 1024
    return pl.pallas_call(
        _ada_kernel,
        grid=(DEPTH, 6 * D_MODEL // tn),
        in_specs=[
            pl.BlockSpec((rows, D_MODEL), lambda l, j: (0, 0)),
            pl.BlockSpec((None, D_MODEL, tn), lambda l, j: (l, 0, j)),
            pl.BlockSpec((None, 1, tn), lambda l, j: (l, 0, j)),
        ],
        out_specs=pl.BlockSpec((None, rows, tn), lambda l, j: (l, 0, j)),
        out_shape=jax.ShapeDtypeStruct((DEPTH, rows, 6 * D_MODEL), F32),
        compiler_params=_params(("parallel", "parallel")),
        name="ada_mod",
    )(c_all, ada_w, ada_b.reshape(DEPTH, 1, 6 * D_MODEL))


def _mixin_kernel(x_ref, sh_ref, sc_ref, n1_ref, w_ref, bd_ref, qg_ref, kg_ref, wa2_ref, ba_ref,
                  q_ref, k_ref, v_ref, gq_ref, gk_ref, gv_ref, la_ref, rb_ref, sga_ref, sgb_ref):
    x = x_ref[...]
    ms = jnp.mean(x * x, axis=-1, keepdims=True)
    h = x * lax.rsqrt(ms + EPS) * n1_ref[...]
    h = h * (1.0 + sc_ref[...]) + sh_ref[...]
    hb = h.astype(BF16)

    def proj(a, b):
        return _dot(hb, w_ref[:, a:b])

    q = proj(_C_Q, _C_K)
    ssq = _dot((q * q).astype(BF16), bd_ref[...])
    q_ref[...] = (q * lax.rsqrt(ssq * (1.0 / HEAD_DIM) + EPS) * qg_ref[...]).astype(BF16)
    k = proj(_C_K, _C_V)
    ssk = _dot((k * k).astype(BF16), bd_ref[0:KV_W, 0:KV_W])
    k_ref[...] = k * lax.rsqrt(ssk * (1.0 / HEAD_DIM) + EPS) * kg_ref[...]
    v_ref[...] = proj(_C_V, _C_GQ)
    gq_ref[...] = proj(_C_GQ, _C_GK) * (GLA_DK ** -0.5)
    gk_ref[...] = proj(_C_GK, _C_GV)
    gv_ref[...] = proj(_C_GV, _C_RB)
    rb_ref[...] = _silu(proj(_C_RB, _C_GA)).astype(BF16)
    sga_ref[...] = _sigmoid(proj(_C_GA, _C_GB)).astype(BF16)
    sgb_ref[...] = _sigmoid(proj(_C_GB, _C_LR)).astype(BF16)
    ga = proj(_C_LR, PROJ_PAD)
    xg = _dot(ga.astype(BF16), wa2_ref[...]) + ba_ref[...]
    la_ref[...] = (jnp.minimum(xg, 0.0) - jnp.log1p(jnp.exp(-jnp.abs(xg)))) * (1.0 / GATE_TAU)


def _mixin(x, sh, sc, n1, w, bd, qg, kg, wa2, ba, *, tm, per_row_mod):
    n = x.shape[0]
    nt = n // tm
    mod_spec = _mod_spec(tm, n, sh, per_row_mod)

    def row(i):
        return (i, 0)

    def const(shape):
        return pl.BlockSpec(shape, lambda i: (0,) * len(shape), pipeline_mode=pl.Buffered(1))

    def out(width, dtype):
        return pl.BlockSpec((tm, width), row), jax.ShapeDtypeStruct((n, width), dtype)

    outs = [out(Q_W, BF16), out(KV_W, F32), out(KV_W, F32), out(GK_W, F32), out(GK_W, F32),
            out(GV_W, F32), out(GK_W, F32), out(GV_W, BF16), out(D_MODEL, BF16), out(D_MODEL, BF16)]
    return pl.pallas_call(
        _mixin_kernel,
        grid=(nt,),
        in_specs=[
            pl.BlockSpec((tm, D_MODEL), row), mod_spec, mod_spec, const((1, D_MODEL)),
            const((D_MODEL, PROJ_PAD)), const((Q_W, Q_W)), const((1, Q_W)), const((1, KV_W)),
            const((GATE_PAD, GK_W)), const((1, GK_W)),
        ],
        out_specs=[o[0] for o in outs],
        out_shape=[o[1] for o in outs],
        compiler_params=_params(("parallel",)),
        name="mixer_in",
    )(x, sh, sc, n1, w, bd, qg, kg, wa2, ba)


SWA_BLOCKS = 8


def _head_slope(h):
    return float(2.0 ** (-8.0 * (h + 1) / N_HEADS))


def _swa_prompt_kernel(sink_ref, q_ref, kp_ref, kc_ref, vp_ref, vc_ref, o_ref):
    n = pl.program_id(1)
    blk = WINDOW
    kall = jnp.concatenate([kp_ref[...], kc_ref[...]], axis=0).astype(BF16)
    vall = jnp.concatenate([vp_ref[...], vc_ref[...]], axis=0).astype(BF16)
    row = lax.broadcasted_iota(jnp.int32, (blk, 2 * blk), 0)
    col = lax.broadcasted_iota(jnp.int32, (blk, 2 * blk), 1)
    dist = row + blk - col
    in_window = (dist >= 0) & (dist <= WINDOW)
    first_key = jnp.where(n > 0, 0, blk)
    distf = dist.astype(F32)
    klane = lax.broadcasted_iota(jnp.int32, kall.shape, 1)
    kall_kv = [jnp.where((klane // HEAD_DIM) == kv, kall, jnp.zeros_like(kall)) for kv in range(N_KV)]
    olane = lax.broadcasted_iota(jnp.int32, (blk, KV_W), 1)
    q = q_ref[...]
    for j in range(SWA_BLOCKS):
        valid = in_window & (col >= first_key) if j == 0 else in_window
        vv = vall[j * blk:(j + 2) * blk]
        outs = []
        for g in range(GROUP):
            qp = q[j * blk:(j + 1) * blk, g * KV_W:(g + 1) * KV_W]
            pair = []
            for kv in range(N_KV):
                h = kv * GROUP + g
                s = _dot_nt(qp, kall_kv[kv][j * blk:(j + 2) * blk]) - _head_slope(h) * distf
                s = jnp.where(valid, s, -jnp.inf)
                sink = sink_ref[h]
                m = jnp.maximum(jnp.max(s, axis=-1, keepdims=True), sink)
                p = jnp.exp(s - m)
                den = jnp.sum(p, axis=-1, keepdims=True) + jnp.exp(sink - m)
                pair.append(_dot(p.astype(BF16), vv) * (1.0 / den))
            outs.append(jnp.where(olane < HEAD_DIM, pair[0], pair[1]))
        o_ref[j * blk:(j + 1) * blk, :] = jnp.concatenate(outs, axis=-1).astype(BF16)


def _swa_prompt(sinks, q, k, v):
    nb = SEQ // WINDOW
    steps = nb // SWA_BLOCKS
    tq = SWA_BLOCKS * WINDOW

    def cur(b, n):
        return (b * steps + n, 0)

    def prev(b, n):
        return (b * nb + jnp.maximum(n * SWA_BLOCKS - 1, 0), 0)

    return pl.pallas_call(
        _swa_prompt_kernel,
        grid=(BATCH, steps),
        in_specs=[
            pl.BlockSpec(memory_space=pltpu.SMEM),
            pl.BlockSpec((tq, Q_W), cur),
            pl.BlockSpec((WINDOW, KV_W), prev), pl.BlockSpec((tq, KV_W), cur),
            pl.BlockSpec((WINDOW, KV_W), prev), pl.BlockSpec((tq, KV_W), cur),
        ],
        out_specs=pl.BlockSpec((tq, Q_W), cur),
        out_shape=jax.ShapeDtypeStruct((BATCH * SEQ, Q_W), BF16),
        compiler_params=_params(("parallel", "parallel")),
        name="swa_prompt",
    )(sinks, q, k, k, v, v)


SAMPLE_TB = 8
CACHE_ROWS = WINDOW * N_KV


def _sample_head_of_row():
    j = np.arange(N_HEADS)
    return (j % N_KV) * GROUP + j // N_KV


def _swa_sample_bias():
    j = np.arange(N_HEADS)[:, None]
    c = np.arange(CACHE_ROWS)[None, :]
    slope = 2.0 ** (-8.0 * (_sample_head_of_row()[:, None] + 1) / N_HEADS)
    bias = -slope * (WINDOW - c // N_KV)
    return np.where(c % N_KV == j % N_KV, bias, -np.inf).astype(np.float32)


def _swa_sample_kernel(q_ref, kn_ref, vn_ref, ck_ref, cv_ref, bias_ref, sk_ref, o_ref, ok_ref, ov_ref):
    rows = CACHE_ROWS
    sink = sk_ref[...][:, 0:1]
    q = q_ref[...]
    kn = kn_ref[...]
    vn = vn_ref[...]
    kc = ck_ref[...]
    vc = cv_ref[...]
    kn8 = jnp.concatenate([kn] * GROUP, axis=1)
    vn8 = jnp.concatenate([vn] * GROUP, axis=1)
    s = lax.dot_general(q.astype(BF16), kc.astype(BF16), (((2,), (2,)), ((0,), (0,))),
                        preferred_element_type=F32) + bias_ref[...]
    s_new = jnp.sum(q * kn8, axis=-1, keepdims=True)
    m = jnp.maximum(jnp.maximum(jnp.max(s, axis=-1, keepdims=True), s_new), sink)
    p = jnp.exp(s - m)
    p_new = jnp.exp(s_new - m)
    den = jnp.sum(p, axis=-1, keepdims=True) + p_new + jnp.exp(sink - m)
    o = lax.dot_general(p.astype(BF16), vc.astype(BF16), (((2,), (1,)), ((0,), (0,))),
                        preferred_element_type=F32) + p_new * vn8
    o_ref[...] = o * (1.0 / den)
    ok_ref[:, pl.ds(0, rows - N_KV), :] = kc[:, N_KV:, :]
    ok_ref[:, pl.ds(rows - N_KV, N_KV), :] = kn
    ov_ref[:, pl.ds(0, rows - N_KV), :] = vc[:, N_KV:, :]
    ov_ref[:, pl.ds(rows - N_KV, N_KV), :] = vn


def _swa_sample(q, kn, vn, ck, cv, bias, sinks):
    tb = 2 * SAMPLE_TB
    nb = DEC_BATCH // tb

    def const(shape):
        return pl.BlockSpec(shape, lambda i: (0,) * len(shape))

    def per_seq(*dims):
        return pl.BlockSpec((tb,) + dims, lambda i: (i, 0, 0))

    return pl.pallas_call(
        _swa_sample_kernel,
        grid=(nb,),
        in_specs=[per_seq(N_HEADS, HEAD_DIM), per_seq(N_KV, HEAD_DIM), per_seq(N_KV, HEAD_DIM),
                  per_seq(CACHE_ROWS, HEAD_DIM), per_seq(CACHE_ROWS, HEAD_DIM),
                  const((N_HEADS, CACHE_ROWS)), const((N_HEADS, LANES))],
        out_specs=[per_seq(N_HEADS, HEAD_DIM), per_seq(CACHE_ROWS, HEAD_DIM), per_seq(CACHE_ROWS, HEAD_DIM)],
        out_shape=[
            jax.ShapeDtypeStruct((DEC_BATCH, N_HEADS, HEAD_DIM), F32),
            jax.ShapeDtypeStruct((DEC_BATCH, CACHE_ROWS, HEAD_DIM), F32),
            jax.ShapeDtypeStruct((DEC_BATCH, CACHE_ROWS, HEAD_DIM), F32),
        ],
        compiler_params=_params(("parallel",)),
        name="swa_sample",
    )(q, kn, vn, ck, cv, bias, sinks)


def _gla_cum_matrix():
    c = GLA_CHUNK
    tri = np.tril(np.ones((c, c), np.float32))
    i = np.arange(c)
    blocks = []
    for lvl in range(GLA_MXU_LEVELS):
        half = 1 << lvl
        mid = (i // (2 * half)) * (2 * half) + half - 1
        blocks.append(tri - tri[mid])
    blocks.append(tri)
    return np.concatenate(blocks, axis=0)


def _split3(x):
    hi = x.astype(BF16)
    r1 = x - hi.astype(F32)
    mid = r1.astype(BF16)
    lo = (r1 - mid.astype(F32)).astype(BF16)
    return hi, mid, lo


def _gla_prompt_kernel(q_ref, k_ref, la_ref, v_ref, rb_ref, gn_ref, ut_ref, o_ref, s_ref, st_ref):
    c = pl.program_id(1)
    cl = GLA_CHUNK

    @pl.when(c == 0)
    def _():
        st_ref[0] = jnp.zeros(st_ref.shape[1:], F32)

    ut = ut_ref[...]
    hi, mid, lo = _split3(la_ref[...])
    tall = _dot(ut, hi) + _dot(ut, mid) + _dot(ut, lo)
    q = q_ref[...]
    k = k_ref[...]
    lane = lax.broadcasted_iota(jnp.int32, (cl, GK_W), 1)
    row = lax.broadcasted_iota(jnp.int32, (cl, GK_W), 0)
    head_of_lane = lane // GLA_DK
    ri = lax.broadcasted_iota(jnp.int32, (GLA_HEADS * cl, cl), 0) & (cl - 1)
    ci = lax.broadcasted_iota(jnp.int32, (GLA_HEADS * cl, cl), 1)
    attn_all = jnp.zeros((GLA_HEADS * cl, cl), F32)
    cum = tall[GLA_MXU_LEVELS * cl:(GLA_MXU_LEVELS + 1) * cl]
    for lvl in range(GLA_LEVELS + 1):
        if lvl < GLA_LEVELS:
            if lvl < GLA_MXU_LEVELS:
                t_lvl = tall[lvl * cl:(lvl + 1) * cl]
            else:
                half = 1 << lvl
                mids = [jnp.broadcast_to(cum[b0 + half - 1:b0 + half, :], (2 * half, GK_W))
                        for b0 in range(0, cl, 2 * half)]
                t_lvl = cum - (mids[0] if len(mids) == 1 else jnp.concatenate(mids, axis=0))
            e = jnp.exp(-jnp.abs(t_lvl))
            upper = ((row >> lvl) & 1) == 1
            qt = jnp.where(upper, q * e, 0.0).astype(BF16)
            kt = jnp.where(upper, 0.0, k * e).astype(BF16)
            pair_ok = (ri >> (lvl + 1)) == (ci >> (lvl + 1))
        else:
            qt = q.astype(BF16)
            kt = k.astype(BF16)
            pair_ok = ri == ci
        q_heads = jnp.concatenate(
            [jnp.where(head_of_lane == h, qt, jnp.zeros_like(qt)) for h in range(GLA_HEADS)], axis=0)
        attn_all = attn_all + jnp.where(pair_ok, _dot_nt(q_heads, kt), 0.0)
    attn = [attn_all[h * cl:(h + 1) * cl] for h in range(GLA_HEADS)]

    last = cum[cl - 1:cl, :]
    qe = (q * jnp.exp(cum)).astype(BF16)
    kd = (k * jnp.exp(last - cum)).astype(BF16)
    ones = jnp.ones((cl, LANES), BF16)
    dec = jnp.exp(_dot_tn(hi, ones) + _dot_tn(mid, ones) + _dot_tn(lo, ones))
    v = v_ref[...]
    rb = rb_ref[...]
    outs = []
    for h in range(GLA_HEADS):
        vh = v[:, h * GLA_DV:(h + 1) * GLA_DV].astype(BF16)
        st = st_ref[c % 2, h]
        o = _dot(attn[h].astype(BF16), vh) + _dot(qe, st.astype(BF16))
        kdh = jnp.where(head_of_lane == h, kd, jnp.zeros_like(kd))
        st_ref[(c + 1) % 2, h] = st * dec + _dot_tn(kdh, vh)
        ms = jnp.mean(o * o, axis=-1, keepdims=True)
        g = o * lax.rsqrt(ms + EPS) * gn_ref[...]
        outs.append(g * rb[:, h * GLA_DV:(h + 1) * GLA_DV].astype(F32))
    o_ref[...] = jnp.concatenate(outs, axis=-1).astype(BF16)

    @pl.when(c == pl.num_programs(1) - 1)
    def _():
        fin = (SEQ // GLA_CHUNK) % 2
        s_ref[...] = st_ref[fin, 0] + st_ref[fin, 1] + st_ref[fin, 2] + st_ref[fin, 3]


def _gla_prompt(gq, gk, la, gv, rbs, gn, ut):
    cl = GLA_CHUNK
    nc = SEQ // cl

    def row(b, c):
        return (b * nc + c, 0)

    return pl.pallas_call(
        _gla_prompt_kernel,
        grid=(BATCH, nc),
        in_specs=[
            pl.BlockSpec((cl, GK_W), row), pl.BlockSpec((cl, GK_W), row), pl.BlockSpec((cl, GK_W), row),
            pl.BlockSpec((cl, GV_W), row), pl.BlockSpec((cl, GV_W), row),
            pl.BlockSpec((1, GLA_DV), lambda b, c: (0, 0)),
            pl.BlockSpec(((GLA_MXU_LEVELS + 1) * cl, cl), lambda b, c: (0, 0)),
        ],
        out_specs=[
            pl.BlockSpec((cl, GV_W), row),
            pl.BlockSpec((None, GK_W, GLA_DV), lambda b, c: (b, 0, 0)),
        ],
        out_shape=[
            jax.ShapeDtypeStruct((BATCH * SEQ, GV_W), BF16),
            jax.ShapeDtypeStruct((BATCH, GK_W, GLA_DV), F32),
        ],
        scratch_shapes=[pltpu.VMEM((2, GLA_HEADS, GK_W, GLA_DV), F32)],
        compiler_params=_params(("parallel", "arbitrary")),
        name="gla_prompt",
    )(gq, gk, la, gv, rbs, gn, ut)


def _gla_sample_kernel(q_ref, k_ref, la_ref, v_ref, rb_ref, gn_ref, s_ref, o_ref, so_ref):
    tb = SAMPLE_TB
    dec = jnp.exp(la_ref[...])
    pieces = []
    for x in (dec, k_ref[...], q_ref[...]):
        hi, mid, lo = _split3(x)
        stacked = jnp.concatenate(
            [hi.astype(F32), mid.astype(F32), lo.astype(F32), jnp.zeros_like(x)], axis=0)
        pieces.append(stacked.astype(BF16))
    prow = lax.broadcasted_iota(jnp.int32, (4 * tb, LANES), 0)
    rb = rb_ref[...].astype(F32)
    v = v_ref[...]
    for bi in range(tb):
        sel = jnp.where((prow % tb) == bi, 1.0, 0.0).astype(BF16)
        a_col, k_col, q_col = [_dot_tn(p, sel) for p in pieces]
        for h in range(GLA_HEADS):
            rs = slice(h * GLA_DK, (h + 1) * GLA_DK)
            vs = slice(h * GLA_DV, (h + 1) * GLA_DV)
            s_new = a_col[rs] * s_ref[bi, rs, :] + k_col[rs] * v[bi:bi + 1, vs]
            so_ref[bi, rs, :] = s_new
            o = jnp.sum(q_col[rs] * s_new, axis=0, keepdims=True)
            ms = jnp.mean(o * o, axis=-1, keepdims=True)
            g = o * lax.rsqrt(ms + EPS) * gn_ref[...]
            o_ref[bi:bi + 1, vs] = g * rb[bi:bi + 1, vs]


def _gla_sample(gq, gk, la, gv, rbs, gn, state):
    tb = SAMPLE_TB
    nb = DEC_BATCH // tb

    def row(w):
        return pl.BlockSpec((tb, w), lambda i: (i, 0))

    st_spec = pl.BlockSpec((tb, GK_W, GLA_DV), lambda i: (i, 0, 0))
    return pl.pallas_call(
        _gla_sample_kernel,
        grid=(nb,),
        in_specs=[row(GK_W), row(GK_W), row(GK_W), row(GV_W), row(GV_W),
                  pl.BlockSpec((1, GLA_DV), lambda i: (0, 0)), st_spec],
        out_specs=[row(GV_W), st_spec],
        out_shape=[
            jax.ShapeDtypeStruct((DEC_BATCH, GV_W), F32),
            jax.ShapeDtypeStruct((DEC_BATCH, GK_W, GLA_DV), F32),
        ],
        compiler_params=_params(("parallel",)),
        name="gla_sample",
    )(gq, gk, la, gv, rbs, gn, state)


def _merge_kernel(x_ref, a_ref, g_ref, sga_ref, sgb_ref, wpa_ref, wpb_ref, wo_ref,
                  g1_ref, sh_ref, sc_ref, n2_ref, *rest):
    with_router = len(rest) == 8
    x1_ref, h2_ref = rest[3:5] if with_router else rest
    ya = _dot(a_ref[...].astype(BF16), wpa_ref[...])
    yb = _dot(g_ref[...].astype(BF16), wpb_ref[...])
    merged = sga_ref[...].astype(F32) * ya + sgb_ref[...].astype(F32) * yb
    mix = _dot(merged.astype(BF16), wo_ref[...])
    x1 = x_ref[...] + g1_ref[...] * mix
    x1_ref[...] = x1
    ms = jnp.mean(x1 * x1, axis=-1, keepdims=True)
    h = x1 * lax.rsqrt(ms + EPS) * n2_ref[...]
    h2 = h * (1.0 + sc_ref[...]) + sh_ref[...]
    h2_ref[...] = h2.astype(h2_ref.dtype)
    if with_router:
        rw_ref, rbias_ref, tri_ref = rest[0:3]
        route_ref, route_t_ref, cnt_ref = rest[5:8]
        logits = _dot(h2.astype(BF16), rw_ref[...]) + rbias_ref[...]
        packed, counts = _route_pack(logits, tri_ref[...])
        route_ref[...] = packed
        route_t_ref[...] = packed.T[0:route_t_ref.shape[0], :]
        cnt_ref[...] = jnp.broadcast_to(counts, cnt_ref.shape)


def _mod_spec(tm, n, mod, per_row_mod):
    if per_row_mod:
        return pl.BlockSpec((tm, D_MODEL), lambda i, *_: (i, 0))
    tiles_per_mod = (n // mod.shape[0]) // tm
    return pl.BlockSpec((None, 1, D_MODEL), lambda i, *_: (i // tiles_per_mod, 0, 0))


def _merge(x, a, g, sga, sgb, wpa, wpb, wo, g1, sh2, sc2, n2, *, tm, per_row_mod, h2_dtype=BF16,
           router=None):
    n = x.shape[0]
    mod_spec = _mod_spec(tm, n, g1, per_row_mod)

    def row(w):
        return pl.BlockSpec((tm, w), lambda i: (i, 0))

    def const(shape):
        return pl.BlockSpec(shape, lambda i: (0,) * len(shape))

    in_specs = [row(D_MODEL), row(a.shape[1]), row(GV_W), row(D_MODEL), row(D_MODEL),
                const(wpa.shape), const(wpb.shape), const(wo.shape),
                mod_spec, mod_spec, mod_spec, const((1, D_MODEL))]
    out_specs = [row(D_MODEL), row(D_MODEL)]
    out_shape = [jax.ShapeDtypeStruct((n, D_MODEL), F32), jax.ShapeDtypeStruct((n, D_MODEL), h2_dtype)]
    args = [x, a, g, sga, sgb, wpa, wpb, wo, g1, sh2, sc2, n2]
    if router is not None:
        tri = jnp.asarray(np.tril(np.ones((tm, tm), np.float32), -1), BF16)
        in_specs += [const(router[0].shape), const(router[1].shape), const((tm, tm))]
        out_specs += [row(ROUTER_PAD), pl.BlockSpec((8, tm), lambda i: (0, i)),
                      pl.BlockSpec((None, 8, ROUTER_PAD), lambda i: (i, 0, 0))]
        out_shape += [jax.ShapeDtypeStruct((n, ROUTER_PAD), F32), jax.ShapeDtypeStruct((8, n), F32),
                      jax.ShapeDtypeStruct((n // tm, 8, ROUTER_PAD), F32)]
        args += [router[0], router[1], tri]
    return pl.pallas_call(
        _merge_kernel,
        grid=(n // tm,),
        in_specs=in_specs,
        out_specs=out_specs,
        out_shape=out_shape,
        compiler_params=_params(("parallel",)),
        name="merge_out",
    )(*args)


def _ffn_kernel(h_ref, x_ref, g2_ref, wg_ref, wu_ref, wd_ref, o_ref, acc_ref):
    f = pl.program_id(1)
    hb = h_ref[...]
    act = (_silu(_dot(hb, wg_ref[...])) * _dot(hb, wu_ref[...])).astype(BF16)
    y = _dot(act, wd_ref[...])

    @pl.when(f == 0)
    def _():
        acc_ref[...] = y

    @pl.when(f > 0)
    def _():
        acc_ref[...] += y

    @pl.when(f == pl.num_programs(1) - 1)
    def _():
        o_ref[...] = x_ref[...] + g2_ref[...] * acc_ref[...]


def _ffn(h2, x1, g2, wg, wu, wd, *, tm, tf, per_row_mod):
    n = h2.shape[0]
    mod_spec = _mod_spec(tm, n, g2, per_row_mod)
    return pl.pallas_call(
        _ffn_kernel,
        grid=(n // tm, D_FF // tf),
        in_specs=[
            pl.BlockSpec((tm, D_MODEL), lambda i, f: (i, 0)),
            pl.BlockSpec((tm, D_MODEL), lambda i, f: (i, 0)),
            mod_spec,
            pl.BlockSpec((D_MODEL, tf), lambda i, f: (0, f)),
            pl.BlockSpec((D_MODEL, tf), lambda i, f: (0, f)),
            pl.BlockSpec((tf, D_MODEL), lambda i, f: (f, 0)),
        ],
        out_specs=pl.BlockSpec((tm, D_MODEL), lambda i, f: (i, 0)),
        out_shape=jax.ShapeDtypeStruct((n, D_MODEL), F32),
        scratch_shapes=[pltpu.VMEM((tm, D_MODEL), F32)],
        compiler_params=_params(("parallel", "arbitrary")),
        name="ffn_dense",
    )(h2, x1, g2, wg, wu, wd)


def _moe_kernel(h_ref, x_ref, g2_ref, rw_ref, rbias_ref, wg_ref, wu_ref, wd_ref, o_ref, acc_ref, gate_ref):
    e = pl.program_id(1)
    hb = h_ref[...]
    tm = hb.shape[0]
    lane = lax.broadcasted_iota(jnp.int32, (tm, ROUTER_PAD), 1).astype(F32)

    @pl.when(e == 0)
    def _():
        logits = _dot(hb, rw_ref[...]) + rbias_ref[...]
        i1, i2, p1, p2 = _top2(logits, lane)
        gate_ref[...] = jnp.where(lane == i1, p1, 0.0) + jnp.where(lane == i2, p2, 0.0)
        acc_ref[...] = jnp.zeros_like(acc_ref)

    ge = jnp.sum(jnp.where(lane == e.astype(F32), gate_ref[...], 0.0), axis=-1, keepdims=True)
    act = (_silu(_dot(hb, wg_ref[...])) * _dot(hb, wu_ref[...])).astype(BF16)
    acc_ref[...] += ge * _dot(act, wd_ref[...])

    @pl.when(e == pl.num_programs(1) - 1)
    def _():
        o_ref[...] = x_ref[...] + g2_ref[...] * acc_ref[...]


def _moe(h2, x1, g2, rw, rbias, wg, wu, wd, *, tm, per_row_mod):
    n = h2.shape[0]
    mod_spec = _mod_spec(tm, n, g2, per_row_mod)
    fe = D_FF_EXPERT
    return pl.pallas_call(
        _moe_kernel,
        grid=(n // tm, N_EXPERTS),
        in_specs=[
            pl.BlockSpec((tm, D_MODEL), lambda i, e: (i, 0)),
            pl.BlockSpec((tm, D_MODEL), lambda i, e: (i, 0)),
            mod_spec,
            pl.BlockSpec((D_MODEL, ROUTER_PAD), lambda i, e: (0, 0)),
            pl.BlockSpec((1, ROUTER_PAD), lambda i, e: (0, 0)),
            pl.BlockSpec((None, D_MODEL, fe), lambda i, e: (e, 0, 0)),
            pl.BlockSpec((None, D_MODEL, fe), lambda i, e: (e, 0, 0)),
            pl.BlockSpec((None, fe, D_MODEL), lambda i, e: (e, 0, 0)),
        ],
        out_specs=pl.BlockSpec((tm, D_MODEL), lambda i, e: (i, 0)),
        out_shape=jax.ShapeDtypeStruct((n, D_MODEL), F32),
        scratch_shapes=[pltpu.VMEM((tm, D_MODEL), F32), pltpu.VMEM((tm, ROUTER_PAD), F32)],
        compiler_params=_params(("parallel", "arbitrary")),
        name="moe",
    )(h2, x1, g2, rw, rbias, wg, wu, wd)


def _top2(logits, lane):
    lg = jnp.where(lane < N_EXPERTS, logits, -jnp.inf)
    m1 = jnp.max(lg, axis=-1, keepdims=True)
    i1 = jnp.min(jnp.where(lg == m1, lane, float(ROUTER_PAD)), axis=-1, keepdims=True)
    lg2 = jnp.where(lane == i1, -jnp.inf, lg)
    m2 = jnp.max(lg2, axis=-1, keepdims=True)
    i2 = jnp.min(jnp.where(lg2 == m2, lane, float(ROUTER_PAD)), axis=-1, keepdims=True)
    e2 = jnp.exp(m2 - m1)
    p1 = 1.0 / (1.0 + e2)
    return i1, i2, p1, e2 * p1


def _route_pack(logits, tri):
    lane = lax.broadcasted_iota(jnp.int32, logits.shape, 1).astype(F32)
    i1, i2, p1, p2 = _top2(logits, lane)
    oh1 = jnp.where(lane == i1, 1.0, 0.0)
    oh2 = jnp.where(lane == i2, 1.0, 0.0)
    cnt1 = jnp.sum(oh1, axis=0, keepdims=True)
    cnt2 = jnp.sum(oh2, axis=0, keepdims=True)
    rank1 = jnp.sum(_dot(tri, oh1.astype(BF16)) * oh1, axis=-1, keepdims=True)
    rank2 = jnp.sum((_dot(tri, oh2.astype(BF16)) + cnt1) * oh2, axis=-1, keepdims=True)
    packed = jnp.zeros_like(logits)
    for k, val in enumerate((p1, p2, i1, i2, rank1, rank2)):
        packed = jnp.where(lane == float(k), val, packed)
    return packed, cnt1 + cnt2


def _route_tables(route_t, tile_counts, tm, n_tiles):
    n = route_t.shape[1]
    tile_cnt = tile_counts[:, 0, :N_EXPERTS].astype(jnp.int32)
    cnt = jnp.sum(tile_cnt, axis=0)
    gsz = ((cnt + tm - 1) // tm) * tm
    gend = jnp.cumsum(gsz)
    seg_start = (gend - gsz)[None, :] + jnp.cumsum(tile_cnt, axis=0) - tile_cnt
    seg_of_token = jnp.repeat(seg_start.T, n // tile_cnt.shape[0], axis=1)
    experts = jnp.arange(N_EXPERTS, dtype=jnp.int32)[:, None]
    pos = []
    for k in range(2):
        e_k = route_t[2 + k].astype(jnp.int32)
        rank_k = route_t[4 + k].astype(jnp.int32)
        pos.append(jnp.sum(jnp.where(e_k[None, :] == experts, seg_of_token, 0), axis=0) + rank_k)
    pos = jnp.concatenate(pos)
    tile_start = jnp.arange(n_tiles, dtype=jnp.int32) * tm
    tile_expert = jnp.sum((tile_start[:, None] >= gend[None, :]).astype(jnp.int32), axis=1)
    tile_expert = jnp.minimum(tile_expert, N_EXPERTS - 1)
    live = (gend[-1] // tm).reshape(1)
    return pos, tile_expert, live, gend - gsz + cnt, gend


def _row_copy_wait(src_hbm, dst, sem, rows):
    pltpu.make_async_copy(src_hbm.at[pl.ds(0, rows)], dst, sem).wait()


def _moe_dispatch_kernel(pos_ref, pad_lo_ref, pad_hi_ref, live_ref, h_ref, xs_hbm, zbuf, sem, zsem):
    i = pl.program_id(0)
    tt = h_ref.shape[0]
    n = tt * pl.num_programs(0)
    tm = zbuf.shape[0]

    base = i * tt
    for r in range(tt):
        row = h_ref.at[pl.ds(r, 1)]
        pltpu.make_async_copy(row, xs_hbm.at[pl.ds(pos_ref[base + r], 1)], sem).start(priority=0)
        pltpu.make_async_copy(row, xs_hbm.at[pl.ds(pos_ref[n + base + r], 1)], sem).start(priority=1)
    for _ in range(2):
        pltpu.make_async_copy(h_ref, xs_hbm.at[pl.ds(0, tt)], sem).wait()

    @pl.when(i == pl.num_programs(0) - 1)
    def _():
        zbuf[...] = jnp.zeros(zbuf.shape, F32)

        def fill_row(p):
            return pltpu.make_async_copy(zbuf.at[pl.ds(0, 1)], xs_hbm.at[pl.ds(p, 1)], zsem)

        def fill_tile(t):
            return pltpu.make_async_copy(zbuf, xs_hbm.at[pl.ds(pl.multiple_of(t * tm, tm), tm)], zsem)

        for e in range(N_EXPERTS):
            lo, hi = pad_lo_ref[e], pad_hi_ref[e]
            lax.fori_loop(lo, hi, lambda p, c: (fill_row(p).start(), c)[1], 0)
            lax.fori_loop(lo, hi, lambda p, c: (fill_row(p).wait(), c)[1], 0)
        lo, hi = live_ref[0], xs_hbm.shape[0] // tm
        lax.fori_loop(lo, hi, lambda t, c: (fill_tile(t).start(), c)[1], 0)
        lax.fori_loop(lo, hi, lambda t, c: (fill_tile(t).wait(), c)[1], 0)


def _moe_dispatch(h2, pos, pad_lo, pad_hi, live, *, tt, tm, n_tiles):
    n = h2.shape[0]
    grid_spec = pltpu.PrefetchScalarGridSpec(
        num_scalar_prefetch=4,
        grid=(n // tt,),
        in_specs=[pl.BlockSpec((tt, D_MODEL), lambda i, *_: (i, 0))],
        out_specs=pl.BlockSpec(memory_space=pl.ANY),
        scratch_shapes=[pltpu.VMEM((tm, D_MODEL), F32), pltpu.SemaphoreType.DMA(()),
                        pltpu.SemaphoreType.DMA(())],
    )
    return pl.pallas_call(
        _moe_dispatch_kernel,
        grid_spec=grid_spec,
        out_shape=jax.ShapeDtypeStruct((n_tiles * tm, D_MODEL), F32),
        compiler_params=_params(("arbitrary",)),
        name="moe_dispatch",
    )(pos, pad_lo, pad_hi, live, h2)


def _moe_expert_kernel(te_ref, live_ref, x_ref, wg_ref, wu_ref, wd_ref, y_ref):
    del te_ref
    is_live = pl.program_id(0) < live_ref[0]

    @pl.when(is_live)
    def _():
        hb = x_ref[...].astype(BF16)
        act = (_silu(_dot(hb, wg_ref[...])) * _dot(hb, wu_ref[...])).astype(BF16)
        y_ref[...] = _dot(act, wd_ref[...])

    @pl.when(jnp.logical_not(is_live))
    def _():
        y_ref[...] = jnp.zeros(y_ref.shape, F32)


def _moe_experts(xs, tile_expert, live, wg, wu, wd, *, tm, n_tiles):
    fe = D_FF_EXPERT

    def in_tile(t, te, live):
        return (jnp.minimum(t, live[0] - 1), 0)

    def out_tile(t, te, live):
        return (t, 0)

    def expert(t, te, live):
        return (te[t], 0, 0)

    grid_spec = pltpu.PrefetchScalarGridSpec(
        num_scalar_prefetch=2,
        grid=(n_tiles,),
        in_specs=[
            pl.BlockSpec((tm, D_MODEL), in_tile),
            pl.BlockSpec((None, D_MODEL, fe), expert),
            pl.BlockSpec((None, D_MODEL, fe), expert),
            pl.BlockSpec((None, fe, D_MODEL), expert),
        ],
        out_specs=pl.BlockSpec((tm, D_MODEL), out_tile),
    )
    return pl.pallas_call(
        _moe_expert_kernel,
        grid_spec=grid_spec,
        out_shape=jax.ShapeDtypeStruct(xs.shape, F32),
        compiler_params=_params(("arbitrary",)),
        name="moe_experts",
    )(tile_expert, live, xs, wg, wu, wd)


def _moe_combine_kernel(pos_ref, x_ref, g2_ref, r_ref, ys_hbm, o_ref, buf, sem):
    i = pl.program_id(0)
    nt = pl.num_programs(0)
    tt = x_ref.shape[0]
    n = nt * tt

    def fetch(tile, sl):
        base = tile * tt
        for r in range(tt):
            for s in range(2):
                row = pos_ref[s * n + base + r]
                pltpu.make_async_copy(ys_hbm.at[pl.ds(row, 1)], buf.at[sl, pl.ds(s * tt + r, 1)],
                                      sem.at[sl]).start(priority=s)

    @pl.when(i == 0)
    def _():
        fetch(0, 0)

    @pl.when(i + 1 < nt)
    def _():
        fetch(i + 1, (i + 1) % 2)

    sl = i % 2
    _row_copy_wait(ys_hbm, buf.at[sl], sem.at[sl], 2 * tt)
    r = r_ref[...]
    f = r[:, 0:1] * buf[sl, pl.ds(0, tt), :] + r[:, 1:2] * buf[sl, pl.ds(tt, tt), :]
    o_ref[...] = x_ref[...] + g2_ref[...] * f


def _moe_combine(x1, g2, route, pos, ys, *, tt):
    n = x1.shape[0]
    tiles_per_mod = (n // g2.shape[0]) // tt
    grid_spec = pltpu.PrefetchScalarGridSpec(
        num_scalar_prefetch=1,
        grid=(n // tt,),
        in_specs=[
            pl.BlockSpec((tt, D_MODEL), lambda i, pos: (i, 0)),
            pl.BlockSpec((None, 1, D_MODEL), lambda i, pos: (i // tiles_per_mod, 0, 0)),
            pl.BlockSpec((tt, ROUTER_PAD), lambda i, pos: (i, 0)),
            pl.BlockSpec(memory_space=pl.ANY),
        ],
        out_specs=pl.BlockSpec((tt, D_MODEL), lambda i, pos: (i, 0)),
        scratch_shapes=[pltpu.VMEM((2, 2 * tt, D_MODEL), F32), pltpu.SemaphoreType.DMA((2,))],
    )
    return pl.pallas_call(
        _moe_combine_kernel,
        grid_spec=grid_spec,
        out_shape=jax.ShapeDtypeStruct((n, D_MODEL), F32),
        compiler_params=_params(("arbitrary",)),
        name="moe_combine",
    )(pos, x1, g2, route, ys)


def _moe_routed(h2, x1, g2, route, route_t, tile_counts, wg, wu, wd, *, tm, tt):
    n = h2.shape[0]
    n_tiles = (2 * n) // tm + N_EXPERTS
    pos, tile_expert, live, pad_lo, pad_hi = _route_tables(route_t, tile_counts, tm, n_tiles)
    xs = _moe_dispatch(h2, pos, pad_lo, pad_hi, live, tt=tt, tm=tm, n_tiles=n_tiles)
    ys = _moe_experts(xs, tile_expert, live, wg, wu, wd, tm=tm, n_tiles=n_tiles)
    return _moe_combine(x1, g2, route, pos, ys, tt=tt)


def _head_perm():
    idx = []
    for g in range(GROUP):
        for kv in range(N_KV):
            h = kv * GROUP + g
            idx.extend(range(h * HEAD_DIM, (h + 1) * HEAD_DIM))
    return np.asarray(idx, np.int32)


def _relayout_w_in(w):
    pts = np.cumsum([0, Q_W, KV_W, KV_W, GK_W, GK_W, GV_W, GATE_RANK, GV_W, D_MODEL, D_MODEL])
    qa, ka, va, qb, kb, vb, ga, rb, gta, gtb = [w[:, pts[i]:pts[i + 1]] for i in range(10)]
    qa = qa[:, _head_perm()]
    ga = jnp.pad(ga, ((0, 0), (0, GATE_PAD - GATE_RANK)))
    return jnp.concatenate([qa, ka, va, qb, kb, vb, rb, gta, gtb, ga], axis=1).astype(BF16)


def kernel(x_prompt, x_sample, cache_k, cache_v, state_gla, c_prompt, c_sample, ada_w, ada_b, norm1_g, norm2_g, w_in, q_norm_g, k_norm_g, attn_sinks, gla_wa2, gla_ba, gla_norm_g, w_branch_a, w_branch_b, w_out, ffn_w_gate, ffn_w_up, ffn_w_down, router_w, router_b, moe_w_gate, moe_w_up, moe_w_down):
    n_p = BATCH * SEQ
    xp = x_prompt.reshape(n_p, D_MODEL)
    xs = x_sample.reshape(DEC_BATCH, D_MODEL)

    c_rows = BATCH + DEC_BATCH
    c_pad = -c_rows % 8
    c_all = jnp.pad(jnp.concatenate([c_prompt, c_sample], axis=0), ((0, c_pad), (0, 0)))
    mod = _ada(c_all, ada_w, ada_b)

    bd = jnp.asarray(np.kron(np.eye(N_HEADS), np.ones((HEAD_DIM, HEAD_DIM))), BF16)
    ut = jnp.asarray(_gla_cum_matrix(), BF16)
    perm = _head_perm()
    swa_bias = jnp.asarray(_swa_sample_bias())

    kp_l, vp_l, sp_l, ks_l, vs_l, ss_l = [], [], [], [], [], []
    for l in range(DEPTH):
        m = mod[l].reshape(c_rows + c_pad, 6, D_MODEL)
        mod_p = [m[:BATCH, i].reshape(BATCH, 1, D_MODEL) for i in range(6)]
        mod_s = [m[BATCH:c_rows, i] for i in range(6)]

        w = _relayout_w_in(w_in[l])
        qg = (jnp.tile(q_norm_g[l], N_HEADS) * (HEAD_DIM ** -0.5)).reshape(1, Q_W)
        kg = jnp.tile(k_norm_g[l], N_KV).reshape(1, KV_W)
        wa2 = jnp.pad(gla_wa2[l], ((0, GATE_PAD - GATE_RANK), (0, 0))).astype(BF16)
        ba = gla_ba[l].reshape(1, GK_W)
        n1 = norm1_g[l].reshape(1, D_MODEL)
        n2 = norm2_g[l].reshape(1, D_MODEL)
        gn = gla_norm_g[l].reshape(1, GLA_DV)
        wpa = w_branch_a[l][perm].astype(BF16)
        wpb = w_branch_b[l].astype(BF16)
        wo = w_out[l].astype(BF16)
        sink_rows = jnp.broadcast_to(attn_sinks[l][_sample_head_of_row()][:, None], (N_HEADS, LANES))

        q, k, v, gq, gk, gv, la, rbs, sga, sgb = _mixin(
            xp, mod_p[0], mod_p[1], n1, w, bd, qg, kg, wa2, ba, tm=1024, per_row_mod=False)
        a_out = _swa_prompt(attn_sinks[l], q, k, v)
        g_out, s_fin = _gla_prompt(gq, gk, la, gv, rbs, gn, ut)
        is_moe = l % 2 == 1
        if is_moe:
            rw = jnp.pad(router_w[l // 2], ((0, 0), (0, ROUTER_PAD - N_EXPERTS))).astype(BF16)
            rbias = jnp.pad(router_b[l // 2], (0, ROUTER_PAD - N_EXPERTS)).reshape(1, ROUTER_PAD)
        merged = _merge(xp, a_out, g_out, sga, sgb, wpa, wpb, wo, mod_p[2], mod_p[3], mod_p[4], n2,
                        tm=512, per_row_mod=False, h2_dtype=F32 if is_moe else BF16,
                        router=(rw, rbias) if is_moe else None)
        x1, h2 = merged[0], merged[1]
        kp_l.append(k.reshape(BATCH, SEQ, N_KV, HEAD_DIM)[:, SEQ - WINDOW:])
        vp_l.append(v.reshape(BATCH, SEQ, N_KV, HEAD_DIM)[:, SEQ - WINDOW:])
        sp_l.append(s_fin.reshape(BATCH, GLA_HEADS, GLA_DK, GLA_DV))

        qs, ksn, vsn, gqs, gks, gvs, las, rbss, sgas, sgbs = _mixin(
            xs, mod_s[0], mod_s[1], n1, w, bd, qg, kg, wa2, ba, tm=DEC_BATCH, per_row_mod=True)
        a_s, nk, nv = _swa_sample(
            qs.astype(F32).reshape(DEC_BATCH, N_HEADS, HEAD_DIM),
            ksn.reshape(DEC_BATCH, N_KV, HEAD_DIM), vsn.reshape(DEC_BATCH, N_KV, HEAD_DIM),
            cache_k[l].reshape(DEC_BATCH, CACHE_ROWS, HEAD_DIM),
            cache_v[l].reshape(DEC_BATCH, CACHE_ROWS, HEAD_DIM), swa_bias, sink_rows)
        g_s, s_new = _gla_sample(gqs, gks, las, gvs, rbss, gn, state_gla[l].reshape(DEC_BATCH, GK_W, GLA_DV))
        x1s, h2s = _merge(xs, a_s.reshape(DEC_BATCH, Q_W), g_s, sgas, sgbs, wpa, wpb, wo,
                          mod_s[2], mod_s[3], mod_s[4], n2, tm=DEC_BATCH, per_row_mod=True)
        ks_l.append(nk.reshape(DEC_BATCH, WINDOW, N_KV, HEAD_DIM))
        vs_l.append(nv.reshape(DEC_BATCH, WINDOW, N_KV, HEAD_DIM))
        ss_l.append(s_new.reshape(DEC_BATCH, GLA_HEADS, GLA_DK, GLA_DV))

        i = l // 2
        if not is_moe:
            wg, wu, wd = ffn_w_gate[i].astype(BF16), ffn_w_up[i].astype(BF16), ffn_w_down[i].astype(BF16)
            xp = _ffn(h2, x1, mod_p[5], wg, wu, wd, tm=512, tf=D_FF // 2, per_row_mod=False)
            xs = _ffn(h2s, x1s, mod_s[5], wg, wu, wd, tm=DEC_BATCH, tf=D_FF // 2, per_row_mod=True)
        else:
            wg, wu, wd = moe_w_gate[i].astype(BF16), moe_w_up[i].astype(BF16), moe_w_down[i].astype(BF16)
            xp = _moe_routed(h2, x1, mod_p[5], *merged[2:5], wg, wu, wd, tm=512, tt=512)
            xs = _moe(h2s, x1s, mod_s[5], rw, rbias, wg, wu, wd, tm=DEC_BATCH, per_row_mod=True)

    return (xp.reshape(BATCH, SEQ, D_MODEL), xs.reshape(DEC_BATCH, 1, D_MODEL),
            jnp.stack(kp_l), jnp.stack(vp_l), jnp.stack(sp_l),
            jnp.stack(ks_l), jnp.stack(vs_l), jnp.stack(ss_l))
```

```python
import functools

import jax
import jax.numpy as jnp
import numpy as np
from jax import lax
from jax.experimental import pallas as pl
from jax.experimental.pallas import tpu as pltpu

D_MODEL = 1024
BATCH = 4
SEQ = 4096
DEPTH = 2
DEC_BATCH = 128
N_HEADS = 8
N_KV = 2
HEAD_DIM = 64
GROUP = N_HEADS // N_KV
WINDOW = 128
GLA_HEADS = 4
GLA_DK = 64
GLA_DV = 128
GATE_RANK = 16
GATE_TAU = 16.0
D_FF = 2816
N_EXPERTS = 8
D_FF_EXPERT = 1408
EPS = 1e-6

Q_W = N_HEADS * HEAD_DIM
KV_W = N_KV * HEAD_DIM
GK_W = GLA_HEADS * GLA_DK
GV_W = GLA_HEADS * GLA_DV

LANES = 128
GATE_PAD = LANES
ROUTER_PAD = LANES
VMEM_LIMIT = 56 * 1024 * 1024

F32 = jnp.float32
BF16 = jnp.bfloat16

_C_Q = 0
_C_K = _C_Q + Q_W
_C_V = _C_K + KV_W
_C_GQ = _C_V + KV_W
_C_GK = _C_GQ + GK_W
_C_GV = _C_GK + GK_W
_C_RB = _C_GV + GV_W
_C_GA = _C_RB + GV_W
_C_GB = _C_GA + D_MODEL
_C_LR = _C_GB + D_MODEL
PROJ_PAD = _C_LR + GATE_PAD

GLA_CHUNK = 128
GLA_LEVELS = 7
GLA_MXU_LEVELS = 3
GLA_CHUNKS_PER_STEP = 4


def _params(sem, vmem=VMEM_LIMIT):
    return pltpu.CompilerParams(dimension_semantics=sem, vmem_limit_bytes=vmem)


def _dot(a, b):
    return jnp.dot(a, b, preferred_element_type=F32)


def _dot_nt(a, b):
    return lax.dot_general(a, b, (((1,), (1,)), ((), ())), preferred_element_type=F32)


def _dot_tn(a, b):
    return lax.dot_general(a, b, (((0,), (0,)), ((), ())), preferred_element_type=F32)


def _sigmoid(x):
    return 1.0 / (1.0 + jnp.exp(-x))


def _silu(x):
    return x * _sigmoid(x)


def _ada_kernel(c_ref, w_ref, b_ref, o_ref):
    c = c_ref[...]
    o_ref[...] = _dot(_silu(c).astype(BF16), w_ref[...].astype(BF16)) + b_ref[...]


def _ada(c_all, ada_w, ada_b):
    rows = c_all.shape[0]
    tn = 1024
    return pl.pallas_call(
        _ada_kernel,
        grid=(DEPTH, 6 * D_MODEL // tn),
        in_specs=[
            pl.BlockSpec((rows, D_MODEL), lambda l, j: (0, 0)),
            pl.BlockSpec((None, D_MODEL, tn), lambda l, j: (l, 0, j)),
            pl.BlockSpec((None, 1, tn), lambda l, j: (l, 0, j)),
        ],
        out_specs=pl.BlockSpec((None, rows, tn), lambda l, j: (l, 0, j)),
        out_shape=jax.ShapeDtypeStruct((DEPTH, rows, 6 * D_MODEL), F32),
        compiler_params=_params(("parallel", "parallel")),
        name="ada_mod",
    )(c_all, ada_w, ada_b.reshape(DEPTH, 1, 6 * D_MODEL))


def _mixin_kernel(x_ref, sh_ref, sc_ref, n1_ref, w_ref, bd_ref, qg_ref, kg_ref, wa2_ref, ba_ref,
                  q_ref, k_ref, v_ref, gq_ref, gk_ref, gv_ref, la_ref, rb_ref, sga_ref, sgb_ref):
    x = x_ref[...]
    ms = jnp.mean(x * x, axis=-1, keepdims=True)
    h = x * lax.rsqrt(ms + EPS) * n1_ref[...]
    h = h * (1.0 + sc_ref[...]) + sh_ref[...]
    hb = h.astype(BF16)

    def proj(a, b):
        return _dot(hb, w_ref[:, a:b])

    q = proj(_C_Q, _C_K)
    ssq = _dot((q * q).astype(BF16), bd_ref[...])
    q_ref[...] = (q * lax.rsqrt(ssq * (1.0 / HEAD_DIM) + EPS) * qg_ref[...]).astype(BF16)
    k = proj(_C_K, _C_V)
    ssk = _dot((k * k).astype(BF16), bd_ref[0:KV_W, 0:KV_W])
    k_ref[...] = k * lax.rsqrt(ssk * (1.0 / HEAD_DIM) + EPS) * kg_ref[...]
    v_ref[...] = proj(_C_V, _C_GQ)
    gq_ref[...] = proj(_C_GQ, _C_GK) * (GLA_DK ** -0.5)
    gk_ref[...] = proj(_C_GK, _C_GV)
    gv_ref[...] = proj(_C_GV, _C_RB)
    rb_ref[...] = _silu(proj(_C_RB, _C_GA)).astype(BF16)
    sga_ref[...] = _sigmoid(proj(_C_GA, _C_GB)).astype(BF16)
    sgb_ref[...] = _sigmoid(proj(_C_GB, _C_LR)).astype(BF16)
    ga = proj(_C_LR, PROJ_PAD)
    xg = _dot(ga.astype(BF16), wa2_ref[...]) + ba_ref[...]
    la_ref[...] = (jnp.minimum(xg, 0.0) - jnp.log1p(jnp.exp(-jnp.abs(xg)))) * (1.0 / GATE_TAU)


def _mixin(x, sh, sc, n1, w, bd, qg, kg, wa2, ba, *, tm, per_row_mod):
    n = x.shape[0]
    nt = n // tm
    mod_spec = _mod_spec(tm, n, sh, per_row_mod)

    def row(i):
        return (i, 0)

    def const(shape):
        return pl.BlockSpec(shape, lambda i: (0,) * len(shape), pipeline_mode=pl.Buffered(1))

    def out(width, dtype):
        return pl.BlockSpec((tm, width), row), jax.ShapeDtypeStruct((n, width), dtype)

    outs = [out(Q_W, BF16), out(KV_W, F32), out(KV_W, F32), out(GK_W, F32), out(GK_W, F32),
            out(GV_W, F32), out(GK_W, F32), out(GV_W, BF16), out(D_MODEL, BF16), out(D_MODEL, BF16)]
    return pl.pallas_call(
        _mixin_kernel,
        grid=(nt,),
        in_specs=[
            pl.BlockSpec((tm, D_MODEL), row), mod_spec, mod_spec, const((1, D_MODEL)),
            const((D_MODEL, PROJ_PAD)), const((Q_W, Q_W)), const((1, Q_W)), const((1, KV_W)),
            const((GATE_PAD, GK_W)), const((1, GK_W)),
        ],
        out_specs=[o[0] for o in outs],
        out_shape=[o[1] for o in outs],
        compiler_params=_params(("parallel",)),
        name="mixer_in",
    )(x, sh, sc, n1, w, bd, qg, kg, wa2, ba)


SWA_BLOCKS = 8


def _head_slope(h):
    return float(2.0 ** (-8.0 * (h + 1) / N_HEADS))


def _swa_prompt_kernel(sink_ref, q_ref, kp_ref, kc_ref, vp_ref, vc_ref, o_ref):
    n = pl.program_id(1)
    blk = WINDOW
    kall = jnp.concatenate([kp_ref[...], kc_ref[...]], axis=0).astype(BF16)
    vall = jnp.concatenate([vp_ref[...], vc_ref[...]], axis=0).astype(BF16)
    row = lax.broadcasted_iota(jnp.int32, (blk, 2 * blk), 0)
    col = lax.broadcasted_iota(jnp.int32, (blk, 2 * blk), 1)
    dist = row + blk - col
    in_window = (dist >= 0) & (dist <= WINDOW)
    first_key = jnp.where(n > 0, 0, blk)
    distf = dist.astype(F32)
    klane = lax.broadcasted_iota(jnp.int32, kall.shape, 1)
    kall_kv = [jnp.where((klane // HEAD_DIM) == kv, kall, jnp.zeros_like(kall)) for kv in range(N_KV)]
    olane = lax.broadcasted_iota(jnp.int32, (blk, KV_W), 1)
    q = q_ref[...]
    for j in range(SWA_BLOCKS):
        valid = in_window & (col >= first_key) if j == 0 else in_window
        vv = vall[j * blk:(j + 2) * blk]
        outs = []
        for g in range(GROUP):
            qp = q[j * blk:(j + 1) * blk, g * KV_W:(g + 1) * KV_W]
            pair = []
            for kv in range(N_KV):
                h = kv * GROUP + g
                s = _dot_nt(qp, kall_kv[kv][j * blk:(j + 2) * blk]) - _head_slope(h) * distf
                s = jnp.where(valid, s, -jnp.inf)
                sink = sink_ref[h]
                m = jnp.maximum(jnp.max(s, axis=-1, keepdims=True), sink)
                p = jnp.exp(s - m)
                den = jnp.sum(p, axis=-1, keepdims=True) + jnp.exp(sink - m)
                pair.append(_dot(p.astype(BF16), vv) * (1.0 / den))
            outs.append(jnp.where(olane < HEAD_DIM, pair[0], pair[1]))
        o_ref[j * blk:(j + 1) * blk, :] = jnp.concatenate(outs, axis=-1).astype(BF16)


def _swa_prompt(sinks, q, k, v):
    nb = SEQ // WINDOW
    steps = nb // SWA_BLOCKS
    tq = SWA_BLOCKS * WINDOW

    def cur(b, n):
        return (b * steps + n, 0)

    def prev(b, n):
        return (b * nb + jnp.maximum(n * SWA_BLOCKS - 1, 0), 0)

    return pl.pallas_call(
        _swa_prompt_kernel,
        grid=(BATCH, steps),
        in_specs=[
            pl.BlockSpec(memory_space=pltpu.SMEM),
            pl.BlockSpec((tq, Q_W), cur),
            pl.BlockSpec((WINDOW, KV_W), prev), pl.BlockSpec((tq, KV_W), cur),
            pl.BlockSpec((WINDOW, KV_W), prev), pl.BlockSpec((tq, KV_W), cur),
        ],
        out_specs=pl.BlockSpec((tq, Q_W), cur),
        out_shape=jax.ShapeDtypeStruct((BATCH * SEQ, Q_W), BF16),
        compiler_params=_params(("parallel", "parallel")),
        name="swa_prompt",
    )(sinks, q, k, k, v, v)


SAMPLE_TB = 8
CACHE_ROWS = WINDOW * N_KV


def _sample_head_of_row():
    j = np.arange(N_HEADS)
    return (j % N_KV) * GROUP + j // N_KV


def _swa_sample_bias():
    j = np.arange(N_HEADS)[:, None]
    c = np.arange(CACHE_ROWS)[None, :]
    slope = 2.0 ** (-8.0 * (_sample_head_of_row()[:, None] + 1) / N_HEADS)
    bias = -slope * (WINDOW - c // N_KV)
    return np.where(c % N_KV == j % N_KV, bias, -np.inf).astype(np.float32)


def _swa_sample_kernel(q_ref, kn_ref, vn_ref, ck_ref, cv_ref, bias_ref, sk_ref, o_ref, ok_ref, ov_ref):
    rows = CACHE_ROWS
    sink = sk_ref[...][:, 0:1]
    q = q_ref[...]
    kn = kn_ref[...]
    vn = vn_ref[...]
    kc = ck_ref[...]
    vc = cv_ref[...]
    kn8 = jnp.concatenate([kn] * GROUP, axis=1)
    vn8 = jnp.concatenate([vn] * GROUP, axis=1)
    s = lax.dot_general(q.astype(BF16), kc.astype(BF16), (((2,), (2,)), ((0,), (0,))),
                        preferred_element_type=F32) + bias_ref[...]
    s_new = jnp.sum(q * kn8, axis=-1, keepdims=True)
    m = jnp.maximum(jnp.maximum(jnp.max(s, axis=-1, keepdims=True), s_new), sink)
    p = jnp.exp(s - m)
    p_new = jnp.exp(s_new - m)
    den = jnp.sum(p, axis=-1, keepdims=True) + p_new + jnp.exp(sink - m)
    o = lax.dot_general(p.astype(BF16), vc.astype(BF16), (((2,), (1,)), ((0,), (0,))),
                        preferred_element_type=F32) + p_new * vn8
    o_ref[...] = o * (1.0 / den)
    ok_ref[:, pl.ds(0, rows - N_KV), :] = kc[:, N_KV:, :]
    ok_ref[:, pl.ds(rows - N_KV, N_KV), :] = kn
    ov_ref[:, pl.ds(0, rows - N_KV), :] = vc[:, N_KV:, :]
    ov_ref[:, pl.ds(rows - N_KV, N_KV), :] = vn


def _swa_sample(q, kn, vn, ck, cv, bias, sinks):
    tb = 2 * SAMPLE_TB
    nb = DEC_BATCH // tb

    def const(shape):
        return pl.BlockSpec(shape, lambda i: (0,) * len(shape))

    def per_seq(*dims):
        return pl.BlockSpec((tb,) + dims, lambda i: (i, 0, 0))

    return pl.pallas_call(
        _swa_sample_kernel,
        grid=(nb,),
        in_specs=[per_seq(N_HEADS, HEAD_DIM), per_seq(N_KV, HEAD_DIM), per_seq(N_KV, HEAD_DIM),
                  per_seq(CACHE_ROWS, HEAD_DIM), per_seq(CACHE_ROWS, HEAD_DIM),
                  const((N_HEADS, CACHE_ROWS)), const((N_HEADS, LANES))],
        out_specs=[per_seq(N_HEADS, HEAD_DIM), per_seq(CACHE_ROWS, HEAD_DIM), per_seq(CACHE_ROWS, HEAD_DIM)],
        out_shape=[
            jax.ShapeDtypeStruct((DEC_BATCH, N_HEADS, HEAD_DIM), F32),
            jax.ShapeDtypeStruct((DEC_BATCH, CACHE_ROWS, HEAD_DIM), F32),
            jax.ShapeDtypeStruct((DEC_BATCH, CACHE_ROWS, HEAD_DIM), F32),
        ],
        compiler_params=_params(("parallel",)),
        name="swa_sample",
    )(q, kn, vn, ck, cv, bias, sinks)


def _gla_cum_matrix():
    c = GLA_CHUNK
    tri = np.tril(np.ones((c, c), np.float32))
    i = np.arange(c)
    blocks = []
    for lvl in range(GLA_MXU_LEVELS):
        half = 1 << lvl
        mid = (i // (2 * half)) * (2 * half) + half - 1
        blocks.append(tri - tri[mid])
    blocks.append(tri)
    return np.concatenate(blocks, axis=0)


def _split3(x):
    hi = x.astype(BF16)
    r1 = x - hi.astype(F32)
    mid = r1.astype(BF16)
    lo = (r1 - mid.astype(F32)).astype(BF16)
    return hi, mid, lo


def _gla_level_masks():
    cl = GLA_CHUNK
    r = np.arange(cl)
    upper = np.stack([np.broadcast_to(((r >> lvl) & 1)[:, None], (cl, GK_W)) for lvl in range(GLA_LEVELS)])
    ri = np.tile(r, GLA_HEADS)[:, None]
    pairs = [(ri >> (lvl + 1)) == (r[None, :] >> (lvl + 1)) for lvl in range(GLA_LEVELS)]
    pairs.append(ri == r[None, :])
    return upper.astype(np.float32), np.stack(pairs).astype(np.float32)


def _gla_prompt_kernel(q_ref, k_ref, la_ref, v_ref, rb_ref, gn_ref, ut_ref, up_ref, pm_ref,
                       o_ref, s_ref, st_ref):
    c = pl.program_id(1)
    cl = GLA_CHUNK

    @pl.when(c == 0)
    def _():
        st_ref[0] = jnp.zeros(st_ref.shape[1:], F32)

    ut = ut_ref[...]
    lane = lax.broadcasted_iota(jnp.int32, (cl, GK_W), 1)
    head_of_lane = lane // GLA_DK
    ones = jnp.ones((cl, LANES), BF16)
    state = [st_ref[c % 2, h] for h in range(GLA_HEADS)]
    for cc in range(GLA_CHUNKS_PER_STEP):
        rows = slice(cc * cl, (cc + 1) * cl)
        hi, mid, lo = _split3(la_ref[rows, :])
        tall = _dot(ut, hi) + _dot(ut, mid) + _dot(ut, lo)
        q = q_ref[rows, :]
        k = k_ref[rows, :]
        attn_all = jnp.zeros((GLA_HEADS * cl, cl), F32)
        cum = tall[GLA_MXU_LEVELS * cl:(GLA_MXU_LEVELS + 1) * cl]
        for lvl in range(GLA_LEVELS + 1):
            if lvl < GLA_LEVELS:
                if lvl < GLA_MXU_LEVELS:
                    t_lvl = tall[lvl * cl:(lvl + 1) * cl]
                else:
                    half = 1 << lvl
                    mids = [jnp.broadcast_to(cum[b0 + half - 1:b0 + half, :], (2 * half, GK_W))
                            for b0 in range(0, cl, 2 * half)]
                    t_lvl = cum - (mids[0] if len(mids) == 1 else jnp.concatenate(mids, axis=0))
                e = jnp.exp(-jnp.abs(t_lvl))
                e_up = e * up_ref[lvl]
                qt = (q * e_up).astype(BF16)
                kt = (k * (e - e_up)).astype(BF16)
            else:
                qt = q.astype(BF16)
                kt = k.astype(BF16)
            q_heads = jnp.concatenate(
                [jnp.where(head_of_lane == h, qt, jnp.zeros_like(qt)) for h in range(GLA_HEADS)], axis=0)
            attn_all = attn_all + _dot_nt(q_heads, kt) * pm_ref[lvl]

        last = cum[cl - 1:cl, :]
        qe = (q * jnp.exp(cum)).astype(BF16)
        kd = (k * jnp.exp(last - cum)).astype(BF16)
        dec = jnp.exp(_dot_tn(hi, ones) + _dot_tn(mid, ones) + _dot_tn(lo, ones))
        v = v_ref[rows, :]
        rb = rb_ref[rows, :]
        outs = []
        for h in range(GLA_HEADS):
            vh = v[:, h * GLA_DV:(h + 1) * GLA_DV].astype(BF16)
            o = _dot(attn_all[h * cl:(h + 1) * cl].astype(BF16), vh) + _dot(qe, state[h].astype(BF16))
            kdh = jnp.where(head_of_lane == h, kd, jnp.zeros_like(kd))
            state[h] = state[h] * dec + _dot_tn(kdh, vh)
            ms = jnp.mean(o * o, axis=-1, keepdims=True)
            g = o * lax.rsqrt(ms + EPS) * gn_ref[...]
            outs.append(g * rb[:, h * GLA_DV:(h + 1) * GLA_DV].astype(F32))
        o_ref[rows, :] = jnp.concatenate(outs, axis=-1).astype(BF16)

    for h in range(GLA_HEADS):
        st_ref[(c + 1) % 2, h] = state[h]

    @pl.when(c == pl.num_programs(1) - 1)
    def _():
        fin = (SEQ // (GLA_CHUNK * GLA_CHUNKS_PER_STEP)) % 2
        s_ref[...] = st_ref[fin, 0] + st_ref[fin, 1] + st_ref[fin, 2] + st_ref[fin, 3]


def _gla_prompt(gq, gk, la, gv, rbs, gn, ut, upper, pairs):
    cl = GLA_CHUNK * GLA_CHUNKS_PER_STEP
    nc = SEQ // cl

    def row(b, c):
        return (b * nc + c, 0)

    def const(shape):
        return pl.BlockSpec(shape, lambda b, c: (0,) * len(shape), pipeline_mode=pl.Buffered(1))

    return pl.pallas_call(
        _gla_prompt_kernel,
        grid=(BATCH, nc),
        in_specs=[
            pl.BlockSpec((cl, GK_W), row), pl.BlockSpec((cl, GK_W), row), pl.BlockSpec((cl, GK_W), row),
            pl.BlockSpec((cl, GV_W), row), pl.BlockSpec((cl, GV_W), row),
            const((1, GLA_DV)), const(ut.shape), const(upper.shape), const(pairs.shape),
        ],
        out_specs=[
            pl.BlockSpec((cl, GV_W), row),
            pl.BlockSpec((None, GK_W, GLA_DV), lambda b, c: (b, 0, 0)),
        ],
        out_shape=[
            jax.ShapeDtypeStruct((BATCH * SEQ, GV_W), BF16),
            jax.ShapeDtypeStruct((BATCH, GK_W, GLA_DV), F32),
        ],
        scratch_shapes=[pltpu.VMEM((2, GLA_HEADS, GK_W, GLA_DV), F32)],
        compiler_params=_params(("parallel", "arbitrary")),
        name="gla_prompt",
    )(gq, gk, la, gv, rbs, gn, ut, upper, pairs)


def _gla_sample_kernel(q_ref, k_ref, la_ref, v_ref, rb_ref, gn_ref, s_ref, o_ref, so_ref):
    tb = SAMPLE_TB
    dec = jnp.exp(la_ref[...])
    pieces = []
    for x in (dec, k_ref[...], q_ref[...]):
        hi, mid, lo = _split3(x)
        stacked = jnp.concatenate(
            [hi.astype(F32), mid.astype(F32), lo.astype(F32), jnp.zeros_like(x)], axis=0)
        pieces.append(stacked.astype(BF16))
    prow = lax.broadcasted_iota(jnp.int32, (4 * tb, LANES), 0)
    rb = rb_ref[...].astype(F32)
    v = v_ref[...]
    for bi in range(tb):
        sel = jnp.where((prow % tb) == bi, 1.0, 0.0).astype(BF16)
        a_col, k_col, q_col = [_dot_tn(p, sel) for p in pieces]
        for h in range(GLA_HEADS):
            rs = slice(h * GLA_DK, (h + 1) * GLA_DK)
            vs = slice(h * GLA_DV, (h + 1) * GLA_DV)
            s_new = a_col[rs] * s_ref[bi, rs, :] + k_col[rs] * v[bi:bi + 1, vs]
            so_ref[bi, rs, :] = s_new
            o = jnp.sum(q_col[rs] * s_new, axis=0, keepdims=True)
            ms = jnp.mean(o * o, axis=-1, keepdims=True)
            g = o * lax.rsqrt(ms + EPS) * gn_ref[...]
            o_ref[bi:bi + 1, vs] = g * rb[bi:bi + 1, vs]


def _gla_sample(gq, gk, la, gv, rbs, gn, state):
    tb = SAMPLE_TB
    nb = DEC_BATCH // tb

    def row(w):
        return pl.BlockSpec((tb, w), lambda i: (i, 0))

    st_spec = pl.BlockSpec((tb, GK_W, GLA_DV), lambda i: (i, 0, 0))
    return pl.pallas_call(
        _gla_sample_kernel,
        grid=(nb,),
        in_specs=[row(GK_W), row(GK_W), row(GK_W), row(GV_W), row(GV_W),
                  pl.BlockSpec((1, GLA_DV), lambda i: (0, 0)), st_spec],
        out_specs=[row(GV_W), st_spec],
        out_shape=[
            jax.ShapeDtypeStruct((DEC_BATCH, GV_W), F32),
            jax.ShapeDtypeStruct((DEC_BATCH, GK_W, GLA_DV), F32),
        ],
        compiler_params=_params(("parallel",)),
        name="gla_sample",
    )(gq, gk, la, gv, rbs, gn, state)


def _merge_kernel(x_ref, a_ref, g_ref, sga_ref, sgb_ref, wpa_ref, wpb_ref, wo_ref,
                  g1_ref, sh_ref, sc_ref, n2_ref, *rest):
    with_router = len(rest) == 8
    x1_ref, h2_ref = rest[3:5] if with_router else rest
    ya = _dot(a_ref[...].astype(BF16), wpa_ref[...])
    yb = _dot(g_ref[...].astype(BF16), wpb_ref[...])
    merged = sga_ref[...].astype(F32) * ya + sgb_ref[...].astype(F32) * yb
    mix = _dot(merged.astype(BF16), wo_ref[...])
    x1 = x_ref[...] + g1_ref[...] * mix
    x1_ref[...] = x1
    ms = jnp.mean(x1 * x1, axis=-1, keepdims=True)
    h = x1 * lax.rsqrt(ms + EPS) * n2_ref[...]
    h2 = h * (1.0 + sc_ref[...]) + sh_ref[...]
    h2_ref[...] = h2.astype(h2_ref.dtype)
    if with_router:
        rw_ref, rbias_ref, tri_ref = rest[0:3]
        route_ref, route_t_ref, cnt_ref = rest[5:8]
        logits = _dot(h2.astype(BF16), rw_ref[...]) + rbias_ref[...]
        packed, counts = _route_pack(logits, tri_ref[...])
        route_ref[...] = packed
        route_t_ref[...] = packed.T[0:route_t_ref.shape[0], :]
        cnt_ref[...] = jnp.broadcast_to(counts, cnt_ref.shape)


def _mod_spec(tm, n, mod, per_row_mod):
    if per_row_mod:
        return pl.BlockSpec((tm, D_MODEL), lambda i, *_: (i, 0))
    tiles_per_mod = (n // mod.shape[0]) // tm
    return pl.BlockSpec((None, 1, D_MODEL), lambda i, *_: (i // tiles_per_mod, 0, 0))


def _merge(x, a, g, sga, sgb, wpa, wpb, wo, g1, sh2, sc2, n2, *, tm, per_row_mod, h2_dtype=BF16,
           router=None):
    n = x.shape[0]
    mod_spec = _mod_spec(tm, n, g1, per_row_mod)

    def row(w):
        return pl.BlockSpec((tm, w), lambda i: (i, 0))

    def const(shape):
        return pl.BlockSpec(shape, lambda i: (0,) * len(shape))

    in_specs = [row(D_MODEL), row(a.shape[1]), row(GV_W), row(D_MODEL), row(D_MODEL),
                const(wpa.shape), const(wpb.shape), const(wo.shape),
                mod_spec, mod_spec, mod_spec, const((1, D_MODEL))]
    out_specs = [row(D_MODEL), row(D_MODEL)]
    out_shape = [jax.ShapeDtypeStruct((n, D_MODEL), F32), jax.ShapeDtypeStruct((n, D_MODEL), h2_dtype)]
    args = [x, a, g, sga, sgb, wpa, wpb, wo, g1, sh2, sc2, n2]
    if router is not None:
        tri = jnp.asarray(np.tril(np.ones((tm, tm), np.float32), -1), BF16)
        in_specs += [const(router[0].shape), const(router[1].shape), const((tm, tm))]
        out_specs += [row(ROUTER_PAD), pl.BlockSpec((8, tm), lambda i: (0, i)),
                      pl.BlockSpec((None, 8, ROUTER_PAD), lambda i: (i, 0, 0))]
        out_shape += [jax.ShapeDtypeStruct((n, ROUTER_PAD), F32), jax.ShapeDtypeStruct((8, n), F32),
                      jax.ShapeDtypeStruct((n // tm, 8, ROUTER_PAD), F32)]
        args += [router[0], router[1], tri]
    return pl.pallas_call(
        _merge_kernel,
        grid=(n // tm,),
        in_specs=in_specs,
        out_specs=out_specs,
        out_shape=out_shape,
        compiler_params=_params(("parallel",)),
        name="merge_out",
    )(*args)


def _ffn_kernel(h_ref, x_ref, g2_ref, wg_ref, wu_ref, wd_ref, o_ref, acc_ref):
    f = pl.program_id(1)
    hb = h_ref[...]
    act = (_silu(_dot(hb, wg_ref[...])) * _dot(hb, wu_ref[...])).astype(BF16)
    y = _dot(act, wd_ref[...])

    @pl.when(f == 0)
    def _():
        acc_ref[...] = y

    @pl.when(f > 0)
    def _():
        acc_ref[...] += y

    @pl.when(f == pl.num_programs(1) - 1)
    def _():
        o_ref[...] = x_ref[...] + g2_ref[...] * acc_ref[...]


def _ffn(h2, x1, g2, wg, wu, wd, *, tm, tf, per_row_mod):
    n = h2.shape[0]
    mod_spec = _mod_spec(tm, n, g2, per_row_mod)
    return pl.pallas_call(
        _ffn_kernel,
        grid=(n // tm, D_FF // tf),
        in_specs=[
            pl.BlockSpec((tm, D_MODEL), lambda i, f: (i, 0)),
            pl.BlockSpec((tm, D_MODEL), lambda i, f: (i, 0)),
            mod_spec,
            pl.BlockSpec((D_MODEL, tf), lambda i, f: (0, f)),
            pl.BlockSpec((D_MODEL, tf), lambda i, f: (0, f)),
            pl.BlockSpec((tf, D_MODEL), lambda i, f: (f, 0)),
        ],
        out_specs=pl.BlockSpec((tm, D_MODEL), lambda i, f: (i, 0)),
        out_shape=jax.ShapeDtypeStruct((n, D_MODEL), F32),
        scratch_shapes=[pltpu.VMEM((tm, D_MODEL), F32)],
        compiler_params=_params(("parallel", "arbitrary")),
        name="ffn_dense",
    )(h2, x1, g2, wg, wu, wd)


def _moe_kernel(h_ref, x_ref, g2_ref, rw_ref, rbias_ref, wg_ref, wu_ref, wd_ref, o_ref, acc_ref, gate_ref):
    e = pl.program_id(1)
    hb = h_ref[...]
    tm = hb.shape[0]
    lane = lax.broadcasted_iota(jnp.int32, (tm, ROUTER_PAD), 1).astype(F32)

    @pl.when(e == 0)
    def _():
        logits = _dot(hb, rw_ref[...]) + rbias_ref[...]
        i1, i2, p1, p2 = _top2(logits, lane)
        gate_ref[...] = jnp.where(lane == i1, p1, 0.0) + jnp.where(lane == i2, p2, 0.0)
        acc_ref[...] = jnp.zeros_like(acc_ref)

    ge = jnp.sum(jnp.where(lane == e.astype(F32), gate_ref[...], 0.0), axis=-1, keepdims=True)
    act = (_silu(_dot(hb, wg_ref[...])) * _dot(hb, wu_ref[...])).astype(BF16)
    acc_ref[...] += ge * _dot(act, wd_ref[...])

    @pl.when(e == pl.num_programs(1) - 1)
    def _():
        o_ref[...] = x_ref[...] + g2_ref[...] * acc_ref[...]


def _moe(h2, x1, g2, rw, rbias, wg, wu, wd, *, tm, per_row_mod):
    n = h2.shape[0]
    mod_spec = _mod_spec(tm, n, g2, per_row_mod)
    fe = D_FF_EXPERT
    return pl.pallas_call(
        _moe_kernel,
        grid=(n // tm, N_EXPERTS),
        in_specs=[
            pl.BlockSpec((tm, D_MODEL), lambda i, e: (i, 0)),
            pl.BlockSpec((tm, D_MODEL), lambda i, e: (i, 0)),
            mod_spec,
            pl.BlockSpec((D_MODEL, ROUTER_PAD), lambda i, e: (0, 0)),
            pl.BlockSpec((1, ROUTER_PAD), lambda i, e: (0, 0)),
            pl.BlockSpec((None, D_MODEL, fe), lambda i, e: (e, 0, 0)),
            pl.BlockSpec((None, D_MODEL, fe), lambda i, e: (e, 0, 0)),
            pl.BlockSpec((None, fe, D_MODEL), lambda i, e: (e, 0, 0)),
        ],
        out_specs=pl.BlockSpec((tm, D_MODEL), lambda i, e: (i, 0)),
        out_shape=jax.ShapeDtypeStruct((n, D_MODEL), F32),
        scratch_shapes=[pltpu.VMEM((tm, D_MODEL), F32), pltpu.VMEM((tm, ROUTER_PAD), F32)],
        compiler_params=_params(("parallel", "arbitrary")),
        name="moe",
    )(h2, x1, g2, rw, rbias, wg, wu, wd)


def _top2(logits, lane):
    lg = jnp.where(lane < N_EXPERTS, logits, -jnp.inf)
    m1 = jnp.max(lg, axis=-1, keepdims=True)
    i1 = jnp.min(jnp.where(lg == m1, lane, float(ROUTER_PAD)), axis=-1, keepdims=True)
    lg2 = jnp.where(lane == i1, -jnp.inf, lg)
    m2 = jnp.max(lg2, axis=-1, keepdims=True)
    i2 = jnp.min(jnp.where(lg2 == m2, lane, float(ROUTER_PAD)), axis=-1, keepdims=True)
    e2 = jnp.exp(m2 - m1)
    p1 = 1.0 / (1.0 + e2)
    return i1, i2, p1, e2 * p1


def _route_pack(logits, tri):
    lane = lax.broadcasted_iota(jnp.int32, logits.shape, 1).astype(F32)
    i1, i2, p1, p2 = _top2(logits, lane)
    oh1 = jnp.where(lane == i1, 1.0, 0.0)
    oh2 = jnp.where(lane == i2, 1.0, 0.0)
    cnt1 = jnp.sum(oh1, axis=0, keepdims=True)
    cnt2 = jnp.sum(oh2, axis=0, keepdims=True)
    rank1 = jnp.sum(_dot(tri, oh1.astype(BF16)) * oh1, axis=-1, keepdims=True)
    rank2 = jnp.sum((_dot(tri, oh2.astype(BF16)) + cnt1) * oh2, axis=-1, keepdims=True)
    packed = jnp.zeros_like(logits)
    for k, val in enumerate((p1, p2, i1, i2, rank1, rank2)):
        packed = jnp.where(lane == float(k), val, packed)
    return packed, cnt1 + cnt2


def _route_tables(route_t, tile_counts, tm, n_tiles):
    n = route_t.shape[1]
    tile_cnt = tile_counts[:, 0, :N_EXPERTS].astype(jnp.int32)
    cnt = jnp.sum(tile_cnt, axis=0)
    gsz = ((cnt + tm - 1) // tm) * tm
    gend = jnp.cumsum(gsz)
    seg_start = (gend - gsz)[None, :] + jnp.cumsum(tile_cnt, axis=0) - tile_cnt
    seg_of_token = jnp.repeat(seg_start.T, n // tile_cnt.shape[0], axis=1)
    experts = jnp.arange(N_EXPERTS, dtype=jnp.int32)[:, None]
    pos = []
    for k in range(2):
        e_k = route_t[2 + k].astype(jnp.int32)
        rank_k = route_t[4 + k].astype(jnp.int32)
        pos.append(jnp.sum(jnp.where(e_k[None, :] == experts, seg_of_token, 0), axis=0) + rank_k)
    pos = jnp.concatenate(pos)
    tile_start = jnp.arange(n_tiles, dtype=jnp.int32) * tm
    tile_expert = jnp.sum((tile_start[:, None] >= gend[None, :]).astype(jnp.int32), axis=1)
    tile_expert = jnp.minimum(tile_expert, N_EXPERTS - 1)
    live = (gend[-1] // tm).reshape(1)
    return pos, tile_expert, live, gend - gsz + cnt, gend


def _row_copy_wait(src_hbm, dst, sem, rows):
    pltpu.make_async_copy(src_hbm.at[pl.ds(0, rows)], dst, sem).wait()


def _moe_dispatch_kernel(pos_ref, pad_lo_ref, pad_hi_ref, live_ref, h_ref, xs_hbm, zbuf, sem, zsem):
    i = pl.program_id(0)
    tt = h_ref.shape[0]
    n = tt * pl.num_programs(0)
    tm = zbuf.shape[0]

    base = i * tt
    for r in range(tt):
        row = h_ref.at[pl.ds(r, 1)]
        pltpu.make_async_copy(row, xs_hbm.at[pl.ds(pos_ref[base + r], 1)], sem).start(priority=0)
        pltpu.make_async_copy(row, xs_hbm.at[pl.ds(pos_ref[n + base + r], 1)], sem).start(priority=1)
    for _ in range(2):
        pltpu.make_async_copy(h_ref, xs_hbm.at[pl.ds(0, tt)], sem).wait()

    @pl.when(i == pl.num_programs(0) - 1)
    def _():
        zbuf[...] = jnp.zeros(zbuf.shape, F32)

        def fill_row(p):
            return pltpu.make_async_copy(zbuf.at[pl.ds(0, 1)], xs_hbm.at[pl.ds(p, 1)], zsem)

        def fill_tile(t):
            return pltpu.make_async_copy(zbuf, xs_hbm.at[pl.ds(pl.multiple_of(t * tm, tm), tm)], zsem)

        for e in range(N_EXPERTS):
            lo, hi = pad_lo_ref[e], pad_hi_ref[e]
            lax.fori_loop(lo, hi, lambda p, c: (fill_row(p).start(), c)[1], 0)
            lax.fori_loop(lo, hi, lambda p, c: (fill_row(p).wait(), c)[1], 0)
        lo, hi = live_ref[0], xs_hbm.shape[0] // tm
        lax.fori_loop(lo, hi, lambda t, c: (fill_tile(t).start(), c)[1], 0)
        lax.fori_loop(lo, hi, lambda t, c: (fill_tile(t).wait(), c)[1], 0)


def _moe_dispatch(h2, pos, pad_lo, pad_hi, live, *, tt, tm, n_tiles):
    n = h2.shape[0]
    grid_spec = pltpu.PrefetchScalarGridSpec(
        num_scalar_prefetch=4,
        grid=(n // tt,),
        in_specs=[pl.BlockSpec((tt, D_MODEL), lambda i, *_: (i, 0))],
        out_specs=pl.BlockSpec(memory_space=pl.ANY),
        scratch_shapes=[pltpu.VMEM((tm, D_MODEL), F32), pltpu.SemaphoreType.DMA(()),
                        pltpu.SemaphoreType.DMA(())],
    )
    return pl.pallas_call(
        _moe_dispatch_kernel,
        grid_spec=grid_spec,
        out_shape=jax.ShapeDtypeStruct((n_tiles * tm, D_MODEL), F32),
        compiler_params=_params(("arbitrary",)),
        name="moe_dispatch",
    )(pos, pad_lo, pad_hi, live, h2)


def _moe_expert_kernel(te_ref, live_ref, x_ref, wg_ref, wu_ref, wd_ref, y_ref):
    del te_ref
    is_live = pl.program_id(0) < live_ref[0]

    @pl.when(is_live)
    def _():
        hb = x_ref[...].astype(BF16)
        act = (_silu(_dot(hb, wg_ref[...])) * _dot(hb, wu_ref[...])).astype(BF16)
        y_ref[...] = _dot(act, wd_ref[...])

    @pl.when(jnp.logical_not(is_live))
    def _():
        y_ref[...] = jnp.zeros(y_ref.shape, F32)


def _moe_experts(xs, tile_expert, live, wg, wu, wd, *, tm, n_tiles):
    fe = D_FF_EXPERT

    def in_tile(t, te, live):
        return (jnp.minimum(t, live[0] - 1), 0)

    def out_tile(t, te, live):
        return (t, 0)

    def expert(t, te, live):
        return (te[t], 0, 0)

    grid_spec = pltpu.PrefetchScalarGridSpec(
        num_scalar_prefetch=2,
        grid=(n_tiles,),
        in_specs=[
            pl.BlockSpec((tm, D_MODEL), in_tile),
            pl.BlockSpec((None, D_MODEL, fe), expert),
            pl.BlockSpec((None, D_MODEL, fe), expert),
            pl.BlockSpec((None, fe, D_MODEL), expert),
        ],
        out_specs=pl.BlockSpec((tm, D_MODEL), out_tile),
    )
    return pl.pallas_call(
        _moe_expert_kernel,
        grid_spec=grid_spec,
        out_shape=jax.ShapeDtypeStruct(xs.shape, F32),
        compiler_params=_params(("arbitrary",)),
        name="moe_experts",
    )(tile_expert, live, xs, wg, wu, wd)


def _moe_combine_kernel(pos_ref, x_ref, g2_ref, r_ref, ys_hbm, o_ref, buf, sem):
    i = pl.program_id(0)
    nt = pl.num_programs(0)
    tt = x_ref.shape[0]
    n = nt * tt

    def fetch(tile, sl):
        base = tile * tt
        for r in range(tt):
            for s in range(2):
                row = pos_ref[s * n + base + r]
                pltpu.make_async_copy(ys_hbm.at[pl.ds(row, 1)], buf.at[sl, pl.ds(s * tt + r, 1)],
                                      sem.at[sl]).start(priority=s)

    @pl.when(i == 0)
    def _():
        fetch(0, 0)

    @pl.when(i + 1 < nt)
    def _():
        fetch(i + 1, (i + 1) % 2)

    sl = i % 2
    _row_copy_wait(ys_hbm, buf.at[sl], sem.at[sl], 2 * tt)
    r = r_ref[...]
    f = r[:, 0:1] * buf[sl, pl.ds(0, tt), :] + r[:, 1:2] * buf[sl, pl.ds(tt, tt), :]
    o_ref[...] = x_ref[...] + g2_ref[...] * f


def _moe_combine(x1, g2, route, pos, ys, *, tt):
    n = x1.shape[0]
    tiles_per_mod = (n // g2.shape[0]) // tt
    grid_spec = pltpu.PrefetchScalarGridSpec(
        num_scalar_prefetch=1,
        grid=(n // tt,),
        in_specs=[
            pl.BlockSpec((tt, D_MODEL), lambda i, pos: (i, 0)),
            pl.BlockSpec((None, 1, D_MODEL), lambda i, pos: (i // tiles_per_mod, 0, 0)),
            pl.BlockSpec((tt, ROUTER_PAD), lambda i, pos: (i, 0)),
            pl.BlockSpec(memory_space=pl.ANY),
        ],
        out_specs=pl.BlockSpec((tt, D_MODEL), lambda i, pos: (i, 0)),
        scratch_shapes=[pltpu.VMEM((2, 2 * tt, D_MODEL), F32), pltpu.SemaphoreType.DMA((2,))],
    )
    return pl.pallas_call(
        _moe_combine_kernel,
        grid_spec=grid_spec,
        out_shape=jax.ShapeDtypeStruct((n, D_MODEL), F32),
        compiler_params=_params(("arbitrary",)),
        name="moe_combine",
    )(pos, x1, g2, route, ys)


def _moe_routed(h2, x1, g2, route, route_t, tile_counts, wg, wu, wd, *, tm, tt):
    n = h2.shape[0]
    n_tiles = (2 * n) // tm + N_EXPERTS
    pos, tile_expert, live, pad_lo, pad_hi = _route_tables(route_t, tile_counts, tm, n_tiles)
    xs = _moe_dispatch(h2, pos, pad_lo, pad_hi, live, tt=tt, tm=tm, n_tiles=n_tiles)
    ys = _moe_experts(xs, tile_expert, live, wg, wu, wd, tm=tm, n_tiles=n_tiles)
    return _moe_combine(x1, g2, route, pos, ys, tt=tt)


def _head_perm():
    idx = []
    for g in range(GROUP):
        for kv in range(N_KV):
            h = kv * GROUP + g
            idx.extend(range(h * HEAD_DIM, (h + 1) * HEAD_DIM))
    return np.asarray(idx, np.int32)


def _relayout_w_in(w):
    pts = np.cumsum([0, Q_W, KV_W, KV_W, GK_W, GK_W, GV_W, GATE_RANK, GV_W, D_MODEL, D_MODEL])
    qa, ka, va, qb, kb, vb, ga, rb, gta, gtb = [w[:, pts[i]:pts[i + 1]] for i in range(10)]
    qa = qa[:, _head_perm()]
    ga = jnp.pad(ga, ((0, 0), (0, GATE_PAD - GATE_RANK)))
    return jnp.concatenate([qa, ka, va, qb, kb, vb, rb, gta, gtb, ga], axis=1).astype(BF16)


def kernel(x_prompt, x_sample, cache_k, cache_v, state_gla, c_prompt, c_sample, ada_w, ada_b, norm1_g, norm2_g, w_in, q_norm_g, k_norm_g, attn_sinks, gla_wa2, gla_ba, gla_norm_g, w_branch_a, w_branch_b, w_out, ffn_w_gate, ffn_w_up, ffn_w_down, router_w, router_b, moe_w_gate, moe_w_up, moe_w_down):
    n_p = BATCH * SEQ
    xp = x_prompt.reshape(n_p, D_MODEL)
    xs = x_sample.reshape(DEC_BATCH, D_MODEL)

    c_rows = BATCH + DEC_BATCH
    c_pad = -c_rows % 8
    c_all = jnp.pad(jnp.concatenate([c_prompt, c_sample], axis=0), ((0, c_pad), (0, 0)))
    mod = _ada(c_all, ada_w, ada_b)

    bd = jnp.asarray(np.kron(np.eye(N_HEADS), np.ones((HEAD_DIM, HEAD_DIM))), BF16)
    ut = jnp.asarray(_gla_cum_matrix(), BF16)
    gla_upper, gla_pairs = (jnp.asarray(m) for m in _gla_level_masks())
    perm = _head_perm()
    swa_bias = jnp.asarray(_swa_sample_bias())

    kp_l, vp_l, sp_l, ks_l, vs_l, ss_l = [], [], [], [], [], []
    for l in range(DEPTH):
        m = mod[l].reshape(c_rows + c_pad, 6, D_MODEL)
        mod_p = [m[:BATCH, i].reshape(BATCH, 1, D_MODEL) for i in range(6)]
        mod_s = [m[BATCH:c_rows, i] for i in range(6)]

        w = _relayout_w_in(w_in[l])
        qg = (jnp.tile(q_norm_g[l], N_HEADS) * (HEAD_DIM ** -0.5)).reshape(1, Q_W)
        kg = jnp.tile(k_norm_g[l], N_KV).reshape(1, KV_W)
        wa2 = jnp.pad(gla_wa2[l], ((0, GATE_PAD - GATE_RANK), (0, 0))).astype(BF16)
        ba = gla_ba[l].reshape(1, GK_W)
        n1 = norm1_g[l].reshape(1, D_MODEL)
        n2 = norm2_g[l].reshape(1, D_MODEL)
        gn = gla_norm_g[l].reshape(1, GLA_DV)
        wpa = w_branch_a[l][perm].astype(BF16)
        wpb = w_branch_b[l].astype(BF16)
        wo = w_out[l].astype(BF16)
        sink_rows = jnp.broadcast_to(attn_sinks[l][_sample_head_of_row()][:, None], (N_HEADS, LANES))

        q, k, v, gq, gk, gv, la, rbs, sga, sgb = _mixin(
            xp, mod_p[0], mod_p[1], n1, w, bd, qg, kg, wa2, ba, tm=1024, per_row_mod=False)
        a_out = _swa_prompt(attn_sinks[l], q, k, v)
        g_out, s_fin = _gla_prompt(gq, gk, la, gv, rbs, gn, ut, gla_upper, gla_pairs)
        is_moe = l % 2 == 1
        if is_moe:
            rw = jnp.pad(router_w[l // 2], ((0, 0), (0, ROUTER_PAD - N_EXPERTS))).astype(BF16)
            rbias = jnp.pad(router_b[l // 2], (0, ROUTER_PAD - N_EXPERTS)).reshape(1, ROUTER_PAD)
        merged = _merge(xp, a_out, g_out, sga, sgb, wpa, wpb, wo, mod_p[2], mod_p[3], mod_p[4], n2,
                        tm=512, per_row_mod=False, h2_dtype=F32 if is_moe else BF16,
                        router=(rw, rbias) if is_moe else None)
        x1, h2 = merged[0], merged[1]
        kp_l.append(k.reshape(BATCH, SEQ, N_KV, HEAD_DIM)[:, SEQ - WINDOW:])
        vp_l.append(v.reshape(BATCH, SEQ, N_KV, HEAD_DIM)[:, SEQ - WINDOW:])
        sp_l.append(s_fin.reshape(BATCH, GLA_HEADS, GLA_DK, GLA_DV))

        qs, ksn, vsn, gqs, gks, gvs, las, rbss, sgas, sgbs = _mixin(
            xs, mod_s[0], mod_s[1], n1, w, bd, qg, kg, wa2, ba, tm=DEC_BATCH, per_row_mod=True)
        a_s, nk, nv = _swa_sample(
            qs.astype(F32).reshape(DEC_BATCH, N_HEADS, HEAD_DIM),
            ksn.reshape(DEC_BATCH, N_KV, HEAD_DIM), vsn.reshape(DEC_BATCH, N_KV, HEAD_DIM),
            cache_k[l].reshape(DEC_BATCH, CACHE_ROWS, HEAD_DIM),
            cache_v[l].reshape(DEC_BATCH, CACHE_ROWS, HEAD_DIM), swa_bias, sink_rows)
        g_s, s_new = _gla_sample(gqs, gks, las, gvs, rbss, gn, state_gla[l].reshape(DEC_BATCH, GK_W, GLA_DV))
        x1s, h2s = _merge(xs, a_s.reshape(DEC_BATCH, Q_W), g_s, sgas, sgbs, wpa, wpb, wo,
                          mod_s[2], mod_s[3], mod_s[4], n2, tm=DEC_BATCH, per_row_mod=True)
        ks_l.append(nk.reshape(DEC_BATCH, WINDOW, N_KV, HEAD_DIM))
        vs_l.append(nv.reshape(DEC_BATCH, WINDOW, N_KV, HEAD_DIM))
        ss_l.append(s_new.reshape(DEC_BATCH, GLA_HEADS, GLA_DK, GLA_DV))

        i = l // 2
        if not is_moe:
            wg, wu, wd = ffn_w_gate[i].astype(BF16), ffn_w_up[i].astype(BF16), ffn_w_down[i].astype(BF16)
            xp = _ffn(h2, x1, mod_p[5], wg, wu, wd, tm=512, tf=D_FF // 2, per_row_mod=False)
            xs = _ffn(h2s, x1s, mod_s[5], wg, wu, wd, tm=DEC_BATCH, tf=D_FF // 2, per_row_mod=True)
        else:
            wg, wu, wd = moe_w_gate[i].astype(BF16), moe_w_up[i].astype(BF16), moe_w_down[i].astype(BF16)
            xp = _moe_routed(h2, x1, mod_p[5], *merged[2:5], wg, wu, wd, tm=512, tt=512)
            xs = _moe(h2s, x1s, mod_s[5], rw, rbias, wg, wu, wd, tm=DEC_BATCH, per_row_mod=True)

    return (xp.reshape(BATCH, SEQ, D_MODEL), xs.reshape(DEC_BATCH, 1, D_MODEL),
            jnp.stack(kp_l), jnp.stack(vp_l), jnp.stack(sp_l),
            jnp.stack(ks_l), jnp.stack(vs_l), jnp.stack(ss_l))
```

```python
import functools

import jax
import jax.numpy as jnp
import numpy as np
from jax import lax
from jax.experimental import pallas as pl
from jax.experimental.pallas import tpu as pltpu

D_MODEL = 1024
BATCH = 4
SEQ = 4096
DEPTH = 2
DEC_BATCH = 128
N_HEADS = 8
N_KV = 2
HEAD_DIM = 64
GROUP = N_HEADS // N_KV
WINDOW = 128
GLA_HEADS = 4
GLA_DK = 64
GLA_DV = 128
GATE_RANK = 16
GATE_TAU = 16.0
D_FF = 2816
N_EXPERTS = 8
D_FF_EXPERT = 1408
EPS = 1e-6

Q_W = N_HEADS * HEAD_DIM
KV_W = N_KV * HEAD_DIM
GK_W = GLA_HEADS * GLA_DK
GV_W = GLA_HEADS * GLA_DV

LANES = 128
GATE_PAD = LANES
ROUTER_PAD = LANES
VMEM_LIMIT = 56 * 1024 * 1024

F32 = jnp.float32
BF16 = jnp.bfloat16

_C_Q = 0
_C_K = _C_Q + Q_W
_C_V = _C_K + KV_W
_C_GQ = _C_V + KV_W
_C_GK = _C_GQ + GK_W
_C_GV = _C_GK + GK_W
_C_RB = _C_GV + GV_W
_C_GA = _C_RB + GV_W
_C_GB = _C_GA + D_MODEL
_C_LR = _C_GB + D_MODEL
PROJ_PAD = _C_LR + GATE_PAD

GLA_CHUNK = 128
GLA_LEVELS = 7
GLA_MXU_LEVELS = 3
GLA_CHUNKS_PER_STEP = 4


def _params(sem, vmem=VMEM_LIMIT):
    return pltpu.CompilerParams(dimension_semantics=sem, vmem_limit_bytes=vmem)


def _dot(a, b):
    return jnp.dot(a, b, preferred_element_type=F32)


def _dot_nt(a, b):
    return lax.dot_general(a, b, (((1,), (1,)), ((), ())), preferred_element_type=F32)


def _dot_tn(a, b):
    return lax.dot_general(a, b, (((0,), (0,)), ((), ())), preferred_element_type=F32)


def _sigmoid(x):
    return 0.5 * jnp.tanh(0.5 * x) + 0.5


def _silu(x):
    return x * _sigmoid(x)


def _ada_kernel(c_ref, w_ref, b_ref, o_ref):
    c = c_ref[...]
    o_ref[...] = _dot(_silu(c).astype(BF16), w_ref[...].astype(BF16)) + b_ref[...]


def _ada(c_all, ada_w, ada_b):
    rows = c_all.shape[0]
    tn = 1024
    return pl.pallas_call(
        _ada_kernel,
        grid=(DEPTH, 6 * D_MODEL // tn),
        in_specs=[
            pl.BlockSpec((rows, D_MODEL), lambda l, j: (0, 0)),
            pl.BlockSpec((None, D_MODEL, tn), lambda l, j: (l, 0, j)),
            pl.BlockSpec((None, 1, tn), lambda l, j: (l, 0, j)),
        ],
        out_specs=pl.BlockSpec((None, rows, tn), lambda l, j: (l, 0, j)),
        out_shape=jax.ShapeDtypeStruct((DEPTH, rows, 6 * D_MODEL), F32),
        compiler_params=_params(("parallel", "parallel")),
        name="ada_mod",
    )(c_all, ada_w, ada_b.reshape(DEPTH, 1, 6 * D_MODEL))


def _mixin_kernel(x_ref, sh_ref, sc_ref, n1_ref, w_ref, bd_ref, qg_ref, kg_ref, wa2_ref, ba_ref,
                  q_ref, k_ref, v_ref, gq_ref, gk_ref, gv_ref, la_ref, rb_ref, sga_ref, sgb_ref):
    x = x_ref[...]
    ms = jnp.mean(x * x, axis=-1, keepdims=True)
    h = x * lax.rsqrt(ms + EPS) * n1_ref[...]
    h = h * (1.0 + sc_ref[...]) + sh_ref[...]
    hb = h.astype(BF16)

    def proj(a, b):
        return _dot(hb, w_ref[:, a:b])

    q = proj(_C_Q, _C_K)
    ssq = _dot((q * q).astype(BF16), bd_ref[...])
    q_ref[...] = (q * lax.rsqrt(ssq * (1.0 / HEAD_DIM) + EPS) * qg_ref[...]).astype(BF16)
    k = proj(_C_K, _C_V)
    ssk = _dot((k * k).astype(BF16), bd_ref[0:KV_W, 0:KV_W])
    k_ref[...] = k * lax.rsqrt(ssk * (1.0 / HEAD_DIM) + EPS) * kg_ref[...]
    v_ref[...] = proj(_C_V, _C_GQ)
    gq_ref[...] = proj(_C_GQ, _C_GK) * (GLA_DK ** -0.5)
    gk_ref[...] = proj(_C_GK, _C_GV)
    gv_ref[...] = proj(_C_GV, _C_RB)
    rb_ref[...] = _silu(proj(_C_RB, _C_GA)).astype(BF16)
    sga_ref[...] = _sigmoid(proj(_C_GA, _C_GB)).astype(BF16)
    sgb_ref[...] = _sigmoid(proj(_C_GB, _C_LR)).astype(BF16)
    ga = proj(_C_LR, PROJ_PAD)
    xg = _dot(ga.astype(BF16), wa2_ref[...]) + ba_ref[...]
    la_ref[...] = (jnp.minimum(xg, 0.0) - jnp.log1p(jnp.exp(-jnp.abs(xg)))) * (1.0 / GATE_TAU)


def _mixin(x, sh, sc, n1, w, bd, qg, kg, wa2, ba, *, tm, per_row_mod):
    n = x.shape[0]
    nt = n // tm
    mod_spec = _mod_spec(tm, n, sh, per_row_mod)

    def row(i):
        return (i, 0)

    def const(shape):
        return pl.BlockSpec(shape, lambda i: (0,) * len(shape), pipeline_mode=pl.Buffered(1))

    def out(width, dtype):
        return pl.BlockSpec((tm, width), row), jax.ShapeDtypeStruct((n, width), dtype)

    outs = [out(Q_W, BF16), out(KV_W, F32), out(KV_W, F32), out(GK_W, F32), out(GK_W, F32),
            out(GV_W, F32), out(GK_W, F32), out(GV_W, BF16), out(D_MODEL, BF16), out(D_MODEL, BF16)]
    return pl.pallas_call(
        _mixin_kernel,
        grid=(nt,),
        in_specs=[
            pl.BlockSpec((tm, D_MODEL), row), mod_spec, mod_spec, const((1, D_MODEL)),
            const((D_MODEL, PROJ_PAD)), const((Q_W, Q_W)), const((1, Q_W)), const((1, KV_W)),
            const((GATE_PAD, GK_W)), const((1, GK_W)),
        ],
        out_specs=[o[0] for o in outs],
        out_shape=[o[1] for o in outs],
        compiler_params=_params(("parallel",)),
        name="mixer_in",
    )(x, sh, sc, n1, w, bd, qg, kg, wa2, ba)


SWA_BLOCKS = 8


def _head_slope(h):
    return float(2.0 ** (-8.0 * (h + 1) / N_HEADS))


def _swa_prompt_bias():
    blk = WINDOW
    dist = np.arange(blk)[:, None] + blk - np.arange(2 * blk)[None, :]
    slopes = np.asarray([_head_slope(h) for h in range(N_HEADS)])[:, None, None]
    return np.where((dist >= 0) & (dist <= WINDOW), -slopes * dist, -np.inf).astype(np.float32)


def _swa_prompt_kernel(sink_ref, q_ref, kp_ref, kc_ref, vp_ref, vc_ref, bias_ref, o_ref):
    n = pl.program_id(1)
    blk = WINDOW
    kall = jnp.concatenate([kp_ref[...], kc_ref[...]], axis=0).astype(BF16)
    vall = jnp.concatenate([vp_ref[...], vc_ref[...]], axis=0).astype(BF16)
    col = lax.broadcasted_iota(jnp.int32, (blk, 2 * blk), 1)
    first_key = jnp.where(n > 0, 0, blk)
    klane = lax.broadcasted_iota(jnp.int32, kall.shape, 1)
    kall_kv = [jnp.where((klane // HEAD_DIM) == kv, kall, jnp.zeros_like(kall)) for kv in range(N_KV)]
    olane = lax.broadcasted_iota(jnp.int32, (blk, KV_W), 1)
    q = q_ref[...]
    for j in range(SWA_BLOCKS):
        vv = vall[j * blk:(j + 2) * blk]
        outs = []
        for g in range(GROUP):
            qp = q[j * blk:(j + 1) * blk, g * KV_W:(g + 1) * KV_W]
            pair = []
            for kv in range(N_KV):
                h = kv * GROUP + g
                s = _dot_nt(qp, kall_kv[kv][j * blk:(j + 2) * blk]) + bias_ref[h]
                if j == 0:
                    s = jnp.where(col >= first_key, s, -jnp.inf)
                sink = sink_ref[h]
                m = jnp.maximum(jnp.max(s, axis=-1, keepdims=True), sink)
                p = jnp.exp(s - m)
                den = jnp.sum(p, axis=-1, keepdims=True) + jnp.exp(sink - m)
                pair.append(_dot(p.astype(BF16), vv) * (1.0 / den))
            outs.append(jnp.where(olane < HEAD_DIM, pair[0], pair[1]))
        o_ref[j * blk:(j + 1) * blk, :] = jnp.concatenate(outs, axis=-1).astype(BF16)


def _swa_prompt(sinks, q, k, v, bias):
    nb = SEQ // WINDOW
    steps = nb // SWA_BLOCKS
    tq = SWA_BLOCKS * WINDOW

    def cur(b, n):
        return (b * steps + n, 0)

    def prev(b, n):
        return (b * nb + jnp.maximum(n * SWA_BLOCKS - 1, 0), 0)

    return pl.pallas_call(
        _swa_prompt_kernel,
        grid=(BATCH, steps),
        in_specs=[
            pl.BlockSpec(memory_space=pltpu.SMEM),
            pl.BlockSpec((tq, Q_W), cur),
            pl.BlockSpec((WINDOW, KV_W), prev), pl.BlockSpec((tq, KV_W), cur),
            pl.BlockSpec((WINDOW, KV_W), prev), pl.BlockSpec((tq, KV_W), cur),
            pl.BlockSpec(bias.shape, lambda b, n: (0, 0, 0), pipeline_mode=pl.Buffered(1)),
        ],
        out_specs=pl.BlockSpec((tq, Q_W), cur),
        out_shape=jax.ShapeDtypeStruct((BATCH * SEQ, Q_W), BF16),
        compiler_params=_params(("parallel", "parallel")),
        name="swa_prompt",
    )(sinks, q, k, k, v, v, bias)


SAMPLE_TB = 8
CACHE_ROWS = WINDOW * N_KV


def _sample_head_of_row():
    j = np.arange(N_HEADS)
    return (j % N_KV) * GROUP + j // N_KV


def _swa_sample_bias():
    j = np.arange(N_HEADS)[:, None]
    c = np.arange(CACHE_ROWS)[None, :]
    slope = 2.0 ** (-8.0 * (_sample_head_of_row()[:, None] + 1) / N_HEADS)
    bias = -slope * (WINDOW - c // N_KV)
    return np.where(c % N_KV == j % N_KV, bias, -np.inf).astype(np.float32)


def _swa_sample_kernel(q_ref, kn_ref, vn_ref, ck_ref, cv_ref, bias_ref, sk_ref, o_ref, ok_ref, ov_ref):
    rows = CACHE_ROWS
    sink = sk_ref[...][:, 0:1]
    q = q_ref[...]
    kn = kn_ref[...]
    vn = vn_ref[...]
    kc = ck_ref[...]
    vc = cv_ref[...]
    kn8 = jnp.concatenate([kn] * GROUP, axis=1)
    vn8 = jnp.concatenate([vn] * GROUP, axis=1)
    s = lax.dot_general(q.astype(BF16), kc.astype(BF16), (((2,), (2,)), ((0,), (0,))),
                        preferred_element_type=F32) + bias_ref[...]
    s_new = jnp.sum(q * kn8, axis=-1, keepdims=True)
    m = jnp.maximum(jnp.maximum(jnp.max(s, axis=-1, keepdims=True), s_new), sink)
    p = jnp.exp(s - m)
    p_new = jnp.exp(s_new - m)
    den = jnp.sum(p, axis=-1, keepdims=True) + p_new + jnp.exp(sink - m)
    o = lax.dot_general(p.astype(BF16), vc.astype(BF16), (((2,), (1,)), ((0,), (0,))),
                        preferred_element_type=F32) + p_new * vn8
    o_ref[...] = o * (1.0 / den)
    ok_ref[:, pl.ds(0, rows - N_KV), :] = kc[:, N_KV:, :]
    ok_ref[:, pl.ds(rows - N_KV, N_KV), :] = kn
    ov_ref[:, pl.ds(0, rows - N_KV), :] = vc[:, N_KV:, :]
    ov_ref[:, pl.ds(rows - N_KV, N_KV), :] = vn


def _swa_sample(q, kn, vn, ck, cv, bias, sinks):
    tb = 2 * SAMPLE_TB
    nb = DEC_BATCH // tb

    def const(shape):
        return pl.BlockSpec(shape, lambda i: (0,) * len(shape))

    def per_seq(*dims):
        return pl.BlockSpec((tb,) + dims, lambda i: (i, 0, 0))

    return pl.pallas_call(
        _swa_sample_kernel,
        grid=(nb,),
        in_specs=[per_seq(N_HEADS, HEAD_DIM), per_seq(N_KV, HEAD_DIM), per_seq(N_KV, HEAD_DIM),
                  per_seq(CACHE_ROWS, HEAD_DIM), per_seq(CACHE_ROWS, HEAD_DIM),
                  const((N_HEADS, CACHE_ROWS)), const((N_HEADS, LANES))],
        out_specs=[per_seq(N_HEADS, HEAD_DIM), per_seq(CACHE_ROWS, HEAD_DIM), per_seq(CACHE_ROWS, HEAD_DIM)],
        out_shape=[
            jax.ShapeDtypeStruct((DEC_BATCH, N_HEADS, HEAD_DIM), F32),
            jax.ShapeDtypeStruct((DEC_BATCH, CACHE_ROWS, HEAD_DIM), F32),
            jax.ShapeDtypeStruct((DEC_BATCH, CACHE_ROWS, HEAD_DIM), F32),
        ],
        compiler_params=_params(("parallel",)),
        name="swa_sample",
    )(q, kn, vn, ck, cv, bias, sinks)


def _gla_cum_matrix():
    c = GLA_CHUNK
    tri = np.tril(np.ones((c, c), np.float32))
    i = np.arange(c)
    blocks = []
    for lvl in range(GLA_MXU_LEVELS):
        half = 1 << lvl
        mid = (i // (2 * half)) * (2 * half) + half - 1
        blocks.append(tri - tri[mid])
    blocks.append(tri)
    return np.concatenate(blocks, axis=0)


def _split3(x):
    hi = x.astype(BF16)
    r1 = x - hi.astype(F32)
    mid = r1.astype(BF16)
    lo = (r1 - mid.astype(F32)).astype(BF16)
    return hi, mid, lo


def _gla_level_masks():
    cl = GLA_CHUNK
    r = np.arange(cl)
    upper = np.stack([np.broadcast_to(((r >> lvl) & 1)[:, None], (cl, GK_W)) for lvl in range(GLA_LEVELS)])
    ri = np.tile(r, GLA_HEADS)[:, None]
    pairs = [(ri >> (lvl + 1)) == (r[None, :] >> (lvl + 1)) for lvl in range(GLA_LEVELS)]
    pairs.append(ri == r[None, :])
    return upper.astype(np.float32), np.stack(pairs).astype(np.float32)


def _gla_prompt_kernel(q_ref, k_ref, la_ref, v_ref, rb_ref, gn_ref, ut_ref, up_ref, pm_ref,
                       o_ref, s_ref, st_ref):
    c = pl.program_id(1)
    cl = GLA_CHUNK

    @pl.when(c == 0)
    def _():
        st_ref[0] = jnp.zeros(st_ref.shape[1:], F32)

    ut = ut_ref[...]
    lane = lax.broadcasted_iota(jnp.int32, (cl, GK_W), 1)
    head_of_lane = lane // GLA_DK
    ones = jnp.ones((cl, LANES), BF16)
    state = [st_ref[c % 2, h] for h in range(GLA_HEADS)]
    for cc in range(GLA_CHUNKS_PER_STEP):
        rows = slice(cc * cl, (cc + 1) * cl)
        hi, mid, lo = _split3(la_ref[rows, :])
        tall = _dot(ut, hi) + _dot(ut, mid) + _dot(ut, lo)
        q = q_ref[rows, :]
        k = k_ref[rows, :]
        attn_all = jnp.zeros((GLA_HEADS * cl, cl), F32)
        cum = tall[GLA_MXU_LEVELS * cl:(GLA_MXU_LEVELS + 1) * cl]
        for lvl in range(GLA_LEVELS + 1):
            if lvl < GLA_LEVELS:
                if lvl < GLA_MXU_LEVELS:
                    t_lvl = tall[lvl * cl:(lvl + 1) * cl]
                else:
                    half = 1 << lvl
                    mids = [jnp.broadcast_to(cum[b0 + half - 1:b0 + half, :], (2 * half, GK_W))
                            for b0 in range(0, cl, 2 * half)]
                    t_lvl = cum - (mids[0] if len(mids) == 1 else jnp.concatenate(mids, axis=0))
                e = jnp.exp(-jnp.abs(t_lvl))
                e_up = e * up_ref[lvl]
                qt = (q * e_up).astype(BF16)
                kt = (k * (e - e_up)).astype(BF16)
            else:
                qt = q.astype(BF16)
                kt = k.astype(BF16)
            q_heads = jnp.concatenate(
                [jnp.where(head_of_lane == h, qt, jnp.zeros_like(qt)) for h in range(GLA_HEADS)], axis=0)
            attn_all = attn_all + _dot_nt(q_heads, kt) * pm_ref[lvl]

        last = cum[cl - 1:cl, :]
        qe = (q * jnp.exp(cum)).astype(BF16)
        kd = (k * jnp.exp(last - cum)).astype(BF16)
        dec = jnp.exp(_dot_tn(hi, ones) + _dot_tn(mid, ones) + _dot_tn(lo, ones))
        v = v_ref[rows, :]
        rb = rb_ref[rows, :]
        outs = []
        for h in range(GLA_HEADS):
            vh = v[:, h * GLA_DV:(h + 1) * GLA_DV].astype(BF16)
            o = _dot(attn_all[h * cl:(h + 1) * cl].astype(BF16), vh) + _dot(qe, state[h].astype(BF16))
            kdh = jnp.where(head_of_lane == h, kd, jnp.zeros_like(kd))
            state[h] = state[h] * dec + _dot_tn(kdh, vh)
            ms = jnp.mean(o * o, axis=-1, keepdims=True)
            g = o * lax.rsqrt(ms + EPS) * gn_ref[...]
            outs.append(g * rb[:, h * GLA_DV:(h + 1) * GLA_DV].astype(F32))
        o_ref[rows, :] = jnp.concatenate(outs, axis=-1).astype(BF16)

    for h in range(GLA_HEADS):
        st_ref[(c + 1) % 2, h] = state[h]

    @pl.when(c == pl.num_programs(1) - 1)
    def _():
        fin = (SEQ // (GLA_CHUNK * GLA_CHUNKS_PER_STEP)) % 2
        s_ref[...] = st_ref[fin, 0] + st_ref[fin, 1] + st_ref[fin, 2] + st_ref[fin, 3]


def _gla_prompt(gq, gk, la, gv, rbs, gn, ut, upper, pairs):
    cl = GLA_CHUNK * GLA_CHUNKS_PER_STEP
    nc = SEQ // cl

    def row(b, c):
        return (b * nc + c, 0)

    def const(shape):
        return pl.BlockSpec(shape, lambda b, c: (0,) * len(shape), pipeline_mode=pl.Buffered(1))

    return pl.pallas_call(
        _gla_prompt_kernel,
        grid=(BATCH, nc),
        in_specs=[
            pl.BlockSpec((cl, GK_W), row), pl.BlockSpec((cl, GK_W), row), pl.BlockSpec((cl, GK_W), row),
            pl.BlockSpec((cl, GV_W), row), pl.BlockSpec((cl, GV_W), row),
            const((1, GLA_DV)), const(ut.shape), const(upper.shape), const(pairs.shape),
        ],
        out_specs=[
            pl.BlockSpec((cl, GV_W), row),
            pl.BlockSpec((None, GK_W, GLA_DV), lambda b, c: (b, 0, 0)),
        ],
        out_shape=[
            jax.ShapeDtypeStruct((BATCH * SEQ, GV_W), BF16),
            jax.ShapeDtypeStruct((BATCH, GK_W, GLA_DV), F32),
        ],
        scratch_shapes=[pltpu.VMEM((2, GLA_HEADS, GK_W, GLA_DV), F32)],
        compiler_params=_params(("parallel", "arbitrary")),
        name="gla_prompt",
    )(gq, gk, la, gv, rbs, gn, ut, upper, pairs)


def _gla_sample_kernel(q_ref, k_ref, la_ref, v_ref, rb_ref, gn_ref, s_ref, o_ref, so_ref):
    tb = SAMPLE_TB
    dec = jnp.exp(la_ref[...])
    pieces = []
    for x in (dec, k_ref[...], q_ref[...]):
        hi, mid, lo = _split3(x)
        stacked = jnp.concatenate(
            [hi.astype(F32), mid.astype(F32), lo.astype(F32), jnp.zeros_like(x)], axis=0)
        pieces.append(stacked.astype(BF16))
    prow = lax.broadcasted_iota(jnp.int32, (4 * tb, LANES), 0)
    rb = rb_ref[...].astype(F32)
    v = v_ref[...]
    for bi in range(tb):
        sel = jnp.where((prow % tb) == bi, 1.0, 0.0).astype(BF16)
        a_col, k_col, q_col = [_dot_tn(p, sel) for p in pieces]
        for h in range(GLA_HEADS):
            rs = slice(h * GLA_DK, (h + 1) * GLA_DK)
            vs = slice(h * GLA_DV, (h + 1) * GLA_DV)
            s_new = a_col[rs] * s_ref[bi, rs, :] + k_col[rs] * v[bi:bi + 1, vs]
            so_ref[bi, rs, :] = s_new
            o = jnp.sum(q_col[rs] * s_new, axis=0, keepdims=True)
            ms = jnp.mean(o * o, axis=-1, keepdims=True)
            g = o * lax.rsqrt(ms + EPS) * gn_ref[...]
            o_ref[bi:bi + 1, vs] = g * rb[bi:bi + 1, vs]


def _gla_sample(gq, gk, la, gv, rbs, gn, state):
    tb = SAMPLE_TB
    nb = DEC_BATCH // tb

    def row(w):
        return pl.BlockSpec((tb, w), lambda i: (i, 0))

    st_spec = pl.BlockSpec((tb, GK_W, GLA_DV), lambda i: (i, 0, 0))
    return pl.pallas_call(
        _gla_sample_kernel,
        grid=(nb,),
        in_specs=[row(GK_W), row(GK_W), row(GK_W), row(GV_W), row(GV_W),
                  pl.BlockSpec((1, GLA_DV), lambda i: (0, 0)), st_spec],
        out_specs=[row(GV_W), st_spec],
        out_shape=[
            jax.ShapeDtypeStruct((DEC_BATCH, GV_W), F32),
            jax.ShapeDtypeStruct((DEC_BATCH, GK_W, GLA_DV), F32),
        ],
        compiler_params=_params(("parallel",)),
        name="gla_sample",
    )(gq, gk, la, gv, rbs, gn, state)


def _merge_kernel(x_ref, a_ref, g_ref, sga_ref, sgb_ref, wpa_ref, wpb_ref, wo_ref,
                  g1_ref, sh_ref, sc_ref, n2_ref, *rest):
    with_router = len(rest) == 8
    x1_ref, h2_ref = rest[3:5] if with_router else rest
    ya = _dot(a_ref[...].astype(BF16), wpa_ref[...])
    yb = _dot(g_ref[...].astype(BF16), wpb_ref[...])
    merged = sga_ref[...].astype(F32) * ya + sgb_ref[...].astype(F32) * yb
    mix = _dot(merged.astype(BF16), wo_ref[...])
    x1 = x_ref[...] + g1_ref[...] * mix
    x1_ref[...] = x1
    ms = jnp.mean(x1 * x1, axis=-1, keepdims=True)
    h = x1 * lax.rsqrt(ms + EPS) * n2_ref[...]
    h2 = h * (1.0 + sc_ref[...]) + sh_ref[...]
    h2_ref[...] = h2.astype(h2_ref.dtype)
    if with_router:
        rw_ref, rbias_ref, tri_ref = rest[0:3]
        route_ref, route_t_ref, cnt_ref = rest[5:8]
        logits = _dot(h2.astype(BF16), rw_ref[...]) + rbias_ref[...]
        packed, counts = _route_pack(logits, tri_ref[...])
        route_ref[...] = packed
        route_t_ref[...] = packed.T[0:route_t_ref.shape[0], :]
        cnt_ref[...] = jnp.broadcast_to(counts, cnt_ref.shape)


def _mod_spec(tm, n, mod, per_row_mod):
    if per_row_mod:
        return pl.BlockSpec((tm, D_MODEL), lambda i, *_: (i, 0))
    tiles_per_mod = (n // mod.shape[0]) // tm
    return pl.BlockSpec((None, 1, D_MODEL), lambda i, *_: (i // tiles_per_mod, 0, 0))


def _merge(x, a, g, sga, sgb, wpa, wpb, wo, g1, sh2, sc2, n2, *, tm, per_row_mod, h2_dtype=BF16,
           router=None):
    n = x.shape[0]
    mod_spec = _mod_spec(tm, n, g1, per_row_mod)

    def row(w):
        return pl.BlockSpec((tm, w), lambda i: (i, 0))

    def const(shape):
        return pl.BlockSpec(shape, lambda i: (0,) * len(shape))

    in_specs = [row(D_MODEL), row(a.shape[1]), row(GV_W), row(D_MODEL), row(D_MODEL),
                const(wpa.shape), const(wpb.shape), const(wo.shape),
                mod_spec, mod_spec, mod_spec, const((1, D_MODEL))]
    out_specs = [row(D_MODEL), row(D_MODEL)]
    out_shape = [jax.ShapeDtypeStruct((n, D_MODEL), F32), jax.ShapeDtypeStruct((n, D_MODEL), h2_dtype)]
    args = [x, a, g, sga, sgb, wpa, wpb, wo, g1, sh2, sc2, n2]
    if router is not None:
        tri = jnp.asarray(np.tril(np.ones((tm, tm), np.float32), -1), BF16)
        in_specs += [const(router[0].shape), const(router[1].shape), const((tm, tm))]
        out_specs += [row(ROUTER_PAD), pl.BlockSpec((8, tm), lambda i: (0, i)),
                      pl.BlockSpec((None, 8, ROUTER_PAD), lambda i: (i, 0, 0))]
        out_shape += [jax.ShapeDtypeStruct((n, ROUTER_PAD), F32), jax.ShapeDtypeStruct((8, n), F32),
                      jax.ShapeDtypeStruct((n // tm, 8, ROUTER_PAD), F32)]
        args += [router[0], router[1], tri]
    return pl.pallas_call(
        _merge_kernel,
        grid=(n // tm,),
        in_specs=in_specs,
        out_specs=out_specs,
        out_shape=out_shape,
        compiler_params=_params(("parallel",)),
        name="merge_out",
    )(*args)


def _swiglu(hb, wg_ref, wu_ref, wd_ref):
    act = (_silu(_dot(hb, wg_ref[...])) * _dot(hb, wu_ref[...])).astype(BF16)
    return _dot(act, wd_ref[...])


def _ffn_kernel(h_ref, x_ref, g2_ref, wg_ref, wu_ref, wd_ref, o_ref, acc_ref):
    f = pl.program_id(1)
    y = _swiglu(h_ref[...], wg_ref, wu_ref, wd_ref)

    @pl.when(f == 0)
    def _():
        acc_ref[...] = y

    @pl.when(f > 0)
    def _():
        acc_ref[...] += y

    @pl.when(f == pl.num_programs(1) - 1)
    def _():
        o_ref[...] = x_ref[...] + g2_ref[...] * acc_ref[...]


def _ffn(h2, x1, g2, wg, wu, wd, *, tm, tf, per_row_mod):
    n = h2.shape[0]
    mod_spec = _mod_spec(tm, n, g2, per_row_mod)
    return pl.pallas_call(
        _ffn_kernel,
        grid=(n // tm, D_FF // tf),
        in_specs=[
            pl.BlockSpec((tm, D_MODEL), lambda i, f: (i, 0)),
            pl.BlockSpec((tm, D_MODEL), lambda i, f: (i, 0)),
            mod_spec,
            pl.BlockSpec((D_MODEL, tf), lambda i, f: (0, f)),
            pl.BlockSpec((D_MODEL, tf), lambda i, f: (0, f)),
            pl.BlockSpec((tf, D_MODEL), lambda i, f: (f, 0)),
        ],
        out_specs=pl.BlockSpec((tm, D_MODEL), lambda i, f: (i, 0)),
        out_shape=jax.ShapeDtypeStruct((n, D_MODEL), F32),
        scratch_shapes=[pltpu.VMEM((tm, D_MODEL), F32)],
        compiler_params=_params(("parallel", "arbitrary")),
        name="ffn_dense",
    )(h2, x1, g2, wg, wu, wd)


def _moe_kernel(h_ref, x_ref, g2_ref, rw_ref, rbias_ref, wg_ref, wu_ref, wd_ref, o_ref, acc_ref, gate_ref):
    e = pl.program_id(1)
    hb = h_ref[...]
    tm = hb.shape[0]
    lane = lax.broadcasted_iota(jnp.int32, (tm, ROUTER_PAD), 1).astype(F32)

    @pl.when(e == 0)
    def _():
        logits = _dot(hb, rw_ref[...]) + rbias_ref[...]
        i1, i2, p1, p2 = _top2(logits, lane)
        gate_ref[...] = jnp.where(lane == i1, p1, 0.0) + jnp.where(lane == i2, p2, 0.0)
        acc_ref[...] = jnp.zeros_like(acc_ref)

    ge = jnp.sum(jnp.where(lane == e.astype(F32), gate_ref[...], 0.0), axis=-1, keepdims=True)
    act = (_silu(_dot(hb, wg_ref[...])) * _dot(hb, wu_ref[...])).astype(BF16)
    acc_ref[...] += ge * _dot(act, wd_ref[...])

    @pl.when(e == pl.num_programs(1) - 1)
    def _():
        o_ref[...] = x_ref[...] + g2_ref[...] * acc_ref[...]


def _moe(h2, x1, g2, rw, rbias, wg, wu, wd, *, tm, per_row_mod):
    n = h2.shape[0]
    mod_spec = _mod_spec(tm, n, g2, per_row_mod)
    fe = D_FF_EXPERT
    return pl.pallas_call(
        _moe_kernel,
        grid=(n // tm, N_EXPERTS),
        in_specs=[
            pl.BlockSpec((tm, D_MODEL), lambda i, e: (i, 0)),
            pl.BlockSpec((tm, D_MODEL), lambda i, e: (i, 0)),
            mod_spec,
            pl.BlockSpec((D_MODEL, ROUTER_PAD), lambda i, e: (0, 0)),
            pl.BlockSpec((1, ROUTER_PAD), lambda i, e: (0, 0)),
            pl.BlockSpec((None, D_MODEL, fe), lambda i, e: (e, 0, 0)),
            pl.BlockSpec((None, D_MODEL, fe), lambda i, e: (e, 0, 0)),
            pl.BlockSpec((None, fe, D_MODEL), lambda i, e: (e, 0, 0)),
        ],
        out_specs=pl.BlockSpec((tm, D_MODEL), lambda i, e: (i, 0)),
        out_shape=jax.ShapeDtypeStruct((n, D_MODEL), F32),
        scratch_shapes=[pltpu.VMEM((tm, D_MODEL), F32), pltpu.VMEM((tm, ROUTER_PAD), F32)],
        compiler_params=_params(("parallel", "arbitrary")),
        name="moe",
    )(h2, x1, g2, rw, rbias, wg, wu, wd)


def _top2(logits, lane):
    lg = jnp.where(lane < N_EXPERTS, logits, -jnp.inf)
    m1 = jnp.max(lg, axis=-1, keepdims=True)
    i1 = jnp.min(jnp.where(lg == m1, lane, float(ROUTER_PAD)), axis=-1, keepdims=True)
    lg2 = jnp.where(lane == i1, -jnp.inf, lg)
    m2 = jnp.max(lg2, axis=-1, keepdims=True)
    i2 = jnp.min(jnp.where(lg2 == m2, lane, float(ROUTER_PAD)), axis=-1, keepdims=True)
    e2 = jnp.exp(m2 - m1)
    p1 = 1.0 / (1.0 + e2)
    return i1, i2, p1, e2 * p1


def _route_pack(logits, tri):
    lane = lax.broadcasted_iota(jnp.int32, logits.shape, 1).astype(F32)
    i1, i2, p1, p2 = _top2(logits, lane)
    oh1 = jnp.where(lane == i1, 1.0, 0.0)
    oh2 = jnp.where(lane == i2, 1.0, 0.0)
    cnt1 = jnp.sum(oh1, axis=0, keepdims=True)
    cnt2 = jnp.sum(oh2, axis=0, keepdims=True)
    rank1 = jnp.sum(_dot(tri, oh1.astype(BF16)) * oh1, axis=-1, keepdims=True)
    rank2 = jnp.sum((_dot(tri, oh2.astype(BF16)) + cnt1) * oh2, axis=-1, keepdims=True)
    packed = jnp.zeros_like(logits)
    for k, val in enumerate((p1, p2, i1, i2, rank1, rank2)):
        packed = jnp.where(lane == float(k), val, packed)
    return packed, cnt1 + cnt2


def _route_tables(route_t, tile_counts, tm, n_tiles):
    n = route_t.shape[1]
    tile_cnt = tile_counts[:, 0, :N_EXPERTS].astype(jnp.int32)
    cnt = jnp.sum(tile_cnt, axis=0)
    gsz = ((cnt + tm - 1) // tm) * tm
    gend = jnp.cumsum(gsz)
    seg_start = (gend - gsz)[None, :] + jnp.cumsum(tile_cnt, axis=0) - tile_cnt
    seg_of_token = jnp.repeat(seg_start.T, n // tile_cnt.shape[0], axis=1)
    experts = jnp.arange(N_EXPERTS, dtype=jnp.int32)[:, None]
    pos = []
    for k in range(2):
        e_k = route_t[2 + k].astype(jnp.int32)
        rank_k = route_t[4 + k].astype(jnp.int32)
        pos.append(jnp.sum(jnp.where(e_k[None, :] == experts, seg_of_token, 0), axis=0) + rank_k)
    pos = jnp.concatenate(pos)
    tile_start = jnp.arange(n_tiles, dtype=jnp.int32) * tm
    tile_expert = jnp.sum((tile_start[:, None] >= gend[None, :]).astype(jnp.int32), axis=1)
    tile_expert = jnp.minimum(tile_expert, N_EXPERTS - 1)
    live = (gend[-1] // tm).reshape(1)
    return pos, tile_expert, live, gend - gsz + cnt, gend


def _row_copy_wait(src_hbm, dst, sem, rows):
    pltpu.make_async_copy(src_hbm.at[pl.ds(0, rows)], dst, sem).wait()


def _moe_dispatch_kernel(pos_ref, pad_lo_ref, pad_hi_ref, live_ref, h_ref, xs_hbm, zbuf, sem, zsem):
    i = pl.program_id(0)
    tt = h_ref.shape[0]
    n = tt * pl.num_programs(0)
    tm = zbuf.shape[0]

    base = i * tt
    for r in range(tt):
        row = h_ref.at[pl.ds(r, 1)]
        pltpu.make_async_copy(row, xs_hbm.at[pl.ds(pos_ref[base + r], 1)], sem).start(priority=0)
        pltpu.make_async_copy(row, xs_hbm.at[pl.ds(pos_ref[n + base + r], 1)], sem).start(priority=1)
    for _ in range(2):
        pltpu.make_async_copy(h_ref, xs_hbm.at[pl.ds(0, tt)], sem).wait()

    @pl.when(i == pl.num_programs(0) - 1)
    def _():
        zbuf[...] = jnp.zeros(zbuf.shape, F32)

        def fill_row(p):
            return pltpu.make_async_copy(zbuf.at[pl.ds(0, 1)], xs_hbm.at[pl.ds(p, 1)], zsem)

        def fill_tile(t):
            return pltpu.make_async_copy(zbuf, xs_hbm.at[pl.ds(pl.multiple_of(t * tm, tm), tm)], zsem)

        for e in range(N_EXPERTS):
            lo, hi = pad_lo_ref[e], pad_hi_ref[e]
            lax.fori_loop(lo, hi, lambda p, c: (fill_row(p).start(), c)[1], 0)
            lax.fori_loop(lo, hi, lambda p, c: (fill_row(p).wait(), c)[1], 0)
        lo, hi = live_ref[0], xs_hbm.shape[0] // tm
        lax.fori_loop(lo, hi, lambda t, c: (fill_tile(t).start(), c)[1], 0)
        lax.fori_loop(lo, hi, lambda t, c: (fill_tile(t).wait(), c)[1], 0)


def _moe_dispatch(h2, pos, pad_lo, pad_hi, live, *, tt, tm, n_tiles):
    n = h2.shape[0]
    grid_spec = pltpu.PrefetchScalarGridSpec(
        num_scalar_prefetch=4,
        grid=(n // tt,),
        in_specs=[pl.BlockSpec((tt, D_MODEL), lambda i, *_: (i, 0))],
        out_specs=pl.BlockSpec(memory_space=pl.ANY),
        scratch_shapes=[pltpu.VMEM((tm, D_MODEL), F32), pltpu.SemaphoreType.DMA(()),
                        pltpu.SemaphoreType.DMA(())],
    )
    return pl.pallas_call(
        _moe_dispatch_kernel,
        grid_spec=grid_spec,
        out_shape=jax.ShapeDtypeStruct((n_tiles * tm, D_MODEL), F32),
        compiler_params=_params(("arbitrary",)),
        name="moe_dispatch",
    )(pos, pad_lo, pad_hi, live, h2)


def _moe_expert_kernel(te_ref, live_ref, x_ref, wg_ref, wu_ref, wd_ref, y_ref):
    del te_ref
    is_live = pl.program_id(0) < live_ref[0]

    @pl.when(is_live)
    def _():
        y_ref[...] = _swiglu(x_ref[...].astype(BF16), wg_ref, wu_ref, wd_ref)

    @pl.when(jnp.logical_not(is_live))
    def _():
        y_ref[...] = jnp.zeros(y_ref.shape, F32)


def _moe_experts(xs, tile_expert, live, wg, wu, wd, *, tm, n_tiles):
    fe = D_FF_EXPERT

    def in_tile(t, te, live):
        return (jnp.minimum(t, live[0] - 1), 0)

    def out_tile(t, te, live):
        return (t, 0)

    def expert(t, te, live):
        return (te[t], 0, 0)

    grid_spec = pltpu.PrefetchScalarGridSpec(
        num_scalar_prefetch=2,
        grid=(n_tiles,),
        in_specs=[
            pl.BlockSpec((tm, D_MODEL), in_tile),
            pl.BlockSpec((None, D_MODEL, fe), expert),
            pl.BlockSpec((None, D_MODEL, fe), expert),
            pl.BlockSpec((None, fe, D_MODEL), expert),
        ],
        out_specs=pl.BlockSpec((tm, D_MODEL), out_tile),
    )
    return pl.pallas_call(
        _moe_expert_kernel,
        grid_spec=grid_spec,
        out_shape=jax.ShapeDtypeStruct(xs.shape, F32),
        compiler_params=_params(("arbitrary",)),
        name="moe_experts",
    )(tile_expert, live, xs, wg, wu, wd)


def _moe_combine_kernel(pos_ref, x_ref, g2_ref, r_ref, ys_hbm, o_ref, buf, sem):
    i = pl.program_id(0)
    nt = pl.num_programs(0)
    tt = x_ref.shape[0]
    n = nt * tt

    def fetch(tile, sl):
        base = tile * tt
        for r in range(tt):
            for s in range(2):
                row = pos_ref[s * n + base + r]
                pltpu.make_async_copy(ys_hbm.at[pl.ds(row, 1)], buf.at[sl, pl.ds(s * tt + r, 1)],
                                      sem.at[sl]).start(priority=s)

    @pl.when(i == 0)
    def _():
        fetch(0, 0)

    @pl.when(i + 1 < nt)
    def _():
        fetch(i + 1, (i + 1) % 2)

    sl = i % 2
    _row_copy_wait(ys_hbm, buf.at[sl], sem.at[sl], 2 * tt)
    r = r_ref[...]
    f = r[:, 0:1] * buf[sl, pl.ds(0, tt), :] + r[:, 1:2] * buf[sl, pl.ds(tt, tt), :]
    o_ref[...] = x_ref[...] + g2_ref[...] * f


def _moe_combine(x1, g2, route, pos, ys, *, tt):
    n = x1.shape[0]
    tiles_per_mod = (n // g2.shape[0]) // tt
    grid_spec = pltpu.PrefetchScalarGridSpec(
        num_scalar_prefetch=1,
        grid=(n // tt,),
        in_specs=[
            pl.BlockSpec((tt, D_MODEL), lambda i, pos: (i, 0)),
            pl.BlockSpec((None, 1, D_MODEL), lambda i, pos: (i // tiles_per_mod, 0, 0)),
            pl.BlockSpec((tt, ROUTER_PAD), lambda i, pos: (i, 0)),
            pl.BlockSpec(memory_space=pl.ANY),
        ],
        out_specs=pl.BlockSpec((tt, D_MODEL), lambda i, pos: (i, 0)),
        scratch_shapes=[pltpu.VMEM((2, 2 * tt, D_MODEL), F32), pltpu.SemaphoreType.DMA((2,))],
    )
    return pl.pallas_call(
        _moe_combine_kernel,
        grid_spec=grid_spec,
        out_shape=jax.ShapeDtypeStruct((n, D_MODEL), F32),
        compiler_params=_params(("arbitrary",)),
        name="moe_combine",
    )(pos, x1, g2, route, ys)


def _moe_routed(h2, x1, g2, route, route_t, tile_counts, wg, wu, wd, *, tm, tt):
    n = h2.shape[0]
    n_tiles = (2 * n) // tm + N_EXPERTS
    pos, tile_expert, live, pad_lo, pad_hi = _route_tables(route_t, tile_counts, tm, n_tiles)
    xs = _moe_dispatch(h2, pos, pad_lo, pad_hi, live, tt=tt, tm=tm, n_tiles=n_tiles)
    ys = _moe_experts(xs, tile_expert, live, wg, wu, wd, tm=tm, n_tiles=n_tiles)
    return _moe_combine(x1, g2, route, pos, ys, tt=tt)


def _head_perm():
    idx = []
    for g in range(GROUP):
        for kv in range(N_KV):
            h = kv * GROUP + g
            idx.extend(range(h * HEAD_DIM, (h + 1) * HEAD_DIM))
    return np.asarray(idx, np.int32)


def _relayout_w_in(w):
    pts = np.cumsum([0, Q_W, KV_W, KV_W, GK_W, GK_W, GV_W, GATE_RANK, GV_W, D_MODEL, D_MODEL])
    qa, ka, va, qb, kb, vb, ga, rb, gta, gtb = [w[:, pts[i]:pts[i + 1]] for i in range(10)]
    qa = qa[:, _head_perm()]
    ga = jnp.pad(ga, ((0, 0), (0, GATE_PAD - GATE_RANK)))
    return jnp.concatenate([qa, ka, va, qb, kb, vb, rb, gta, gtb, ga], axis=1).astype(BF16)


def kernel(x_prompt, x_sample, cache_k, cache_v, state_gla, c_prompt, c_sample, ada_w, ada_b, norm1_g, norm2_g, w_in, q_norm_g, k_norm_g, attn_sinks, gla_wa2, gla_ba, gla_norm_g, w_branch_a, w_branch_b, w_out, ffn_w_gate, ffn_w_up, ffn_w_down, router_w, router_b, moe_w_gate, moe_w_up, moe_w_down):
    n_p = BATCH * SEQ
    xp = x_prompt.reshape(n_p, D_MODEL)
    xs = x_sample.reshape(DEC_BATCH, D_MODEL)

    c_rows = BATCH + DEC_BATCH
    c_pad = -c_rows % 8
    c_all = jnp.pad(jnp.concatenate([c_prompt, c_sample], axis=0), ((0, c_pad), (0, 0)))
    mod = _ada(c_all, ada_w, ada_b)

    bd = jnp.asarray(np.kron(np.eye(N_HEADS), np.ones((HEAD_DIM, HEAD_DIM))), BF16)
    ut = jnp.asarray(_gla_cum_matrix(), BF16)
    gla_upper, gla_pairs = (jnp.asarray(m) for m in _gla_level_masks())
    perm = _head_perm()
    swa_bias = jnp.asarray(_swa_sample_bias())
    swa_prompt_bias = jnp.asarray(_swa_prompt_bias())

    kp_l, vp_l, sp_l, ks_l, vs_l, ss_l = [], [], [], [], [], []
    for l in range(DEPTH):
        m = mod[l].reshape(c_rows + c_pad, 6, D_MODEL)
        mod_p = [m[:BATCH, i].reshape(BATCH, 1, D_MODEL) for i in range(6)]
        mod_s = [m[BATCH:c_rows, i] for i in range(6)]

        w = _relayout_w_in(w_in[l])
        qg = (jnp.tile(q_norm_g[l], N_HEADS) * (HEAD_DIM ** -0.5)).reshape(1, Q_W)
        kg = jnp.tile(k_norm_g[l], N_KV).reshape(1, KV_W)
        wa2 = jnp.pad(gla_wa2[l], ((0, GATE_PAD - GATE_RANK), (0, 0))).astype(BF16)
        ba = gla_ba[l].reshape(1, GK_W)
        n1 = norm1_g[l].reshape(1, D_MODEL)
        n2 = norm2_g[l].reshape(1, D_MODEL)
        gn = gla_norm_g[l].reshape(1, GLA_DV)
        wpa = w_branch_a[l][perm].astype(BF16)
        wpb = w_branch_b[l].astype(BF16)
        wo = w_out[l].astype(BF16)
        sink_rows = jnp.broadcast_to(attn_sinks[l][_sample_head_of_row()][:, None], (N_HEADS, LANES))

        q, k, v, gq, gk, gv, la, rbs, sga, sgb = _mixin(
            xp, mod_p[0], mod_p[1], n1, w, bd, qg, kg, wa2, ba, tm=1024, per_row_mod=False)
        a_out = _swa_prompt(attn_sinks[l], q, k, v, swa_prompt_bias)
        g_out, s_fin = _gla_prompt(gq, gk, la, gv, rbs, gn, ut, gla_upper, gla_pairs)
        is_moe = l % 2 == 1
        if is_moe:
            rw = jnp.pad(router_w[l // 2], ((0, 0), (0, ROUTER_PAD - N_EXPERTS))).astype(BF16)
            rbias = jnp.pad(router_b[l // 2], (0, ROUTER_PAD - N_EXPERTS)).reshape(1, ROUTER_PAD)
        merged = _merge(xp, a_out, g_out, sga, sgb, wpa, wpb, wo, mod_p[2], mod_p[3], mod_p[4], n2,
                        tm=512, per_row_mod=False, h2_dtype=F32 if is_moe else BF16,
                        router=(rw, rbias) if is_moe else None)
        x1, h2 = merged[0], merged[1]
        kp_l.append(k.reshape(BATCH, SEQ, N_KV, HEAD_DIM)[:, SEQ - WINDOW:])
        vp_l.append(v.reshape(BATCH, SEQ, N_KV, HEAD_DIM)[:, SEQ - WINDOW:])
        sp_l.append(s_fin.reshape(BATCH, GLA_HEADS, GLA_DK, GLA_DV))

        qs, ksn, vsn, gqs, gks, gvs, las, rbss, sgas, sgbs = _mixin(
            xs, mod_s[0], mod_s[1], n1, w, bd, qg, kg, wa2, ba, tm=DEC_BATCH, per_row_mod=True)
        a_s, nk, nv = _swa_sample(
            qs.astype(F32).reshape(DEC_BATCH, N_HEADS, HEAD_DIM),
            ksn.reshape(DEC_BATCH, N_KV, HEAD_DIM), vsn.reshape(DEC_BATCH, N_KV, HEAD_DIM),
            cache_k[l].reshape(DEC_BATCH, CACHE_ROWS, HEAD_DIM),
            cache_v[l].reshape(DEC_BATCH, CACHE_ROWS, HEAD_DIM), swa_bias, sink_rows)
        g_s, s_new = _gla_sample(gqs, gks, las, gvs, rbss, gn, state_gla[l].reshape(DEC_BATCH, GK_W, GLA_DV))
        x1s, h2s = _merge(xs, a_s.reshape(DEC_BATCH, Q_W), g_s, sgas, sgbs, wpa, wpb, wo,
                          mod_s[2], mod_s[3], mod_s[4], n2, tm=DEC_BATCH, per_row_mod=True)
        ks_l.append(nk.reshape(DEC_BATCH, WINDOW, N_KV, HEAD_DIM))
        vs_l.append(nv.reshape(DEC_BATCH, WINDOW, N_KV, HEAD_DIM))
        ss_l.append(s_new.reshape(DEC_BATCH, GLA_HEADS, GLA_DK, GLA_DV))

        i = l // 2
        if not is_moe:
            wg, wu, wd = ffn_w_gate[i].astype(BF16), ffn_w_up[i].astype(BF16), ffn_w_down[i].astype(BF16)
            xp = _ffn(h2, x1, mod_p[5], wg, wu, wd, tm=512, tf=D_FF // 2, per_row_mod=False)
            xs = _ffn(h2s, x1s, mod_s[5], wg, wu, wd, tm=DEC_BATCH, tf=D_FF // 2, per_row_mod=True)
        else:
            wg, wu, wd = moe_w_gate[i].astype(BF16), moe_w_up[i].astype(BF16), moe_w_down[i].astype(BF16)
            xp = _moe_routed(h2, x1, mod_p[5], *merged[2:5], wg, wu, wd, tm=512, tt=512)
            xs = _moe(h2s, x1s, mod_s[5], rw, rbias, wg, wu, wd, tm=DEC_BATCH, per_row_mod=True)

    return (xp.reshape(BATCH, SEQ, D_MODEL), xs.reshape(DEC_BATCH, 1, D_MODEL),
            jnp.stack(kp_l), jnp.stack(vp_l), jnp.stack(sp_l),
            jnp.stack(ks_l), jnp.stack(vs_l), jnp.stack(ss_l))
```

```python
import functools
from typing import NamedTuple

import jax
import jax.numpy as jnp
import numpy as np
from jax import lax
from jax.experimental import pallas as pl
from jax.experimental.pallas import tpu as pltpu

D_MODEL = 1024
BATCH = 4
SEQ = 4096
DEPTH = 2
DEC_BATCH = 128
N_HEADS = 8
N_KV = 2
HEAD_DIM = 64
GROUP = N_HEADS // N_KV
WINDOW = 128
GLA_HEADS = 4
GLA_DK = 64
GLA_DV = 128
GATE_RANK = 16
GATE_TAU = 16.0
D_FF = 2816
N_EXPERTS = 8
D_FF_EXPERT = 1408
EPS = 1e-6

Q_W = N_HEADS * HEAD_DIM
KV_W = N_KV * HEAD_DIM
GK_W = GLA_HEADS * GLA_DK
GV_W = GLA_HEADS * GLA_DV

LANES = 128
GATE_PAD = LANES
ROUTER_PAD = LANES
VMEM_LIMIT = 56 * 1024 * 1024

F32 = jnp.float32
BF16 = jnp.bfloat16

_C_Q = 0
_C_K = _C_Q + Q_W
_C_V = _C_K + KV_W
_C_GQ = _C_V + KV_W
_C_GK = _C_GQ + GK_W
_C_GV = _C_GK + GK_W
_C_RB = _C_GV + GV_W
_C_GA = _C_RB + GV_W
_C_GB = _C_GA + D_MODEL
_C_LR = _C_GB + D_MODEL
PROJ_PAD = _C_LR + GATE_PAD

GLA_CHUNK = 128
GLA_LEVELS = 7
GLA_MXU_LEVELS = 3
GLA_CHUNKS_PER_STEP = 4


def _params(sem, vmem=VMEM_LIMIT):
    return pltpu.CompilerParams(dimension_semantics=sem, vmem_limit_bytes=vmem)


def _dot(a, b):
    return jnp.dot(a, b, preferred_element_type=F32)


def _dot_nt(a, b):
    return lax.dot_general(a, b, (((1,), (1,)), ((), ())), preferred_element_type=F32)


def _dot_tn(a, b):
    return lax.dot_general(a, b, (((0,), (0,)), ((), ())), preferred_element_type=F32)


def _sigmoid(x):
    return 0.5 * jnp.tanh(0.5 * x) + 0.5


def _silu(x):
    return x * _sigmoid(x)


N_MOD = 6


class _Mod(NamedTuple):
    arr: jax.Array
    layer: int
    idx: int


def _ada_kernel(c_ref, w_ref, b_ref, os_ref, op_ref):
    c = c_ref[...]
    mod = _dot(_silu(c).astype(BF16), w_ref[...].astype(BF16)) + b_ref[...]
    n_s = os_ref.shape[0]
    os_ref[...] = mod[0:n_s]
    for b in range(op_ref.shape[0]):
        op_ref[b] = mod[n_s + b:n_s + b + 1]


def _ada(c_all, ada_w, ada_b, n_sample, n_prompt):
    rows = c_all.shape[0]
    return pl.pallas_call(
        _ada_kernel,
        grid=(DEPTH, N_MOD),
        in_specs=[
            pl.BlockSpec((rows, D_MODEL), lambda l, j: (0, 0)),
            pl.BlockSpec((None, D_MODEL, D_MODEL), lambda l, j: (l, 0, j)),
            pl.BlockSpec((None, 1, D_MODEL), lambda l, j: (l, 0, j)),
        ],
        out_specs=[
            pl.BlockSpec((None, None, n_sample, D_MODEL), lambda l, j: (l, j, 0, 0)),
            pl.BlockSpec((None, None, n_prompt, 1, D_MODEL), lambda l, j: (l, j, 0, 0, 0)),
        ],
        out_shape=[
            jax.ShapeDtypeStruct((DEPTH, N_MOD, n_sample, D_MODEL), F32),
            jax.ShapeDtypeStruct((DEPTH, N_MOD, n_prompt, 1, D_MODEL), F32),
        ],
        compiler_params=_params(("parallel", "parallel")),
        name="ada_mod",
    )(c_all, ada_w, ada_b.reshape(DEPTH, 1, N_MOD * D_MODEL))


def _mixin_kernel(x_ref, sh_ref, sc_ref, n1_ref, w_ref, bd_ref, qg_ref, kg_ref, wa2_ref, ba_ref,
                  q_ref, k_ref, v_ref, gq_ref, gk_ref, gv_ref, la_ref, rb_ref, sga_ref, sgb_ref):
    x = x_ref[...]
    ms = jnp.mean(x * x, axis=-1, keepdims=True)
    h = x * lax.rsqrt(ms + EPS) * n1_ref[...]
    h = h * (1.0 + sc_ref[...]) + sh_ref[...]
    hb = h.astype(BF16)

    def proj(a, b):
        return _dot(hb, w_ref[:, a:b])

    q = proj(_C_Q, _C_K)
    ssq = _dot((q * q).astype(BF16), bd_ref[...])
    q_ref[...] = (q * lax.rsqrt(ssq * (1.0 / HEAD_DIM) + EPS) * qg_ref[...]).astype(BF16)
    k = proj(_C_K, _C_V)
    ssk = _dot((k * k).astype(BF16), bd_ref[0:KV_W, 0:KV_W])
    k_ref[...] = k * lax.rsqrt(ssk * (1.0 / HEAD_DIM) + EPS) * kg_ref[...]
    v_ref[...] = proj(_C_V, _C_GQ)
    gq_ref[...] = proj(_C_GQ, _C_GK) * (GLA_DK ** -0.5)
    gk_ref[...] = proj(_C_GK, _C_GV)
    gv_ref[...] = proj(_C_GV, _C_RB)
    rb_ref[...] = _silu(proj(_C_RB, _C_GA)).astype(BF16)
    sga_ref[...] = _sigmoid(proj(_C_GA, _C_GB)).astype(BF16)
    sgb_ref[...] = _sigmoid(proj(_C_GB, _C_LR)).astype(BF16)
    ga = proj(_C_LR, PROJ_PAD)
    xg = _dot(ga.astype(BF16), wa2_ref[...]) + ba_ref[...]
    la_ref[...] = (jnp.minimum(xg, 0.0) - jnp.log1p(jnp.exp(-jnp.abs(xg)))) * (1.0 / GATE_TAU)


def _mixin(x, sh, sc, n1, w, bd, qg, kg, wa2, ba, *, tm):
    n = x.shape[0]
    nt = n // tm

    def row(i):
        return (i, 0)

    def const(shape):
        return pl.BlockSpec(shape, lambda i: (0,) * len(shape), pipeline_mode=pl.Buffered(1))

    def out(width, dtype):
        return pl.BlockSpec((tm, width), row), jax.ShapeDtypeStruct((n, width), dtype)

    outs = [out(Q_W, BF16), out(KV_W, F32), out(KV_W, F32), out(GK_W, F32), out(GK_W, F32),
            out(GV_W, F32), out(GK_W, F32), out(GV_W, BF16), out(D_MODEL, BF16), out(D_MODEL, BF16)]
    return pl.pallas_call(
        _mixin_kernel,
        grid=(nt,),
        in_specs=[
            pl.BlockSpec((tm, D_MODEL), row), _mod_spec(tm, n, sh), _mod_spec(tm, n, sc), const((1, D_MODEL)),
            const((D_MODEL, PROJ_PAD)), const((Q_W, Q_W)), const((1, Q_W)), const((1, KV_W)),
            const((GATE_PAD, GK_W)), const((1, GK_W)),
        ],
        out_specs=[o[0] for o in outs],
        out_shape=[o[1] for o in outs],
        compiler_params=_params(("parallel",)),
        name="mixer_in",
    )(x, sh.arr, sc.arr, n1, w, bd, qg, kg, wa2, ba)


SWA_BLOCKS = 8


def _head_slope(h):
    return float(2.0 ** (-8.0 * (h + 1) / N_HEADS))


def _swa_prompt_bias():
    blk = WINDOW
    dist = np.arange(blk)[:, None] + blk - np.arange(2 * blk)[None, :]
    slopes = np.asarray([_head_slope(h) for h in range(N_HEADS)])[:, None, None]
    return np.where((dist >= 0) & (dist <= WINDOW), -slopes * dist, -np.inf).astype(np.float32)


def _swa_prompt_kernel(sink_ref, q_ref, kp_ref, kc_ref, vp_ref, vc_ref, bias_ref, o_ref):
    n = pl.program_id(1)
    blk = WINDOW
    kall = jnp.concatenate([kp_ref[...], kc_ref[...]], axis=0).astype(BF16)
    vall = jnp.concatenate([vp_ref[...], vc_ref[...]], axis=0).astype(BF16)
    col = lax.broadcasted_iota(jnp.int32, (blk, 2 * blk), 1)
    first_key = jnp.where(n > 0, 0, blk)
    klane = lax.broadcasted_iota(jnp.int32, kall.shape, 1)
    kall_kv = [jnp.where((klane // HEAD_DIM) == kv, kall, jnp.zeros_like(kall)) for kv in range(N_KV)]
    olane = lax.broadcasted_iota(jnp.int32, (blk, KV_W), 1)
    q = q_ref[...]
    for j in range(SWA_BLOCKS):
        vv = vall[j * blk:(j + 2) * blk]
        outs = []
        for g in range(GROUP):
            qp = q[j * blk:(j + 1) * blk, g * KV_W:(g + 1) * KV_W]
            pair = []
            for kv in range(N_KV):
                h = kv * GROUP + g
                s = _dot_nt(qp, kall_kv[kv][j * blk:(j + 2) * blk]) + bias_ref[h]
                if j == 0:
                    s = jnp.where(col >= first_key, s, -jnp.inf)
                sink = sink_ref[h]
                m = jnp.maximum(jnp.max(s, axis=-1, keepdims=True), sink)
                p = jnp.exp(s - m)
                den = jnp.sum(p, axis=-1, keepdims=True) + jnp.exp(sink - m)
                pair.append(_dot(p.astype(BF16), vv) * (1.0 / den))
            outs.append(jnp.where(olane < HEAD_DIM, pair[0], pair[1]))
        o_ref[j * blk:(j + 1) * blk, :] = jnp.concatenate(outs, axis=-1).astype(BF16)


def _swa_prompt(sinks, q, k, v, bias):
    nb = SEQ // WINDOW
    steps = nb // SWA_BLOCKS
    tq = SWA_BLOCKS * WINDOW

    def cur(b, n):
        return (b * steps + n, 0)

    def prev(b, n):
        return (b * nb + jnp.maximum(n * SWA_BLOCKS - 1, 0), 0)

    return pl.pallas_call(
        _swa_prompt_kernel,
        grid=(BATCH, steps),
        in_specs=[
            pl.BlockSpec(memory_space=pltpu.SMEM),
            pl.BlockSpec((tq, Q_W), cur),
            pl.BlockSpec((WINDOW, KV_W), prev), pl.BlockSpec((tq, KV_W), cur),
            pl.BlockSpec((WINDOW, KV_W), prev), pl.BlockSpec((tq, KV_W), cur),
            pl.BlockSpec(bias.shape, lambda b, n: (0, 0, 0), pipeline_mode=pl.Buffered(1)),
        ],
        out_specs=pl.BlockSpec((tq, Q_W), cur),
        out_shape=jax.ShapeDtypeStruct((BATCH * SEQ, Q_W), BF16),
        compiler_params=_params(("parallel", "parallel")),
        name="swa_prompt",
    )(sinks, q, k, k, v, v, bias)


SAMPLE_TB = 8
CACHE_ROWS = WINDOW * N_KV


def _sample_head_of_row():
    j = np.arange(N_HEADS)
    return (j % N_KV) * GROUP + j // N_KV


def _swa_sample_bias():
    j = np.arange(N_HEADS)[:, None]
    c = np.arange(CACHE_ROWS)[None, :]
    slope = 2.0 ** (-8.0 * (_sample_head_of_row()[:, None] + 1) / N_HEADS)
    bias = -slope * (WINDOW - c // N_KV)
    return np.where(c % N_KV == j % N_KV, bias, -np.inf).astype(np.float32)


def _swa_sample_kernel(q_ref, kn_ref, vn_ref, ck_ref, cv_ref, bias_ref, sk_ref, o_ref, ok_ref, ov_ref):
    rows = CACHE_ROWS
    sink = sk_ref[...][:, 0:1]
    q = q_ref[...]
    kn = kn_ref[...]
    vn = vn_ref[...]
    kc = ck_ref[...]
    vc = cv_ref[...]
    kn8 = jnp.concatenate([kn] * GROUP, axis=1)
    vn8 = jnp.concatenate([vn] * GROUP, axis=1)
    s = lax.dot_general(q.astype(BF16), kc.astype(BF16), (((2,), (2,)), ((0,), (0,))),
                        preferred_element_type=F32) + bias_ref[...]
    s_new = jnp.sum(q * kn8, axis=-1, keepdims=True)
    m = jnp.maximum(jnp.maximum(jnp.max(s, axis=-1, keepdims=True), s_new), sink)
    p = jnp.exp(s - m)
    p_new = jnp.exp(s_new - m)
    den = jnp.sum(p, axis=-1, keepdims=True) + p_new + jnp.exp(sink - m)
    o = lax.dot_general(p.astype(BF16), vc.astype(BF16), (((2,), (1,)), ((0,), (0,))),
                        preferred_element_type=F32) + p_new * vn8
    o_ref[...] = o * (1.0 / den)
    ok_ref[:, pl.ds(0, rows - N_KV), :] = kc[:, N_KV:, :]
    ok_ref[:, pl.ds(rows - N_KV, N_KV), :] = kn
    ov_ref[:, pl.ds(0, rows - N_KV), :] = vc[:, N_KV:, :]
    ov_ref[:, pl.ds(rows - N_KV, N_KV), :] = vn


def _swa_sample(q, kn, vn, ck, cv, bias, sinks, *, layer):
    tb = 2 * SAMPLE_TB
    nb = DEC_BATCH // tb

    def const(shape):
        return pl.BlockSpec(shape, lambda i: (0,) * len(shape))

    def per_seq(*dims):
        return pl.BlockSpec((tb,) + dims, lambda i: (i, 0, 0))

    cache_spec = pl.BlockSpec((None, tb, CACHE_ROWS, HEAD_DIM), lambda i: (layer, i, 0, 0))
    return pl.pallas_call(
        _swa_sample_kernel,
        grid=(nb,),
        in_specs=[per_seq(N_HEADS, HEAD_DIM), per_seq(N_KV, HEAD_DIM), per_seq(N_KV, HEAD_DIM),
                  cache_spec, cache_spec,
                  const((N_HEADS, CACHE_ROWS)), const((N_HEADS, LANES))],
        out_specs=[per_seq(N_HEADS, HEAD_DIM), per_seq(CACHE_ROWS, HEAD_DIM), per_seq(CACHE_ROWS, HEAD_DIM)],
        out_shape=[
            jax.ShapeDtypeStruct((DEC_BATCH, N_HEADS, HEAD_DIM), F32),
            jax.ShapeDtypeStruct((DEC_BATCH, CACHE_ROWS, HEAD_DIM), F32),
            jax.ShapeDtypeStruct((DEC_BATCH, CACHE_ROWS, HEAD_DIM), F32),
        ],
        compiler_params=_params(("parallel",)),
        name="swa_sample",
    )(q, kn, vn, ck, cv, bias, sinks)


def _gla_cum_matrix():
    c = GLA_CHUNK
    tri = np.tril(np.ones((c, c), np.float32))
    i = np.arange(c)
    blocks = []
    for lvl in range(GLA_MXU_LEVELS):
        half = 1 << lvl
        mid = (i // (2 * half)) * (2 * half) + half - 1
        blocks.append(tri - tri[mid])
    blocks.append(tri)
    return np.concatenate(blocks, axis=0)


def _split3(x):
    hi = x.astype(BF16)
    r1 = x - hi.astype(F32)
    mid = r1.astype(BF16)
    lo = (r1 - mid.astype(F32)).astype(BF16)
    return hi, mid, lo


def _gla_level_masks():
    cl = GLA_CHUNK
    r = np.arange(cl)
    upper = np.stack([np.broadcast_to(((r >> lvl) & 1)[:, None], (cl, GK_W)) for lvl in range(GLA_LEVELS)])
    ri = np.tile(r, GLA_HEADS)[:, None]
    pairs = [(ri >> (lvl + 1)) == (r[None, :] >> (lvl + 1)) for lvl in range(GLA_LEVELS)]
    pairs.append(ri == r[None, :])
    return upper.astype(np.float32), np.stack(pairs).astype(np.float32)


def _gla_prompt_kernel(q_ref, k_ref, la_ref, v_ref, rb_ref, gn_ref, ut_ref, up_ref, pm_ref,
                       o_ref, s_ref, st_ref):
    c = pl.program_id(1)
    cl = GLA_CHUNK

    @pl.when(c == 0)
    def _():
        st_ref[0] = jnp.zeros(st_ref.shape[1:], F32)

    ut = ut_ref[...]
    lane = lax.broadcasted_iota(jnp.int32, (cl, GK_W), 1)
    head_of_lane = lane // GLA_DK
    ones = jnp.ones((cl, LANES), BF16)
    state = [st_ref[c % 2, h] for h in range(GLA_HEADS)]
    for cc in range(GLA_CHUNKS_PER_STEP):
        rows = slice(cc * cl, (cc + 1) * cl)
        hi, mid, lo = _split3(la_ref[rows, :])
        tall = _dot(ut, hi) + _dot(ut, mid) + _dot(ut, lo)
        q = q_ref[rows, :]
        k = k_ref[rows, :]
        attn_all = jnp.zeros((GLA_HEADS * cl, cl), F32)
        cum = tall[GLA_MXU_LEVELS * cl:(GLA_MXU_LEVELS + 1) * cl]
        for lvl in range(GLA_LEVELS + 1):
            if lvl < GLA_LEVELS:
                if lvl < GLA_MXU_LEVELS:
                    t_lvl = tall[lvl * cl:(lvl + 1) * cl]
                else:
                    half = 1 << lvl
                    mids = [jnp.broadcast_to(cum[b0 + half - 1:b0 + half, :], (2 * half, GK_W))
                            for b0 in range(0, cl, 2 * half)]
                    t_lvl = cum - (mids[0] if len(mids) == 1 else jnp.concatenate(mids, axis=0))
                e = jnp.exp(-jnp.abs(t_lvl))
                e_up = e * up_ref[lvl]
                qt = (q * e_up).astype(BF16)
                kt = (k * (e - e_up)).astype(BF16)
            else:
                qt = q.astype(BF16)
                kt = k.astype(BF16)
            q_heads = jnp.concatenate(
                [jnp.where(head_of_lane == h, qt, jnp.zeros_like(qt)) for h in range(GLA_HEADS)], axis=0)
            attn_all = attn_all + _dot_nt(q_heads, kt) * pm_ref[lvl]

        last = cum[cl - 1:cl, :]
        qe = (q * jnp.exp(cum)).astype(BF16)
        kd = (k * jnp.exp(last - cum)).astype(BF16)
        dec = jnp.exp(_dot_tn(hi, ones) + _dot_tn(mid, ones) + _dot_tn(lo, ones))
        v = v_ref[rows, :]
        rb = rb_ref[rows, :]
        outs = []
        for h in range(GLA_HEADS):
            vh = v[:, h * GLA_DV:(h + 1) * GLA_DV].astype(BF16)
            o = _dot(attn_all[h * cl:(h + 1) * cl].astype(BF16), vh) + _dot(qe, state[h].astype(BF16))
            kdh = jnp.where(head_of_lane == h, kd, jnp.zeros_like(kd))
            state[h] = state[h] * dec + _dot_tn(kdh, vh)
            ms = jnp.mean(o * o, axis=-1, keepdims=True)
            g = o * lax.rsqrt(ms + EPS) * gn_ref[...]
            outs.append(g * rb[:, h * GLA_DV:(h + 1) * GLA_DV].astype(F32))
        o_ref[rows, :] = jnp.concatenate(outs, axis=-1).astype(BF16)

    for h in range(GLA_HEADS):
        st_ref[(c + 1) % 2, h] = state[h]

    @pl.when(c == pl.num_programs(1) - 1)
    def _():
        fin = (SEQ // (GLA_CHUNK * GLA_CHUNKS_PER_STEP)) % 2
        s_ref[...] = st_ref[fin, 0] + st_ref[fin, 1] + st_ref[fin, 2] + st_ref[fin, 3]


def _gla_prompt(gq, gk, la, gv, rbs, gn, ut, upper, pairs):
    cl = GLA_CHUNK * GLA_CHUNKS_PER_STEP
    nc = SEQ // cl

    def row(b, c):
        return (b * nc + c, 0)

    def const(shape):
        return pl.BlockSpec(shape, lambda b, c: (0,) * len(shape), pipeline_mode=pl.Buffered(1))

    return pl.pallas_call(
        _gla_prompt_kernel,
        grid=(BATCH, nc),
        in_specs=[
            pl.BlockSpec((cl, GK_W), row), pl.BlockSpec((cl, GK_W), row), pl.BlockSpec((cl, GK_W), row),
            pl.BlockSpec((cl, GV_W), row), pl.BlockSpec((cl, GV_W), row),
            const((1, GLA_DV)), const(ut.shape), const(upper.shape), const(pairs.shape),
        ],
        out_specs=[
            pl.BlockSpec((cl, GV_W), row),
            pl.BlockSpec((None, GK_W, GLA_DV), lambda b, c: (b, 0, 0)),
        ],
        out_shape=[
            jax.ShapeDtypeStruct((BATCH * SEQ, GV_W), BF16),
            jax.ShapeDtypeStruct((BATCH, GK_W, GLA_DV), F32),
        ],
        scratch_shapes=[pltpu.VMEM((2, GLA_HEADS, GK_W, GLA_DV), F32)],
        compiler_params=_params(("parallel", "arbitrary")),
        name="gla_prompt",
    )(gq, gk, la, gv, rbs, gn, ut, upper, pairs)


def _gla_sample_kernel(q_ref, k_ref, la_ref, v_ref, rb_ref, gn_ref, s_ref, o_ref, so_ref):
    tb = SAMPLE_TB
    dec = jnp.exp(la_ref[...])
    pieces = []
    for x in (dec, k_ref[...], q_ref[...]):
        hi, mid, lo = _split3(x)
        stacked = jnp.concatenate(
            [hi.astype(F32), mid.astype(F32), lo.astype(F32), jnp.zeros_like(x)], axis=0)
        pieces.append(stacked.astype(BF16))
    prow = lax.broadcasted_iota(jnp.int32, (4 * tb, LANES), 0)
    rb = rb_ref[...].astype(F32)
    v = v_ref[...]
    for bi in range(tb):
        sel = jnp.where((prow % tb) == bi, 1.0, 0.0).astype(BF16)
        a_col, k_col, q_col = [_dot_tn(p, sel) for p in pieces]
        for h in range(GLA_HEADS):
            rs = slice(h * GLA_DK, (h + 1) * GLA_DK)
            vs = slice(h * GLA_DV, (h + 1) * GLA_DV)
            s_new = a_col[rs] * s_ref[bi, rs, :] + k_col[rs] * v[bi:bi + 1, vs]
            so_ref[bi, rs, :] = s_new
            o = jnp.sum(q_col[rs] * s_new, axis=0, keepdims=True)
            ms = jnp.mean(o * o, axis=-1, keepdims=True)
            g = o * lax.rsqrt(ms + EPS) * gn_ref[...]
            o_ref[bi:bi + 1, vs] = g * rb[bi:bi + 1, vs]


def _gla_sample(gq, gk, la, gv, rbs, gn, state, *, layer):
    tb = SAMPLE_TB
    nb = DEC_BATCH // tb

    def row(w):
        return pl.BlockSpec((tb, w), lambda i: (i, 0))

    st_spec = pl.BlockSpec((tb, GK_W, GLA_DV), lambda i: (i, 0, 0))
    st_in_spec = pl.BlockSpec((None, tb, GK_W, GLA_DV), lambda i: (layer, i, 0, 0))
    return pl.pallas_call(
        _gla_sample_kernel,
        grid=(nb,),
        in_specs=[row(GK_W), row(GK_W), row(GK_W), row(GV_W), row(GV_W),
                  pl.BlockSpec((1, GLA_DV), lambda i: (0, 0)), st_in_spec],
        out_specs=[row(GV_W), st_spec],
        out_shape=[
            jax.ShapeDtypeStruct((DEC_BATCH, GV_W), F32),
            jax.ShapeDtypeStruct((DEC_BATCH, GK_W, GLA_DV), F32),
        ],
        compiler_params=_params(("parallel",)),
        name="gla_sample",
    )(gq, gk, la, gv, rbs, gn, state)


def _merge_kernel(x_ref, a_ref, g_ref, sga_ref, sgb_ref, wpa_ref, wpb_ref, wo_ref,
                  g1_ref, sh_ref, sc_ref, n2_ref, *rest):
    with_router = len(rest) == 8
    x1_ref, h2_ref = rest[3:5] if with_router else rest
    ya = _dot(a_ref[...].astype(BF16), wpa_ref[...])
    yb = _dot(g_ref[...].astype(BF16), wpb_ref[...])
    merged = sga_ref[...].astype(F32) * ya + sgb_ref[...].astype(F32) * yb
    mix = _dot(merged.astype(BF16), wo_ref[...])
    x1 = x_ref[...] + g1_ref[...] * mix
    x1_ref[...] = x1
    ms = jnp.mean(x1 * x1, axis=-1, keepdims=True)
    h = x1 * lax.rsqrt(ms + EPS) * n2_ref[...]
    h2 = h * (1.0 + sc_ref[...]) + sh_ref[...]
    h2_ref[...] = h2.astype(h2_ref.dtype)
    if with_router:
        rw_ref, rbias_ref, tri_ref = rest[0:3]
        route_ref, route_t_ref, cnt_ref = rest[5:8]
        logits = _dot(h2.astype(BF16), rw_ref[...]) + rbias_ref[...]
        packed, counts = _route_pack(logits, tri_ref[...])
        route_ref[...] = packed
        route_t_ref[...] = packed.T[0:route_t_ref.shape[0], :]
        cnt_ref[...] = jnp.broadcast_to(counts, cnt_ref.shape)


def _mod_spec(tm, n, mod):
    if mod.arr.ndim == 4:
        return pl.BlockSpec((None, None, tm, D_MODEL), lambda i, *_: (mod.layer, mod.idx, i, 0))
    tiles_per_seq = (n // mod.arr.shape[2]) // tm
    return pl.BlockSpec((None, None, None, 1, D_MODEL),
                        lambda i, *_: (mod.layer, mod.idx, i // tiles_per_seq, 0, 0))


def _merge(x, a, g, sga, sgb, wpa, wpb, wo, g1, sh2, sc2, n2, *, tm, h2_dtype=BF16, router=None):
    n = x.shape[0]

    def row(w):
        return pl.BlockSpec((tm, w), lambda i: (i, 0))

    def const(shape):
        return pl.BlockSpec(shape, lambda i: (0,) * len(shape))

    in_specs = [row(D_MODEL), row(a.shape[1]), row(GV_W), row(D_MODEL), row(D_MODEL),
                const(wpa.shape), const(wpb.shape), const(wo.shape),
                _mod_spec(tm, n, g1), _mod_spec(tm, n, sh2), _mod_spec(tm, n, sc2), const((1, D_MODEL))]
    out_specs = [row(D_MODEL), row(D_MODEL)]
    out_shape = [jax.ShapeDtypeStruct((n, D_MODEL), F32), jax.ShapeDtypeStruct((n, D_MODEL), h2_dtype)]
    args = [x, a, g, sga, sgb, wpa, wpb, wo, g1.arr, sh2.arr, sc2.arr, n2]
    if router is not None:
        tri = jnp.asarray(np.tril(np.ones((tm, tm), np.float32), -1), BF16)
        in_specs += [const(router[0].shape), const(router[1].shape), const((tm, tm))]
        out_specs += [row(ROUTER_PAD), pl.BlockSpec((8, tm), lambda i: (0, i)),
                      pl.BlockSpec((None, 8, ROUTER_PAD), lambda i: (i, 0, 0))]
        out_shape += [jax.ShapeDtypeStruct((n, ROUTER_PAD), F32), jax.ShapeDtypeStruct((8, n), F32),
                      jax.ShapeDtypeStruct((n // tm, 8, ROUTER_PAD), F32)]
        args += [router[0], router[1], tri]
    return pl.pallas_call(
        _merge_kernel,
        grid=(n // tm,),
        in_specs=in_specs,
        out_specs=out_specs,
        out_shape=out_shape,
        compiler_params=_params(("parallel",)),
        name="merge_out",
    )(*args)


def _swiglu(hb, wg_ref, wu_ref, wd_ref):
    act = (_silu(_dot(hb, wg_ref[...])) * _dot(hb, wu_ref[...])).astype(BF16)
    return _dot(act, wd_ref[...])


def _ffn_kernel(h_ref, x_ref, g2_ref, wg_ref, wu_ref, wd_ref, o_ref, acc_ref):
    f = pl.program_id(1)
    y = _swiglu(h_ref[...], wg_ref, wu_ref, wd_ref)

    @pl.when(f == 0)
    def _():
        acc_ref[...] = y

    @pl.when(f > 0)
    def _():
        acc_ref[...] += y

    @pl.when(f == pl.num_programs(1) - 1)
    def _():
        o_ref[...] = x_ref[...] + g2_ref[...] * acc_ref[...]


def _ffn(h2, x1, g2, wg, wu, wd, *, tm, tf):
    n = h2.shape[0]
    mod_spec = _mod_spec(tm, n, g2)
    return pl.pallas_call(
        _ffn_kernel,
        grid=(n // tm, D_FF // tf),
        in_specs=[
            pl.BlockSpec((tm, D_MODEL), lambda i, f: (i, 0)),
            pl.BlockSpec((tm, D_MODEL), lambda i, f: (i, 0)),
            mod_spec,
            pl.BlockSpec((D_MODEL, tf), lambda i, f: (0, f)),
            pl.BlockSpec((D_MODEL, tf), lambda i, f: (0, f)),
            pl.BlockSpec((tf, D_MODEL), lambda i, f: (f, 0)),
        ],
        out_specs=pl.BlockSpec((tm, D_MODEL), lambda i, f: (i, 0)),
        out_shape=jax.ShapeDtypeStruct((n, D_MODEL), F32),
        scratch_shapes=[pltpu.VMEM((tm, D_MODEL), F32)],
        compiler_params=_params(("parallel", "arbitrary")),
        name="ffn_dense",
    )(h2, x1, g2.arr, wg, wu, wd)


def _moe_kernel(h_ref, x_ref, g2_ref, rw_ref, rbias_ref, wg_ref, wu_ref, wd_ref, o_ref, acc_ref, gate_ref):
    e = pl.program_id(1)
    hb = h_ref[...]
    tm = hb.shape[0]
    lane = lax.broadcasted_iota(jnp.int32, (tm, ROUTER_PAD), 1).astype(F32)

    @pl.when(e == 0)
    def _():
        logits = _dot(hb, rw_ref[...]) + rbias_ref[...]
        i1, i2, p1, p2 = _top2(logits, lane)
        gate_ref[...] = jnp.where(lane == i1, p1, 0.0) + jnp.where(lane == i2, p2, 0.0)
        acc_ref[...] = jnp.zeros_like(acc_ref)

    ge = jnp.sum(jnp.where(lane == e.astype(F32), gate_ref[...], 0.0), axis=-1, keepdims=True)
    act = (_silu(_dot(hb, wg_ref[...])) * _dot(hb, wu_ref[...])).astype(BF16)
    acc_ref[...] += ge * _dot(act, wd_ref[...])

    @pl.when(e == pl.num_programs(1) - 1)
    def _():
        o_ref[...] = x_ref[...] + g2_ref[...] * acc_ref[...]


def _moe(h2, x1, g2, rw, rbias, wg, wu, wd, *, tm):
    n = h2.shape[0]
    mod_spec = _mod_spec(tm, n, g2)
    fe = D_FF_EXPERT
    return pl.pallas_call(
        _moe_kernel,
        grid=(n // tm, N_EXPERTS),
        in_specs=[
            pl.BlockSpec((tm, D_MODEL), lambda i, e: (i, 0)),
            pl.BlockSpec((tm, D_MODEL), lambda i, e: (i, 0)),
            mod_spec,
            pl.BlockSpec((D_MODEL, ROUTER_PAD), lambda i, e: (0, 0)),
            pl.BlockSpec((1, ROUTER_PAD), lambda i, e: (0, 0)),
            pl.BlockSpec((None, D_MODEL, fe), lambda i, e: (e, 0, 0)),
            pl.BlockSpec((None, D_MODEL, fe), lambda i, e: (e, 0, 0)),
            pl.BlockSpec((None, fe, D_MODEL), lambda i, e: (e, 0, 0)),
        ],
        out_specs=pl.BlockSpec((tm, D_MODEL), lambda i, e: (i, 0)),
        out_shape=jax.ShapeDtypeStruct((n, D_MODEL), F32),
        scratch_shapes=[pltpu.VMEM((tm, D_MODEL), F32), pltpu.VMEM((tm, ROUTER_PAD), F32)],
        compiler_params=_params(("parallel", "arbitrary")),
        name="moe",
    )(h2, x1, g2.arr, rw, rbias, wg, wu, wd)


def _top2(logits, lane):
    lg = jnp.where(lane < N_EXPERTS, logits, -jnp.inf)
    m1 = jnp.max(lg, axis=-1, keepdims=True)
    i1 = jnp.min(jnp.where(lg == m1, lane, float(ROUTER_PAD)), axis=-1, keepdims=True)
    lg2 = jnp.where(lane == i1, -jnp.inf, lg)
    m2 = jnp.max(lg2, axis=-1, keepdims=True)
    i2 = jnp.min(jnp.where(lg2 == m2, lane, float(ROUTER_PAD)), axis=-1, keepdims=True)
    e2 = jnp.exp(m2 - m1)
    p1 = 1.0 / (1.0 + e2)
    return i1, i2, p1, e2 * p1


def _route_pack(logits, tri):
    lane = lax.broadcasted_iota(jnp.int32, logits.shape, 1).astype(F32)
    i1, i2, p1, p2 = _top2(logits, lane)
    oh1 = jnp.where(lane == i1, 1.0, 0.0)
    oh2 = jnp.where(lane == i2, 1.0, 0.0)
    cnt1 = jnp.sum(oh1, axis=0, keepdims=True)
    cnt2 = jnp.sum(oh2, axis=0, keepdims=True)
    rank1 = jnp.sum(_dot(tri, oh1.astype(BF16)) * oh1, axis=-1, keepdims=True)
    rank2 = jnp.sum((_dot(tri, oh2.astype(BF16)) + cnt1) * oh2, axis=-1, keepdims=True)
    packed = jnp.zeros_like(logits)
    for k, val in enumerate((p1, p2, i1, i2, rank1, rank2)):
        packed = jnp.where(lane == float(k), val, packed)
    return packed, cnt1 + cnt2


def _route_tables(route_t, tile_counts, tm, n_tiles):
    n = route_t.shape[1]
    tile_cnt = tile_counts[:, 0, :N_EXPERTS].astype(jnp.int32)
    cnt = jnp.sum(tile_cnt, axis=0)
    gsz = ((cnt + tm - 1) // tm) * tm
    gend = jnp.cumsum(gsz)
    seg_start = (gend - gsz)[None, :] + jnp.cumsum(tile_cnt, axis=0) - tile_cnt
    seg_of_token = jnp.repeat(seg_start.T, n // tile_cnt.shape[0], axis=1)
    experts = jnp.arange(N_EXPERTS, dtype=jnp.int32)[:, None]
    pos = []
    for k in range(2):
        e_k = route_t[2 + k].astype(jnp.int32)
        rank_k = route_t[4 + k].astype(jnp.int32)
        pos.append(jnp.sum(jnp.where(e_k[None, :] == experts, seg_of_token, 0), axis=0) + rank_k)
    pos = jnp.concatenate(pos)
    tile_start = jnp.arange(n_tiles, dtype=jnp.int32) * tm
    tile_expert = jnp.sum((tile_start[:, None] >= gend[None, :]).astype(jnp.int32), axis=1)
    tile_expert = jnp.minimum(tile_expert, N_EXPERTS - 1)
    live = (gend[-1] // tm).reshape(1)
    return pos, tile_expert, live, gend - gsz + cnt, gend


def _row_copy_wait(src_hbm, dst, sem, rows):
    pltpu.make_async_copy(src_hbm.at[pl.ds(0, rows)], dst, sem).wait()


def _moe_dispatch_kernel(pos_ref, pad_lo_ref, pad_hi_ref, live_ref, h_ref, xs_hbm, zbuf, sem, zsem):
    i = pl.program_id(0)
    tt = h_ref.shape[0]
    n = tt * pl.num_programs(0)
    tm = zbuf.shape[0]

    base = i * tt
    for r in range(tt):
        row = h_ref.at[pl.ds(r, 1)]
        pltpu.make_async_copy(row, xs_hbm.at[pl.ds(pos_ref[base + r], 1)], sem).start(priority=0)
        pltpu.make_async_copy(row, xs_hbm.at[pl.ds(pos_ref[n + base + r], 1)], sem).start(priority=1)
    for _ in range(2):
        pltpu.make_async_copy(h_ref, xs_hbm.at[pl.ds(0, tt)], sem).wait()

    @pl.when(i == pl.num_programs(0) - 1)
    def _():
        zbuf[...] = jnp.zeros(zbuf.shape, F32)

        def fill_row(p):
            return pltpu.make_async_copy(zbuf.at[pl.ds(0, 1)], xs_hbm.at[pl.ds(p, 1)], zsem)

        def fill_tile(t):
            return pltpu.make_async_copy(zbuf, xs_hbm.at[pl.ds(pl.multiple_of(t * tm, tm), tm)], zsem)

        for e in range(N_EXPERTS):
            lo, hi = pad_lo_ref[e], pad_hi_ref[e]
            lax.fori_loop(lo, hi, lambda p, c: (fill_row(p).start(), c)[1], 0)
            lax.fori_loop(lo, hi, lambda p, c: (fill_row(p).wait(), c)[1], 0)
        lo, hi = live_ref[0], xs_hbm.shape[0] // tm
        lax.fori_loop(lo, hi, lambda t, c: (fill_tile(t).start(), c)[1], 0)
        lax.fori_loop(lo, hi, lambda t, c: (fill_tile(t).wait(), c)[1], 0)


def _moe_dispatch(h2, pos, pad_lo, pad_hi, live, *, tt, tm, n_tiles):
    n = h2.shape[0]
    grid_spec = pltpu.PrefetchScalarGridSpec(
        num_scalar_prefetch=4,
        grid=(n // tt,),
        in_specs=[pl.BlockSpec((tt, D_MODEL), lambda i, *_: (i, 0))],
        out_specs=pl.BlockSpec(memory_space=pl.ANY),
        scratch_shapes=[pltpu.VMEM((tm, D_MODEL), F32), pltpu.SemaphoreType.DMA(()),
                        pltpu.SemaphoreType.DMA(())],
    )
    return pl.pallas_call(
        _moe_dispatch_kernel,
        grid_spec=grid_spec,
        out_shape=jax.ShapeDtypeStruct((n_tiles * tm, D_MODEL), F32),
        compiler_params=_params(("arbitrary",)),
        name="moe_dispatch",
    )(pos, pad_lo, pad_hi, live, h2)


def _moe_expert_kernel(te_ref, live_ref, x_ref, wg_ref, wu_ref, wd_ref, y_ref):
    del te_ref
    is_live = pl.program_id(0) < live_ref[0]

    @pl.when(is_live)
    def _():
        y_ref[...] = _swiglu(x_ref[...].astype(BF16), wg_ref, wu_ref, wd_ref)

    @pl.when(jnp.logical_not(is_live))
    def _():
        y_ref[...] = jnp.zeros(y_ref.shape, F32)


def _moe_experts(xs, tile_expert, live, wg, wu, wd, *, tm, n_tiles):
    fe = D_FF_EXPERT

    def in_tile(t, te, live):
        return (jnp.minimum(t, live[0] - 1), 0)

    def out_tile(t, te, live):
        return (t, 0)

    def expert(t, te, live):
        return (te[t], 0, 0)

    grid_spec = pltpu.PrefetchScalarGridSpec(
        num_scalar_prefetch=2,
        grid=(n_tiles,),
        in_specs=[
            pl.BlockSpec((tm, D_MODEL), in_tile),
            pl.BlockSpec((None, D_MODEL, fe), expert),
            pl.BlockSpec((None, D_MODEL, fe), expert),
            pl.BlockSpec((None, fe, D_MODEL), expert),
        ],
        out_specs=pl.BlockSpec((tm, D_MODEL), out_tile),
    )
    return pl.pallas_call(
        _moe_expert_kernel,
        grid_spec=grid_spec,
        out_shape=jax.ShapeDtypeStruct(xs.shape, F32),
        compiler_params=_params(("arbitrary",)),
        name="moe_experts",
    )(tile_expert, live, xs, wg, wu, wd)


def _moe_combine_kernel(pos_ref, x_ref, g2_ref, r_ref, ys_hbm, o_ref, buf, sem):
    i = pl.program_id(0)
    nt = pl.num_programs(0)
    tt = x_ref.shape[0]
    n = nt * tt

    def fetch(tile, sl):
        base = tile * tt
        for r in range(tt):
            for s in range(2):
                row = pos_ref[s * n + base + r]
                pltpu.make_async_copy(ys_hbm.at[pl.ds(row, 1)], buf.at[sl, pl.ds(s * tt + r, 1)],
                                      sem.at[sl]).start(priority=s)

    @pl.when(i == 0)
    def _():
        fetch(0, 0)

    @pl.when(i + 1 < nt)
    def _():
        fetch(i + 1, (i + 1) % 2)

    sl = i % 2
    _row_copy_wait(ys_hbm, buf.at[sl], sem.at[sl], 2 * tt)
    r = r_ref[...]
    f = r[:, 0:1] * buf[sl, pl.ds(0, tt), :] + r[:, 1:2] * buf[sl, pl.ds(tt, tt), :]
    o_ref[...] = x_ref[...] + g2_ref[...] * f


def _moe_combine(x1, g2, route, pos, ys, *, tt):
    n = x1.shape[0]
    grid_spec = pltpu.PrefetchScalarGridSpec(
        num_scalar_prefetch=1,
        grid=(n // tt,),
        in_specs=[
            pl.BlockSpec((tt, D_MODEL), lambda i, pos: (i, 0)),
            _mod_spec(tt, n, g2),
            pl.BlockSpec((tt, ROUTER_PAD), lambda i, pos: (i, 0)),
            pl.BlockSpec(memory_space=pl.ANY),
        ],
        out_specs=pl.BlockSpec((tt, D_MODEL), lambda i, pos: (i, 0)),
        scratch_shapes=[pltpu.VMEM((2, 2 * tt, D_MODEL), F32), pltpu.SemaphoreType.DMA((2,))],
    )
    return pl.pallas_call(
        _moe_combine_kernel,
        grid_spec=grid_spec,
        out_shape=jax.ShapeDtypeStruct((n, D_MODEL), F32),
        compiler_params=_params(("arbitrary",)),
        name="moe_combine",
    )(pos, x1, g2.arr, route, ys)


def _moe_routed(h2, x1, g2, route, route_t, tile_counts, wg, wu, wd, *, tm, tt):
    n = h2.shape[0]
    n_tiles = (2 * n) // tm + N_EXPERTS
    pos, tile_expert, live, pad_lo, pad_hi = _route_tables(route_t, tile_counts, tm, n_tiles)
    xs = _moe_dispatch(h2, pos, pad_lo, pad_hi, live, tt=tt, tm=tm, n_tiles=n_tiles)
    ys = _moe_experts(xs, tile_expert, live, wg, wu, wd, tm=tm, n_tiles=n_tiles)
    return _moe_combine(x1, g2, route, pos, ys, tt=tt)


def _head_perm():
    idx = []
    for g in range(GROUP):
        for kv in range(N_KV):
            h = kv * GROUP + g
            idx.extend(range(h * HEAD_DIM, (h + 1) * HEAD_DIM))
    return np.asarray(idx, np.int32)


def _relayout_w_in(w):
    pts = np.cumsum([0, Q_W, KV_W, KV_W, GK_W, GK_W, GV_W, GATE_RANK, GV_W, D_MODEL, D_MODEL])
    qa, ka, va, qb, kb, vb, ga, rb, gta, gtb = [w[:, pts[i]:pts[i + 1]] for i in range(10)]
    qa = qa[:, _head_perm()]
    ga = jnp.pad(ga, ((0, 0), (0, GATE_PAD - GATE_RANK)))
    return jnp.concatenate([qa, ka, va, qb, kb, vb, rb, gta, gtb, ga], axis=1).astype(BF16)


def kernel(x_prompt, x_sample, cache_k, cache_v, state_gla, c_prompt, c_sample, ada_w, ada_b, norm1_g, norm2_g, w_in, q_norm_g, k_norm_g, attn_sinks, gla_wa2, gla_ba, gla_norm_g, w_branch_a, w_branch_b, w_out, ffn_w_gate, ffn_w_up, ffn_w_down, router_w, router_b, moe_w_gate, moe_w_up, moe_w_down):
    n_p = BATCH * SEQ
    xp = x_prompt.reshape(n_p, D_MODEL)
    xs = x_sample.reshape(DEC_BATCH, D_MODEL)

    c_pad = -(BATCH + DEC_BATCH) % 8
    c_all = jnp.pad(jnp.concatenate([c_sample, c_prompt], axis=0), ((0, c_pad), (0, 0)))
    mod_sample, mod_prompt = _ada(c_all, ada_w, ada_b, DEC_BATCH, BATCH)
    cache_k_rows = cache_k.reshape(DEPTH, DEC_BATCH, CACHE_ROWS, HEAD_DIM)
    cache_v_rows = cache_v.reshape(DEPTH, DEC_BATCH, CACHE_ROWS, HEAD_DIM)
    state_rows = state_gla.reshape(DEPTH, DEC_BATCH, GK_W, GLA_DV)

    bd = jnp.asarray(np.kron(np.eye(N_HEADS), np.ones((HEAD_DIM, HEAD_DIM))), BF16)
    ut = jnp.asarray(_gla_cum_matrix(), BF16)
    gla_upper, gla_pairs = (jnp.asarray(m) for m in _gla_level_masks())
    perm = _head_perm()
    swa_bias = jnp.asarray(_swa_sample_bias())
    swa_prompt_bias = jnp.asarray(_swa_prompt_bias())

    kp_l, vp_l, sp_l, ks_l, vs_l, ss_l = [], [], [], [], [], []
    for l in range(DEPTH):
        mod_p = [_Mod(mod_prompt, l, i) for i in range(N_MOD)]
        mod_s = [_Mod(mod_sample, l, i) for i in range(N_MOD)]

        w = _relayout_w_in(w_in[l])
        qg = (jnp.tile(q_norm_g[l], N_HEADS) * (HEAD_DIM ** -0.5)).reshape(1, Q_W)
        kg = jnp.tile(k_norm_g[l], N_KV).reshape(1, KV_W)
        wa2 = jnp.pad(gla_wa2[l], ((0, GATE_PAD - GATE_RANK), (0, 0))).astype(BF16)
        ba = gla_ba[l].reshape(1, GK_W)
        n1 = norm1_g[l].reshape(1, D_MODEL)
        n2 = norm2_g[l].reshape(1, D_MODEL)
        gn = gla_norm_g[l].reshape(1, GLA_DV)
        wpa = w_branch_a[l][perm].astype(BF16)
        wpb = w_branch_b[l].astype(BF16)
        wo = w_out[l].astype(BF16)
        sink_rows = jnp.broadcast_to(attn_sinks[l][_sample_head_of_row()][:, None], (N_HEADS, LANES))

        q, k, v, gq, gk, gv, la, rbs, sga, sgb = _mixin(
            xp, mod_p[0], mod_p[1], n1, w, bd, qg, kg, wa2, ba, tm=1024)
        a_out = _swa_prompt(attn_sinks[l], q, k, v, swa_prompt_bias)
        g_out, s_fin = _gla_prompt(gq, gk, la, gv, rbs, gn, ut, gla_upper, gla_pairs)
        is_moe = l % 2 == 1
        if is_moe:
            rw = jnp.pad(router_w[l // 2], ((0, 0), (0, ROUTER_PAD - N_EXPERTS))).astype(BF16)
            rbias = jnp.pad(router_b[l // 2], (0, ROUTER_PAD - N_EXPERTS)).reshape(1, ROUTER_PAD)
        merged = _merge(xp, a_out, g_out, sga, sgb, wpa, wpb, wo, mod_p[2], mod_p[3], mod_p[4], n2,
                        tm=512, h2_dtype=F32 if is_moe else BF16,
                        router=(rw, rbias) if is_moe else None)
        x1, h2 = merged[0], merged[1]
        for cache, full in ((kp_l, k), (vp_l, v)):
            tail = full.reshape(BATCH, SEQ, KV_W)[:, SEQ - WINDOW:]
            cache.append(tail.reshape(BATCH, WINDOW, N_KV, HEAD_DIM))
        sp_l.append(s_fin.reshape(BATCH, GLA_HEADS, GLA_DK, GLA_DV))

        qs, ksn, vsn, gqs, gks, gvs, las, rbss, sgas, sgbs = _mixin(
            xs, mod_s[0], mod_s[1], n1, w, bd, qg, kg, wa2, ba, tm=DEC_BATCH)
        a_s, nk, nv = _swa_sample(
            qs.astype(F32).reshape(DEC_BATCH, N_HEADS, HEAD_DIM),
            ksn.reshape(DEC_BATCH, N_KV, HEAD_DIM), vsn.reshape(DEC_BATCH, N_KV, HEAD_DIM),
            cache_k_rows, cache_v_rows, swa_bias, sink_rows, layer=l)
        g_s, s_new = _gla_sample(gqs, gks, las, gvs, rbss, gn, state_rows, layer=l)
        x1s, h2s = _merge(xs, a_s.reshape(DEC_BATCH, Q_W), g_s, sgas, sgbs, wpa, wpb, wo,
                          mod_s[2], mod_s[3], mod_s[4], n2, tm=DEC_BATCH)
        ks_l.append(nk.reshape(DEC_BATCH, WINDOW, N_KV, HEAD_DIM))
        vs_l.append(nv.reshape(DEC_BATCH, WINDOW, N_KV, HEAD_DIM))
        ss_l.append(s_new.reshape(DEC_BATCH, GLA_HEADS, GLA_DK, GLA_DV))

        i = l // 2
        if not is_moe:
            wg, wu, wd = ffn_w_gate[i].astype(BF16), ffn_w_up[i].astype(BF16), ffn_w_down[i].astype(BF16)
            xp = _ffn(h2, x1, mod_p[5], wg, wu, wd, tm=512, tf=D_FF // 2)
            xs = _ffn(h2s, x1s, mod_s[5], wg, wu, wd, tm=DEC_BATCH, tf=D_FF // 2)
        else:
            wg, wu, wd = moe_w_gate[i].astype(BF16), moe_w_up[i].astype(BF16), moe_w_down[i].astype(BF16)
            xp = _moe_routed(h2, x1, mod_p[5], *merged[2:5], wg, wu, wd, tm=512, tt=512)
            xs = _moe(h2s, x1s, mod_s[5], rw, rbias, wg, wu, wd, tm=DEC_BATCH)

    return (xp.reshape(BATCH, SEQ, D_MODEL), xs.reshape(DEC_BATCH, 1, D_MODEL),
            jnp.stack(kp_l), jnp.stack(vp_l), jnp.stack(sp_l),
            jnp.stack(ks_l), jnp.stack(vs_l), jnp.stack(ss_l))
```

```python
import functools
from typing import NamedTuple

import jax
import jax.numpy as jnp
import numpy as np
from jax import lax
from jax.experimental import pallas as pl
from jax.experimental.pallas import tpu as pltpu

D_MODEL = 1024
BATCH = 4
SEQ = 4096
DEPTH = 2
DEC_BATCH = 128
N_HEADS = 8
N_KV = 2
HEAD_DIM = 64
GROUP = N_HEADS // N_KV
WINDOW = 128
GLA_HEADS = 4
GLA_DK = 64
GLA_DV = 128
GATE_RANK = 16
GATE_TAU = 16.0
D_FF = 2816
N_EXPERTS = 8
D_FF_EXPERT = 1408
EPS = 1e-6

Q_W = N_HEADS * HEAD_DIM
KV_W = N_KV * HEAD_DIM
GK_W = GLA_HEADS * GLA_DK
GV_W = GLA_HEADS * GLA_DV

LANES = 128
GATE_PAD = LANES
ROUTER_PAD = LANES
VMEM_LIMIT = 56 * 1024 * 1024

F32 = jnp.float32
BF16 = jnp.bfloat16

_C_Q = 0
_C_K = _C_Q + Q_W
_C_V = _C_K + KV_W
_C_GQ = _C_V + KV_W
_C_GK = _C_GQ + GK_W
_C_GV = _C_GK + GK_W
_C_RB = _C_GV + GV_W
_C_GA = _C_RB + GV_W
_C_GB = _C_GA + D_MODEL
_C_LR = _C_GB + D_MODEL
PROJ_PAD = _C_LR + GATE_PAD

GLA_CHUNK = 128
GLA_LEVELS = 7
GLA_MXU_LEVELS = 3
GLA_CHUNKS_PER_STEP = 4


def _params(sem, vmem=VMEM_LIMIT):
    return pltpu.CompilerParams(dimension_semantics=sem, vmem_limit_bytes=vmem)


def _dot(a, b):
    return jnp.dot(a, b, preferred_element_type=F32)


def _dot_nt(a, b):
    return lax.dot_general(a, b, (((1,), (1,)), ((), ())), preferred_element_type=F32)


def _dot_tn(a, b):
    return lax.dot_general(a, b, (((0,), (0,)), ((), ())), preferred_element_type=F32)


def _sigmoid(x):
    return 0.5 * jnp.tanh(0.5 * x) + 0.5


def _silu(x):
    return x * _sigmoid(x)


N_MOD = 6


class _Mod(NamedTuple):
    arr: jax.Array
    layer: int
    idx: int


def _ada_kernel(c_ref, w_ref, b_ref, os_ref, op_ref):
    c = c_ref[...]
    mod = _dot(_silu(c).astype(BF16), w_ref[...].astype(BF16)) + b_ref[...]
    n_s = os_ref.shape[0]
    os_ref[...] = mod[0:n_s]
    for b in range(op_ref.shape[0]):
        op_ref[b] = mod[n_s + b:n_s + b + 1]


def _ada(c_all, ada_w, ada_b, n_sample, n_prompt):
    rows = c_all.shape[0]
    return pl.pallas_call(
        _ada_kernel,
        grid=(DEPTH, N_MOD),
        in_specs=[
            pl.BlockSpec((rows, D_MODEL), lambda l, j: (0, 0)),
            pl.BlockSpec((None, D_MODEL, D_MODEL), lambda l, j: (l, 0, j)),
            pl.BlockSpec((None, 1, D_MODEL), lambda l, j: (l, 0, j)),
        ],
        out_specs=[
            pl.BlockSpec((None, None, n_sample, D_MODEL), lambda l, j: (l, j, 0, 0)),
            pl.BlockSpec((None, None, n_prompt, 1, D_MODEL), lambda l, j: (l, j, 0, 0, 0)),
        ],
        out_shape=[
            jax.ShapeDtypeStruct((DEPTH, N_MOD, n_sample, D_MODEL), F32),
            jax.ShapeDtypeStruct((DEPTH, N_MOD, n_prompt, 1, D_MODEL), F32),
        ],
        compiler_params=_params(("parallel", "parallel")),
        name="ada_mod",
    )(c_all, ada_w, ada_b.reshape(DEPTH, 1, N_MOD * D_MODEL))


def _mixin_kernel(x_ref, sh_ref, sc_ref, n1_ref, w_ref, bd_ref, qg_ref, kg_ref, wa2_ref, ba_ref,
                  q_ref, k_ref, v_ref, gq_ref, gk_ref, gv_ref, la_ref, rb_ref, sga_ref, sgb_ref):
    x = x_ref[...]
    ms = jnp.mean(x * x, axis=-1, keepdims=True)
    h = x * lax.rsqrt(ms + EPS) * n1_ref[...]
    h = h * (1.0 + sc_ref[...]) + sh_ref[...]
    hb = h.astype(BF16)

    def proj(a, b):
        return _dot(hb, w_ref[:, a:b])

    q = proj(_C_Q, _C_K)
    ssq = _dot((q * q).astype(BF16), bd_ref[...])
    q_ref[...] = (q * lax.rsqrt(ssq * (1.0 / HEAD_DIM) + EPS) * qg_ref[...]).astype(BF16)
    k = proj(_C_K, _C_V)
    ssk = _dot((k * k).astype(BF16), bd_ref[0:KV_W, 0:KV_W])
    k_ref[...] = k * lax.rsqrt(ssk * (1.0 / HEAD_DIM) + EPS) * kg_ref[...]
    v_ref[...] = proj(_C_V, _C_GQ)
    gq_ref[...] = proj(_C_GQ, _C_GK) * (GLA_DK ** -0.5)
    gk_ref[...] = proj(_C_GK, _C_GV)
    gv_ref[...] = proj(_C_GV, _C_RB)
    rb_ref[...] = _silu(proj(_C_RB, _C_GA)).astype(BF16)
    sga_ref[...] = _sigmoid(proj(_C_GA, _C_GB)).astype(BF16)
    sgb_ref[...] = _sigmoid(proj(_C_GB, _C_LR)).astype(BF16)
    ga = proj(_C_LR, PROJ_PAD)
    xg = _dot(ga.astype(BF16), wa2_ref[...]) + ba_ref[...]
    la_ref[...] = (jnp.minimum(xg, 0.0) - jnp.log1p(jnp.exp(-jnp.abs(xg)))) * (1.0 / GATE_TAU)


def _mixin(x, sh, sc, n1, w, bd, qg, kg, wa2, ba, *, tm):
    n = x.shape[0]
    nt = n // tm

    def row(i):
        return (i, 0)

    def const(shape):
        return pl.BlockSpec(shape, lambda i: (0,) * len(shape), pipeline_mode=pl.Buffered(1))

    def out(width, dtype):
        return pl.BlockSpec((tm, width), row), jax.ShapeDtypeStruct((n, width), dtype)

    outs = [out(Q_W, BF16), out(KV_W, F32), out(KV_W, F32), out(GK_W, F32), out(GK_W, F32),
            out(GV_W, F32), out(GK_W, F32), out(GV_W, BF16), out(D_MODEL, BF16), out(D_MODEL, BF16)]
    return pl.pallas_call(
        _mixin_kernel,
        grid=(nt,),
        in_specs=[
            pl.BlockSpec((tm, D_MODEL), row), _mod_spec(tm, n, sh), _mod_spec(tm, n, sc), const((1, D_MODEL)),
            const((D_MODEL, PROJ_PAD)), const((Q_W, Q_W)), const((1, Q_W)), const((1, KV_W)),
            const((GATE_PAD, GK_W)), const((1, GK_W)),
        ],
        out_specs=[o[0] for o in outs],
        out_shape=[o[1] for o in outs],
        compiler_params=_params(("parallel",)),
        name="mixer_in",
    )(x, sh.arr, sc.arr, n1, w, bd, qg, kg, wa2, ba)


SWA_BLOCKS = 8


def _head_slope(h):
    return float(2.0 ** (-8.0 * (h + 1) / N_HEADS))


def _swa_prompt_bias():
    blk = WINDOW
    dist = np.arange(blk)[:, None] + blk - np.arange(2 * blk)[None, :]
    slopes = np.asarray([_head_slope(h) for h in range(N_HEADS)])[:, None, None]
    return np.where((dist >= 0) & (dist <= WINDOW), -slopes * dist, -np.inf).astype(np.float32)


def _swa_prompt_kernel(sink_ref, q_ref, kp_ref, kc_ref, vp_ref, vc_ref, bias_ref, o_ref):
    n = pl.program_id(1)
    blk = WINDOW
    kall = jnp.concatenate([kp_ref[...], kc_ref[...]], axis=0).astype(BF16)
    vall = jnp.concatenate([vp_ref[...], vc_ref[...]], axis=0).astype(BF16)
    col = lax.broadcasted_iota(jnp.int32, (blk, 2 * blk), 1)
    first_key = jnp.where(n > 0, 0, blk)
    klane = lax.broadcasted_iota(jnp.int32, kall.shape, 1)
    kall_kv = [jnp.where((klane // HEAD_DIM) == kv, kall, jnp.zeros_like(kall)) for kv in range(N_KV)]
    olane = lax.broadcasted_iota(jnp.int32, (blk, KV_W), 1)
    q = q_ref[...]
    for j in range(SWA_BLOCKS):
        vv = vall[j * blk:(j + 2) * blk]
        outs = []
        for g in range(GROUP):
            qp = q[j * blk:(j + 1) * blk, g * KV_W:(g + 1) * KV_W]
            pair = []
            for kv in range(N_KV):
                h = kv * GROUP + g
                s = _dot_nt(qp, kall_kv[kv][j * blk:(j + 2) * blk]) + bias_ref[h]
                if j == 0:
                    s = jnp.where(col >= first_key, s, -jnp.inf)
                sink = sink_ref[h]
                m = jnp.maximum(jnp.max(s, axis=-1, keepdims=True), sink)
                p = jnp.exp(s - m)
                den = jnp.sum(p, axis=-1, keepdims=True) + jnp.exp(sink - m)
                pair.append(_dot(p.astype(BF16), vv) * (1.0 / den))
            outs.append(jnp.where(olane < HEAD_DIM, pair[0], pair[1]))
        o_ref[j * blk:(j + 1) * blk, :] = jnp.concatenate(outs, axis=-1).astype(BF16)


def _swa_prompt(sinks, q, k, v, bias):
    nb = SEQ // WINDOW
    steps = nb // SWA_BLOCKS
    tq = SWA_BLOCKS * WINDOW

    def cur(b, n):
        return (b * steps + n, 0)

    def prev(b, n):
        return (b * nb + jnp.maximum(n * SWA_BLOCKS - 1, 0), 0)

    return pl.pallas_call(
        _swa_prompt_kernel,
        grid=(BATCH, steps),
        in_specs=[
            pl.BlockSpec(memory_space=pltpu.SMEM),
            pl.BlockSpec((tq, Q_W), cur),
            pl.BlockSpec((WINDOW, KV_W), prev), pl.BlockSpec((tq, KV_W), cur),
            pl.BlockSpec((WINDOW, KV_W), prev), pl.BlockSpec((tq, KV_W), cur),
            pl.BlockSpec(bias.shape, lambda b, n: (0, 0, 0), pipeline_mode=pl.Buffered(1)),
        ],
        out_specs=pl.BlockSpec((tq, Q_W), cur),
        out_shape=jax.ShapeDtypeStruct((BATCH * SEQ, Q_W), BF16),
        compiler_params=_params(("parallel", "parallel")),
        name="swa_prompt",
    )(sinks, q, k, k, v, v, bias)


SAMPLE_TB = 8
CACHE_ROWS = WINDOW * N_KV


def _sample_head_of_row():
    j = np.arange(N_HEADS)
    return (j % N_KV) * GROUP + j // N_KV


def _swa_sample_bias():
    j = np.arange(N_HEADS)[:, None]
    c = np.arange(CACHE_ROWS)[None, :]
    slope = 2.0 ** (-8.0 * (_sample_head_of_row()[:, None] + 1) / N_HEADS)
    bias = -slope * (WINDOW - c // N_KV)
    return np.where(c % N_KV == j % N_KV, bias, -np.inf).astype(np.float32)


def _swa_sample_kernel(q_ref, kn_ref, vn_ref, ck_ref, cv_ref, bias_ref, sk_ref, o_ref, ok_ref, ov_ref):
    rows = CACHE_ROWS
    sink = sk_ref[...][:, 0:1]
    q = q_ref[...]
    kn = kn_ref[...]
    vn = vn_ref[...]
    kc = ck_ref[...]
    vc = cv_ref[...]
    kn8 = jnp.concatenate([kn] * GROUP, axis=1)
    vn8 = jnp.concatenate([vn] * GROUP, axis=1)
    s = lax.dot_general(q.astype(BF16), kc.astype(BF16), (((2,), (2,)), ((0,), (0,))),
                        preferred_element_type=F32) + bias_ref[...]
    s_new = jnp.sum(q * kn8, axis=-1, keepdims=True)
    m = jnp.maximum(jnp.maximum(jnp.max(s, axis=-1, keepdims=True), s_new), sink)
    p = jnp.exp(s - m)
    p_new = jnp.exp(s_new - m)
    den = jnp.sum(p, axis=-1, keepdims=True) + p_new + jnp.exp(sink - m)
    o = lax.dot_general(p.astype(BF16), vc.astype(BF16), (((2,), (1,)), ((0,), (0,))),
                        preferred_element_type=F32) + p_new * vn8
    o_ref[...] = o * (1.0 / den)
    ok_ref[:, pl.ds(0, rows - N_KV), :] = kc[:, N_KV:, :]
    ok_ref[:, pl.ds(rows - N_KV, N_KV), :] = kn
    ov_ref[:, pl.ds(0, rows - N_KV), :] = vc[:, N_KV:, :]
    ov_ref[:, pl.ds(rows - N_KV, N_KV), :] = vn


def _swa_sample(q, kn, vn, ck, cv, bias, sinks, *, layer):
    tb = 2 * SAMPLE_TB
    nb = DEC_BATCH // tb

    def const(shape):
        return pl.BlockSpec(shape, lambda i: (0,) * len(shape))

    def per_seq(*dims):
        return pl.BlockSpec((tb,) + dims, lambda i: (i, 0, 0))

    cache_spec = pl.BlockSpec((None, tb, CACHE_ROWS, HEAD_DIM), lambda i: (layer, i, 0, 0))
    return pl.pallas_call(
        _swa_sample_kernel,
        grid=(nb,),
        in_specs=[per_seq(N_HEADS, HEAD_DIM), per_seq(N_KV, HEAD_DIM), per_seq(N_KV, HEAD_DIM),
                  cache_spec, cache_spec,
                  const((N_HEADS, CACHE_ROWS)), const((N_HEADS, LANES))],
        out_specs=[per_seq(N_HEADS, HEAD_DIM), per_seq(CACHE_ROWS, HEAD_DIM), per_seq(CACHE_ROWS, HEAD_DIM)],
        out_shape=[
            jax.ShapeDtypeStruct((DEC_BATCH, N_HEADS, HEAD_DIM), F32),
            jax.ShapeDtypeStruct((DEC_BATCH, CACHE_ROWS, HEAD_DIM), F32),
            jax.ShapeDtypeStruct((DEC_BATCH, CACHE_ROWS, HEAD_DIM), F32),
        ],
        compiler_params=_params(("parallel",)),
        name="swa_sample",
    )(q, kn, vn, ck, cv, bias, sinks)


def _gla_cum_matrix():
    c = GLA_CHUNK
    tri = np.tril(np.ones((c, c), np.float32))
    i = np.arange(c)
    blocks = []
    for lvl in range(GLA_MXU_LEVELS):
        half = 1 << lvl
        mid = (i // (2 * half)) * (2 * half) + half - 1
        blocks.append(tri - tri[mid])
    blocks.append(tri)
    return np.concatenate(blocks, axis=0)


def _split3(x):
    hi = x.astype(BF16)
    r1 = x - hi.astype(F32)
    mid = r1.astype(BF16)
    lo = (r1 - mid.astype(F32)).astype(BF16)
    return hi, mid, lo


def _gla_level_masks():
    cl = GLA_CHUNK
    r = np.arange(cl)
    upper = np.stack([np.broadcast_to(((r >> lvl) & 1)[:, None], (cl, GK_W)) for lvl in range(GLA_LEVELS)])
    ri = np.tile(r, GLA_HEADS)[:, None]
    pairs = [(ri >> (lvl + 1)) == (r[None, :] >> (lvl + 1)) for lvl in range(GLA_LEVELS)]
    pairs.append(ri == r[None, :])
    return upper.astype(np.float32), np.stack(pairs).astype(np.float32)


def _gla_prompt_kernel(q_ref, k_ref, la_ref, v_ref, rb_ref, gn_ref, ut_ref, up_ref, pm_ref,
                       o_ref, s_ref, st_ref):
    c = pl.program_id(1)
    cl = GLA_CHUNK

    @pl.when(c == 0)
    def _():
        st_ref[0] = jnp.zeros(st_ref.shape[1:], F32)

    ut = ut_ref[...]
    lane = lax.broadcasted_iota(jnp.int32, (cl, GK_W), 1)
    head_of_lane = lane // GLA_DK
    ones = jnp.ones((cl, LANES), BF16)
    state = [st_ref[c % 2, h] for h in range(GLA_HEADS)]
    for cc in range(GLA_CHUNKS_PER_STEP):
        rows = slice(cc * cl, (cc + 1) * cl)
        hi, mid, lo = _split3(la_ref[rows, :])
        tall = _dot(ut, hi) + _dot(ut, mid) + _dot(ut, lo)
        q = q_ref[rows, :]
        k = k_ref[rows, :]
        attn_all = jnp.zeros((GLA_HEADS * cl, cl), F32)
        cum = tall[GLA_MXU_LEVELS * cl:(GLA_MXU_LEVELS + 1) * cl]
        for lvl in range(GLA_LEVELS + 1):
            if lvl < GLA_LEVELS:
                if lvl < GLA_MXU_LEVELS:
                    t_lvl = tall[lvl * cl:(lvl + 1) * cl]
                else:
                    half = 1 << lvl
                    mids = [jnp.broadcast_to(cum[b0 + half - 1:b0 + half, :], (2 * half, GK_W))
                            for b0 in range(0, cl, 2 * half)]
                    t_lvl = cum - (mids[0] if len(mids) == 1 else jnp.concatenate(mids, axis=0))
                e = jnp.exp(-jnp.abs(t_lvl))
                e_up = e * up_ref[lvl]
                qt = (q * e_up).astype(BF16)
                kt = (k * (e - e_up)).astype(BF16)
            else:
                qt = q.astype(BF16)
                kt = k.astype(BF16)
            q_heads = jnp.concatenate(
                [jnp.where(head_of_lane == h, qt, jnp.zeros_like(qt)) for h in range(GLA_HEADS)], axis=0)
            attn_all = attn_all + _dot_nt(q_heads, kt) * pm_ref[lvl]

        last = cum[cl - 1:cl, :]
        qe = (q * jnp.exp(cum)).astype(BF16)
        kd = (k * jnp.exp(last - cum)).astype(BF16)
        dec = jnp.exp(_dot_tn(hi, ones) + _dot_tn(mid, ones) + _dot_tn(lo, ones))
        v = v_ref[rows, :]
        rb = rb_ref[rows, :]
        outs = []
        for h in range(GLA_HEADS):
            vh = v[:, h * GLA_DV:(h + 1) * GLA_DV].astype(BF16)
            o = _dot(attn_all[h * cl:(h + 1) * cl].astype(BF16), vh) + _dot(qe, state[h].astype(BF16))
            kdh = jnp.where(head_of_lane == h, kd, jnp.zeros_like(kd))
            state[h] = state[h] * dec + _dot_tn(kdh, vh)
            ms = jnp.mean(o * o, axis=-1, keepdims=True)
            g = o * lax.rsqrt(ms + EPS) * gn_ref[...]
            outs.append(g * rb[:, h * GLA_DV:(h + 1) * GLA_DV].astype(F32))
        o_ref[rows, :] = jnp.concatenate(outs, axis=-1).astype(BF16)

    for h in range(GLA_HEADS):
        st_ref[(c + 1) % 2, h] = state[h]

    @pl.when(c == pl.num_programs(1) - 1)
    def _():
        fin = (SEQ // (GLA_CHUNK * GLA_CHUNKS_PER_STEP)) % 2
        s_ref[...] = st_ref[fin, 0] + st_ref[fin, 1] + st_ref[fin, 2] + st_ref[fin, 3]


def _gla_prompt(gq, gk, la, gv, rbs, gn, ut, upper, pairs):
    cl = GLA_CHUNK * GLA_CHUNKS_PER_STEP
    nc = SEQ // cl

    def row(b, c):
        return (b * nc + c, 0)

    def const(shape):
        return pl.BlockSpec(shape, lambda b, c: (0,) * len(shape), pipeline_mode=pl.Buffered(1))

    return pl.pallas_call(
        _gla_prompt_kernel,
        grid=(BATCH, nc),
        in_specs=[
            pl.BlockSpec((cl, GK_W), row), pl.BlockSpec((cl, GK_W), row), pl.BlockSpec((cl, GK_W), row),
            pl.BlockSpec((cl, GV_W), row), pl.BlockSpec((cl, GV_W), row),
            const((1, GLA_DV)), const(ut.shape), const(upper.shape), const(pairs.shape),
        ],
        out_specs=[
            pl.BlockSpec((cl, GV_W), row),
            pl.BlockSpec((None, GK_W, GLA_DV), lambda b, c: (b, 0, 0)),
        ],
        out_shape=[
            jax.ShapeDtypeStruct((BATCH * SEQ, GV_W), BF16),
            jax.ShapeDtypeStruct((BATCH, GK_W, GLA_DV), F32),
        ],
        scratch_shapes=[pltpu.VMEM((2, GLA_HEADS, GK_W, GLA_DV), F32)],
        compiler_params=_params(("parallel", "arbitrary")),
        name="gla_prompt",
    )(gq, gk, la, gv, rbs, gn, ut, upper, pairs)


def _gla_sample_kernel(q_ref, k_ref, la_ref, v_ref, rb_ref, gn_ref, s_ref, o_ref, so_ref):
    tb = SAMPLE_TB
    dec = jnp.exp(la_ref[...])
    pieces = []
    for x in (dec, k_ref[...], q_ref[...]):
        hi, mid, lo = _split3(x)
        stacked = jnp.concatenate(
            [hi.astype(F32), mid.astype(F32), lo.astype(F32), jnp.zeros_like(x)], axis=0)
        pieces.append(stacked.astype(BF16))
    prow = lax.broadcasted_iota(jnp.int32, (4 * tb, LANES), 0)
    rb = rb_ref[...].astype(F32)
    v = v_ref[...]
    for bi in range(tb):
        sel = jnp.where((prow % tb) == bi, 1.0, 0.0).astype(BF16)
        a_col, k_col, q_col = [_dot_tn(p, sel) for p in pieces]
        for h in range(GLA_HEADS):
            rs = slice(h * GLA_DK, (h + 1) * GLA_DK)
            vs = slice(h * GLA_DV, (h + 1) * GLA_DV)
            s_new = a_col[rs] * s_ref[bi, rs, :] + k_col[rs] * v[bi:bi + 1, vs]
            so_ref[bi, rs, :] = s_new
            o = jnp.sum(q_col[rs] * s_new, axis=0, keepdims=True)
            ms = jnp.mean(o * o, axis=-1, keepdims=True)
            g = o * lax.rsqrt(ms + EPS) * gn_ref[...]
            o_ref[bi:bi + 1, vs] = g * rb[bi:bi + 1, vs]


def _gla_sample(gq, gk, la, gv, rbs, gn, state, *, layer):
    tb = SAMPLE_TB
    nb = DEC_BATCH // tb

    def row(w):
        return pl.BlockSpec((tb, w), lambda i: (i, 0))

    st_spec = pl.BlockSpec((tb, GK_W, GLA_DV), lambda i: (i, 0, 0))
    st_in_spec = pl.BlockSpec((None, tb, GK_W, GLA_DV), lambda i: (layer, i, 0, 0))
    return pl.pallas_call(
        _gla_sample_kernel,
        grid=(nb,),
        in_specs=[row(GK_W), row(GK_W), row(GK_W), row(GV_W), row(GV_W),
                  pl.BlockSpec((1, GLA_DV), lambda i: (0, 0)), st_in_spec],
        out_specs=[row(GV_W), st_spec],
        out_shape=[
            jax.ShapeDtypeStruct((DEC_BATCH, GV_W), F32),
            jax.ShapeDtypeStruct((DEC_BATCH, GK_W, GLA_DV), F32),
        ],
        compiler_params=_params(("parallel",)),
        name="gla_sample",
    )(gq, gk, la, gv, rbs, gn, state)


def _merge_kernel(x_ref, a_ref, g_ref, sga_ref, sgb_ref, wpa_ref, wpb_ref, wo_ref,
                  g1_ref, sh_ref, sc_ref, n2_ref, *rest):
    with_router = len(rest) == 8
    x1_ref, h2_ref = rest[3:5] if with_router else rest
    ya = _dot(a_ref[...].astype(BF16), wpa_ref[...])
    yb = _dot(g_ref[...].astype(BF16), wpb_ref[...])
    merged = sga_ref[...].astype(F32) * ya + sgb_ref[...].astype(F32) * yb
    mix = _dot(merged.astype(BF16), wo_ref[...])
    x1 = x_ref[...] + g1_ref[...] * mix
    x1_ref[...] = x1
    ms = jnp.mean(x1 * x1, axis=-1, keepdims=True)
    h = x1 * lax.rsqrt(ms + EPS) * n2_ref[...]
    h2 = h * (1.0 + sc_ref[...]) + sh_ref[...]
    h2_ref[...] = h2.astype(h2_ref.dtype)
    if with_router:
        rw_ref, rbias_ref, tri_ref = rest[0:3]
        route_ref, route_t_ref, cnt_ref = rest[5:8]
        logits = _dot(h2.astype(BF16), rw_ref[...]) + rbias_ref[...]
        packed, counts = _route_pack(logits, tri_ref[...])
        route_ref[...] = packed
        route_t_ref[...] = packed.T[0:route_t_ref.shape[0], :]
        cnt_ref[...] = jnp.broadcast_to(counts, cnt_ref.shape)


def _mod_spec(tm, n, mod):
    if mod.arr.ndim == 4:
        return pl.BlockSpec((None, None, tm, D_MODEL), lambda i, *_: (mod.layer, mod.idx, i, 0))
    tiles_per_seq = (n // mod.arr.shape[2]) // tm
    return pl.BlockSpec((None, None, None, 1, D_MODEL),
                        lambda i, *_: (mod.layer, mod.idx, i // tiles_per_seq, 0, 0))


def _merge(x, a, g, sga, sgb, wpa, wpb, wo, g1, sh2, sc2, n2, *, tm, h2_dtype=BF16, router=None):
    n = x.shape[0]

    def row(w):
        return pl.BlockSpec((tm, w), lambda i: (i, 0))

    def const(shape):
        return pl.BlockSpec(shape, lambda i: (0,) * len(shape))

    in_specs = [row(D_MODEL), row(a.shape[1]), row(GV_W), row(D_MODEL), row(D_MODEL),
                const(wpa.shape), const(wpb.shape), const(wo.shape),
                _mod_spec(tm, n, g1), _mod_spec(tm, n, sh2), _mod_spec(tm, n, sc2), const((1, D_MODEL))]
    out_specs = [row(D_MODEL), row(D_MODEL)]
    out_shape = [jax.ShapeDtypeStruct((n, D_MODEL), F32), jax.ShapeDtypeStruct((n, D_MODEL), h2_dtype)]
    args = [x, a, g, sga, sgb, wpa, wpb, wo, g1.arr, sh2.arr, sc2.arr, n2]
    if router is not None:
        tri = jnp.asarray(np.tril(np.ones((tm, tm), np.float32), -1), BF16)
        in_specs += [const(router[0].shape), const(router[1].shape), const((tm, tm))]
        out_specs += [row(ROUTER_PAD), pl.BlockSpec((8, tm), lambda i: (0, i)),
                      pl.BlockSpec((None, 8, ROUTER_PAD), lambda i: (i, 0, 0))]
        out_shape += [jax.ShapeDtypeStruct((n, ROUTER_PAD), F32), jax.ShapeDtypeStruct((8, n), F32),
                      jax.ShapeDtypeStruct((n // tm, 8, ROUTER_PAD), F32)]
        args += [router[0], router[1], tri]
    return pl.pallas_call(
        _merge_kernel,
        grid=(n // tm,),
        in_specs=in_specs,
        out_specs=out_specs,
        out_shape=out_shape,
        compiler_params=_params(("parallel",)),
        name="merge_out",
    )(*args)


def _swiglu(hb, wg_ref, wu_ref, wd_ref):
    act = (_silu(_dot(hb, wg_ref[...])) * _dot(hb, wu_ref[...])).astype(BF16)
    return _dot(act, wd_ref[...])


def _ffn_kernel(h_ref, x_ref, g2_ref, wg_ref, wu_ref, wd_ref, o_ref, acc_ref):
    f = pl.program_id(1)
    y = _swiglu(h_ref[...], wg_ref, wu_ref, wd_ref)

    @pl.when(f == 0)
    def _():
        acc_ref[...] = y

    @pl.when(f > 0)
    def _():
        acc_ref[...] += y

    @pl.when(f == pl.num_programs(1) - 1)
    def _():
        o_ref[...] = x_ref[...] + g2_ref[...] * acc_ref[...]


def _ffn(h2, x1, g2, wg, wu, wd, *, tm, tf):
    n = h2.shape[0]
    mod_spec = _mod_spec(tm, n, g2)
    return pl.pallas_call(
        _ffn_kernel,
        grid=(n // tm, D_FF // tf),
        in_specs=[
            pl.BlockSpec((tm, D_MODEL), lambda i, f: (i, 0)),
            pl.BlockSpec((tm, D_MODEL), lambda i, f: (i, 0)),
            mod_spec,
            pl.BlockSpec((D_MODEL, tf), lambda i, f: (0, f)),
            pl.BlockSpec((D_MODEL, tf), lambda i, f: (0, f)),
            pl.BlockSpec((tf, D_MODEL), lambda i, f: (f, 0)),
        ],
        out_specs=pl.BlockSpec((tm, D_MODEL), lambda i, f: (i, 0)),
        out_shape=jax.ShapeDtypeStruct((n, D_MODEL), F32),
        scratch_shapes=[pltpu.VMEM((tm, D_MODEL), F32)],
        compiler_params=_params(("parallel", "arbitrary")),
        name="ffn_dense",
    )(h2, x1, g2.arr, wg, wu, wd)


def _moe_kernel(h_ref, x_ref, g2_ref, rw_ref, rbias_ref, wg_ref, wu_ref, wd_ref, o_ref, acc_ref, gate_ref):
    e = pl.program_id(1)
    hb = h_ref[...]
    tm = hb.shape[0]
    lane = lax.broadcasted_iota(jnp.int32, (tm, ROUTER_PAD), 1).astype(F32)

    @pl.when(e == 0)
    def _():
        logits = _dot(hb, rw_ref[...]) + rbias_ref[...]
        i1, i2, p1, p2 = _top2(logits, lane)
        gate_ref[...] = jnp.where(lane == i1, p1, 0.0) + jnp.where(lane == i2, p2, 0.0)
        acc_ref[...] = jnp.zeros_like(acc_ref)

    ge = jnp.sum(jnp.where(lane == e.astype(F32), gate_ref[...], 0.0), axis=-1, keepdims=True)
    act = (_silu(_dot(hb, wg_ref[...])) * _dot(hb, wu_ref[...])).astype(BF16)
    acc_ref[...] += ge * _dot(act, wd_ref[...])

    @pl.when(e == pl.num_programs(1) - 1)
    def _():
        o_ref[...] = x_ref[...] + g2_ref[...] * acc_ref[...]


def _moe(h2, x1, g2, rw, rbias, wg, wu, wd, *, tm):
    n = h2.shape[0]
    mod_spec = _mod_spec(tm, n, g2)
    fe = D_FF_EXPERT
    return pl.pallas_call(
        _moe_kernel,
        grid=(n // tm, N_EXPERTS),
        in_specs=[
            pl.BlockSpec((tm, D_MODEL), lambda i, e: (i, 0)),
            pl.BlockSpec((tm, D_MODEL), lambda i, e: (i, 0)),
            mod_spec,
            pl.BlockSpec((D_MODEL, ROUTER_PAD), lambda i, e: (0, 0)),
            pl.BlockSpec((1, ROUTER_PAD), lambda i, e: (0, 0)),
            pl.BlockSpec((None, D_MODEL, fe), lambda i, e: (e, 0, 0)),
            pl.BlockSpec((None, D_MODEL, fe), lambda i, e: (e, 0, 0)),
            pl.BlockSpec((None, fe, D_MODEL), lambda i, e: (e, 0, 0)),
        ],
        out_specs=pl.BlockSpec((tm, D_MODEL), lambda i, e: (i, 0)),
        out_shape=jax.ShapeDtypeStruct((n, D_MODEL), F32),
        scratch_shapes=[pltpu.VMEM((tm, D_MODEL), F32), pltpu.VMEM((tm, ROUTER_PAD), F32)],
        compiler_params=_params(("parallel", "arbitrary")),
        name="moe",
    )(h2, x1, g2.arr, rw, rbias, wg, wu, wd)


def _top2(logits, lane):
    lg = jnp.where(lane < N_EXPERTS, logits, -jnp.inf)
    m1 = jnp.max(lg, axis=-1, keepdims=True)
    i1 = jnp.min(jnp.where(lg == m1, lane, float(ROUTER_PAD)), axis=-1, keepdims=True)
    lg2 = jnp.where(lane == i1, -jnp.inf, lg)
    m2 = jnp.max(lg2, axis=-1, keepdims=True)
    i2 = jnp.min(jnp.where(lg2 == m2, lane, float(ROUTER_PAD)), axis=-1, keepdims=True)
    e2 = jnp.exp(m2 - m1)
    p1 = 1.0 / (1.0 + e2)
    return i1, i2, p1, e2 * p1


def _route_pack(logits, tri):
    lane = lax.broadcasted_iota(jnp.int32, logits.shape, 1).astype(F32)
    i1, i2, p1, p2 = _top2(logits, lane)
    oh1 = jnp.where(lane == i1, 1.0, 0.0)
    oh2 = jnp.where(lane == i2, 1.0, 0.0)
    cnt1 = jnp.sum(oh1, axis=0, keepdims=True)
    cnt2 = jnp.sum(oh2, axis=0, keepdims=True)
    rank1 = jnp.sum(_dot(tri, oh1.astype(BF16)) * oh1, axis=-1, keepdims=True)
    rank2 = jnp.sum((_dot(tri, oh2.astype(BF16)) + cnt1) * oh2, axis=-1, keepdims=True)
    packed = jnp.zeros_like(logits)
    for k, val in enumerate((p1, p2, i1, i2, rank1, rank2)):
        packed = jnp.where(lane == float(k), val, packed)
    return packed, cnt1 + cnt2


def _route_tables(route_t, tile_counts, tm, n_tiles):
    n = route_t.shape[1]
    tile_cnt = tile_counts[:, 0, :N_EXPERTS].astype(jnp.int32)
    cnt = jnp.sum(tile_cnt, axis=0)
    gsz = ((cnt + tm - 1) // tm) * tm
    gend = jnp.cumsum(gsz)
    seg_start = (gend - gsz)[None, :] + jnp.cumsum(tile_cnt, axis=0) - tile_cnt
    seg_of_token = jnp.repeat(seg_start.T, n // tile_cnt.shape[0], axis=1)
    experts = jnp.arange(N_EXPERTS, dtype=jnp.int32)[:, None]
    pos = []
    for k in range(2):
        e_k = route_t[2 + k].astype(jnp.int32)
        rank_k = route_t[4 + k].astype(jnp.int32)
        pos.append(jnp.sum(jnp.where(e_k[None, :] == experts, seg_of_token, 0), axis=0) + rank_k)
    pos = jnp.concatenate(pos)
    tile_start = jnp.arange(n_tiles, dtype=jnp.int32) * tm
    tile_expert = jnp.sum((tile_start[:, None] >= gend[None, :]).astype(jnp.int32), axis=1)
    tile_expert = jnp.minimum(tile_expert, N_EXPERTS - 1)
    live = (gend[-1] // tm).reshape(1)
    return pos, tile_expert, live, gend - gsz + cnt, gend


def _row_copy_wait(src_hbm, dst, sem, rows):
    pltpu.make_async_copy(src_hbm.at[pl.ds(0, rows)], dst, sem).wait()


def _moe_dispatch_kernel(pos_ref, pad_lo_ref, pad_hi_ref, live_ref, h_ref, xs_hbm, zbuf, sem, zsem):
    i = pl.program_id(0)
    tt = h_ref.shape[0]
    n = tt * pl.num_programs(0)
    tm = zbuf.shape[0]

    base = i * tt
    for r in range(tt):
        row = h_ref.at[pl.ds(r, 1)]
        pltpu.make_async_copy(row, xs_hbm.at[pl.ds(pos_ref[base + r], 1)], sem).start(priority=0)
        pltpu.make_async_copy(row, xs_hbm.at[pl.ds(pos_ref[n + base + r], 1)], sem).start(priority=1)
    for _ in range(2):
        pltpu.make_async_copy(h_ref, xs_hbm.at[pl.ds(0, tt)], sem).wait()

    @pl.when(i == pl.num_programs(0) - 1)
    def _():
        zbuf[...] = jnp.zeros(zbuf.shape, F32)

        def fill_row(p):
            return pltpu.make_async_copy(zbuf.at[pl.ds(0, 1)], xs_hbm.at[pl.ds(p, 1)], zsem)

        def fill_tile(t):
            return pltpu.make_async_copy(zbuf, xs_hbm.at[pl.ds(pl.multiple_of(t * tm, tm), tm)], zsem)

        for e in range(N_EXPERTS):
            lo, hi = pad_lo_ref[e], pad_hi_ref[e]
            lax.fori_loop(lo, hi, lambda p, c: (fill_row(p).start(), c)[1], 0)
            lax.fori_loop(lo, hi, lambda p, c: (fill_row(p).wait(), c)[1], 0)
        lo, hi = live_ref[0], xs_hbm.shape[0] // tm
        lax.fori_loop(lo, hi, lambda t, c: (fill_tile(t).start(), c)[1], 0)
        lax.fori_loop(lo, hi, lambda t, c: (fill_tile(t).wait(), c)[1], 0)


def _moe_dispatch(h2, pos, pad_lo, pad_hi, live, *, tt, tm, n_tiles):
    n = h2.shape[0]
    grid_spec = pltpu.PrefetchScalarGridSpec(
        num_scalar_prefetch=4,
        grid=(n // tt,),
        in_specs=[pl.BlockSpec((tt, D_MODEL), lambda i, *_: (i, 0))],
        out_specs=pl.BlockSpec(memory_space=pl.ANY),
        scratch_shapes=[pltpu.VMEM((tm, D_MODEL), F32), pltpu.SemaphoreType.DMA(()),
                        pltpu.SemaphoreType.DMA(())],
    )
    return pl.pallas_call(
        _moe_dispatch_kernel,
        grid_spec=grid_spec,
        out_shape=jax.ShapeDtypeStruct((n_tiles * tm, D_MODEL), F32),
        compiler_params=_params(("arbitrary",)),
        name="moe_dispatch",
    )(pos, pad_lo, pad_hi, live, h2)


def _moe_expert_kernel(te_ref, live_ref, x_ref, wg_ref, wu_ref, wd_ref, y_ref):
    del te_ref
    is_live = pl.program_id(0) < live_ref[0]

    @pl.when(is_live)
    def _():
        y_ref[...] = _swiglu(x_ref[...].astype(BF16), wg_ref, wu_ref, wd_ref)

    @pl.when(jnp.logical_not(is_live))
    def _():
        y_ref[...] = jnp.zeros(y_ref.shape, F32)


def _moe_experts(xs, tile_expert, live, wg, wu, wd, *, tm, n_tiles):
    fe = D_FF_EXPERT

    def in_tile(t, te, live):
        return (jnp.minimum(t, live[0] - 1), 0)

    def out_tile(t, te, live):
        return (t, 0)

    def expert(t, te, live):
        return (te[t], 0, 0)

    grid_spec = pltpu.PrefetchScalarGridSpec(
        num_scalar_prefetch=2,
        grid=(n_tiles,),
        in_specs=[
            pl.BlockSpec((tm, D_MODEL), in_tile),
            pl.BlockSpec((None, D_MODEL, fe), expert),
            pl.BlockSpec((None, D_MODEL, fe), expert),
            pl.BlockSpec((None, fe, D_MODEL), expert),
        ],
        out_specs=pl.BlockSpec((tm, D_MODEL), out_tile),
    )
    return pl.pallas_call(
        _moe_expert_kernel,
        grid_spec=grid_spec,
        out_shape=jax.ShapeDtypeStruct(xs.shape, F32),
        compiler_params=_params(("arbitrary",)),
        name="moe_experts",
    )(tile_expert, live, xs, wg, wu, wd)


def _moe_combine_kernel(pos_ref, x_ref, g2_ref, r_ref, ys_hbm, o_ref, buf, sem):
    i = pl.program_id(0)
    nt = pl.num_programs(0)
    tt = x_ref.shape[0]
    n = nt * tt

    def fetch(tile, sl):
        base = tile * tt
        for r in range(tt):
            for s in range(2):
                row = pos_ref[s * n + base + r]
                pltpu.make_async_copy(ys_hbm.at[pl.ds(row, 1)], buf.at[sl, pl.ds(s * tt + r, 1)],
                                      sem.at[sl]).start(priority=s)

    @pl.when(i == 0)
    def _():
        fetch(0, 0)

    @pl.when(i + 1 < nt)
    def _():
        fetch(i + 1, (i + 1) % 2)

    sl = i % 2
    _row_copy_wait(ys_hbm, buf.at[sl], sem.at[sl], 2 * tt)
    r = r_ref[...]
    f = r[:, 0:1] * buf[sl, pl.ds(0, tt), :] + r[:, 1:2] * buf[sl, pl.ds(tt, tt), :]
    o_ref[...] = x_ref[...] + g2_ref[...] * f


def _moe_combine(x1, g2, route, pos, ys, *, tt):
    n = x1.shape[0]
    grid_spec = pltpu.PrefetchScalarGridSpec(
        num_scalar_prefetch=1,
        grid=(n // tt,),
        in_specs=[
            pl.BlockSpec((tt, D_MODEL), lambda i, pos: (i, 0)),
            _mod_spec(tt, n, g2),
            pl.BlockSpec((tt, ROUTER_PAD), lambda i, pos: (i, 0)),
            pl.BlockSpec(memory_space=pl.ANY),
        ],
        out_specs=pl.BlockSpec((tt, D_MODEL), lambda i, pos: (i, 0)),
        scratch_shapes=[pltpu.VMEM((2, 2 * tt, D_MODEL), F32), pltpu.SemaphoreType.DMA((2,))],
    )
    return pl.pallas_call(
        _moe_combine_kernel,
        grid_spec=grid_spec,
        out_shape=jax.ShapeDtypeStruct((n, D_MODEL), F32),
        compiler_params=_params(("arbitrary",)),
        name="moe_combine",
    )(pos, x1, g2.arr, route, ys)


def _moe_routed(h2, x1, g2, route, route_t, tile_counts, wg, wu, wd, *, tm, tt):
    n = h2.shape[0]
    n_tiles = (2 * n) // tm + N_EXPERTS
    pos, tile_expert, live, pad_lo, pad_hi = _route_tables(route_t, tile_counts, tm, n_tiles)
    xs = _moe_dispatch(h2, pos, pad_lo, pad_hi, live, tt=tt, tm=tm, n_tiles=n_tiles)
    ys = _moe_experts(xs, tile_expert, live, wg, wu, wd, tm=tm, n_tiles=n_tiles)
    return _moe_combine(x1, g2, route, pos, ys, tt=tt)


def _head_perm():
    idx = []
    for g in range(GROUP):
        for kv in range(N_KV):
            h = kv * GROUP + g
            idx.extend(range(h * HEAD_DIM, (h + 1) * HEAD_DIM))
    return np.asarray(idx, np.int32)


def _relayout_w_in(w):
    pts = np.cumsum([0, Q_W, KV_W, KV_W, GK_W, GK_W, GV_W, GATE_RANK, GV_W, D_MODEL, D_MODEL])
    qa, ka, va, qb, kb, vb, ga, rb, gta, gtb = [w[:, pts[i]:pts[i + 1]] for i in range(10)]
    qa = qa[:, _head_perm()]
    ga = jnp.pad(ga, ((0, 0), (0, GATE_PAD - GATE_RANK)))
    return jnp.concatenate([qa, ka, va, qb, kb, vb, rb, gta, gtb, ga], axis=1).astype(BF16)


def kernel(x_prompt, x_sample, cache_k, cache_v, state_gla, c_prompt, c_sample, ada_w, ada_b, norm1_g, norm2_g, w_in, q_norm_g, k_norm_g, attn_sinks, gla_wa2, gla_ba, gla_norm_g, w_branch_a, w_branch_b, w_out, ffn_w_gate, ffn_w_up, ffn_w_down, router_w, router_b, moe_w_gate, moe_w_up, moe_w_down):
    n_p = BATCH * SEQ
    xp = x_prompt.reshape(n_p, D_MODEL)
    xs = x_sample.reshape(DEC_BATCH, D_MODEL)

    c_pad = -(BATCH + DEC_BATCH) % 8
    c_all = jnp.pad(jnp.concatenate([c_sample, c_prompt], axis=0), ((0, c_pad), (0, 0)))
    mod_sample, mod_prompt = _ada(c_all, ada_w, ada_b, DEC_BATCH, BATCH)
    state_rows = state_gla.reshape(DEPTH, DEC_BATCH, GK_W, GLA_DV)

    bd = jnp.asarray(np.kron(np.eye(N_HEADS), np.ones((HEAD_DIM, HEAD_DIM))), BF16)
    ut = jnp.asarray(_gla_cum_matrix(), BF16)
    gla_upper, gla_pairs = (jnp.asarray(m) for m in _gla_level_masks())
    perm = _head_perm()
    swa_bias = jnp.asarray(_swa_sample_bias())
    swa_prompt_bias = jnp.asarray(_swa_prompt_bias())

    kp_l, vp_l, sp_l, ks_l, vs_l, ss_l = [], [], [], [], [], []
    for l in range(DEPTH):
        mod_p = [_Mod(mod_prompt, l, i) for i in range(N_MOD)]
        mod_s = [_Mod(mod_sample, l, i) for i in range(N_MOD)]

        w = _relayout_w_in(w_in[l])
        qg = (jnp.tile(q_norm_g[l], N_HEADS) * (HEAD_DIM ** -0.5)).reshape(1, Q_W)
        kg = jnp.tile(k_norm_g[l], N_KV).reshape(1, KV_W)
        wa2 = jnp.pad(gla_wa2[l], ((0, GATE_PAD - GATE_RANK), (0, 0))).astype(BF16)
        ba = gla_ba[l].reshape(1, GK_W)
        n1 = norm1_g[l].reshape(1, D_MODEL)
        n2 = norm2_g[l].reshape(1, D_MODEL)
        gn = gla_norm_g[l].reshape(1, GLA_DV)
        wpa = w_branch_a[l][perm].astype(BF16)
        wpb = w_branch_b[l].astype(BF16)
        wo = w_out[l].astype(BF16)
        sink_rows = jnp.broadcast_to(attn_sinks[l][_sample_head_of_row()][:, None], (N_HEADS, LANES))

        q, k, v, gq, gk, gv, la, rbs, sga, sgb = _mixin(
            xp, mod_p[0], mod_p[1], n1, w, bd, qg, kg, wa2, ba, tm=1024)
        a_out = _swa_prompt(attn_sinks[l], q, k, v, swa_prompt_bias)
        g_out, s_fin = _gla_prompt(gq, gk, la, gv, rbs, gn, ut, gla_upper, gla_pairs)
        is_moe = l % 2 == 1
        if is_moe:
            rw = jnp.pad(router_w[l // 2], ((0, 0), (0, ROUTER_PAD - N_EXPERTS))).astype(BF16)
            rbias = jnp.pad(router_b[l // 2], (0, ROUTER_PAD - N_EXPERTS)).reshape(1, ROUTER_PAD)
        merged = _merge(xp, a_out, g_out, sga, sgb, wpa, wpb, wo, mod_p[2], mod_p[3], mod_p[4], n2,
                        tm=512, h2_dtype=F32 if is_moe else BF16,
                        router=(rw, rbias) if is_moe else None)
        x1, h2 = merged[0], merged[1]
        for cache, full in ((kp_l, k), (vp_l, v)):
            tail = full.reshape(BATCH, SEQ, KV_W)[:, SEQ - WINDOW:]
            cache.append(tail.reshape(BATCH, WINDOW, N_KV, HEAD_DIM))
        sp_l.append(s_fin.reshape(BATCH, GLA_HEADS, GLA_DK, GLA_DV))

        qs, ksn, vsn, gqs, gks, gvs, las, rbss, sgas, sgbs = _mixin(
            xs, mod_s[0], mod_s[1], n1, w, bd, qg, kg, wa2, ba, tm=DEC_BATCH)
        a_s, nk, nv = _swa_sample(
            qs.astype(F32).reshape(DEC_BATCH, N_HEADS, HEAD_DIM),
            ksn.reshape(DEC_BATCH, N_KV, HEAD_DIM), vsn.reshape(DEC_BATCH, N_KV, HEAD_DIM),
            cache_k[l].reshape(1, DEC_BATCH, CACHE_ROWS, HEAD_DIM),
            cache_v[l].reshape(1, DEC_BATCH, CACHE_ROWS, HEAD_DIM), swa_bias, sink_rows, layer=0)
        g_s, s_new = _gla_sample(gqs, gks, las, gvs, rbss, gn, state_rows, layer=l)
        x1s, h2s = _merge(xs, a_s.reshape(DEC_BATCH, Q_W), g_s, sgas, sgbs, wpa, wpb, wo,
                          mod_s[2], mod_s[3], mod_s[4], n2, tm=DEC_BATCH)
        ks_l.append(nk.reshape(DEC_BATCH, WINDOW, N_KV, HEAD_DIM))
        vs_l.append(nv.reshape(DEC_BATCH, WINDOW, N_KV, HEAD_DIM))
        ss_l.append(s_new.reshape(DEC_BATCH, GLA_HEADS, GLA_DK, GLA_DV))

        i = l // 2
        if not is_moe:
            wg, wu, wd = ffn_w_gate[i].astype(BF16), ffn_w_up[i].astype(BF16), ffn_w_down[i].astype(BF16)
            xp = _ffn(h2, x1, mod_p[5], wg, wu, wd, tm=512, tf=D_FF // 2)
            xs = _ffn(h2s, x1s, mod_s[5], wg, wu, wd, tm=DEC_BATCH, tf=D_FF // 2)
        else:
            wg, wu, wd = moe_w_gate[i].astype(BF16), moe_w_up[i].astype(BF16), moe_w_down[i].astype(BF16)
            xp = _moe_routed(h2, x1, mod_p[5], *merged[2:5], wg, wu, wd, tm=512, tt=512)
            xs = _moe(h2s, x1s, mod_s[5], rw, rbias, wg, wu, wd, tm=DEC_BATCH)

    return (xp.reshape(BATCH, SEQ, D_MODEL), xs.reshape(DEC_BATCH, 1, D_MODEL),
            jnp.stack(kp_l), jnp.stack(vp_l), jnp.stack(sp_l),
            jnp.stack(ks_l), jnp.stack(vs_l), jnp.stack(ss_l))
```

```python
import functools
from typing import NamedTuple

import jax
import jax.numpy as jnp
import numpy as np
from jax import lax
from jax.experimental import pallas as pl
from jax.experimental.pallas import tpu as pltpu

D_MODEL = 1024
BATCH = 4
SEQ = 4096
DEPTH = 2
DEC_BATCH = 128
N_HEADS = 8
N_KV = 2
HEAD_DIM = 64
GROUP = N_HEADS // N_KV
WINDOW = 128
GLA_HEADS = 4
GLA_DK = 64
GLA_DV = 128
GATE_RANK = 16
GATE_TAU = 16.0
D_FF = 2816
N_EXPERTS = 8
D_FF_EXPERT = 1408
EPS = 1e-6

Q_W = N_HEADS * HEAD_DIM
KV_W = N_KV * HEAD_DIM
GK_W = GLA_HEADS * GLA_DK
GV_W = GLA_HEADS * GLA_DV

LANES = 128
GATE_PAD = LANES
ROUTER_PAD = LANES
VMEM_LIMIT = 56 * 1024 * 1024

ROW_TILE_RESIDENT = 1024
ROW_TILE_SWIGLU = 512
MOE_TOKEN_TILE = 512

F32 = jnp.float32
BF16 = jnp.bfloat16

_C_Q = 0
_C_K = _C_Q + Q_W
_C_V = _C_K + KV_W
_C_GQ = _C_V + KV_W
_C_GK = _C_GQ + GK_W
_C_GV = _C_GK + GK_W
_C_RB = _C_GV + GV_W
_C_GA = _C_RB + GV_W
_C_GB = _C_GA + D_MODEL
_C_LR = _C_GB + D_MODEL
PROJ_PAD = _C_LR + GATE_PAD

GLA_CHUNK = 128
GLA_LEVELS = 7
GLA_MXU_LEVELS = 3
GLA_CHUNKS_PER_STEP = 4


def _params(sem, vmem=VMEM_LIMIT):
    return pltpu.CompilerParams(dimension_semantics=sem, vmem_limit_bytes=vmem)


def _dot(a, b):
    return jnp.dot(a, b, preferred_element_type=F32)


def _dot_nt(a, b):
    return lax.dot_general(a, b, (((1,), (1,)), ((), ())), preferred_element_type=F32)


def _dot_tn(a, b):
    return lax.dot_general(a, b, (((0,), (0,)), ((), ())), preferred_element_type=F32)


def _sigmoid(x):
    return 0.5 * jnp.tanh(0.5 * x) + 0.5


def _silu(x):
    return x * _sigmoid(x)


N_MOD = 6


class _Mod(NamedTuple):
    arr: jax.Array
    layer: int
    idx: int


def _ada_kernel(c_ref, w_ref, b_ref, os_ref, op_ref):
    c = c_ref[...]
    mod = _dot(_silu(c).astype(BF16), w_ref[...].astype(BF16)) + b_ref[...]
    n_s = os_ref.shape[0]
    os_ref[...] = mod[0:n_s]
    for b in range(op_ref.shape[0]):
        op_ref[b] = mod[n_s + b:n_s + b + 1]


def _ada(c_all, ada_w, ada_b, n_sample, n_prompt):
    rows = c_all.shape[0]
    return pl.pallas_call(
        _ada_kernel,
        grid=(DEPTH, N_MOD),
        in_specs=[
            pl.BlockSpec((rows, D_MODEL), lambda l, j: (0, 0)),
            pl.BlockSpec((None, D_MODEL, D_MODEL), lambda l, j: (l, 0, j)),
            pl.BlockSpec((None, 1, D_MODEL), lambda l, j: (l, 0, j)),
        ],
        out_specs=[
            pl.BlockSpec((None, None, n_sample, D_MODEL), lambda l, j: (l, j, 0, 0)),
            pl.BlockSpec((None, None, n_prompt, 1, D_MODEL), lambda l, j: (l, j, 0, 0, 0)),
        ],
        out_shape=[
            jax.ShapeDtypeStruct((DEPTH, N_MOD, n_sample, D_MODEL), F32),
            jax.ShapeDtypeStruct((DEPTH, N_MOD, n_prompt, 1, D_MODEL), F32),
        ],
        compiler_params=_params(("parallel", "parallel")),
        name="ada_mod",
    )(c_all, ada_w, ada_b.reshape(DEPTH, 1, N_MOD * D_MODEL))


def _mixin_kernel(x_ref, sh_ref, sc_ref, n1_ref, w_ref, bd_ref, qg_ref, kg_ref, wa2_ref, ba_ref,
                  q_ref, k_ref, v_ref, gq_ref, gk_ref, gv_ref, la_ref, rb_ref, sga_ref, sgb_ref):
    x = x_ref[...]
    ms = jnp.mean(x * x, axis=-1, keepdims=True)
    h = x * lax.rsqrt(ms + EPS) * n1_ref[...]
    h = h * (1.0 + sc_ref[...]) + sh_ref[...]
    hb = h.astype(BF16)

    def proj(a, b):
        return _dot(hb, w_ref[:, a:b])

    q = proj(_C_Q, _C_K)
    ssq = _dot((q * q).astype(BF16), bd_ref[...])
    q_ref[...] = (q * lax.rsqrt(ssq * (1.0 / HEAD_DIM) + EPS) * qg_ref[...]).astype(BF16)
    k = proj(_C_K, _C_V)
    ssk = _dot((k * k).astype(BF16), bd_ref[0:KV_W, 0:KV_W])
    k_ref[...] = k * lax.rsqrt(ssk * (1.0 / HEAD_DIM) + EPS) * kg_ref[...]
    v_ref[...] = proj(_C_V, _C_GQ)
    gq_ref[...] = proj(_C_GQ, _C_GK) * (GLA_DK ** -0.5)
    gk_ref[...] = proj(_C_GK, _C_GV)
    gv_ref[...] = proj(_C_GV, _C_RB)
    rb_ref[...] = _silu(proj(_C_RB, _C_GA)).astype(BF16)
    sga_ref[...] = _sigmoid(proj(_C_GA, _C_GB)).astype(BF16)
    sgb_ref[...] = _sigmoid(proj(_C_GB, _C_LR)).astype(BF16)
    ga = proj(_C_LR, PROJ_PAD)
    xg = _dot(ga.astype(BF16), wa2_ref[...]) + ba_ref[...]
    la_ref[...] = (jnp.minimum(xg, 0.0) - jnp.log1p(jnp.exp(-jnp.abs(xg)))) * (1.0 / GATE_TAU)


def _mixin(x, sh, sc, n1, w, bd, qg, kg, wa2, ba, *, tm):
    n = x.shape[0]
    nt = n // tm

    def row(i):
        return (i, 0)

    def const(shape):
        return pl.BlockSpec(shape, lambda i: (0,) * len(shape), pipeline_mode=pl.Buffered(1))

    def out(width, dtype):
        return pl.BlockSpec((tm, width), row), jax.ShapeDtypeStruct((n, width), dtype)

    outs = [out(Q_W, BF16), out(KV_W, F32), out(KV_W, F32), out(GK_W, F32), out(GK_W, F32),
            out(GV_W, F32), out(GK_W, F32), out(GV_W, BF16), out(D_MODEL, BF16), out(D_MODEL, BF16)]
    return pl.pallas_call(
        _mixin_kernel,
        grid=(nt,),
        in_specs=[
            pl.BlockSpec((tm, D_MODEL), row), _mod_spec(tm, n, sh), _mod_spec(tm, n, sc), const((1, D_MODEL)),
            const((D_MODEL, PROJ_PAD)), const((Q_W, Q_W)), const((1, Q_W)), const((1, KV_W)),
            const((GATE_PAD, GK_W)), const((1, GK_W)),
        ],
        out_specs=[o[0] for o in outs],
        out_shape=[o[1] for o in outs],
        compiler_params=_params(("parallel",)),
        name="mixer_in",
    )(x, sh.arr, sc.arr, n1, w, bd, qg, kg, wa2, ba)


SWA_BLOCKS = 8


def _head_slope(h):
    return float(2.0 ** (-8.0 * (h + 1) / N_HEADS))


def _swa_prompt_bias():
    blk = WINDOW
    dist = np.arange(blk)[:, None] + blk - np.arange(2 * blk)[None, :]
    slopes = np.asarray([_head_slope(h) for h in range(N_HEADS)])[:, None, None]
    return np.where((dist >= 0) & (dist <= WINDOW), -slopes * dist, -np.inf).astype(np.float32)


def _swa_prompt_kernel(sink_ref, q_ref, kp_ref, kc_ref, vp_ref, vc_ref, bias_ref, o_ref):
    n = pl.program_id(1)
    blk = WINDOW
    kall = jnp.concatenate([kp_ref[...], kc_ref[...]], axis=0).astype(BF16)
    vall = jnp.concatenate([vp_ref[...], vc_ref[...]], axis=0).astype(BF16)
    col = lax.broadcasted_iota(jnp.int32, (blk, 2 * blk), 1)
    first_key = jnp.where(n > 0, 0, blk)
    klane = lax.broadcasted_iota(jnp.int32, kall.shape, 1)
    kall_kv = [jnp.where((klane // HEAD_DIM) == kv, kall, jnp.zeros_like(kall)) for kv in range(N_KV)]
    olane = lax.broadcasted_iota(jnp.int32, (blk, KV_W), 1)
    q = q_ref[...]
    for j in range(SWA_BLOCKS):
        vv = vall[j * blk:(j + 2) * blk]
        outs = []
        for g in range(GROUP):
            qp = q[j * blk:(j + 1) * blk, g * KV_W:(g + 1) * KV_W]
            pair = []
            for kv in range(N_KV):
                h = kv * GROUP + g
                s = _dot_nt(qp, kall_kv[kv][j * blk:(j + 2) * blk]) + bias_ref[h]
                if j == 0:
                    s = jnp.where(col >= first_key, s, -jnp.inf)
                sink = sink_ref[h]
                m = jnp.maximum(jnp.max(s, axis=-1, keepdims=True), sink)
                p = jnp.exp(s - m)
                den = jnp.sum(p, axis=-1, keepdims=True) + jnp.exp(sink - m)
                pair.append(_dot(p.astype(BF16), vv) * (1.0 / den))
            outs.append(jnp.where(olane < HEAD_DIM, pair[0], pair[1]))
        o_ref[j * blk:(j + 1) * blk, :] = jnp.concatenate(outs, axis=-1).astype(BF16)


def _swa_prompt(sinks, q, k, v, bias):
    nb = SEQ // WINDOW
    steps = nb // SWA_BLOCKS
    tq = SWA_BLOCKS * WINDOW

    def cur(b, n):
        return (b * steps + n, 0)

    def prev(b, n):
        return (b * nb + jnp.maximum(n * SWA_BLOCKS - 1, 0), 0)

    return pl.pallas_call(
        _swa_prompt_kernel,
        grid=(BATCH, steps),
        in_specs=[
            pl.BlockSpec(memory_space=pltpu.SMEM),
            pl.BlockSpec((tq, Q_W), cur),
            pl.BlockSpec((WINDOW, KV_W), prev), pl.BlockSpec((tq, KV_W), cur),
            pl.BlockSpec((WINDOW, KV_W), prev), pl.BlockSpec((tq, KV_W), cur),
            pl.BlockSpec(bias.shape, lambda b, n: (0, 0, 0), pipeline_mode=pl.Buffered(1)),
        ],
        out_specs=pl.BlockSpec((tq, Q_W), cur),
        out_shape=jax.ShapeDtypeStruct((BATCH * SEQ, Q_W), BF16),
        compiler_params=_params(("parallel", "parallel")),
        name="swa_prompt",
    )(sinks, q, k, k, v, v, bias)


SAMPLE_TB = 8
CACHE_ROWS = WINDOW * N_KV


def _sample_head_of_row():
    j = np.arange(N_HEADS)
    return (j % N_KV) * GROUP + j // N_KV


def _swa_sample_bias():
    j = np.arange(N_HEADS)[:, None]
    c = np.arange(CACHE_ROWS)[None, :]
    slope = 2.0 ** (-8.0 * (_sample_head_of_row()[:, None] + 1) / N_HEADS)
    bias = -slope * (WINDOW - c // N_KV)
    return np.where(c % N_KV == j % N_KV, bias, -np.inf).astype(np.float32)


def _swa_sample_kernel(q_ref, kn_ref, vn_ref, ck_ref, cv_ref, bias_ref, sk_ref, o_ref, ok_ref, ov_ref):
    rows = CACHE_ROWS
    sink = sk_ref[...][:, 0:1]
    q = q_ref[...]
    kn = kn_ref[...]
    vn = vn_ref[...]
    kc = ck_ref[...]
    vc = cv_ref[...]
    kn8 = jnp.concatenate([kn] * GROUP, axis=1)
    vn8 = jnp.concatenate([vn] * GROUP, axis=1)
    s = lax.dot_general(q.astype(BF16), kc.astype(BF16), (((2,), (2,)), ((0,), (0,))),
                        preferred_element_type=F32) + bias_ref[...]
    s_new = jnp.sum(q * kn8, axis=-1, keepdims=True)
    m = jnp.maximum(jnp.maximum(jnp.max(s, axis=-1, keepdims=True), s_new), sink)
    p = jnp.exp(s - m)
    p_new = jnp.exp(s_new - m)
    den = jnp.sum(p, axis=-1, keepdims=True) + p_new + jnp.exp(sink - m)
    o = lax.dot_general(p.astype(BF16), vc.astype(BF16), (((2,), (1,)), ((0,), (0,))),
                        preferred_element_type=F32) + p_new * vn8
    o_ref[...] = o * (1.0 / den)
    ok_ref[:, pl.ds(0, rows - N_KV), :] = kc[:, N_KV:, :]
    ok_ref[:, pl.ds(rows - N_KV, N_KV), :] = kn
    ov_ref[:, pl.ds(0, rows - N_KV), :] = vc[:, N_KV:, :]
    ov_ref[:, pl.ds(rows - N_KV, N_KV), :] = vn


def _swa_sample(q, kn, vn, ck, cv, bias, sinks, *, layer):
    tb = 2 * SAMPLE_TB
    nb = DEC_BATCH // tb

    def const(shape):
        return pl.BlockSpec(shape, lambda i: (0,) * len(shape))

    def per_seq(*dims):
        return pl.BlockSpec((tb,) + dims, lambda i: (i, 0, 0))

    cache_spec = pl.BlockSpec((None, tb, CACHE_ROWS, HEAD_DIM), lambda i: (layer, i, 0, 0))
    return pl.pallas_call(
        _swa_sample_kernel,
        grid=(nb,),
        in_specs=[per_seq(N_HEADS, HEAD_DIM), per_seq(N_KV, HEAD_DIM), per_seq(N_KV, HEAD_DIM),
                  cache_spec, cache_spec,
                  const((N_HEADS, CACHE_ROWS)), const((N_HEADS, LANES))],
        out_specs=[per_seq(N_HEADS, HEAD_DIM), per_seq(CACHE_ROWS, HEAD_DIM), per_seq(CACHE_ROWS, HEAD_DIM)],
        out_shape=[
            jax.ShapeDtypeStruct((DEC_BATCH, N_HEADS, HEAD_DIM), F32),
            jax.ShapeDtypeStruct((DEC_BATCH, CACHE_ROWS, HEAD_DIM), F32),
            jax.ShapeDtypeStruct((DEC_BATCH, CACHE_ROWS, HEAD_DIM), F32),
        ],
        compiler_params=_params(("parallel",)),
        name="swa_sample",
    )(q, kn, vn, ck, cv, bias, sinks)


def _gla_cum_matrix():
    c = GLA_CHUNK
    tri = np.tril(np.ones((c, c), np.float32))
    i = np.arange(c)
    blocks = []
    for lvl in range(GLA_MXU_LEVELS):
        half = 1 << lvl
        mid = (i // (2 * half)) * (2 * half) + half - 1
        blocks.append(tri - tri[mid])
    blocks.append(tri)
    return np.concatenate(blocks, axis=0)


def _split3(x):
    hi = x.astype(BF16)
    r1 = x - hi.astype(F32)
    mid = r1.astype(BF16)
    lo = (r1 - mid.astype(F32)).astype(BF16)
    return hi, mid, lo


def _gla_level_masks():
    cl = GLA_CHUNK
    r = np.arange(cl)
    upper = np.stack([np.broadcast_to(((r >> lvl) & 1)[:, None], (cl, GK_W)) for lvl in range(GLA_LEVELS)])
    ri = np.tile(r, GLA_HEADS)[:, None]
    pairs = [(ri >> (lvl + 1)) == (r[None, :] >> (lvl + 1)) for lvl in range(GLA_LEVELS)]
    pairs.append(ri == r[None, :])
    return upper.astype(np.float32), np.stack(pairs).astype(np.float32)


def _gla_prompt_kernel(q_ref, k_ref, la_ref, v_ref, rb_ref, gn_ref, ut_ref, up_ref, pm_ref,
                       o_ref, s_ref, st_ref):
    c = pl.program_id(1)
    cl = GLA_CHUNK

    @pl.when(c == 0)
    def _():
        st_ref[0] = jnp.zeros(st_ref.shape[1:], F32)

    ut = ut_ref[...]
    lane = lax.broadcasted_iota(jnp.int32, (cl, GK_W), 1)
    head_of_lane = lane // GLA_DK
    ones = jnp.ones((cl, LANES), BF16)
    state = [st_ref[c % 2, h] for h in range(GLA_HEADS)]
    for cc in range(GLA_CHUNKS_PER_STEP):
        rows = slice(cc * cl, (cc + 1) * cl)
        hi, mid, lo = _split3(la_ref[rows, :])
        tall = _dot(ut, hi) + _dot(ut, mid) + _dot(ut, lo)
        q = q_ref[rows, :]
        k = k_ref[rows, :]
        attn_all = jnp.zeros((GLA_HEADS * cl, cl), F32)
        cum = tall[GLA_MXU_LEVELS * cl:(GLA_MXU_LEVELS + 1) * cl]
        for lvl in range(GLA_LEVELS + 1):
            if lvl < GLA_LEVELS:
                if lvl < GLA_MXU_LEVELS:
                    t_lvl = tall[lvl * cl:(lvl + 1) * cl]
                else:
                    half = 1 << lvl
                    mids = [jnp.broadcast_to(cum[b0 + half - 1:b0 + half, :], (2 * half, GK_W))
                            for b0 in range(0, cl, 2 * half)]
                    t_lvl = cum - (mids[0] if len(mids) == 1 else jnp.concatenate(mids, axis=0))
                e = jnp.exp(-jnp.abs(t_lvl))
                e_up = e * up_ref[lvl]
                qt = (q * e_up).astype(BF16)
                kt = (k * (e - e_up)).astype(BF16)
            else:
                qt = q.astype(BF16)
                kt = k.astype(BF16)
            q_heads = jnp.concatenate(
                [jnp.where(head_of_lane == h, qt, jnp.zeros_like(qt)) for h in range(GLA_HEADS)], axis=0)
            attn_all = attn_all + _dot_nt(q_heads, kt) * pm_ref[lvl]

        last = cum[cl - 1:cl, :]
        qe = (q * jnp.exp(cum)).astype(BF16)
        kd = (k * jnp.exp(last - cum)).astype(BF16)
        dec = jnp.exp(_dot_tn(hi, ones) + _dot_tn(mid, ones) + _dot_tn(lo, ones))
        v = v_ref[rows, :]
        rb = rb_ref[rows, :]
        outs = []
        for h in range(GLA_HEADS):
            vh = v[:, h * GLA_DV:(h + 1) * GLA_DV].astype(BF16)
            o = _dot(attn_all[h * cl:(h + 1) * cl].astype(BF16), vh) + _dot(qe, state[h].astype(BF16))
            kdh = jnp.where(head_of_lane == h, kd, jnp.zeros_like(kd))
            state[h] = state[h] * dec + _dot_tn(kdh, vh)
            ms = jnp.mean(o * o, axis=-1, keepdims=True)
            g = o * lax.rsqrt(ms + EPS) * gn_ref[...]
            outs.append(g * rb[:, h * GLA_DV:(h + 1) * GLA_DV].astype(F32))
        o_ref[rows, :] = jnp.concatenate(outs, axis=-1).astype(BF16)

    for h in range(GLA_HEADS):
        st_ref[(c + 1) % 2, h] = state[h]

    @pl.when(c == pl.num_programs(1) - 1)
    def _():
        fin = (SEQ // (GLA_CHUNK * GLA_CHUNKS_PER_STEP)) % 2
        s_ref[...] = st_ref[fin, 0] + st_ref[fin, 1] + st_ref[fin, 2] + st_ref[fin, 3]


def _gla_prompt(gq, gk, la, gv, rbs, gn, ut, upper, pairs):
    cl = GLA_CHUNK * GLA_CHUNKS_PER_STEP
    nc = SEQ // cl

    def row(b, c):
        return (b * nc + c, 0)

    def const(shape):
        return pl.BlockSpec(shape, lambda b, c: (0,) * len(shape), pipeline_mode=pl.Buffered(1))

    return pl.pallas_call(
        _gla_prompt_kernel,
        grid=(BATCH, nc),
        in_specs=[
            pl.BlockSpec((cl, GK_W), row), pl.BlockSpec((cl, GK_W), row), pl.BlockSpec((cl, GK_W), row),
            pl.BlockSpec((cl, GV_W), row), pl.BlockSpec((cl, GV_W), row),
            const((1, GLA_DV)), const(ut.shape), const(upper.shape), const(pairs.shape),
        ],
        out_specs=[
            pl.BlockSpec((cl, GV_W), row),
            pl.BlockSpec((None, GK_W, GLA_DV), lambda b, c: (b, 0, 0)),
        ],
        out_shape=[
            jax.ShapeDtypeStruct((BATCH * SEQ, GV_W), BF16),
            jax.ShapeDtypeStruct((BATCH, GK_W, GLA_DV), F32),
        ],
        scratch_shapes=[pltpu.VMEM((2, GLA_HEADS, GK_W, GLA_DV), F32)],
        compiler_params=_params(("parallel", "arbitrary")),
        name="gla_prompt",
    )(gq, gk, la, gv, rbs, gn, ut, upper, pairs)


def _gla_sample_kernel(q_ref, k_ref, la_ref, v_ref, rb_ref, gn_ref, s_ref, o_ref, so_ref):
    tb = SAMPLE_TB
    dec = jnp.exp(la_ref[...])
    pieces = []
    for x in (dec, k_ref[...], q_ref[...]):
        hi, mid, lo = _split3(x)
        stacked = jnp.concatenate(
            [hi.astype(F32), mid.astype(F32), lo.astype(F32), jnp.zeros_like(x)], axis=0)
        pieces.append(stacked.astype(BF16))
    prow = lax.broadcasted_iota(jnp.int32, (4 * tb, LANES), 0)
    rb = rb_ref[...].astype(F32)
    v = v_ref[...]
    for bi in range(tb):
        sel = jnp.where((prow % tb) == bi, 1.0, 0.0).astype(BF16)
        a_col, k_col, q_col = [_dot_tn(p, sel) for p in pieces]
        for h in range(GLA_HEADS):
            rs = slice(h * GLA_DK, (h + 1) * GLA_DK)
            vs = slice(h * GLA_DV, (h + 1) * GLA_DV)
            s_new = a_col[rs] * s_ref[bi, rs, :] + k_col[rs] * v[bi:bi + 1, vs]
            so_ref[bi, rs, :] = s_new
            o = jnp.sum(q_col[rs] * s_new, axis=0, keepdims=True)
            ms = jnp.mean(o * o, axis=-1, keepdims=True)
            g = o * lax.rsqrt(ms + EPS) * gn_ref[...]
            o_ref[bi:bi + 1, vs] = g * rb[bi:bi + 1, vs]


def _gla_sample(gq, gk, la, gv, rbs, gn, state, *, layer):
    tb = SAMPLE_TB
    nb = DEC_BATCH // tb

    def row(w):
        return pl.BlockSpec((tb, w), lambda i: (i, 0))

    st_spec = pl.BlockSpec((tb, GK_W, GLA_DV), lambda i: (i, 0, 0))
    st_in_spec = pl.BlockSpec((None, tb, GK_W, GLA_DV), lambda i: (layer, i, 0, 0))
    return pl.pallas_call(
        _gla_sample_kernel,
        grid=(nb,),
        in_specs=[row(GK_W), row(GK_W), row(GK_W), row(GV_W), row(GV_W),
                  pl.BlockSpec((1, GLA_DV), lambda i: (0, 0)), st_in_spec],
        out_specs=[row(GV_W), st_spec],
        out_shape=[
            jax.ShapeDtypeStruct((DEC_BATCH, GV_W), F32),
            jax.ShapeDtypeStruct((DEC_BATCH, GK_W, GLA_DV), F32),
        ],
        compiler_params=_params(("parallel",)),
        name="gla_sample",
    )(gq, gk, la, gv, rbs, gn, state)


def _merge_kernel(x_ref, a_ref, g_ref, sga_ref, sgb_ref, wpa_ref, wpb_ref, wo_ref,
                  g1_ref, sh_ref, sc_ref, n2_ref, *rest):
    with_router = len(rest) == 8
    x1_ref, h2_ref = rest[3:5] if with_router else rest
    ya = _dot(a_ref[...].astype(BF16), wpa_ref[...])
    yb = _dot(g_ref[...].astype(BF16), wpb_ref[...])
    merged = sga_ref[...].astype(F32) * ya + sgb_ref[...].astype(F32) * yb
    mix = _dot(merged.astype(BF16), wo_ref[...])
    x1 = x_ref[...] + g1_ref[...] * mix
    x1_ref[...] = x1
    ms = jnp.mean(x1 * x1, axis=-1, keepdims=True)
    h = x1 * lax.rsqrt(ms + EPS) * n2_ref[...]
    h2 = h * (1.0 + sc_ref[...]) + sh_ref[...]
    h2_ref[...] = h2.astype(h2_ref.dtype)
    if with_router:
        rw_ref, rbias_ref, tri_ref = rest[0:3]
        route_ref, route_t_ref, cnt_ref = rest[5:8]
        logits = _dot(h2.astype(BF16), rw_ref[...]) + rbias_ref[...]
        packed, counts = _route_pack(logits, tri_ref[...])
        route_ref[...] = packed
        route_t_ref[...] = packed.T[0:route_t_ref.shape[0], :]
        cnt_ref[...] = jnp.broadcast_to(counts, cnt_ref.shape)


def _mod_spec(tm, n, mod):
    if mod.arr.ndim == 4:
        return pl.BlockSpec((None, None, tm, D_MODEL), lambda i, *_: (mod.layer, mod.idx, i, 0))
    tiles_per_seq = (n // mod.arr.shape[2]) // tm
    return pl.BlockSpec((None, None, None, 1, D_MODEL),
                        lambda i, *_: (mod.layer, mod.idx, i // tiles_per_seq, 0, 0))


def _merge(x, a, g, sga, sgb, wpa, wpb, wo, g1, sh2, sc2, n2, *, tm, h2_dtype=BF16, router=None):
    n = x.shape[0]

    def row(w):
        return pl.BlockSpec((tm, w), lambda i: (i, 0))

    def const(shape):
        return pl.BlockSpec(shape, lambda i: (0,) * len(shape))

    in_specs = [row(D_MODEL), row(a.shape[1]), row(GV_W), row(D_MODEL), row(D_MODEL),
                const(wpa.shape), const(wpb.shape), const(wo.shape),
                _mod_spec(tm, n, g1), _mod_spec(tm, n, sh2), _mod_spec(tm, n, sc2), const((1, D_MODEL))]
    out_specs = [row(D_MODEL), row(D_MODEL)]
    out_shape = [jax.ShapeDtypeStruct((n, D_MODEL), F32), jax.ShapeDtypeStruct((n, D_MODEL), h2_dtype)]
    args = [x, a, g, sga, sgb, wpa, wpb, wo, g1.arr, sh2.arr, sc2.arr, n2]
    if router is not None:
        tri = jnp.asarray(np.tril(np.ones((tm, tm), np.float32), -1), BF16)
        in_specs += [const(router[0].shape), const(router[1].shape), const((tm, tm))]
        out_specs += [row(ROUTER_PAD), pl.BlockSpec((8, tm), lambda i: (0, i)),
                      pl.BlockSpec((None, 8, ROUTER_PAD), lambda i: (i, 0, 0))]
        out_shape += [jax.ShapeDtypeStruct((n, ROUTER_PAD), F32), jax.ShapeDtypeStruct((8, n), F32),
                      jax.ShapeDtypeStruct((n // tm, 8, ROUTER_PAD), F32)]
        args += [router[0], router[1], tri]
    return pl.pallas_call(
        _merge_kernel,
        grid=(n // tm,),
        in_specs=in_specs,
        out_specs=out_specs,
        out_shape=out_shape,
        compiler_params=_params(("parallel",)),
        name="merge_out",
    )(*args)


def _swiglu(hb, wg_ref, wu_ref, wd_ref):
    act = (_silu(_dot(hb, wg_ref[...])) * _dot(hb, wu_ref[...])).astype(BF16)
    return _dot(act, wd_ref[...])


def _ffn_kernel(h_ref, x_ref, g2_ref, wg_ref, wu_ref, wd_ref, o_ref, acc_ref):
    f = pl.program_id(1)
    y = _swiglu(h_ref[...], wg_ref, wu_ref, wd_ref)

    @pl.when(f == 0)
    def _():
        acc_ref[...] = y

    @pl.when(f > 0)
    def _():
        acc_ref[...] += y

    @pl.when(f == pl.num_programs(1) - 1)
    def _():
        o_ref[...] = x_ref[...] + g2_ref[...] * acc_ref[...]


def _ffn(h2, x1, g2, wg, wu, wd, *, tm, tf):
    n = h2.shape[0]
    mod_spec = _mod_spec(tm, n, g2)
    return pl.pallas_call(
        _ffn_kernel,
        grid=(n // tm, D_FF // tf),
        in_specs=[
            pl.BlockSpec((tm, D_MODEL), lambda i, f: (i, 0)),
            pl.BlockSpec((tm, D_MODEL), lambda i, f: (i, 0)),
            mod_spec,
            pl.BlockSpec((D_MODEL, tf), lambda i, f: (0, f)),
            pl.BlockSpec((D_MODEL, tf), lambda i, f: (0, f)),
            pl.BlockSpec((tf, D_MODEL), lambda i, f: (f, 0)),
        ],
        out_specs=pl.BlockSpec((tm, D_MODEL), lambda i, f: (i, 0)),
        out_shape=jax.ShapeDtypeStruct((n, D_MODEL), F32),
        scratch_shapes=[pltpu.VMEM((tm, D_MODEL), F32)],
        compiler_params=_params(("parallel", "arbitrary")),
        name="ffn_dense",
    )(h2, x1, g2.arr, wg, wu, wd)


def _moe_kernel(h_ref, x_ref, g2_ref, rw_ref, rbias_ref, wg_ref, wu_ref, wd_ref, o_ref, acc_ref, gate_ref):
    e = pl.program_id(1)
    hb = h_ref[...]
    tm = hb.shape[0]
    lane = lax.broadcasted_iota(jnp.int32, (tm, ROUTER_PAD), 1).astype(F32)

    @pl.when(e == 0)
    def _():
        logits = _dot(hb, rw_ref[...]) + rbias_ref[...]
        i1, i2, p1, p2 = _top2(logits, lane)
        gate_ref[...] = jnp.where(lane == i1, p1, 0.0) + jnp.where(lane == i2, p2, 0.0)
        acc_ref[...] = jnp.zeros_like(acc_ref)

    ge = jnp.sum(jnp.where(lane == e.astype(F32), gate_ref[...], 0.0), axis=-1, keepdims=True)
    act = (_silu(_dot(hb, wg_ref[...])) * _dot(hb, wu_ref[...])).astype(BF16)
    acc_ref[...] += ge * _dot(act, wd_ref[...])

    @pl.when(e == pl.num_programs(1) - 1)
    def _():
        o_ref[...] = x_ref[...] + g2_ref[...] * acc_ref[...]


def _moe(h2, x1, g2, rw, rbias, wg, wu, wd, *, tm):
    n = h2.shape[0]
    mod_spec = _mod_spec(tm, n, g2)
    fe = D_FF_EXPERT
    return pl.pallas_call(
        _moe_kernel,
        grid=(n // tm, N_EXPERTS),
        in_specs=[
            pl.BlockSpec((tm, D_MODEL), lambda i, e: (i, 0)),
            pl.BlockSpec((tm, D_MODEL), lambda i, e: (i, 0)),
            mod_spec,
            pl.BlockSpec((D_MODEL, ROUTER_PAD), lambda i, e: (0, 0)),
            pl.BlockSpec((1, ROUTER_PAD), lambda i, e: (0, 0)),
            pl.BlockSpec((None, D_MODEL, fe), lambda i, e: (e, 0, 0)),
            pl.BlockSpec((None, D_MODEL, fe), lambda i, e: (e, 0, 0)),
            pl.BlockSpec((None, fe, D_MODEL), lambda i, e: (e, 0, 0)),
        ],
        out_specs=pl.BlockSpec((tm, D_MODEL), lambda i, e: (i, 0)),
        out_shape=jax.ShapeDtypeStruct((n, D_MODEL), F32),
        scratch_shapes=[pltpu.VMEM((tm, D_MODEL), F32), pltpu.VMEM((tm, ROUTER_PAD), F32)],
        compiler_params=_params(("parallel", "arbitrary")),
        name="moe",
    )(h2, x1, g2.arr, rw, rbias, wg, wu, wd)


def _top2(logits, lane):
    lg = jnp.where(lane < N_EXPERTS, logits, -jnp.inf)
    m1 = jnp.max(lg, axis=-1, keepdims=True)
    i1 = jnp.min(jnp.where(lg == m1, lane, float(ROUTER_PAD)), axis=-1, keepdims=True)
    lg2 = jnp.where(lane == i1, -jnp.inf, lg)
    m2 = jnp.max(lg2, axis=-1, keepdims=True)
    i2 = jnp.min(jnp.where(lg2 == m2, lane, float(ROUTER_PAD)), axis=-1, keepdims=True)
    e2 = jnp.exp(m2 - m1)
    p1 = 1.0 / (1.0 + e2)
    return i1, i2, p1, e2 * p1


def _route_pack(logits, tri):
    lane = lax.broadcasted_iota(jnp.int32, logits.shape, 1).astype(F32)
    i1, i2, p1, p2 = _top2(logits, lane)
    oh1 = jnp.where(lane == i1, 1.0, 0.0)
    oh2 = jnp.where(lane == i2, 1.0, 0.0)
    cnt1 = jnp.sum(oh1, axis=0, keepdims=True)
    cnt2 = jnp.sum(oh2, axis=0, keepdims=True)
    rank1 = jnp.sum(_dot(tri, oh1.astype(BF16)) * oh1, axis=-1, keepdims=True)
    rank2 = jnp.sum((_dot(tri, oh2.astype(BF16)) + cnt1) * oh2, axis=-1, keepdims=True)
    packed = jnp.zeros_like(logits)
    for k, val in enumerate((p1, p2, i1, i2, rank1, rank2)):
        packed = jnp.where(lane == float(k), val, packed)
    return packed, cnt1 + cnt2


def _route_tables(route_t, tile_counts, tm, n_tiles):
    n = route_t.shape[1]
    tile_cnt = tile_counts[:, 0, :N_EXPERTS].astype(jnp.int32)
    cnt = jnp.sum(tile_cnt, axis=0)
    gsz = ((cnt + tm - 1) // tm) * tm
    gend = jnp.cumsum(gsz)
    seg_start = (gend - gsz)[None, :] + jnp.cumsum(tile_cnt, axis=0) - tile_cnt
    seg_of_token = jnp.repeat(seg_start.T, n // tile_cnt.shape[0], axis=1)
    experts = jnp.arange(N_EXPERTS, dtype=jnp.int32)[:, None]
    pos = []
    for k in range(2):
        e_k = route_t[2 + k].astype(jnp.int32)
        rank_k = route_t[4 + k].astype(jnp.int32)
        pos.append(jnp.sum(jnp.where(e_k[None, :] == experts, seg_of_token, 0), axis=0) + rank_k)
    pos = jnp.concatenate(pos)
    tile_start = jnp.arange(n_tiles, dtype=jnp.int32) * tm
    tile_expert = jnp.sum((tile_start[:, None] >= gend[None, :]).astype(jnp.int32), axis=1)
    tile_expert = jnp.minimum(tile_expert, N_EXPERTS - 1)
    live = (gend[-1] // tm).reshape(1)
    return pos, tile_expert, live, gend - gsz + cnt, gend


def _row_copy_wait(src_hbm, dst, sem, rows):
    pltpu.make_async_copy(src_hbm.at[pl.ds(0, rows)], dst, sem).wait()


def _moe_dispatch_kernel(pos_ref, pad_lo_ref, pad_hi_ref, live_ref, h_ref, xs_hbm, zbuf, sem, zsem):
    i = pl.program_id(0)
    tt = h_ref.shape[0]
    n = tt * pl.num_programs(0)
    tm = zbuf.shape[0]

    base = i * tt
    for r in range(tt):
        row = h_ref.at[pl.ds(r, 1)]
        pltpu.make_async_copy(row, xs_hbm.at[pl.ds(pos_ref[base + r], 1)], sem).start(priority=0)
        pltpu.make_async_copy(row, xs_hbm.at[pl.ds(pos_ref[n + base + r], 1)], sem).start(priority=1)
    for _ in range(2):
        pltpu.make_async_copy(h_ref, xs_hbm.at[pl.ds(0, tt)], sem).wait()

    @pl.when(i == pl.num_programs(0) - 1)
    def _():
        zbuf[...] = jnp.zeros(zbuf.shape, F32)

        def fill_row(p):
            return pltpu.make_async_copy(zbuf.at[pl.ds(0, 1)], xs_hbm.at[pl.ds(p, 1)], zsem)

        def fill_tile(t):
            return pltpu.make_async_copy(zbuf, xs_hbm.at[pl.ds(pl.multiple_of(t * tm, tm), tm)], zsem)

        for e in range(N_EXPERTS):
            lo, hi = pad_lo_ref[e], pad_hi_ref[e]
            lax.fori_loop(lo, hi, lambda p, c: (fill_row(p).start(), c)[1], 0)
            lax.fori_loop(lo, hi, lambda p, c: (fill_row(p).wait(), c)[1], 0)
        lo, hi = live_ref[0], xs_hbm.shape[0] // tm
        lax.fori_loop(lo, hi, lambda t, c: (fill_tile(t).start(), c)[1], 0)
        lax.fori_loop(lo, hi, lambda t, c: (fill_tile(t).wait(), c)[1], 0)


def _moe_dispatch(h2, pos, pad_lo, pad_hi, live, *, tt, tm, n_tiles):
    n = h2.shape[0]
    grid_spec = pltpu.PrefetchScalarGridSpec(
        num_scalar_prefetch=4,
        grid=(n // tt,),
        in_specs=[pl.BlockSpec((tt, D_MODEL), lambda i, *_: (i, 0))],
        out_specs=pl.BlockSpec(memory_space=pl.ANY),
        scratch_shapes=[pltpu.VMEM((tm, D_MODEL), F32), pltpu.SemaphoreType.DMA(()),
                        pltpu.SemaphoreType.DMA(())],
    )
    return pl.pallas_call(
        _moe_dispatch_kernel,
        grid_spec=grid_spec,
        out_shape=jax.ShapeDtypeStruct((n_tiles * tm, D_MODEL), F32),
        compiler_params=_params(("arbitrary",)),
        name="moe_dispatch",
    )(pos, pad_lo, pad_hi, live, h2)


def _moe_expert_kernel(te_ref, live_ref, x_ref, wg_ref, wu_ref, wd_ref, y_ref):
    del te_ref
    is_live = pl.program_id(0) < live_ref[0]

    @pl.when(is_live)
    def _():
        y_ref[...] = _swiglu(x_ref[...].astype(BF16), wg_ref, wu_ref, wd_ref)

    @pl.when(jnp.logical_not(is_live))
    def _():
        y_ref[...] = jnp.zeros(y_ref.shape, F32)


def _moe_experts(xs, tile_expert, live, wg, wu, wd, *, tm, n_tiles):
    fe = D_FF_EXPERT

    def in_tile(t, te, live):
        return (jnp.minimum(t, live[0] - 1), 0)

    def out_tile(t, te, live):
        return (t, 0)

    def expert(t, te, live):
        return (te[t], 0, 0)

    grid_spec = pltpu.PrefetchScalarGridSpec(
        num_scalar_prefetch=2,
        grid=(n_tiles,),
        in_specs=[
            pl.BlockSpec((tm, D_MODEL), in_tile),
            pl.BlockSpec((None, D_MODEL, fe), expert),
            pl.BlockSpec((None, D_MODEL, fe), expert),
            pl.BlockSpec((None, fe, D_MODEL), expert),
        ],
        out_specs=pl.BlockSpec((tm, D_MODEL), out_tile),
    )
    return pl.pallas_call(
        _moe_expert_kernel,
        grid_spec=grid_spec,
        out_shape=jax.ShapeDtypeStruct(xs.shape, F32),
        compiler_params=_params(("arbitrary",)),
        name="moe_experts",
    )(tile_expert, live, xs, wg, wu, wd)


def _moe_combine_kernel(pos_ref, x_ref, g2_ref, r_ref, ys_hbm, o_ref, buf, sem):
    i = pl.program_id(0)
    nt = pl.num_programs(0)
    tt = x_ref.shape[0]
    n = nt * tt

    def fetch(tile, sl):
        base = tile * tt
        for r in range(tt):
            for s in range(2):
                row = pos_ref[s * n + base + r]
                pltpu.make_async_copy(ys_hbm.at[pl.ds(row, 1)], buf.at[sl, pl.ds(s * tt + r, 1)],
                                      sem.at[sl]).start(priority=s)

    @pl.when(i == 0)
    def _():
        fetch(0, 0)

    @pl.when(i + 1 < nt)
    def _():
        fetch(i + 1, (i + 1) % 2)

    sl = i % 2
    _row_copy_wait(ys_hbm, buf.at[sl], sem.at[sl], 2 * tt)
    r = r_ref[...]
    f = r[:, 0:1] * buf[sl, pl.ds(0, tt), :] + r[:, 1:2] * buf[sl, pl.ds(tt, tt), :]
    o_ref[...] = x_ref[...] + g2_ref[...] * f


def _moe_combine(x1, g2, route, pos, ys, *, tt):
    n = x1.shape[0]
    grid_spec = pltpu.PrefetchScalarGridSpec(
        num_scalar_prefetch=1,
        grid=(n // tt,),
        in_specs=[
            pl.BlockSpec((tt, D_MODEL), lambda i, pos: (i, 0)),
            _mod_spec(tt, n, g2),
            pl.BlockSpec((tt, ROUTER_PAD), lambda i, pos: (i, 0)),
            pl.BlockSpec(memory_space=pl.ANY),
        ],
        out_specs=pl.BlockSpec((tt, D_MODEL), lambda i, pos: (i, 0)),
        scratch_shapes=[pltpu.VMEM((2, 2 * tt, D_MODEL), F32), pltpu.SemaphoreType.DMA((2,))],
    )
    return pl.pallas_call(
        _moe_combine_kernel,
        grid_spec=grid_spec,
        out_shape=jax.ShapeDtypeStruct((n, D_MODEL), F32),
        compiler_params=_params(("arbitrary",)),
        name="moe_combine",
    )(pos, x1, g2.arr, route, ys)


def _moe_routed(h2, x1, g2, route, route_t, tile_counts, wg, wu, wd, *, tm, tt):
    n = h2.shape[0]
    n_tiles = (2 * n) // tm + N_EXPERTS
    pos, tile_expert, live, pad_lo, pad_hi = _route_tables(route_t, tile_counts, tm, n_tiles)
    xs = _moe_dispatch(h2, pos, pad_lo, pad_hi, live, tt=tt, tm=tm, n_tiles=n_tiles)
    ys = _moe_experts(xs, tile_expert, live, wg, wu, wd, tm=tm, n_tiles=n_tiles)
    return _moe_combine(x1, g2, route, pos, ys, tt=tt)


def _head_perm():
    idx = []
    for g in range(GROUP):
        for kv in range(N_KV):
            h = kv * GROUP + g
            idx.extend(range(h * HEAD_DIM, (h + 1) * HEAD_DIM))
    return np.asarray(idx, np.int32)


def _relayout_w_in(w):
    pts = np.cumsum([0, Q_W, KV_W, KV_W, GK_W, GK_W, GV_W, GATE_RANK, GV_W, D_MODEL, D_MODEL])
    qa, ka, va, qb, kb, vb, ga, rb, gta, gtb = [w[:, pts[i]:pts[i + 1]] for i in range(10)]
    qa = qa[:, _head_perm()]
    ga = jnp.pad(ga, ((0, 0), (0, GATE_PAD - GATE_RANK)))
    return jnp.concatenate([qa, ka, va, qb, kb, vb, rb, gta, gtb, ga], axis=1).astype(BF16)


def kernel(x_prompt, x_sample, cache_k, cache_v, state_gla, c_prompt, c_sample, ada_w, ada_b, norm1_g, norm2_g, w_in, q_norm_g, k_norm_g, attn_sinks, gla_wa2, gla_ba, gla_norm_g, w_branch_a, w_branch_b, w_out, ffn_w_gate, ffn_w_up, ffn_w_down, router_w, router_b, moe_w_gate, moe_w_up, moe_w_down):
    n_p = BATCH * SEQ
    xp = x_prompt.reshape(n_p, D_MODEL)
    xs = x_sample.reshape(DEC_BATCH, D_MODEL)

    c_pad = -(BATCH + DEC_BATCH) % 8
    c_all = jnp.pad(jnp.concatenate([c_sample, c_prompt], axis=0), ((0, c_pad), (0, 0)))
    mod_sample, mod_prompt = _ada(c_all, ada_w, ada_b, DEC_BATCH, BATCH)
    state_rows = state_gla.reshape(DEPTH, DEC_BATCH, GK_W, GLA_DV)

    bd = jnp.asarray(np.kron(np.eye(N_HEADS), np.ones((HEAD_DIM, HEAD_DIM))), BF16)
    ut = jnp.asarray(_gla_cum_matrix(), BF16)
    gla_upper, gla_pairs = (jnp.asarray(m) for m in _gla_level_masks())
    perm = _head_perm()
    swa_bias = jnp.asarray(_swa_sample_bias())
    swa_prompt_bias = jnp.asarray(_swa_prompt_bias())

    kp_l, vp_l, sp_l, ks_l, vs_l, ss_l = [], [], [], [], [], []
    for l in range(DEPTH):
        mod_p = [_Mod(mod_prompt, l, i) for i in range(N_MOD)]
        mod_s = [_Mod(mod_sample, l, i) for i in range(N_MOD)]

        w = _relayout_w_in(w_in[l])
        qg = (jnp.tile(q_norm_g[l], N_HEADS) * (HEAD_DIM ** -0.5)).reshape(1, Q_W)
        kg = jnp.tile(k_norm_g[l], N_KV).reshape(1, KV_W)
        wa2 = jnp.pad(gla_wa2[l], ((0, GATE_PAD - GATE_RANK), (0, 0))).astype(BF16)
        ba = gla_ba[l].reshape(1, GK_W)
        n1 = norm1_g[l].reshape(1, D_MODEL)
        n2 = norm2_g[l].reshape(1, D_MODEL)
        gn = gla_norm_g[l].reshape(1, GLA_DV)
        wpa = w_branch_a[l][perm].astype(BF16)
        wpb = w_branch_b[l].astype(BF16)
        wo = w_out[l].astype(BF16)
        sink_rows = jnp.broadcast_to(attn_sinks[l][_sample_head_of_row()][:, None], (N_HEADS, LANES))

        q, k, v, gq, gk, gv, la, rbs, sga, sgb = _mixin(
            xp, mod_p[0], mod_p[1], n1, w, bd, qg, kg, wa2, ba, tm=ROW_TILE_RESIDENT)
        a_out = _swa_prompt(attn_sinks[l], q, k, v, swa_prompt_bias)
        g_out, s_fin = _gla_prompt(gq, gk, la, gv, rbs, gn, ut, gla_upper, gla_pairs)
        is_moe = l % 2 == 1
        if is_moe:
            rw = jnp.pad(router_w[l // 2], ((0, 0), (0, ROUTER_PAD - N_EXPERTS))).astype(BF16)
            rbias = jnp.pad(router_b[l // 2], (0, ROUTER_PAD - N_EXPERTS)).reshape(1, ROUTER_PAD)
        merged = _merge(xp, a_out, g_out, sga, sgb, wpa, wpb, wo, mod_p[2], mod_p[3], mod_p[4], n2,
                        tm=ROW_TILE_RESIDENT, h2_dtype=F32 if is_moe else BF16,
                        router=(rw, rbias) if is_moe else None)
        x1, h2 = merged[0], merged[1]
        for cache, full in ((kp_l, k), (vp_l, v)):
            tail = full.reshape(BATCH, SEQ, KV_W)[:, SEQ - WINDOW:]
            cache.append(tail.reshape(BATCH, WINDOW, N_KV, HEAD_DIM))
        sp_l.append(s_fin.reshape(BATCH, GLA_HEADS, GLA_DK, GLA_DV))

        qs, ksn, vsn, gqs, gks, gvs, las, rbss, sgas, sgbs = _mixin(
            xs, mod_s[0], mod_s[1], n1, w, bd, qg, kg, wa2, ba, tm=DEC_BATCH)
        a_s, nk, nv = _swa_sample(
            qs.astype(F32).reshape(DEC_BATCH, N_HEADS, HEAD_DIM),
            ksn.reshape(DEC_BATCH, N_KV, HEAD_DIM), vsn.reshape(DEC_BATCH, N_KV, HEAD_DIM),
            cache_k[l].reshape(1, DEC_BATCH, CACHE_ROWS, HEAD_DIM),
            cache_v[l].reshape(1, DEC_BATCH, CACHE_ROWS, HEAD_DIM), swa_bias, sink_rows, layer=0)
        g_s, s_new = _gla_sample(gqs, gks, las, gvs, rbss, gn, state_rows, layer=l)
        x1s, h2s = _merge(xs, a_s.reshape(DEC_BATCH, Q_W), g_s, sgas, sgbs, wpa, wpb, wo,
                          mod_s[2], mod_s[3], mod_s[4], n2, tm=DEC_BATCH)
        ks_l.append(nk.reshape(DEC_BATCH, WINDOW, N_KV, HEAD_DIM))
        vs_l.append(nv.reshape(DEC_BATCH, WINDOW, N_KV, HEAD_DIM))
        ss_l.append(s_new.reshape(DEC_BATCH, GLA_HEADS, GLA_DK, GLA_DV))

        i = l // 2
        if not is_moe:
            wg, wu, wd = ffn_w_gate[i].astype(BF16), ffn_w_up[i].astype(BF16), ffn_w_down[i].astype(BF16)
            xp = _ffn(h2, x1, mod_p[5], wg, wu, wd, tm=ROW_TILE_SWIGLU, tf=D_FF // 2)
            xs = _ffn(h2s, x1s, mod_s[5], wg, wu, wd, tm=DEC_BATCH, tf=D_FF // 2)
        else:
            wg, wu, wd = moe_w_gate[i].astype(BF16), moe_w_up[i].astype(BF16), moe_w_down[i].astype(BF16)
            xp = _moe_routed(h2, x1, mod_p[5], *merged[2:5], wg, wu, wd,
                             tm=ROW_TILE_SWIGLU, tt=MOE_TOKEN_TILE)
            xs = _moe(h2s, x1s, mod_s[5], rw, rbias, wg, wu, wd, tm=DEC_BATCH)

    return (xp.reshape(BATCH, SEQ, D_MODEL), xs.reshape(DEC_BATCH, 1, D_MODEL),
            jnp.stack(kp_l), jnp.stack(vp_l), jnp.stack(sp_l),
            jnp.stack(ks_l), jnp.stack(vs_l), jnp.stack(ss_l))
```

```python
import functools
from typing import NamedTuple

import jax
import jax.numpy as jnp
import numpy as np
from jax import lax
from jax.experimental import pallas as pl
from jax.experimental.pallas import tpu as pltpu

D_MODEL = 1024
BATCH = 4
SEQ = 4096
DEPTH = 2
DEC_BATCH = 128
N_HEADS = 8
N_KV = 2
HEAD_DIM = 64
GROUP = N_HEADS // N_KV
WINDOW = 128
GLA_HEADS = 4
GLA_DK = 64
GLA_DV = 128
GATE_RANK = 16
GATE_TAU = 16.0
D_FF = 2816
N_EXPERTS = 8
D_FF_EXPERT = 1408
EPS = 1e-6

Q_W = N_HEADS * HEAD_DIM
KV_W = N_KV * HEAD_DIM
GK_W = GLA_HEADS * GLA_DK
GV_W = GLA_HEADS * GLA_DV

LANES = 128
GATE_PAD = LANES
ROUTER_PAD = LANES
VMEM_LIMIT = 56 * 1024 * 1024

ROW_TILE_RESIDENT = 1024
ROW_TILE_SWIGLU = 512
MOE_TOKEN_TILE = 512

F32 = jnp.float32
BF16 = jnp.bfloat16

_C_Q = 0
_C_K = _C_Q + Q_W
_C_V = _C_K + KV_W
_C_GQ = _C_V + KV_W
_C_GK = _C_GQ + GK_W
_C_GV = _C_GK + GK_W
_C_RB = _C_GV + GV_W
_C_GA = _C_RB + GV_W
_C_GB = _C_GA + D_MODEL
_C_LR = _C_GB + D_MODEL
PROJ_PAD = _C_LR + GATE_PAD

GLA_CHUNK = 128
GLA_LEVELS = 7
GLA_MXU_LEVELS = 3
GLA_CHUNKS_PER_STEP = 4


def _params(sem, vmem=VMEM_LIMIT):
    return pltpu.CompilerParams(dimension_semantics=sem, vmem_limit_bytes=vmem)


def _dot(a, b):
    return jnp.dot(a, b, preferred_element_type=F32)


def _dot_nt(a, b):
    return lax.dot_general(a, b, (((1,), (1,)), ((), ())), preferred_element_type=F32)


def _dot_tn(a, b):
    return lax.dot_general(a, b, (((0,), (0,)), ((), ())), preferred_element_type=F32)


def _sigmoid(x):
    return 0.5 * jnp.tanh(0.5 * x) + 0.5


def _silu(x):
    return x * _sigmoid(x)


N_MOD = 6


class _Mod(NamedTuple):
    arr: jax.Array
    layer: int
    idx: int


def _ada_kernel(c_ref, w_ref, b_ref, os_ref, op_ref):
    c = c_ref[...]
    mod = _dot(_silu(c).astype(BF16), w_ref[...].astype(BF16)) + b_ref[...]
    n_s = os_ref.shape[0]
    os_ref[...] = mod[0:n_s]
    for b in range(op_ref.shape[0]):
        op_ref[b] = mod[n_s + b:n_s + b + 1]


def _ada(c_all, ada_w, ada_b, n_sample, n_prompt):
    rows = c_all.shape[0]
    return pl.pallas_call(
        _ada_kernel,
        grid=(DEPTH, N_MOD),
        in_specs=[
            pl.BlockSpec((rows, D_MODEL), lambda l, j: (0, 0)),
            pl.BlockSpec((None, D_MODEL, D_MODEL), lambda l, j: (l, 0, j)),
            pl.BlockSpec((None, 1, D_MODEL), lambda l, j: (l, 0, j)),
        ],
        out_specs=[
            pl.BlockSpec((None, None, n_sample, D_MODEL), lambda l, j: (l, j, 0, 0)),
            pl.BlockSpec((None, None, n_prompt, 1, D_MODEL), lambda l, j: (l, j, 0, 0, 0)),
        ],
        out_shape=[
            jax.ShapeDtypeStruct((DEPTH, N_MOD, n_sample, D_MODEL), F32),
            jax.ShapeDtypeStruct((DEPTH, N_MOD, n_prompt, 1, D_MODEL), F32),
        ],
        compiler_params=_params(("parallel", "parallel")),
        name="ada_mod",
    )(c_all, ada_w, ada_b.reshape(DEPTH, 1, N_MOD * D_MODEL))


def _mixin_kernel(x_ref, sh_ref, sc_ref, n1_ref, w_ref, bd_ref, qg_ref, kg_ref, wa2_ref, ba_ref,
                  q_ref, k_ref, v_ref, gq_ref, gk_ref, gv_ref, la_ref, rb_ref, sga_ref, sgb_ref):
    x = x_ref[...]
    ms = jnp.mean(x * x, axis=-1, keepdims=True)
    h = x * lax.rsqrt(ms + EPS) * n1_ref[...]
    h = h * (1.0 + sc_ref[...]) + sh_ref[...]
    hb = h.astype(BF16)

    def proj(a, b):
        return _dot(hb, w_ref[:, a:b])

    q = proj(_C_Q, _C_K)
    ssq = _dot((q * q).astype(BF16), bd_ref[...])
    q_ref[...] = (q * lax.rsqrt(ssq * (1.0 / HEAD_DIM) + EPS) * qg_ref[...]).astype(BF16)
    k = proj(_C_K, _C_V)
    ssk = _dot((k * k).astype(BF16), bd_ref[0:KV_W, 0:KV_W])
    k_ref[...] = k * lax.rsqrt(ssk * (1.0 / HEAD_DIM) + EPS) * kg_ref[...]
    v_ref[...] = proj(_C_V, _C_GQ)
    gq_ref[...] = proj(_C_GQ, _C_GK) * (GLA_DK ** -0.5)
    gk_ref[...] = proj(_C_GK, _C_GV)
    gv_ref[...] = proj(_C_GV, _C_RB)
    rb_ref[...] = _silu(proj(_C_RB, _C_GA)).astype(BF16)
    sga_ref[...] = _sigmoid(proj(_C_GA, _C_GB)).astype(BF16)
    sgb_ref[...] = _sigmoid(proj(_C_GB, _C_LR)).astype(BF16)
    ga = proj(_C_LR, PROJ_PAD)
    xg = _dot(ga.astype(BF16), wa2_ref[...]) + ba_ref[...]
    la_ref[...] = (jnp.minimum(xg, 0.0) - jnp.log1p(jnp.exp(-jnp.abs(xg)))) * (1.0 / GATE_TAU)


def _mixin(x, sh, sc, n1, w, bd, qg, kg, wa2, ba, *, tm):
    n = x.shape[0]
    nt = n // tm

    def row(i):
        return (i, 0)

    def const(shape):
        return pl.BlockSpec(shape, lambda i: (0,) * len(shape), pipeline_mode=pl.Buffered(1))

    def out(width, dtype):
        return pl.BlockSpec((tm, width), row), jax.ShapeDtypeStruct((n, width), dtype)

    outs = [out(Q_W, BF16), out(KV_W, F32), out(KV_W, F32), out(GK_W, F32), out(GK_W, F32),
            out(GV_W, F32), out(GK_W, F32), out(GV_W, BF16), out(D_MODEL, BF16), out(D_MODEL, BF16)]
    return pl.pallas_call(
        _mixin_kernel,
        grid=(nt,),
        in_specs=[
            pl.BlockSpec((tm, D_MODEL), row), _mod_spec(tm, n, sh), _mod_spec(tm, n, sc), const((1, D_MODEL)),
            const((D_MODEL, PROJ_PAD)), const((Q_W, Q_W)), const((1, Q_W)), const((1, KV_W)),
            const((GATE_PAD, GK_W)), const((1, GK_W)),
        ],
        out_specs=[o[0] for o in outs],
        out_shape=[o[1] for o in outs],
        compiler_params=_params(("parallel",)),
        name="mixer_in",
    )(x, sh.arr, sc.arr, n1, w, bd, qg, kg, wa2, ba)


SWA_BLOCKS = 8


def _head_slope(h):
    return float(2.0 ** (-8.0 * (h + 1) / N_HEADS))


def _swa_prompt_bias():
    blk = WINDOW
    dist = np.arange(blk)[:, None] + blk - np.arange(2 * blk)[None, :]
    slopes = np.asarray([_head_slope(h) for h in range(N_HEADS)])[:, None, None]
    return np.where((dist >= 0) & (dist <= WINDOW), -slopes * dist, -np.inf).astype(np.float32)


def _swa_prompt_kernel(sink_ref, q_ref, kp_ref, kc_ref, vp_ref, vc_ref, bias_ref, o_ref):
    n = pl.program_id(1)
    blk = WINDOW
    kall = jnp.concatenate([kp_ref[...], kc_ref[...]], axis=0).astype(BF16)
    vall = jnp.concatenate([vp_ref[...], vc_ref[...]], axis=0).astype(BF16)
    col = lax.broadcasted_iota(jnp.int32, (blk, 2 * blk), 1)
    first_key = jnp.where(n > 0, 0, blk)
    klane = lax.broadcasted_iota(jnp.int32, kall.shape, 1)
    kall_kv = [jnp.where((klane // HEAD_DIM) == kv, kall, jnp.zeros_like(kall)) for kv in range(N_KV)]
    olane = lax.broadcasted_iota(jnp.int32, (blk, KV_W), 1)
    q = q_ref[...]
    for j in range(SWA_BLOCKS):
        vv = vall[j * blk:(j + 2) * blk]
        outs = []
        for g in range(GROUP):
            qp = q[j * blk:(j + 1) * blk, g * KV_W:(g + 1) * KV_W]
            pair = []
            for kv in range(N_KV):
                h = kv * GROUP + g
                s = _dot_nt(qp, kall_kv[kv][j * blk:(j + 2) * blk]) + bias_ref[h]
                if j == 0:
                    s = jnp.where(col >= first_key, s, -jnp.inf)
                sink = sink_ref[h]
                m = jnp.maximum(jnp.max(s, axis=-1, keepdims=True), sink)
                p = jnp.exp(s - m)
                den = jnp.sum(p, axis=-1, keepdims=True) + jnp.exp(sink - m)
                pair.append(_dot(p.astype(BF16), vv) * (1.0 / den))
            outs.append(jnp.where(olane < HEAD_DIM, pair[0], pair[1]))
        o_ref[j * blk:(j + 1) * blk, :] = jnp.concatenate(outs, axis=-1).astype(BF16)


def _swa_prompt(sinks, q, k, v, bias):
    nb = SEQ // WINDOW
    steps = nb // SWA_BLOCKS
    tq = SWA_BLOCKS * WINDOW

    def cur(b, n):
        return (b * steps + n, 0)

    def prev(b, n):
        return (b * nb + jnp.maximum(n * SWA_BLOCKS - 1, 0), 0)

    return pl.pallas_call(
        _swa_prompt_kernel,
        grid=(BATCH, steps),
        in_specs=[
            pl.BlockSpec(memory_space=pltpu.SMEM),
            pl.BlockSpec((tq, Q_W), cur),
            pl.BlockSpec((WINDOW, KV_W), prev), pl.BlockSpec((tq, KV_W), cur),
            pl.BlockSpec((WINDOW, KV_W), prev), pl.BlockSpec((tq, KV_W), cur),
            pl.BlockSpec(bias.shape, lambda b, n: (0, 0, 0), pipeline_mode=pl.Buffered(1)),
        ],
        out_specs=pl.BlockSpec((tq, Q_W), cur),
        out_shape=jax.ShapeDtypeStruct((BATCH * SEQ, Q_W), BF16),
        compiler_params=_params(("parallel", "parallel")),
        name="swa_prompt",
    )(sinks, q, k, k, v, v, bias)


SAMPLE_TB = 8
CACHE_ROWS = WINDOW * N_KV


def _sample_head_of_row():
    j = np.arange(N_HEADS)
    return (j % N_KV) * GROUP + j // N_KV


def _swa_sample_bias():
    j = np.arange(N_HEADS)[:, None]
    c = np.arange(CACHE_ROWS)[None, :]
    slope = 2.0 ** (-8.0 * (_sample_head_of_row()[:, None] + 1) / N_HEADS)
    bias = -slope * (WINDOW - c // N_KV)
    return np.where(c % N_KV == j % N_KV, bias, -np.inf).astype(np.float32)


def _swa_sample_kernel(q_ref, kn_ref, vn_ref, ck_ref, cv_ref, bias_ref, sk_ref, o_ref, ok_ref, ov_ref):
    rows = CACHE_ROWS
    sink = sk_ref[...][:, 0:1]
    q = q_ref[...]
    kn = kn_ref[...]
    vn = vn_ref[...]
    kc = ck_ref[...]
    vc = cv_ref[...]
    kn8 = jnp.concatenate([kn] * GROUP, axis=1)
    vn8 = jnp.concatenate([vn] * GROUP, axis=1)
    s = lax.dot_general(q.astype(BF16), kc.astype(BF16), (((2,), (2,)), ((0,), (0,))),
                        preferred_element_type=F32) + bias_ref[...]
    s_new = jnp.sum(q * kn8, axis=-1, keepdims=True)
    m = jnp.maximum(jnp.maximum(jnp.max(s, axis=-1, keepdims=True), s_new), sink)
    p = jnp.exp(s - m)
    p_new = jnp.exp(s_new - m)
    den = jnp.sum(p, axis=-1, keepdims=True) + p_new + jnp.exp(sink - m)
    o = lax.dot_general(p.astype(BF16), vc.astype(BF16), (((2,), (1,)), ((0,), (0,))),
                        preferred_element_type=F32) + p_new * vn8
    o_ref[...] = o * (1.0 / den)
    ok_ref[:, pl.ds(0, rows - N_KV), :] = kc[:, N_KV:, :]
    ok_ref[:, pl.ds(rows - N_KV, N_KV), :] = kn
    ov_ref[:, pl.ds(0, rows - N_KV), :] = vc[:, N_KV:, :]
    ov_ref[:, pl.ds(rows - N_KV, N_KV), :] = vn


def _swa_sample(q, kn, vn, ck, cv, bias, sinks, *, layer):
    tb = 2 * SAMPLE_TB
    nb = DEC_BATCH // tb

    def const(shape):
        return pl.BlockSpec(shape, lambda i: (0,) * len(shape))

    def per_seq(*dims):
        return pl.BlockSpec((tb,) + dims, lambda i: (i, 0, 0))

    cache_spec = pl.BlockSpec((None, tb, CACHE_ROWS, HEAD_DIM), lambda i: (layer, i, 0, 0))
    return pl.pallas_call(
        _swa_sample_kernel,
        grid=(nb,),
        in_specs=[per_seq(N_HEADS, HEAD_DIM), per_seq(N_KV, HEAD_DIM), per_seq(N_KV, HEAD_DIM),
                  cache_spec, cache_spec,
                  const((N_HEADS, CACHE_ROWS)), const((N_HEADS, LANES))],
        out_specs=[per_seq(N_HEADS, HEAD_DIM), per_seq(CACHE_ROWS, HEAD_DIM), per_seq(CACHE_ROWS, HEAD_DIM)],
        out_shape=[
            jax.ShapeDtypeStruct((DEC_BATCH, N_HEADS, HEAD_DIM), F32),
            jax.ShapeDtypeStruct((DEC_BATCH, CACHE_ROWS, HEAD_DIM), F32),
            jax.ShapeDtypeStruct((DEC_BATCH, CACHE_ROWS, HEAD_DIM), F32),
        ],
        compiler_params=_params(("parallel",)),
        name="swa_sample",
    )(q, kn, vn, ck, cv, bias, sinks)


def _gla_cum_matrix():
    c = GLA_CHUNK
    tri = np.tril(np.ones((c, c), np.float32))
    i = np.arange(c)
    blocks = []
    for lvl in range(GLA_MXU_LEVELS):
        half = 1 << lvl
        mid = (i // (2 * half)) * (2 * half) + half - 1
        blocks.append(tri - tri[mid])
    blocks.append(tri)
    return np.concatenate(blocks, axis=0)


def _split3(x):
    hi = x.astype(BF16)
    r1 = x - hi.astype(F32)
    mid = r1.astype(BF16)
    lo = (r1 - mid.astype(F32)).astype(BF16)
    return hi, mid, lo


def _gla_level_masks():
    cl = GLA_CHUNK
    r = np.arange(cl)
    upper = np.stack([np.broadcast_to(((r >> lvl) & 1)[:, None], (cl, GK_W)) for lvl in range(GLA_LEVELS)])
    ri = np.tile(r, GLA_HEADS)[:, None]
    pairs = [(ri >> (lvl + 1)) == (r[None, :] >> (lvl + 1)) for lvl in range(GLA_LEVELS)]
    pairs.append(ri == r[None, :])
    return upper.astype(np.float32), np.stack(pairs).astype(np.float32)


def _gla_prompt_kernel(q_ref, k_ref, la_ref, v_ref, rb_ref, gn_ref, ut_ref, up_ref, pm_ref,
                       o_ref, s_ref, st_ref):
    c = pl.program_id(1)
    cl = GLA_CHUNK

    @pl.when(c == 0)
    def _():
        st_ref[0] = jnp.zeros(st_ref.shape[1:], F32)

    ut = ut_ref[...]
    lane = lax.broadcasted_iota(jnp.int32, (cl, GK_W), 1)
    head_of_lane = lane // GLA_DK
    ones = jnp.ones((cl, LANES), BF16)
    state = [st_ref[c % 2, h] for h in range(GLA_HEADS)]
    for cc in range(GLA_CHUNKS_PER_STEP):
        rows = slice(cc * cl, (cc + 1) * cl)
        hi, mid, lo = _split3(la_ref[rows, :])
        tall = _dot(ut, hi) + _dot(ut, mid) + _dot(ut, lo)
        q = q_ref[rows, :]
        k = k_ref[rows, :]
        attn_all = jnp.zeros((GLA_HEADS * cl, cl), F32)
        cum = tall[GLA_MXU_LEVELS * cl:(GLA_MXU_LEVELS + 1) * cl]
        for lvl in range(GLA_LEVELS + 1):
            if lvl < GLA_LEVELS:
                if lvl < GLA_MXU_LEVELS:
                    t_lvl = tall[lvl * cl:(lvl + 1) * cl]
                else:
                    half = 1 << lvl
                    mids = [jnp.broadcast_to(cum[b0 + half - 1:b0 + half, :], (2 * half, GK_W))
                            for b0 in range(0, cl, 2 * half)]
                    t_lvl = cum - (mids[0] if len(mids) == 1 else jnp.concatenate(mids, axis=0))
                e = jnp.exp(-jnp.abs(t_lvl))
                e_up = e * up_ref[lvl]
                qt = (q * e_up).astype(BF16)
                kt = (k * (e - e_up)).astype(BF16)
            else:
                qt = q.astype(BF16)
                kt = k.astype(BF16)
            q_heads = jnp.concatenate(
                [jnp.where(head_of_lane == h, qt, jnp.zeros_like(qt)) for h in range(GLA_HEADS)], axis=0)
            attn_all = attn_all + _dot_nt(q_heads, kt) * pm_ref[lvl]

        last = cum[cl - 1:cl, :]
        qe = (q * jnp.exp(cum)).astype(BF16)
        kd = (k * jnp.exp(last - cum)).astype(BF16)
        dec = jnp.exp(_dot_tn(hi, ones) + _dot_tn(mid, ones) + _dot_tn(lo, ones))
        v = v_ref[rows, :]
        rb = rb_ref[rows, :]
        outs = []
        for h in range(GLA_HEADS):
            vh = v[:, h * GLA_DV:(h + 1) * GLA_DV].astype(BF16)
            o = _dot(attn_all[h * cl:(h + 1) * cl].astype(BF16), vh) + _dot(qe, state[h].astype(BF16))
            kdh = jnp.where(head_of_lane == h, kd, jnp.zeros_like(kd))
            state[h] = state[h] * dec + _dot_tn(kdh, vh)
            ms = jnp.mean(o * o, axis=-1, keepdims=True)
            g = o * lax.rsqrt(ms + EPS) * gn_ref[...]
            outs.append(g * rb[:, h * GLA_DV:(h + 1) * GLA_DV].astype(F32))
        o_ref[rows, :] = jnp.concatenate(outs, axis=-1).astype(BF16)

    for h in range(GLA_HEADS):
        st_ref[(c + 1) % 2, h] = state[h]

    @pl.when(c == pl.num_programs(1) - 1)
    def _():
        fin = (SEQ // (GLA_CHUNK * GLA_CHUNKS_PER_STEP)) % 2
        s_ref[...] = st_ref[fin, 0] + st_ref[fin, 1] + st_ref[fin, 2] + st_ref[fin, 3]


def _gla_prompt(gq, gk, la, gv, rbs, gn, ut, upper, pairs):
    cl = GLA_CHUNK * GLA_CHUNKS_PER_STEP
    nc = SEQ // cl

    def row(b, c):
        return (b * nc + c, 0)

    def const(shape):
        return pl.BlockSpec(shape, lambda b, c: (0,) * len(shape), pipeline_mode=pl.Buffered(1))

    return pl.pallas_call(
        _gla_prompt_kernel,
        grid=(BATCH, nc),
        in_specs=[
            pl.BlockSpec((cl, GK_W), row), pl.BlockSpec((cl, GK_W), row), pl.BlockSpec((cl, GK_W), row),
            pl.BlockSpec((cl, GV_W), row), pl.BlockSpec((cl, GV_W), row),
            const((1, GLA_DV)), const(ut.shape), const(upper.shape), const(pairs.shape),
        ],
        out_specs=[
            pl.BlockSpec((cl, GV_W), row),
            pl.BlockSpec((None, GK_W, GLA_DV), lambda b, c: (b, 0, 0)),
        ],
        out_shape=[
            jax.ShapeDtypeStruct((BATCH * SEQ, GV_W), BF16),
            jax.ShapeDtypeStruct((BATCH, GK_W, GLA_DV), F32),
        ],
        scratch_shapes=[pltpu.VMEM((2, GLA_HEADS, GK_W, GLA_DV), F32)],
        compiler_params=_params(("parallel", "arbitrary")),
        name="gla_prompt",
    )(gq, gk, la, gv, rbs, gn, ut, upper, pairs)


def _gla_sample_kernel(q_ref, k_ref, la_ref, v_ref, rb_ref, gn_ref, s_ref, o_ref, so_ref):
    tb = SAMPLE_TB
    dec = jnp.exp(la_ref[...])
    pieces = []
    for x in (dec, k_ref[...], q_ref[...]):
        hi, mid, lo = _split3(x)
        stacked = jnp.concatenate(
            [hi.astype(F32), mid.astype(F32), lo.astype(F32), jnp.zeros_like(x)], axis=0)
        pieces.append(stacked.astype(BF16))
    prow = lax.broadcasted_iota(jnp.int32, (4 * tb, LANES), 0)
    rb = rb_ref[...].astype(F32)
    v = v_ref[...]
    for bi in range(tb):
        sel = jnp.where((prow % tb) == bi, 1.0, 0.0).astype(BF16)
        a_col, k_col, q_col = [_dot_tn(p, sel) for p in pieces]
        for h in range(GLA_HEADS):
            rs = slice(h * GLA_DK, (h + 1) * GLA_DK)
            vs = slice(h * GLA_DV, (h + 1) * GLA_DV)
            s_new = a_col[rs] * s_ref[bi, rs, :] + k_col[rs] * v[bi:bi + 1, vs]
            so_ref[bi, rs, :] = s_new
            o = jnp.sum(q_col[rs] * s_new, axis=0, keepdims=True)
            ms = jnp.mean(o * o, axis=-1, keepdims=True)
            g = o * lax.rsqrt(ms + EPS) * gn_ref[...]
            o_ref[bi:bi + 1, vs] = g * rb[bi:bi + 1, vs]


def _gla_sample(gq, gk, la, gv, rbs, gn, state, *, layer):
    tb = SAMPLE_TB
    nb = DEC_BATCH // tb

    def row(w):
        return pl.BlockSpec((tb, w), lambda i: (i, 0))

    st_spec = pl.BlockSpec((tb, GK_W, GLA_DV), lambda i: (i, 0, 0))
    st_in_spec = pl.BlockSpec((None, tb, GK_W, GLA_DV), lambda i: (layer, i, 0, 0))
    return pl.pallas_call(
        _gla_sample_kernel,
        grid=(nb,),
        in_specs=[row(GK_W), row(GK_W), row(GK_W), row(GV_W), row(GV_W),
                  pl.BlockSpec((1, GLA_DV), lambda i: (0, 0)), st_in_spec],
        out_specs=[row(GV_W), st_spec],
        out_shape=[
            jax.ShapeDtypeStruct((DEC_BATCH, GV_W), F32),
            jax.ShapeDtypeStruct((DEC_BATCH, GK_W, GLA_DV), F32),
        ],
        compiler_params=_params(("parallel",)),
        name="gla_sample",
    )(gq, gk, la, gv, rbs, gn, state)


def _merge_kernel(x_ref, a_ref, g_ref, sga_ref, sgb_ref, wpa_ref, wpb_ref, wo_ref,
                  g1_ref, sh_ref, sc_ref, n2_ref, *rest):
    with_router = len(rest) == 8
    x1_ref, h2_ref = rest[3:5] if with_router else rest
    ya = _dot(a_ref[...].astype(BF16), wpa_ref[...])
    yb = _dot(g_ref[...].astype(BF16), wpb_ref[...])
    merged = sga_ref[...].astype(F32) * ya + sgb_ref[...].astype(F32) * yb
    mix = _dot(merged.astype(BF16), wo_ref[...])
    x1 = x_ref[...] + g1_ref[...] * mix
    x1_ref[...] = x1
    ms = jnp.mean(x1 * x1, axis=-1, keepdims=True)
    h = x1 * lax.rsqrt(ms + EPS) * n2_ref[...]
    h2 = h * (1.0 + sc_ref[...]) + sh_ref[...]
    h2_ref[...] = h2.astype(h2_ref.dtype)
    if with_router:
        rw_ref, rbias_ref, tri_ref = rest[0:3]
        route_ref, route_t_ref, cnt_ref = rest[5:8]
        logits = _dot(h2.astype(BF16), rw_ref[...]) + rbias_ref[...]
        packed, counts = _route_pack(logits, tri_ref[...])
        route_ref[...] = packed
        route_t_ref[...] = packed.T[0:route_t_ref.shape[0], :]
        cnt_ref[...] = jnp.broadcast_to(counts, cnt_ref.shape)


def _mod_spec(tm, n, mod):
    if mod.arr.ndim == 4:
        return pl.BlockSpec((None, None, tm, D_MODEL), lambda i, *_: (mod.layer, mod.idx, i, 0))
    tiles_per_seq = (n // mod.arr.shape[2]) // tm
    return pl.BlockSpec((None, None, None, 1, D_MODEL),
                        lambda i, *_: (mod.layer, mod.idx, i // tiles_per_seq, 0, 0))


def _merge(x, a, g, sga, sgb, wpa, wpb, wo, g1, sh2, sc2, n2, *, tm, router=None):
    n = x.shape[0]

    def row(w):
        return pl.BlockSpec((tm, w), lambda i: (i, 0))

    def const(shape):
        return pl.BlockSpec(shape, lambda i: (0,) * len(shape))

    in_specs = [row(D_MODEL), row(a.shape[1]), row(GV_W), row(D_MODEL), row(D_MODEL),
                const(wpa.shape), const(wpb.shape), const(wo.shape),
                _mod_spec(tm, n, g1), _mod_spec(tm, n, sh2), _mod_spec(tm, n, sc2), const((1, D_MODEL))]
    out_specs = [row(D_MODEL), row(D_MODEL)]
    out_shape = [jax.ShapeDtypeStruct((n, D_MODEL), F32), jax.ShapeDtypeStruct((n, D_MODEL), BF16)]
    args = [x, a, g, sga, sgb, wpa, wpb, wo, g1.arr, sh2.arr, sc2.arr, n2]
    if router is not None:
        tri = jnp.asarray(np.tril(np.ones((tm, tm), np.float32), -1), BF16)
        in_specs += [const(router[0].shape), const(router[1].shape), const((tm, tm))]
        out_specs += [row(ROUTER_PAD), pl.BlockSpec((8, tm), lambda i: (0, i)),
                      pl.BlockSpec((None, 8, ROUTER_PAD), lambda i: (i, 0, 0))]
        out_shape += [jax.ShapeDtypeStruct((n, ROUTER_PAD), F32), jax.ShapeDtypeStruct((8, n), F32),
                      jax.ShapeDtypeStruct((n // tm, 8, ROUTER_PAD), F32)]
        args += [router[0], router[1], tri]
    return pl.pallas_call(
        _merge_kernel,
        grid=(n // tm,),
        in_specs=in_specs,
        out_specs=out_specs,
        out_shape=out_shape,
        compiler_params=_params(("parallel",)),
        name="merge_out",
    )(*args)


def _swiglu(hb, wg_ref, wu_ref, wd_ref):
    act = (_silu(_dot(hb, wg_ref[...])) * _dot(hb, wu_ref[...])).astype(BF16)
    return _dot(act, wd_ref[...])


def _ffn_kernel(h_ref, x_ref, g2_ref, wg_ref, wu_ref, wd_ref, o_ref, acc_ref):
    f = pl.program_id(1)
    y = _swiglu(h_ref[...], wg_ref, wu_ref, wd_ref)

    @pl.when(f == 0)
    def _():
        acc_ref[...] = y

    @pl.when(f > 0)
    def _():
        acc_ref[...] += y

    @pl.when(f == pl.num_programs(1) - 1)
    def _():
        o_ref[...] = x_ref[...] + g2_ref[...] * acc_ref[...]


def _ffn(h2, x1, g2, wg, wu, wd, *, tm, tf):
    n = h2.shape[0]
    mod_spec = _mod_spec(tm, n, g2)
    return pl.pallas_call(
        _ffn_kernel,
        grid=(n // tm, D_FF // tf),
        in_specs=[
            pl.BlockSpec((tm, D_MODEL), lambda i, f: (i, 0)),
            pl.BlockSpec((tm, D_MODEL), lambda i, f: (i, 0)),
            mod_spec,
            pl.BlockSpec((D_MODEL, tf), lambda i, f: (0, f)),
            pl.BlockSpec((D_MODEL, tf), lambda i, f: (0, f)),
            pl.BlockSpec((tf, D_MODEL), lambda i, f: (f, 0)),
        ],
        out_specs=pl.BlockSpec((tm, D_MODEL), lambda i, f: (i, 0)),
        out_shape=jax.ShapeDtypeStruct((n, D_MODEL), F32),
        scratch_shapes=[pltpu.VMEM((tm, D_MODEL), F32)],
        compiler_params=_params(("parallel", "arbitrary")),
        name="ffn_dense",
    )(h2, x1, g2.arr, wg, wu, wd)


def _moe_kernel(h_ref, x_ref, g2_ref, rw_ref, rbias_ref, wg_ref, wu_ref, wd_ref, o_ref, acc_ref, gate_ref):
    e = pl.program_id(1)
    hb = h_ref[...]
    tm = hb.shape[0]
    lane = lax.broadcasted_iota(jnp.int32, (tm, ROUTER_PAD), 1).astype(F32)

    @pl.when(e == 0)
    def _():
        logits = _dot(hb, rw_ref[...]) + rbias_ref[...]
        i1, i2, p1, p2 = _top2(logits, lane)
        gate_ref[...] = jnp.where(lane == i1, p1, 0.0) + jnp.where(lane == i2, p2, 0.0)
        acc_ref[...] = jnp.zeros_like(acc_ref)

    ge = jnp.sum(jnp.where(lane == e.astype(F32), gate_ref[...], 0.0), axis=-1, keepdims=True)
    act = (_silu(_dot(hb, wg_ref[...])) * _dot(hb, wu_ref[...])).astype(BF16)
    acc_ref[...] += ge * _dot(act, wd_ref[...])

    @pl.when(e == pl.num_programs(1) - 1)
    def _():
        o_ref[...] = x_ref[...] + g2_ref[...] * acc_ref[...]


def _moe(h2, x1, g2, rw, rbias, wg, wu, wd, *, tm):
    n = h2.shape[0]
    mod_spec = _mod_spec(tm, n, g2)
    fe = D_FF_EXPERT
    return pl.pallas_call(
        _moe_kernel,
        grid=(n // tm, N_EXPERTS),
        in_specs=[
            pl.BlockSpec((tm, D_MODEL), lambda i, e: (i, 0)),
            pl.BlockSpec((tm, D_MODEL), lambda i, e: (i, 0)),
            mod_spec,
            pl.BlockSpec((D_MODEL, ROUTER_PAD), lambda i, e: (0, 0)),
            pl.BlockSpec((1, ROUTER_PAD), lambda i, e: (0, 0)),
            pl.BlockSpec((None, D_MODEL, fe), lambda i, e: (e, 0, 0)),
            pl.BlockSpec((None, D_MODEL, fe), lambda i, e: (e, 0, 0)),
            pl.BlockSpec((None, fe, D_MODEL), lambda i, e: (e, 0, 0)),
        ],
        out_specs=pl.BlockSpec((tm, D_MODEL), lambda i, e: (i, 0)),
        out_shape=jax.ShapeDtypeStruct((n, D_MODEL), F32),
        scratch_shapes=[pltpu.VMEM((tm, D_MODEL), F32), pltpu.VMEM((tm, ROUTER_PAD), F32)],
        compiler_params=_params(("parallel", "arbitrary")),
        name="moe",
    )(h2, x1, g2.arr, rw, rbias, wg, wu, wd)


def _top2(logits, lane):
    lg = jnp.where(lane < N_EXPERTS, logits, -jnp.inf)
    m1 = jnp.max(lg, axis=-1, keepdims=True)
    i1 = jnp.min(jnp.where(lg == m1, lane, float(ROUTER_PAD)), axis=-1, keepdims=True)
    lg2 = jnp.where(lane == i1, -jnp.inf, lg)
    m2 = jnp.max(lg2, axis=-1, keepdims=True)
    i2 = jnp.min(jnp.where(lg2 == m2, lane, float(ROUTER_PAD)), axis=-1, keepdims=True)
    e2 = jnp.exp(m2 - m1)
    p1 = 1.0 / (1.0 + e2)
    return i1, i2, p1, e2 * p1


def _route_pack(logits, tri):
    lane = lax.broadcasted_iota(jnp.int32, logits.shape, 1).astype(F32)
    i1, i2, p1, p2 = _top2(logits, lane)
    oh1 = jnp.where(lane == i1, 1.0, 0.0)
    oh2 = jnp.where(lane == i2, 1.0, 0.0)
    cnt1 = jnp.sum(oh1, axis=0, keepdims=True)
    cnt2 = jnp.sum(oh2, axis=0, keepdims=True)
    rank1 = jnp.sum(_dot(tri, oh1.astype(BF16)) * oh1, axis=-1, keepdims=True)
    rank2 = jnp.sum((_dot(tri, oh2.astype(BF16)) + cnt1) * oh2, axis=-1, keepdims=True)
    packed = jnp.zeros_like(logits)
    for k, val in enumerate((p1, p2, i1, i2, rank1, rank2)):
        packed = jnp.where(lane == float(k), val, packed)
    return packed, cnt1 + cnt2


def _route_tables(route_t, tile_counts, tm, n_tiles):
    n = route_t.shape[1]
    tile_cnt = tile_counts[:, 0, :N_EXPERTS].astype(jnp.int32)
    cnt = jnp.sum(tile_cnt, axis=0)
    gsz = ((cnt + tm - 1) // tm) * tm
    gend = jnp.cumsum(gsz)
    seg_start = (gend - gsz)[None, :] + jnp.cumsum(tile_cnt, axis=0) - tile_cnt
    seg_of_token = jnp.repeat(seg_start.T, n // tile_cnt.shape[0], axis=1)
    experts = jnp.arange(N_EXPERTS, dtype=jnp.int32)[:, None]
    pos = []
    for k in range(2):
        e_k = route_t[2 + k].astype(jnp.int32)
        rank_k = route_t[4 + k].astype(jnp.int32)
        pos.append(jnp.sum(jnp.where(e_k[None, :] == experts, seg_of_token, 0), axis=0) + rank_k)
    pos = jnp.concatenate(pos)
    tile_start = jnp.arange(n_tiles, dtype=jnp.int32) * tm
    tile_expert = jnp.sum((tile_start[:, None] >= gend[None, :]).astype(jnp.int32), axis=1)
    tile_expert = jnp.minimum(tile_expert, N_EXPERTS - 1)
    live = (gend[-1] // tm).reshape(1)
    return pos, tile_expert, live, gend - gsz + cnt, gend


def _row_copy_wait(src_hbm, dst, sem, rows):
    pltpu.make_async_copy(src_hbm.at[pl.ds(0, rows)], dst, sem).wait()


def _moe_dispatch_kernel(pos_ref, pad_lo_ref, pad_hi_ref, live_ref, h_ref, xs_hbm, rows, zbuf, rsem, zsem):
    i = pl.program_id(0)
    steps = pl.num_programs(0)
    tt = h_ref.shape[0]
    n = tt * steps
    tm = zbuf.shape[0]
    slot = i % 2

    def wait_rows(sl):
        for _ in range(2):
            pltpu.make_async_copy(rows.at[sl], xs_hbm.at[pl.ds(0, tt)], rsem.at[sl]).wait()

    @pl.when(i >= 2)
    def _():
        wait_rows(slot)

    rows[slot] = h_ref[...].astype(F32)
    base = i * tt
    for r in range(tt):
        row = rows.at[slot, pl.ds(r, 1)]
        pltpu.make_async_copy(row, xs_hbm.at[pl.ds(pos_ref[base + r], 1)], rsem.at[slot]).start(priority=0)
        pltpu.make_async_copy(row, xs_hbm.at[pl.ds(pos_ref[n + base + r], 1)], rsem.at[slot]).start(priority=1)

    @pl.when(i == steps - 1)
    def _():
        if steps > 1:
            wait_rows(1 - slot)
        wait_rows(slot)
        zbuf[...] = jnp.zeros(zbuf.shape, F32)

        def fill_row(p):
            return pltpu.make_async_copy(zbuf.at[pl.ds(0, 1)], xs_hbm.at[pl.ds(p, 1)], zsem)

        def fill_tile(t):
            return pltpu.make_async_copy(zbuf, xs_hbm.at[pl.ds(pl.multiple_of(t * tm, tm), tm)], zsem)

        for e in range(N_EXPERTS):
            lo, hi = pad_lo_ref[e], pad_hi_ref[e]
            lax.fori_loop(lo, hi, lambda p, c: (fill_row(p).start(), c)[1], 0)
            lax.fori_loop(lo, hi, lambda p, c: (fill_row(p).wait(), c)[1], 0)
        lo, hi = live_ref[0], xs_hbm.shape[0] // tm
        lax.fori_loop(lo, hi, lambda t, c: (fill_tile(t).start(), c)[1], 0)
        lax.fori_loop(lo, hi, lambda t, c: (fill_tile(t).wait(), c)[1], 0)


def _moe_dispatch(h2, pos, pad_lo, pad_hi, live, *, tt, tm, n_tiles):
    n = h2.shape[0]
    grid_spec = pltpu.PrefetchScalarGridSpec(
        num_scalar_prefetch=4,
        grid=(n // tt,),
        in_specs=[pl.BlockSpec((tt, D_MODEL), lambda i, *_: (i, 0))],
        out_specs=pl.BlockSpec(memory_space=pl.ANY),
        scratch_shapes=[pltpu.VMEM((2, tt, D_MODEL), F32), pltpu.VMEM((tm, D_MODEL), F32),
                        pltpu.SemaphoreType.DMA((2,)), pltpu.SemaphoreType.DMA(())],
    )
    return pl.pallas_call(
        _moe_dispatch_kernel,
        grid_spec=grid_spec,
        out_shape=jax.ShapeDtypeStruct((n_tiles * tm, D_MODEL), F32),
        compiler_params=_params(("arbitrary",)),
        name="moe_dispatch",
    )(pos, pad_lo, pad_hi, live, h2)


def _moe_expert_kernel(te_ref, live_ref, x_ref, wg_ref, wu_ref, wd_ref, y_ref):
    del te_ref
    is_live = pl.program_id(0) < live_ref[0]

    @pl.when(is_live)
    def _():
        y_ref[...] = _swiglu(x_ref[...].astype(BF16), wg_ref, wu_ref, wd_ref)

    @pl.when(jnp.logical_not(is_live))
    def _():
        y_ref[...] = jnp.zeros(y_ref.shape, F32)


def _moe_experts(xs, tile_expert, live, wg, wu, wd, *, tm, n_tiles):
    fe = D_FF_EXPERT

    def in_tile(t, te, live):
        return (jnp.minimum(t, live[0] - 1), 0)

    def out_tile(t, te, live):
        return (t, 0)

    def expert(t, te, live):
        return (te[t], 0, 0)

    grid_spec = pltpu.PrefetchScalarGridSpec(
        num_scalar_prefetch=2,
        grid=(n_tiles,),
        in_specs=[
            pl.BlockSpec((tm, D_MODEL), in_tile),
            pl.BlockSpec((None, D_MODEL, fe), expert),
            pl.BlockSpec((None, D_MODEL, fe), expert),
            pl.BlockSpec((None, fe, D_MODEL), expert),
        ],
        out_specs=pl.BlockSpec((tm, D_MODEL), out_tile),
    )
    return pl.pallas_call(
        _moe_expert_kernel,
        grid_spec=grid_spec,
        out_shape=jax.ShapeDtypeStruct(xs.shape, F32),
        compiler_params=_params(("arbitrary",)),
        name="moe_experts",
    )(tile_expert, live, xs, wg, wu, wd)


def _moe_combine_kernel(pos_ref, x_ref, g2_ref, r_ref, ys_hbm, o_ref, buf, sem):
    i = pl.program_id(0)
    nt = pl.num_programs(0)
    tt = x_ref.shape[0]
    n = nt * tt

    def fetch(tile, sl):
        base = tile * tt
        for r in range(tt):
            for s in range(2):
                row = pos_ref[s * n + base + r]
                pltpu.make_async_copy(ys_hbm.at[pl.ds(row, 1)], buf.at[sl, pl.ds(s * tt + r, 1)],
                                      sem.at[sl]).start(priority=s)

    @pl.when(i == 0)
    def _():
        fetch(0, 0)

    @pl.when(i + 1 < nt)
    def _():
        fetch(i + 1, (i + 1) % 2)

    sl = i % 2
    _row_copy_wait(ys_hbm, buf.at[sl], sem.at[sl], 2 * tt)
    r = r_ref[...]
    f = r[:, 0:1] * buf[sl, pl.ds(0, tt), :] + r[:, 1:2] * buf[sl, pl.ds(tt, tt), :]
    o_ref[...] = x_ref[...] + g2_ref[...] * f


def _moe_combine(x1, g2, route, pos, ys, *, tt):
    n = x1.shape[0]
    grid_spec = pltpu.PrefetchScalarGridSpec(
        num_scalar_prefetch=1,
        grid=(n // tt,),
        in_specs=[
            pl.BlockSpec((tt, D_MODEL), lambda i, pos: (i, 0)),
            _mod_spec(tt, n, g2),
            pl.BlockSpec((tt, ROUTER_PAD), lambda i, pos: (i, 0)),
            pl.BlockSpec(memory_space=pl.ANY),
        ],
        out_specs=pl.BlockSpec((tt, D_MODEL), lambda i, pos: (i, 0)),
        scratch_shapes=[pltpu.VMEM((2, 2 * tt, D_MODEL), F32), pltpu.SemaphoreType.DMA((2,))],
    )
    return pl.pallas_call(
        _moe_combine_kernel,
        grid_spec=grid_spec,
        out_shape=jax.ShapeDtypeStruct((n, D_MODEL), F32),
        compiler_params=_params(("arbitrary",)),
        name="moe_combine",
    )(pos, x1, g2.arr, route, ys)


def _moe_routed(h2, x1, g2, route, route_t, tile_counts, wg, wu, wd, *, tm, tt):
    n = h2.shape[0]
    n_tiles = (2 * n) // tm + N_EXPERTS
    pos, tile_expert, live, pad_lo, pad_hi = _route_tables(route_t, tile_counts, tm, n_tiles)
    xs = _moe_dispatch(h2, pos, pad_lo, pad_hi, live, tt=tt, tm=tm, n_tiles=n_tiles)
    ys = _moe_experts(xs, tile_expert, live, wg, wu, wd, tm=tm, n_tiles=n_tiles)
    return _moe_combine(x1, g2, route, pos, ys, tt=tt)


def _head_perm():
    idx = []
    for g in range(GROUP):
        for kv in range(N_KV):
            h = kv * GROUP + g
            idx.extend(range(h * HEAD_DIM, (h + 1) * HEAD_DIM))
    return np.asarray(idx, np.int32)


def _relayout_w_in(w):
    pts = np.cumsum([0, Q_W, KV_W, KV_W, GK_W, GK_W, GV_W, GATE_RANK, GV_W, D_MODEL, D_MODEL])
    qa, ka, va, qb, kb, vb, ga, rb, gta, gtb = [w[:, pts[i]:pts[i + 1]] for i in range(10)]
    qa = qa[:, _head_perm()]
    ga = jnp.pad(ga, ((0, 0), (0, GATE_PAD - GATE_RANK)))
    return jnp.concatenate([qa, ka, va, qb, kb, vb, rb, gta, gtb, ga], axis=1).astype(BF16)


def kernel(x_prompt, x_sample, cache_k, cache_v, state_gla, c_prompt, c_sample, ada_w, ada_b, norm1_g, norm2_g, w_in, q_norm_g, k_norm_g, attn_sinks, gla_wa2, gla_ba, gla_norm_g, w_branch_a, w_branch_b, w_out, ffn_w_gate, ffn_w_up, ffn_w_down, router_w, router_b, moe_w_gate, moe_w_up, moe_w_down):
    n_p = BATCH * SEQ
    xp = x_prompt.reshape(n_p, D_MODEL)
    xs = x_sample.reshape(DEC_BATCH, D_MODEL)

    c_pad = -(BATCH + DEC_BATCH) % 8
    c_all = jnp.pad(jnp.concatenate([c_sample, c_prompt], axis=0), ((0, c_pad), (0, 0)))
    mod_sample, mod_prompt = _ada(c_all, ada_w, ada_b, DEC_BATCH, BATCH)
    state_rows = state_gla.reshape(DEPTH, DEC_BATCH, GK_W, GLA_DV)

    bd = jnp.asarray(np.kron(np.eye(N_HEADS), np.ones((HEAD_DIM, HEAD_DIM))), BF16)
    ut = jnp.asarray(_gla_cum_matrix(), BF16)
    gla_upper, gla_pairs = (jnp.asarray(m) for m in _gla_level_masks())
    perm = _head_perm()
    swa_bias = jnp.asarray(_swa_sample_bias())
    swa_prompt_bias = jnp.asarray(_swa_prompt_bias())

    kp_l, vp_l, sp_l, ks_l, vs_l, ss_l = [], [], [], [], [], []
    for l in range(DEPTH):
        mod_p = [_Mod(mod_prompt, l, i) for i in range(N_MOD)]
        mod_s = [_Mod(mod_sample, l, i) for i in range(N_MOD)]

        w = _relayout_w_in(w_in[l])
        qg = (jnp.tile(q_norm_g[l], N_HEADS) * (HEAD_DIM ** -0.5)).reshape(1, Q_W)
        kg = jnp.tile(k_norm_g[l], N_KV).reshape(1, KV_W)
        wa2 = jnp.pad(gla_wa2[l], ((0, GATE_PAD - GATE_RANK), (0, 0))).astype(BF16)
        ba = gla_ba[l].reshape(1, GK_W)
        n1 = norm1_g[l].reshape(1, D_MODEL)
        n2 = norm2_g[l].reshape(1, D_MODEL)
        gn = gla_norm_g[l].reshape(1, GLA_DV)
        wpa = w_branch_a[l][perm].astype(BF16)
        wpb = w_branch_b[l].astype(BF16)
        wo = w_out[l].astype(BF16)
        sink_rows = jnp.broadcast_to(attn_sinks[l][_sample_head_of_row()][:, None], (N_HEADS, LANES))

        q, k, v, gq, gk, gv, la, rbs, sga, sgb = _mixin(
            xp, mod_p[0], mod_p[1], n1, w, bd, qg, kg, wa2, ba, tm=ROW_TILE_RESIDENT)
        a_out = _swa_prompt(attn_sinks[l], q, k, v, swa_prompt_bias)
        g_out, s_fin = _gla_prompt(gq, gk, la, gv, rbs, gn, ut, gla_upper, gla_pairs)
        is_moe = l % 2 == 1
        if is_moe:
            rw = jnp.pad(router_w[l // 2], ((0, 0), (0, ROUTER_PAD - N_EXPERTS))).astype(BF16)
            rbias = jnp.pad(router_b[l // 2], (0, ROUTER_PAD - N_EXPERTS)).reshape(1, ROUTER_PAD)
        merged = _merge(xp, a_out, g_out, sga, sgb, wpa, wpb, wo, mod_p[2], mod_p[3], mod_p[4], n2,
                        tm=ROW_TILE_RESIDENT, router=(rw, rbias) if is_moe else None)
        x1, h2 = merged[0], merged[1]
        for cache, full in ((kp_l, k), (vp_l, v)):
            tail = full.reshape(BATCH, SEQ, KV_W)[:, SEQ - WINDOW:]
            cache.append(tail.reshape(BATCH, WINDOW, N_KV, HEAD_DIM))
        sp_l.append(s_fin.reshape(BATCH, GLA_HEADS, GLA_DK, GLA_DV))

        qs, ksn, vsn, gqs, gks, gvs, las, rbss, sgas, sgbs = _mixin(
            xs, mod_s[0], mod_s[1], n1, w, bd, qg, kg, wa2, ba, tm=DEC_BATCH)
        a_s, nk, nv = _swa_sample(
            qs.astype(F32).reshape(DEC_BATCH, N_HEADS, HEAD_DIM),
            ksn.reshape(DEC_BATCH, N_KV, HEAD_DIM), vsn.reshape(DEC_BATCH, N_KV, HEAD_DIM),
            cache_k[l].reshape(1, DEC_BATCH, CACHE_ROWS, HEAD_DIM),
            cache_v[l].reshape(1, DEC_BATCH, CACHE_ROWS, HEAD_DIM), swa_bias, sink_rows, layer=0)
        g_s, s_new = _gla_sample(gqs, gks, las, gvs, rbss, gn, state_rows, layer=l)
        x1s, h2s = _merge(xs, a_s.reshape(DEC_BATCH, Q_W), g_s, sgas, sgbs, wpa, wpb, wo,
                          mod_s[2], mod_s[3], mod_s[4], n2, tm=DEC_BATCH)
        ks_l.append(nk.reshape(DEC_BATCH, WINDOW, N_KV, HEAD_DIM))
        vs_l.append(nv.reshape(DEC_BATCH, WINDOW, N_KV, HEAD_DIM))
        ss_l.append(s_new.reshape(DEC_BATCH, GLA_HEADS, GLA_DK, GLA_DV))

        i = l // 2
        if not is_moe:
            wg, wu, wd = ffn_w_gate[i].astype(BF16), ffn_w_up[i].astype(BF16), ffn_w_down[i].astype(BF16)
            xp = _ffn(h2, x1, mod_p[5], wg, wu, wd, tm=ROW_TILE_SWIGLU, tf=D_FF // 2)
            xs = _ffn(h2s, x1s, mod_s[5], wg, wu, wd, tm=DEC_BATCH, tf=D_FF // 2)
        else:
            wg, wu, wd = moe_w_gate[i].astype(BF16), moe_w_up[i].astype(BF16), moe_w_down[i].astype(BF16)
            xp = _moe_routed(h2, x1, mod_p[5], *merged[2:5], wg, wu, wd,
                             tm=ROW_TILE_SWIGLU, tt=MOE_TOKEN_TILE)
            xs = _moe(h2s, x1s, mod_s[5], rw, rbias, wg, wu, wd, tm=DEC_BATCH)

    return (xp.reshape(BATCH, SEQ, D_MODEL), xs.reshape(DEC_BATCH, 1, D_MODEL),
            jnp.stack(kp_l), jnp.stack(vp_l), jnp.stack(sp_l),
            jnp.stack(ks_l), jnp.stack(vs_l), jnp.stack(ss_l))
```

```python
import functools
from typing import NamedTuple

import jax
import jax.numpy as jnp
import numpy as np
from jax import lax
from jax.experimental import pallas as pl
from jax.experimental.pallas import tpu as pltpu

D_MODEL = 1024
BATCH = 4
SEQ = 4096
DEPTH = 2
DEC_BATCH = 128
N_HEADS = 8
N_KV = 2
HEAD_DIM = 64
GROUP = N_HEADS // N_KV
WINDOW = 128
GLA_HEADS = 4
GLA_DK = 64
GLA_DV = 128
GATE_RANK = 16
GATE_TAU = 16.0
D_FF = 2816
N_EXPERTS = 8
D_FF_EXPERT = 1408
EPS = 1e-6

Q_W = N_HEADS * HEAD_DIM
KV_W = N_KV * HEAD_DIM
GK_W = GLA_HEADS * GLA_DK
GV_W = GLA_HEADS * GLA_DV

LANES = 128
GATE_PAD = LANES
ROUTER_PAD = LANES
VMEM_LIMIT = 56 * 1024 * 1024

ROW_TILE_RESIDENT = 1024
ROW_TILE_SWIGLU = 512
MOE_TOKEN_TILE = 512

F32 = jnp.float32
BF16 = jnp.bfloat16

_C_Q = 0
_C_K = _C_Q + Q_W
_C_V = _C_K + KV_W
_C_GQ = _C_V + KV_W
_C_GK = _C_GQ + GK_W
_C_GV = _C_GK + GK_W
_C_RB = _C_GV + GV_W
_C_GA = _C_RB + GV_W
_C_GB = _C_GA + D_MODEL
_C_LR = _C_GB + D_MODEL
PROJ_PAD = _C_LR + GATE_PAD

GLA_CHUNK = 128
GLA_LEVELS = 7
GLA_MXU_LEVELS = 3
GLA_CHUNKS_PER_STEP = 4


def _params(sem, vmem=VMEM_LIMIT):
    return pltpu.CompilerParams(dimension_semantics=sem, vmem_limit_bytes=vmem)


def _dot(a, b):
    return jnp.dot(a, b, preferred_element_type=F32)


def _dot_nt(a, b):
    return lax.dot_general(a, b, (((1,), (1,)), ((), ())), preferred_element_type=F32)


def _dot_tn(a, b):
    return lax.dot_general(a, b, (((0,), (0,)), ((), ())), preferred_element_type=F32)


def _sigmoid(x):
    return 0.5 * jnp.tanh(0.5 * x) + 0.5


def _silu(x):
    return x * _sigmoid(x)


N_MOD = 6


class _Mod(NamedTuple):
    arr: jax.Array
    layer: int
    idx: int


def _ada_kernel(c_ref, w_ref, b_ref, os_ref, op_ref):
    c = c_ref[...]
    mod = _dot(_silu(c).astype(BF16), w_ref[...].astype(BF16)) + b_ref[...]
    n_s = os_ref.shape[0]
    os_ref[...] = mod[0:n_s]
    for b in range(op_ref.shape[0]):
        op_ref[b] = mod[n_s + b:n_s + b + 1]


def _ada(c_all, ada_w, ada_b, n_sample, n_prompt):
    rows = c_all.shape[0]
    return pl.pallas_call(
        _ada_kernel,
        grid=(DEPTH, N_MOD),
        in_specs=[
            pl.BlockSpec((rows, D_MODEL), lambda l, j: (0, 0)),
            pl.BlockSpec((None, D_MODEL, D_MODEL), lambda l, j: (l, 0, j)),
            pl.BlockSpec((None, 1, D_MODEL), lambda l, j: (l, 0, j)),
        ],
        out_specs=[
            pl.BlockSpec((None, None, n_sample, D_MODEL), lambda l, j: (l, j, 0, 0)),
            pl.BlockSpec((None, None, n_prompt, 1, D_MODEL), lambda l, j: (l, j, 0, 0, 0)),
        ],
        out_shape=[
            jax.ShapeDtypeStruct((DEPTH, N_MOD, n_sample, D_MODEL), F32),
            jax.ShapeDtypeStruct((DEPTH, N_MOD, n_prompt, 1, D_MODEL), F32),
        ],
        compiler_params=_params(("parallel", "parallel")),
        name="ada_mod",
    )(c_all, ada_w, ada_b.reshape(DEPTH, 1, N_MOD * D_MODEL))


def _mixin_kernel(x_ref, sh_ref, sc_ref, n1_ref, w_ref, bd_ref, qg_ref, kg_ref, wa2_ref, ba_ref,
                  q_ref, k_ref, v_ref, gq_ref, gk_ref, gv_ref, la_ref, rb_ref, sga_ref, sgb_ref):
    x = x_ref[...]
    ms = jnp.mean(x * x, axis=-1, keepdims=True)
    h = x * lax.rsqrt(ms + EPS) * n1_ref[...]
    h = h * (1.0 + sc_ref[...]) + sh_ref[...]
    hb = h.astype(BF16)

    def proj(a, b):
        return _dot(hb, w_ref[:, a:b])

    q = proj(_C_Q, _C_K)
    ssq = _dot((q * q).astype(BF16), bd_ref[...])
    q_ref[...] = (q * lax.rsqrt(ssq * (1.0 / HEAD_DIM) + EPS) * qg_ref[...]).astype(BF16)
    k = proj(_C_K, _C_V)
    ssk = _dot((k * k).astype(BF16), bd_ref[0:KV_W, 0:KV_W])
    k_ref[...] = k * lax.rsqrt(ssk * (1.0 / HEAD_DIM) + EPS) * kg_ref[...]
    v_ref[...] = proj(_C_V, _C_GQ)
    gq_ref[...] = proj(_C_GQ, _C_GK) * (GLA_DK ** -0.5)
    gk_ref[...] = proj(_C_GK, _C_GV)
    gv_ref[...] = proj(_C_GV, _C_RB)
    rb_ref[...] = _silu(proj(_C_RB, _C_GA)).astype(BF16)
    sga_ref[...] = _sigmoid(proj(_C_GA, _C_GB)).astype(BF16)
    sgb_ref[...] = _sigmoid(proj(_C_GB, _C_LR)).astype(BF16)
    ga = proj(_C_LR, PROJ_PAD)
    xg = _dot(ga.astype(BF16), wa2_ref[...]) + ba_ref[...]
    la_ref[...] = (jnp.minimum(xg, 0.0) - jnp.log1p(jnp.exp(-jnp.abs(xg)))) * (1.0 / GATE_TAU)


def _mixin(x, sh, sc, n1, w, bd, qg, kg, wa2, ba, *, tm):
    n = x.shape[0]
    nt = n // tm

    def row(i):
        return (i, 0)

    def const(shape):
        return pl.BlockSpec(shape, lambda i: (0,) * len(shape), pipeline_mode=pl.Buffered(1))

    def out(width, dtype):
        return pl.BlockSpec((tm, width), row), jax.ShapeDtypeStruct((n, width), dtype)

    outs = [out(Q_W, BF16), out(KV_W, F32), out(KV_W, F32), out(GK_W, F32), out(GK_W, F32),
            out(GV_W, F32), out(GK_W, F32), out(GV_W, BF16), out(D_MODEL, BF16), out(D_MODEL, BF16)]
    return pl.pallas_call(
        _mixin_kernel,
        grid=(nt,),
        in_specs=[
            pl.BlockSpec((tm, D_MODEL), row), _mod_spec(tm, n, sh), _mod_spec(tm, n, sc), const((1, D_MODEL)),
            const((D_MODEL, PROJ_PAD)), const((Q_W, Q_W)), const((1, Q_W)), const((1, KV_W)),
            const((GATE_PAD, GK_W)), const((1, GK_W)),
        ],
        out_specs=[o[0] for o in outs],
        out_shape=[o[1] for o in outs],
        compiler_params=_params(("parallel",)),
        name="mixer_in",
    )(x, sh.arr, sc.arr, n1, w, bd, qg, kg, wa2, ba)


SWA_BLOCKS = 8


def _head_slope(h):
    return float(2.0 ** (-8.0 * (h + 1) / N_HEADS))


def _swa_prompt_bias():
    blk = WINDOW
    dist = np.arange(blk)[:, None] + blk - np.arange(2 * blk)[None, :]
    slopes = np.asarray([_head_slope(h) for h in range(N_HEADS)])[:, None, None]
    return np.where((dist >= 0) & (dist <= WINDOW), -slopes * dist, -np.inf).astype(np.float32)


def _swa_prompt_kernel(sink_ref, q_ref, kp_ref, kc_ref, vp_ref, vc_ref, bias_ref, o_ref):
    n = pl.program_id(1)
    blk = WINDOW
    kall = jnp.concatenate([kp_ref[...], kc_ref[...]], axis=0).astype(BF16)
    vall = jnp.concatenate([vp_ref[...], vc_ref[...]], axis=0).astype(BF16)
    col = lax.broadcasted_iota(jnp.int32, (blk, 2 * blk), 1)
    first_key = jnp.where(n > 0, 0, blk)
    klane = lax.broadcasted_iota(jnp.int32, kall.shape, 1)
    kall_kv = [jnp.where((klane // HEAD_DIM) == kv, kall, jnp.zeros_like(kall)) for kv in range(N_KV)]
    olane = lax.broadcasted_iota(jnp.int32, (blk, KV_W), 1)
    q = q_ref[...]
    for j in range(SWA_BLOCKS):
        vv = vall[j * blk:(j + 2) * blk]
        outs = []
        for g in range(GROUP):
            qp = q[j * blk:(j + 1) * blk, g * KV_W:(g + 1) * KV_W]
            pair = []
            for kv in range(N_KV):
                h = kv * GROUP + g
                s = _dot_nt(qp, kall_kv[kv][j * blk:(j + 2) * blk]) + bias_ref[h]
                if j == 0:
                    s = jnp.where(col >= first_key, s, -jnp.inf)
                sink = sink_ref[h]
                m = jnp.maximum(jnp.max(s, axis=-1, keepdims=True), sink)
                p = jnp.exp(s - m)
                den = jnp.sum(p, axis=-1, keepdims=True) + jnp.exp(sink - m)
                pair.append(_dot(p.astype(BF16), vv) * (1.0 / den))
            outs.append(jnp.where(olane < HEAD_DIM, pair[0], pair[1]))
        o_ref[j * blk:(j + 1) * blk, :] = jnp.concatenate(outs, axis=-1).astype(BF16)


def _swa_prompt(sinks, q, k, v, bias):
    nb = SEQ // WINDOW
    steps = nb // SWA_BLOCKS
    tq = SWA_BLOCKS * WINDOW

    def cur(b, n):
        return (b * steps + n, 0)

    def prev(b, n):
        return (b * nb + jnp.maximum(n * SWA_BLOCKS - 1, 0), 0)

    return pl.pallas_call(
        _swa_prompt_kernel,
        grid=(BATCH, steps),
        in_specs=[
            pl.BlockSpec(memory_space=pltpu.SMEM),
            pl.BlockSpec((tq, Q_W), cur),
            pl.BlockSpec((WINDOW, KV_W), prev), pl.BlockSpec((tq, KV_W), cur),
            pl.BlockSpec((WINDOW, KV_W), prev), pl.BlockSpec((tq, KV_W), cur),
            pl.BlockSpec(bias.shape, lambda b, n: (0, 0, 0), pipeline_mode=pl.Buffered(1)),
        ],
        out_specs=pl.BlockSpec((tq, Q_W), cur),
        out_shape=jax.ShapeDtypeStruct((BATCH * SEQ, Q_W), BF16),
        compiler_params=_params(("parallel", "parallel")),
        name="swa_prompt",
    )(sinks, q, k, k, v, v, bias)


SAMPLE_TB = 8
CACHE_ROWS = WINDOW * N_KV


def _sample_head_of_row():
    j = np.arange(N_HEADS)
    return (j % N_KV) * GROUP + j // N_KV


def _swa_sample_bias():
    j = np.arange(N_HEADS)[:, None]
    c = np.arange(CACHE_ROWS)[None, :]
    slope = 2.0 ** (-8.0 * (_sample_head_of_row()[:, None] + 1) / N_HEADS)
    bias = -slope * (WINDOW - c // N_KV)
    return np.where(c % N_KV == j % N_KV, bias, -np.inf).astype(np.float32)


def _swa_sample_kernel(q_ref, kn_ref, vn_ref, ck_ref, cv_ref, bias_ref, sk_ref, o_ref, ok_ref, ov_ref):
    rows = CACHE_ROWS
    sink = sk_ref[...][:, 0:1]
    q = q_ref[...]
    kn = kn_ref[...]
    vn = vn_ref[...]
    kc = ck_ref[...]
    vc = cv_ref[...]
    kn8 = jnp.concatenate([kn] * GROUP, axis=1)
    vn8 = jnp.concatenate([vn] * GROUP, axis=1)
    s = lax.dot_general(q.astype(BF16), kc.astype(BF16), (((2,), (2,)), ((0,), (0,))),
                        preferred_element_type=F32) + bias_ref[...]
    s_new = jnp.sum(q * kn8, axis=-1, keepdims=True)
    m = jnp.maximum(jnp.maximum(jnp.max(s, axis=-1, keepdims=True), s_new), sink)
    p = jnp.exp(s - m)
    p_new = jnp.exp(s_new - m)
    den = jnp.sum(p, axis=-1, keepdims=True) + p_new + jnp.exp(sink - m)
    o = lax.dot_general(p.astype(BF16), vc.astype(BF16), (((2,), (1,)), ((0,), (0,))),
                        preferred_element_type=F32) + p_new * vn8
    o_ref[...] = o * (1.0 / den)
    ok_ref[:, pl.ds(0, rows - N_KV), :] = kc[:, N_KV:, :]
    ok_ref[:, pl.ds(rows - N_KV, N_KV), :] = kn
    ov_ref[:, pl.ds(0, rows - N_KV), :] = vc[:, N_KV:, :]
    ov_ref[:, pl.ds(rows - N_KV, N_KV), :] = vn


def _swa_sample(q, kn, vn, ck, cv, bias, sinks, *, layer):
    tb = 2 * SAMPLE_TB
    nb = DEC_BATCH // tb

    def const(shape):
        return pl.BlockSpec(shape, lambda i: (0,) * len(shape))

    def per_seq(*dims):
        return pl.BlockSpec((tb,) + dims, lambda i: (i, 0, 0))

    cache_spec = pl.BlockSpec((None, tb, CACHE_ROWS, HEAD_DIM), lambda i: (layer, i, 0, 0))
    return pl.pallas_call(
        _swa_sample_kernel,
        grid=(nb,),
        in_specs=[per_seq(N_HEADS, HEAD_DIM), per_seq(N_KV, HEAD_DIM), per_seq(N_KV, HEAD_DIM),
                  cache_spec, cache_spec,
                  const((N_HEADS, CACHE_ROWS)), const((N_HEADS, LANES))],
        out_specs=[per_seq(N_HEADS, HEAD_DIM), per_seq(CACHE_ROWS, HEAD_DIM), per_seq(CACHE_ROWS, HEAD_DIM)],
        out_shape=[
            jax.ShapeDtypeStruct((DEC_BATCH, N_HEADS, HEAD_DIM), F32),
            jax.ShapeDtypeStruct((DEC_BATCH, CACHE_ROWS, HEAD_DIM), F32),
            jax.ShapeDtypeStruct((DEC_BATCH, CACHE_ROWS, HEAD_DIM), F32),
        ],
        compiler_params=_params(("parallel",)),
        name="swa_sample",
    )(q, kn, vn, ck, cv, bias, sinks)


def _gla_cum_matrix():
    c = GLA_CHUNK
    tri = np.tril(np.ones((c, c), np.float32))
    i = np.arange(c)
    blocks = []
    for lvl in range(GLA_MXU_LEVELS):
        half = 1 << lvl
        mid = (i // (2 * half)) * (2 * half) + half - 1
        blocks.append(tri - tri[mid])
    blocks.append(tri)
    return np.concatenate(blocks, axis=0)


def _split3(x):
    hi = x.astype(BF16)
    r1 = x - hi.astype(F32)
    mid = r1.astype(BF16)
    lo = (r1 - mid.astype(F32)).astype(BF16)
    return hi, mid, lo


def _gla_level_masks():
    cl = GLA_CHUNK
    r = np.arange(cl)
    upper = np.stack([np.broadcast_to(((r >> lvl) & 1)[:, None], (cl, GK_W)) for lvl in range(GLA_LEVELS)])
    ri = np.tile(r, GLA_HEADS)[:, None]
    pairs = [(ri >> (lvl + 1)) == (r[None, :] >> (lvl + 1)) for lvl in range(GLA_LEVELS)]
    pairs.append(ri == r[None, :])
    return upper.astype(np.float32), np.stack(pairs).astype(np.float32)


def _gla_prompt_kernel(q_ref, k_ref, la_ref, v_ref, rb_ref, gn_ref, ut_ref, up_ref, pm_ref,
                       o_ref, s_ref, st_ref):
    c = pl.program_id(1)
    cl = GLA_CHUNK

    @pl.when(c == 0)
    def _():
        st_ref[0] = jnp.zeros(st_ref.shape[1:], F32)

    ut = ut_ref[...]
    lane = lax.broadcasted_iota(jnp.int32, (cl, GK_W), 1)
    head_of_lane = lane // GLA_DK
    ones = jnp.ones((cl, LANES), BF16)
    state = [st_ref[c % 2, h] for h in range(GLA_HEADS)]
    for cc in range(GLA_CHUNKS_PER_STEP):
        rows = slice(cc * cl, (cc + 1) * cl)
        hi, mid, lo = _split3(la_ref[rows, :])
        tall = _dot(ut, hi) + _dot(ut, mid) + _dot(ut, lo)
        q = q_ref[rows, :]
        k = k_ref[rows, :]
        attn_all = jnp.zeros((GLA_HEADS * cl, cl), F32)
        cum = tall[GLA_MXU_LEVELS * cl:(GLA_MXU_LEVELS + 1) * cl]
        for lvl in range(GLA_LEVELS + 1):
            if lvl < GLA_LEVELS:
                if lvl < GLA_MXU_LEVELS:
                    t_lvl = tall[lvl * cl:(lvl + 1) * cl]
                else:
                    half = 1 << lvl
                    mids = [jnp.broadcast_to(cum[b0 + half - 1:b0 + half, :], (2 * half, GK_W))
                            for b0 in range(0, cl, 2 * half)]
                    t_lvl = cum - (mids[0] if len(mids) == 1 else jnp.concatenate(mids, axis=0))
                e = jnp.exp(-jnp.abs(t_lvl))
                e_up = e * up_ref[lvl]
                qt = (q * e_up).astype(BF16)
                kt = (k * (e - e_up)).astype(BF16)
            else:
                qt = q.astype(BF16)
                kt = k.astype(BF16)
            q_heads = jnp.concatenate(
                [jnp.where(head_of_lane == h, qt, jnp.zeros_like(qt)) for h in range(GLA_HEADS)], axis=0)
            attn_all = attn_all + _dot_nt(q_heads, kt) * pm_ref[lvl]

        last = cum[cl - 1:cl, :]
        qe = (q * jnp.exp(cum)).astype(BF16)
        kd = (k * jnp.exp(last - cum)).astype(BF16)
        dec = jnp.exp(_dot_tn(hi, ones) + _dot_tn(mid, ones) + _dot_tn(lo, ones))
        v = v_ref[rows, :]
        rb = rb_ref[rows, :]
        outs = []
        for h in range(GLA_HEADS):
            vh = v[:, h * GLA_DV:(h + 1) * GLA_DV].astype(BF16)
            o = _dot(attn_all[h * cl:(h + 1) * cl].astype(BF16), vh) + _dot(qe, state[h].astype(BF16))
            kdh = jnp.where(head_of_lane == h, kd, jnp.zeros_like(kd))
            state[h] = state[h] * dec + _dot_tn(kdh, vh)
            ms = jnp.mean(o * o, axis=-1, keepdims=True)
            g = o * lax.rsqrt(ms + EPS) * gn_ref[...]
            outs.append(g * rb[:, h * GLA_DV:(h + 1) * GLA_DV].astype(F32))
        o_ref[rows, :] = jnp.concatenate(outs, axis=-1).astype(BF16)

    for h in range(GLA_HEADS):
        st_ref[(c + 1) % 2, h] = state[h]

    @pl.when(c == pl.num_programs(1) - 1)
    def _():
        fin = (SEQ // (GLA_CHUNK * GLA_CHUNKS_PER_STEP)) % 2
        s_ref[...] = st_ref[fin, 0] + st_ref[fin, 1] + st_ref[fin, 2] + st_ref[fin, 3]


def _gla_prompt(gq, gk, la, gv, rbs, gn, ut, upper, pairs):
    cl = GLA_CHUNK * GLA_CHUNKS_PER_STEP
    nc = SEQ // cl

    def row(b, c):
        return (b * nc + c, 0)

    def const(shape):
        return pl.BlockSpec(shape, lambda b, c: (0,) * len(shape), pipeline_mode=pl.Buffered(1))

    return pl.pallas_call(
        _gla_prompt_kernel,
        grid=(BATCH, nc),
        in_specs=[
            pl.BlockSpec((cl, GK_W), row), pl.BlockSpec((cl, GK_W), row), pl.BlockSpec((cl, GK_W), row),
            pl.BlockSpec((cl, GV_W), row), pl.BlockSpec((cl, GV_W), row),
            const((1, GLA_DV)), const(ut.shape), const(upper.shape), const(pairs.shape),
        ],
        out_specs=[
            pl.BlockSpec((cl, GV_W), row),
            pl.BlockSpec((None, GK_W, GLA_DV), lambda b, c: (b, 0, 0)),
        ],
        out_shape=[
            jax.ShapeDtypeStruct((BATCH * SEQ, GV_W), BF16),
            jax.ShapeDtypeStruct((BATCH, GK_W, GLA_DV), F32),
        ],
        scratch_shapes=[pltpu.VMEM((2, GLA_HEADS, GK_W, GLA_DV), F32)],
        compiler_params=_params(("parallel", "arbitrary")),
        name="gla_prompt",
    )(gq, gk, la, gv, rbs, gn, ut, upper, pairs)


def _gla_sample_kernel(q_ref, k_ref, la_ref, v_ref, rb_ref, gn_ref, s_ref, o_ref, so_ref):
    tb = SAMPLE_TB
    dec = jnp.exp(la_ref[...])
    pieces = []
    for x in (dec, k_ref[...], q_ref[...]):
        hi, mid, lo = _split3(x)
        stacked = jnp.concatenate(
            [hi.astype(F32), mid.astype(F32), lo.astype(F32), jnp.zeros_like(x)], axis=0)
        pieces.append(stacked.astype(BF16))
    prow = lax.broadcasted_iota(jnp.int32, (4 * tb, LANES), 0)
    rb = rb_ref[...].astype(F32)
    v = v_ref[...]
    for bi in range(tb):
        sel = jnp.where((prow % tb) == bi, 1.0, 0.0).astype(BF16)
        a_col, k_col, q_col = [_dot_tn(p, sel) for p in pieces]
        for h in range(GLA_HEADS):
            rs = slice(h * GLA_DK, (h + 1) * GLA_DK)
            vs = slice(h * GLA_DV, (h + 1) * GLA_DV)
            s_new = a_col[rs] * s_ref[bi, rs, :] + k_col[rs] * v[bi:bi + 1, vs]
            so_ref[bi, rs, :] = s_new
            o = jnp.sum(q_col[rs] * s_new, axis=0, keepdims=True)
            ms = jnp.mean(o * o, axis=-1, keepdims=True)
            g = o * lax.rsqrt(ms + EPS) * gn_ref[...]
            o_ref[bi:bi + 1, vs] = g * rb[bi:bi + 1, vs]


def _gla_sample_alias_kernel(q_ref, k_ref, la_ref, v_ref, rb_ref, gn_ref, s_ref, stacked_ref, o_ref, so_ref):
    del stacked_ref
    _gla_sample_kernel(q_ref, k_ref, la_ref, v_ref, rb_ref, gn_ref, s_ref, o_ref, so_ref)


def _gla_sample_fill_kernel(*refs, layer):
    so_ref = refs[-1]
    d = pl.program_id(1)

    @pl.when(d == layer)
    def _():
        _gla_sample_kernel(*refs)

    @pl.when(d != layer)
    def _():
        so_ref[...] = jnp.zeros(so_ref.shape, F32)


def _gla_sample(gq, gk, la, gv, rbs, gn, state, new_states, *, layer):
    tb = SAMPLE_TB
    nb = DEC_BATCH // tb
    first = new_states is None

    def row(w):
        return pl.BlockSpec((tb, w), lambda i, *_: (i, 0))

    st_in_spec = pl.BlockSpec((None, tb, GK_W, GLA_DV), lambda i, *_: (layer, i, 0, 0))
    if first:
        st_out_spec = pl.BlockSpec((None, tb, GK_W, GLA_DV), lambda i, d: (d, i, 0, 0))
    else:
        st_out_spec = st_in_spec
    in_specs = [row(GK_W), row(GK_W), row(GK_W), row(GV_W), row(GV_W),
                pl.BlockSpec((1, GLA_DV), lambda i, *_: (0, 0)), st_in_spec]
    args = [gq, gk, la, gv, rbs, gn, state]
    if not first:
        in_specs.append(pl.BlockSpec(memory_space=pl.ANY))
        args.append(new_states)
    return pl.pallas_call(
        functools.partial(_gla_sample_fill_kernel, layer=layer) if first else _gla_sample_alias_kernel,
        grid=(nb, DEPTH) if first else (nb,),
        in_specs=in_specs,
        out_specs=[row(GV_W), st_out_spec],
        out_shape=[
            jax.ShapeDtypeStruct((DEC_BATCH, GV_W), F32),
            jax.ShapeDtypeStruct((DEPTH, DEC_BATCH, GK_W, GLA_DV), F32),
        ],
        input_output_aliases={} if first else {len(args) - 1: 1},
        compiler_params=_params(("parallel", "arbitrary") if first else ("parallel",)),
        name="gla_sample",
    )(*args)


def _merge_kernel(x_ref, a_ref, g_ref, sga_ref, sgb_ref, wpa_ref, wpb_ref, wo_ref,
                  g1_ref, sh_ref, sc_ref, n2_ref, *rest):
    with_router = len(rest) == 8
    x1_ref, h2_ref = rest[3:5] if with_router else rest
    ya = _dot(a_ref[...].astype(BF16), wpa_ref[...])
    yb = _dot(g_ref[...].astype(BF16), wpb_ref[...])
    merged = sga_ref[...].astype(F32) * ya + sgb_ref[...].astype(F32) * yb
    mix = _dot(merged.astype(BF16), wo_ref[...])
    x1 = x_ref[...] + g1_ref[...] * mix
    x1_ref[...] = x1
    ms = jnp.mean(x1 * x1, axis=-1, keepdims=True)
    h = x1 * lax.rsqrt(ms + EPS) * n2_ref[...]
    h2 = h * (1.0 + sc_ref[...]) + sh_ref[...]
    h2_ref[...] = h2.astype(h2_ref.dtype)
    if with_router:
        rw_ref, rbias_ref, tri_ref = rest[0:3]
        route_ref, route_t_ref, cnt_ref = rest[5:8]
        logits = _dot(h2.astype(BF16), rw_ref[...]) + rbias_ref[...]
        packed, counts = _route_pack(logits, tri_ref[...])
        route_ref[...] = packed
        route_t_ref[...] = packed.T[0:route_t_ref.shape[0], :]
        cnt_ref[...] = jnp.broadcast_to(counts, cnt_ref.shape)


def _mod_spec(tm, n, mod):
    if mod.arr.ndim == 4:
        return pl.BlockSpec((None, None, tm, D_MODEL), lambda i, *_: (mod.layer, mod.idx, i, 0))
    tiles_per_seq = (n // mod.arr.shape[2]) // tm
    return pl.BlockSpec((None, None, None, 1, D_MODEL),
                        lambda i, *_: (mod.layer, mod.idx, i // tiles_per_seq, 0, 0))


def _merge(x, a, g, sga, sgb, wpa, wpb, wo, g1, sh2, sc2, n2, *, tm, router=None):
    n = x.shape[0]

    def row(w):
        return pl.BlockSpec((tm, w), lambda i: (i, 0))

    def const(shape):
        return pl.BlockSpec(shape, lambda i: (0,) * len(shape))

    in_specs = [row(D_MODEL), row(a.shape[1]), row(GV_W), row(D_MODEL), row(D_MODEL),
                const(wpa.shape), const(wpb.shape), const(wo.shape),
                _mod_spec(tm, n, g1), _mod_spec(tm, n, sh2), _mod_spec(tm, n, sc2), const((1, D_MODEL))]
    out_specs = [row(D_MODEL), row(D_MODEL)]
    out_shape = [jax.ShapeDtypeStruct((n, D_MODEL), F32), jax.ShapeDtypeStruct((n, D_MODEL), BF16)]
    args = [x, a, g, sga, sgb, wpa, wpb, wo, g1.arr, sh2.arr, sc2.arr, n2]
    if router is not None:
        tri = jnp.asarray(np.tril(np.ones((tm, tm), np.float32), -1), BF16)
        in_specs += [const(router[0].shape), const(router[1].shape), const((tm, tm))]
        out_specs += [row(ROUTER_PAD), pl.BlockSpec((8, tm), lambda i: (0, i)),
                      pl.BlockSpec((None, 8, ROUTER_PAD), lambda i: (i, 0, 0))]
        out_shape += [jax.ShapeDtypeStruct((n, ROUTER_PAD), F32), jax.ShapeDtypeStruct((8, n), F32),
                      jax.ShapeDtypeStruct((n // tm, 8, ROUTER_PAD), F32)]
        args += [router[0], router[1], tri]
    return pl.pallas_call(
        _merge_kernel,
        grid=(n // tm,),
        in_specs=in_specs,
        out_specs=out_specs,
        out_shape=out_shape,
        compiler_params=_params(("parallel",)),
        name="merge_out",
    )(*args)


def _swiglu(hb, wg_ref, wu_ref, wd_ref):
    act = (_silu(_dot(hb, wg_ref[...])) * _dot(hb, wu_ref[...])).astype(BF16)
    return _dot(act, wd_ref[...])


def _ffn_kernel(h_ref, x_ref, g2_ref, wg_ref, wu_ref, wd_ref, o_ref, acc_ref):
    f = pl.program_id(1)
    y = _swiglu(h_ref[...], wg_ref, wu_ref, wd_ref)

    @pl.when(f == 0)
    def _():
        acc_ref[...] = y

    @pl.when(f > 0)
    def _():
        acc_ref[...] += y

    @pl.when(f == pl.num_programs(1) - 1)
    def _():
        o_ref[...] = x_ref[...] + g2_ref[...] * acc_ref[...]


def _ffn(h2, x1, g2, wg, wu, wd, *, tm, tf):
    n = h2.shape[0]
    mod_spec = _mod_spec(tm, n, g2)
    return pl.pallas_call(
        _ffn_kernel,
        grid=(n // tm, D_FF // tf),
        in_specs=[
            pl.BlockSpec((tm, D_MODEL), lambda i, f: (i, 0)),
            pl.BlockSpec((tm, D_MODEL), lambda i, f: (i, 0)),
            mod_spec,
            pl.BlockSpec((D_MODEL, tf), lambda i, f: (0, f)),
            pl.BlockSpec((D_MODEL, tf), lambda i, f: (0, f)),
            pl.BlockSpec((tf, D_MODEL), lambda i, f: (f, 0)),
        ],
        out_specs=pl.BlockSpec((tm, D_MODEL), lambda i, f: (i, 0)),
        out_shape=jax.ShapeDtypeStruct((n, D_MODEL), F32),
        scratch_shapes=[pltpu.VMEM((tm, D_MODEL), F32)],
        compiler_params=_params(("parallel", "arbitrary")),
        name="ffn_dense",
    )(h2, x1, g2.arr, wg, wu, wd)


def _moe_kernel(h_ref, x_ref, g2_ref, rw_ref, rbias_ref, wg_ref, wu_ref, wd_ref, o_ref, acc_ref, gate_ref):
    e = pl.program_id(1)
    hb = h_ref[...]
    tm = hb.shape[0]
    lane = lax.broadcasted_iota(jnp.int32, (tm, ROUTER_PAD), 1).astype(F32)

    @pl.when(e == 0)
    def _():
        logits = _dot(hb, rw_ref[...]) + rbias_ref[...]
        i1, i2, p1, p2 = _top2(logits, lane)
        gate_ref[...] = jnp.where(lane == i1, p1, 0.0) + jnp.where(lane == i2, p2, 0.0)
        acc_ref[...] = jnp.zeros_like(acc_ref)

    ge = jnp.sum(jnp.where(lane == e.astype(F32), gate_ref[...], 0.0), axis=-1, keepdims=True)
    act = (_silu(_dot(hb, wg_ref[...])) * _dot(hb, wu_ref[...])).astype(BF16)
    acc_ref[...] += ge * _dot(act, wd_ref[...])

    @pl.when(e == pl.num_programs(1) - 1)
    def _():
        o_ref[...] = x_ref[...] + g2_ref[...] * acc_ref[...]


def _moe(h2, x1, g2, rw, rbias, wg, wu, wd, *, tm):
    n = h2.shape[0]
    mod_spec = _mod_spec(tm, n, g2)
    fe = D_FF_EXPERT
    return pl.pallas_call(
        _moe_kernel,
        grid=(n // tm, N_EXPERTS),
        in_specs=[
            pl.BlockSpec((tm, D_MODEL), lambda i, e: (i, 0)),
            pl.BlockSpec((tm, D_MODEL), lambda i, e: (i, 0)),
            mod_spec,
            pl.BlockSpec((D_MODEL, ROUTER_PAD), lambda i, e: (0, 0)),
            pl.BlockSpec((1, ROUTER_PAD), lambda i, e: (0, 0)),
            pl.BlockSpec((None, D_MODEL, fe), lambda i, e: (e, 0, 0)),
            pl.BlockSpec((None, D_MODEL, fe), lambda i, e: (e, 0, 0)),
            pl.BlockSpec((None, fe, D_MODEL), lambda i, e: (e, 0, 0)),
        ],
        out_specs=pl.BlockSpec((tm, D_MODEL), lambda i, e: (i, 0)),
        out_shape=jax.ShapeDtypeStruct((n, D_MODEL), F32),
        scratch_shapes=[pltpu.VMEM((tm, D_MODEL), F32), pltpu.VMEM((tm, ROUTER_PAD), F32)],
        compiler_params=_params(("parallel", "arbitrary")),
        name="moe",
    )(h2, x1, g2.arr, rw, rbias, wg, wu, wd)


def _top2(logits, lane):
    lg = jnp.where(lane < N_EXPERTS, logits, -jnp.inf)
    m1 = jnp.max(lg, axis=-1, keepdims=True)
    i1 = jnp.min(jnp.where(lg == m1, lane, float(ROUTER_PAD)), axis=-1, keepdims=True)
    lg2 = jnp.where(lane == i1, -jnp.inf, lg)
    m2 = jnp.max(lg2, axis=-1, keepdims=True)
    i2 = jnp.min(jnp.where(lg2 == m2, lane, float(ROUTER_PAD)), axis=-1, keepdims=True)
    e2 = jnp.exp(m2 - m1)
    p1 = 1.0 / (1.0 + e2)
    return i1, i2, p1, e2 * p1


def _route_pack(logits, tri):
    lane = lax.broadcasted_iota(jnp.int32, logits.shape, 1).astype(F32)
    i1, i2, p1, p2 = _top2(logits, lane)
    oh1 = jnp.where(lane == i1, 1.0, 0.0)
    oh2 = jnp.where(lane == i2, 1.0, 0.0)
    cnt1 = jnp.sum(oh1, axis=0, keepdims=True)
    cnt2 = jnp.sum(oh2, axis=0, keepdims=True)
    rank1 = jnp.sum(_dot(tri, oh1.astype(BF16)) * oh1, axis=-1, keepdims=True)
    rank2 = jnp.sum((_dot(tri, oh2.astype(BF16)) + cnt1) * oh2, axis=-1, keepdims=True)
    packed = jnp.zeros_like(logits)
    for k, val in enumerate((p1, p2, i1, i2, rank1, rank2)):
        packed = jnp.where(lane == float(k), val, packed)
    return packed, cnt1 + cnt2


def _route_tables(route_t, tile_counts, tm, n_tiles):
    n = route_t.shape[1]
    tile_cnt = tile_counts[:, 0, :N_EXPERTS].astype(jnp.int32)
    cnt = jnp.sum(tile_cnt, axis=0)
    gsz = ((cnt + tm - 1) // tm) * tm
    gend = jnp.cumsum(gsz)
    seg_start = (gend - gsz)[None, :] + jnp.cumsum(tile_cnt, axis=0) - tile_cnt
    seg_of_token = jnp.repeat(seg_start.T, n // tile_cnt.shape[0], axis=1)
    experts = jnp.arange(N_EXPERTS, dtype=jnp.int32)[:, None]
    pos = []
    for k in range(2):
        e_k = route_t[2 + k].astype(jnp.int32)
        rank_k = route_t[4 + k].astype(jnp.int32)
        pos.append(jnp.sum(jnp.where(e_k[None, :] == experts, seg_of_token, 0), axis=0) + rank_k)
    pos = jnp.concatenate(pos)
    tile_start = jnp.arange(n_tiles, dtype=jnp.int32) * tm
    tile_expert = jnp.sum((tile_start[:, None] >= gend[None, :]).astype(jnp.int32), axis=1)
    tile_expert = jnp.minimum(tile_expert, N_EXPERTS - 1)
    live = (gend[-1] // tm).reshape(1)
    return pos, tile_expert, live, gend - gsz + cnt, gend


def _row_copy_wait(src_hbm, dst, sem, rows):
    pltpu.make_async_copy(src_hbm.at[pl.ds(0, rows)], dst, sem).wait()


def _moe_dispatch_kernel(pos_ref, pad_lo_ref, pad_hi_ref, live_ref, h_ref, xs_hbm, rows, zbuf, rsem, zsem):
    i = pl.program_id(0)
    steps = pl.num_programs(0)
    tt = h_ref.shape[0]
    n = tt * steps
    tm = zbuf.shape[0]
    slot = i % 2

    def wait_rows(sl):
        for _ in range(2):
            pltpu.make_async_copy(rows.at[sl], xs_hbm.at[pl.ds(0, tt)], rsem.at[sl]).wait()

    @pl.when(i >= 2)
    def _():
        wait_rows(slot)

    rows[slot] = h_ref[...].astype(F32)
    base = i * tt
    for r in range(tt):
        row = rows.at[slot, pl.ds(r, 1)]
        pltpu.make_async_copy(row, xs_hbm.at[pl.ds(pos_ref[base + r], 1)], rsem.at[slot]).start(priority=0)
        pltpu.make_async_copy(row, xs_hbm.at[pl.ds(pos_ref[n + base + r], 1)], rsem.at[slot]).start(priority=1)

    @pl.when(i == steps - 1)
    def _():
        if steps > 1:
            wait_rows(1 - slot)
        wait_rows(slot)
        zbuf[...] = jnp.zeros(zbuf.shape, F32)

        def fill_row(p):
            return pltpu.make_async_copy(zbuf.at[pl.ds(0, 1)], xs_hbm.at[pl.ds(p, 1)], zsem)

        def fill_tile(t):
            return pltpu.make_async_copy(zbuf, xs_hbm.at[pl.ds(pl.multiple_of(t * tm, tm), tm)], zsem)

        for e in range(N_EXPERTS):
            lo, hi = pad_lo_ref[e], pad_hi_ref[e]
            lax.fori_loop(lo, hi, lambda p, c: (fill_row(p).start(), c)[1], 0)
            lax.fori_loop(lo, hi, lambda p, c: (fill_row(p).wait(), c)[1], 0)
        lo, hi = live_ref[0], xs_hbm.shape[0] // tm
        lax.fori_loop(lo, hi, lambda t, c: (fill_tile(t).start(), c)[1], 0)
        lax.fori_loop(lo, hi, lambda t, c: (fill_tile(t).wait(), c)[1], 0)


def _moe_dispatch(h2, pos, pad_lo, pad_hi, live, *, tt, tm, n_tiles):
    n = h2.shape[0]
    grid_spec = pltpu.PrefetchScalarGridSpec(
        num_scalar_prefetch=4,
        grid=(n // tt,),
        in_specs=[pl.BlockSpec((tt, D_MODEL), lambda i, *_: (i, 0))],
        out_specs=pl.BlockSpec(memory_space=pl.ANY),
        scratch_shapes=[pltpu.VMEM((2, tt, D_MODEL), F32), pltpu.VMEM((tm, D_MODEL), F32),
                        pltpu.SemaphoreType.DMA((2,)), pltpu.SemaphoreType.DMA(())],
    )
    return pl.pallas_call(
        _moe_dispatch_kernel,
        grid_spec=grid_spec,
        out_shape=jax.ShapeDtypeStruct((n_tiles * tm, D_MODEL), F32),
        compiler_params=_params(("arbitrary",)),
        name="moe_dispatch",
    )(pos, pad_lo, pad_hi, live, h2)


def _moe_expert_kernel(te_ref, live_ref, x_ref, wg_ref, wu_ref, wd_ref, y_ref):
    del te_ref
    is_live = pl.program_id(0) < live_ref[0]

    @pl.when(is_live)
    def _():
        y_ref[...] = _swiglu(x_ref[...].astype(BF16), wg_ref, wu_ref, wd_ref)

    @pl.when(jnp.logical_not(is_live))
    def _():
        y_ref[...] = jnp.zeros(y_ref.shape, F32)


def _moe_experts(xs, tile_expert, live, wg, wu, wd, *, tm, n_tiles):
    fe = D_FF_EXPERT

    def in_tile(t, te, live):
        return (jnp.minimum(t, live[0] - 1), 0)

    def out_tile(t, te, live):
        return (t, 0)

    def expert(t, te, live):
        return (te[t], 0, 0)

    grid_spec = pltpu.PrefetchScalarGridSpec(
        num_scalar_prefetch=2,
        grid=(n_tiles,),
        in_specs=[
            pl.BlockSpec((tm, D_MODEL), in_tile),
            pl.BlockSpec((None, D_MODEL, fe), expert),
            pl.BlockSpec((None, D_MODEL, fe), expert),
            pl.BlockSpec((None, fe, D_MODEL), expert),
        ],
        out_specs=pl.BlockSpec((tm, D_MODEL), out_tile),
    )
    return pl.pallas_call(
        _moe_expert_kernel,
        grid_spec=grid_spec,
        out_shape=jax.ShapeDtypeStruct(xs.shape, F32),
        compiler_params=_params(("arbitrary",)),
        name="moe_experts",
    )(tile_expert, live, xs, wg, wu, wd)


def _moe_combine_kernel(pos_ref, x_ref, g2_ref, r_ref, ys_hbm, o_ref, buf, sem):
    i = pl.program_id(0)
    nt = pl.num_programs(0)
    tt = x_ref.shape[0]
    n = nt * tt

    def fetch(tile, sl):
        base = tile * tt
        for r in range(tt):
            for s in range(2):
                row = pos_ref[s * n + base + r]
                pltpu.make_async_copy(ys_hbm.at[pl.ds(row, 1)], buf.at[sl, pl.ds(s * tt + r, 1)],
                                      sem.at[sl]).start(priority=s)

    @pl.when(i == 0)
    def _():
        fetch(0, 0)

    @pl.when(i + 1 < nt)
    def _():
        fetch(i + 1, (i + 1) % 2)

    sl = i % 2
    _row_copy_wait(ys_hbm, buf.at[sl], sem.at[sl], 2 * tt)
    r = r_ref[...]
    f = r[:, 0:1] * buf[sl, pl.ds(0, tt), :] + r[:, 1:2] * buf[sl, pl.ds(tt, tt), :]
    o_ref[...] = x_ref[...] + g2_ref[...] * f


def _moe_combine(x1, g2, route, pos, ys, *, tt):
    n = x1.shape[0]
    grid_spec = pltpu.PrefetchScalarGridSpec(
        num_scalar_prefetch=1,
        grid=(n // tt,),
        in_specs=[
            pl.BlockSpec((tt, D_MODEL), lambda i, pos: (i, 0)),
            _mod_spec(tt, n, g2),
            pl.BlockSpec((tt, ROUTER_PAD), lambda i, pos: (i, 0)),
            pl.BlockSpec(memory_space=pl.ANY),
        ],
        out_specs=pl.BlockSpec((tt, D_MODEL), lambda i, pos: (i, 0)),
        scratch_shapes=[pltpu.VMEM((2, 2 * tt, D_MODEL), F32), pltpu.SemaphoreType.DMA((2,))],
    )
    return pl.pallas_call(
        _moe_combine_kernel,
        grid_spec=grid_spec,
        out_shape=jax.ShapeDtypeStruct((n, D_MODEL), F32),
        compiler_params=_params(("arbitrary",)),
        name="moe_combine",
    )(pos, x1, g2.arr, route, ys)


def _moe_routed(h2, x1, g2, route, route_t, tile_counts, wg, wu, wd, *, tm, tt):
    n = h2.shape[0]
    n_tiles = (2 * n) // tm + N_EXPERTS
    pos, tile_expert, live, pad_lo, pad_hi = _route_tables(route_t, tile_counts, tm, n_tiles)
    xs = _moe_dispatch(h2, pos, pad_lo, pad_hi, live, tt=tt, tm=tm, n_tiles=n_tiles)
    ys = _moe_experts(xs, tile_expert, live, wg, wu, wd, tm=tm, n_tiles=n_tiles)
    return _moe_combine(x1, g2, route, pos, ys, tt=tt)


def _head_perm():
    idx = []
    for g in range(GROUP):
        for kv in range(N_KV):
            h = kv * GROUP + g
            idx.extend(range(h * HEAD_DIM, (h + 1) * HEAD_DIM))
    return np.asarray(idx, np.int32)


def _relayout_w_in(w):
    pts = np.cumsum([0, Q_W, KV_W, KV_W, GK_W, GK_W, GV_W, GATE_RANK, GV_W, D_MODEL, D_MODEL])
    qa, ka, va, qb, kb, vb, ga, rb, gta, gtb = [w[:, pts[i]:pts[i + 1]] for i in range(10)]
    qa = qa[:, _head_perm()]
    ga = jnp.pad(ga, ((0, 0), (0, GATE_PAD - GATE_RANK)))
    return jnp.concatenate([qa, ka, va, qb, kb, vb, rb, gta, gtb, ga], axis=1).astype(BF16)


def kernel(x_prompt, x_sample, cache_k, cache_v, state_gla, c_prompt, c_sample, ada_w, ada_b, norm1_g, norm2_g, w_in, q_norm_g, k_norm_g, attn_sinks, gla_wa2, gla_ba, gla_norm_g, w_branch_a, w_branch_b, w_out, ffn_w_gate, ffn_w_up, ffn_w_down, router_w, router_b, moe_w_gate, moe_w_up, moe_w_down):
    n_p = BATCH * SEQ
    xp = x_prompt.reshape(n_p, D_MODEL)
    xs = x_sample.reshape(DEC_BATCH, D_MODEL)

    c_pad = -(BATCH + DEC_BATCH) % 8
    c_all = jnp.pad(jnp.concatenate([c_sample, c_prompt], axis=0), ((0, c_pad), (0, 0)))
    mod_sample, mod_prompt = _ada(c_all, ada_w, ada_b, DEC_BATCH, BATCH)
    state_rows = state_gla.reshape(DEPTH, DEC_BATCH, GK_W, GLA_DV)

    bd = jnp.asarray(np.kron(np.eye(N_HEADS), np.ones((HEAD_DIM, HEAD_DIM))), BF16)
    ut = jnp.asarray(_gla_cum_matrix(), BF16)
    gla_upper, gla_pairs = (jnp.asarray(m) for m in _gla_level_masks())
    perm = _head_perm()
    swa_bias = jnp.asarray(_swa_sample_bias())
    swa_prompt_bias = jnp.asarray(_swa_prompt_bias())

    kp_l, vp_l, sp_l, ks_l, vs_l = [], [], [], [], []
    new_states = None
    for l in range(DEPTH):
        mod_p = [_Mod(mod_prompt, l, i) for i in range(N_MOD)]
        mod_s = [_Mod(mod_sample, l, i) for i in range(N_MOD)]

        w = _relayout_w_in(w_in[l])
        qg = (jnp.tile(q_norm_g[l], N_HEADS) * (HEAD_DIM ** -0.5)).reshape(1, Q_W)
        kg = jnp.tile(k_norm_g[l], N_KV).reshape(1, KV_W)
        wa2 = jnp.pad(gla_wa2[l], ((0, GATE_PAD - GATE_RANK), (0, 0))).astype(BF16)
        ba = gla_ba[l].reshape(1, GK_W)
        n1 = norm1_g[l].reshape(1, D_MODEL)
        n2 = norm2_g[l].reshape(1, D_MODEL)
        gn = gla_norm_g[l].reshape(1, GLA_DV)
        wpa = w_branch_a[l][perm].astype(BF16)
        wpb = w_branch_b[l].astype(BF16)
        wo = w_out[l].astype(BF16)
        sink_rows = jnp.broadcast_to(attn_sinks[l][_sample_head_of_row()][:, None], (N_HEADS, LANES))

        q, k, v, gq, gk, gv, la, rbs, sga, sgb = _mixin(
            xp, mod_p[0], mod_p[1], n1, w, bd, qg, kg, wa2, ba, tm=ROW_TILE_RESIDENT)
        a_out = _swa_prompt(attn_sinks[l], q, k, v, swa_prompt_bias)
        g_out, s_fin = _gla_prompt(gq, gk, la, gv, rbs, gn, ut, gla_upper, gla_pairs)
        is_moe = l % 2 == 1
        if is_moe:
            rw = jnp.pad(router_w[l // 2], ((0, 0), (0, ROUTER_PAD - N_EXPERTS))).astype(BF16)
            rbias = jnp.pad(router_b[l // 2], (0, ROUTER_PAD - N_EXPERTS)).reshape(1, ROUTER_PAD)
        merged = _merge(xp, a_out, g_out, sga, sgb, wpa, wpb, wo, mod_p[2], mod_p[3], mod_p[4], n2,
                        tm=ROW_TILE_RESIDENT, router=(rw, rbias) if is_moe else None)
        x1, h2 = merged[0], merged[1]
        for cache, full in ((kp_l, k), (vp_l, v)):
            tail = full.reshape(BATCH, SEQ, KV_W)[:, SEQ - WINDOW:]
            cache.append(tail.reshape(BATCH, WINDOW, N_KV, HEAD_DIM))
        sp_l.append(s_fin.reshape(BATCH, GLA_HEADS, GLA_DK, GLA_DV))

        qs, ksn, vsn, gqs, gks, gvs, las, rbss, sgas, sgbs = _mixin(
            xs, mod_s[0], mod_s[1], n1, w, bd, qg, kg, wa2, ba, tm=DEC_BATCH)
        a_s, nk, nv = _swa_sample(
            qs.astype(F32).reshape(DEC_BATCH, N_HEADS, HEAD_DIM),
            ksn.reshape(DEC_BATCH, N_KV, HEAD_DIM), vsn.reshape(DEC_BATCH, N_KV, HEAD_DIM),
            cache_k[l].reshape(1, DEC_BATCH, CACHE_ROWS, HEAD_DIM),
            cache_v[l].reshape(1, DEC_BATCH, CACHE_ROWS, HEAD_DIM), swa_bias, sink_rows, layer=0)
        g_s, new_states = _gla_sample(gqs, gks, las, gvs, rbss, gn, state_rows, new_states, layer=l)
        x1s, h2s = _merge(xs, a_s.reshape(DEC_BATCH, Q_W), g_s, sgas, sgbs, wpa, wpb, wo,
                          mod_s[2], mod_s[3], mod_s[4], n2, tm=DEC_BATCH)
        ks_l.append(nk.reshape(DEC_BATCH, WINDOW, N_KV, HEAD_DIM))
        vs_l.append(nv.reshape(DEC_BATCH, WINDOW, N_KV, HEAD_DIM))

        i = l // 2
        if not is_moe:
            wg, wu, wd = ffn_w_gate[i].astype(BF16), ffn_w_up[i].astype(BF16), ffn_w_down[i].astype(BF16)
            xp = _ffn(h2, x1, mod_p[5], wg, wu, wd, tm=ROW_TILE_SWIGLU, tf=D_FF // 2)
            xs = _ffn(h2s, x1s, mod_s[5], wg, wu, wd, tm=DEC_BATCH, tf=D_FF // 2)
        else:
            wg, wu, wd = moe_w_gate[i].astype(BF16), moe_w_up[i].astype(BF16), moe_w_down[i].astype(BF16)
            xp = _moe_routed(h2, x1, mod_p[5], *merged[2:5], wg, wu, wd,
                             tm=ROW_TILE_SWIGLU, tt=MOE_TOKEN_TILE)
            xs = _moe(h2s, x1s, mod_s[5], rw, rbias, wg, wu, wd, tm=DEC_BATCH)

    return (xp.reshape(BATCH, SEQ, D_MODEL), xs.reshape(DEC_BATCH, 1, D_MODEL),
            jnp.stack(kp_l), jnp.stack(vp_l), jnp.stack(sp_l),
            jnp.stack(ks_l), jnp.stack(vs_l),
            new_states.reshape(DEPTH, DEC_BATCH, GLA_HEADS, GLA_DK, GLA_DV))
```

```python
from typing import NamedTuple

import jax
import jax.numpy as jnp
import numpy as np
from jax import lax
from jax.experimental import pallas as pl
from jax.experimental.pallas import tpu as pltpu

D_MODEL = 1024
BATCH = 4
SEQ = 4096
DEPTH = 2
DEC_BATCH = 128
N_HEADS = 8
N_KV = 2
HEAD_DIM = 64
GROUP = N_HEADS // N_KV
WINDOW = 128
GLA_HEADS = 4
GLA_DK = 64
GLA_DV = 128
GATE_RANK = 16
GATE_TAU = 16.0
D_FF = 2816
N_EXPERTS = 8
D_FF_EXPERT = 1408
EPS = 1e-6

Q_W = N_HEADS * HEAD_DIM
KV_W = N_KV * HEAD_DIM
GK_W = GLA_HEADS * GLA_DK
GV_W = GLA_HEADS * GLA_DV

LANES = 128
GATE_PAD = LANES
ROUTER_PAD = LANES
VMEM_LIMIT = 56 * 1024 * 1024

ROW_TILE_RESIDENT = 1024
ROW_TILE_SWIGLU = 512
MOE_TOKEN_TILE = 512

F32 = jnp.float32
BF16 = jnp.bfloat16

_C_Q = 0
_C_K = _C_Q + Q_W
_C_V = _C_K + KV_W
_C_GQ = _C_V + KV_W
_C_GK = _C_GQ + GK_W
_C_GV = _C_GK + GK_W
_C_RB = _C_GV + GV_W
_C_GA = _C_RB + GV_W
_C_GB = _C_GA + D_MODEL
_C_LR = _C_GB + D_MODEL
PROJ_PAD = _C_LR + GATE_PAD

GLA_CHUNK = 128
GLA_LEVELS = 7
GLA_MXU_LEVELS = 3
GLA_CHUNKS_PER_STEP = 4


def _params(sem, vmem=VMEM_LIMIT):
    return pltpu.CompilerParams(dimension_semantics=sem, vmem_limit_bytes=vmem)


def _dot(a, b):
    return jnp.dot(a, b, preferred_element_type=F32)


def _dot_nt(a, b):
    return lax.dot_general(a, b, (((1,), (1,)), ((), ())), preferred_element_type=F32)


def _dot_tn(a, b):
    return lax.dot_general(a, b, (((0,), (0,)), ((), ())), preferred_element_type=F32)


def _sigmoid(x):
    return 0.5 * jnp.tanh(0.5 * x) + 0.5


def _silu(x):
    return x * _sigmoid(x)


N_MOD = 6


class _Mod(NamedTuple):
    arr: jax.Array
    layer: int
    idx: int


def _ada_kernel(c_ref, w_ref, b_ref, os_ref, op_ref):
    c = c_ref[...]
    mod = _dot(_silu(c).astype(BF16), w_ref[...].astype(BF16)) + b_ref[...]
    n_s = os_ref.shape[0]
    os_ref[...] = mod[0:n_s]
    for b in range(op_ref.shape[0]):
        op_ref[b] = mod[n_s + b:n_s + b + 1]


def _ada(c_all, ada_w, ada_b, n_sample, n_prompt):
    rows = c_all.shape[0]
    return pl.pallas_call(
        _ada_kernel,
        grid=(DEPTH, N_MOD),
        in_specs=[
            pl.BlockSpec((rows, D_MODEL), lambda l, j: (0, 0)),
            pl.BlockSpec((None, D_MODEL, D_MODEL), lambda l, j: (l, 0, j)),
            pl.BlockSpec((None, 1, D_MODEL), lambda l, j: (l, 0, j)),
        ],
        out_specs=[
            pl.BlockSpec((None, None, n_sample, D_MODEL), lambda l, j: (l, j, 0, 0)),
            pl.BlockSpec((None, None, n_prompt, 1, D_MODEL), lambda l, j: (l, j, 0, 0, 0)),
        ],
        out_shape=[
            jax.ShapeDtypeStruct((DEPTH, N_MOD, n_sample, D_MODEL), F32),
            jax.ShapeDtypeStruct((DEPTH, N_MOD, n_prompt, 1, D_MODEL), F32),
        ],
        compiler_params=_params(("parallel", "parallel")),
        name="ada_mod",
    )(c_all, ada_w, ada_b.reshape(DEPTH, 1, N_MOD * D_MODEL))


def _mixin_kernel(x_ref, sh_ref, sc_ref, n1_ref, w_ref, bd_ref, qg_ref, kg_ref, wa2_ref, ba_ref,
                  q_ref, k_ref, v_ref, gq_ref, gk_ref, gv_ref, la_ref, rb_ref, sga_ref, sgb_ref):
    x = x_ref[...]
    ms = jnp.mean(x * x, axis=-1, keepdims=True)
    h = x * lax.rsqrt(ms + EPS) * n1_ref[...]
    h = h * (1.0 + sc_ref[...]) + sh_ref[...]
    hb = h.astype(BF16)

    def proj(a, b):
        return _dot(hb, w_ref[:, a:b])

    q = proj(_C_Q, _C_K)
    ssq = _dot((q * q).astype(BF16), bd_ref[...])
    q_ref[...] = (q * lax.rsqrt(ssq * (1.0 / HEAD_DIM) + EPS) * qg_ref[...]).astype(BF16)
    k = proj(_C_K, _C_V)
    ssk = _dot((k * k).astype(BF16), bd_ref[0:KV_W, 0:KV_W])
    k_ref[...] = k * lax.rsqrt(ssk * (1.0 / HEAD_DIM) + EPS) * kg_ref[...]
    v_ref[...] = proj(_C_V, _C_GQ)
    gq_ref[...] = proj(_C_GQ, _C_GK) * (GLA_DK ** -0.5)
    gk_ref[...] = proj(_C_GK, _C_GV)
    gv_ref[...] = proj(_C_GV, _C_RB)
    rb_ref[...] = _silu(proj(_C_RB, _C_GA)).astype(BF16)
    sga_ref[...] = _sigmoid(proj(_C_GA, _C_GB)).astype(BF16)
    sgb_ref[...] = _sigmoid(proj(_C_GB, _C_LR)).astype(BF16)
    ga = proj(_C_LR, PROJ_PAD)
    xg = _dot(ga.astype(BF16), wa2_ref[...]) + ba_ref[...]
    la_ref[...] = (jnp.minimum(xg, 0.0) - jnp.log1p(jnp.exp(-jnp.abs(xg)))) * (1.0 / GATE_TAU)


def _mixin(x, sh, sc, n1, w, bd, qg, kg, wa2, ba, *, tm):
    n = x.shape[0]
    nt = n // tm

    def row(i):
        return (i, 0)

    def const(shape):
        return pl.BlockSpec(shape, lambda i: (0,) * len(shape), pipeline_mode=pl.Buffered(1))

    def out(width, dtype):
        return pl.BlockSpec((tm, width), row), jax.ShapeDtypeStruct((n, width), dtype)

    outs = [out(Q_W, BF16), out(KV_W, F32), out(KV_W, F32), out(GK_W, F32), out(GK_W, F32),
            out(GV_W, F32), out(GK_W, F32), out(GV_W, BF16), out(D_MODEL, BF16), out(D_MODEL, BF16)]
    return pl.pallas_call(
        _mixin_kernel,
        grid=(nt,),
        in_specs=[
            pl.BlockSpec((tm, D_MODEL), row), _mod_spec(tm, n, sh), _mod_spec(tm, n, sc), const((1, D_MODEL)),
            const((D_MODEL, PROJ_PAD)), const((Q_W, Q_W)), const((1, Q_W)), const((1, KV_W)),
            const((GATE_PAD, GK_W)), const((1, GK_W)),
        ],
        out_specs=[o[0] for o in outs],
        out_shape=[o[1] for o in outs],
        compiler_params=_params(("parallel",)),
        name="mixer_in",
    )(x, sh.arr, sc.arr, n1, w, bd, qg, kg, wa2, ba)


SWA_BLOCKS = 8


def _head_slope(h):
    return float(2.0 ** (-8.0 * (h + 1) / N_HEADS))


def _swa_prompt_bias():
    blk = WINDOW
    dist = np.arange(blk)[:, None] + blk - np.arange(2 * blk)[None, :]
    slopes = np.asarray([_head_slope(h) for h in range(N_HEADS)])[:, None, None]
    return np.where((dist >= 0) & (dist <= WINDOW), -slopes * dist, -np.inf).astype(np.float32)


def _swa_prompt_kernel(sink_ref, q_ref, kp_ref, kc_ref, vp_ref, vc_ref, bias_ref, o_ref):
    n = pl.program_id(1)
    blk = WINDOW
    kall = jnp.concatenate([kp_ref[...], kc_ref[...]], axis=0).astype(BF16)
    vall = jnp.concatenate([vp_ref[...], vc_ref[...]], axis=0).astype(BF16)
    col = lax.broadcasted_iota(jnp.int32, (blk, 2 * blk), 1)
    first_key = jnp.where(n > 0, 0, blk)
    klane = lax.broadcasted_iota(jnp.int32, kall.shape, 1)
    kall_kv = [jnp.where((klane // HEAD_DIM) == kv, kall, jnp.zeros_like(kall)) for kv in range(N_KV)]
    olane = lax.broadcasted_iota(jnp.int32, (blk, KV_W), 1)
    q = q_ref[...]
    for j in range(SWA_BLOCKS):
        vv = vall[j * blk:(j + 2) * blk]
        outs = []
        for g in range(GROUP):
            qp = q[j * blk:(j + 1) * blk, g * KV_W:(g + 1) * KV_W]
            pair = []
            for kv in range(N_KV):
                h = kv * GROUP + g
                s = _dot_nt(qp, kall_kv[kv][j * blk:(j + 2) * blk]) + bias_ref[h]
                if j == 0:
                    s = jnp.where(col >= first_key, s, -jnp.inf)
                sink = sink_ref[h]
                m = jnp.maximum(jnp.max(s, axis=-1, keepdims=True), sink)
                p = jnp.exp(s - m)
                den = jnp.sum(p, axis=-1, keepdims=True) + jnp.exp(sink - m)
                pair.append(_dot(p.astype(BF16), vv) * (1.0 / den))
            outs.append(jnp.where(olane < HEAD_DIM, pair[0], pair[1]))
        o_ref[j * blk:(j + 1) * blk, :] = jnp.concatenate(outs, axis=-1).astype(BF16)


def _swa_prompt(sinks, q, k, v, bias):
    nb = SEQ // WINDOW
    steps = nb // SWA_BLOCKS
    tq = SWA_BLOCKS * WINDOW

    def cur(b, n):
        return (b * steps + n, 0)

    def prev(b, n):
        return (b * nb + jnp.maximum(n * SWA_BLOCKS - 1, 0), 0)

    return pl.pallas_call(
        _swa_prompt_kernel,
        grid=(BATCH, steps),
        in_specs=[
            pl.BlockSpec(memory_space=pltpu.SMEM),
            pl.BlockSpec((tq, Q_W), cur),
            pl.BlockSpec((WINDOW, KV_W), prev), pl.BlockSpec((tq, KV_W), cur),
            pl.BlockSpec((WINDOW, KV_W), prev), pl.BlockSpec((tq, KV_W), cur),
            pl.BlockSpec(bias.shape, lambda b, n: (0, 0, 0), pipeline_mode=pl.Buffered(1)),
        ],
        out_specs=pl.BlockSpec((tq, Q_W), cur),
        out_shape=jax.ShapeDtypeStruct((BATCH * SEQ, Q_W), BF16),
        compiler_params=_params(("parallel", "parallel")),
        name="swa_prompt",
    )(sinks, q, k, k, v, v, bias)


SAMPLE_TB = 8
CACHE_ROWS = WINDOW * N_KV


def _sample_head_of_row():
    j = np.arange(N_HEADS)
    return (j % N_KV) * GROUP + j // N_KV


def _swa_sample_bias():
    j = np.arange(N_HEADS)[:, None]
    c = np.arange(CACHE_ROWS)[None, :]
    slope = 2.0 ** (-8.0 * (_sample_head_of_row()[:, None] + 1) / N_HEADS)
    bias = -slope * (WINDOW - c // N_KV)
    return np.where(c % N_KV == j % N_KV, bias, -np.inf).astype(np.float32)


def _swa_sample_kernel(q_ref, kn_ref, vn_ref, ck_ref, cv_ref, bias_ref, sk_ref, o_ref, ok_ref, ov_ref):
    rows = CACHE_ROWS
    sink = sk_ref[...][:, 0:1]
    q = q_ref[...]
    kn = kn_ref[...]
    vn = vn_ref[...]
    kc = ck_ref[...]
    vc = cv_ref[...]
    kn8 = jnp.concatenate([kn] * GROUP, axis=1)
    vn8 = jnp.concatenate([vn] * GROUP, axis=1)
    s = lax.dot_general(q.astype(BF16), kc.astype(BF16), (((2,), (2,)), ((0,), (0,))),
                        preferred_element_type=F32) + bias_ref[...]
    s_new = jnp.sum(q * kn8, axis=-1, keepdims=True)
    m = jnp.maximum(jnp.maximum(jnp.max(s, axis=-1, keepdims=True), s_new), sink)
    p = jnp.exp(s - m)
    p_new = jnp.exp(s_new - m)
    den = jnp.sum(p, axis=-1, keepdims=True) + p_new + jnp.exp(sink - m)
    o = lax.dot_general(p.astype(BF16), vc.astype(BF16), (((2,), (1,)), ((0,), (0,))),
                        preferred_element_type=F32) + p_new * vn8
    o_ref[...] = o * (1.0 / den)
    ok_ref[:, pl.ds(0, rows - N_KV), :] = kc[:, N_KV:, :]
    ok_ref[:, pl.ds(rows - N_KV, N_KV), :] = kn
    ov_ref[:, pl.ds(0, rows - N_KV), :] = vc[:, N_KV:, :]
    ov_ref[:, pl.ds(rows - N_KV, N_KV), :] = vn


def _swa_sample(q, kn, vn, ck, cv, bias, sinks, *, layer):
    tb = 2 * SAMPLE_TB
    nb = DEC_BATCH // tb

    def const(shape):
        return pl.BlockSpec(shape, lambda i: (0,) * len(shape))

    def per_seq(*dims):
        return pl.BlockSpec((tb,) + dims, lambda i: (i, 0, 0))

    cache_spec = pl.BlockSpec((None, tb, CACHE_ROWS, HEAD_DIM), lambda i: (layer, i, 0, 0))
    return pl.pallas_call(
        _swa_sample_kernel,
        grid=(nb,),
        in_specs=[per_seq(N_HEADS, HEAD_DIM), per_seq(N_KV, HEAD_DIM), per_seq(N_KV, HEAD_DIM),
                  cache_spec, cache_spec,
                  const((N_HEADS, CACHE_ROWS)), const((N_HEADS, LANES))],
        out_specs=[per_seq(N_HEADS, HEAD_DIM), per_seq(CACHE_ROWS, HEAD_DIM), per_seq(CACHE_ROWS, HEAD_DIM)],
        out_shape=[
            jax.ShapeDtypeStruct((DEC_BATCH, N_HEADS, HEAD_DIM), F32),
            jax.ShapeDtypeStruct((DEC_BATCH, CACHE_ROWS, HEAD_DIM), F32),
            jax.ShapeDtypeStruct((DEC_BATCH, CACHE_ROWS, HEAD_DIM), F32),
        ],
        compiler_params=_params(("parallel",)),
        name="swa_sample",
    )(q, kn, vn, ck, cv, bias, sinks)


def _gla_cum_matrix():
    c = GLA_CHUNK
    tri = np.tril(np.ones((c, c), np.float32))
    i = np.arange(c)
    blocks = []
    for lvl in range(GLA_MXU_LEVELS):
        half = 1 << lvl
        mid = (i // (2 * half)) * (2 * half) + half - 1
        blocks.append(tri - tri[mid])
    blocks.append(tri)
    return np.concatenate(blocks, axis=0)


def _split3(x):
    hi = x.astype(BF16)
    r1 = x - hi.astype(F32)
    mid = r1.astype(BF16)
    lo = (r1 - mid.astype(F32)).astype(BF16)
    return hi, mid, lo


def _gla_level_masks():
    cl = GLA_CHUNK
    r = np.arange(cl)
    upper = np.stack([np.broadcast_to(((r >> lvl) & 1)[:, None], (cl, GK_W)) for lvl in range(GLA_LEVELS)])
    ri = np.tile(r, GLA_HEADS)[:, None]
    pairs = [(ri >> (lvl + 1)) == (r[None, :] >> (lvl + 1)) for lvl in range(GLA_LEVELS)]
    pairs.append(ri == r[None, :])
    return upper.astype(np.float32), np.stack(pairs).astype(np.float32)


def _gla_prompt_kernel(q_ref, k_ref, la_ref, v_ref, rb_ref, gn_ref, ut_ref, up_ref, pm_ref,
                       o_ref, s_ref, st_ref):
    c = pl.program_id(1)
    cl = GLA_CHUNK

    @pl.when(c == 0)
    def _():
        st_ref[0] = jnp.zeros(st_ref.shape[1:], F32)

    ut = ut_ref[...]
    lane = lax.broadcasted_iota(jnp.int32, (cl, GK_W), 1)
    head_of_lane = lane // GLA_DK
    ones = jnp.ones((cl, LANES), BF16)
    state = [st_ref[c % 2, h] for h in range(GLA_HEADS)]
    for cc in range(GLA_CHUNKS_PER_STEP):
        rows = slice(cc * cl, (cc + 1) * cl)
        hi, mid, lo = _split3(la_ref[rows, :])
        tall = _dot(ut, hi) + _dot(ut, mid) + _dot(ut, lo)
        q = q_ref[rows, :]
        k = k_ref[rows, :]
        attn_all = jnp.zeros((GLA_HEADS * cl, cl), F32)
        cum = tall[GLA_MXU_LEVELS * cl:(GLA_MXU_LEVELS + 1) * cl]
        for lvl in range(GLA_LEVELS + 1):
            if lvl < GLA_LEVELS:
                if lvl < GLA_MXU_LEVELS:
                    t_lvl = tall[lvl * cl:(lvl + 1) * cl]
                else:
                    half = 1 << lvl
                    mids = [jnp.broadcast_to(cum[b0 + half - 1:b0 + half, :], (2 * half, GK_W))
                            for b0 in range(0, cl, 2 * half)]
                    t_lvl = cum - (mids[0] if len(mids) == 1 else jnp.concatenate(mids, axis=0))
                e = jnp.exp(-jnp.abs(t_lvl))
                e_up = e * up_ref[lvl]
                qt = (q * e_up).astype(BF16)
                kt = (k * (e - e_up)).astype(BF16)
            else:
                qt = q.astype(BF16)
                kt = k.astype(BF16)
            q_heads = jnp.concatenate(
                [jnp.where(head_of_lane == h, qt, jnp.zeros_like(qt)) for h in range(GLA_HEADS)], axis=0)
            attn_all = attn_all + _dot_nt(q_heads, kt) * pm_ref[lvl]

        last = cum[cl - 1:cl, :]
        qe = (q * jnp.exp(cum)).astype(BF16)
        kd = (k * jnp.exp(last - cum)).astype(BF16)
        dec = jnp.exp(_dot_tn(hi, ones) + _dot_tn(mid, ones) + _dot_tn(lo, ones))
        v = v_ref[rows, :]
        rb = rb_ref[rows, :]
        outs = []
        for h in range(GLA_HEADS):
            vh = v[:, h * GLA_DV:(h + 1) * GLA_DV].astype(BF16)
            o = _dot(attn_all[h * cl:(h + 1) * cl].astype(BF16), vh) + _dot(qe, state[h].astype(BF16))
            kdh = jnp.where(head_of_lane == h, kd, jnp.zeros_like(kd))
            state[h] = state[h] * dec + _dot_tn(kdh, vh)
            ms = jnp.mean(o * o, axis=-1, keepdims=True)
            g = o * lax.rsqrt(ms + EPS) * gn_ref[...]
            outs.append(g * rb[:, h * GLA_DV:(h + 1) * GLA_DV].astype(F32))
        o_ref[rows, :] = jnp.concatenate(outs, axis=-1).astype(BF16)

    for h in range(GLA_HEADS):
        st_ref[(c + 1) % 2, h] = state[h]

    @pl.when(c == pl.num_programs(1) - 1)
    def _():
        fin = (SEQ // (GLA_CHUNK * GLA_CHUNKS_PER_STEP)) % 2
        s_ref[...] = st_ref[fin, 0] + st_ref[fin, 1] + st_ref[fin, 2] + st_ref[fin, 3]


def _gla_prompt(gq, gk, la, gv, rbs, gn, ut, upper, pairs):
    cl = GLA_CHUNK * GLA_CHUNKS_PER_STEP
    nc = SEQ // cl

    def row(b, c):
        return (b * nc + c, 0)

    def const(shape):
        return pl.BlockSpec(shape, lambda b, c: (0,) * len(shape), pipeline_mode=pl.Buffered(1))

    return pl.pallas_call(
        _gla_prompt_kernel,
        grid=(BATCH, nc),
        in_specs=[
            pl.BlockSpec((cl, GK_W), row), pl.BlockSpec((cl, GK_W), row), pl.BlockSpec((cl, GK_W), row),
            pl.BlockSpec((cl, GV_W), row), pl.BlockSpec((cl, GV_W), row),
            const((1, GLA_DV)), const(ut.shape), const(upper.shape), const(pairs.shape),
        ],
        out_specs=[
            pl.BlockSpec((cl, GV_W), row),
            pl.BlockSpec((None, GK_W, GLA_DV), lambda b, c: (b, 0, 0)),
        ],
        out_shape=[
            jax.ShapeDtypeStruct((BATCH * SEQ, GV_W), BF16),
            jax.ShapeDtypeStruct((BATCH, GK_W, GLA_DV), F32),
        ],
        scratch_shapes=[pltpu.VMEM((2, GLA_HEADS, GK_W, GLA_DV), F32)],
        compiler_params=_params(("parallel", "arbitrary")),
        name="gla_prompt",
    )(gq, gk, la, gv, rbs, gn, ut, upper, pairs)


def _gla_sample_kernel(q_ref, k_ref, la_ref, v_ref, rb_ref, gn_ref, s_ref, o_ref, so_ref):
    tb = SAMPLE_TB
    dec = jnp.exp(la_ref[...])
    pieces = []
    for x in (dec, k_ref[...], q_ref[...]):
        hi, mid, lo = _split3(x)
        stacked = jnp.concatenate(
            [hi.astype(F32), mid.astype(F32), lo.astype(F32), jnp.zeros_like(x)], axis=0)
        pieces.append(stacked.astype(BF16))
    prow = lax.broadcasted_iota(jnp.int32, (4 * tb, LANES), 0)
    rb = rb_ref[...].astype(F32)
    v = v_ref[...]
    for bi in range(tb):
        sel = jnp.where((prow % tb) == bi, 1.0, 0.0).astype(BF16)
        a_col, k_col, q_col = [_dot_tn(p, sel) for p in pieces]
        for h in range(GLA_HEADS):
            rs = slice(h * GLA_DK, (h + 1) * GLA_DK)
            vs = slice(h * GLA_DV, (h + 1) * GLA_DV)
            s_new = a_col[rs] * s_ref[bi, rs, :] + k_col[rs] * v[bi:bi + 1, vs]
            so_ref[bi, rs, :] = s_new
            o = jnp.sum(q_col[rs] * s_new, axis=0, keepdims=True)
            ms = jnp.mean(o * o, axis=-1, keepdims=True)
            g = o * lax.rsqrt(ms + EPS) * gn_ref[...]
            o_ref[bi:bi + 1, vs] = g * rb[bi:bi + 1, vs]


def _gla_sample(gq, gk, la, gv, rbs, gn, state, *, layer):
    tb = SAMPLE_TB
    nb = DEC_BATCH // tb

    def row(w):
        return pl.BlockSpec((tb, w), lambda i: (i, 0))

    st_spec = pl.BlockSpec((tb, GK_W, GLA_DV), lambda i: (i, 0, 0))
    st_in_spec = pl.BlockSpec((None, tb, GK_W, GLA_DV), lambda i: (layer, i, 0, 0))
    return pl.pallas_call(
        _gla_sample_kernel,
        grid=(nb,),
        in_specs=[row(GK_W), row(GK_W), row(GK_W), row(GV_W), row(GV_W),
                  pl.BlockSpec((1, GLA_DV), lambda i: (0, 0)), st_in_spec],
        out_specs=[row(GV_W), st_spec],
        out_shape=[
            jax.ShapeDtypeStruct((DEC_BATCH, GV_W), F32),
            jax.ShapeDtypeStruct((DEC_BATCH, GK_W, GLA_DV), F32),
        ],
        compiler_params=_params(("parallel",)),
        name="gla_sample",
    )(gq, gk, la, gv, rbs, gn, state)


def _merge_kernel(x_ref, a_ref, g_ref, sga_ref, sgb_ref, wpa_ref, wpb_ref, wo_ref,
                  g1_ref, sh_ref, sc_ref, n2_ref, *rest):
    with_router = len(rest) == 8
    x1_ref, h2_ref = rest[3:5] if with_router else rest
    ya = _dot(a_ref[...].astype(BF16), wpa_ref[...])
    yb = _dot(g_ref[...].astype(BF16), wpb_ref[...])
    merged = sga_ref[...].astype(F32) * ya + sgb_ref[...].astype(F32) * yb
    mix = _dot(merged.astype(BF16), wo_ref[...])
    x1 = x_ref[...] + g1_ref[...] * mix
    x1_ref[...] = x1
    ms = jnp.mean(x1 * x1, axis=-1, keepdims=True)
    h = x1 * lax.rsqrt(ms + EPS) * n2_ref[...]
    h2 = h * (1.0 + sc_ref[...]) + sh_ref[...]
    h2_ref[...] = h2.astype(h2_ref.dtype)
    if with_router:
        rw_ref, rbias_ref, tri_ref = rest[0:3]
        route_ref, route_t_ref, cnt_ref = rest[5:8]
        logits = _dot(h2.astype(BF16), rw_ref[...]) + rbias_ref[...]
        packed, counts = _route_pack(logits, tri_ref[...])
        route_ref[...] = packed
        route_t_ref[...] = packed.T[0:route_t_ref.shape[0], :]
        cnt_ref[...] = jnp.broadcast_to(counts, cnt_ref.shape)


def _mod_spec(tm, n, mod):
    if mod.arr.ndim == 4:
        return pl.BlockSpec((None, None, tm, D_MODEL), lambda i, *_: (mod.layer, mod.idx, i, 0))
    tiles_per_seq = (n // mod.arr.shape[2]) // tm
    return pl.BlockSpec((None, None, None, 1, D_MODEL),
                        lambda i, *_: (mod.layer, mod.idx, i // tiles_per_seq, 0, 0))


def _merge(x, a, g, sga, sgb, wpa, wpb, wo, g1, sh2, sc2, n2, *, tm, router=None):
    n = x.shape[0]

    def row(w):
        return pl.BlockSpec((tm, w), lambda i: (i, 0))

    def const(shape):
        return pl.BlockSpec(shape, lambda i: (0,) * len(shape))

    in_specs = [row(D_MODEL), row(a.shape[1]), row(GV_W), row(D_MODEL), row(D_MODEL),
                const(wpa.shape), const(wpb.shape), const(wo.shape),
                _mod_spec(tm, n, g1), _mod_spec(tm, n, sh2), _mod_spec(tm, n, sc2), const((1, D_MODEL))]
    out_specs = [row(D_MODEL), row(D_MODEL)]
    out_shape = [jax.ShapeDtypeStruct((n, D_MODEL), F32), jax.ShapeDtypeStruct((n, D_MODEL), BF16)]
    args = [x, a, g, sga, sgb, wpa, wpb, wo, g1.arr, sh2.arr, sc2.arr, n2]
    if router is not None:
        tri = jnp.asarray(np.tril(np.ones((tm, tm), np.float32), -1), BF16)
        in_specs += [const(router[0].shape), const(router[1].shape), const((tm, tm))]
        out_specs += [row(ROUTER_PAD), pl.BlockSpec((8, tm), lambda i: (0, i)),
                      pl.BlockSpec((None, 8, ROUTER_PAD), lambda i: (i, 0, 0))]
        out_shape += [jax.ShapeDtypeStruct((n, ROUTER_PAD), F32), jax.ShapeDtypeStruct((8, n), F32),
                      jax.ShapeDtypeStruct((n // tm, 8, ROUTER_PAD), F32)]
        args += [router[0], router[1], tri]
    return pl.pallas_call(
        _merge_kernel,
        grid=(n // tm,),
        in_specs=in_specs,
        out_specs=out_specs,
        out_shape=out_shape,
        compiler_params=_params(("parallel",)),
        name="merge_out",
    )(*args)


def _swiglu(hb, wg_ref, wu_ref, wd_ref):
    act = (_silu(_dot(hb, wg_ref[...])) * _dot(hb, wu_ref[...])).astype(BF16)
    return _dot(act, wd_ref[...])


def _ffn_kernel(h_ref, x_ref, g2_ref, wg_ref, wu_ref, wd_ref, o_ref, acc_ref):
    f = pl.program_id(1)
    y = _swiglu(h_ref[...], wg_ref, wu_ref, wd_ref)

    @pl.when(f == 0)
    def _():
        acc_ref[...] = y

    @pl.when(f > 0)
    def _():
        acc_ref[...] += y

    @pl.when(f == pl.num_programs(1) - 1)
    def _():
        o_ref[...] = x_ref[...] + g2_ref[...] * acc_ref[...]


def _ffn(h2, x1, g2, wg, wu, wd, *, tm, tf):
    n = h2.shape[0]
    mod_spec = _mod_spec(tm, n, g2)
    return pl.pallas_call(
        _ffn_kernel,
        grid=(n // tm, D_FF // tf),
        in_specs=[
            pl.BlockSpec((tm, D_MODEL), lambda i, f: (i, 0)),
            pl.BlockSpec((tm, D_MODEL), lambda i, f: (i, 0)),
            mod_spec,
            pl.BlockSpec((D_MODEL, tf), lambda i, f: (0, f)),
            pl.BlockSpec((D_MODEL, tf), lambda i, f: (0, f)),
            pl.BlockSpec((tf, D_MODEL), lambda i, f: (f, 0)),
        ],
        out_specs=pl.BlockSpec((tm, D_MODEL), lambda i, f: (i, 0)),
        out_shape=jax.ShapeDtypeStruct((n, D_MODEL), F32),
        scratch_shapes=[pltpu.VMEM((tm, D_MODEL), F32)],
        compiler_params=_params(("parallel", "arbitrary")),
        name="ffn_dense",
    )(h2, x1, g2.arr, wg, wu, wd)


def _moe_kernel(h_ref, x_ref, g2_ref, rw_ref, rbias_ref, wg_ref, wu_ref, wd_ref, o_ref, acc_ref, gate_ref):
    e = pl.program_id(1)
    hb = h_ref[...]
    tm = hb.shape[0]
    lane = lax.broadcasted_iota(jnp.int32, (tm, ROUTER_PAD), 1).astype(F32)

    @pl.when(e == 0)
    def _():
        logits = _dot(hb, rw_ref[...]) + rbias_ref[...]
        i1, i2, p1, p2 = _top2(logits, lane)
        gate_ref[...] = jnp.where(lane == i1, p1, 0.0) + jnp.where(lane == i2, p2, 0.0)
        acc_ref[...] = jnp.zeros_like(acc_ref)

    ge = jnp.sum(jnp.where(lane == e.astype(F32), gate_ref[...], 0.0), axis=-1, keepdims=True)
    act = (_silu(_dot(hb, wg_ref[...])) * _dot(hb, wu_ref[...])).astype(BF16)
    acc_ref[...] += ge * _dot(act, wd_ref[...])

    @pl.when(e == pl.num_programs(1) - 1)
    def _():
        o_ref[...] = x_ref[...] + g2_ref[...] * acc_ref[...]


def _moe(h2, x1, g2, rw, rbias, wg, wu, wd, *, tm):
    n = h2.shape[0]
    mod_spec = _mod_spec(tm, n, g2)
    fe = D_FF_EXPERT
    return pl.pallas_call(
        _moe_kernel,
        grid=(n // tm, N_EXPERTS),
        in_specs=[
            pl.BlockSpec((tm, D_MODEL), lambda i, e: (i, 0)),
            pl.BlockSpec((tm, D_MODEL), lambda i, e: (i, 0)),
            mod_spec,
            pl.BlockSpec((D_MODEL, ROUTER_PAD), lambda i, e: (0, 0)),
            pl.BlockSpec((1, ROUTER_PAD), lambda i, e: (0, 0)),
            pl.BlockSpec((None, D_MODEL, fe), lambda i, e: (e, 0, 0)),
            pl.BlockSpec((None, D_MODEL, fe), lambda i, e: (e, 0, 0)),
            pl.BlockSpec((None, fe, D_MODEL), lambda i, e: (e, 0, 0)),
        ],
        out_specs=pl.BlockSpec((tm, D_MODEL), lambda i, e: (i, 0)),
        out_shape=jax.ShapeDtypeStruct((n, D_MODEL), F32),
        scratch_shapes=[pltpu.VMEM((tm, D_MODEL), F32), pltpu.VMEM((tm, ROUTER_PAD), F32)],
        compiler_params=_params(("parallel", "arbitrary")),
        name="moe",
    )(h2, x1, g2.arr, rw, rbias, wg, wu, wd)


def _top2(logits, lane):
    lg = jnp.where(lane < N_EXPERTS, logits, -jnp.inf)
    m1 = jnp.max(lg, axis=-1, keepdims=True)
    i1 = jnp.min(jnp.where(lg == m1, lane, float(ROUTER_PAD)), axis=-1, keepdims=True)
    lg2 = jnp.where(lane == i1, -jnp.inf, lg)
    m2 = jnp.max(lg2, axis=-1, keepdims=True)
    i2 = jnp.min(jnp.where(lg2 == m2, lane, float(ROUTER_PAD)), axis=-1, keepdims=True)
    e2 = jnp.exp(m2 - m1)
    p1 = 1.0 / (1.0 + e2)
    return i1, i2, p1, e2 * p1


def _route_pack(logits, tri):
    lane = lax.broadcasted_iota(jnp.int32, logits.shape, 1).astype(F32)
    i1, i2, p1, p2 = _top2(logits, lane)
    oh1 = jnp.where(lane == i1, 1.0, 0.0)
    oh2 = jnp.where(lane == i2, 1.0, 0.0)
    cnt1 = jnp.sum(oh1, axis=0, keepdims=True)
    cnt2 = jnp.sum(oh2, axis=0, keepdims=True)
    rank1 = jnp.sum(_dot(tri, oh1.astype(BF16)) * oh1, axis=-1, keepdims=True)
    rank2 = jnp.sum((_dot(tri, oh2.astype(BF16)) + cnt1) * oh2, axis=-1, keepdims=True)
    packed = jnp.zeros_like(logits)
    for k, val in enumerate((p1, p2, i1, i2, rank1, rank2)):
        packed = jnp.where(lane == float(k), val, packed)
    return packed, cnt1 + cnt2


def _route_tables(route_t, tile_counts, tm, n_tiles):
    n = route_t.shape[1]
    tile_cnt = tile_counts[:, 0, :N_EXPERTS].astype(jnp.int32)
    cnt = jnp.sum(tile_cnt, axis=0)
    gsz = ((cnt + tm - 1) // tm) * tm
    gend = jnp.cumsum(gsz)
    seg_start = (gend - gsz)[None, :] + jnp.cumsum(tile_cnt, axis=0) - tile_cnt
    seg_of_token = jnp.repeat(seg_start.T, n // tile_cnt.shape[0], axis=1)
    experts = jnp.arange(N_EXPERTS, dtype=jnp.int32)[:, None]
    pos = []
    for k in range(2):
        e_k = route_t[2 + k].astype(jnp.int32)
        rank_k = route_t[4 + k].astype(jnp.int32)
        pos.append(jnp.sum(jnp.where(e_k[None, :] == experts, seg_of_token, 0), axis=0) + rank_k)
    pos = jnp.concatenate(pos)
    tile_start = jnp.arange(n_tiles, dtype=jnp.int32) * tm
    tile_expert = jnp.sum((tile_start[:, None] >= gend[None, :]).astype(jnp.int32), axis=1)
    tile_expert = jnp.minimum(tile_expert, N_EXPERTS - 1)
    live = (gend[-1] // tm).reshape(1)
    return pos, tile_expert, live, gend - gsz + cnt, gend


def _row_copy_wait(src_hbm, dst, sem, rows):
    pltpu.make_async_copy(src_hbm.at[pl.ds(0, rows)], dst, sem).wait()


def _moe_dispatch_kernel(pos_ref, pad_lo_ref, pad_hi_ref, live_ref, h_ref, xs_hbm, rows, zbuf, rsem, zsem):
    i = pl.program_id(0)
    steps = pl.num_programs(0)
    tt = h_ref.shape[0]
    n = tt * steps
    tm = zbuf.shape[0]
    slot = i % 2

    def wait_rows(sl):
        for _ in range(2):
            pltpu.make_async_copy(rows.at[sl], xs_hbm.at[pl.ds(0, tt)], rsem.at[sl]).wait()

    @pl.when(i >= 2)
    def _():
        wait_rows(slot)

    rows[slot] = h_ref[...].astype(F32)
    base = i * tt
    for r in range(tt):
        row = rows.at[slot, pl.ds(r, 1)]
        pltpu.make_async_copy(row, xs_hbm.at[pl.ds(pos_ref[base + r], 1)], rsem.at[slot]).start(priority=0)
        pltpu.make_async_copy(row, xs_hbm.at[pl.ds(pos_ref[n + base + r], 1)], rsem.at[slot]).start(priority=1)

    @pl.when(i == steps - 1)
    def _():
        if steps > 1:
            wait_rows(1 - slot)
        wait_rows(slot)
        zbuf[...] = jnp.zeros(zbuf.shape, F32)

        def fill_row(p):
            return pltpu.make_async_copy(zbuf.at[pl.ds(0, 1)], xs_hbm.at[pl.ds(p, 1)], zsem)

        def fill_tile(t):
            return pltpu.make_async_copy(zbuf, xs_hbm.at[pl.ds(pl.multiple_of(t * tm, tm), tm)], zsem)

        for e in range(N_EXPERTS):
            lo, hi = pad_lo_ref[e], pad_hi_ref[e]
            lax.fori_loop(lo, hi, lambda p, c: (fill_row(p).start(), c)[1], 0)
            lax.fori_loop(lo, hi, lambda p, c: (fill_row(p).wait(), c)[1], 0)
        lo, hi = live_ref[0], xs_hbm.shape[0] // tm
        lax.fori_loop(lo, hi, lambda t, c: (fill_tile(t).start(), c)[1], 0)
        lax.fori_loop(lo, hi, lambda t, c: (fill_tile(t).wait(), c)[1], 0)


def _moe_dispatch(h2, pos, pad_lo, pad_hi, live, *, tt, tm, n_tiles):
    n = h2.shape[0]
    grid_spec = pltpu.PrefetchScalarGridSpec(
        num_scalar_prefetch=4,
        grid=(n // tt,),
        in_specs=[pl.BlockSpec((tt, D_MODEL), lambda i, *_: (i, 0))],
        out_specs=pl.BlockSpec(memory_space=pl.ANY),
        scratch_shapes=[pltpu.VMEM((2, tt, D_MODEL), F32), pltpu.VMEM((tm, D_MODEL), F32),
                        pltpu.SemaphoreType.DMA((2,)), pltpu.SemaphoreType.DMA(())],
    )
    return pl.pallas_call(
        _moe_dispatch_kernel,
        grid_spec=grid_spec,
        out_shape=jax.ShapeDtypeStruct((n_tiles * tm, D_MODEL), F32),
        compiler_params=_params(("arbitrary",)),
        name="moe_dispatch",
    )(pos, pad_lo, pad_hi, live, h2)


def _moe_expert_kernel(te_ref, live_ref, x_ref, wg_ref, wu_ref, wd_ref, y_ref):
    del te_ref
    is_live = pl.program_id(0) < live_ref[0]

    @pl.when(is_live)
    def _():
        y_ref[...] = _swiglu(x_ref[...].astype(BF16), wg_ref, wu_ref, wd_ref)

    @pl.when(jnp.logical_not(is_live))
    def _():
        y_ref[...] = jnp.zeros(y_ref.shape, F32)


def _moe_experts(xs, tile_expert, live, wg, wu, wd, *, tm, n_tiles):
    fe = D_FF_EXPERT

    def in_tile(t, te, live):
        return (jnp.minimum(t, live[0] - 1), 0)

    def out_tile(t, te, live):
        return (t, 0)

    def expert(t, te, live):
        return (te[t], 0, 0)

    grid_spec = pltpu.PrefetchScalarGridSpec(
        num_scalar_prefetch=2,
        grid=(n_tiles,),
        in_specs=[
            pl.BlockSpec((tm, D_MODEL), in_tile),
            pl.BlockSpec((None, D_MODEL, fe), expert),
            pl.BlockSpec((None, D_MODEL, fe), expert),
            pl.BlockSpec((None, fe, D_MODEL), expert),
        ],
        out_specs=pl.BlockSpec((tm, D_MODEL), out_tile),
    )
    return pl.pallas_call(
        _moe_expert_kernel,
        grid_spec=grid_spec,
        out_shape=jax.ShapeDtypeStruct(xs.shape, F32),
        compiler_params=_params(("arbitrary",)),
        name="moe_experts",
    )(tile_expert, live, xs, wg, wu, wd)


def _moe_combine_kernel(pos_ref, x_ref, g2_ref, r_ref, ys_hbm, o_ref, buf, sem):
    i = pl.program_id(0)
    nt = pl.num_programs(0)
    tt = x_ref.shape[0]
    n = nt * tt

    def fetch(tile, sl):
        base = tile * tt
        for r in range(tt):
            for s in range(2):
                row = pos_ref[s * n + base + r]
                pltpu.make_async_copy(ys_hbm.at[pl.ds(row, 1)], buf.at[sl, pl.ds(s * tt + r, 1)],
                                      sem.at[sl]).start(priority=s)

    @pl.when(i == 0)
    def _():
        fetch(0, 0)

    @pl.when(i + 1 < nt)
    def _():
        fetch(i + 1, (i + 1) % 2)

    sl = i % 2
    _row_copy_wait(ys_hbm, buf.at[sl], sem.at[sl], 2 * tt)
    r = r_ref[...]
    f = r[:, 0:1] * buf[sl, pl.ds(0, tt), :] + r[:, 1:2] * buf[sl, pl.ds(tt, tt), :]
    o_ref[...] = x_ref[...] + g2_ref[...] * f


def _moe_combine(x1, g2, route, pos, ys, *, tt):
    n = x1.shape[0]
    grid_spec = pltpu.PrefetchScalarGridSpec(
        num_scalar_prefetch=1,
        grid=(n // tt,),
        in_specs=[
            pl.BlockSpec((tt, D_MODEL), lambda i, pos: (i, 0)),
            _mod_spec(tt, n, g2),
            pl.BlockSpec((tt, ROUTER_PAD), lambda i, pos: (i, 0)),
            pl.BlockSpec(memory_space=pl.ANY),
        ],
        out_specs=pl.BlockSpec((tt, D_MODEL), lambda i, pos: (i, 0)),
        scratch_shapes=[pltpu.VMEM((2, 2 * tt, D_MODEL), F32), pltpu.SemaphoreType.DMA((2,))],
    )
    return pl.pallas_call(
        _moe_combine_kernel,
        grid_spec=grid_spec,
        out_shape=jax.ShapeDtypeStruct((n, D_MODEL), F32),
        compiler_params=_params(("arbitrary",)),
        name="moe_combine",
    )(pos, x1, g2.arr, route, ys)


def _moe_routed(h2, x1, g2, route, route_t, tile_counts, wg, wu, wd, *, tm, tt):
    n = h2.shape[0]
    n_tiles = (2 * n) // tm + N_EXPERTS
    pos, tile_expert, live, pad_lo, pad_hi = _route_tables(route_t, tile_counts, tm, n_tiles)
    xs = _moe_dispatch(h2, pos, pad_lo, pad_hi, live, tt=tt, tm=tm, n_tiles=n_tiles)
    ys = _moe_experts(xs, tile_expert, live, wg, wu, wd, tm=tm, n_tiles=n_tiles)
    return _moe_combine(x1, g2, route, pos, ys, tt=tt)


def _head_perm():
    idx = []
    for g in range(GROUP):
        for kv in range(N_KV):
            h = kv * GROUP + g
            idx.extend(range(h * HEAD_DIM, (h + 1) * HEAD_DIM))
    return np.asarray(idx, np.int32)


def _relayout_w_in(w):
    pts = np.cumsum([0, Q_W, KV_W, KV_W, GK_W, GK_W, GV_W, GATE_RANK, GV_W, D_MODEL, D_MODEL])
    qa, ka, va, qb, kb, vb, ga, rb, gta, gtb = [w[:, pts[i]:pts[i + 1]] for i in range(10)]
    qa = qa[:, _head_perm()]
    ga = jnp.pad(ga, ((0, 0), (0, GATE_PAD - GATE_RANK)))
    return jnp.concatenate([qa, ka, va, qb, kb, vb, rb, gta, gtb, ga], axis=1).astype(BF16)


def kernel(x_prompt, x_sample, cache_k, cache_v, state_gla, c_prompt, c_sample, ada_w, ada_b, norm1_g, norm2_g, w_in, q_norm_g, k_norm_g, attn_sinks, gla_wa2, gla_ba, gla_norm_g, w_branch_a, w_branch_b, w_out, ffn_w_gate, ffn_w_up, ffn_w_down, router_w, router_b, moe_w_gate, moe_w_up, moe_w_down):
    n_p = BATCH * SEQ
    xp = x_prompt.reshape(n_p, D_MODEL)
    xs = x_sample.reshape(DEC_BATCH, D_MODEL)

    c_pad = -(BATCH + DEC_BATCH) % 8
    c_all = jnp.pad(jnp.concatenate([c_sample, c_prompt], axis=0), ((0, c_pad), (0, 0)))
    mod_sample, mod_prompt = _ada(c_all, ada_w, ada_b, DEC_BATCH, BATCH)
    state_rows = state_gla.reshape(DEPTH, DEC_BATCH, GK_W, GLA_DV)

    bd = jnp.asarray(np.kron(np.eye(N_HEADS), np.ones((HEAD_DIM, HEAD_DIM))), BF16)
    ut = jnp.asarray(_gla_cum_matrix(), BF16)
    gla_upper, gla_pairs = (jnp.asarray(m) for m in _gla_level_masks())
    perm = _head_perm()
    swa_bias = jnp.asarray(_swa_sample_bias())
    swa_prompt_bias = jnp.asarray(_swa_prompt_bias())

    kp_l, vp_l, sp_l, ks_l, vs_l, ss_l = [], [], [], [], [], []
    for l in range(DEPTH):
        mod_p = [_Mod(mod_prompt, l, i) for i in range(N_MOD)]
        mod_s = [_Mod(mod_sample, l, i) for i in range(N_MOD)]

        w = _relayout_w_in(w_in[l])
        qg = (jnp.tile(q_norm_g[l], N_HEADS) * (HEAD_DIM ** -0.5)).reshape(1, Q_W)
        kg = jnp.tile(k_norm_g[l], N_KV).reshape(1, KV_W)
        wa2 = jnp.pad(gla_wa2[l], ((0, GATE_PAD - GATE_RANK), (0, 0))).astype(BF16)
        ba = gla_ba[l].reshape(1, GK_W)
        n1 = norm1_g[l].reshape(1, D_MODEL)
        n2 = norm2_g[l].reshape(1, D_MODEL)
        gn = gla_norm_g[l].reshape(1, GLA_DV)
        wpa = w_branch_a[l][perm].astype(BF16)
        wpb = w_branch_b[l].astype(BF16)
        wo = w_out[l].astype(BF16)
        sink_rows = jnp.broadcast_to(attn_sinks[l][_sample_head_of_row()][:, None], (N_HEADS, LANES))

        q, k, v, gq, gk, gv, la, rbs, sga, sgb = _mixin(
            xp, mod_p[0], mod_p[1], n1, w, bd, qg, kg, wa2, ba, tm=ROW_TILE_RESIDENT)
        a_out = _swa_prompt(attn_sinks[l], q, k, v, swa_prompt_bias)
        g_out, s_fin = _gla_prompt(gq, gk, la, gv, rbs, gn, ut, gla_upper, gla_pairs)
        is_moe = l % 2 == 1
        if is_moe:
            rw = jnp.pad(router_w[l // 2], ((0, 0), (0, ROUTER_PAD - N_EXPERTS))).astype(BF16)
            rbias = jnp.pad(router_b[l // 2], (0, ROUTER_PAD - N_EXPERTS)).reshape(1, ROUTER_PAD)
        merged = _merge(xp, a_out, g_out, sga, sgb, wpa, wpb, wo, mod_p[2], mod_p[3], mod_p[4], n2,
                        tm=ROW_TILE_RESIDENT, router=(rw, rbias) if is_moe else None)
        x1, h2 = merged[0], merged[1]
        for cache, full in ((kp_l, k), (vp_l, v)):
            tail = full.reshape(BATCH, SEQ, KV_W)[:, SEQ - WINDOW:]
            cache.append(tail.reshape(BATCH, WINDOW, N_KV, HEAD_DIM))
        sp_l.append(s_fin.reshape(BATCH, GLA_HEADS, GLA_DK, GLA_DV))

        qs, ksn, vsn, gqs, gks, gvs, las, rbss, sgas, sgbs = _mixin(
            xs, mod_s[0], mod_s[1], n1, w, bd, qg, kg, wa2, ba, tm=DEC_BATCH)
        a_s, nk, nv = _swa_sample(
            qs.astype(F32).reshape(DEC_BATCH, N_HEADS, HEAD_DIM),
            ksn.reshape(DEC_BATCH, N_KV, HEAD_DIM), vsn.reshape(DEC_BATCH, N_KV, HEAD_DIM),
            cache_k[l].reshape(1, DEC_BATCH, CACHE_ROWS, HEAD_DIM),
            cache_v[l].reshape(1, DEC_BATCH, CACHE_ROWS, HEAD_DIM), swa_bias, sink_rows, layer=0)
        g_s, s_new = _gla_sample(gqs, gks, las, gvs, rbss, gn, state_rows, layer=l)
        x1s, h2s = _merge(xs, a_s.reshape(DEC_BATCH, Q_W), g_s, sgas, sgbs, wpa, wpb, wo,
                          mod_s[2], mod_s[3], mod_s[4], n2, tm=DEC_BATCH)
        ks_l.append(nk.reshape(DEC_BATCH, WINDOW, N_KV, HEAD_DIM))
        vs_l.append(nv.reshape(DEC_BATCH, WINDOW, N_KV, HEAD_DIM))
        ss_l.append(s_new.reshape(DEC_BATCH, GLA_HEADS, GLA_DK, GLA_DV))

        i = l // 2
        if not is_moe:
            wg, wu, wd = ffn_w_gate[i].astype(BF16), ffn_w_up[i].astype(BF16), ffn_w_down[i].astype(BF16)
            xp = _ffn(h2, x1, mod_p[5], wg, wu, wd, tm=ROW_TILE_SWIGLU, tf=D_FF // 2)
            xs = _ffn(h2s, x1s, mod_s[5], wg, wu, wd, tm=DEC_BATCH, tf=D_FF // 2)
        else:
            wg, wu, wd = moe_w_gate[i].astype(BF16), moe_w_up[i].astype(BF16), moe_w_down[i].astype(BF16)
            xp = _moe_routed(h2, x1, mod_p[5], *merged[2:5], wg, wu, wd,
                             tm=ROW_TILE_SWIGLU, tt=MOE_TOKEN_TILE)
            xs = _moe(h2s, x1s, mod_s[5], rw, rbias, wg, wu, wd, tm=DEC_BATCH)

    return (xp.reshape(BATCH, SEQ, D_MODEL), xs.reshape(DEC_BATCH, 1, D_MODEL),
            jnp.stack(kp_l), jnp.stack(vp_l), jnp.stack(sp_l),
            jnp.stack(ks_l), jnp.stack(vs_l), jnp.stack(ss_l))
```

```python
from typing import NamedTuple

import jax
import jax.numpy as jnp
import numpy as np
from jax import lax
from jax.experimental import pallas as pl
from jax.experimental.pallas import tpu as pltpu

D_MODEL = 1024
BATCH = 4
SEQ = 4096
DEPTH = 2
DEC_BATCH = 128
N_HEADS = 8
N_KV = 2
HEAD_DIM = 64
GROUP = N_HEADS // N_KV
WINDOW = 128
GLA_HEADS = 4
GLA_DK = 64
GLA_DV = 128
GATE_RANK = 16
GATE_TAU = 16.0
D_FF = 2816
N_EXPERTS = 8
D_FF_EXPERT = 1408
EPS = 1e-6

Q_W = N_HEADS * HEAD_DIM
KV_W = N_KV * HEAD_DIM
GK_W = GLA_HEADS * GLA_DK
GV_W = GLA_HEADS * GLA_DV

LANES = 128
GATE_PAD = LANES
ROUTER_PAD = LANES
VMEM_LIMIT = 56 * 1024 * 1024

ROW_TILE_RESIDENT = 1024
ROW_TILE_SWIGLU = 512
MOE_TOKEN_TILE = 512

F32 = jnp.float32
BF16 = jnp.bfloat16

_C_Q = 0
_C_K = _C_Q + Q_W
_C_V = _C_K + KV_W
_C_GQ = _C_V + KV_W
_C_GK = _C_GQ + GK_W
_C_GV = _C_GK + GK_W
_C_RB = _C_GV + GV_W
_C_GA = _C_RB + GV_W
_C_GB = _C_GA + D_MODEL
_C_LR = _C_GB + D_MODEL
PROJ_PAD = _C_LR + GATE_PAD

GLA_CHUNK = 128
GLA_LEVELS = 7
GLA_MXU_LEVELS = 3
GLA_CHUNKS_PER_STEP = 4


def _params(sem, vmem=VMEM_LIMIT):
    return pltpu.CompilerParams(dimension_semantics=sem, vmem_limit_bytes=vmem)


def _dot(a, b):
    return jnp.dot(a, b, preferred_element_type=F32)


def _dot_nt(a, b):
    return lax.dot_general(a, b, (((1,), (1,)), ((), ())), preferred_element_type=F32)


def _dot_tn(a, b):
    return lax.dot_general(a, b, (((0,), (0,)), ((), ())), preferred_element_type=F32)


def _sigmoid(x):
    return 0.5 * jnp.tanh(0.5 * x) + 0.5


def _silu(x):
    return x * _sigmoid(x)


N_MOD = 6


class _Mod(NamedTuple):
    arr: jax.Array
    layer: int
    idx: int


def _ada_kernel(c_ref, w_ref, b_ref, os_ref, op_ref):
    c = c_ref[...]
    mod = _dot(_silu(c).astype(BF16), w_ref[...].astype(BF16)) + b_ref[...]
    n_s = os_ref.shape[0]
    os_ref[...] = mod[0:n_s]
    for b in range(op_ref.shape[0]):
        op_ref[b] = mod[n_s + b:n_s + b + 1]


def _ada(c_all, ada_w, ada_b, n_sample, n_prompt):
    rows = c_all.shape[0]
    return pl.pallas_call(
        _ada_kernel,
        grid=(DEPTH, N_MOD),
        in_specs=[
            pl.BlockSpec((rows, D_MODEL), lambda l, j: (0, 0)),
            pl.BlockSpec((None, D_MODEL, D_MODEL), lambda l, j: (l, 0, j)),
            pl.BlockSpec((None, 1, D_MODEL), lambda l, j: (l, 0, j)),
        ],
        out_specs=[
            pl.BlockSpec((None, None, n_sample, D_MODEL), lambda l, j: (l, j, 0, 0)),
            pl.BlockSpec((None, None, n_prompt, 1, D_MODEL), lambda l, j: (l, j, 0, 0, 0)),
        ],
        out_shape=[
            jax.ShapeDtypeStruct((DEPTH, N_MOD, n_sample, D_MODEL), F32),
            jax.ShapeDtypeStruct((DEPTH, N_MOD, n_prompt, 1, D_MODEL), F32),
        ],
        compiler_params=_params(("parallel", "parallel")),
        name="ada_mod",
    )(c_all, ada_w, ada_b.reshape(DEPTH, 1, N_MOD * D_MODEL))


def _mixin_kernel(x_ref, sh_ref, sc_ref, n1_ref, w_ref, bd_ref, qg_ref, kg_ref, wa2_ref, ba_ref,
                  q_ref, k_ref, v_ref, gq_ref, gk_ref, gv_ref, la_ref, rb_ref, sga_ref, sgb_ref):
    x = x_ref[...]
    ms = jnp.mean(x * x, axis=-1, keepdims=True)
    h = x * lax.rsqrt(ms + EPS) * n1_ref[...]
    h = h * (1.0 + sc_ref[...]) + sh_ref[...]
    hb = h.astype(BF16)

    def proj(a, b):
        return _dot(hb, w_ref[:, a:b])

    q = proj(_C_Q, _C_K)
    ssq = _dot((q * q).astype(BF16), bd_ref[...])
    q_ref[...] = (q * lax.rsqrt(ssq * (1.0 / HEAD_DIM) + EPS) * qg_ref[...]).astype(BF16)
    k = proj(_C_K, _C_V)
    ssk = _dot((k * k).astype(BF16), bd_ref[0:KV_W, 0:KV_W])
    k_ref[...] = k * lax.rsqrt(ssk * (1.0 / HEAD_DIM) + EPS) * kg_ref[...]
    v_ref[...] = proj(_C_V, _C_GQ)
    gq_ref[...] = proj(_C_GQ, _C_GK) * (GLA_DK ** -0.5)
    gk_ref[...] = proj(_C_GK, _C_GV)
    gv_ref[...] = proj(_C_GV, _C_RB)
    rb_ref[...] = _silu(proj(_C_RB, _C_GA)).astype(BF16)
    sga_ref[...] = _sigmoid(proj(_C_GA, _C_GB)).astype(BF16)
    sgb_ref[...] = _sigmoid(proj(_C_GB, _C_LR)).astype(BF16)
    ga = proj(_C_LR, PROJ_PAD)
    xg = _dot(ga.astype(BF16), wa2_ref[...]) + ba_ref[...]
    la_ref[...] = (jnp.minimum(xg, 0.0) - jnp.log1p(jnp.exp(-jnp.abs(xg)))) * (1.0 / GATE_TAU)


def _mixin(x, sh, sc, n1, w, bd, qg, kg, wa2, ba, *, tm):
    n = x.shape[0]
    nt = n // tm

    def row(i):
        return (i, 0)

    def const(shape):
        return pl.BlockSpec(shape, lambda i: (0,) * len(shape), pipeline_mode=pl.Buffered(1))

    def out(width, dtype):
        return pl.BlockSpec((tm, width), row), jax.ShapeDtypeStruct((n, width), dtype)

    outs = [out(Q_W, BF16), out(KV_W, F32), out(KV_W, F32), out(GK_W, F32), out(GK_W, F32),
            out(GV_W, F32), out(GK_W, F32), out(GV_W, BF16), out(D_MODEL, BF16), out(D_MODEL, BF16)]
    return pl.pallas_call(
        _mixin_kernel,
        grid=(nt,),
        in_specs=[
            pl.BlockSpec((tm, D_MODEL), row), _mod_spec(tm, n, sh), _mod_spec(tm, n, sc), const((1, D_MODEL)),
            const((D_MODEL, PROJ_PAD)), const((Q_W, Q_W)), const((1, Q_W)), const((1, KV_W)),
            const((GATE_PAD, GK_W)), const((1, GK_W)),
        ],
        out_specs=[o[0] for o in outs],
        out_shape=[o[1] for o in outs],
        compiler_params=_params(("parallel",)),
        name="mixer_in",
    )(x, sh.arr, sc.arr, n1, w, bd, qg, kg, wa2, ba)


SWA_BLOCKS = 8


def _head_slope(h):
    return float(2.0 ** (-8.0 * (h + 1) / N_HEADS))


def _swa_prompt_bias():
    blk = WINDOW
    dist = np.arange(blk)[:, None] + blk - np.arange(2 * blk)[None, :]
    slopes = np.asarray([_head_slope(h) for h in range(N_HEADS)])[:, None, None]
    return np.where((dist >= 0) & (dist <= WINDOW), -slopes * dist, -np.inf).astype(np.float32)


def _swa_prompt_kernel(sink_ref, q_ref, kp_ref, kc_ref, vp_ref, vc_ref, bias_ref, o_ref):
    n = pl.program_id(1)
    blk = WINDOW
    kall = jnp.concatenate([kp_ref[...], kc_ref[...]], axis=0).astype(BF16)
    vall = jnp.concatenate([vp_ref[...], vc_ref[...]], axis=0).astype(BF16)
    col = lax.broadcasted_iota(jnp.int32, (blk, 2 * blk), 1)
    first_key = jnp.where(n > 0, 0, blk)
    klane = lax.broadcasted_iota(jnp.int32, kall.shape, 1)
    kall_kv = [jnp.where((klane // HEAD_DIM) == kv, kall, jnp.zeros_like(kall)) for kv in range(N_KV)]
    olane = lax.broadcasted_iota(jnp.int32, (blk, KV_W), 1)
    q = q_ref[...]
    for j in range(SWA_BLOCKS):
        vv = vall[j * blk:(j + 2) * blk]
        outs = []
        for g in range(GROUP):
            qp = q[j * blk:(j + 1) * blk, g * KV_W:(g + 1) * KV_W]
            pair = []
            for kv in range(N_KV):
                h = kv * GROUP + g
                s = _dot_nt(qp, kall_kv[kv][j * blk:(j + 2) * blk]) + bias_ref[h]
                if j == 0:
                    s = jnp.where(col >= first_key, s, -jnp.inf)
                sink = sink_ref[h]
                m = jnp.maximum(jnp.max(s, axis=-1, keepdims=True), sink)
                p = jnp.exp(s - m)
                den = jnp.sum(p, axis=-1, keepdims=True) + jnp.exp(sink - m)
                pair.append(_dot(p.astype(BF16), vv) * (1.0 / den))
            outs.append(jnp.where(olane < HEAD_DIM, pair[0], pair[1]))
        o_ref[j * blk:(j + 1) * blk, :] = jnp.concatenate(outs, axis=-1).astype(BF16)


def _swa_prompt(sinks, q, k, v, bias):
    nb = SEQ // WINDOW
    steps = nb // SWA_BLOCKS
    tq = SWA_BLOCKS * WINDOW

    def cur(b, n):
        return (b * steps + n, 0)

    def prev(b, n):
        return (b * nb + jnp.maximum(n * SWA_BLOCKS - 1, 0), 0)

    return pl.pallas_call(
        _swa_prompt_kernel,
        grid=(BATCH, steps),
        in_specs=[
            pl.BlockSpec(memory_space=pltpu.SMEM),
            pl.BlockSpec((tq, Q_W), cur),
            pl.BlockSpec((WINDOW, KV_W), prev), pl.BlockSpec((tq, KV_W), cur),
            pl.BlockSpec((WINDOW, KV_W), prev), pl.BlockSpec((tq, KV_W), cur),
            pl.BlockSpec(bias.shape, lambda b, n: (0, 0, 0), pipeline_mode=pl.Buffered(1)),
        ],
        out_specs=pl.BlockSpec((tq, Q_W), cur),
        out_shape=jax.ShapeDtypeStruct((BATCH * SEQ, Q_W), BF16),
        compiler_params=_params(("parallel", "parallel")),
        name="swa_prompt",
    )(sinks, q, k, k, v, v, bias)


SAMPLE_TB = 8
CACHE_ROWS = WINDOW * N_KV


def _sample_head_of_row():
    j = np.arange(N_HEADS)
    return (j % N_KV) * GROUP + j // N_KV


def _swa_sample_bias():
    j = np.arange(N_HEADS)[:, None]
    c = np.arange(CACHE_ROWS)[None, :]
    slope = 2.0 ** (-8.0 * (_sample_head_of_row()[:, None] + 1) / N_HEADS)
    bias = -slope * (WINDOW - c // N_KV)
    return np.where(c % N_KV == j % N_KV, bias, -np.inf).astype(np.float32)


def _swa_sample_kernel(q_ref, kn_ref, vn_ref, ck_ref, cv_ref, bias_ref, sk_ref, o_ref, ok_ref, ov_ref):
    rows = CACHE_ROWS
    sink = sk_ref[...][:, 0:1]
    q = q_ref[...]
    kn = kn_ref[...]
    vn = vn_ref[...]
    kc = ck_ref[...]
    vc = cv_ref[...]
    kn8 = jnp.concatenate([kn] * GROUP, axis=1)
    vn8 = jnp.concatenate([vn] * GROUP, axis=1)
    s = lax.dot_general(q.astype(BF16), kc.astype(BF16), (((2,), (2,)), ((0,), (0,))),
                        preferred_element_type=F32) + bias_ref[...]
    s_new = jnp.sum(q * kn8, axis=-1, keepdims=True)
    m = jnp.maximum(jnp.maximum(jnp.max(s, axis=-1, keepdims=True), s_new), sink)
    p = jnp.exp(s - m)
    p_new = jnp.exp(s_new - m)
    den = jnp.sum(p, axis=-1, keepdims=True) + p_new + jnp.exp(sink - m)
    o = lax.dot_general(p.astype(BF16), vc.astype(BF16), (((2,), (1,)), ((0,), (0,))),
                        preferred_element_type=F32) + p_new * vn8
    o_ref[...] = o * (1.0 / den)
    ok_ref[:, pl.ds(0, rows - N_KV), :] = kc[:, N_KV:, :]
    ok_ref[:, pl.ds(rows - N_KV, N_KV), :] = kn
    ov_ref[:, pl.ds(0, rows - N_KV), :] = vc[:, N_KV:, :]
    ov_ref[:, pl.ds(rows - N_KV, N_KV), :] = vn


def _swa_sample(q, kn, vn, ck, cv, bias, sinks, *, layer):
    tb = 2 * SAMPLE_TB
    nb = DEC_BATCH // tb

    def const(shape):
        return pl.BlockSpec(shape, lambda i: (0,) * len(shape))

    def per_seq(*dims):
        return pl.BlockSpec((tb,) + dims, lambda i: (i, 0, 0))

    cache_spec = pl.BlockSpec((None, tb, CACHE_ROWS, HEAD_DIM), lambda i: (layer, i, 0, 0))
    return pl.pallas_call(
        _swa_sample_kernel,
        grid=(nb,),
        in_specs=[per_seq(N_HEADS, HEAD_DIM), per_seq(N_KV, HEAD_DIM), per_seq(N_KV, HEAD_DIM),
                  cache_spec, cache_spec,
                  const((N_HEADS, CACHE_ROWS)), const((N_HEADS, LANES))],
        out_specs=[per_seq(N_HEADS, HEAD_DIM), per_seq(CACHE_ROWS, HEAD_DIM), per_seq(CACHE_ROWS, HEAD_DIM)],
        out_shape=[
            jax.ShapeDtypeStruct((DEC_BATCH, N_HEADS, HEAD_DIM), F32),
            jax.ShapeDtypeStruct((DEC_BATCH, CACHE_ROWS, HEAD_DIM), F32),
            jax.ShapeDtypeStruct((DEC_BATCH, CACHE_ROWS, HEAD_DIM), F32),
        ],
        compiler_params=_params(("parallel",)),
        name="swa_sample",
    )(q, kn, vn, ck, cv, bias, sinks)


def _gla_cum_matrix():
    c = GLA_CHUNK
    tri = np.tril(np.ones((c, c), np.float32))
    i = np.arange(c)
    blocks = []
    for lvl in range(GLA_MXU_LEVELS):
        half = 1 << lvl
        mid = (i // (2 * half)) * (2 * half) + half - 1
        blocks.append(tri - tri[mid])
    blocks.append(tri)
    return np.concatenate(blocks, axis=0)


def _split3(x):
    hi = x.astype(BF16)
    r1 = x - hi.astype(F32)
    mid = r1.astype(BF16)
    lo = (r1 - mid.astype(F32)).astype(BF16)
    return hi, mid, lo


def _gla_level_masks():
    cl = GLA_CHUNK
    r = np.arange(cl)
    upper = np.stack([np.broadcast_to(((r >> lvl) & 1)[:, None], (cl, GK_W)) for lvl in range(GLA_LEVELS)])
    ri = np.tile(r, GLA_HEADS)[:, None]
    pairs = [(ri >> (lvl + 1)) == (r[None, :] >> (lvl + 1)) for lvl in range(GLA_LEVELS)]
    pairs.append(ri == r[None, :])
    return upper.astype(np.float32), np.stack(pairs).astype(np.float32)


def _gla_prompt_kernel(q_ref, k_ref, la_ref, v_ref, rb_ref, gn_ref, ut_ref, up_ref, pm_ref,
                       o_ref, s_ref, st_ref):
    c = pl.program_id(1)
    cl = GLA_CHUNK

    @pl.when(c == 0)
    def _():
        st_ref[0] = jnp.zeros(st_ref.shape[1:], F32)

    ut = ut_ref[...]
    lane = lax.broadcasted_iota(jnp.int32, (cl, GK_W), 1)
    head_of_lane = lane // GLA_DK
    ones = jnp.ones((cl, LANES), BF16)
    state = [st_ref[c % 2, h] for h in range(GLA_HEADS)]
    for cc in range(GLA_CHUNKS_PER_STEP):
        rows = slice(cc * cl, (cc + 1) * cl)
        hi, mid, lo = _split3(la_ref[rows, :])
        tall = _dot(ut, hi) + _dot(ut, mid) + _dot(ut, lo)
        q = q_ref[rows, :]
        k = k_ref[rows, :]
        attn_all = jnp.zeros((GLA_HEADS * cl, cl), F32)
        cum = tall[GLA_MXU_LEVELS * cl:(GLA_MXU_LEVELS + 1) * cl]
        for lvl in range(GLA_LEVELS + 1):
            if lvl < GLA_LEVELS:
                if lvl < GLA_MXU_LEVELS:
                    t_lvl = tall[lvl * cl:(lvl + 1) * cl]
                else:
                    half = 1 << lvl
                    mids = [jnp.broadcast_to(cum[b0 + half - 1:b0 + half, :], (2 * half, GK_W))
                            for b0 in range(0, cl, 2 * half)]
                    t_lvl = cum - (mids[0] if len(mids) == 1 else jnp.concatenate(mids, axis=0))
                e = jnp.exp(-jnp.abs(t_lvl))
                e_up = e * up_ref[lvl]
                qt = (q * e_up).astype(BF16)
                kt = (k * (e - e_up)).astype(BF16)
            else:
                qt = q.astype(BF16)
                kt = k.astype(BF16)
            q_heads = jnp.concatenate(
                [jnp.where(head_of_lane == h, qt, jnp.zeros_like(qt)) for h in range(GLA_HEADS)], axis=0)
            attn_all = attn_all + _dot_nt(q_heads, kt) * pm_ref[lvl]

        last = cum[cl - 1:cl, :]
        qe = (q * jnp.exp(cum)).astype(BF16)
        kd = (k * jnp.exp(last - cum)).astype(BF16)
        dec = jnp.exp(_dot_tn(hi, ones) + _dot_tn(mid, ones) + _dot_tn(lo, ones))
        v = v_ref[rows, :]
        rb = rb_ref[rows, :]
        outs = []
        for h in range(GLA_HEADS):
            vh = v[:, h * GLA_DV:(h + 1) * GLA_DV].astype(BF16)
            o = _dot(attn_all[h * cl:(h + 1) * cl].astype(BF16), vh) + _dot(qe, state[h].astype(BF16))
            kdh = jnp.where(head_of_lane == h, kd, jnp.zeros_like(kd))
            state[h] = state[h] * dec + _dot_tn(kdh, vh)
            ms = jnp.mean(o * o, axis=-1, keepdims=True)
            g = o * lax.rsqrt(ms + EPS) * gn_ref[...]
            outs.append(g * rb[:, h * GLA_DV:(h + 1) * GLA_DV].astype(F32))
        o_ref[rows, :] = jnp.concatenate(outs, axis=-1).astype(BF16)

    for h in range(GLA_HEADS):
        st_ref[(c + 1) % 2, h] = state[h]

    @pl.when(c == pl.num_programs(1) - 1)
    def _():
        fin = (SEQ // (GLA_CHUNK * GLA_CHUNKS_PER_STEP)) % 2
        s_ref[...] = st_ref[fin, 0] + st_ref[fin, 1] + st_ref[fin, 2] + st_ref[fin, 3]


def _gla_prompt(gq, gk, la, gv, rbs, gn, ut, upper, pairs):
    cl = GLA_CHUNK * GLA_CHUNKS_PER_STEP
    nc = SEQ // cl

    def row(b, c):
        return (b * nc + c, 0)

    def const(shape):
        return pl.BlockSpec(shape, lambda b, c: (0,) * len(shape), pipeline_mode=pl.Buffered(1))

    return pl.pallas_call(
        _gla_prompt_kernel,
        grid=(BATCH, nc),
        in_specs=[
            pl.BlockSpec((cl, GK_W), row), pl.BlockSpec((cl, GK_W), row), pl.BlockSpec((cl, GK_W), row),
            pl.BlockSpec((cl, GV_W), row), pl.BlockSpec((cl, GV_W), row),
            const((1, GLA_DV)), const(ut.shape), const(upper.shape), const(pairs.shape),
        ],
        out_specs=[
            pl.BlockSpec((cl, GV_W), row),
            pl.BlockSpec((None, GK_W, GLA_DV), lambda b, c: (b, 0, 0)),
        ],
        out_shape=[
            jax.ShapeDtypeStruct((BATCH * SEQ, GV_W), BF16),
            jax.ShapeDtypeStruct((BATCH, GK_W, GLA_DV), F32),
        ],
        scratch_shapes=[pltpu.VMEM((2, GLA_HEADS, GK_W, GLA_DV), F32)],
        compiler_params=_params(("parallel", "arbitrary")),
        name="gla_prompt",
    )(gq, gk, la, gv, rbs, gn, ut, upper, pairs)


def _gla_sample_kernel(q_ref, k_ref, la_ref, v_ref, rb_ref, gn_ref, s_ref, o_ref, so_ref):
    tb = SAMPLE_TB
    dec = jnp.exp(la_ref[...])
    pieces = []
    for x in (dec, k_ref[...], q_ref[...]):
        hi, mid, lo = _split3(x)
        stacked = jnp.concatenate(
            [hi.astype(F32), mid.astype(F32), lo.astype(F32), jnp.zeros_like(x)], axis=0)
        pieces.append(stacked.astype(BF16))
    prow = lax.broadcasted_iota(jnp.int32, (4 * tb, LANES), 0)
    rb = rb_ref[...].astype(F32)
    v = v_ref[...]
    for bi in range(tb):
        sel = jnp.where((prow % tb) == bi, 1.0, 0.0).astype(BF16)
        a_col, k_col, q_col = [_dot_tn(p, sel) for p in pieces]
        for h in range(GLA_HEADS):
            rs = slice(h * GLA_DK, (h + 1) * GLA_DK)
            vs = slice(h * GLA_DV, (h + 1) * GLA_DV)
            s_new = a_col[rs] * s_ref[bi, rs, :] + k_col[rs] * v[bi:bi + 1, vs]
            so_ref[bi, rs, :] = s_new
            o = jnp.sum(q_col[rs] * s_new, axis=0, keepdims=True)
            ms = jnp.mean(o * o, axis=-1, keepdims=True)
            g = o * lax.rsqrt(ms + EPS) * gn_ref[...]
            o_ref[bi:bi + 1, vs] = g * rb[bi:bi + 1, vs]


def _gla_sample(gq, gk, la, gv, rbs, gn, state, *, layer):
    tb = SAMPLE_TB
    nb = DEC_BATCH // tb

    def row(w):
        return pl.BlockSpec((tb, w), lambda i: (i, 0))

    st_spec = pl.BlockSpec((tb, GK_W, GLA_DV), lambda i: (i, 0, 0))
    st_in_spec = pl.BlockSpec((None, tb, GK_W, GLA_DV), lambda i: (layer, i, 0, 0))
    return pl.pallas_call(
        _gla_sample_kernel,
        grid=(nb,),
        in_specs=[row(GK_W), row(GK_W), row(GK_W), row(GV_W), row(GV_W),
                  pl.BlockSpec((1, GLA_DV), lambda i: (0, 0)), st_in_spec],
        out_specs=[row(GV_W), st_spec],
        out_shape=[
            jax.ShapeDtypeStruct((DEC_BATCH, GV_W), F32),
            jax.ShapeDtypeStruct((DEC_BATCH, GK_W, GLA_DV), F32),
        ],
        compiler_params=_params(("parallel",)),
        name="gla_sample",
    )(gq, gk, la, gv, rbs, gn, state)


def _merge_kernel(x_ref, a_ref, g_ref, sga_ref, sgb_ref, wpa_ref, wpb_ref, wo_ref,
                  g1_ref, sh_ref, sc_ref, n2_ref, *rest):
    with_router = len(rest) == 8
    x1_ref, h2_ref = rest[3:5] if with_router else rest
    ya = _dot(a_ref[...].astype(BF16), wpa_ref[...])
    yb = _dot(g_ref[...].astype(BF16), wpb_ref[...])
    merged = sga_ref[...].astype(F32) * ya + sgb_ref[...].astype(F32) * yb
    mix = _dot(merged.astype(BF16), wo_ref[...])
    x1 = x_ref[...] + g1_ref[...] * mix
    x1_ref[...] = x1
    ms = jnp.mean(x1 * x1, axis=-1, keepdims=True)
    h = x1 * lax.rsqrt(ms + EPS) * n2_ref[...]
    h2 = h * (1.0 + sc_ref[...]) + sh_ref[...]
    h2_ref[...] = h2.astype(h2_ref.dtype)
    if with_router:
        rw_ref, rbias_ref, tri_ref = rest[0:3]
        route_ref, route_t_ref, cnt_ref = rest[5:8]
        logits = _dot(h2.astype(BF16), rw_ref[...]) + rbias_ref[...]
        packed, counts = _route_pack(logits, tri_ref[...])
        route_ref[...] = packed
        route_t_ref[...] = packed.T[0:route_t_ref.shape[0], :]
        cnt_ref[...] = jnp.broadcast_to(counts, cnt_ref.shape)


def _mod_spec(tm, n, mod):
    if mod.arr.ndim == 4:
        return pl.BlockSpec((None, None, tm, D_MODEL), lambda i, *_: (mod.layer, mod.idx, i, 0))
    tiles_per_seq = (n // mod.arr.shape[2]) // tm
    return pl.BlockSpec((None, None, None, 1, D_MODEL),
                        lambda i, *_: (mod.layer, mod.idx, i // tiles_per_seq, 0, 0))


def _merge(x, a, g, sga, sgb, wpa, wpb, wo, g1, sh2, sc2, n2, *, tm, router=None):
    n = x.shape[0]

    def row(w):
        return pl.BlockSpec((tm, w), lambda i: (i, 0))

    def const(shape):
        return pl.BlockSpec(shape, lambda i: (0,) * len(shape))

    in_specs = [row(D_MODEL), row(a.shape[1]), row(GV_W), row(D_MODEL), row(D_MODEL),
                const(wpa.shape), const(wpb.shape), const(wo.shape),
                _mod_spec(tm, n, g1), _mod_spec(tm, n, sh2), _mod_spec(tm, n, sc2), const((1, D_MODEL))]
    out_specs = [row(D_MODEL), row(D_MODEL)]
    out_shape = [jax.ShapeDtypeStruct((n, D_MODEL), F32), jax.ShapeDtypeStruct((n, D_MODEL), BF16)]
    args = [x, a, g, sga, sgb, wpa, wpb, wo, g1.arr, sh2.arr, sc2.arr, n2]
    if router is not None:
        tri = jnp.asarray(np.tril(np.ones((tm, tm), np.float32), -1), BF16)
        in_specs += [const(router[0].shape), const(router[1].shape), const((tm, tm))]
        out_specs += [row(ROUTER_PAD), pl.BlockSpec((8, tm), lambda i: (0, i)),
                      pl.BlockSpec((None, 8, ROUTER_PAD), lambda i: (i, 0, 0))]
        out_shape += [jax.ShapeDtypeStruct((n, ROUTER_PAD), F32), jax.ShapeDtypeStruct((8, n), F32),
                      jax.ShapeDtypeStruct((n // tm, 8, ROUTER_PAD), F32)]
        args += [router[0], router[1], tri]
    return pl.pallas_call(
        _merge_kernel,
        grid=(n // tm,),
        in_specs=in_specs,
        out_specs=out_specs,
        out_shape=out_shape,
        compiler_params=_params(("parallel",)),
        name="merge_out",
    )(*args)


def _swiglu(hb, wg_ref, wu_ref, wd_ref):
    act = (_silu(_dot(hb, wg_ref[...])) * _dot(hb, wu_ref[...])).astype(BF16)
    return _dot(act, wd_ref[...])


def _ffn_kernel(h_ref, x_ref, g2_ref, wg_ref, wu_ref, wd_ref, o_ref, acc_ref):
    f = pl.program_id(1)
    y = _swiglu(h_ref[...], wg_ref, wu_ref, wd_ref)

    @pl.when(f == 0)
    def _():
        acc_ref[...] = y

    @pl.when(f > 0)
    def _():
        acc_ref[...] += y

    @pl.when(f == pl.num_programs(1) - 1)
    def _():
        o_ref[...] = x_ref[...] + g2_ref[...] * acc_ref[...]


def _ffn(h2, x1, g2, wg, wu, wd, *, tm, tf):
    n = h2.shape[0]
    mod_spec = _mod_spec(tm, n, g2)
    return pl.pallas_call(
        _ffn_kernel,
        grid=(n // tm, D_FF // tf),
        in_specs=[
            pl.BlockSpec((tm, D_MODEL), lambda i, f: (i, 0)),
            pl.BlockSpec((tm, D_MODEL), lambda i, f: (i, 0)),
            mod_spec,
            pl.BlockSpec((D_MODEL, tf), lambda i, f: (0, f)),
            pl.BlockSpec((D_MODEL, tf), lambda i, f: (0, f)),
            pl.BlockSpec((tf, D_MODEL), lambda i, f: (f, 0)),
        ],
        out_specs=pl.BlockSpec((tm, D_MODEL), lambda i, f: (i, 0)),
        out_shape=jax.ShapeDtypeStruct((n, D_MODEL), F32),
        scratch_shapes=[pltpu.VMEM((tm, D_MODEL), F32)],
        compiler_params=_params(("parallel", "arbitrary")),
        name="ffn_dense",
    )(h2, x1, g2.arr, wg, wu, wd)


def _moe_kernel(h_ref, x_ref, g2_ref, rw_ref, rbias_ref, wg_ref, wu_ref, wd_ref,
                o_ref, wg16_ref, wu16_ref, wd16_ref, acc_ref, gate_ref):
    e = pl.program_id(0)
    f = pl.program_id(1)
    hb = h_ref[...]
    tm = hb.shape[0]
    lane = lax.broadcasted_iota(jnp.int32, (tm, ROUTER_PAD), 1).astype(F32)

    @pl.when((e == 0) & (f == 0))
    def _():
        logits = _dot(hb, rw_ref[...]) + rbias_ref[...]
        i1, i2, p1, p2 = _top2(logits, lane)
        gate_ref[...] = jnp.where(lane == i1, p1, 0.0) + jnp.where(lane == i2, p2, 0.0)
        acc_ref[...] = jnp.zeros_like(acc_ref)

    wg = wg_ref[...].astype(BF16)
    wu = wu_ref[...].astype(BF16)
    wd = wd_ref[...].astype(BF16)
    wg16_ref[...] = wg
    wu16_ref[...] = wu
    wd16_ref[...] = wd
    ge = jnp.sum(jnp.where(lane == e.astype(F32), gate_ref[...], 0.0), axis=-1, keepdims=True)
    act = (_silu(_dot(hb, wg)) * _dot(hb, wu)).astype(BF16)
    acc_ref[...] += ge * _dot(act, wd)

    @pl.when((e == pl.num_programs(0) - 1) & (f == pl.num_programs(1) - 1))
    def _():
        o_ref[...] = x_ref[...] + g2_ref[...] * acc_ref[...]


def _moe(h2, x1, g2, rw, rbias, wg, wu, wd):
    n = h2.shape[0]
    fe = D_FF_EXPERT
    tf = fe

    def const(shape):
        return pl.BlockSpec(shape, lambda e, f: (0,) * len(shape))

    up_spec = pl.BlockSpec((None, D_MODEL, tf), lambda e, f: (e, 0, f))
    down_spec = pl.BlockSpec((None, tf, D_MODEL), lambda e, f: (e, f, 0))
    mod_spec = pl.BlockSpec((None, None, n, D_MODEL), lambda e, f: (g2.layer, g2.idx, 0, 0))
    return pl.pallas_call(
        _moe_kernel,
        grid=(N_EXPERTS, fe // tf),
        in_specs=[const((n, D_MODEL)), const((n, D_MODEL)), mod_spec,
                  const((D_MODEL, ROUTER_PAD)), const((1, ROUTER_PAD)),
                  up_spec, up_spec, down_spec],
        out_specs=[const((n, D_MODEL)), up_spec, up_spec, down_spec],
        out_shape=[
            jax.ShapeDtypeStruct((n, D_MODEL), F32),
            jax.ShapeDtypeStruct(wg.shape, BF16),
            jax.ShapeDtypeStruct(wu.shape, BF16),
            jax.ShapeDtypeStruct(wd.shape, BF16),
        ],
        scratch_shapes=[pltpu.VMEM((n, D_MODEL), F32), pltpu.VMEM((n, ROUTER_PAD), F32)],
        compiler_params=_params(("arbitrary", "arbitrary")),
        name="moe",
    )(h2, x1, g2.arr, rw, rbias, wg, wu, wd)


def _top2(logits, lane):
    lg = jnp.where(lane < N_EXPERTS, logits, -jnp.inf)
    m1 = jnp.max(lg, axis=-1, keepdims=True)
    i1 = jnp.min(jnp.where(lg == m1, lane, float(ROUTER_PAD)), axis=-1, keepdims=True)
    lg2 = jnp.where(lane == i1, -jnp.inf, lg)
    m2 = jnp.max(lg2, axis=-1, keepdims=True)
    i2 = jnp.min(jnp.where(lg2 == m2, lane, float(ROUTER_PAD)), axis=-1, keepdims=True)
    e2 = jnp.exp(m2 - m1)
    p1 = 1.0 / (1.0 + e2)
    return i1, i2, p1, e2 * p1


def _route_pack(logits, tri):
    lane = lax.broadcasted_iota(jnp.int32, logits.shape, 1).astype(F32)
    i1, i2, p1, p2 = _top2(logits, lane)
    oh1 = jnp.where(lane == i1, 1.0, 0.0)
    oh2 = jnp.where(lane == i2, 1.0, 0.0)
    cnt1 = jnp.sum(oh1, axis=0, keepdims=True)
    cnt2 = jnp.sum(oh2, axis=0, keepdims=True)
    rank1 = jnp.sum(_dot(tri, oh1.astype(BF16)) * oh1, axis=-1, keepdims=True)
    rank2 = jnp.sum((_dot(tri, oh2.astype(BF16)) + cnt1) * oh2, axis=-1, keepdims=True)
    packed = jnp.zeros_like(logits)
    for k, val in enumerate((p1, p2, i1, i2, rank1, rank2)):
        packed = jnp.where(lane == float(k), val, packed)
    return packed, cnt1 + cnt2


def _route_tables(route_t, tile_counts, tm, n_tiles):
    n = route_t.shape[1]
    tile_cnt = tile_counts[:, 0, :N_EXPERTS].astype(jnp.int32)
    cnt = jnp.sum(tile_cnt, axis=0)
    gsz = ((cnt + tm - 1) // tm) * tm
    gend = jnp.cumsum(gsz)
    seg_start = (gend - gsz)[None, :] + jnp.cumsum(tile_cnt, axis=0) - tile_cnt
    seg_of_token = jnp.repeat(seg_start.T, n // tile_cnt.shape[0], axis=1)
    experts = jnp.arange(N_EXPERTS, dtype=jnp.int32)[:, None]
    pos = []
    for k in range(2):
        e_k = route_t[2 + k].astype(jnp.int32)
        rank_k = route_t[4 + k].astype(jnp.int32)
        pos.append(jnp.sum(jnp.where(e_k[None, :] == experts, seg_of_token, 0), axis=0) + rank_k)
    pos = jnp.concatenate(pos)
    tile_start = jnp.arange(n_tiles, dtype=jnp.int32) * tm
    tile_expert = jnp.sum((tile_start[:, None] >= gend[None, :]).astype(jnp.int32), axis=1)
    tile_expert = jnp.minimum(tile_expert, N_EXPERTS - 1)
    live = (gend[-1] // tm).reshape(1)
    return pos, tile_expert, live, gend - gsz + cnt, gend


def _row_copy_wait(src_hbm, dst, sem, rows):
    pltpu.make_async_copy(src_hbm.at[pl.ds(0, rows)], dst, sem).wait()


def _moe_dispatch_kernel(pos_ref, pad_lo_ref, pad_hi_ref, live_ref, h_ref, xs_hbm, rows, zbuf, rsem, zsem):
    i = pl.program_id(0)
    steps = pl.num_programs(0)
    tt = h_ref.shape[0]
    n = tt * steps
    tm = zbuf.shape[0]
    slot = i % 2

    def wait_rows(sl):
        for _ in range(2):
            pltpu.make_async_copy(rows.at[sl], xs_hbm.at[pl.ds(0, tt)], rsem.at[sl]).wait()

    @pl.when(i >= 2)
    def _():
        wait_rows(slot)

    rows[slot] = h_ref[...].astype(F32)
    base = i * tt
    for r in range(tt):
        row = rows.at[slot, pl.ds(r, 1)]
        pltpu.make_async_copy(row, xs_hbm.at[pl.ds(pos_ref[base + r], 1)], rsem.at[slot]).start(priority=0)
        pltpu.make_async_copy(row, xs_hbm.at[pl.ds(pos_ref[n + base + r], 1)], rsem.at[slot]).start(priority=1)

    @pl.when(i == steps - 1)
    def _():
        if steps > 1:
            wait_rows(1 - slot)
        wait_rows(slot)
        zbuf[...] = jnp.zeros(zbuf.shape, F32)

        def fill_row(p):
            return pltpu.make_async_copy(zbuf.at[pl.ds(0, 1)], xs_hbm.at[pl.ds(p, 1)], zsem)

        def fill_tile(t):
            return pltpu.make_async_copy(zbuf, xs_hbm.at[pl.ds(pl.multiple_of(t * tm, tm), tm)], zsem)

        for e in range(N_EXPERTS):
            lo, hi = pad_lo_ref[e], pad_hi_ref[e]
            lax.fori_loop(lo, hi, lambda p, c: (fill_row(p).start(), c)[1], 0)
            lax.fori_loop(lo, hi, lambda p, c: (fill_row(p).wait(), c)[1], 0)
        lo, hi = live_ref[0], xs_hbm.shape[0] // tm
        lax.fori_loop(lo, hi, lambda t, c: (fill_tile(t).start(), c)[1], 0)
        lax.fori_loop(lo, hi, lambda t, c: (fill_tile(t).wait(), c)[1], 0)


def _moe_dispatch(h2, pos, pad_lo, pad_hi, live, *, tt, tm, n_tiles):
    n = h2.shape[0]
    grid_spec = pltpu.PrefetchScalarGridSpec(
        num_scalar_prefetch=4,
        grid=(n // tt,),
        in_specs=[pl.BlockSpec((tt, D_MODEL), lambda i, *_: (i, 0))],
        out_specs=pl.BlockSpec(memory_space=pl.ANY),
        scratch_shapes=[pltpu.VMEM((2, tt, D_MODEL), F32), pltpu.VMEM((tm, D_MODEL), F32),
                        pltpu.SemaphoreType.DMA((2,)), pltpu.SemaphoreType.DMA(())],
    )
    return pl.pallas_call(
        _moe_dispatch_kernel,
        grid_spec=grid_spec,
        out_shape=jax.ShapeDtypeStruct((n_tiles * tm, D_MODEL), F32),
        compiler_params=_params(("arbitrary",)),
        name="moe_dispatch",
    )(pos, pad_lo, pad_hi, live, h2)


def _moe_expert_kernel(te_ref, live_ref, x_ref, wg_ref, wu_ref, wd_ref, y_ref):
    del te_ref
    is_live = pl.program_id(0) < live_ref[0]

    @pl.when(is_live)
    def _():
        y_ref[...] = _swiglu(x_ref[...].astype(BF16), wg_ref, wu_ref, wd_ref)

    @pl.when(jnp.logical_not(is_live))
    def _():
        y_ref[...] = jnp.zeros(y_ref.shape, F32)


def _moe_experts(xs, tile_expert, live, wg, wu, wd, *, tm, n_tiles):
    fe = D_FF_EXPERT

    def in_tile(t, te, live):
        return (jnp.minimum(t, live[0] - 1), 0)

    def out_tile(t, te, live):
        return (t, 0)

    def expert(t, te, live):
        return (te[t], 0, 0)

    grid_spec = pltpu.PrefetchScalarGridSpec(
        num_scalar_prefetch=2,
        grid=(n_tiles,),
        in_specs=[
            pl.BlockSpec((tm, D_MODEL), in_tile),
            pl.BlockSpec((None, D_MODEL, fe), expert),
            pl.BlockSpec((None, D_MODEL, fe), expert),
            pl.BlockSpec((None, fe, D_MODEL), expert),
        ],
        out_specs=pl.BlockSpec((tm, D_MODEL), out_tile),
    )
    return pl.pallas_call(
        _moe_expert_kernel,
        grid_spec=grid_spec,
        out_shape=jax.ShapeDtypeStruct(xs.shape, F32),
        compiler_params=_params(("arbitrary",)),
        name="moe_experts",
    )(tile_expert, live, xs, wg, wu, wd)


def _moe_combine_kernel(pos_ref, x_ref, g2_ref, r_ref, ys_hbm, o_ref, buf, sem):
    i = pl.program_id(0)
    nt = pl.num_programs(0)
    tt = x_ref.shape[0]
    n = nt * tt

    def fetch(tile, sl):
        base = tile * tt
        for r in range(tt):
            for s in range(2):
                row = pos_ref[s * n + base + r]
                pltpu.make_async_copy(ys_hbm.at[pl.ds(row, 1)], buf.at[sl, pl.ds(s * tt + r, 1)],
                                      sem.at[sl]).start(priority=s)

    @pl.when(i == 0)
    def _():
        fetch(0, 0)

    @pl.when(i + 1 < nt)
    def _():
        fetch(i + 1, (i + 1) % 2)

    sl = i % 2
    _row_copy_wait(ys_hbm, buf.at[sl], sem.at[sl], 2 * tt)
    r = r_ref[...]
    f = r[:, 0:1] * buf[sl, pl.ds(0, tt), :] + r[:, 1:2] * buf[sl, pl.ds(tt, tt), :]
    o_ref[...] = x_ref[...] + g2_ref[...] * f


def _moe_combine(x1, g2, route, pos, ys, *, tt):
    n = x1.shape[0]
    grid_spec = pltpu.PrefetchScalarGridSpec(
        num_scalar_prefetch=1,
        grid=(n // tt,),
        in_specs=[
            pl.BlockSpec((tt, D_MODEL), lambda i, pos: (i, 0)),
            _mod_spec(tt, n, g2),
            pl.BlockSpec((tt, ROUTER_PAD), lambda i, pos: (i, 0)),
            pl.BlockSpec(memory_space=pl.ANY),
        ],
        out_specs=pl.BlockSpec((tt, D_MODEL), lambda i, pos: (i, 0)),
        scratch_shapes=[pltpu.VMEM((2, 2 * tt, D_MODEL), F32), pltpu.SemaphoreType.DMA((2,))],
    )
    return pl.pallas_call(
        _moe_combine_kernel,
        grid_spec=grid_spec,
        out_shape=jax.ShapeDtypeStruct((n, D_MODEL), F32),
        compiler_params=_params(("arbitrary",)),
        name="moe_combine",
    )(pos, x1, g2.arr, route, ys)


def _moe_routed(h2, x1, g2, route, route_t, tile_counts, wg, wu, wd, *, tm, tt):
    n = h2.shape[0]
    n_tiles = (2 * n) // tm + N_EXPERTS
    pos, tile_expert, live, pad_lo, pad_hi = _route_tables(route_t, tile_counts, tm, n_tiles)
    xs = _moe_dispatch(h2, pos, pad_lo, pad_hi, live, tt=tt, tm=tm, n_tiles=n_tiles)
    ys = _moe_experts(xs, tile_expert, live, wg, wu, wd, tm=tm, n_tiles=n_tiles)
    return _moe_combine(x1, g2, route, pos, ys, tt=tt)


def _head_perm():
    idx = []
    for g in range(GROUP):
        for kv in range(N_KV):
            h = kv * GROUP + g
            idx.extend(range(h * HEAD_DIM, (h + 1) * HEAD_DIM))
    return np.asarray(idx, np.int32)


def _relayout_w_in(w):
    pts = np.cumsum([0, Q_W, KV_W, KV_W, GK_W, GK_W, GV_W, GATE_RANK, GV_W, D_MODEL, D_MODEL])
    qa, ka, va, qb, kb, vb, ga, rb, gta, gtb = [w[:, pts[i]:pts[i + 1]] for i in range(10)]
    qa = qa[:, _head_perm()]
    ga = jnp.pad(ga, ((0, 0), (0, GATE_PAD - GATE_RANK)))
    return jnp.concatenate([qa, ka, va, qb, kb, vb, rb, gta, gtb, ga], axis=1).astype(BF16)


def kernel(x_prompt, x_sample, cache_k, cache_v, state_gla, c_prompt, c_sample, ada_w, ada_b, norm1_g, norm2_g, w_in, q_norm_g, k_norm_g, attn_sinks, gla_wa2, gla_ba, gla_norm_g, w_branch_a, w_branch_b, w_out, ffn_w_gate, ffn_w_up, ffn_w_down, router_w, router_b, moe_w_gate, moe_w_up, moe_w_down):
    n_p = BATCH * SEQ
    xp = x_prompt.reshape(n_p, D_MODEL)
    xs = x_sample.reshape(DEC_BATCH, D_MODEL)

    c_pad = -(BATCH + DEC_BATCH) % 8
    c_all = jnp.pad(jnp.concatenate([c_sample, c_prompt], axis=0), ((0, c_pad), (0, 0)))
    mod_sample, mod_prompt = _ada(c_all, ada_w, ada_b, DEC_BATCH, BATCH)
    state_rows = state_gla.reshape(DEPTH, DEC_BATCH, GK_W, GLA_DV)

    bd = jnp.asarray(np.kron(np.eye(N_HEADS), np.ones((HEAD_DIM, HEAD_DIM))), BF16)
    ut = jnp.asarray(_gla_cum_matrix(), BF16)
    gla_upper, gla_pairs = (jnp.asarray(m) for m in _gla_level_masks())
    perm = _head_perm()
    swa_bias = jnp.asarray(_swa_sample_bias())
    swa_prompt_bias = jnp.asarray(_swa_prompt_bias())

    kp_l, vp_l, sp_l, ks_l, vs_l, ss_l = [], [], [], [], [], []
    for l in range(DEPTH):
        mod_p = [_Mod(mod_prompt, l, i) for i in range(N_MOD)]
        mod_s = [_Mod(mod_sample, l, i) for i in range(N_MOD)]

        w = _relayout_w_in(w_in[l])
        qg = (jnp.tile(q_norm_g[l], N_HEADS) * (HEAD_DIM ** -0.5)).reshape(1, Q_W)
        kg = jnp.tile(k_norm_g[l], N_KV).reshape(1, KV_W)
        wa2 = jnp.pad(gla_wa2[l], ((0, GATE_PAD - GATE_RANK), (0, 0))).astype(BF16)
        ba = gla_ba[l].reshape(1, GK_W)
        n1 = norm1_g[l].reshape(1, D_MODEL)
        n2 = norm2_g[l].reshape(1, D_MODEL)
        gn = gla_norm_g[l].reshape(1, GLA_DV)
        wpa = w_branch_a[l][perm].astype(BF16)
        wpb = w_branch_b[l].astype(BF16)
        wo = w_out[l].astype(BF16)
        sink_rows = jnp.broadcast_to(attn_sinks[l][_sample_head_of_row()][:, None], (N_HEADS, LANES))

        q, k, v, gq, gk, gv, la, rbs, sga, sgb = _mixin(
            xp, mod_p[0], mod_p[1], n1, w, bd, qg, kg, wa2, ba, tm=ROW_TILE_RESIDENT)
        a_out = _swa_prompt(attn_sinks[l], q, k, v, swa_prompt_bias)
        g_out, s_fin = _gla_prompt(gq, gk, la, gv, rbs, gn, ut, gla_upper, gla_pairs)
        is_moe = l % 2 == 1
        if is_moe:
            rw = jnp.pad(router_w[l // 2], ((0, 0), (0, ROUTER_PAD - N_EXPERTS))).astype(BF16)
            rbias = jnp.pad(router_b[l // 2], (0, ROUTER_PAD - N_EXPERTS)).reshape(1, ROUTER_PAD)
        merged = _merge(xp, a_out, g_out, sga, sgb, wpa, wpb, wo, mod_p[2], mod_p[3], mod_p[4], n2,
                        tm=ROW_TILE_RESIDENT, router=(rw, rbias) if is_moe else None)
        x1, h2 = merged[0], merged[1]
        for cache, full in ((kp_l, k), (vp_l, v)):
            tail = full.reshape(BATCH, SEQ, KV_W)[:, SEQ - WINDOW:]
            cache.append(tail.reshape(BATCH, WINDOW, N_KV, HEAD_DIM))
        sp_l.append(s_fin.reshape(BATCH, GLA_HEADS, GLA_DK, GLA_DV))

        qs, ksn, vsn, gqs, gks, gvs, las, rbss, sgas, sgbs = _mixin(
            xs, mod_s[0], mod_s[1], n1, w, bd, qg, kg, wa2, ba, tm=DEC_BATCH)
        a_s, nk, nv = _swa_sample(
            qs.astype(F32).reshape(DEC_BATCH, N_HEADS, HEAD_DIM),
            ksn.reshape(DEC_BATCH, N_KV, HEAD_DIM), vsn.reshape(DEC_BATCH, N_KV, HEAD_DIM),
            cache_k[l].reshape(1, DEC_BATCH, CACHE_ROWS, HEAD_DIM),
            cache_v[l].reshape(1, DEC_BATCH, CACHE_ROWS, HEAD_DIM), swa_bias, sink_rows, layer=0)
        g_s, s_new = _gla_sample(gqs, gks, las, gvs, rbss, gn, state_rows, layer=l)
        x1s, h2s = _merge(xs, a_s.reshape(DEC_BATCH, Q_W), g_s, sgas, sgbs, wpa, wpb, wo,
                          mod_s[2], mod_s[3], mod_s[4], n2, tm=DEC_BATCH)
        ks_l.append(nk.reshape(DEC_BATCH, WINDOW, N_KV, HEAD_DIM))
        vs_l.append(nv.reshape(DEC_BATCH, WINDOW, N_KV, HEAD_DIM))
        ss_l.append(s_new.reshape(DEC_BATCH, GLA_HEADS, GLA_DK, GLA_DV))

        i = l // 2
        if not is_moe:
            wg, wu, wd = ffn_w_gate[i].astype(BF16), ffn_w_up[i].astype(BF16), ffn_w_down[i].astype(BF16)
            xp = _ffn(h2, x1, mod_p[5], wg, wu, wd, tm=ROW_TILE_SWIGLU, tf=D_FF // 2)
            xs = _ffn(h2s, x1s, mod_s[5], wg, wu, wd, tm=DEC_BATCH, tf=D_FF // 2)
        else:
            xs, wg, wu, wd = _moe(h2s, x1s, mod_s[5], rw, rbias, moe_w_gate[i], moe_w_up[i], moe_w_down[i])
            xp = _moe_routed(h2, x1, mod_p[5], *merged[2:5], wg, wu, wd,
                             tm=ROW_TILE_SWIGLU, tt=MOE_TOKEN_TILE)

    return (xp.reshape(BATCH, SEQ, D_MODEL), xs.reshape(DEC_BATCH, 1, D_MODEL),
            jnp.stack(kp_l), jnp.stack(vp_l), jnp.stack(sp_l),
            jnp.stack(ks_l), jnp.stack(vs_l), jnp.stack(ss_l))
```

```python
from typing import NamedTuple

import jax
import jax.numpy as jnp
import numpy as np
from jax import lax
from jax.experimental import pallas as pl
from jax.experimental.pallas import tpu as pltpu

D_MODEL = 1024
BATCH = 4
SEQ = 4096
DEPTH = 2
DEC_BATCH = 128
N_HEADS = 8
N_KV = 2
HEAD_DIM = 64
GROUP = N_HEADS // N_KV
WINDOW = 128
GLA_HEADS = 4
GLA_DK = 64
GLA_DV = 128
GATE_RANK = 16
GATE_TAU = 16.0
D_FF = 2816
N_EXPERTS = 8
D_FF_EXPERT = 1408
EPS = 1e-6

Q_W = N_HEADS * HEAD_DIM
KV_W = N_KV * HEAD_DIM
GK_W = GLA_HEADS * GLA_DK
GV_W = GLA_HEADS * GLA_DV

LANES = 128
GATE_PAD = LANES
ROUTER_PAD = LANES
VMEM_LIMIT = 56 * 1024 * 1024

ROW_TILE_RESIDENT = 1024
ROW_TILE_SWIGLU = 512
MOE_TOKEN_TILE = 512

F32 = jnp.float32
BF16 = jnp.bfloat16

_C_Q = 0
_C_K = _C_Q + Q_W
_C_V = _C_K + KV_W
_C_GQ = _C_V + KV_W
_C_GK = _C_GQ + GK_W
_C_GV = _C_GK + GK_W
_C_RB = _C_GV + GV_W
_C_GA = _C_RB + GV_W
_C_GB = _C_GA + D_MODEL
_C_LR = _C_GB + D_MODEL
PROJ_PAD = _C_LR + GATE_PAD

GLA_CHUNK = 128
GLA_LEVELS = 7
GLA_MXU_LEVELS = 3
GLA_CHUNKS_PER_STEP = 4


def _params(sem, vmem=VMEM_LIMIT):
    return pltpu.CompilerParams(dimension_semantics=sem, vmem_limit_bytes=vmem)


def _dot(a, b):
    return jnp.dot(a, b, preferred_element_type=F32)


def _dot_nt(a, b):
    return lax.dot_general(a, b, (((1,), (1,)), ((), ())), preferred_element_type=F32)


def _dot_tn(a, b):
    return lax.dot_general(a, b, (((0,), (0,)), ((), ())), preferred_element_type=F32)


def _sigmoid(x):
    return 0.5 * jnp.tanh(0.5 * x) + 0.5


def _silu(x):
    return x * _sigmoid(x)


N_MOD = 6


class _Mod(NamedTuple):
    arr: jax.Array
    layer: int
    idx: int


def _ada_kernel(c_ref, w_ref, b_ref, os_ref, op_ref):
    c = c_ref[...]
    mod = _dot(_silu(c).astype(BF16), w_ref[...].astype(BF16)) + b_ref[...]
    n_s = os_ref.shape[0]
    os_ref[...] = mod[0:n_s]
    for b in range(op_ref.shape[0]):
        op_ref[b] = mod[n_s + b:n_s + b + 1]


def _ada(c_all, ada_w, ada_b, n_sample, n_prompt):
    rows = c_all.shape[0]
    return pl.pallas_call(
        _ada_kernel,
        grid=(DEPTH, N_MOD),
        in_specs=[
            pl.BlockSpec((rows, D_MODEL), lambda l, j: (0, 0)),
            pl.BlockSpec((None, D_MODEL, D_MODEL), lambda l, j: (l, 0, j)),
            pl.BlockSpec((None, 1, D_MODEL), lambda l, j: (l, 0, j)),
        ],
        out_specs=[
            pl.BlockSpec((None, None, n_sample, D_MODEL), lambda l, j: (l, j, 0, 0)),
            pl.BlockSpec((None, None, n_prompt, 1, D_MODEL), lambda l, j: (l, j, 0, 0, 0)),
        ],
        out_shape=[
            jax.ShapeDtypeStruct((DEPTH, N_MOD, n_sample, D_MODEL), F32),
            jax.ShapeDtypeStruct((DEPTH, N_MOD, n_prompt, 1, D_MODEL), F32),
        ],
        compiler_params=_params(("parallel", "parallel")),
        name="ada_mod",
    )(c_all, ada_w, ada_b.reshape(DEPTH, 1, N_MOD * D_MODEL))


def _mixin_kernel(x_ref, sh_ref, sc_ref, n1_ref, w_ref, bd_ref, qg_ref, kg_ref, wa2_ref, ba_ref,
                  q_ref, k_ref, v_ref, gq_ref, gk_ref, gv_ref, la_ref, rb_ref, sga_ref, sgb_ref):
    x = x_ref[...]
    ms = jnp.mean(x * x, axis=-1, keepdims=True)
    h = x * lax.rsqrt(ms + EPS) * n1_ref[...]
    h = h * (1.0 + sc_ref[...]) + sh_ref[...]
    hb = h.astype(BF16)

    def proj(a, b):
        return _dot(hb, w_ref[:, a:b])

    q = proj(_C_Q, _C_K)
    ssq = _dot((q * q).astype(BF16), bd_ref[...])
    q_ref[...] = (q * lax.rsqrt(ssq * (1.0 / HEAD_DIM) + EPS) * qg_ref[...]).astype(BF16)
    k = proj(_C_K, _C_V)
    ssk = _dot((k * k).astype(BF16), bd_ref[0:KV_W, 0:KV_W])
    k_ref[...] = k * lax.rsqrt(ssk * (1.0 / HEAD_DIM) + EPS) * kg_ref[...]
    v_ref[...] = proj(_C_V, _C_GQ)
    gq_ref[...] = proj(_C_GQ, _C_GK) * (GLA_DK ** -0.5)
    gk_ref[...] = proj(_C_GK, _C_GV)
    gv_ref[...] = proj(_C_GV, _C_RB)
    rb_ref[...] = _silu(proj(_C_RB, _C_GA)).astype(BF16)
    sga_ref[...] = _sigmoid(proj(_C_GA, _C_GB)).astype(BF16)
    sgb_ref[...] = _sigmoid(proj(_C_GB, _C_LR)).astype(BF16)
    ga = proj(_C_LR, PROJ_PAD)
    xg = _dot(ga.astype(BF16), wa2_ref[...]) + ba_ref[...]
    la_ref[...] = (jnp.minimum(xg, 0.0) - jnp.log1p(jnp.exp(-jnp.abs(xg)))) * (1.0 / GATE_TAU)


def _mixin(x, sh, sc, n1, w, bd, qg, kg, wa2, ba, *, tm):
    n = x.shape[0]
    nt = n // tm

    def row(i):
        return (i, 0)

    def const(shape):
        return pl.BlockSpec(shape, lambda i: (0,) * len(shape), pipeline_mode=pl.Buffered(1))

    def out(width, dtype):
        return pl.BlockSpec((tm, width), row), jax.ShapeDtypeStruct((n, width), dtype)

    outs = [out(Q_W, BF16), out(KV_W, F32), out(KV_W, F32), out(GK_W, F32), out(GK_W, F32),
            out(GV_W, F32), out(GK_W, F32), out(GV_W, BF16), out(D_MODEL, BF16), out(D_MODEL, BF16)]
    return pl.pallas_call(
        _mixin_kernel,
        grid=(nt,),
        in_specs=[
            pl.BlockSpec((tm, D_MODEL), row), _mod_spec(tm, n, sh), _mod_spec(tm, n, sc), const((1, D_MODEL)),
            const((D_MODEL, PROJ_PAD)), const((Q_W, Q_W)), const((1, Q_W)), const((1, KV_W)),
            const((GATE_PAD, GK_W)), const((1, GK_W)),
        ],
        out_specs=[o[0] for o in outs],
        out_shape=[o[1] for o in outs],
        compiler_params=_params(("parallel",)),
        name="mixer_in",
    )(x, sh.arr, sc.arr, n1, w, bd, qg, kg, wa2, ba)


SWA_BLOCKS = 8


def _head_slope(h):
    return float(2.0 ** (-8.0 * (h + 1) / N_HEADS))


def _swa_prompt_bias():
    blk = WINDOW
    dist = np.arange(blk)[:, None] + blk - np.arange(2 * blk)[None, :]
    slopes = np.asarray([_head_slope(h) for h in range(N_HEADS)])[:, None, None]
    return np.where((dist >= 0) & (dist <= WINDOW), -slopes * dist, -np.inf).astype(np.float32)


def _swa_prompt_kernel(sink_ref, q_ref, kp_ref, kc_ref, vp_ref, vc_ref, bias_ref, o_ref):
    n = pl.program_id(1)
    blk = WINDOW
    kall = jnp.concatenate([kp_ref[...], kc_ref[...]], axis=0).astype(BF16)
    vall = jnp.concatenate([vp_ref[...], vc_ref[...]], axis=0).astype(BF16)
    col = lax.broadcasted_iota(jnp.int32, (blk, 2 * blk), 1)
    first_key = jnp.where(n > 0, 0, blk)
    klane = lax.broadcasted_iota(jnp.int32, kall.shape, 1)
    kall_kv = [jnp.where((klane // HEAD_DIM) == kv, kall, jnp.zeros_like(kall)) for kv in range(N_KV)]
    olane = lax.broadcasted_iota(jnp.int32, (blk, KV_W), 1)
    q = q_ref[...]
    for j in range(SWA_BLOCKS):
        vv = vall[j * blk:(j + 2) * blk]
        outs = []
        for g in range(GROUP):
            qp = q[j * blk:(j + 1) * blk, g * KV_W:(g + 1) * KV_W]
            pair = []
            for kv in range(N_KV):
                h = kv * GROUP + g
                s = _dot_nt(qp, kall_kv[kv][j * blk:(j + 2) * blk]) + bias_ref[h]
                if j == 0:
                    s = jnp.where(col >= first_key, s, -jnp.inf)
                sink = sink_ref[h]
                m = jnp.maximum(jnp.max(s, axis=-1, keepdims=True), sink)
                p = jnp.exp(s - m)
                den = jnp.sum(p, axis=-1, keepdims=True) + jnp.exp(sink - m)
                pair.append(_dot(p.astype(BF16), vv) * (1.0 / den))
            outs.append(jnp.where(olane < HEAD_DIM, pair[0], pair[1]))
        o_ref[j * blk:(j + 1) * blk, :] = jnp.concatenate(outs, axis=-1).astype(BF16)


def _swa_prompt(sinks, q, k, v, bias):
    nb = SEQ // WINDOW
    steps = nb // SWA_BLOCKS
    tq = SWA_BLOCKS * WINDOW

    def cur(b, n):
        return (b * steps + n, 0)

    def prev(b, n):
        return (b * nb + jnp.maximum(n * SWA_BLOCKS - 1, 0), 0)

    return pl.pallas_call(
        _swa_prompt_kernel,
        grid=(BATCH, steps),
        in_specs=[
            pl.BlockSpec(memory_space=pltpu.SMEM),
            pl.BlockSpec((tq, Q_W), cur),
            pl.BlockSpec((WINDOW, KV_W), prev), pl.BlockSpec((tq, KV_W), cur),
            pl.BlockSpec((WINDOW, KV_W), prev), pl.BlockSpec((tq, KV_W), cur),
            pl.BlockSpec(bias.shape, lambda b, n: (0, 0, 0), pipeline_mode=pl.Buffered(1)),
        ],
        out_specs=pl.BlockSpec((tq, Q_W), cur),
        out_shape=jax.ShapeDtypeStruct((BATCH * SEQ, Q_W), BF16),
        compiler_params=_params(("parallel", "parallel")),
        name="swa_prompt",
    )(sinks, q, k, k, v, v, bias)


SAMPLE_TB = 8
CACHE_ROWS = WINDOW * N_KV


def _sample_head_of_row():
    j = np.arange(N_HEADS)
    return (j % N_KV) * GROUP + j // N_KV


def _swa_sample_bias():
    j = np.arange(N_HEADS)[:, None]
    c = np.arange(CACHE_ROWS)[None, :]
    slope = 2.0 ** (-8.0 * (_sample_head_of_row()[:, None] + 1) / N_HEADS)
    bias = -slope * (WINDOW - c // N_KV)
    return np.where(c % N_KV == j % N_KV, bias, -np.inf).astype(np.float32)


def _swa_sample_kernel(q_ref, kn_ref, vn_ref, ck_ref, cv_ref, bias_ref, sk_ref, o_ref, ok_ref, ov_ref):
    rows = CACHE_ROWS
    sink = sk_ref[...][:, 0:1]
    q = q_ref[...]
    kn = kn_ref[...]
    vn = vn_ref[...]
    kc = ck_ref[...]
    vc = cv_ref[...]
    kn8 = jnp.concatenate([kn] * GROUP, axis=1)
    vn8 = jnp.concatenate([vn] * GROUP, axis=1)
    s = lax.dot_general(q.astype(BF16), kc.astype(BF16), (((2,), (2,)), ((0,), (0,))),
                        preferred_element_type=F32) + bias_ref[...]
    s_new = jnp.sum(q * kn8, axis=-1, keepdims=True)
    m = jnp.maximum(jnp.maximum(jnp.max(s, axis=-1, keepdims=True), s_new), sink)
    p = jnp.exp(s - m)
    p_new = jnp.exp(s_new - m)
    den = jnp.sum(p, axis=-1, keepdims=True) + p_new + jnp.exp(sink - m)
    o = lax.dot_general(p.astype(BF16), vc.astype(BF16), (((2,), (1,)), ((0,), (0,))),
                        preferred_element_type=F32) + p_new * vn8
    o_ref[...] = o * (1.0 / den)
    ok_ref[:, pl.ds(0, rows - N_KV), :] = kc[:, N_KV:, :]
    ok_ref[:, pl.ds(rows - N_KV, N_KV), :] = kn
    ov_ref[:, pl.ds(0, rows - N_KV), :] = vc[:, N_KV:, :]
    ov_ref[:, pl.ds(rows - N_KV, N_KV), :] = vn


def _swa_sample(q, kn, vn, ck, cv, bias, sinks, *, layer):
    tb = 2 * SAMPLE_TB
    nb = DEC_BATCH // tb

    def const(shape):
        return pl.BlockSpec(shape, lambda i: (0,) * len(shape))

    def per_seq(*dims):
        return pl.BlockSpec((tb,) + dims, lambda i: (i, 0, 0))

    cache_spec = pl.BlockSpec((None, tb, CACHE_ROWS, HEAD_DIM), lambda i: (layer, i, 0, 0))
    return pl.pallas_call(
        _swa_sample_kernel,
        grid=(nb,),
        in_specs=[per_seq(N_HEADS, HEAD_DIM), per_seq(N_KV, HEAD_DIM), per_seq(N_KV, HEAD_DIM),
                  cache_spec, cache_spec,
                  const((N_HEADS, CACHE_ROWS)), const((N_HEADS, LANES))],
        out_specs=[per_seq(N_HEADS, HEAD_DIM), per_seq(CACHE_ROWS, HEAD_DIM), per_seq(CACHE_ROWS, HEAD_DIM)],
        out_shape=[
            jax.ShapeDtypeStruct((DEC_BATCH, N_HEADS, HEAD_DIM), F32),
            jax.ShapeDtypeStruct((DEC_BATCH, CACHE_ROWS, HEAD_DIM), F32),
            jax.ShapeDtypeStruct((DEC_BATCH, CACHE_ROWS, HEAD_DIM), F32),
        ],
        compiler_params=_params(("parallel",)),
        name="swa_sample",
    )(q, kn, vn, ck, cv, bias, sinks)


def _gla_cum_matrix():
    c = GLA_CHUNK
    tri = np.tril(np.ones((c, c), np.float32))
    i = np.arange(c)
    blocks = []
    for lvl in range(GLA_MXU_LEVELS):
        half = 1 << lvl
        mid = (i // (2 * half)) * (2 * half) + half - 1
        blocks.append(tri - tri[mid])
    blocks.append(tri)
    return np.concatenate(blocks, axis=0)


def _split3(x):
    hi = x.astype(BF16)
    r1 = x - hi.astype(F32)
    mid = r1.astype(BF16)
    lo = (r1 - mid.astype(F32)).astype(BF16)
    return hi, mid, lo


def _gla_level_masks():
    cl = GLA_CHUNK
    r = np.arange(cl)
    upper = np.stack([np.broadcast_to(((r >> lvl) & 1)[:, None], (cl, GK_W)) for lvl in range(GLA_LEVELS)])
    ri = np.tile(r, GLA_HEADS)[:, None]
    pairs = [(ri >> (lvl + 1)) == (r[None, :] >> (lvl + 1)) for lvl in range(GLA_LEVELS)]
    pairs.append(ri == r[None, :])
    return upper.astype(np.float32), np.stack(pairs).astype(np.float32)


def _gla_prompt_kernel(q_ref, k_ref, la_ref, v_ref, rb_ref, gn_ref, ut_ref, up_ref, pm_ref,
                       o_ref, s_ref, st_ref):
    c = pl.program_id(1)
    cl = GLA_CHUNK

    @pl.when(c == 0)
    def _():
        st_ref[0] = jnp.zeros(st_ref.shape[1:], F32)

    ut = ut_ref[...]
    lane = lax.broadcasted_iota(jnp.int32, (cl, GK_W), 1)
    head_of_lane = lane // GLA_DK
    ones = jnp.ones((cl, LANES), BF16)
    state = [st_ref[c % 2, h] for h in range(GLA_HEADS)]
    for cc in range(GLA_CHUNKS_PER_STEP):
        rows = slice(cc * cl, (cc + 1) * cl)
        hi, mid, lo = _split3(la_ref[rows, :])
        tall = _dot(ut, hi) + _dot(ut, mid) + _dot(ut, lo)
        q = q_ref[rows, :]
        k = k_ref[rows, :]
        attn_all = jnp.zeros((GLA_HEADS * cl, cl), F32)
        cum = tall[GLA_MXU_LEVELS * cl:(GLA_MXU_LEVELS + 1) * cl]
        for lvl in range(GLA_LEVELS + 1):
            if lvl < GLA_LEVELS:
                if lvl < GLA_MXU_LEVELS:
                    t_lvl = tall[lvl * cl:(lvl + 1) * cl]
                else:
                    half = 1 << lvl
                    mids = [jnp.broadcast_to(cum[b0 + half - 1:b0 + half, :], (2 * half, GK_W))
                            for b0 in range(0, cl, 2 * half)]
                    t_lvl = cum - (mids[0] if len(mids) == 1 else jnp.concatenate(mids, axis=0))
                e = jnp.exp(-jnp.abs(t_lvl))
                e_up = e * up_ref[lvl]
                qt = (q * e_up).astype(BF16)
                kt = (k * (e - e_up)).astype(BF16)
            else:
                qt = q.astype(BF16)
                kt = k.astype(BF16)
            q_heads = jnp.concatenate(
                [jnp.where(head_of_lane == h, qt, jnp.zeros_like(qt)) for h in range(GLA_HEADS)], axis=0)
            attn_all = attn_all + _dot_nt(q_heads, kt) * pm_ref[lvl]

        last = cum[cl - 1:cl, :]
        qe = (q * jnp.exp(cum)).astype(BF16)
        kd = (k * jnp.exp(last - cum)).astype(BF16)
        dec = jnp.exp(_dot_tn(hi, ones) + _dot_tn(mid, ones) + _dot_tn(lo, ones))
        v = v_ref[rows, :]
        rb = rb_ref[rows, :]
        outs = []
        for h in range(GLA_HEADS):
            vh = v[:, h * GLA_DV:(h + 1) * GLA_DV].astype(BF16)
            o = _dot(attn_all[h * cl:(h + 1) * cl].astype(BF16), vh) + _dot(qe, state[h].astype(BF16))
            kdh = jnp.where(head_of_lane == h, kd, jnp.zeros_like(kd))
            state[h] = state[h] * dec + _dot_tn(kdh, vh)
            ms = jnp.mean(o * o, axis=-1, keepdims=True)
            g = o * lax.rsqrt(ms + EPS) * gn_ref[...]
            outs.append(g * rb[:, h * GLA_DV:(h + 1) * GLA_DV].astype(F32))
        o_ref[rows, :] = jnp.concatenate(outs, axis=-1).astype(BF16)

    for h in range(GLA_HEADS):
        st_ref[(c + 1) % 2, h] = state[h]

    @pl.when(c == pl.num_programs(1) - 1)
    def _():
        fin = (SEQ // (GLA_CHUNK * GLA_CHUNKS_PER_STEP)) % 2
        s_ref[...] = st_ref[fin, 0] + st_ref[fin, 1] + st_ref[fin, 2] + st_ref[fin, 3]


def _gla_prompt(gq, gk, la, gv, rbs, gn, ut, upper, pairs):
    cl = GLA_CHUNK * GLA_CHUNKS_PER_STEP
    nc = SEQ // cl

    def row(b, c):
        return (b * nc + c, 0)

    def const(shape):
        return pl.BlockSpec(shape, lambda b, c: (0,) * len(shape), pipeline_mode=pl.Buffered(1))

    return pl.pallas_call(
        _gla_prompt_kernel,
        grid=(BATCH, nc),
        in_specs=[
            pl.BlockSpec((cl, GK_W), row), pl.BlockSpec((cl, GK_W), row), pl.BlockSpec((cl, GK_W), row),
            pl.BlockSpec((cl, GV_W), row), pl.BlockSpec((cl, GV_W), row),
            const((1, GLA_DV)), const(ut.shape), const(upper.shape), const(pairs.shape),
        ],
        out_specs=[
            pl.BlockSpec((cl, GV_W), row),
            pl.BlockSpec((None, GK_W, GLA_DV), lambda b, c: (b, 0, 0)),
        ],
        out_shape=[
            jax.ShapeDtypeStruct((BATCH * SEQ, GV_W), BF16),
            jax.ShapeDtypeStruct((BATCH, GK_W, GLA_DV), F32),
        ],
        scratch_shapes=[pltpu.VMEM((2, GLA_HEADS, GK_W, GLA_DV), F32)],
        compiler_params=_params(("parallel", "arbitrary")),
        name="gla_prompt",
    )(gq, gk, la, gv, rbs, gn, ut, upper, pairs)


def _gla_sample_kernel(q_ref, k_ref, la_ref, v_ref, rb_ref, gn_ref, s_ref, o_ref, so_ref):
    tb = SAMPLE_TB
    dec = jnp.exp(la_ref[...])
    pieces = []
    for x in (dec, k_ref[...], q_ref[...]):
        hi, mid, lo = _split3(x)
        stacked = jnp.concatenate(
            [hi.astype(F32), mid.astype(F32), lo.astype(F32), jnp.zeros_like(x)], axis=0)
        pieces.append(stacked.astype(BF16))
    prow = lax.broadcasted_iota(jnp.int32, (4 * tb, LANES), 0)
    rb = rb_ref[...].astype(F32)
    v = v_ref[...]
    for bi in range(tb):
        sel = jnp.where((prow % tb) == bi, 1.0, 0.0).astype(BF16)
        a_col, k_col, q_col = [_dot_tn(p, sel) for p in pieces]
        for h in range(GLA_HEADS):
            rs = slice(h * GLA_DK, (h + 1) * GLA_DK)
            vs = slice(h * GLA_DV, (h + 1) * GLA_DV)
            s_new = a_col[rs] * s_ref[bi, rs, :] + k_col[rs] * v[bi:bi + 1, vs]
            so_ref[bi, rs, :] = s_new
            o = jnp.sum(q_col[rs] * s_new, axis=0, keepdims=True)
            ms = jnp.mean(o * o, axis=-1, keepdims=True)
            g = o * lax.rsqrt(ms + EPS) * gn_ref[...]
            o_ref[bi:bi + 1, vs] = g * rb[bi:bi + 1, vs]


def _gla_sample(gq, gk, la, gv, rbs, gn, state, *, layer):
    tb = SAMPLE_TB
    nb = DEC_BATCH // tb

    def row(w):
        return pl.BlockSpec((tb, w), lambda i: (i, 0))

    st_spec = pl.BlockSpec((tb, GK_W, GLA_DV), lambda i: (i, 0, 0))
    st_in_spec = pl.BlockSpec((None, tb, GK_W, GLA_DV), lambda i: (layer, i, 0, 0))
    return pl.pallas_call(
        _gla_sample_kernel,
        grid=(nb,),
        in_specs=[row(GK_W), row(GK_W), row(GK_W), row(GV_W), row(GV_W),
                  pl.BlockSpec((1, GLA_DV), lambda i: (0, 0)), st_in_spec],
        out_specs=[row(GV_W), st_spec],
        out_shape=[
            jax.ShapeDtypeStruct((DEC_BATCH, GV_W), F32),
            jax.ShapeDtypeStruct((DEC_BATCH, GK_W, GLA_DV), F32),
        ],
        compiler_params=_params(("parallel",)),
        name="gla_sample",
    )(gq, gk, la, gv, rbs, gn, state)


def _merge_kernel(x_ref, a_ref, g_ref, sga_ref, sgb_ref, wpa_ref, wpb_ref, wo_ref,
                  g1_ref, sh_ref, sc_ref, n2_ref, *rest):
    with_router = len(rest) == 8
    x1_ref, h2_ref = rest[3:5] if with_router else rest
    ya = _dot(a_ref[...].astype(BF16), wpa_ref[...])
    yb = _dot(g_ref[...].astype(BF16), wpb_ref[...])
    merged = sga_ref[...].astype(F32) * ya + sgb_ref[...].astype(F32) * yb
    mix = _dot(merged.astype(BF16), wo_ref[...])
    x1 = x_ref[...] + g1_ref[...] * mix
    x1_ref[...] = x1
    ms = jnp.mean(x1 * x1, axis=-1, keepdims=True)
    h = x1 * lax.rsqrt(ms + EPS) * n2_ref[...]
    h2 = h * (1.0 + sc_ref[...]) + sh_ref[...]
    h2_ref[...] = h2.astype(h2_ref.dtype)
    if with_router:
        rw_ref, rbias_ref, tri_ref = rest[0:3]
        route_ref, route_t_ref, cnt_ref = rest[5:8]
        logits = _dot(h2.astype(BF16), rw_ref[...]) + rbias_ref[...]
        packed, counts = _route_pack(logits, tri_ref[...])
        route_ref[...] = packed
        route_t_ref[...] = packed.T[0:route_t_ref.shape[0], :]
        cnt_ref[...] = jnp.broadcast_to(counts, cnt_ref.shape)


def _mod_spec(tm, n, mod):
    if mod.arr.ndim == 4:
        return pl.BlockSpec((None, None, tm, D_MODEL), lambda i, *_: (mod.layer, mod.idx, i, 0))
    tiles_per_seq = (n // mod.arr.shape[2]) // tm
    return pl.BlockSpec((None, None, None, 1, D_MODEL),
                        lambda i, *_: (mod.layer, mod.idx, i // tiles_per_seq, 0, 0))


def _merge(x, a, g, sga, sgb, wpa, wpb, wo, g1, sh2, sc2, n2, *, tm, router=None):
    n = x.shape[0]

    def row(w):
        return pl.BlockSpec((tm, w), lambda i: (i, 0))

    def const(shape):
        return pl.BlockSpec(shape, lambda i: (0,) * len(shape))

    in_specs = [row(D_MODEL), row(a.shape[1]), row(GV_W), row(D_MODEL), row(D_MODEL),
                const(wpa.shape), const(wpb.shape), const(wo.shape),
                _mod_spec(tm, n, g1), _mod_spec(tm, n, sh2), _mod_spec(tm, n, sc2), const((1, D_MODEL))]
    out_specs = [row(D_MODEL), row(D_MODEL)]
    out_shape = [jax.ShapeDtypeStruct((n, D_MODEL), F32), jax.ShapeDtypeStruct((n, D_MODEL), BF16)]
    args = [x, a, g, sga, sgb, wpa, wpb, wo, g1.arr, sh2.arr, sc2.arr, n2]
    if router is not None:
        tri = jnp.asarray(np.tril(np.ones((tm, tm), np.float32), -1), BF16)
        in_specs += [const(router[0].shape), const(router[1].shape), const((tm, tm))]
        out_specs += [row(ROUTER_PAD), pl.BlockSpec((8, tm), lambda i: (0, i)),
                      pl.BlockSpec((None, 8, ROUTER_PAD), lambda i: (i, 0, 0))]
        out_shape += [jax.ShapeDtypeStruct((n, ROUTER_PAD), F32), jax.ShapeDtypeStruct((8, n), F32),
                      jax.ShapeDtypeStruct((n // tm, 8, ROUTER_PAD), F32)]
        args += [router[0], router[1], tri]
    return pl.pallas_call(
        _merge_kernel,
        grid=(n // tm,),
        in_specs=in_specs,
        out_specs=out_specs,
        out_shape=out_shape,
        compiler_params=_params(("parallel",)),
        name="merge_out",
    )(*args)


def _swiglu(hb, wg_ref, wu_ref, wd_ref):
    act = (_silu(_dot(hb, wg_ref[...])) * _dot(hb, wu_ref[...])).astype(BF16)
    return _dot(act, wd_ref[...])


def _ffn_kernel(h_ref, x_ref, g2_ref, wg_ref, wu_ref, wd_ref, o_ref, acc_ref):
    f = pl.program_id(1)
    y = _swiglu(h_ref[...], wg_ref, wu_ref, wd_ref)

    @pl.when(f == 0)
    def _():
        acc_ref[...] = y

    @pl.when(f > 0)
    def _():
        acc_ref[...] += y

    @pl.when(f == pl.num_programs(1) - 1)
    def _():
        o_ref[...] = x_ref[...] + g2_ref[...] * acc_ref[...]


def _ffn_cast_kernel(h_ref, x_ref, g2_ref, wg_ref, wu_ref, wd_ref, o_ref, wg16_ref, wu16_ref, wd16_ref, acc_ref):
    wg16_ref[...] = wg_ref[...].astype(BF16)
    wu16_ref[...] = wu_ref[...].astype(BF16)
    wd16_ref[...] = wd_ref[...].astype(BF16)
    _ffn_kernel(h_ref, x_ref, g2_ref, wg16_ref, wu16_ref, wd16_ref, o_ref, acc_ref)


def _ffn(h2, x1, g2, wg, wu, wd, *, tm, tf, emit_bf16=False):
    n = h2.shape[0]
    assert not emit_bf16 or n == tm
    mod_spec = _mod_spec(tm, n, g2)
    up_spec = pl.BlockSpec((D_MODEL, tf), lambda i, f: (0, f))
    down_spec = pl.BlockSpec((tf, D_MODEL), lambda i, f: (f, 0))
    out_specs = [pl.BlockSpec((tm, D_MODEL), lambda i, f: (i, 0))]
    out_shape = [jax.ShapeDtypeStruct((n, D_MODEL), F32)]
    if emit_bf16:
        out_specs += [up_spec, up_spec, down_spec]
        out_shape += [jax.ShapeDtypeStruct(w.shape, BF16) for w in (wg, wu, wd)]
    out = pl.pallas_call(
        _ffn_cast_kernel if emit_bf16 else _ffn_kernel,
        grid=(n // tm, D_FF // tf),
        in_specs=[
            pl.BlockSpec((tm, D_MODEL), lambda i, f: (i, 0)),
            pl.BlockSpec((tm, D_MODEL), lambda i, f: (i, 0)),
            mod_spec, up_spec, up_spec, down_spec,
        ],
        out_specs=out_specs,
        out_shape=out_shape,
        scratch_shapes=[pltpu.VMEM((tm, D_MODEL), F32)],
        compiler_params=_params(("parallel", "arbitrary")),
        name="ffn_dense",
    )(h2, x1, g2.arr, wg, wu, wd)
    return out if emit_bf16 else out[0]


def _moe_kernel(h_ref, x_ref, g2_ref, rw_ref, rbias_ref, wg_ref, wu_ref, wd_ref,
                o_ref, wg16_ref, wu16_ref, wd16_ref, acc_ref, gate_ref):
    e = pl.program_id(0)
    f = pl.program_id(1)
    hb = h_ref[...]
    tm = hb.shape[0]
    lane = lax.broadcasted_iota(jnp.int32, (tm, ROUTER_PAD), 1).astype(F32)

    @pl.when((e == 0) & (f == 0))
    def _():
        logits = _dot(hb, rw_ref[...]) + rbias_ref[...]
        i1, i2, p1, p2 = _top2(logits, lane)
        gate_ref[...] = jnp.where(lane == i1, p1, 0.0) + jnp.where(lane == i2, p2, 0.0)
        acc_ref[...] = jnp.zeros_like(acc_ref)

    wg = wg_ref[...].astype(BF16)
    wu = wu_ref[...].astype(BF16)
    wd = wd_ref[...].astype(BF16)
    wg16_ref[...] = wg
    wu16_ref[...] = wu
    wd16_ref[...] = wd
    ge = jnp.sum(jnp.where(lane == e.astype(F32), gate_ref[...], 0.0), axis=-1, keepdims=True)
    act = (_silu(_dot(hb, wg)) * _dot(hb, wu)).astype(BF16)
    acc_ref[...] += ge * _dot(act, wd)

    @pl.when((e == pl.num_programs(0) - 1) & (f == pl.num_programs(1) - 1))
    def _():
        o_ref[...] = x_ref[...] + g2_ref[...] * acc_ref[...]


def _moe(h2, x1, g2, rw, rbias, wg, wu, wd):
    n = h2.shape[0]
    fe = D_FF_EXPERT
    tf = fe

    def const(shape):
        return pl.BlockSpec(shape, lambda e, f: (0,) * len(shape))

    up_spec = pl.BlockSpec((None, D_MODEL, tf), lambda e, f: (e, 0, f))
    down_spec = pl.BlockSpec((None, tf, D_MODEL), lambda e, f: (e, f, 0))
    mod_spec = pl.BlockSpec((None, None, n, D_MODEL), lambda e, f: (g2.layer, g2.idx, 0, 0))
    return pl.pallas_call(
        _moe_kernel,
        grid=(N_EXPERTS, fe // tf),
        in_specs=[const((n, D_MODEL)), const((n, D_MODEL)), mod_spec,
                  const((D_MODEL, ROUTER_PAD)), const((1, ROUTER_PAD)),
                  up_spec, up_spec, down_spec],
        out_specs=[const((n, D_MODEL)), up_spec, up_spec, down_spec],
        out_shape=[
            jax.ShapeDtypeStruct((n, D_MODEL), F32),
            jax.ShapeDtypeStruct(wg.shape, BF16),
            jax.ShapeDtypeStruct(wu.shape, BF16),
            jax.ShapeDtypeStruct(wd.shape, BF16),
        ],
        scratch_shapes=[pltpu.VMEM((n, D_MODEL), F32), pltpu.VMEM((n, ROUTER_PAD), F32)],
        compiler_params=_params(("arbitrary", "arbitrary")),
        name="moe",
    )(h2, x1, g2.arr, rw, rbias, wg, wu, wd)


def _top2(logits, lane):
    lg = jnp.where(lane < N_EXPERTS, logits, -jnp.inf)
    m1 = jnp.max(lg, axis=-1, keepdims=True)
    i1 = jnp.min(jnp.where(lg == m1, lane, float(ROUTER_PAD)), axis=-1, keepdims=True)
    lg2 = jnp.where(lane == i1, -jnp.inf, lg)
    m2 = jnp.max(lg2, axis=-1, keepdims=True)
    i2 = jnp.min(jnp.where(lg2 == m2, lane, float(ROUTER_PAD)), axis=-1, keepdims=True)
    e2 = jnp.exp(m2 - m1)
    p1 = 1.0 / (1.0 + e2)
    return i1, i2, p1, e2 * p1


def _route_pack(logits, tri):
    lane = lax.broadcasted_iota(jnp.int32, logits.shape, 1).astype(F32)
    i1, i2, p1, p2 = _top2(logits, lane)
    oh1 = jnp.where(lane == i1, 1.0, 0.0)
    oh2 = jnp.where(lane == i2, 1.0, 0.0)
    cnt1 = jnp.sum(oh1, axis=0, keepdims=True)
    cnt2 = jnp.sum(oh2, axis=0, keepdims=True)
    rank1 = jnp.sum(_dot(tri, oh1.astype(BF16)) * oh1, axis=-1, keepdims=True)
    rank2 = jnp.sum((_dot(tri, oh2.astype(BF16)) + cnt1) * oh2, axis=-1, keepdims=True)
    packed = jnp.zeros_like(logits)
    for k, val in enumerate((p1, p2, i1, i2, rank1, rank2)):
        packed = jnp.where(lane == float(k), val, packed)
    return packed, cnt1 + cnt2


def _route_tables(route_t, tile_counts, tm, n_tiles):
    n = route_t.shape[1]
    tile_cnt = tile_counts[:, 0, :N_EXPERTS].astype(jnp.int32)
    cnt = jnp.sum(tile_cnt, axis=0)
    gsz = ((cnt + tm - 1) // tm) * tm
    gend = jnp.cumsum(gsz)
    seg_start = (gend - gsz)[None, :] + jnp.cumsum(tile_cnt, axis=0) - tile_cnt
    seg_of_token = jnp.repeat(seg_start.T, n // tile_cnt.shape[0], axis=1)
    experts = jnp.arange(N_EXPERTS, dtype=jnp.int32)[:, None]
    pos = []
    for k in range(2):
        e_k = route_t[2 + k].astype(jnp.int32)
        rank_k = route_t[4 + k].astype(jnp.int32)
        pos.append(jnp.sum(jnp.where(e_k[None, :] == experts, seg_of_token, 0), axis=0) + rank_k)
    pos = jnp.concatenate(pos)
    tile_start = jnp.arange(n_tiles, dtype=jnp.int32) * tm
    tile_expert = jnp.sum((tile_start[:, None] >= gend[None, :]).astype(jnp.int32), axis=1)
    tile_expert = jnp.minimum(tile_expert, N_EXPERTS - 1)
    live = (gend[-1] // tm).reshape(1)
    return pos, tile_expert, live, gend - gsz + cnt, gend


def _row_copy_wait(src_hbm, dst, sem, rows):
    pltpu.make_async_copy(src_hbm.at[pl.ds(0, rows)], dst, sem).wait()


def _moe_dispatch_kernel(pos_ref, pad_lo_ref, pad_hi_ref, live_ref, h_ref, xs_hbm, rows, zbuf, rsem, zsem):
    i = pl.program_id(0)
    steps = pl.num_programs(0)
    tt = h_ref.shape[0]
    n = tt * steps
    tm = zbuf.shape[0]
    slot = i % 2

    def wait_rows(sl):
        for _ in range(2):
            pltpu.make_async_copy(rows.at[sl], xs_hbm.at[pl.ds(0, tt)], rsem.at[sl]).wait()

    @pl.when(i >= 2)
    def _():
        wait_rows(slot)

    rows[slot] = h_ref[...].astype(F32)
    base = i * tt
    for r in range(tt):
        row = rows.at[slot, pl.ds(r, 1)]
        pltpu.make_async_copy(row, xs_hbm.at[pl.ds(pos_ref[base + r], 1)], rsem.at[slot]).start(priority=0)
        pltpu.make_async_copy(row, xs_hbm.at[pl.ds(pos_ref[n + base + r], 1)], rsem.at[slot]).start(priority=1)

    @pl.when(i == steps - 1)
    def _():
        if steps > 1:
            wait_rows(1 - slot)
        wait_rows(slot)
        zbuf[...] = jnp.zeros(zbuf.shape, F32)

        def fill_row(p):
            return pltpu.make_async_copy(zbuf.at[pl.ds(0, 1)], xs_hbm.at[pl.ds(p, 1)], zsem)

        def fill_tile(t):
            return pltpu.make_async_copy(zbuf, xs_hbm.at[pl.ds(pl.multiple_of(t * tm, tm), tm)], zsem)

        for e in range(N_EXPERTS):
            lo, hi = pad_lo_ref[e], pad_hi_ref[e]
            lax.fori_loop(lo, hi, lambda p, c: (fill_row(p).start(), c)[1], 0)
            lax.fori_loop(lo, hi, lambda p, c: (fill_row(p).wait(), c)[1], 0)
        lo, hi = live_ref[0], xs_hbm.shape[0] // tm
        lax.fori_loop(lo, hi, lambda t, c: (fill_tile(t).start(), c)[1], 0)
        lax.fori_loop(lo, hi, lambda t, c: (fill_tile(t).wait(), c)[1], 0)


def _moe_dispatch(h2, pos, pad_lo, pad_hi, live, *, tt, tm, n_tiles):
    n = h2.shape[0]
    grid_spec = pltpu.PrefetchScalarGridSpec(
        num_scalar_prefetch=4,
        grid=(n // tt,),
        in_specs=[pl.BlockSpec((tt, D_MODEL), lambda i, *_: (i, 0))],
        out_specs=pl.BlockSpec(memory_space=pl.ANY),
        scratch_shapes=[pltpu.VMEM((2, tt, D_MODEL), F32), pltpu.VMEM((tm, D_MODEL), F32),
                        pltpu.SemaphoreType.DMA((2,)), pltpu.SemaphoreType.DMA(())],
    )
    return pl.pallas_call(
        _moe_dispatch_kernel,
        grid_spec=grid_spec,
        out_shape=jax.ShapeDtypeStruct((n_tiles * tm, D_MODEL), F32),
        compiler_params=_params(("arbitrary",)),
        name="moe_dispatch",
    )(pos, pad_lo, pad_hi, live, h2)


def _moe_expert_kernel(te_ref, live_ref, x_ref, wg_ref, wu_ref, wd_ref, y_ref):
    del te_ref
    is_live = pl.program_id(0) < live_ref[0]

    @pl.when(is_live)
    def _():
        y_ref[...] = _swiglu(x_ref[...].astype(BF16), wg_ref, wu_ref, wd_ref)

    @pl.when(jnp.logical_not(is_live))
    def _():
        y_ref[...] = jnp.zeros(y_ref.shape, F32)


def _moe_experts(xs, tile_expert, live, wg, wu, wd, *, tm, n_tiles):
    fe = D_FF_EXPERT

    def in_tile(t, te, live):
        return (jnp.minimum(t, live[0] - 1), 0)

    def out_tile(t, te, live):
        return (t, 0)

    def expert(t, te, live):
        return (te[t], 0, 0)

    grid_spec = pltpu.PrefetchScalarGridSpec(
        num_scalar_prefetch=2,
        grid=(n_tiles,),
        in_specs=[
            pl.BlockSpec((tm, D_MODEL), in_tile),
            pl.BlockSpec((None, D_MODEL, fe), expert),
            pl.BlockSpec((None, D_MODEL, fe), expert),
            pl.BlockSpec((None, fe, D_MODEL), expert),
        ],
        out_specs=pl.BlockSpec((tm, D_MODEL), out_tile),
    )
    return pl.pallas_call(
        _moe_expert_kernel,
        grid_spec=grid_spec,
        out_shape=jax.ShapeDtypeStruct(xs.shape, F32),
        compiler_params=_params(("arbitrary",)),
        name="moe_experts",
    )(tile_expert, live, xs, wg, wu, wd)


def _moe_combine_kernel(pos_ref, x_ref, g2_ref, r_ref, ys_hbm, o_ref, buf, sem):
    i = pl.program_id(0)
    nt = pl.num_programs(0)
    tt = x_ref.shape[0]
    n = nt * tt

    def fetch(tile, sl):
        base = tile * tt
        for r in range(tt):
            for s in range(2):
                row = pos_ref[s * n + base + r]
                pltpu.make_async_copy(ys_hbm.at[pl.ds(row, 1)], buf.at[sl, pl.ds(s * tt + r, 1)],
                                      sem.at[sl]).start(priority=s)

    @pl.when(i == 0)
    def _():
        fetch(0, 0)

    @pl.when(i + 1 < nt)
    def _():
        fetch(i + 1, (i + 1) % 2)

    sl = i % 2
    _row_copy_wait(ys_hbm, buf.at[sl], sem.at[sl], 2 * tt)
    r = r_ref[...]
    f = r[:, 0:1] * buf[sl, pl.ds(0, tt), :] + r[:, 1:2] * buf[sl, pl.ds(tt, tt), :]
    o_ref[...] = x_ref[...] + g2_ref[...] * f


def _moe_combine(x1, g2, route, pos, ys, *, tt):
    n = x1.shape[0]
    grid_spec = pltpu.PrefetchScalarGridSpec(
        num_scalar_prefetch=1,
        grid=(n // tt,),
        in_specs=[
            pl.BlockSpec((tt, D_MODEL), lambda i, pos: (i, 0)),
            _mod_spec(tt, n, g2),
            pl.BlockSpec((tt, ROUTER_PAD), lambda i, pos: (i, 0)),
            pl.BlockSpec(memory_space=pl.ANY),
        ],
        out_specs=pl.BlockSpec((tt, D_MODEL), lambda i, pos: (i, 0)),
        scratch_shapes=[pltpu.VMEM((2, 2 * tt, D_MODEL), F32), pltpu.SemaphoreType.DMA((2,))],
    )
    return pl.pallas_call(
        _moe_combine_kernel,
        grid_spec=grid_spec,
        out_shape=jax.ShapeDtypeStruct((n, D_MODEL), F32),
        compiler_params=_params(("arbitrary",)),
        name="moe_combine",
    )(pos, x1, g2.arr, route, ys)


def _moe_routed(h2, x1, g2, route, route_t, tile_counts, wg, wu, wd, *, tm, tt):
    n = h2.shape[0]
    n_tiles = (2 * n) // tm + N_EXPERTS
    pos, tile_expert, live, pad_lo, pad_hi = _route_tables(route_t, tile_counts, tm, n_tiles)
    xs = _moe_dispatch(h2, pos, pad_lo, pad_hi, live, tt=tt, tm=tm, n_tiles=n_tiles)
    ys = _moe_experts(xs, tile_expert, live, wg, wu, wd, tm=tm, n_tiles=n_tiles)
    return _moe_combine(x1, g2, route, pos, ys, tt=tt)


def _head_perm():
    idx = []
    for g in range(GROUP):
        for kv in range(N_KV):
            h = kv * GROUP + g
            idx.extend(range(h * HEAD_DIM, (h + 1) * HEAD_DIM))
    return np.asarray(idx, np.int32)


def _relayout_w_in(w):
    pts = np.cumsum([0, Q_W, KV_W, KV_W, GK_W, GK_W, GV_W, GATE_RANK, GV_W, D_MODEL, D_MODEL])
    qa, ka, va, qb, kb, vb, ga, rb, gta, gtb = [w[:, pts[i]:pts[i + 1]] for i in range(10)]
    qa = qa[:, _head_perm()]
    ga = jnp.pad(ga, ((0, 0), (0, GATE_PAD - GATE_RANK)))
    return jnp.concatenate([qa, ka, va, qb, kb, vb, rb, gta, gtb, ga], axis=1).astype(BF16)


def kernel(x_prompt, x_sample, cache_k, cache_v, state_gla, c_prompt, c_sample, ada_w, ada_b, norm1_g, norm2_g, w_in, q_norm_g, k_norm_g, attn_sinks, gla_wa2, gla_ba, gla_norm_g, w_branch_a, w_branch_b, w_out, ffn_w_gate, ffn_w_up, ffn_w_down, router_w, router_b, moe_w_gate, moe_w_up, moe_w_down):
    n_p = BATCH * SEQ
    xp = x_prompt.reshape(n_p, D_MODEL)
    xs = x_sample.reshape(DEC_BATCH, D_MODEL)

    c_pad = -(BATCH + DEC_BATCH) % 8
    c_all = jnp.pad(jnp.concatenate([c_sample, c_prompt], axis=0), ((0, c_pad), (0, 0)))
    mod_sample, mod_prompt = _ada(c_all, ada_w, ada_b, DEC_BATCH, BATCH)
    state_rows = state_gla.reshape(DEPTH, DEC_BATCH, GK_W, GLA_DV)

    bd = jnp.asarray(np.kron(np.eye(N_HEADS), np.ones((HEAD_DIM, HEAD_DIM))), BF16)
    ut = jnp.asarray(_gla_cum_matrix(), BF16)
    gla_upper, gla_pairs = (jnp.asarray(m) for m in _gla_level_masks())
    perm = _head_perm()
    swa_bias = jnp.asarray(_swa_sample_bias())
    swa_prompt_bias = jnp.asarray(_swa_prompt_bias())

    kp_l, vp_l, sp_l, ks_l, vs_l, ss_l = [], [], [], [], [], []
    for l in range(DEPTH):
        mod_p = [_Mod(mod_prompt, l, i) for i in range(N_MOD)]
        mod_s = [_Mod(mod_sample, l, i) for i in range(N_MOD)]

        w = _relayout_w_in(w_in[l])
        qg = (jnp.tile(q_norm_g[l], N_HEADS) * (HEAD_DIM ** -0.5)).reshape(1, Q_W)
        kg = jnp.tile(k_norm_g[l], N_KV).reshape(1, KV_W)
        wa2 = jnp.pad(gla_wa2[l], ((0, GATE_PAD - GATE_RANK), (0, 0))).astype(BF16)
        ba = gla_ba[l].reshape(1, GK_W)
        n1 = norm1_g[l].reshape(1, D_MODEL)
        n2 = norm2_g[l].reshape(1, D_MODEL)
        gn = gla_norm_g[l].reshape(1, GLA_DV)
        wpa = w_branch_a[l][perm].astype(BF16)
        wpb = w_branch_b[l].astype(BF16)
        wo = w_out[l].astype(BF16)
        sink_rows = jnp.broadcast_to(attn_sinks[l][_sample_head_of_row()][:, None], (N_HEADS, LANES))

        q, k, v, gq, gk, gv, la, rbs, sga, sgb = _mixin(
            xp, mod_p[0], mod_p[1], n1, w, bd, qg, kg, wa2, ba, tm=ROW_TILE_RESIDENT)
        a_out = _swa_prompt(attn_sinks[l], q, k, v, swa_prompt_bias)
        g_out, s_fin = _gla_prompt(gq, gk, la, gv, rbs, gn, ut, gla_upper, gla_pairs)
        is_moe = l % 2 == 1
        if is_moe:
            rw = jnp.pad(router_w[l // 2], ((0, 0), (0, ROUTER_PAD - N_EXPERTS))).astype(BF16)
            rbias = jnp.pad(router_b[l // 2], (0, ROUTER_PAD - N_EXPERTS)).reshape(1, ROUTER_PAD)
        merged = _merge(xp, a_out, g_out, sga, sgb, wpa, wpb, wo, mod_p[2], mod_p[3], mod_p[4], n2,
                        tm=ROW_TILE_RESIDENT, router=(rw, rbias) if is_moe else None)
        x1, h2 = merged[0], merged[1]
        for cache, full in ((kp_l, k), (vp_l, v)):
            tail = full.reshape(BATCH, SEQ, KV_W)[:, SEQ - WINDOW:]
            cache.append(tail.reshape(BATCH, WINDOW, N_KV, HEAD_DIM))
        sp_l.append(s_fin.reshape(BATCH, GLA_HEADS, GLA_DK, GLA_DV))

        qs, ksn, vsn, gqs, gks, gvs, las, rbss, sgas, sgbs = _mixin(
            xs, mod_s[0], mod_s[1], n1, w, bd, qg, kg, wa2, ba, tm=DEC_BATCH)
        a_s, nk, nv = _swa_sample(
            qs.astype(F32).reshape(DEC_BATCH, N_HEADS, HEAD_DIM),
            ksn.reshape(DEC_BATCH, N_KV, HEAD_DIM), vsn.reshape(DEC_BATCH, N_KV, HEAD_DIM),
            cache_k[l].reshape(1, DEC_BATCH, CACHE_ROWS, HEAD_DIM),
            cache_v[l].reshape(1, DEC_BATCH, CACHE_ROWS, HEAD_DIM), swa_bias, sink_rows, layer=0)
        g_s, s_new = _gla_sample(gqs, gks, las, gvs, rbss, gn, state_rows, layer=l)
        x1s, h2s = _merge(xs, a_s.reshape(DEC_BATCH, Q_W), g_s, sgas, sgbs, wpa, wpb, wo,
                          mod_s[2], mod_s[3], mod_s[4], n2, tm=DEC_BATCH)
        ks_l.append(nk.reshape(DEC_BATCH, WINDOW, N_KV, HEAD_DIM))
        vs_l.append(nv.reshape(DEC_BATCH, WINDOW, N_KV, HEAD_DIM))
        ss_l.append(s_new.reshape(DEC_BATCH, GLA_HEADS, GLA_DK, GLA_DV))

        i = l // 2
        if not is_moe:
            xs, wg, wu, wd = _ffn(h2s, x1s, mod_s[5], ffn_w_gate[i], ffn_w_up[i], ffn_w_down[i],
                                  tm=DEC_BATCH, tf=D_FF // 2, emit_bf16=True)
            xp = _ffn(h2, x1, mod_p[5], wg, wu, wd, tm=ROW_TILE_SWIGLU, tf=D_FF // 2)
        else:
            xs, wg, wu, wd = _moe(h2s, x1s, mod_s[5], rw, rbias, moe_w_gate[i], moe_w_up[i], moe_w_down[i])
            xp = _moe_routed(h2, x1, mod_p[5], *merged[2:5], wg, wu, wd,
                             tm=ROW_TILE_SWIGLU, tt=MOE_TOKEN_TILE)

    return (xp.reshape(BATCH, SEQ, D_MODEL), xs.reshape(DEC_BATCH, 1, D_MODEL),
            jnp.stack(kp_l), jnp.stack(vp_l), jnp.stack(sp_l),
            jnp.stack(ks_l), jnp.stack(vs_l), jnp.stack(ss_l))
```

```python
from typing import NamedTuple

import jax
import jax.numpy as jnp
import numpy as np
from jax import lax
from jax.experimental import pallas as pl
from jax.experimental.pallas import tpu as pltpu

D_MODEL = 1024
BATCH = 4
SEQ = 4096
DEPTH = 2
DEC_BATCH = 128
N_HEADS = 8
N_KV = 2
HEAD_DIM = 64
GROUP = N_HEADS // N_KV
WINDOW = 128
GLA_HEADS = 4
GLA_DK = 64
GLA_DV = 128
GATE_RANK = 16
GATE_TAU = 16.0
D_FF = 2816
N_EXPERTS = 8
D_FF_EXPERT = 1408
EPS = 1e-6

Q_W = N_HEADS * HEAD_DIM
KV_W = N_KV * HEAD_DIM
GK_W = GLA_HEADS * GLA_DK
GV_W = GLA_HEADS * GLA_DV

LANES = 128
GATE_PAD = LANES
ROUTER_PAD = LANES
VMEM_LIMIT = 56 * 1024 * 1024

ROW_TILE_RESIDENT = 1024
ROW_TILE_SWIGLU = 512
MOE_TOKEN_TILE = 512

F32 = jnp.float32
BF16 = jnp.bfloat16

_C_Q = 0
_C_K = _C_Q + Q_W
_C_V = _C_K + KV_W
_C_GQ = _C_V + KV_W
_C_GK = _C_GQ + GK_W
_C_GV = _C_GK + GK_W
_C_RB = _C_GV + GV_W
_C_GA = _C_RB + GV_W
_C_GB = _C_GA + D_MODEL
_C_LR = _C_GB + D_MODEL
PROJ_PAD = _C_LR + GATE_PAD

GLA_CHUNK = 128
GLA_LEVELS = 7
GLA_MXU_LEVELS = 3
GLA_CHUNKS_PER_STEP = 4


def _params(sem, vmem=VMEM_LIMIT):
    return pltpu.CompilerParams(dimension_semantics=sem, vmem_limit_bytes=vmem)


def _dot(a, b):
    return jnp.dot(a, b, preferred_element_type=F32)


def _dot_nt(a, b):
    return lax.dot_general(a, b, (((1,), (1,)), ((), ())), preferred_element_type=F32)


def _dot_tn(a, b):
    return lax.dot_general(a, b, (((0,), (0,)), ((), ())), preferred_element_type=F32)


def _sigmoid(x):
    return 0.5 * jnp.tanh(0.5 * x) + 0.5


def _silu(x):
    return x * _sigmoid(x)


N_MOD = 6


class _Mod(NamedTuple):
    arr: jax.Array
    layer: int
    idx: int


def _ada_kernel(c_ref, w_ref, b_ref, os_ref, op_ref):
    c = c_ref[...]
    mod = _dot(_silu(c).astype(BF16), w_ref[...].astype(BF16)) + b_ref[...]
    n_s = os_ref.shape[0]
    os_ref[...] = mod[0:n_s]
    for b in range(op_ref.shape[0]):
        op_ref[b] = mod[n_s + b:n_s + b + 1]


def _ada(c_all, ada_w, ada_b, n_sample, n_prompt):
    rows = c_all.shape[0]
    return pl.pallas_call(
        _ada_kernel,
        grid=(DEPTH, N_MOD),
        in_specs=[
            pl.BlockSpec((rows, D_MODEL), lambda l, j: (0, 0)),
            pl.BlockSpec((None, D_MODEL, D_MODEL), lambda l, j: (l, 0, j)),
            pl.BlockSpec((None, 1, D_MODEL), lambda l, j: (l, 0, j)),
        ],
        out_specs=[
            pl.BlockSpec((None, None, n_sample, D_MODEL), lambda l, j: (l, j, 0, 0)),
            pl.BlockSpec((None, None, n_prompt, 1, D_MODEL), lambda l, j: (l, j, 0, 0, 0)),
        ],
        out_shape=[
            jax.ShapeDtypeStruct((DEPTH, N_MOD, n_sample, D_MODEL), F32),
            jax.ShapeDtypeStruct((DEPTH, N_MOD, n_prompt, 1, D_MODEL), F32),
        ],
        compiler_params=_params(("parallel", "parallel")),
        name="ada_mod",
    )(c_all, ada_w, ada_b.reshape(DEPTH, 1, N_MOD * D_MODEL))


def _mixin_kernel(x_ref, sh_ref, sc_ref, n1_ref, w_ref, bd_ref, qg_ref, kg_ref, wa2_ref, ba_ref,
                  q_ref, k_ref, v_ref, gq_ref, gk_ref, gv_ref, la_ref, rb_ref, sga_ref, sgb_ref):
    x = x_ref[...]
    ms = jnp.mean(x * x, axis=-1, keepdims=True)
    h = x * lax.rsqrt(ms + EPS) * n1_ref[...]
    h = h * (1.0 + sc_ref[...]) + sh_ref[...]
    hb = h.astype(BF16)

    def proj(a, b):
        return _dot(hb, w_ref[:, a:b])

    q = proj(_C_Q, _C_K)
    ssq = _dot((q * q).astype(BF16), bd_ref[...])
    q_ref[...] = (q * lax.rsqrt(ssq * (1.0 / HEAD_DIM) + EPS) * qg_ref[...]).astype(BF16)
    k = proj(_C_K, _C_V)
    ssk = _dot((k * k).astype(BF16), bd_ref[0:KV_W, 0:KV_W])
    k_ref[...] = k * lax.rsqrt(ssk * (1.0 / HEAD_DIM) + EPS) * kg_ref[...]
    v_ref[...] = proj(_C_V, _C_GQ)
    gq_ref[...] = proj(_C_GQ, _C_GK) * (GLA_DK ** -0.5)
    gk_ref[...] = proj(_C_GK, _C_GV)
    gv_ref[...] = proj(_C_GV, _C_RB)
    rb_ref[...] = _silu(proj(_C_RB, _C_GA)).astype(BF16)
    sga_ref[...] = _sigmoid(proj(_C_GA, _C_GB)).astype(BF16)
    sgb_ref[...] = _sigmoid(proj(_C_GB, _C_LR)).astype(BF16)
    ga = proj(_C_LR, PROJ_PAD)
    xg = _dot(ga.astype(BF16), wa2_ref[...]) + ba_ref[...]
    la_ref[...] = (jnp.minimum(xg, 0.0) - jnp.log1p(jnp.exp(-jnp.abs(xg)))) * (1.0 / GATE_TAU)


def _mixin_cast_kernel(x_ref, sh_ref, sc_ref, n1_ref, w_ref, *rest):
    wre_ref = rest[-1]
    for j in range(N_HEADS):
        h = (j % N_KV) * GROUP + j // N_KV
        wre_ref[:, j * HEAD_DIM:(j + 1) * HEAD_DIM] = w_ref[:, h * HEAD_DIM:(h + 1) * HEAD_DIM].astype(BF16)
    wre_ref[:, _C_K:_C_RB] = w_ref[:, _C_K:_C_RB].astype(BF16)
    tail = _C_LR - _C_RB
    wre_ref[:, _C_RB:_C_LR] = w_ref[:, _C_RB + GATE_RANK:_C_RB + GATE_RANK + tail].astype(BF16)
    wre_ref[:, _C_LR:_C_LR + GATE_RANK] = w_ref[:, _C_RB:_C_RB + GATE_RANK].astype(BF16)
    wre_ref[:, _C_LR + GATE_RANK:PROJ_PAD] = jnp.zeros((D_MODEL, GATE_PAD - GATE_RANK), BF16)
    _mixin_kernel(x_ref, sh_ref, sc_ref, n1_ref, wre_ref, *rest[:-1])


def _mixin(x, sh, sc, n1, w, bd, qg, kg, wa2, ba, *, tm, emit_w=False):
    n = x.shape[0]
    nt = n // tm
    assert not emit_w or nt == 1

    def row(i):
        return (i, 0)

    def const(shape):
        return pl.BlockSpec(shape, lambda i: (0,) * len(shape), pipeline_mode=pl.Buffered(1))

    def out(width, dtype):
        return pl.BlockSpec((tm, width), row), jax.ShapeDtypeStruct((n, width), dtype)

    outs = [out(Q_W, BF16), out(KV_W, F32), out(KV_W, F32), out(GK_W, F32), out(GK_W, F32),
            out(GV_W, F32), out(GK_W, F32), out(GV_W, BF16), out(D_MODEL, BF16), out(D_MODEL, BF16)]
    if emit_w:
        outs.append((pl.BlockSpec((D_MODEL, PROJ_PAD), lambda i: (0, 0)),
                     jax.ShapeDtypeStruct((D_MODEL, PROJ_PAD), BF16)))
    return pl.pallas_call(
        _mixin_cast_kernel if emit_w else _mixin_kernel,
        grid=(nt,),
        in_specs=[
            pl.BlockSpec((tm, D_MODEL), row), _mod_spec(tm, n, sh), _mod_spec(tm, n, sc), const((1, D_MODEL)),
            const(w.shape), const((Q_W, Q_W)), const((1, Q_W)), const((1, KV_W)),
            const((GATE_PAD, GK_W)), const((1, GK_W)),
        ],
        out_specs=[o[0] for o in outs],
        out_shape=[o[1] for o in outs],
        compiler_params=_params(("parallel",)),
        name="mixer_in",
    )(x, sh.arr, sc.arr, n1, w, bd, qg, kg, wa2, ba)


SWA_BLOCKS = 8


def _head_slope(h):
    return float(2.0 ** (-8.0 * (h + 1) / N_HEADS))


def _swa_prompt_bias():
    blk = WINDOW
    dist = np.arange(blk)[:, None] + blk - np.arange(2 * blk)[None, :]
    slopes = np.asarray([_head_slope(h) for h in range(N_HEADS)])[:, None, None]
    return np.where((dist >= 0) & (dist <= WINDOW), -slopes * dist, -np.inf).astype(np.float32)


def _swa_prompt_kernel(sink_ref, q_ref, kp_ref, kc_ref, vp_ref, vc_ref, bias_ref, o_ref):
    n = pl.program_id(1)
    blk = WINDOW
    kall = jnp.concatenate([kp_ref[...], kc_ref[...]], axis=0).astype(BF16)
    vall = jnp.concatenate([vp_ref[...], vc_ref[...]], axis=0).astype(BF16)
    col = lax.broadcasted_iota(jnp.int32, (blk, 2 * blk), 1)
    first_key = jnp.where(n > 0, 0, blk)
    klane = lax.broadcasted_iota(jnp.int32, kall.shape, 1)
    kall_kv = [jnp.where((klane // HEAD_DIM) == kv, kall, jnp.zeros_like(kall)) for kv in range(N_KV)]
    olane = lax.broadcasted_iota(jnp.int32, (blk, KV_W), 1)
    q = q_ref[...]
    for j in range(SWA_BLOCKS):
        vv = vall[j * blk:(j + 2) * blk]
        outs = []
        for g in range(GROUP):
            qp = q[j * blk:(j + 1) * blk, g * KV_W:(g + 1) * KV_W]
            pair = []
            for kv in range(N_KV):
                h = kv * GROUP + g
                s = _dot_nt(qp, kall_kv[kv][j * blk:(j + 2) * blk]) + bias_ref[h]
                if j == 0:
                    s = jnp.where(col >= first_key, s, -jnp.inf)
                sink = sink_ref[h]
                m = jnp.maximum(jnp.max(s, axis=-1, keepdims=True), sink)
                p = jnp.exp(s - m)
                den = jnp.sum(p, axis=-1, keepdims=True) + jnp.exp(sink - m)
                pair.append(_dot(p.astype(BF16), vv) * (1.0 / den))
            outs.append(jnp.where(olane < HEAD_DIM, pair[0], pair[1]))
        o_ref[j * blk:(j + 1) * blk, :] = jnp.concatenate(outs, axis=-1).astype(BF16)


def _swa_prompt(sinks, q, k, v, bias):
    nb = SEQ // WINDOW
    steps = nb // SWA_BLOCKS
    tq = SWA_BLOCKS * WINDOW

    def cur(b, n):
        return (b * steps + n, 0)

    def prev(b, n):
        return (b * nb + jnp.maximum(n * SWA_BLOCKS - 1, 0), 0)

    return pl.pallas_call(
        _swa_prompt_kernel,
        grid=(BATCH, steps),
        in_specs=[
            pl.BlockSpec(memory_space=pltpu.SMEM),
            pl.BlockSpec((tq, Q_W), cur),
            pl.BlockSpec((WINDOW, KV_W), prev), pl.BlockSpec((tq, KV_W), cur),
            pl.BlockSpec((WINDOW, KV_W), prev), pl.BlockSpec((tq, KV_W), cur),
            pl.BlockSpec(bias.shape, lambda b, n: (0, 0, 0), pipeline_mode=pl.Buffered(1)),
        ],
        out_specs=pl.BlockSpec((tq, Q_W), cur),
        out_shape=jax.ShapeDtypeStruct((BATCH * SEQ, Q_W), BF16),
        compiler_params=_params(("parallel", "parallel")),
        name="swa_prompt",
    )(sinks, q, k, k, v, v, bias)


SAMPLE_TB = 8
CACHE_ROWS = WINDOW * N_KV


def _sample_head_of_row():
    j = np.arange(N_HEADS)
    return (j % N_KV) * GROUP + j // N_KV


def _swa_sample_bias():
    j = np.arange(N_HEADS)[:, None]
    c = np.arange(CACHE_ROWS)[None, :]
    slope = 2.0 ** (-8.0 * (_sample_head_of_row()[:, None] + 1) / N_HEADS)
    bias = -slope * (WINDOW - c // N_KV)
    return np.where(c % N_KV == j % N_KV, bias, -np.inf).astype(np.float32)


def _swa_sample_kernel(q_ref, kn_ref, vn_ref, ck_ref, cv_ref, bias_ref, sk_ref, o_ref, ok_ref, ov_ref):
    rows = CACHE_ROWS
    sink = sk_ref[...][:, 0:1]
    q = q_ref[...]
    kn = kn_ref[...]
    vn = vn_ref[...]
    kc = ck_ref[...]
    vc = cv_ref[...]
    kn8 = jnp.concatenate([kn] * GROUP, axis=1)
    vn8 = jnp.concatenate([vn] * GROUP, axis=1)
    s = lax.dot_general(q.astype(BF16), kc.astype(BF16), (((2,), (2,)), ((0,), (0,))),
                        preferred_element_type=F32) + bias_ref[...]
    s_new = jnp.sum(q * kn8, axis=-1, keepdims=True)
    m = jnp.maximum(jnp.maximum(jnp.max(s, axis=-1, keepdims=True), s_new), sink)
    p = jnp.exp(s - m)
    p_new = jnp.exp(s_new - m)
    den = jnp.sum(p, axis=-1, keepdims=True) + p_new + jnp.exp(sink - m)
    o = lax.dot_general(p.astype(BF16), vc.astype(BF16), (((2,), (1,)), ((0,), (0,))),
                        preferred_element_type=F32) + p_new * vn8
    o_ref[...] = o * (1.0 / den)
    ok_ref[:, pl.ds(0, rows - N_KV), :] = kc[:, N_KV:, :]
    ok_ref[:, pl.ds(rows - N_KV, N_KV), :] = kn
    ov_ref[:, pl.ds(0, rows - N_KV), :] = vc[:, N_KV:, :]
    ov_ref[:, pl.ds(rows - N_KV, N_KV), :] = vn


def _swa_sample(q, kn, vn, ck, cv, bias, sinks, *, layer):
    tb = 2 * SAMPLE_TB
    nb = DEC_BATCH // tb

    def const(shape):
        return pl.BlockSpec(shape, lambda i: (0,) * len(shape))

    def per_seq(*dims):
        return pl.BlockSpec((tb,) + dims, lambda i: (i, 0, 0))

    cache_spec = pl.BlockSpec((None, tb, CACHE_ROWS, HEAD_DIM), lambda i: (layer, i, 0, 0))
    return pl.pallas_call(
        _swa_sample_kernel,
        grid=(nb,),
        in_specs=[per_seq(N_HEADS, HEAD_DIM), per_seq(N_KV, HEAD_DIM), per_seq(N_KV, HEAD_DIM),
                  cache_spec, cache_spec,
                  const((N_HEADS, CACHE_ROWS)), const((N_HEADS, LANES))],
        out_specs=[per_seq(N_HEADS, HEAD_DIM), per_seq(CACHE_ROWS, HEAD_DIM), per_seq(CACHE_ROWS, HEAD_DIM)],
        out_shape=[
            jax.ShapeDtypeStruct((DEC_BATCH, N_HEADS, HEAD_DIM), F32),
            jax.ShapeDtypeStruct((DEC_BATCH, CACHE_ROWS, HEAD_DIM), F32),
            jax.ShapeDtypeStruct((DEC_BATCH, CACHE_ROWS, HEAD_DIM), F32),
        ],
        compiler_params=_params(("parallel",)),
        name="swa_sample",
    )(q, kn, vn, ck, cv, bias, sinks)


def _gla_cum_matrix():
    c = GLA_CHUNK
    tri = np.tril(np.ones((c, c), np.float32))
    i = np.arange(c)
    blocks = []
    for lvl in range(GLA_MXU_LEVELS):
        half = 1 << lvl
        mid = (i // (2 * half)) * (2 * half) + half - 1
        blocks.append(tri - tri[mid])
    blocks.append(tri)
    return np.concatenate(blocks, axis=0)


def _split3(x):
    hi = x.astype(BF16)
    r1 = x - hi.astype(F32)
    mid = r1.astype(BF16)
    lo = (r1 - mid.astype(F32)).astype(BF16)
    return hi, mid, lo


def _gla_level_masks():
    cl = GLA_CHUNK
    r = np.arange(cl)
    upper = np.stack([np.broadcast_to(((r >> lvl) & 1)[:, None], (cl, GK_W)) for lvl in range(GLA_LEVELS)])
    ri = np.tile(r, GLA_HEADS)[:, None]
    pairs = [(ri >> (lvl + 1)) == (r[None, :] >> (lvl + 1)) for lvl in range(GLA_LEVELS)]
    pairs.append(ri == r[None, :])
    return upper.astype(np.float32), np.stack(pairs).astype(np.float32)


def _gla_prompt_kernel(q_ref, k_ref, la_ref, v_ref, rb_ref, gn_ref, ut_ref, up_ref, pm_ref,
                       o_ref, s_ref, st_ref):
    c = pl.program_id(1)
    cl = GLA_CHUNK

    @pl.when(c == 0)
    def _():
        st_ref[0] = jnp.zeros(st_ref.shape[1:], F32)

    ut = ut_ref[...]
    lane = lax.broadcasted_iota(jnp.int32, (cl, GK_W), 1)
    head_of_lane = lane // GLA_DK
    ones = jnp.ones((cl, LANES), BF16)
    state = [st_ref[c % 2, h] for h in range(GLA_HEADS)]
    for cc in range(GLA_CHUNKS_PER_STEP):
        rows = slice(cc * cl, (cc + 1) * cl)
        hi, mid, lo = _split3(la_ref[rows, :])
        tall = _dot(ut, hi) + _dot(ut, mid) + _dot(ut, lo)
        q = q_ref[rows, :]
        k = k_ref[rows, :]
        attn_all = jnp.zeros((GLA_HEADS * cl, cl), F32)
        cum = tall[GLA_MXU_LEVELS * cl:(GLA_MXU_LEVELS + 1) * cl]
        for lvl in range(GLA_LEVELS + 1):
            if lvl < GLA_LEVELS:
                if lvl < GLA_MXU_LEVELS:
                    t_lvl = tall[lvl * cl:(lvl + 1) * cl]
                else:
                    half = 1 << lvl
                    mids = [jnp.broadcast_to(cum[b0 + half - 1:b0 + half, :], (2 * half, GK_W))
                            for b0 in range(0, cl, 2 * half)]
                    t_lvl = cum - (mids[0] if len(mids) == 1 else jnp.concatenate(mids, axis=0))
                e = jnp.exp(-jnp.abs(t_lvl))
                e_up = e * up_ref[lvl]
                qt = (q * e_up).astype(BF16)
                kt = (k * (e - e_up)).astype(BF16)
            else:
                qt = q.astype(BF16)
                kt = k.astype(BF16)
            q_heads = jnp.concatenate(
                [jnp.where(head_of_lane == h, qt, jnp.zeros_like(qt)) for h in range(GLA_HEADS)], axis=0)
            attn_all = attn_all + _dot_nt(q_heads, kt) * pm_ref[lvl]

        last = cum[cl - 1:cl, :]
        qe = (q * jnp.exp(cum)).astype(BF16)
        kd = (k * jnp.exp(last - cum)).astype(BF16)
        dec = jnp.exp(_dot_tn(hi, ones) + _dot_tn(mid, ones) + _dot_tn(lo, ones))
        v = v_ref[rows, :]
        rb = rb_ref[rows, :]
        outs = []
        for h in range(GLA_HEADS):
            vh = v[:, h * GLA_DV:(h + 1) * GLA_DV].astype(BF16)
            o = _dot(attn_all[h * cl:(h + 1) * cl].astype(BF16), vh) + _dot(qe, state[h].astype(BF16))
            kdh = jnp.where(head_of_lane == h, kd, jnp.zeros_like(kd))
            state[h] = state[h] * dec + _dot_tn(kdh, vh)
            ms = jnp.mean(o * o, axis=-1, keepdims=True)
            g = o * lax.rsqrt(ms + EPS) * gn_ref[...]
            outs.append(g * rb[:, h * GLA_DV:(h + 1) * GLA_DV].astype(F32))
        o_ref[rows, :] = jnp.concatenate(outs, axis=-1).astype(BF16)

    for h in range(GLA_HEADS):
        st_ref[(c + 1) % 2, h] = state[h]

    @pl.when(c == pl.num_programs(1) - 1)
    def _():
        fin = (SEQ // (GLA_CHUNK * GLA_CHUNKS_PER_STEP)) % 2
        s_ref[...] = st_ref[fin, 0] + st_ref[fin, 1] + st_ref[fin, 2] + st_ref[fin, 3]


def _gla_prompt(gq, gk, la, gv, rbs, gn, ut, upper, pairs):
    cl = GLA_CHUNK * GLA_CHUNKS_PER_STEP
    nc = SEQ // cl

    def row(b, c):
        return (b * nc + c, 0)

    def const(shape):
        return pl.BlockSpec(shape, lambda b, c: (0,) * len(shape), pipeline_mode=pl.Buffered(1))

    return pl.pallas_call(
        _gla_prompt_kernel,
        grid=(BATCH, nc),
        in_specs=[
            pl.BlockSpec((cl, GK_W), row), pl.BlockSpec((cl, GK_W), row), pl.BlockSpec((cl, GK_W), row),
            pl.BlockSpec((cl, GV_W), row), pl.BlockSpec((cl, GV_W), row),
            const((1, GLA_DV)), const(ut.shape), const(upper.shape), const(pairs.shape),
        ],
        out_specs=[
            pl.BlockSpec((cl, GV_W), row),
            pl.BlockSpec((None, GK_W, GLA_DV), lambda b, c: (b, 0, 0)),
        ],
        out_shape=[
            jax.ShapeDtypeStruct((BATCH * SEQ, GV_W), BF16),
            jax.ShapeDtypeStruct((BATCH, GK_W, GLA_DV), F32),
        ],
        scratch_shapes=[pltpu.VMEM((2, GLA_HEADS, GK_W, GLA_DV), F32)],
        compiler_params=_params(("parallel", "arbitrary")),
        name="gla_prompt",
    )(gq, gk, la, gv, rbs, gn, ut, upper, pairs)


def _gla_sample_kernel(q_ref, k_ref, la_ref, v_ref, rb_ref, gn_ref, s_ref, o_ref, so_ref):
    tb = SAMPLE_TB
    dec = jnp.exp(la_ref[...])
    pieces = []
    for x in (dec, k_ref[...], q_ref[...]):
        hi, mid, lo = _split3(x)
        stacked = jnp.concatenate(
            [hi.astype(F32), mid.astype(F32), lo.astype(F32), jnp.zeros_like(x)], axis=0)
        pieces.append(stacked.astype(BF16))
    prow = lax.broadcasted_iota(jnp.int32, (4 * tb, LANES), 0)
    rb = rb_ref[...].astype(F32)
    v = v_ref[...]
    for bi in range(tb):
        sel = jnp.where((prow % tb) == bi, 1.0, 0.0).astype(BF16)
        a_col, k_col, q_col = [_dot_tn(p, sel) for p in pieces]
        for h in range(GLA_HEADS):
            rs = slice(h * GLA_DK, (h + 1) * GLA_DK)
            vs = slice(h * GLA_DV, (h + 1) * GLA_DV)
            s_new = a_col[rs] * s_ref[bi, rs, :] + k_col[rs] * v[bi:bi + 1, vs]
            so_ref[bi, rs, :] = s_new
            o = jnp.sum(q_col[rs] * s_new, axis=0, keepdims=True)
            ms = jnp.mean(o * o, axis=-1, keepdims=True)
            g = o * lax.rsqrt(ms + EPS) * gn_ref[...]
            o_ref[bi:bi + 1, vs] = g * rb[bi:bi + 1, vs]


def _gla_sample(gq, gk, la, gv, rbs, gn, state, *, layer):
    tb = SAMPLE_TB
    nb = DEC_BATCH // tb

    def row(w):
        return pl.BlockSpec((tb, w), lambda i: (i, 0))

    st_spec = pl.BlockSpec((tb, GK_W, GLA_DV), lambda i: (i, 0, 0))
    st_in_spec = pl.BlockSpec((None, tb, GK_W, GLA_DV), lambda i: (layer, i, 0, 0))
    return pl.pallas_call(
        _gla_sample_kernel,
        grid=(nb,),
        in_specs=[row(GK_W), row(GK_W), row(GK_W), row(GV_W), row(GV_W),
                  pl.BlockSpec((1, GLA_DV), lambda i: (0, 0)), st_in_spec],
        out_specs=[row(GV_W), st_spec],
        out_shape=[
            jax.ShapeDtypeStruct((DEC_BATCH, GV_W), F32),
            jax.ShapeDtypeStruct((DEC_BATCH, GK_W, GLA_DV), F32),
        ],
        compiler_params=_params(("parallel",)),
        name="gla_sample",
    )(gq, gk, la, gv, rbs, gn, state)


def _merge_kernel(x_ref, a_ref, g_ref, sga_ref, sgb_ref, wpa_ref, wpb_ref, wo_ref,
                  g1_ref, sh_ref, sc_ref, n2_ref, *rest):
    with_router = len(rest) == 8
    x1_ref, h2_ref = rest[3:5] if with_router else rest
    ya = _dot(a_ref[...].astype(BF16), wpa_ref[...])
    yb = _dot(g_ref[...].astype(BF16), wpb_ref[...])
    merged = sga_ref[...].astype(F32) * ya + sgb_ref[...].astype(F32) * yb
    mix = _dot(merged.astype(BF16), wo_ref[...])
    x1 = x_ref[...] + g1_ref[...] * mix
    x1_ref[...] = x1
    ms = jnp.mean(x1 * x1, axis=-1, keepdims=True)
    h = x1 * lax.rsqrt(ms + EPS) * n2_ref[...]
    h2 = h * (1.0 + sc_ref[...]) + sh_ref[...]
    h2_ref[...] = h2.astype(h2_ref.dtype)
    if with_router:
        rw_ref, rbias_ref, tri_ref = rest[0:3]
        route_ref, route_t_ref, cnt_ref = rest[5:8]
        logits = _dot(h2.astype(BF16), rw_ref[...]) + rbias_ref[...]
        packed, counts = _route_pack(logits, tri_ref[...])
        route_ref[...] = packed
        route_t_ref[...] = packed.T[0:route_t_ref.shape[0], :]
        cnt_ref[...] = jnp.broadcast_to(counts, cnt_ref.shape)


def _mod_spec(tm, n, mod):
    if mod.arr.ndim == 4:
        return pl.BlockSpec((None, None, tm, D_MODEL), lambda i, *_: (mod.layer, mod.idx, i, 0))
    tiles_per_seq = (n // mod.arr.shape[2]) // tm
    return pl.BlockSpec((None, None, None, 1, D_MODEL),
                        lambda i, *_: (mod.layer, mod.idx, i // tiles_per_seq, 0, 0))


def _merge(x, a, g, sga, sgb, wpa, wpb, wo, g1, sh2, sc2, n2, *, tm, router=None):
    n = x.shape[0]

    def row(w):
        return pl.BlockSpec((tm, w), lambda i: (i, 0))

    def const(shape):
        return pl.BlockSpec(shape, lambda i: (0,) * len(shape))

    in_specs = [row(D_MODEL), row(a.shape[1]), row(GV_W), row(D_MODEL), row(D_MODEL),
                const(wpa.shape), const(wpb.shape), const(wo.shape),
                _mod_spec(tm, n, g1), _mod_spec(tm, n, sh2), _mod_spec(tm, n, sc2), const((1, D_MODEL))]
    out_specs = [row(D_MODEL), row(D_MODEL)]
    out_shape = [jax.ShapeDtypeStruct((n, D_MODEL), F32), jax.ShapeDtypeStruct((n, D_MODEL), BF16)]
    args = [x, a, g, sga, sgb, wpa, wpb, wo, g1.arr, sh2.arr, sc2.arr, n2]
    if router is not None:
        tri = jnp.asarray(np.tril(np.ones((tm, tm), np.float32), -1), BF16)
        in_specs += [const(router[0].shape), const(router[1].shape), const((tm, tm))]
        out_specs += [row(ROUTER_PAD), pl.BlockSpec((8, tm), lambda i: (0, i)),
                      pl.BlockSpec((None, 8, ROUTER_PAD), lambda i: (i, 0, 0))]
        out_shape += [jax.ShapeDtypeStruct((n, ROUTER_PAD), F32), jax.ShapeDtypeStruct((8, n), F32),
                      jax.ShapeDtypeStruct((n // tm, 8, ROUTER_PAD), F32)]
        args += [router[0], router[1], tri]
    return pl.pallas_call(
        _merge_kernel,
        grid=(n // tm,),
        in_specs=in_specs,
        out_specs=out_specs,
        out_shape=out_shape,
        compiler_params=_params(("parallel",)),
        name="merge_out",
    )(*args)


def _swiglu(hb, wg_ref, wu_ref, wd_ref):
    act = (_silu(_dot(hb, wg_ref[...])) * _dot(hb, wu_ref[...])).astype(BF16)
    return _dot(act, wd_ref[...])


def _ffn_kernel(h_ref, x_ref, g2_ref, wg_ref, wu_ref, wd_ref, o_ref, acc_ref):
    f = pl.program_id(1)
    y = _swiglu(h_ref[...], wg_ref, wu_ref, wd_ref)

    @pl.when(f == 0)
    def _():
        acc_ref[...] = y

    @pl.when(f > 0)
    def _():
        acc_ref[...] += y

    @pl.when(f == pl.num_programs(1) - 1)
    def _():
        o_ref[...] = x_ref[...] + g2_ref[...] * acc_ref[...]


def _ffn_cast_kernel(h_ref, x_ref, g2_ref, wg_ref, wu_ref, wd_ref, o_ref, wg16_ref, wu16_ref, wd16_ref, acc_ref):
    wg16_ref[...] = wg_ref[...].astype(BF16)
    wu16_ref[...] = wu_ref[...].astype(BF16)
    wd16_ref[...] = wd_ref[...].astype(BF16)
    _ffn_kernel(h_ref, x_ref, g2_ref, wg16_ref, wu16_ref, wd16_ref, o_ref, acc_ref)


def _ffn(h2, x1, g2, wg, wu, wd, *, tm, tf, emit_bf16=False):
    n = h2.shape[0]
    assert not emit_bf16 or n == tm
    mod_spec = _mod_spec(tm, n, g2)
    up_spec = pl.BlockSpec((D_MODEL, tf), lambda i, f: (0, f))
    down_spec = pl.BlockSpec((tf, D_MODEL), lambda i, f: (f, 0))
    out_specs = [pl.BlockSpec((tm, D_MODEL), lambda i, f: (i, 0))]
    out_shape = [jax.ShapeDtypeStruct((n, D_MODEL), F32)]
    if emit_bf16:
        out_specs += [up_spec, up_spec, down_spec]
        out_shape += [jax.ShapeDtypeStruct(w.shape, BF16) for w in (wg, wu, wd)]
    out = pl.pallas_call(
        _ffn_cast_kernel if emit_bf16 else _ffn_kernel,
        grid=(n // tm, D_FF // tf),
        in_specs=[
            pl.BlockSpec((tm, D_MODEL), lambda i, f: (i, 0)),
            pl.BlockSpec((tm, D_MODEL), lambda i, f: (i, 0)),
            mod_spec, up_spec, up_spec, down_spec,
        ],
        out_specs=out_specs,
        out_shape=out_shape,
        scratch_shapes=[pltpu.VMEM((tm, D_MODEL), F32)],
        compiler_params=_params(("parallel", "arbitrary")),
        name="ffn_dense",
    )(h2, x1, g2.arr, wg, wu, wd)
    return out if emit_bf16 else out[0]


def _moe_kernel(h_ref, x_ref, g2_ref, rw_ref, rbias_ref, wg_ref, wu_ref, wd_ref,
                o_ref, wg16_ref, wu16_ref, wd16_ref, acc_ref, gate_ref):
    e = pl.program_id(0)
    f = pl.program_id(1)
    hb = h_ref[...]
    tm = hb.shape[0]
    lane = lax.broadcasted_iota(jnp.int32, (tm, ROUTER_PAD), 1).astype(F32)

    @pl.when((e == 0) & (f == 0))
    def _():
        logits = _dot(hb, rw_ref[...]) + rbias_ref[...]
        i1, i2, p1, p2 = _top2(logits, lane)
        gate_ref[...] = jnp.where(lane == i1, p1, 0.0) + jnp.where(lane == i2, p2, 0.0)
        acc_ref[...] = jnp.zeros_like(acc_ref)

    wg = wg_ref[...].astype(BF16)
    wu = wu_ref[...].astype(BF16)
    wd = wd_ref[...].astype(BF16)
    wg16_ref[...] = wg
    wu16_ref[...] = wu
    wd16_ref[...] = wd
    ge = jnp.sum(jnp.where(lane == e.astype(F32), gate_ref[...], 0.0), axis=-1, keepdims=True)
    act = (_silu(_dot(hb, wg)) * _dot(hb, wu)).astype(BF16)
    acc_ref[...] += ge * _dot(act, wd)

    @pl.when((e == pl.num_programs(0) - 1) & (f == pl.num_programs(1) - 1))
    def _():
        o_ref[...] = x_ref[...] + g2_ref[...] * acc_ref[...]


def _moe(h2, x1, g2, rw, rbias, wg, wu, wd):
    n = h2.shape[0]
    fe = D_FF_EXPERT
    tf = fe

    def const(shape):
        return pl.BlockSpec(shape, lambda e, f: (0,) * len(shape))

    up_spec = pl.BlockSpec((None, D_MODEL, tf), lambda e, f: (e, 0, f))
    down_spec = pl.BlockSpec((None, tf, D_MODEL), lambda e, f: (e, f, 0))
    mod_spec = pl.BlockSpec((None, None, n, D_MODEL), lambda e, f: (g2.layer, g2.idx, 0, 0))
    return pl.pallas_call(
        _moe_kernel,
        grid=(N_EXPERTS, fe // tf),
        in_specs=[const((n, D_MODEL)), const((n, D_MODEL)), mod_spec,
                  const((D_MODEL, ROUTER_PAD)), const((1, ROUTER_PAD)),
                  up_spec, up_spec, down_spec],
        out_specs=[const((n, D_MODEL)), up_spec, up_spec, down_spec],
        out_shape=[
            jax.ShapeDtypeStruct((n, D_MODEL), F32),
            jax.ShapeDtypeStruct(wg.shape, BF16),
            jax.ShapeDtypeStruct(wu.shape, BF16),
            jax.ShapeDtypeStruct(wd.shape, BF16),
        ],
        scratch_shapes=[pltpu.VMEM((n, D_MODEL), F32), pltpu.VMEM((n, ROUTER_PAD), F32)],
        compiler_params=_params(("arbitrary", "arbitrary")),
        name="moe",
    )(h2, x1, g2.arr, rw, rbias, wg, wu, wd)


def _top2(logits, lane):
    lg = jnp.where(lane < N_EXPERTS, logits, -jnp.inf)
    m1 = jnp.max(lg, axis=-1, keepdims=True)
    i1 = jnp.min(jnp.where(lg == m1, lane, float(ROUTER_PAD)), axis=-1, keepdims=True)
    lg2 = jnp.where(lane == i1, -jnp.inf, lg)
    m2 = jnp.max(lg2, axis=-1, keepdims=True)
    i2 = jnp.min(jnp.where(lg2 == m2, lane, float(ROUTER_PAD)), axis=-1, keepdims=True)
    e2 = jnp.exp(m2 - m1)
    p1 = 1.0 / (1.0 + e2)
    return i1, i2, p1, e2 * p1


def _route_pack(logits, tri):
    lane = lax.broadcasted_iota(jnp.int32, logits.shape, 1).astype(F32)
    i1, i2, p1, p2 = _top2(logits, lane)
    oh1 = jnp.where(lane == i1, 1.0, 0.0)
    oh2 = jnp.where(lane == i2, 1.0, 0.0)
    cnt1 = jnp.sum(oh1, axis=0, keepdims=True)
    cnt2 = jnp.sum(oh2, axis=0, keepdims=True)
    rank1 = jnp.sum(_dot(tri, oh1.astype(BF16)) * oh1, axis=-1, keepdims=True)
    rank2 = jnp.sum((_dot(tri, oh2.astype(BF16)) + cnt1) * oh2, axis=-1, keepdims=True)
    packed = jnp.zeros_like(logits)
    for k, val in enumerate((p1, p2, i1, i2, rank1, rank2)):
        packed = jnp.where(lane == float(k), val, packed)
    return packed, cnt1 + cnt2


def _route_tables(route_t, tile_counts, tm, n_tiles):
    n = route_t.shape[1]
    tile_cnt = tile_counts[:, 0, :N_EXPERTS].astype(jnp.int32)
    cnt = jnp.sum(tile_cnt, axis=0)
    gsz = ((cnt + tm - 1) // tm) * tm
    gend = jnp.cumsum(gsz)
    seg_start = (gend - gsz)[None, :] + jnp.cumsum(tile_cnt, axis=0) - tile_cnt
    seg_of_token = jnp.repeat(seg_start.T, n // tile_cnt.shape[0], axis=1)
    experts = jnp.arange(N_EXPERTS, dtype=jnp.int32)[:, None]
    pos = []
    for k in range(2):
        e_k = route_t[2 + k].astype(jnp.int32)
        rank_k = route_t[4 + k].astype(jnp.int32)
        pos.append(jnp.sum(jnp.where(e_k[None, :] == experts, seg_of_token, 0), axis=0) + rank_k)
    pos = jnp.concatenate(pos)
    tile_start = jnp.arange(n_tiles, dtype=jnp.int32) * tm
    tile_expert = jnp.sum((tile_start[:, None] >= gend[None, :]).astype(jnp.int32), axis=1)
    tile_expert = jnp.minimum(tile_expert, N_EXPERTS - 1)
    live = (gend[-1] // tm).reshape(1)
    return pos, tile_expert, live, gend - gsz + cnt, gend


def _row_copy_wait(src_hbm, dst, sem, rows):
    pltpu.make_async_copy(src_hbm.at[pl.ds(0, rows)], dst, sem).wait()


def _moe_dispatch_kernel(pos_ref, pad_lo_ref, pad_hi_ref, live_ref, h_ref, xs_hbm, rows, zbuf, rsem, zsem):
    i = pl.program_id(0)
    steps = pl.num_programs(0)
    tt = h_ref.shape[0]
    n = tt * steps
    tm = zbuf.shape[0]
    slot = i % 2

    def wait_rows(sl):
        for _ in range(2):
            pltpu.make_async_copy(rows.at[sl], xs_hbm.at[pl.ds(0, tt)], rsem.at[sl]).wait()

    @pl.when(i >= 2)
    def _():
        wait_rows(slot)

    rows[slot] = h_ref[...].astype(F32)
    base = i * tt
    for r in range(tt):
        row = rows.at[slot, pl.ds(r, 1)]
        pltpu.make_async_copy(row, xs_hbm.at[pl.ds(pos_ref[base + r], 1)], rsem.at[slot]).start(priority=0)
        pltpu.make_async_copy(row, xs_hbm.at[pl.ds(pos_ref[n + base + r], 1)], rsem.at[slot]).start(priority=1)

    @pl.when(i == steps - 1)
    def _():
        if steps > 1:
            wait_rows(1 - slot)
        wait_rows(slot)
        zbuf[...] = jnp.zeros(zbuf.shape, F32)

        def fill_row(p):
            return pltpu.make_async_copy(zbuf.at[pl.ds(0, 1)], xs_hbm.at[pl.ds(p, 1)], zsem)

        def fill_tile(t):
            return pltpu.make_async_copy(zbuf, xs_hbm.at[pl.ds(pl.multiple_of(t * tm, tm), tm)], zsem)

        for e in range(N_EXPERTS):
            lo, hi = pad_lo_ref[e], pad_hi_ref[e]
            lax.fori_loop(lo, hi, lambda p, c: (fill_row(p).start(), c)[1], 0)
            lax.fori_loop(lo, hi, lambda p, c: (fill_row(p).wait(), c)[1], 0)
        lo, hi = live_ref[0], xs_hbm.shape[0] // tm
        lax.fori_loop(lo, hi, lambda t, c: (fill_tile(t).start(), c)[1], 0)
        lax.fori_loop(lo, hi, lambda t, c: (fill_tile(t).wait(), c)[1], 0)


def _moe_dispatch(h2, pos, pad_lo, pad_hi, live, *, tt, tm, n_tiles):
    n = h2.shape[0]
    grid_spec = pltpu.PrefetchScalarGridSpec(
        num_scalar_prefetch=4,
        grid=(n // tt,),
        in_specs=[pl.BlockSpec((tt, D_MODEL), lambda i, *_: (i, 0))],
        out_specs=pl.BlockSpec(memory_space=pl.ANY),
        scratch_shapes=[pltpu.VMEM((2, tt, D_MODEL), F32), pltpu.VMEM((tm, D_MODEL), F32),
                        pltpu.SemaphoreType.DMA((2,)), pltpu.SemaphoreType.DMA(())],
    )
    return pl.pallas_call(
        _moe_dispatch_kernel,
        grid_spec=grid_spec,
        out_shape=jax.ShapeDtypeStruct((n_tiles * tm, D_MODEL), F32),
        compiler_params=_params(("arbitrary",)),
        name="moe_dispatch",
    )(pos, pad_lo, pad_hi, live, h2)


def _moe_expert_kernel(te_ref, live_ref, x_ref, wg_ref, wu_ref, wd_ref, y_ref):
    del te_ref
    is_live = pl.program_id(0) < live_ref[0]

    @pl.when(is_live)
    def _():
        y_ref[...] = _swiglu(x_ref[...].astype(BF16), wg_ref, wu_ref, wd_ref)

    @pl.when(jnp.logical_not(is_live))
    def _():
        y_ref[...] = jnp.zeros(y_ref.shape, F32)


def _moe_experts(xs, tile_expert, live, wg, wu, wd, *, tm, n_tiles):
    fe = D_FF_EXPERT

    def in_tile(t, te, live):
        return (jnp.minimum(t, live[0] - 1), 0)

    def out_tile(t, te, live):
        return (t, 0)

    def expert(t, te, live):
        return (te[t], 0, 0)

    grid_spec = pltpu.PrefetchScalarGridSpec(
        num_scalar_prefetch=2,
        grid=(n_tiles,),
        in_specs=[
            pl.BlockSpec((tm, D_MODEL), in_tile),
            pl.BlockSpec((None, D_MODEL, fe), expert),
            pl.BlockSpec((None, D_MODEL, fe), expert),
            pl.BlockSpec((None, fe, D_MODEL), expert),
        ],
        out_specs=pl.BlockSpec((tm, D_MODEL), out_tile),
    )
    return pl.pallas_call(
        _moe_expert_kernel,
        grid_spec=grid_spec,
        out_shape=jax.ShapeDtypeStruct(xs.shape, F32),
        compiler_params=_params(("arbitrary",)),
        name="moe_experts",
    )(tile_expert, live, xs, wg, wu, wd)


def _moe_combine_kernel(pos_ref, x_ref, g2_ref, r_ref, ys_hbm, o_ref, buf, sem):
    i = pl.program_id(0)
    nt = pl.num_programs(0)
    tt = x_ref.shape[0]
    n = nt * tt

    def fetch(tile, sl):
        base = tile * tt
        for r in range(tt):
            for s in range(2):
                row = pos_ref[s * n + base + r]
                pltpu.make_async_copy(ys_hbm.at[pl.ds(row, 1)], buf.at[sl, pl.ds(s * tt + r, 1)],
                                      sem.at[sl]).start(priority=s)

    @pl.when(i == 0)
    def _():
        fetch(0, 0)

    @pl.when(i + 1 < nt)
    def _():
        fetch(i + 1, (i + 1) % 2)

    sl = i % 2
    _row_copy_wait(ys_hbm, buf.at[sl], sem.at[sl], 2 * tt)
    r = r_ref[...]
    f = r[:, 0:1] * buf[sl, pl.ds(0, tt), :] + r[:, 1:2] * buf[sl, pl.ds(tt, tt), :]
    o_ref[...] = x_ref[...] + g2_ref[...] * f


def _moe_combine(x1, g2, route, pos, ys, *, tt):
    n = x1.shape[0]
    grid_spec = pltpu.PrefetchScalarGridSpec(
        num_scalar_prefetch=1,
        grid=(n // tt,),
        in_specs=[
            pl.BlockSpec((tt, D_MODEL), lambda i, pos: (i, 0)),
            _mod_spec(tt, n, g2),
            pl.BlockSpec((tt, ROUTER_PAD), lambda i, pos: (i, 0)),
            pl.BlockSpec(memory_space=pl.ANY),
        ],
        out_specs=pl.BlockSpec((tt, D_MODEL), lambda i, pos: (i, 0)),
        scratch_shapes=[pltpu.VMEM((2, 2 * tt, D_MODEL), F32), pltpu.SemaphoreType.DMA((2,))],
    )
    return pl.pallas_call(
        _moe_combine_kernel,
        grid_spec=grid_spec,
        out_shape=jax.ShapeDtypeStruct((n, D_MODEL), F32),
        compiler_params=_params(("arbitrary",)),
        name="moe_combine",
    )(pos, x1, g2.arr, route, ys)


def _moe_routed(h2, x1, g2, route, route_t, tile_counts, wg, wu, wd, *, tm, tt):
    n = h2.shape[0]
    n_tiles = (2 * n) // tm + N_EXPERTS
    pos, tile_expert, live, pad_lo, pad_hi = _route_tables(route_t, tile_counts, tm, n_tiles)
    xs = _moe_dispatch(h2, pos, pad_lo, pad_hi, live, tt=tt, tm=tm, n_tiles=n_tiles)
    ys = _moe_experts(xs, tile_expert, live, wg, wu, wd, tm=tm, n_tiles=n_tiles)
    return _moe_combine(x1, g2, route, pos, ys, tt=tt)


def _head_perm():
    idx = []
    for g in range(GROUP):
        for kv in range(N_KV):
            h = kv * GROUP + g
            idx.extend(range(h * HEAD_DIM, (h + 1) * HEAD_DIM))
    return np.asarray(idx, np.int32)


def kernel(x_prompt, x_sample, cache_k, cache_v, state_gla, c_prompt, c_sample, ada_w, ada_b, norm1_g, norm2_g, w_in, q_norm_g, k_norm_g, attn_sinks, gla_wa2, gla_ba, gla_norm_g, w_branch_a, w_branch_b, w_out, ffn_w_gate, ffn_w_up, ffn_w_down, router_w, router_b, moe_w_gate, moe_w_up, moe_w_down):
    n_p = BATCH * SEQ
    xp = x_prompt.reshape(n_p, D_MODEL)
    xs = x_sample.reshape(DEC_BATCH, D_MODEL)

    c_pad = -(BATCH + DEC_BATCH) % 8
    c_all = jnp.pad(jnp.concatenate([c_sample, c_prompt], axis=0), ((0, c_pad), (0, 0)))
    mod_sample, mod_prompt = _ada(c_all, ada_w, ada_b, DEC_BATCH, BATCH)
    state_rows = state_gla.reshape(DEPTH, DEC_BATCH, GK_W, GLA_DV)

    bd = jnp.asarray(np.kron(np.eye(N_HEADS), np.ones((HEAD_DIM, HEAD_DIM))), BF16)
    ut = jnp.asarray(_gla_cum_matrix(), BF16)
    gla_upper, gla_pairs = (jnp.asarray(m) for m in _gla_level_masks())
    perm = _head_perm()
    swa_bias = jnp.asarray(_swa_sample_bias())
    swa_prompt_bias = jnp.asarray(_swa_prompt_bias())

    kp_l, vp_l, sp_l, ks_l, vs_l, ss_l = [], [], [], [], [], []
    for l in range(DEPTH):
        mod_p = [_Mod(mod_prompt, l, i) for i in range(N_MOD)]
        mod_s = [_Mod(mod_sample, l, i) for i in range(N_MOD)]

        qg =(jnp.tile(q_norm_g[l], N_HEADS) * (HEAD_DIM ** -0.5)).reshape(1, Q_W)
        kg = jnp.tile(k_norm_g[l], N_KV).reshape(1, KV_W)
        wa2 = jnp.pad(gla_wa2[l], ((0, GATE_PAD - GATE_RANK), (0, 0))).astype(BF16)
        ba = gla_ba[l].reshape(1, GK_W)
        n1 = norm1_g[l].reshape(1, D_MODEL)
        n2 = norm2_g[l].reshape(1, D_MODEL)
        gn = gla_norm_g[l].reshape(1, GLA_DV)
        wpa = w_branch_a[l][perm].astype(BF16)
        wpb = w_branch_b[l].astype(BF16)
        wo = w_out[l].astype(BF16)
        sink_rows = jnp.broadcast_to(attn_sinks[l][_sample_head_of_row()][:, None], (N_HEADS, LANES))

        qs, ksn, vsn, gqs, gks, gvs, las, rbss, sgas, sgbs, w = _mixin(
            xs, mod_s[0], mod_s[1], n1, w_in[l], bd, qg, kg, wa2, ba, tm=DEC_BATCH, emit_w=True)

        q, k, v, gq, gk, gv, la, rbs, sga, sgb = _mixin(
            xp, mod_p[0], mod_p[1], n1, w, bd, qg, kg, wa2, ba, tm=ROW_TILE_RESIDENT)
        a_out = _swa_prompt(attn_sinks[l], q, k, v, swa_prompt_bias)
        g_out, s_fin = _gla_prompt(gq, gk, la, gv, rbs, gn, ut, gla_upper, gla_pairs)
        is_moe = l % 2 == 1
        if is_moe:
            rw = jnp.pad(router_w[l // 2], ((0, 0), (0, ROUTER_PAD - N_EXPERTS))).astype(BF16)
            rbias = jnp.pad(router_b[l // 2], (0, ROUTER_PAD - N_EXPERTS)).reshape(1, ROUTER_PAD)
        merged = _merge(xp, a_out, g_out, sga, sgb, wpa, wpb, wo, mod_p[2], mod_p[3], mod_p[4], n2,
                        tm=ROW_TILE_RESIDENT, router=(rw, rbias) if is_moe else None)
        x1, h2 = merged[0], merged[1]
        for cache, full in ((kp_l, k), (vp_l, v)):
            tail = full.reshape(BATCH, SEQ, KV_W)[:, SEQ - WINDOW:]
            cache.append(tail.reshape(BATCH, WINDOW, N_KV, HEAD_DIM))
        sp_l.append(s_fin.reshape(BATCH, GLA_HEADS, GLA_DK, GLA_DV))

        a_s, nk, nv = _swa_sample(
            qs.astype(F32).reshape(DEC_BATCH, N_HEADS, HEAD_DIM),
            ksn.reshape(DEC_BATCH, N_KV, HEAD_DIM), vsn.reshape(DEC_BATCH, N_KV, HEAD_DIM),
            cache_k[l].reshape(1, DEC_BATCH, CACHE_ROWS, HEAD_DIM),
            cache_v[l].reshape(1, DEC_BATCH, CACHE_ROWS, HEAD_DIM), swa_bias, sink_rows, layer=0)
        g_s, s_new = _gla_sample(gqs, gks, las, gvs, rbss, gn, state_rows, layer=l)
        x1s, h2s = _merge(xs, a_s.reshape(DEC_BATCH, Q_W), g_s, sgas, sgbs, wpa, wpb, wo,
                          mod_s[2], mod_s[3], mod_s[4], n2, tm=DEC_BATCH)
        ks_l.append(nk.reshape(DEC_BATCH, WINDOW, N_KV, HEAD_DIM))
        vs_l.append(nv.reshape(DEC_BATCH, WINDOW, N_KV, HEAD_DIM))
        ss_l.append(s_new.reshape(DEC_BATCH, GLA_HEADS, GLA_DK, GLA_DV))

        i = l // 2
        if not is_moe:
            xs, wg, wu, wd = _ffn(h2s, x1s, mod_s[5], ffn_w_gate[i], ffn_w_up[i], ffn_w_down[i],
                                  tm=DEC_BATCH, tf=D_FF // 2, emit_bf16=True)
            xp = _ffn(h2, x1, mod_p[5], wg, wu, wd, tm=ROW_TILE_SWIGLU, tf=D_FF // 2)
        else:
            xs, wg, wu, wd = _moe(h2s, x1s, mod_s[5], rw, rbias, moe_w_gate[i], moe_w_up[i], moe_w_down[i])
            xp = _moe_routed(h2, x1, mod_p[5], *merged[2:5], wg, wu, wd,
                             tm=ROW_TILE_SWIGLU, tt=MOE_TOKEN_TILE)

    return (xp.reshape(BATCH, SEQ, D_MODEL), xs.reshape(DEC_BATCH, 1, D_MODEL),
            jnp.stack(kp_l), jnp.stack(vp_l), jnp.stack(sp_l),
            jnp.stack(ks_l), jnp.stack(vs_l), jnp.stack(ss_l))
```

```python
from typing import NamedTuple

import jax
import jax.numpy as jnp
import numpy as np
from jax import lax
from jax.experimental import pallas as pl
from jax.experimental.pallas import tpu as pltpu

D_MODEL = 1024
BATCH = 4
SEQ = 4096
DEPTH = 2
DEC_BATCH = 128
N_HEADS = 8
N_KV = 2
HEAD_DIM = 64
GROUP = N_HEADS // N_KV
WINDOW = 128
GLA_HEADS = 4
GLA_DK = 64
GLA_DV = 128
GATE_RANK = 16
GATE_TAU = 16.0
D_FF = 2816
N_EXPERTS = 8
D_FF_EXPERT = 1408
EPS = 1e-6

Q_W = N_HEADS * HEAD_DIM
KV_W = N_KV * HEAD_DIM
GK_W = GLA_HEADS * GLA_DK
GV_W = GLA_HEADS * GLA_DV

LANES = 128
GATE_PAD = LANES
ROUTER_PAD = LANES
VMEM_LIMIT = 56 * 1024 * 1024

ROW_TILE_RESIDENT = 1024
ROW_TILE_SWIGLU = 512
MOE_TOKEN_TILE = 512

F32 = jnp.float32
BF16 = jnp.bfloat16

_C_Q = 0
_C_K = _C_Q + Q_W
_C_V = _C_K + KV_W
_C_GQ = _C_V + KV_W
_C_GK = _C_GQ + GK_W
_C_GV = _C_GK + GK_W
_C_RB = _C_GV + GV_W
_C_GA = _C_RB + GV_W
_C_GB = _C_GA + D_MODEL
_C_LR = _C_GB + D_MODEL
PROJ_PAD = _C_LR + GATE_PAD

GLA_CHUNK = 128
GLA_LEVELS = 7
GLA_MXU_LEVELS = 3
GLA_CHUNKS_PER_STEP = 8


def _params(sem, vmem=VMEM_LIMIT):
    return pltpu.CompilerParams(dimension_semantics=sem, vmem_limit_bytes=vmem)


def _dot(a, b):
    return jnp.dot(a, b, preferred_element_type=F32)


def _dot_nt(a, b):
    return lax.dot_general(a, b, (((1,), (1,)), ((), ())), preferred_element_type=F32)


def _dot_tn(a, b):
    return lax.dot_general(a, b, (((0,), (0,)), ((), ())), preferred_element_type=F32)


def _sigmoid(x):
    return 0.5 * jnp.tanh(0.5 * x) + 0.5


def _silu(x):
    return x * _sigmoid(x)


N_MOD = 6


class _Mod(NamedTuple):
    arr: jax.Array
    layer: int
    idx: int


def _ada_kernel(c_ref, w_ref, b_ref, os_ref, op_ref):
    c = c_ref[...]
    mod = _dot(_silu(c).astype(BF16), w_ref[...].astype(BF16)) + b_ref[...]
    n_s = os_ref.shape[0]
    os_ref[...] = mod[0:n_s]
    for b in range(op_ref.shape[0]):
        op_ref[b] = mod[n_s + b:n_s + b + 1]


def _ada(c_all, ada_w, ada_b, n_sample, n_prompt):
    rows = c_all.shape[0]
    return pl.pallas_call(
        _ada_kernel,
        grid=(DEPTH, N_MOD),
        in_specs=[
            pl.BlockSpec((rows, D_MODEL), lambda l, j: (0, 0)),
            pl.BlockSpec((None, D_MODEL, D_MODEL), lambda l, j: (l, 0, j)),
            pl.BlockSpec((None, 1, D_MODEL), lambda l, j: (l, 0, j)),
        ],
        out_specs=[
            pl.BlockSpec((None, None, n_sample, D_MODEL), lambda l, j: (l, j, 0, 0)),
            pl.BlockSpec((None, None, n_prompt, 1, D_MODEL), lambda l, j: (l, j, 0, 0, 0)),
        ],
        out_shape=[
            jax.ShapeDtypeStruct((DEPTH, N_MOD, n_sample, D_MODEL), F32),
            jax.ShapeDtypeStruct((DEPTH, N_MOD, n_prompt, 1, D_MODEL), F32),
        ],
        compiler_params=_params(("parallel", "parallel")),
        name="ada_mod",
    )(c_all, ada_w, ada_b.reshape(DEPTH, 1, N_MOD * D_MODEL))


def _mixin_kernel(x_ref, sh_ref, sc_ref, n1_ref, w_ref, bd_ref, qg_ref, kg_ref, wa2_ref, ba_ref,
                  q_ref, k_ref, v_ref, gq_ref, gk_ref, gv_ref, la_ref, rb_ref, sga_ref, sgb_ref):
    x = x_ref[...]
    ms = jnp.mean(x * x, axis=-1, keepdims=True)
    h = x * lax.rsqrt(ms + EPS) * n1_ref[...]
    h = h * (1.0 + sc_ref[...]) + sh_ref[...]
    hb = h.astype(BF16)

    def proj(a, b):
        return _dot(hb, w_ref[:, a:b])

    q = proj(_C_Q, _C_K)
    ssq = _dot((q * q).astype(BF16), bd_ref[...])
    q_ref[...] = (q * lax.rsqrt(ssq * (1.0 / HEAD_DIM) + EPS) * qg_ref[...]).astype(BF16)
    k = proj(_C_K, _C_V)
    ssk = _dot((k * k).astype(BF16), bd_ref[0:KV_W, 0:KV_W])
    k_ref[...] = k * lax.rsqrt(ssk * (1.0 / HEAD_DIM) + EPS) * kg_ref[...]
    v_ref[...] = proj(_C_V, _C_GQ)
    gq_ref[...] = proj(_C_GQ, _C_GK) * (GLA_DK ** -0.5)
    gk_ref[...] = proj(_C_GK, _C_GV)
    gv_ref[...] = proj(_C_GV, _C_RB)
    rb_ref[...] = _silu(proj(_C_RB, _C_GA)).astype(BF16)
    sga_ref[...] = _sigmoid(proj(_C_GA, _C_GB)).astype(BF16)
    sgb_ref[...] = _sigmoid(proj(_C_GB, _C_LR)).astype(BF16)
    ga = proj(_C_LR, PROJ_PAD)
    xg = _dot(ga.astype(BF16), wa2_ref[...]) + ba_ref[...]
    la_ref[...] = (jnp.minimum(xg, 0.0) - jnp.log1p(jnp.exp(-jnp.abs(xg)))) * (1.0 / GATE_TAU)


def _mixin_cast_kernel(x_ref, sh_ref, sc_ref, n1_ref, w_ref, *rest):
    wre_ref = rest[-1]
    for j in range(N_HEADS):
        h = (j % N_KV) * GROUP + j // N_KV
        wre_ref[:, j * HEAD_DIM:(j + 1) * HEAD_DIM] = w_ref[:, h * HEAD_DIM:(h + 1) * HEAD_DIM].astype(BF16)
    wre_ref[:, _C_K:_C_RB] = w_ref[:, _C_K:_C_RB].astype(BF16)
    tail = _C_LR - _C_RB
    wre_ref[:, _C_RB:_C_LR] = w_ref[:, _C_RB + GATE_RANK:_C_RB + GATE_RANK + tail].astype(BF16)
    wre_ref[:, _C_LR:_C_LR + GATE_RANK] = w_ref[:, _C_RB:_C_RB + GATE_RANK].astype(BF16)
    wre_ref[:, _C_LR + GATE_RANK:PROJ_PAD] = jnp.zeros((D_MODEL, GATE_PAD - GATE_RANK), BF16)
    _mixin_kernel(x_ref, sh_ref, sc_ref, n1_ref, wre_ref, *rest[:-1])


def _mixin(x, sh, sc, n1, w, bd, qg, kg, wa2, ba, *, tm, emit_w=False):
    n = x.shape[0]
    nt = n // tm
    assert not emit_w or nt == 1

    def row(i):
        return (i, 0)

    def const(shape):
        return pl.BlockSpec(shape, lambda i: (0,) * len(shape), pipeline_mode=pl.Buffered(1))

    def out(width, dtype):
        return pl.BlockSpec((tm, width), row), jax.ShapeDtypeStruct((n, width), dtype)

    outs = [out(Q_W, BF16), out(KV_W, F32), out(KV_W, F32), out(GK_W, F32), out(GK_W, F32),
            out(GV_W, F32), out(GK_W, F32), out(GV_W, BF16), out(D_MODEL, BF16), out(D_MODEL, BF16)]
    if emit_w:
        outs.append((pl.BlockSpec((D_MODEL, PROJ_PAD), lambda i: (0, 0)),
                     jax.ShapeDtypeStruct((D_MODEL, PROJ_PAD), BF16)))
    return pl.pallas_call(
        _mixin_cast_kernel if emit_w else _mixin_kernel,
        grid=(nt,),
        in_specs=[
            pl.BlockSpec((tm, D_MODEL), row), _mod_spec(tm, n, sh), _mod_spec(tm, n, sc), const((1, D_MODEL)),
            const(w.shape), const((Q_W, Q_W)), const((1, Q_W)), const((1, KV_W)),
            const((GATE_PAD, GK_W)), const((1, GK_W)),
        ],
        out_specs=[o[0] for o in outs],
        out_shape=[o[1] for o in outs],
        compiler_params=_params(("parallel",)),
        name="mixer_in",
    )(x, sh.arr, sc.arr, n1, w, bd, qg, kg, wa2, ba)


SWA_BLOCKS = 8


def _head_slope(h):
    return float(2.0 ** (-8.0 * (h + 1) / N_HEADS))


def _swa_prompt_bias():
    blk = WINDOW
    dist = np.arange(blk)[:, None] + blk - np.arange(2 * blk)[None, :]
    slopes = np.asarray([_head_slope(h) for h in range(N_HEADS)])[:, None, None]
    return np.where((dist >= 0) & (dist <= WINDOW), -slopes * dist, -np.inf).astype(np.float32)


def _swa_prompt_kernel(sink_ref, q_ref, kp_ref, kc_ref, vp_ref, vc_ref, bias_ref, o_ref):
    n = pl.program_id(1)
    blk = WINDOW
    kall = jnp.concatenate([kp_ref[...], kc_ref[...]], axis=0).astype(BF16)
    vall = jnp.concatenate([vp_ref[...], vc_ref[...]], axis=0).astype(BF16)
    col = lax.broadcasted_iota(jnp.int32, (blk, 2 * blk), 1)
    first_key = jnp.where(n > 0, 0, blk)
    klane = lax.broadcasted_iota(jnp.int32, kall.shape, 1)
    kall_kv = [jnp.where((klane // HEAD_DIM) == kv, kall, jnp.zeros_like(kall)) for kv in range(N_KV)]
    olane = lax.broadcasted_iota(jnp.int32, (blk, KV_W), 1)
    q = q_ref[...]
    for j in range(SWA_BLOCKS):
        vv = vall[j * blk:(j + 2) * blk]
        outs = []
        for g in range(GROUP):
            qp = q[j * blk:(j + 1) * blk, g * KV_W:(g + 1) * KV_W]
            pair = []
            for kv in range(N_KV):
                h = kv * GROUP + g
                s = _dot_nt(qp, kall_kv[kv][j * blk:(j + 2) * blk]) + bias_ref[h]
                if j == 0:
                    s = jnp.where(col >= first_key, s, -jnp.inf)
                sink = sink_ref[h]
                m = jnp.maximum(jnp.max(s, axis=-1, keepdims=True), sink)
                p = jnp.exp(s - m)
                den = jnp.sum(p, axis=-1, keepdims=True) + jnp.exp(sink - m)
                pair.append(_dot(p.astype(BF16), vv) * (1.0 / den))
            outs.append(jnp.where(olane < HEAD_DIM, pair[0], pair[1]))
        o_ref[j * blk:(j + 1) * blk, :] = jnp.concatenate(outs, axis=-1).astype(BF16)


def _swa_prompt(sinks, q, k, v, bias):
    nb = SEQ // WINDOW
    steps = nb // SWA_BLOCKS
    tq = SWA_BLOCKS * WINDOW

    def cur(b, n):
        return (b * steps + n, 0)

    def prev(b, n):
        return (b * nb + jnp.maximum(n * SWA_BLOCKS - 1, 0), 0)

    return pl.pallas_call(
        _swa_prompt_kernel,
        grid=(BATCH, steps),
        in_specs=[
            pl.BlockSpec(memory_space=pltpu.SMEM),
            pl.BlockSpec((tq, Q_W), cur),
            pl.BlockSpec((WINDOW, KV_W), prev), pl.BlockSpec((tq, KV_W), cur),
            pl.BlockSpec((WINDOW, KV_W), prev), pl.BlockSpec((tq, KV_W), cur),
            pl.BlockSpec(bias.shape, lambda b, n: (0, 0, 0), pipeline_mode=pl.Buffered(1)),
        ],
        out_specs=pl.BlockSpec((tq, Q_W), cur),
        out_shape=jax.ShapeDtypeStruct((BATCH * SEQ, Q_W), BF16),
        compiler_params=_params(("parallel", "parallel")),
        name="swa_prompt",
    )(sinks, q, k, k, v, v, bias)


SAMPLE_TB = 8
CACHE_ROWS = WINDOW * N_KV


def _sample_head_of_row():
    j = np.arange(N_HEADS)
    return (j % N_KV) * GROUP + j // N_KV


def _swa_sample_bias():
    j = np.arange(N_HEADS)[:, None]
    c = np.arange(CACHE_ROWS)[None, :]
    slope = 2.0 ** (-8.0 * (_sample_head_of_row()[:, None] + 1) / N_HEADS)
    bias = -slope * (WINDOW - c // N_KV)
    return np.where(c % N_KV == j % N_KV, bias, -np.inf).astype(np.float32)


def _swa_sample_kernel(q_ref, kn_ref, vn_ref, ck_ref, cv_ref, bias_ref, sk_ref, o_ref, ok_ref, ov_ref):
    rows = CACHE_ROWS
    sink = sk_ref[...][:, 0:1]
    q = q_ref[...]
    kn = kn_ref[...]
    vn = vn_ref[...]
    kc = ck_ref[...]
    vc = cv_ref[...]
    kn8 = jnp.concatenate([kn] * GROUP, axis=1)
    vn8 = jnp.concatenate([vn] * GROUP, axis=1)
    s = lax.dot_general(q.astype(BF16), kc.astype(BF16), (((2,), (2,)), ((0,), (0,))),
                        preferred_element_type=F32) + bias_ref[...]
    s_new = jnp.sum(q * kn8, axis=-1, keepdims=True)
    m = jnp.maximum(jnp.maximum(jnp.max(s, axis=-1, keepdims=True), s_new), sink)
    p = jnp.exp(s - m)
    p_new = jnp.exp(s_new - m)
    den = jnp.sum(p, axis=-1, keepdims=True) + p_new + jnp.exp(sink - m)
    o = lax.dot_general(p.astype(BF16), vc.astype(BF16), (((2,), (1,)), ((0,), (0,))),
                        preferred_element_type=F32) + p_new * vn8
    o_ref[...] = o * (1.0 / den)
    ok_ref[:, pl.ds(0, rows - N_KV), :] = kc[:, N_KV:, :]
    ok_ref[:, pl.ds(rows - N_KV, N_KV), :] = kn
    ov_ref[:, pl.ds(0, rows - N_KV), :] = vc[:, N_KV:, :]
    ov_ref[:, pl.ds(rows - N_KV, N_KV), :] = vn


def _swa_sample(q, kn, vn, ck, cv, bias, sinks, *, layer):
    tb = 2 * SAMPLE_TB
    nb = DEC_BATCH // tb

    def const(shape):
        return pl.BlockSpec(shape, lambda i: (0,) * len(shape))

    def per_seq(*dims):
        return pl.BlockSpec((tb,) + dims, lambda i: (i, 0, 0))

    cache_spec = pl.BlockSpec((None, tb, CACHE_ROWS, HEAD_DIM), lambda i: (layer, i, 0, 0))
    return pl.pallas_call(
        _swa_sample_kernel,
        grid=(nb,),
        in_specs=[per_seq(N_HEADS, HEAD_DIM), per_seq(N_KV, HEAD_DIM), per_seq(N_KV, HEAD_DIM),
                  cache_spec, cache_spec,
                  const((N_HEADS, CACHE_ROWS)), const((N_HEADS, LANES))],
        out_specs=[per_seq(N_HEADS, HEAD_DIM), per_seq(CACHE_ROWS, HEAD_DIM), per_seq(CACHE_ROWS, HEAD_DIM)],
        out_shape=[
            jax.ShapeDtypeStruct((DEC_BATCH, N_HEADS, HEAD_DIM), F32),
            jax.ShapeDtypeStruct((DEC_BATCH, CACHE_ROWS, HEAD_DIM), F32),
            jax.ShapeDtypeStruct((DEC_BATCH, CACHE_ROWS, HEAD_DIM), F32),
        ],
        compiler_params=_params(("parallel",)),
        name="swa_sample",
    )(q, kn, vn, ck, cv, bias, sinks)


def _gla_cum_matrix():
    c = GLA_CHUNK
    tri = np.tril(np.ones((c, c), np.float32))
    i = np.arange(c)
    blocks = []
    for lvl in range(GLA_MXU_LEVELS):
        half = 1 << lvl
        mid = (i // (2 * half)) * (2 * half) + half - 1
        blocks.append(tri - tri[mid])
    blocks.append(tri)
    return np.concatenate(blocks, axis=0)


def _split3(x):
    hi = x.astype(BF16)
    r1 = x - hi.astype(F32)
    mid = r1.astype(BF16)
    lo = (r1 - mid.astype(F32)).astype(BF16)
    return hi, mid, lo


def _gla_level_masks():
    cl = GLA_CHUNK
    r = np.arange(cl)
    upper = np.stack([np.broadcast_to(((r >> lvl) & 1)[:, None], (cl, GK_W)) for lvl in range(GLA_LEVELS)])
    ri = np.tile(r, GLA_HEADS)[:, None]
    pairs = [(ri >> (lvl + 1)) == (r[None, :] >> (lvl + 1)) for lvl in range(GLA_LEVELS)]
    pairs.append(ri == r[None, :])
    return upper.astype(np.float32), np.stack(pairs).astype(np.float32)


def _gla_prompt_kernel(q_ref, k_ref, la_ref, v_ref, rb_ref, gn_ref, ut_ref, up_ref, pm_ref,
                       o_ref, s_ref, st_ref):
    c = pl.program_id(1)
    cl = GLA_CHUNK

    @pl.when(c == 0)
    def _():
        st_ref[0] = jnp.zeros(st_ref.shape[1:], F32)

    ut = ut_ref[...]
    lane = lax.broadcasted_iota(jnp.int32, (cl, GK_W), 1)
    head_of_lane = lane // GLA_DK
    ones = jnp.ones((cl, LANES), BF16)
    state = [st_ref[c % 2, h] for h in range(GLA_HEADS)]
    for cc in range(GLA_CHUNKS_PER_STEP):
        rows = slice(cc * cl, (cc + 1) * cl)
        hi, mid, lo = _split3(la_ref[rows, :])
        tall = _dot(ut, hi) + _dot(ut, mid) + _dot(ut, lo)
        q = q_ref[rows, :]
        k = k_ref[rows, :]
        attn_all = jnp.zeros((GLA_HEADS * cl, cl), F32)
        cum = tall[GLA_MXU_LEVELS * cl:(GLA_MXU_LEVELS + 1) * cl]
        for lvl in range(GLA_LEVELS + 1):
            if lvl < GLA_LEVELS:
                if lvl < GLA_MXU_LEVELS:
                    t_lvl = tall[lvl * cl:(lvl + 1) * cl]
                else:
                    half = 1 << lvl
                    mids = [jnp.broadcast_to(cum[b0 + half - 1:b0 + half, :], (2 * half, GK_W))
                            for b0 in range(0, cl, 2 * half)]
                    t_lvl = cum - (mids[0] if len(mids) == 1 else jnp.concatenate(mids, axis=0))
                e = jnp.exp(-jnp.abs(t_lvl))
                e_up = e * up_ref[lvl]
                qt = (q * e_up).astype(BF16)
                kt = (k * (e - e_up)).astype(BF16)
            else:
                qt = q.astype(BF16)
                kt = k.astype(BF16)
            q_heads = jnp.concatenate(
                [jnp.where(head_of_lane == h, qt, jnp.zeros_like(qt)) for h in range(GLA_HEADS)], axis=0)
            attn_all = attn_all + _dot_nt(q_heads, kt) * pm_ref[lvl]

        last = cum[cl - 1:cl, :]
        qe = (q * jnp.exp(cum)).astype(BF16)
        kd = (k * jnp.exp(last - cum)).astype(BF16)
        dec = jnp.exp(_dot_tn(hi, ones) + _dot_tn(mid, ones) + _dot_tn(lo, ones))
        v = v_ref[rows, :]
        rb = rb_ref[rows, :]
        outs = []
        for h in range(GLA_HEADS):
            vh = v[:, h * GLA_DV:(h + 1) * GLA_DV].astype(BF16)
            o = _dot(attn_all[h * cl:(h + 1) * cl].astype(BF16), vh) + _dot(qe, state[h].astype(BF16))
            kdh = jnp.where(head_of_lane == h, kd, jnp.zeros_like(kd))
            state[h] = state[h] * dec + _dot_tn(kdh, vh)
            ms = jnp.mean(o * o, axis=-1, keepdims=True)
            g = o * lax.rsqrt(ms + EPS) * gn_ref[...]
            outs.append(g * rb[:, h * GLA_DV:(h + 1) * GLA_DV].astype(F32))
        o_ref[rows, :] = jnp.concatenate(outs, axis=-1).astype(BF16)

    for h in range(GLA_HEADS):
        st_ref[(c + 1) % 2, h] = state[h]

    @pl.when(c == pl.num_programs(1) - 1)
    def _():
        fin = (SEQ // (GLA_CHUNK * GLA_CHUNKS_PER_STEP)) % 2
        s_ref[...] = st_ref[fin, 0] + st_ref[fin, 1] + st_ref[fin, 2] + st_ref[fin, 3]


def _gla_prompt(gq, gk, la, gv, rbs, gn, ut, upper, pairs):
    cl = GLA_CHUNK * GLA_CHUNKS_PER_STEP
    nc = SEQ // cl

    def row(b, c):
        return (b * nc + c, 0)

    def const(shape):
        return pl.BlockSpec(shape, lambda b, c: (0,) * len(shape), pipeline_mode=pl.Buffered(1))

    return pl.pallas_call(
        _gla_prompt_kernel,
        grid=(BATCH, nc),
        in_specs=[
            pl.BlockSpec((cl, GK_W), row), pl.BlockSpec((cl, GK_W), row), pl.BlockSpec((cl, GK_W), row),
            pl.BlockSpec((cl, GV_W), row), pl.BlockSpec((cl, GV_W), row),
            const((1, GLA_DV)), const(ut.shape), const(upper.shape), const(pairs.shape),
        ],
        out_specs=[
            pl.BlockSpec((cl, GV_W), row),
            pl.BlockSpec((None, GK_W, GLA_DV), lambda b, c: (b, 0, 0)),
        ],
        out_shape=[
            jax.ShapeDtypeStruct((BATCH * SEQ, GV_W), BF16),
            jax.ShapeDtypeStruct((BATCH, GK_W, GLA_DV), F32),
        ],
        scratch_shapes=[pltpu.VMEM((2, GLA_HEADS, GK_W, GLA_DV), F32)],
        compiler_params=_params(("parallel", "arbitrary")),
        name="gla_prompt",
    )(gq, gk, la, gv, rbs, gn, ut, upper, pairs)


def _gla_sample_kernel(q_ref, k_ref, la_ref, v_ref, rb_ref, gn_ref, s_ref, o_ref, so_ref):
    tb = SAMPLE_TB
    dec = jnp.exp(la_ref[...])
    pieces = []
    for x in (dec, k_ref[...], q_ref[...]):
        hi, mid, lo = _split3(x)
        stacked = jnp.concatenate(
            [hi.astype(F32), mid.astype(F32), lo.astype(F32), jnp.zeros_like(x)], axis=0)
        pieces.append(stacked.astype(BF16))
    prow = lax.broadcasted_iota(jnp.int32, (4 * tb, LANES), 0)
    rb = rb_ref[...].astype(F32)
    v = v_ref[...]
    for bi in range(tb):
        sel = jnp.where((prow % tb) == bi, 1.0, 0.0).astype(BF16)
        a_col, k_col, q_col = [_dot_tn(p, sel) for p in pieces]
        for h in range(GLA_HEADS):
            rs = slice(h * GLA_DK, (h + 1) * GLA_DK)
            vs = slice(h * GLA_DV, (h + 1) * GLA_DV)
            s_new = a_col[rs] * s_ref[bi, rs, :] + k_col[rs] * v[bi:bi + 1, vs]
            so_ref[bi, rs, :] = s_new
            o = jnp.sum(q_col[rs] * s_new, axis=0, keepdims=True)
            ms = jnp.mean(o * o, axis=-1, keepdims=True)
            g = o * lax.rsqrt(ms + EPS) * gn_ref[...]
            o_ref[bi:bi + 1, vs] = g * rb[bi:bi + 1, vs]


def _gla_sample(gq, gk, la, gv, rbs, gn, state, *, layer):
    tb = SAMPLE_TB
    nb = DEC_BATCH // tb

    def row(w):
        return pl.BlockSpec((tb, w), lambda i: (i, 0))

    st_spec = pl.BlockSpec((tb, GK_W, GLA_DV), lambda i: (i, 0, 0))
    st_in_spec = pl.BlockSpec((None, tb, GK_W, GLA_DV), lambda i: (layer, i, 0, 0))
    return pl.pallas_call(
        _gla_sample_kernel,
        grid=(nb,),
        in_specs=[row(GK_W), row(GK_W), row(GK_W), row(GV_W), row(GV_W),
                  pl.BlockSpec((1, GLA_DV), lambda i: (0, 0)), st_in_spec],
        out_specs=[row(GV_W), st_spec],
        out_shape=[
            jax.ShapeDtypeStruct((DEC_BATCH, GV_W), F32),
            jax.ShapeDtypeStruct((DEC_BATCH, GK_W, GLA_DV), F32),
        ],
        compiler_params=_params(("parallel",)),
        name="gla_sample",
    )(gq, gk, la, gv, rbs, gn, state)


def _merge_kernel(x_ref, a_ref, g_ref, sga_ref, sgb_ref, wpa_ref, wpb_ref, wo_ref,
                  g1_ref, sh_ref, sc_ref, n2_ref, *rest):
    with_router = len(rest) == 8
    x1_ref, h2_ref = rest[3:5] if with_router else rest
    ya = _dot(a_ref[...].astype(BF16), wpa_ref[...])
    yb = _dot(g_ref[...].astype(BF16), wpb_ref[...])
    merged = sga_ref[...].astype(F32) * ya + sgb_ref[...].astype(F32) * yb
    mix = _dot(merged.astype(BF16), wo_ref[...])
    x1 = x_ref[...] + g1_ref[...] * mix
    x1_ref[...] = x1
    ms = jnp.mean(x1 * x1, axis=-1, keepdims=True)
    h = x1 * lax.rsqrt(ms + EPS) * n2_ref[...]
    h2 = h * (1.0 + sc_ref[...]) + sh_ref[...]
    h2_ref[...] = h2.astype(h2_ref.dtype)
    if with_router:
        rw_ref, rbias_ref, tri_ref = rest[0:3]
        route_ref, route_t_ref, cnt_ref = rest[5:8]
        logits = _dot(h2.astype(BF16), rw_ref[...]) + rbias_ref[...]
        packed, counts = _route_pack(logits, tri_ref[...])
        route_ref[...] = packed
        route_t_ref[...] = packed.T[0:route_t_ref.shape[0], :]
        cnt_ref[...] = jnp.broadcast_to(counts, cnt_ref.shape)


def _mod_spec(tm, n, mod):
    if mod.arr.ndim == 4:
        return pl.BlockSpec((None, None, tm, D_MODEL), lambda i, *_: (mod.layer, mod.idx, i, 0))
    tiles_per_seq = (n // mod.arr.shape[2]) // tm
    return pl.BlockSpec((None, None, None, 1, D_MODEL),
                        lambda i, *_: (mod.layer, mod.idx, i // tiles_per_seq, 0, 0))


def _merge(x, a, g, sga, sgb, wpa, wpb, wo, g1, sh2, sc2, n2, *, tm, router=None):
    n = x.shape[0]

    def row(w):
        return pl.BlockSpec((tm, w), lambda i: (i, 0))

    def const(shape):
        return pl.BlockSpec(shape, lambda i: (0,) * len(shape))

    in_specs = [row(D_MODEL), row(a.shape[1]), row(GV_W), row(D_MODEL), row(D_MODEL),
                const(wpa.shape), const(wpb.shape), const(wo.shape),
                _mod_spec(tm, n, g1), _mod_spec(tm, n, sh2), _mod_spec(tm, n, sc2), const((1, D_MODEL))]
    out_specs = [row(D_MODEL), row(D_MODEL)]
    out_shape = [jax.ShapeDtypeStruct((n, D_MODEL), F32), jax.ShapeDtypeStruct((n, D_MODEL), BF16)]
    args = [x, a, g, sga, sgb, wpa, wpb, wo, g1.arr, sh2.arr, sc2.arr, n2]
    if router is not None:
        tri = jnp.asarray(np.tril(np.ones((tm, tm), np.float32), -1), BF16)
        in_specs += [const(router[0].shape), const(router[1].shape), const((tm, tm))]
        out_specs += [row(ROUTER_PAD), pl.BlockSpec((8, tm), lambda i: (0, i)),
                      pl.BlockSpec((None, 8, ROUTER_PAD), lambda i: (i, 0, 0))]
        out_shape += [jax.ShapeDtypeStruct((n, ROUTER_PAD), F32), jax.ShapeDtypeStruct((8, n), F32),
                      jax.ShapeDtypeStruct((n // tm, 8, ROUTER_PAD), F32)]
        args += [router[0], router[1], tri]
    return pl.pallas_call(
        _merge_kernel,
        grid=(n // tm,),
        in_specs=in_specs,
        out_specs=out_specs,
        out_shape=out_shape,
        compiler_params=_params(("parallel",)),
        name="merge_out",
    )(*args)


def _swiglu(hb, wg_ref, wu_ref, wd_ref):
    act = (_silu(_dot(hb, wg_ref[...])) * _dot(hb, wu_ref[...])).astype(BF16)
    return _dot(act, wd_ref[...])


def _ffn_kernel(h_ref, x_ref, g2_ref, wg_ref, wu_ref, wd_ref, o_ref, acc_ref):
    f = pl.program_id(1)
    y = _swiglu(h_ref[...], wg_ref, wu_ref, wd_ref)

    @pl.when(f == 0)
    def _():
        acc_ref[...] = y

    @pl.when(f > 0)
    def _():
        acc_ref[...] += y

    @pl.when(f == pl.num_programs(1) - 1)
    def _():
        o_ref[...] = x_ref[...] + g2_ref[...] * acc_ref[...]


def _ffn_cast_kernel(h_ref, x_ref, g2_ref, wg_ref, wu_ref, wd_ref, o_ref, wg16_ref, wu16_ref, wd16_ref, acc_ref):
    wg16_ref[...] = wg_ref[...].astype(BF16)
    wu16_ref[...] = wu_ref[...].astype(BF16)
    wd16_ref[...] = wd_ref[...].astype(BF16)
    _ffn_kernel(h_ref, x_ref, g2_ref, wg16_ref, wu16_ref, wd16_ref, o_ref, acc_ref)


def _ffn(h2, x1, g2, wg, wu, wd, *, tm, tf, emit_bf16=False):
    n = h2.shape[0]
    assert not emit_bf16 or n == tm
    mod_spec = _mod_spec(tm, n, g2)
    up_spec = pl.BlockSpec((D_MODEL, tf), lambda i, f: (0, f))
    down_spec = pl.BlockSpec((tf, D_MODEL), lambda i, f: (f, 0))
    out_specs = [pl.BlockSpec((tm, D_MODEL), lambda i, f: (i, 0))]
    out_shape = [jax.ShapeDtypeStruct((n, D_MODEL), F32)]
    if emit_bf16:
        out_specs += [up_spec, up_spec, down_spec]
        out_shape += [jax.ShapeDtypeStruct(w.shape, BF16) for w in (wg, wu, wd)]
    out = pl.pallas_call(
        _ffn_cast_kernel if emit_bf16 else _ffn_kernel,
        grid=(n // tm, D_FF // tf),
        in_specs=[
            pl.BlockSpec((tm, D_MODEL), lambda i, f: (i, 0)),
            pl.BlockSpec((tm, D_MODEL), lambda i, f: (i, 0)),
            mod_spec, up_spec, up_spec, down_spec,
        ],
        out_specs=out_specs,
        out_shape=out_shape,
        scratch_shapes=[pltpu.VMEM((tm, D_MODEL), F32)],
        compiler_params=_params(("parallel", "arbitrary")),
        name="ffn_dense",
    )(h2, x1, g2.arr, wg, wu, wd)
    return out if emit_bf16 else out[0]


def _moe_kernel(h_ref, x_ref, g2_ref, rw_ref, rbias_ref, wg_ref, wu_ref, wd_ref,
                o_ref, wg16_ref, wu16_ref, wd16_ref, acc_ref, gate_ref):
    e = pl.program_id(0)
    f = pl.program_id(1)
    hb = h_ref[...]
    tm = hb.shape[0]
    lane = lax.broadcasted_iota(jnp.int32, (tm, ROUTER_PAD), 1).astype(F32)

    @pl.when((e == 0) & (f == 0))
    def _():
        logits = _dot(hb, rw_ref[...]) + rbias_ref[...]
        i1, i2, p1, p2 = _top2(logits, lane)
        gate_ref[...] = jnp.where(lane == i1, p1, 0.0) + jnp.where(lane == i2, p2, 0.0)
        acc_ref[...] = jnp.zeros_like(acc_ref)

    wg = wg_ref[...].astype(BF16)
    wu = wu_ref[...].astype(BF16)
    wd = wd_ref[...].astype(BF16)
    wg16_ref[...] = wg
    wu16_ref[...] = wu
    wd16_ref[...] = wd
    ge = jnp.sum(jnp.where(lane == e.astype(F32), gate_ref[...], 0.0), axis=-1, keepdims=True)
    act = (_silu(_dot(hb, wg)) * _dot(hb, wu)).astype(BF16)
    acc_ref[...] += ge * _dot(act, wd)

    @pl.when((e == pl.num_programs(0) - 1) & (f == pl.num_programs(1) - 1))
    def _():
        o_ref[...] = x_ref[...] + g2_ref[...] * acc_ref[...]


def _moe(h2, x1, g2, rw, rbias, wg, wu, wd):
    n = h2.shape[0]
    fe = D_FF_EXPERT
    tf = fe

    def const(shape):
        return pl.BlockSpec(shape, lambda e, f: (0,) * len(shape))

    up_spec = pl.BlockSpec((None, D_MODEL, tf), lambda e, f: (e, 0, f))
    down_spec = pl.BlockSpec((None, tf, D_MODEL), lambda e, f: (e, f, 0))
    mod_spec = pl.BlockSpec((None, None, n, D_MODEL), lambda e, f: (g2.layer, g2.idx, 0, 0))
    return pl.pallas_call(
        _moe_kernel,
        grid=(N_EXPERTS, fe // tf),
        in_specs=[const((n, D_MODEL)), const((n, D_MODEL)), mod_spec,
                  const((D_MODEL, ROUTER_PAD)), const((1, ROUTER_PAD)),
                  up_spec, up_spec, down_spec],
        out_specs=[const((n, D_MODEL)), up_spec, up_spec, down_spec],
        out_shape=[
            jax.ShapeDtypeStruct((n, D_MODEL), F32),
            jax.ShapeDtypeStruct(wg.shape, BF16),
            jax.ShapeDtypeStruct(wu.shape, BF16),
            jax.ShapeDtypeStruct(wd.shape, BF16),
        ],
        scratch_shapes=[pltpu.VMEM((n, D_MODEL), F32), pltpu.VMEM((n, ROUTER_PAD), F32)],
        compiler_params=_params(("arbitrary", "arbitrary")),
        name="moe",
    )(h2, x1, g2.arr, rw, rbias, wg, wu, wd)


def _top2(logits, lane):
    lg = jnp.where(lane < N_EXPERTS, logits, -jnp.inf)
    m1 = jnp.max(lg, axis=-1, keepdims=True)
    i1 = jnp.min(jnp.where(lg == m1, lane, float(ROUTER_PAD)), axis=-1, keepdims=True)
    lg2 = jnp.where(lane == i1, -jnp.inf, lg)
    m2 = jnp.max(lg2, axis=-1, keepdims=True)
    i2 = jnp.min(jnp.where(lg2 == m2, lane, float(ROUTER_PAD)), axis=-1, keepdims=True)
    e2 = jnp.exp(m2 - m1)
    p1 = 1.0 / (1.0 + e2)
    return i1, i2, p1, e2 * p1


def _route_pack(logits, tri):
    lane = lax.broadcasted_iota(jnp.int32, logits.shape, 1).astype(F32)
    i1, i2, p1, p2 = _top2(logits, lane)
    oh1 = jnp.where(lane == i1, 1.0, 0.0)
    oh2 = jnp.where(lane == i2, 1.0, 0.0)
    cnt1 = jnp.sum(oh1, axis=0, keepdims=True)
    cnt2 = jnp.sum(oh2, axis=0, keepdims=True)
    rank1 = jnp.sum(_dot(tri, oh1.astype(BF16)) * oh1, axis=-1, keepdims=True)
    rank2 = jnp.sum((_dot(tri, oh2.astype(BF16)) + cnt1) * oh2, axis=-1, keepdims=True)
    packed = jnp.zeros_like(logits)
    for k, val in enumerate((p1, p2, i1, i2, rank1, rank2)):
        packed = jnp.where(lane == float(k), val, packed)
    return packed, cnt1 + cnt2


def _route_tables(route_t, tile_counts, tm, n_tiles):
    n = route_t.shape[1]
    tile_cnt = tile_counts[:, 0, :N_EXPERTS].astype(jnp.int32)
    cnt = jnp.sum(tile_cnt, axis=0)
    gsz = ((cnt + tm - 1) // tm) * tm
    gend = jnp.cumsum(gsz)
    seg_start = (gend - gsz)[None, :] + jnp.cumsum(tile_cnt, axis=0) - tile_cnt
    seg_of_token = jnp.repeat(seg_start.T, n // tile_cnt.shape[0], axis=1)
    experts = jnp.arange(N_EXPERTS, dtype=jnp.int32)[:, None]
    pos = []
    for k in range(2):
        e_k = route_t[2 + k].astype(jnp.int32)
        rank_k = route_t[4 + k].astype(jnp.int32)
        pos.append(jnp.sum(jnp.where(e_k[None, :] == experts, seg_of_token, 0), axis=0) + rank_k)
    pos = jnp.concatenate(pos)
    tile_start = jnp.arange(n_tiles, dtype=jnp.int32) * tm
    tile_expert = jnp.sum((tile_start[:, None] >= gend[None, :]).astype(jnp.int32), axis=1)
    tile_expert = jnp.minimum(tile_expert, N_EXPERTS - 1)
    live = (gend[-1] // tm).reshape(1)
    return pos, tile_expert, live, gend - gsz + cnt, gend


def _row_copy_wait(src_hbm, dst, sem, rows):
    pltpu.make_async_copy(src_hbm.at[pl.ds(0, rows)], dst, sem).wait()


def _moe_dispatch_kernel(pos_ref, pad_lo_ref, pad_hi_ref, live_ref, h_ref, xs_hbm, rows, zbuf, rsem, zsem):
    i = pl.program_id(0)
    steps = pl.num_programs(0)
    tt = h_ref.shape[0]
    n = tt * steps
    tm = zbuf.shape[0]
    slot = i % 2

    def wait_rows(sl):
        for _ in range(2):
            pltpu.make_async_copy(rows.at[sl], xs_hbm.at[pl.ds(0, tt)], rsem.at[sl]).wait()

    @pl.when(i >= 2)
    def _():
        wait_rows(slot)

    rows[slot] = h_ref[...].astype(F32)
    base = i * tt
    for r in range(tt):
        row = rows.at[slot, pl.ds(r, 1)]
        pltpu.make_async_copy(row, xs_hbm.at[pl.ds(pos_ref[base + r], 1)], rsem.at[slot]).start(priority=0)
        pltpu.make_async_copy(row, xs_hbm.at[pl.ds(pos_ref[n + base + r], 1)], rsem.at[slot]).start(priority=1)

    @pl.when(i == steps - 1)
    def _():
        if steps > 1:
            wait_rows(1 - slot)
        wait_rows(slot)
        zbuf[...] = jnp.zeros(zbuf.shape, F32)

        def fill_row(p):
            return pltpu.make_async_copy(zbuf.at[pl.ds(0, 1)], xs_hbm.at[pl.ds(p, 1)], zsem)

        def fill_tile(t):
            return pltpu.make_async_copy(zbuf, xs_hbm.at[pl.ds(pl.multiple_of(t * tm, tm), tm)], zsem)

        for e in range(N_EXPERTS):
            lo, hi = pad_lo_ref[e], pad_hi_ref[e]
            lax.fori_loop(lo, hi, lambda p, c: (fill_row(p).start(), c)[1], 0)
            lax.fori_loop(lo, hi, lambda p, c: (fill_row(p).wait(), c)[1], 0)
        lo, hi = live_ref[0], xs_hbm.shape[0] // tm
        lax.fori_loop(lo, hi, lambda t, c: (fill_tile(t).start(), c)[1], 0)
        lax.fori_loop(lo, hi, lambda t, c: (fill_tile(t).wait(), c)[1], 0)


def _moe_dispatch(h2, pos, pad_lo, pad_hi, live, *, tt, tm, n_tiles):
    n = h2.shape[0]
    grid_spec = pltpu.PrefetchScalarGridSpec(
        num_scalar_prefetch=4,
        grid=(n // tt,),
        in_specs=[pl.BlockSpec((tt, D_MODEL), lambda i, *_: (i, 0))],
        out_specs=pl.BlockSpec(memory_space=pl.ANY),
        scratch_shapes=[pltpu.VMEM((2, tt, D_MODEL), F32), pltpu.VMEM((tm, D_MODEL), F32),
                        pltpu.SemaphoreType.DMA((2,)), pltpu.SemaphoreType.DMA(())],
    )
    return pl.pallas_call(
        _moe_dispatch_kernel,
        grid_spec=grid_spec,
        out_shape=jax.ShapeDtypeStruct((n_tiles * tm, D_MODEL), F32),
        compiler_params=_params(("arbitrary",)),
        name="moe_dispatch",
    )(pos, pad_lo, pad_hi, live, h2)


def _moe_expert_kernel(te_ref, live_ref, x_ref, wg_ref, wu_ref, wd_ref, y_ref):
    del te_ref
    is_live = pl.program_id(0) < live_ref[0]

    @pl.when(is_live)
    def _():
        y_ref[...] = _swiglu(x_ref[...].astype(BF16), wg_ref, wu_ref, wd_ref)

    @pl.when(jnp.logical_not(is_live))
    def _():
        y_ref[...] = jnp.zeros(y_ref.shape, F32)


def _moe_experts(xs, tile_expert, live, wg, wu, wd, *, tm, n_tiles):
    fe = D_FF_EXPERT

    def in_tile(t, te, live):
        return (jnp.minimum(t, live[0] - 1), 0)

    def out_tile(t, te, live):
        return (t, 0)

    def expert(t, te, live):
        return (te[t], 0, 0)

    grid_spec = pltpu.PrefetchScalarGridSpec(
        num_scalar_prefetch=2,
        grid=(n_tiles,),
        in_specs=[
            pl.BlockSpec((tm, D_MODEL), in_tile),
            pl.BlockSpec((None, D_MODEL, fe), expert),
            pl.BlockSpec((None, D_MODEL, fe), expert),
            pl.BlockSpec((None, fe, D_MODEL), expert),
        ],
        out_specs=pl.BlockSpec((tm, D_MODEL), out_tile),
    )
    return pl.pallas_call(
        _moe_expert_kernel,
        grid_spec=grid_spec,
        out_shape=jax.ShapeDtypeStruct(xs.shape, F32),
        compiler_params=_params(("arbitrary",)),
        name="moe_experts",
    )(tile_expert, live, xs, wg, wu, wd)


def _moe_combine_kernel(pos_ref, x_ref, g2_ref, r_ref, ys_hbm, o_ref, buf, sem):
    i = pl.program_id(0)
    nt = pl.num_programs(0)
    tt = x_ref.shape[0]
    n = nt * tt

    def fetch(tile, sl):
        base = tile * tt
        for r in range(tt):
            for s in range(2):
                row = pos_ref[s * n + base + r]
                pltpu.make_async_copy(ys_hbm.at[pl.ds(row, 1)], buf.at[sl, pl.ds(s * tt + r, 1)],
                                      sem.at[sl]).start(priority=s)

    @pl.when(i == 0)
    def _():
        fetch(0, 0)

    @pl.when(i + 1 < nt)
    def _():
        fetch(i + 1, (i + 1) % 2)

    sl = i % 2
    _row_copy_wait(ys_hbm, buf.at[sl], sem.at[sl], 2 * tt)
    r = r_ref[...]
    f = r[:, 0:1] * buf[sl, pl.ds(0, tt), :] + r[:, 1:2] * buf[sl, pl.ds(tt, tt), :]
    o_ref[...] = x_ref[...] + g2_ref[...] * f


def _moe_combine(x1, g2, route, pos, ys, *, tt):
    n = x1.shape[0]
    grid_spec = pltpu.PrefetchScalarGridSpec(
        num_scalar_prefetch=1,
        grid=(n // tt,),
        in_specs=[
            pl.BlockSpec((tt, D_MODEL), lambda i, pos: (i, 0)),
            _mod_spec(tt, n, g2),
            pl.BlockSpec((tt, ROUTER_PAD), lambda i, pos: (i, 0)),
            pl.BlockSpec(memory_space=pl.ANY),
        ],
        out_specs=pl.BlockSpec((tt, D_MODEL), lambda i, pos: (i, 0)),
        scratch_shapes=[pltpu.VMEM((2, 2 * tt, D_MODEL), F32), pltpu.SemaphoreType.DMA((2,))],
    )
    return pl.pallas_call(
        _moe_combine_kernel,
        grid_spec=grid_spec,
        out_shape=jax.ShapeDtypeStruct((n, D_MODEL), F32),
        compiler_params=_params(("arbitrary",)),
        name="moe_combine",
    )(pos, x1, g2.arr, route, ys)


def _moe_routed(h2, x1, g2, route, route_t, tile_counts, wg, wu, wd, *, tm, tt):
    n = h2.shape[0]
    n_tiles = (2 * n) // tm + N_EXPERTS
    pos, tile_expert, live, pad_lo, pad_hi = _route_tables(route_t, tile_counts, tm, n_tiles)
    xs = _moe_dispatch(h2, pos, pad_lo, pad_hi, live, tt=tt, tm=tm, n_tiles=n_tiles)
    ys = _moe_experts(xs, tile_expert, live, wg, wu, wd, tm=tm, n_tiles=n_tiles)
    return _moe_combine(x1, g2, route, pos, ys, tt=tt)


def _head_perm():
    idx = []
    for g in range(GROUP):
        for kv in range(N_KV):
            h = kv * GROUP + g
            idx.extend(range(h * HEAD_DIM, (h + 1) * HEAD_DIM))
    return np.asarray(idx, np.int32)


def kernel(x_prompt, x_sample, cache_k, cache_v, state_gla, c_prompt, c_sample, ada_w, ada_b, norm1_g, norm2_g, w_in, q_norm_g, k_norm_g, attn_sinks, gla_wa2, gla_ba, gla_norm_g, w_branch_a, w_branch_b, w_out, ffn_w_gate, ffn_w_up, ffn_w_down, router_w, router_b, moe_w_gate, moe_w_up, moe_w_down):
    n_p = BATCH * SEQ
    xp = x_prompt.reshape(n_p, D_MODEL)
    xs = x_sample.reshape(DEC_BATCH, D_MODEL)

    c_pad = -(BATCH + DEC_BATCH) % 8
    c_all = jnp.pad(jnp.concatenate([c_sample, c_prompt], axis=0), ((0, c_pad), (0, 0)))
    mod_sample, mod_prompt = _ada(c_all, ada_w, ada_b, DEC_BATCH, BATCH)
    state_rows = state_gla.reshape(DEPTH, DEC_BATCH, GK_W, GLA_DV)

    bd = jnp.asarray(np.kron(np.eye(N_HEADS), np.ones((HEAD_DIM, HEAD_DIM))), BF16)
    ut = jnp.asarray(_gla_cum_matrix(), BF16)
    gla_upper, gla_pairs = (jnp.asarray(m) for m in _gla_level_masks())
    perm = _head_perm()
    swa_bias = jnp.asarray(_swa_sample_bias())
    swa_prompt_bias = jnp.asarray(_swa_prompt_bias())

    kp_l, vp_l, sp_l, ks_l, vs_l, ss_l = [], [], [], [], [], []
    for l in range(DEPTH):
        mod_p = [_Mod(mod_prompt, l, i) for i in range(N_MOD)]
        mod_s = [_Mod(mod_sample, l, i) for i in range(N_MOD)]

        qg =(jnp.tile(q_norm_g[l], N_HEADS) * (HEAD_DIM ** -0.5)).reshape(1, Q_W)
        kg = jnp.tile(k_norm_g[l], N_KV).reshape(1, KV_W)
        wa2 = jnp.pad(gla_wa2[l], ((0, GATE_PAD - GATE_RANK), (0, 0))).astype(BF16)
        ba = gla_ba[l].reshape(1, GK_W)
        n1 = norm1_g[l].reshape(1, D_MODEL)
        n2 = norm2_g[l].reshape(1, D_MODEL)
        gn = gla_norm_g[l].reshape(1, GLA_DV)
        wpa = w_branch_a[l][perm].astype(BF16)
        wpb = w_branch_b[l].astype(BF16)
        wo = w_out[l].astype(BF16)
        sink_rows = jnp.broadcast_to(attn_sinks[l][_sample_head_of_row()][:, None], (N_HEADS, LANES))

        qs, ksn, vsn, gqs, gks, gvs, las, rbss, sgas, sgbs, w = _mixin(
            xs, mod_s[0], mod_s[1], n1, w_in[l], bd, qg, kg, wa2, ba, tm=DEC_BATCH, emit_w=True)

        q, k, v, gq, gk, gv, la, rbs, sga, sgb = _mixin(
            xp, mod_p[0], mod_p[1], n1, w, bd, qg, kg, wa2, ba, tm=ROW_TILE_RESIDENT)
        a_out = _swa_prompt(attn_sinks[l], q, k, v, swa_prompt_bias)
        g_out, s_fin = _gla_prompt(gq, gk, la, gv, rbs, gn, ut, gla_upper, gla_pairs)
        is_moe = l % 2 == 1
        if is_moe:
            rw = jnp.pad(router_w[l // 2], ((0, 0), (0, ROUTER_PAD - N_EXPERTS))).astype(BF16)
            rbias = jnp.pad(router_b[l // 2], (0, ROUTER_PAD - N_EXPERTS)).reshape(1, ROUTER_PAD)
        merged = _merge(xp, a_out, g_out, sga, sgb, wpa, wpb, wo, mod_p[2], mod_p[3], mod_p[4], n2,
                        tm=ROW_TILE_RESIDENT, router=(rw, rbias) if is_moe else None)
        x1, h2 = merged[0], merged[1]
        for cache, full in ((kp_l, k), (vp_l, v)):
            tail = full.reshape(BATCH, SEQ, KV_W)[:, SEQ - WINDOW:]
            cache.append(tail.reshape(BATCH, WINDOW, N_KV, HEAD_DIM))
        sp_l.append(s_fin.reshape(BATCH, GLA_HEADS, GLA_DK, GLA_DV))

        a_s, nk, nv = _swa_sample(
            qs.astype(F32).reshape(DEC_BATCH, N_HEADS, HEAD_DIM),
            ksn.reshape(DEC_BATCH, N_KV, HEAD_DIM), vsn.reshape(DEC_BATCH, N_KV, HEAD_DIM),
            cache_k[l].reshape(1, DEC_BATCH, CACHE_ROWS, HEAD_DIM),
            cache_v[l].reshape(1, DEC_BATCH, CACHE_ROWS, HEAD_DIM), swa_bias, sink_rows, layer=0)
        g_s, s_new = _gla_sample(gqs, gks, las, gvs, rbss, gn, state_rows, layer=l)
        x1s, h2s = _merge(xs, a_s.reshape(DEC_BATCH, Q_W), g_s, sgas, sgbs, wpa, wpb, wo,
                          mod_s[2], mod_s[3], mod_s[4], n2, tm=DEC_BATCH)
        ks_l.append(nk.reshape(DEC_BATCH, WINDOW, N_KV, HEAD_DIM))
        vs_l.append(nv.reshape(DEC_BATCH, WINDOW, N_KV, HEAD_DIM))
        ss_l.append(s_new.reshape(DEC_BATCH, GLA_HEADS, GLA_DK, GLA_DV))

        i = l // 2
        if not is_moe:
            xs, wg, wu, wd = _ffn(h2s, x1s, mod_s[5], ffn_w_gate[i], ffn_w_up[i], ffn_w_down[i],
                                  tm=DEC_BATCH, tf=D_FF // 2, emit_bf16=True)
            xp = _ffn(h2, x1, mod_p[5], wg, wu, wd, tm=ROW_TILE_SWIGLU, tf=D_FF // 2)
        else:
            xs, wg, wu, wd = _moe(h2s, x1s, mod_s[5], rw, rbias, moe_w_gate[i], moe_w_up[i], moe_w_down[i])
            xp = _moe_routed(h2, x1, mod_p[5], *merged[2:5], wg, wu, wd,
                             tm=ROW_TILE_SWIGLU, tt=MOE_TOKEN_TILE)

    return (xp.reshape(BATCH, SEQ, D_MODEL), xs.reshape(DEC_BATCH, 1, D_MODEL),
            jnp.stack(kp_l), jnp.stack(vp_l), jnp.stack(sp_l),
            jnp.stack(ks_l), jnp.stack(vs_l), jnp.stack(ss_l))
```
